```python
import math
import jax, jax.numpy as jnp
from jax import lax
import numpy as np

D_MODEL = 1024
BATCH = 8
SEQ = 4096
DEPTH = 4

N_POOL_LAYERS = DEPTH // 2
N_ATTN_LAYERS = DEPTH - N_POOL_LAYERS
POOL_WINDOWS = (2, 4, 8, 16)
N_POOL_GROUPS = len(POOL_WINDOWS)
POOL_GROUP_DIM = D_MODEL // N_POOL_GROUPS
BRANCHES = ((128, 1), (512, 4), (2048, 16))
N_BRANCHES = len(BRANCHES)
HEAD_DIM = 64
N_HEADS = D_MODEL // HEAD_DIM
D_ATTN = N_HEADS * HEAD_DIM
ATTN_BLOCK = 128
D_FF = 2816
CONV_WIDTH = 3
EPS = 1e-6
ADA_SCALE = 0.5

kernel_name = "yoco_pool_dilated_alibi_hybrid"


def _rmsnorm(x, g):
    x32 = x.astype(jnp.float32)
    y = x32 * lax.rsqrt(jnp.mean(x32 * x32, axis=-1, keepdims=True) + EPS)
    return (y * g.astype(jnp.float32)).astype(x.dtype)


def _modulate(h, shift, scale):
    return h * (1 + scale[:, None, :]) + shift[:, None, :]


def _alibi_slopes(n):
    def pow2(m):
        start = 2.0 ** (-(2.0 ** -(math.log2(m) - 3)))
        return [start ** (i + 1) for i in range(m)]
    if math.log2(n).is_integer():
        s = pow2(n)
    else:
        c = 2 ** math.floor(math.log2(n))
        s = pow2(c) + pow2(2 * c)[0::2][: n - c]
    s = np.asarray(s, dtype=np.float32)
    return -np.sort(-s)


def _pool_mixer(h, w_in, w_grp, scale, w_out):
    b, s, _ = h.shape
    u = (h @ w_in).reshape(b, s, N_POOL_GROUPS, POOL_GROUP_DIM)
    u32 = u.astype(jnp.float32)
    csum = jnp.cumsum(u32, axis=1)
    t = jnp.arange(s)
    outs = []
    for g, w in enumerate(POOL_WINDOWS):
        cs = csum[:, :, g]
        lag = jnp.pad(cs, ((0, 0), (w, 0), (0, 0)))[:, :s]
        count = jnp.minimum(t + 1, w).astype(jnp.float32)[None, :, None]
        pooled = (cs - lag) / count - u32[:, :, g]
        outs.append(jnp.einsum('bsc,cd->bsd', pooled.astype(h.dtype), w_grp[g]))
    y = jnp.concatenate(outs, axis=-1) * scale
    return y @ w_out


def _causal_dwconv(a, w, bias):
    s = a.shape[1]
    ap = jnp.pad(a, ((0, 0), (CONV_WIDTH - 1, 0), (0, 0)))
    y = bias
    for k in range(CONV_WIDTH):
        y = y + ap[:, k:k + s] * w[k]
    return y


def _conv_ffn(h, w_up, conv_w, conv_b, w_down):
    a, v = jnp.split(h @ w_up, 2, axis=-1)
    a = _causal_dwconv(a, conv_w, conv_b)
    return (jax.nn.silu(a) * v) @ w_down


def _dilated_branch(q, k, v, window, dilation, slopes):
    b, s, h, dh = q.shape
    n_steps = window // dilation
    blk = max(ATTN_BLOCK, n_steps)
    sub_len = s // dilation
    nb = -(-sub_len // blk)
    sub_pad = nb * blk

    def to_sub(t):
        t = t.reshape(b, sub_len, dilation, h, dh).transpose(0, 2, 1, 3, 4).reshape(b * dilation, sub_len, h, dh)
        return jnp.pad(t, ((0, 0), (0, sub_pad - sub_len), (0, 0), (0, 0)))

    def with_prev(t):
        tb = t.reshape(-1, nb, blk, h, dh)
        prev = jnp.pad(tb, ((0, 0), (1, 0), (0, 0), (0, 0), (0, 0)))[:, :nb]
        return jnp.concatenate([prev, tb], axis=2)

    qb = to_sub(q).reshape(-1, nb, blk, h, dh)
    kb = with_prev(to_sub(k))
    vb = with_prev(to_sub(v))

    scores = jnp.einsum('bnqhd,bnkhd->bnhqk', qb, kb).astype(jnp.float32) * (dh ** -0.5)
    qi = jnp.arange(blk)[:, None] + blk
    ki = jnp.arange(2 * blk)[None, :]
    delta = qi - ki
    key_idx = jnp.arange(nb)[:, None] * blk + jnp.arange(2 * blk)[None, :] - blk
    valid = ((delta >= 0) & (delta <= n_steps))[None] & (key_idx >= 0)[:, None, :]
    bias = -slopes[:, None, None] * (delta * dilation).astype(jnp.float32)[None]
    scores = jnp.where(valid[None, :, None], scores + bias[None, None], -jnp.inf)
    lse = jax.nn.logsumexp(scores, axis=-1)
    p = jnp.exp(scores - lse[..., None])
    out = jnp.einsum('bnhqk,bnkhd->bnqhd', p.astype(v.dtype), vb)

    out = out.reshape(b, dilation, sub_pad, h, dh)[:, :, :sub_len]
    out = out.transpose(0, 2, 1, 3, 4).reshape(b, s, h, dh)
    lse = lse.transpose(0, 1, 3, 2).reshape(b, dilation, sub_pad, h)[:, :, :sub_len]
    lse = lse.transpose(0, 2, 1, 3).reshape(b, s, h)
    return out, lse


def _dilated_attention(h, kv, w_q, w_o, slopes):
    b, s, _ = h.shape
    q = (h @ w_q).reshape(b, s, N_BRANCHES, N_HEADS, HEAD_DIM)
    outs, lses = [], []
    for g, (window, dil) in enumerate(BRANCHES):
        o, l = _dilated_branch(q[:, :, g], kv[:, :, 0, g], kv[:, :, 1, g], window, dil, slopes[g])
        outs.append(o)
        lses.append(l)
    wts = jax.nn.softmax(jnp.stack(lses, axis=0), axis=0)
    o = jnp.sum(wts[..., None] * jnp.stack(outs, axis=0).astype(jnp.float32), axis=0)
    return o.reshape(b, s, D_ATTN).astype(h.dtype) @ w_o


def _fwd_setup_inputs(seed: int = 0) -> dict:
    key = jax.random.key(seed)
    ks = jax.random.split(key, 24)

    def nrm(k, shape, scale):
        return jax.random.normal(k, shape, jnp.float32) * scale

    D, F, G = D_MODEL, D_FF, N_BRANCHES
    return {
        "x": nrm(ks[0], (BATCH, SEQ, D), 1.0),
        "c": nrm(ks[1], (BATCH, D), 1.0),
        "ada_w": nrm(ks[2], (DEPTH, D, 6 * D), ADA_SCALE * D ** -0.5),
        "ada_b": nrm(ks[3], (DEPTH, 6 * D), 0.02),
        "norm1_g": 1.0 + nrm(ks[4], (DEPTH, D), 0.05),
        "norm2_g": 1.0 + nrm(ks[5], (DEPTH, D), 0.05),
        "pool_w_in": nrm(ks[6], (N_POOL_LAYERS, D, D), D ** -0.5),
        "pool_w_grp": nrm(ks[7], (N_POOL_LAYERS, N_POOL_GROUPS, POOL_GROUP_DIM, POOL_GROUP_DIM), POOL_GROUP_DIM ** -0.5),
        "pool_scale": 1.0 + nrm(ks[8], (N_POOL_LAYERS, D), 0.1),
        "pool_w_out": nrm(ks[9], (N_POOL_LAYERS, D, D), D ** -0.5),
        "kv_norm_g": 1.0 + nrm(ks[10], (D,), 0.05),
        "kv_ada_w": nrm(ks[11], (D, 2 * D), ADA_SCALE * D ** -0.5),
        "kv_ada_b": nrm(ks[12], (2 * D,), 0.02),
        "w_kv": nrm(ks[13], (D, 2 * G * D_ATTN), D ** -0.5),
        "attn_w_q": nrm(ks[14], (N_ATTN_LAYERS, D, G * D_ATTN), D ** -0.5),
        "attn_w_o": nrm(ks[15], (N_ATTN_LAYERS, D_ATTN, D), D_ATTN ** -0.5),
        "ffn_w_up": nrm(ks[16], (DEPTH, D, 2 * F), D ** -0.5),
        "ffn_conv_w": nrm(ks[17], (DEPTH, CONV_WIDTH, F), CONV_WIDTH ** -0.5),
        "ffn_conv_b": nrm(ks[18], (DEPTH, F), 0.02),
        "ffn_w_down": nrm(ks[19], (DEPTH, F, D), F ** -0.5),
        "final_g": 1.0 + nrm(ks[20], (D,), 0.05),
    }


def _fwd_reference(x, c, ada_w, ada_b, norm1_g, norm2_g, pool_w_in, pool_w_grp, pool_scale, pool_w_out,
              kv_norm_g, kv_ada_w, kv_ada_b, w_kv, attn_w_q, attn_w_o,
              ffn_w_up, ffn_conv_w, ffn_conv_b, ffn_w_down, final_g):
    b, s, _ = x.shape
    cond = jax.nn.silu(c)
    slopes = jnp.asarray(_alibi_slopes(N_BRANCHES * N_HEADS)).reshape(N_BRANCHES, N_HEADS)
    kv = None
    for layer in range(DEPTH):
        mod = cond @ ada_w[layer] + ada_b[layer]
        sh1, sc1, g1, sh2, sc2, g2 = jnp.split(mod, 6, axis=-1)
        h = _modulate(_rmsnorm(x, norm1_g[layer]), sh1, sc1)
        if layer < N_POOL_LAYERS:
            y = _pool_mixer(h, pool_w_in[layer], pool_w_grp[layer], pool_scale[layer], pool_w_out[layer])
        else:
            if layer == N_POOL_LAYERS:
                kv_shift, kv_scale = jnp.split(cond @ kv_ada_w + kv_ada_b, 2, axis=-1)
                hkv = _modulate(_rmsnorm(x, kv_norm_g), kv_shift, kv_scale)
                kv = (hkv @ w_kv).reshape(b, s, 2, N_BRANCHES, N_HEADS, HEAD_DIM)
            j = layer - N_POOL_LAYERS
            y = _dilated_attention(h, kv, attn_w_q[j], attn_w_o[j], slopes)
        x = x + g1[:, None, :] * y
        h = _modulate(_rmsnorm(x, norm2_g[layer]), sh2, sc2)
        x = x + g2[:, None, :] * _conv_ffn(h, ffn_w_up[layer], ffn_conv_w[layer], ffn_conv_b[layer], ffn_w_down[layer])
    return _rmsnorm(x, final_g)


import jax as _jax
import jax.numpy as _jnp

TWIN_FORMAT = 'train_step'
FWD_PARAMS = ['x', 'c', 'ada_w', 'ada_b', 'norm1_g', 'norm2_g', 'pool_w_in', 'pool_w_grp', 'pool_scale', 'pool_w_out', 'kv_norm_g', 'kv_ada_w', 'kv_ada_b', 'w_kv', 'attn_w_q', 'attn_w_o', 'ffn_w_up', 'ffn_conv_w', 'ffn_conv_b', 'ffn_w_down', 'final_g']
TWIN_WEIGHTS = ['ada_w', 'ada_b', 'norm1_g', 'norm2_g', 'pool_w_in', 'pool_w_grp', 'pool_scale', 'pool_w_out', 'kv_norm_g', 'kv_ada_w', 'kv_ada_b', 'w_kv', 'attn_w_q', 'attn_w_o', 'ffn_w_up', 'ffn_conv_w', 'ffn_conv_b', 'ffn_w_down', 'final_g']
TWIN_DIFF_INPUT = 'x'
TWIN_INPUTS = ['x', 'c', 'ada_w', 'ada_b', 'norm1_g', 'norm2_g', 'pool_w_in', 'pool_w_grp', 'pool_scale', 'pool_w_out', 'kv_norm_g', 'kv_ada_w', 'kv_ada_b', 'w_kv', 'attn_w_q', 'attn_w_o', 'ffn_w_up', 'ffn_conv_w', 'ffn_conv_b', 'ffn_w_down', 'final_g', 'loss_target', 'm_ada_w', 'm_ada_b', 'm_norm1_g', 'm_norm2_g', 'm_pool_w_in', 'm_pool_w_grp', 'm_pool_scale', 'm_pool_w_out', 'm_kv_norm_g', 'm_kv_ada_w', 'm_kv_ada_b', 'm_w_kv', 'm_attn_w_q', 'm_attn_w_o', 'm_ffn_w_up', 'm_ffn_conv_w', 'm_ffn_conv_b', 'm_ffn_w_down', 'm_final_g', 'v_ada_w', 'v_ada_b', 'v_norm1_g', 'v_norm2_g', 'v_pool_w_in', 'v_pool_w_grp', 'v_pool_scale', 'v_pool_w_out', 'v_kv_norm_g', 'v_kv_ada_w', 'v_kv_ada_b', 'v_w_kv', 'v_attn_w_q', 'v_attn_w_o', 'v_ffn_w_up', 'v_ffn_conv_w', 'v_ffn_conv_b', 'v_ffn_w_down', 'v_final_g']
TWIN_OUTPUTS = ['loss', 'grad_x', 'grad_ada_w', 'grad_ada_b', 'grad_norm1_g', 'grad_norm2_g', 'grad_pool_w_in', 'grad_pool_w_grp', 'grad_pool_scale', 'grad_pool_w_out', 'grad_kv_norm_g', 'grad_kv_ada_w', 'grad_kv_ada_b', 'grad_w_kv', 'grad_attn_w_q', 'grad_attn_w_o', 'grad_ffn_w_up', 'grad_ffn_conv_w', 'grad_ffn_conv_b', 'grad_ffn_w_down', 'grad_final_g', 'delta_ada_w', 'delta_ada_b', 'delta_norm1_g', 'delta_norm2_g', 'delta_pool_w_in', 'delta_pool_w_grp', 'delta_pool_scale', 'delta_pool_w_out', 'delta_kv_norm_g', 'delta_kv_ada_w', 'delta_kv_ada_b', 'delta_w_kv', 'delta_attn_w_q', 'delta_attn_w_o', 'delta_ffn_w_up', 'delta_ffn_conv_w', 'delta_ffn_conv_b', 'delta_ffn_w_down', 'delta_final_g', 'new_m_ada_w', 'new_m_ada_b', 'new_m_norm1_g', 'new_m_norm2_g', 'new_m_pool_w_in', 'new_m_pool_w_grp', 'new_m_pool_scale', 'new_m_pool_w_out', 'new_m_kv_norm_g', 'new_m_kv_ada_w', 'new_m_kv_ada_b', 'new_m_w_kv', 'new_m_attn_w_q', 'new_m_attn_w_o', 'new_m_ffn_w_up', 'new_m_ffn_conv_w', 'new_m_ffn_conv_b', 'new_m_ffn_w_down', 'new_m_final_g', 'new_v_ada_w', 'new_v_ada_b', 'new_v_norm1_g', 'new_v_norm2_g', 'new_v_pool_w_in', 'new_v_pool_w_grp', 'new_v_pool_scale', 'new_v_pool_w_out', 'new_v_kv_norm_g', 'new_v_kv_ada_w', 'new_v_kv_ada_b', 'new_v_w_kv', 'new_v_attn_w_q', 'new_v_attn_w_o', 'new_v_ffn_w_up', 'new_v_ffn_conv_w', 'new_v_ffn_conv_b', 'new_v_ffn_w_down', 'new_v_final_g']
TWIN_LEAF_KINDS = {'loss': 'loss', 'grad_x': 'grad_x', 'grad_ada_w': 'grad_w', 'grad_ada_b': 'grad_w', 'grad_norm1_g': 'grad_w', 'grad_norm2_g': 'grad_w', 'grad_pool_w_in': 'grad_w', 'grad_pool_w_grp': 'grad_w', 'grad_pool_scale': 'grad_w', 'grad_pool_w_out': 'grad_w', 'grad_kv_norm_g': 'grad_w', 'grad_kv_ada_w': 'grad_w', 'grad_kv_ada_b': 'grad_w', 'grad_w_kv': 'grad_w', 'grad_attn_w_q': 'grad_w', 'grad_attn_w_o': 'grad_w', 'grad_ffn_w_up': 'grad_w', 'grad_ffn_conv_w': 'grad_w', 'grad_ffn_conv_b': 'grad_w', 'grad_ffn_w_down': 'grad_w', 'grad_final_g': 'grad_w', 'delta_ada_w': 'delta_w', 'delta_ada_b': 'delta_w', 'delta_norm1_g': 'delta_w', 'delta_norm2_g': 'delta_w', 'delta_pool_w_in': 'delta_w', 'delta_pool_w_grp': 'delta_w', 'delta_pool_scale': 'delta_w', 'delta_pool_w_out': 'delta_w', 'delta_kv_norm_g': 'delta_w', 'delta_kv_ada_w': 'delta_w', 'delta_kv_ada_b': 'delta_w', 'delta_w_kv': 'delta_w', 'delta_attn_w_q': 'delta_w', 'delta_attn_w_o': 'delta_w', 'delta_ffn_w_up': 'delta_w', 'delta_ffn_conv_w': 'delta_w', 'delta_ffn_conv_b': 'delta_w', 'delta_ffn_w_down': 'delta_w', 'delta_final_g': 'delta_w', 'new_m_ada_w': 'new_m', 'new_m_ada_b': 'new_m', 'new_m_norm1_g': 'new_m', 'new_m_norm2_g': 'new_m', 'new_m_pool_w_in': 'new_m', 'new_m_pool_w_grp': 'new_m', 'new_m_pool_scale': 'new_m', 'new_m_pool_w_out': 'new_m', 'new_m_kv_norm_g': 'new_m', 'new_m_kv_ada_w': 'new_m', 'new_m_kv_ada_b': 'new_m', 'new_m_w_kv': 'new_m', 'new_m_attn_w_q': 'new_m', 'new_m_attn_w_o': 'new_m', 'new_m_ffn_w_up': 'new_m', 'new_m_ffn_conv_w': 'new_m', 'new_m_ffn_conv_b': 'new_m', 'new_m_ffn_w_down': 'new_m', 'new_m_final_g': 'new_m', 'new_v_ada_w': 'new_v', 'new_v_ada_b': 'new_v', 'new_v_norm1_g': 'new_v', 'new_v_norm2_g': 'new_v', 'new_v_pool_w_in': 'new_v', 'new_v_pool_w_grp': 'new_v', 'new_v_pool_scale': 'new_v', 'new_v_pool_w_out': 'new_v', 'new_v_kv_norm_g': 'new_v', 'new_v_kv_ada_w': 'new_v', 'new_v_kv_ada_b': 'new_v', 'new_v_w_kv': 'new_v', 'new_v_attn_w_q': 'new_v', 'new_v_attn_w_o': 'new_v', 'new_v_ffn_w_up': 'new_v', 'new_v_ffn_conv_w': 'new_v', 'new_v_ffn_conv_b': 'new_v', 'new_v_ffn_w_down': 'new_v', 'new_v_final_g': 'new_v'}


def _forward(args):
    return _fwd_reference(*[args[k] for k in FWD_PARAMS])


def _output_shape():
    out = _jax.eval_shape(lambda: _forward(_fwd_setup_inputs(0)))
    return out.shape, out.dtype

N_MICROBATCH = 1
ADAM_LR = 0.001
ADAM_B1 = 0.9
ADAM_B2 = 0.999
ADAM_EPS = 1e-08
ADAM_WD = 0.01
ADAM_STEP = 10
PER_EXAMPLE_BATCH_AXIS = {'x': 0, 'c': 0, 'loss_target': 0}
SHARED_INPUTS = []
_WEIGHT_DTYPES = {'ada_w': _jnp.float32, 'ada_b': _jnp.float32, 'norm1_g': _jnp.float32, 'norm2_g': _jnp.float32, 'pool_w_in': _jnp.float32, 'pool_w_grp': _jnp.float32, 'pool_scale': _jnp.float32, 'pool_w_out': _jnp.float32, 'kv_norm_g': _jnp.float32, 'kv_ada_w': _jnp.float32, 'kv_ada_b': _jnp.float32, 'w_kv': _jnp.float32, 'attn_w_q': _jnp.float32, 'attn_w_o': _jnp.float32, 'ffn_w_up': _jnp.float32, 'ffn_conv_w': _jnp.float32, 'ffn_conv_b': _jnp.float32, 'ffn_w_down': _jnp.float32, 'final_g': _jnp.float32}
MOMENT_SCALE = {'ada_w': 6.233333e-02, 'ada_b': 1.087399e-01, 'norm1_g': 3.991682e-02, 'norm2_g': 5.122819e-02, 'pool_w_in': 5.241963e-02, 'pool_w_grp': 5.250662e-02, 'pool_scale': 5.356019e-02, 'pool_w_out': 5.268079e-02, 'kv_norm_g': 4.101507e-02, 'kv_ada_w': 2.831513e-02, 'kv_ada_b': 4.665953e-02, 'w_kv': 1.799461e-02, 'attn_w_q': 1.144547e-02, 'attn_w_o': 2.400254e-02, 'ffn_w_up': 2.303318e-02, 'ffn_conv_w': 2.351266e-02, 'ffn_conv_b': 2.036020e-02, 'ffn_w_down': 3.774025e-02, 'final_g': 3.213557e+01}


def _to_microbatches(a, axis):
    t = _jnp.moveaxis(a, axis, 0)
    t = t.reshape((N_MICROBATCH, t.shape[0] // N_MICROBATCH) + t.shape[1:])
    return _jnp.moveaxis(t, 1, axis + 1)


def setup_inputs(seed: int = 0) -> dict:
    inp = _fwd_setup_inputs(seed)
    key = _jax.random.fold_in(_jax.random.key(seed), 7919)
    shape, _ = _output_shape()
    out = dict(inp)
    out["loss_target"] = _jax.random.normal(_jax.random.fold_in(key, 0), shape, _jnp.float32)
    for i, name in enumerate(TWIN_WEIGHTS):
        w = inp[name].astype(_jnp.float32)
        if MOMENT_SCALE is None:
            s = _jnp.sqrt(_jnp.mean(_jnp.square(w)) + 1e-30)
        else:
            s = MOMENT_SCALE[name]
        km, kv = _jax.random.split(_jax.random.fold_in(key, i + 1))
        out[name] = w
        out["m_" + name] = s * _jax.random.normal(km, w.shape, _jnp.float32)
        out["v_" + name] = (s * s) * _jax.random.uniform(kv, w.shape, _jnp.float32, 0.5, 1.5)
    if N_MICROBATCH > 1:
        for name, axis in PER_EXAMPLE_BATCH_AXIS.items():
            out[name] = _to_microbatches(out[name], axis)
    return {'x': out['x'], 'c': out['c'], 'ada_w': out['ada_w'], 'ada_b': out['ada_b'], 'norm1_g': out['norm1_g'], 'norm2_g': out['norm2_g'], 'pool_w_in': out['pool_w_in'], 'pool_w_grp': out['pool_w_grp'], 'pool_scale': out['pool_scale'], 'pool_w_out': out['pool_w_out'], 'kv_norm_g': out['kv_norm_g'], 'kv_ada_w': out['kv_ada_w'], 'kv_ada_b': out['kv_ada_b'], 'w_kv': out['w_kv'], 'attn_w_q': out['attn_w_q'], 'attn_w_o': out['attn_w_o'], 'ffn_w_up': out['ffn_w_up'], 'ffn_conv_w': out['ffn_conv_w'], 'ffn_conv_b': out['ffn_conv_b'], 'ffn_w_down': out['ffn_w_down'], 'final_g': out['final_g'], 'loss_target': out['loss_target'], 'm_ada_w': out['m_ada_w'], 'm_ada_b': out['m_ada_b'], 'm_norm1_g': out['m_norm1_g'], 'm_norm2_g': out['m_norm2_g'], 'm_pool_w_in': out['m_pool_w_in'], 'm_pool_w_grp': out['m_pool_w_grp'], 'm_pool_scale': out['m_pool_scale'], 'm_pool_w_out': out['m_pool_w_out'], 'm_kv_norm_g': out['m_kv_norm_g'], 'm_kv_ada_w': out['m_kv_ada_w'], 'm_kv_ada_b': out['m_kv_ada_b'], 'm_w_kv': out['m_w_kv'], 'm_attn_w_q': out['m_attn_w_q'], 'm_attn_w_o': out['m_attn_w_o'], 'm_ffn_w_up': out['m_ffn_w_up'], 'm_ffn_conv_w': out['m_ffn_conv_w'], 'm_ffn_conv_b': out['m_ffn_conv_b'], 'm_ffn_w_down': out['m_ffn_w_down'], 'm_final_g': out['m_final_g'], 'v_ada_w': out['v_ada_w'], 'v_ada_b': out['v_ada_b'], 'v_norm1_g': out['v_norm1_g'], 'v_norm2_g': out['v_norm2_g'], 'v_pool_w_in': out['v_pool_w_in'], 'v_pool_w_grp': out['v_pool_w_grp'], 'v_pool_scale': out['v_pool_scale'], 'v_pool_w_out': out['v_pool_w_out'], 'v_kv_norm_g': out['v_kv_norm_g'], 'v_kv_ada_w': out['v_kv_ada_w'], 'v_kv_ada_b': out['v_kv_ada_b'], 'v_w_kv': out['v_w_kv'], 'v_attn_w_q': out['v_attn_w_q'], 'v_attn_w_o': out['v_attn_w_o'], 'v_ffn_w_up': out['v_ffn_w_up'], 'v_ffn_conv_w': out['v_ffn_conv_w'], 'v_ffn_conv_b': out['v_ffn_conv_b'], 'v_ffn_w_down': out['v_ffn_w_down'], 'v_final_g': out['v_final_g']}


def _loss(weights, diff, rest, loss_target):
    with _jax.named_scope("forward"):
        args = {**rest, TWIN_DIFF_INPUT: diff, **{k: w.astype(_WEIGHT_DTYPES[k]) for k, w in weights.items()}}
        y = _forward(args)
    with _jax.named_scope("loss_head"):
        err = _jnp.square(y.astype(_jnp.float32) - loss_target)
        return 0.5 * _jnp.sum(_jnp.mean(err, axis=-1)) if err.ndim else 0.5 * err


def _adamw(w, g, m, v):
    m = ADAM_B1 * m + (1.0 - ADAM_B1) * g
    v = ADAM_B2 * v + (1.0 - ADAM_B2) * _jnp.square(g)
    m_hat = m / (1.0 - ADAM_B1 ** ADAM_STEP)
    v_hat = v / (1.0 - ADAM_B2 ** ADAM_STEP)
    delta = -ADAM_LR * (m_hat / (_jnp.sqrt(v_hat) + ADAM_EPS) + ADAM_WD * w)
    return delta, m, v


def reference(x, c, ada_w, ada_b, norm1_g, norm2_g, pool_w_in, pool_w_grp, pool_scale, pool_w_out, kv_norm_g, kv_ada_w, kv_ada_b, w_kv, attn_w_q, attn_w_o, ffn_w_up, ffn_conv_w, ffn_conv_b, ffn_w_down, final_g, loss_target, m_ada_w, m_ada_b, m_norm1_g, m_norm2_g, m_pool_w_in, m_pool_w_grp, m_pool_scale, m_pool_w_out, m_kv_norm_g, m_kv_ada_w, m_kv_ada_b, m_w_kv, m_attn_w_q, m_attn_w_o, m_ffn_w_up, m_ffn_conv_w, m_ffn_conv_b, m_ffn_w_down, m_final_g, v_ada_w, v_ada_b, v_norm1_g, v_norm2_g, v_pool_w_in, v_pool_w_grp, v_pool_scale, v_pool_w_out, v_kv_norm_g, v_kv_ada_w, v_kv_ada_b, v_w_kv, v_attn_w_q, v_attn_w_o, v_ffn_w_up, v_ffn_conv_w, v_ffn_conv_b, v_ffn_w_down, v_final_g):
    given = dict(x=x, c=c, ada_w=ada_w, ada_b=ada_b, norm1_g=norm1_g, norm2_g=norm2_g, pool_w_in=pool_w_in, pool_w_grp=pool_w_grp, pool_scale=pool_scale, pool_w_out=pool_w_out, kv_norm_g=kv_norm_g, kv_ada_w=kv_ada_w, kv_ada_b=kv_ada_b, w_kv=w_kv, attn_w_q=attn_w_q, attn_w_o=attn_w_o, ffn_w_up=ffn_w_up, ffn_conv_w=ffn_conv_w, ffn_conv_b=ffn_conv_b, ffn_w_down=ffn_w_down, final_g=final_g, loss_target=loss_target, m_ada_w=m_ada_w, m_ada_b=m_ada_b, m_norm1_g=m_norm1_g, m_norm2_g=m_norm2_g, m_pool_w_in=m_pool_w_in, m_pool_w_grp=m_pool_w_grp, m_pool_scale=m_pool_scale, m_pool_w_out=m_pool_w_out, m_kv_norm_g=m_kv_norm_g, m_kv_ada_w=m_kv_ada_w, m_kv_ada_b=m_kv_ada_b, m_w_kv=m_w_kv, m_attn_w_q=m_attn_w_q, m_attn_w_o=m_attn_w_o, m_ffn_w_up=m_ffn_w_up, m_ffn_conv_w=m_ffn_conv_w, m_ffn_conv_b=m_ffn_conv_b, m_ffn_w_down=m_ffn_w_down, m_final_g=m_final_g, v_ada_w=v_ada_w, v_ada_b=v_ada_b, v_norm1_g=v_norm1_g, v_norm2_g=v_norm2_g, v_pool_w_in=v_pool_w_in, v_pool_w_grp=v_pool_w_grp, v_pool_scale=v_pool_scale, v_pool_w_out=v_pool_w_out, v_kv_norm_g=v_kv_norm_g, v_kv_ada_w=v_kv_ada_w, v_kv_ada_b=v_kv_ada_b, v_w_kv=v_w_kv, v_attn_w_q=v_attn_w_q, v_attn_w_o=v_attn_w_o, v_ffn_w_up=v_ffn_w_up, v_ffn_conv_w=v_ffn_conv_w, v_ffn_conv_b=v_ffn_conv_b, v_ffn_w_down=v_ffn_w_down, v_final_g=v_final_g)
    weights = {n: given[n] for n in TWIN_WEIGHTS}
    shared = {n: given[n] for n in SHARED_INPUTS}
    per_example = {n: given[n] for n in ['x', 'c']}
    grad_fn = _jax.value_and_grad(_loss, argnums=(0, 1))

    def one_microbatch(ex, loss_target):
        ex = dict(ex)
        diff = ex.pop(TWIN_DIFF_INPUT)
        return grad_fn(weights, diff, {**shared, **ex}, loss_target)

    if N_MICROBATCH == 1:
        loss, (grad_w, grad_x) = one_microbatch(per_example, given["loss_target"])
    else:
        def body(carry, xs):
            loss_sum, grad_sum = carry
            l_k, (gw_k, gx_k) = one_microbatch(xs[0], xs[1])
            with _jax.named_scope("update"):
                return (loss_sum + l_k, _jax.tree.map(_jnp.add, grad_sum, gw_k)), gx_k

        init = (_jnp.zeros((), _jnp.float32), _jax.tree.map(_jnp.zeros_like, weights))
        (loss, grad_w), grad_x = _jax.lax.scan(body, init, (per_example, given["loss_target"]))
    with _jax.named_scope("update"):
        delta_w, new_m, new_v = {}, {}, {}
        for n in TWIN_WEIGHTS:
            delta_w[n], new_m[n], new_v[n] = _adamw(weights[n], grad_w[n], given["m_" + n], given["v_" + n])
    return (loss, grad_x, *[grad_w[n] for n in TWIN_WEIGHTS], *[delta_w[n] for n in TWIN_WEIGHTS],
            *[new_m[n] for n in TWIN_WEIGHTS], *[new_v[n] for n in TWIN_WEIGHTS])
```

```python
import functools
import math

import numpy as np
import jax
import jax.numpy as jnp
from jax import lax
from jax.experimental import pallas as pl
from jax.experimental.pallas import tpu as pltpu

F32 = jnp.float32
BF16 = jnp.bfloat16
MESH = pl.DeviceIdType.MESH

POOL_WINDOWS = (2, 4, 8, 16)
BRANCHES = ((128, 1), (512, 4), (2048, 16))
HEAD_DIM = 64
ATTN_BLOCK = 128
EPS = 1e-6
LR, B1, B2, ADAM_EPS, WD, STEP = 0.001, 0.9, 0.999, 1e-08, 0.01, 10

VMEM_LIMIT_BYTES = 56 * 1024 * 1024
LANES = 128
PACK_W = 1024
HALO = 16
NEG = -1e30
N_CHIPS = 4
N_DEV = 8


def _alibi_slopes(n):
    def pow2(m):
        start = 2.0 ** (-(2.0 ** -(math.log2(m) - 3)))
        return [start ** (i + 1) for i in range(m)]
    if math.log2(n).is_integer():
        s = pow2(n)
    else:
        c = 2 ** math.floor(math.log2(n))
        s = pow2(c) + pow2(2 * c)[0::2][: n - c]
    s = np.asarray(s, dtype=np.float32)
    return -np.sort(-s)


def _cparams(sem=None):
    return pltpu.CompilerParams(dimension_semantics=sem, vmem_limit_bytes=VMEM_LIMIT_BYTES)


def _tile(n, pref, unit):
    t = (min(pref, n) // unit) * unit
    while t >= unit:
        if n % t == 0:
            return t
        t -= unit
    return n


def _sigmoid(v):
    return 1.0 / (1.0 + jnp.exp(-v))


def all_gather8(xs, name):
    m_per, n = xs.shape

    def body(x_ref, out_ref, send_sems, recv_sems, local_sem):
        x, y, c = lax.axis_index("x"), lax.axis_index("y"), lax.axis_index("c")
        me, sibling = (x, y, c), (x, y, 1 - c)
        chips = [(1 - x, y), (x, 1 - y), (1 - x, 1 - y)]

        def rows(px, py, pc):
            return out_ref.at[pl.ds((4 * px + 2 * py + pc) * m_per, m_per), :]

        def copy(k, block, to, src=None):
            return pltpu.make_async_remote_copy(src_ref=rows(*block) if src is None else src, dst_ref=rows(*block),
                                                send_sem=send_sems.at[k], recv_sem=recv_sems.at[k], device_id=to, device_id_type=MESH)

        mine = pltpu.make_async_copy(x_ref, rows(*me), local_sem)
        mine.start()
        first = [copy(0, me, sibling, src=x_ref)]
        first += [copy(1 + j, me, (*chip, c), src=x_ref) for j, chip in enumerate(chips)]
        for cp in first:
            cp.start()
        passed = [copy(4 + j, (*chip, c), sibling) for j, chip in enumerate(chips)]
        for j, chip in enumerate(chips):
            copy(1 + j, (*chip, c), me).wait_recv()
            passed[j].start()
        copy(0, sibling, me).wait_recv()
        for j, chip in enumerate(chips):
            copy(4 + j, (*chip, 1 - c), me).wait_recv()
        for cp in first + passed:
            cp.wait_send()
        mine.wait()

    return pl.pallas_call(
        body, name=name,
        out_shape=jax.ShapeDtypeStruct((N_DEV * m_per, n), xs.dtype),
        in_specs=[pl.BlockSpec(memory_space=pltpu.VMEM)],
        out_specs=pl.BlockSpec(memory_space=pltpu.VMEM),
        scratch_shapes=[pltpu.SemaphoreType.DMA((7,)), pltpu.SemaphoreType.DMA((7,)), pltpu.SemaphoreType.DMA],
        compiler_params=pltpu.CompilerParams(vmem_limit_bytes=VMEM_LIMIT_BYTES),
    )(xs)


HBM_SPEC = pl.BlockSpec(memory_space=pltpu.HBM)


def _slot(ref, chip_major, piece, chip):
    return ref.at[chip, piece] if chip_major else ref.at[piece, chip]


def gather_weights(arrs, chip_major, name):
    n = len(arrs)
    n_mine = sum(a.shape[0] // 2 for a in arrs)
    n_pieces = sum(a.shape[0] for a in arrs)

    def body(*refs):
        ins, outs = refs[:n], refs[n:2 * n]
        send_sems, recv_sems, loc_sems = refs[2 * n:]
        x, y, c = lax.axis_index("x"), lax.axis_index("y"), lax.axis_index("c")
        p_me = 2 * x + y
        chips = [(1 - x, y), (x, 1 - y), (1 - x, 1 - y)]
        sib = (x, y, 1 - c)

        def rcopy(src, dst, k, to):
            return pltpu.make_async_remote_copy(src_ref=src, dst_ref=dst, send_sem=send_sems.at[k], recv_sem=recv_sems.at[k],
                                                device_id=to, device_id_type=MESH)

        locs, li = [], 0
        for a in range(n):
            for i in range(ins[a].shape[0]):
                cp = pltpu.make_async_copy(ins[a].at[i], _slot(outs[a], chip_major[a], i, p_me), loc_sems.at[li])
                cp.start()
                locs.append(cp)
                li += 1
        sends, plan, k = [], [], 0
        for a in range(n):
            for t in range(ins[a].shape[0] // 2):
                ii = 2 * t + c
                for j, chip in enumerate(chips):
                    cp = rcopy(ins[a].at[ii], _slot(outs[a], chip_major[a], ii, p_me), k + j, (*chip, c))
                    cp.start()
                    sends.append(cp)
                plan.append((a, t, k))
                k += 6
        for a, t, k in plan:
            ii = 2 * t + c
            for j, chip in enumerate(chips):
                blk = _slot(outs[a], chip_major[a], ii, 2 * chip[0] + chip[1])
                rcopy(blk, blk, k + j, (*chip, c)).wait_recv()
                fw = rcopy(blk, blk, k + 3 + j, sib)
                fw.start()
                sends.append(fw)
        for a, t, k in plan:
            io = 2 * t + 1 - c
            for j, chip in enumerate(chips):
                blk = _slot(outs[a], chip_major[a], io, 2 * chip[0] + chip[1])
                rcopy(blk, blk, k + 3 + j, sib).wait_recv()
        for cp in sends:
            cp.wait_send()
        for cp in locs:
            cp.wait()

    def oshape(a, cm):
        lead = (N_CHIPS, a.shape[0]) if cm else (a.shape[0], N_CHIPS)
        return jax.ShapeDtypeStruct(lead + a.shape[1:], a.dtype)

    return pl.pallas_call(
        body, name=name,
        out_shape=[oshape(a, cm) for a, cm in zip(arrs, chip_major)],
        in_specs=[HBM_SPEC] * n, out_specs=[HBM_SPEC] * n,
        scratch_shapes=[pltpu.SemaphoreType.DMA((6 * n_mine,)), pltpu.SemaphoreType.DMA((6 * n_mine,)), pltpu.SemaphoreType.DMA((n_pieces,))],
    )(*arrs)


def exchange_grads(arrs, chip_major, name):
    n = len(arrs)
    n_l = [a.shape[1] if cm else a.shape[0] for a, cm in zip(arrs, chip_major)]
    n_mine = sum(l // 2 for l in n_l)

    def body(*refs):
        ins, outs = refs[:n], refs[n:2 * n]
        send_sems, recv_sems = refs[2 * n:]
        x, y, c = lax.axis_index("x"), lax.axis_index("y"), lax.axis_index("c")
        chips = [(1 - x, y), (x, 1 - y), (1 - x, 1 - y)]
        sends, plan, k = [], [], 0
        for a in range(n):
            for t in range(n_l[a] // 2):
                ii, io = 2 * t + c, 2 * t + 1 - c
                targets = [(ii, chip, c, j) for j, chip in enumerate(chips)]
                targets += [(io, chip, 1 - c, 3 + j) for j, chip in enumerate([(x, y)] + chips)]
                for piece, chip, core, rel in targets:
                    cp = pltpu.make_async_remote_copy(
                        src_ref=_slot(ins[a], chip_major[a], piece, 2 * chip[0] + chip[1]), dst_ref=outs[a].at[t, rel],
                        send_sem=send_sems.at[k + rel], recv_sem=recv_sems.at[k + rel], device_id=(*chip, core), device_id_type=MESH)
                    cp.start()
                    sends.append(cp)
                plan.append((a, t, k))
                k += 7
        for a, t, k in plan:
            for rel in range(7):
                blk = outs[a].at[t, rel]
                pltpu.make_async_remote_copy(src_ref=blk, dst_ref=blk, send_sem=send_sems.at[k + rel], recv_sem=recv_sems.at[k + rel],
                                             device_id=(x, y, 1 - c), device_id_type=MESH).wait_recv()
        for cp in sends:
            cp.wait_send()

    def oshape(a, cm, l):
        return jax.ShapeDtypeStruct((l // 2, 7) + a.shape[2:], a.dtype)

    return pl.pallas_call(
        body, name=name,
        out_shape=[oshape(a, cm, l) for a, cm, l in zip(arrs, chip_major, n_l)],
        in_specs=[HBM_SPEC] * n, out_specs=[HBM_SPEC] * n,
        scratch_shapes=[pltpu.SemaphoreType.DMA((7 * n_mine,)), pltpu.SemaphoreType.DMA((7 * n_mine,))],
    )(*arrs)


def swap_halves(arrs, name):
    n = len(arrs)
    n_mine = sum(a.shape[0] // 2 for a in arrs)

    def body(*refs):
        ins, outs = refs[:n], refs[n:2 * n]
        send_sems, recv_sems = refs[2 * n:]
        x, y, c = lax.axis_index("x"), lax.axis_index("y"), lax.axis_index("c")
        sib = (x, y, 1 - c)
        sends, k = [], 0
        for a in range(n):
            for t in range(ins[a].shape[0] // 2):
                cp = pltpu.make_async_remote_copy(src_ref=ins[a].at[2 * t + c], dst_ref=outs[a].at[2 * t + c], send_sem=send_sems.at[k],
                                                  recv_sem=recv_sems.at[k], device_id=sib, device_id_type=MESH)
                cp.start()
                sends.append((cp, a, t, k))
                k += 1
        for cp, a, t, k in sends:
            blk = outs[a].at[2 * t + 1 - c]
            pltpu.make_async_remote_copy(src_ref=blk, dst_ref=blk, send_sem=send_sems.at[k], recv_sem=recv_sems.at[k],
                                         device_id=sib, device_id_type=MESH).wait_recv()
        for cp, a, t, k in sends:
            cp.wait_send()

    return pl.pallas_call(
        body, name=name,
        out_shape=[jax.ShapeDtypeStruct(a.shape, a.dtype) for a in arrs],
        in_specs=[HBM_SPEC] * n, out_specs=[HBM_SPEC] * n,
        input_output_aliases={a: a for a in range(n)},
        scratch_shapes=[pltpu.SemaphoreType.DMA((n_mine,)), pltpu.SemaphoreType.DMA((n_mine,))],
    )(*arrs)


def mm_nn(a, w4, layer, *, tn, out_dtype, name, colmap=None, res=None):
    M, K = a.shape
    _, P, _, Ns = w4.shape
    nper = Ns // tn
    ncb = P * nper
    tm = _tile(M, 512, 8)
    cmap = colmap if colmap is not None else (lambda cb: cb)

    def body(*refs):
        if res is None:
            a_ref, w_ref, o_ref = refs
        else:
            a_ref, w_ref, x_ref, g_ref, o_ref, xo_ref = refs
        acc = jnp.dot(a_ref[...].astype(BF16), w_ref[...], preferred_element_type=F32)
        o_ref[...] = acc.astype(o_ref.dtype)
        if res is not None:
            xo_ref[...] = x_ref[...] + g_ref[...] * acc

    in_specs = [pl.BlockSpec((tm, K), lambda i, j: (i, 0)),
                pl.BlockSpec((None, None, K, tn), lambda i, j: (layer, j // nper, 0, j % nper))]
    out_specs = [pl.BlockSpec((tm, tn), lambda i, j: (i, cmap(j)))]
    out_shape = [jax.ShapeDtypeStruct((M, ncb * tn), out_dtype)]
    args = [a, w4]
    if res is not None:
        in_specs += [pl.BlockSpec((tm, tn), lambda i, j: (i, j)), pl.BlockSpec((1, tn), lambda i, j: (0, j))]
        out_specs.append(pl.BlockSpec((tm, tn), lambda i, j: (i, j)))
        out_shape.append(jax.ShapeDtypeStruct((M, ncb * tn), F32))
        args += [res[0], res[1]]
    outs = pl.pallas_call(body, name=name, grid=(M // tm, ncb), in_specs=in_specs, out_specs=out_specs, out_shape=out_shape,
                          compiler_params=_cparams(("parallel", "arbitrary")))(*args)
    return outs[0] if res is None else (outs[0], outs[1])


def mm_nt(g3, w4, layer, *, tn, tk, out_dtype, name, gmap=None):
    _, M, _ = g3.shape
    _, P, K, Ns = w4.shape
    nper = Ns // tn
    ns = P * nper
    tm = _tile(M, 512, 8)
    gm = gmap if gmap is not None else (lambda s: (0, s))

    def body(g_ref, w_ref, o_ref, acc):
        s = pl.program_id(2)

        @pl.when(s == 0)
        def _():
            acc[...] = jnp.zeros_like(acc)
        acc[...] += lax.dot_general(g_ref[...].astype(BF16), w_ref[...], (((1,), (1,)), ((), ())), preferred_element_type=F32)

        @pl.when(s == ns - 1)
        def _():
            o_ref[...] = acc[...].astype(o_ref.dtype)

    return pl.pallas_call(
        body, name=name, grid=(M // tm, K // tk, ns),
        in_specs=[pl.BlockSpec((None, tm, tn), lambda i, kj, s: (gm(s)[0], i, gm(s)[1])),
                  pl.BlockSpec((None, None, tk, tn), lambda i, kj, s: (layer, s // nper, kj, s % nper))],
        out_specs=pl.BlockSpec((tm, tk), lambda i, kj, s: (i, kj)),
        out_shape=jax.ShapeDtypeStruct((M, K), out_dtype),
        scratch_shapes=[pltpu.VMEM((tm, tk), F32)],
        compiler_params=_cparams(("parallel", "parallel", "arbitrary")))(g3, w4)


def mm_tn(a, g3, dw_prev, layer, wshape, *, tn, tk, name, gmap=None):
    M, K = a.shape
    L, P, _, Ns = wshape
    nper = Ns // tn
    ns = P * nper
    tm = _tile(M, 512, 16)
    nm = M // tm
    gm = gmap if gmap is not None else (lambda s: (0, s))

    def body(*refs):
        a_ref, g_ref = refs[0], refs[1]
        o_ref, acc = refs[-2], refs[-1]
        mi = pl.program_id(2)

        @pl.when(mi == 0)
        def _():
            acc[...] = jnp.zeros_like(acc)
        acc[...] += lax.dot_general(a_ref[...].astype(BF16), g_ref[...].astype(BF16), (((0,), (0,)), ((), ())), preferred_element_type=F32)

        @pl.when(mi == nm - 1)
        def _():
            o_ref[...] = acc[...].astype(o_ref.dtype)

    in_specs = [pl.BlockSpec((tm, tk), lambda s, kj, mi: (mi, kj)),
                pl.BlockSpec((None, tm, tn), lambda s, kj, mi: (gm(s)[0], mi, gm(s)[1]))]
    args = [a, g3]
    aliases = {}
    if dw_prev is not None:
        in_specs.append(pl.BlockSpec(memory_space=pl.ANY))
        args.append(dw_prev)
        aliases = {2: 0}
    return pl.pallas_call(
        body, name=name, grid=(ns, K // tk, nm), in_specs=in_specs,
        out_specs=pl.BlockSpec((None, None, tk, tn), lambda s, kj, mi: (layer, s // nper, kj, s % nper)),
        out_shape=jax.ShapeDtypeStruct((L, P, K, Ns), BF16),
        scratch_shapes=[pltpu.VMEM((tk, tn), F32)], input_output_aliases=aliases,
        compiler_params=_cparams(("parallel", "parallel", "arbitrary")))(*args)


def _vspec(d):
    return pl.BlockSpec((1, d), lambda i: (0, 0))


def norm_mod(x, g, sh, sc, name):
    S, D = x.shape
    tm = _tile(S, 512, 16)

    def body(x_ref, g_ref, sh_ref, sc_ref, o_ref):
        xv = x_ref[...]
        r = lax.rsqrt(jnp.mean(xv * xv, axis=-1, keepdims=True) + EPS)
        o_ref[...] = ((xv * r) * g_ref[...] * (1.0 + sc_ref[...]) + sh_ref[...]).astype(o_ref.dtype)

    return pl.pallas_call(body, name=name, grid=(S // tm,),
                          in_specs=[pl.BlockSpec((tm, D), lambda i: (i, 0)), _vspec(D), _vspec(D), _vspec(D)],
                          out_specs=pl.BlockSpec((tm, D), lambda i: (i, 0)), out_shape=jax.ShapeDtypeStruct((S, D), BF16),
                          compiler_params=_cparams(("parallel",)))(x, g, sh, sc)


def norm_mod_bwd(dh, x, dres, g, sc, name):
    S, D = x.shape
    tm = _tile(S, 256, 8)

    def body(dh_ref, x_ref, dr_ref, g_ref, sc_ref, dx_ref, sums_ref):
        xv = x_ref[...]
        dhv = dh_ref[...].astype(F32)
        r = lax.rsqrt(jnp.mean(xv * xv, axis=-1, keepdims=True) + EPS)
        xn = xv * r
        one_sc = 1.0 + sc_ref[...]
        dxn = dhv * g_ref[...] * one_sc
        dx = r * (dxn - xn * jnp.mean(dxn * xn, axis=-1, keepdims=True))
        dx_ref[...] = dx + dr_ref[...]
        part = jnp.concatenate([jnp.sum(dhv, axis=0, keepdims=True), jnp.sum(dhv * xn * g_ref[...], axis=0, keepdims=True),
                                jnp.sum(dhv * one_sc * xn, axis=0, keepdims=True), jnp.zeros((5, D), F32)], axis=0)

        @pl.when(pl.program_id(0) == 0)
        def _():
            sums_ref[...] = jnp.zeros_like(sums_ref)
        sums_ref[...] += part

    row = pl.BlockSpec((tm, D), lambda i: (i, 0))
    return pl.pallas_call(body, name=name, grid=(S // tm,), in_specs=[row, row, row, _vspec(D), _vspec(D)],
                          out_specs=[row, pl.BlockSpec((8, D), lambda i: (0, 0))],
                          out_shape=[jax.ShapeDtypeStruct((S, D), F32), jax.ShapeDtypeStruct((8, D), F32)],
                          compiler_params=_cparams(("arbitrary",)))(dh, x, dres, g, sc)


def gate_bwd(dx, out, gate, name):
    S, D = dx.shape
    tm = _tile(S, 512, 16)

    def body(dx_ref, o_ref, g_ref, do_ref, sums_ref):
        dxv = dx_ref[...]
        do_ref[...] = (g_ref[...] * dxv).astype(do_ref.dtype)
        part = jnp.concatenate([jnp.sum(dxv * o_ref[...].astype(F32), axis=0, keepdims=True), jnp.zeros((7, D), F32)], axis=0)

        @pl.when(pl.program_id(0) == 0)
        def _():
            sums_ref[...] = jnp.zeros_like(sums_ref)
        sums_ref[...] += part

    row = pl.BlockSpec((tm, D), lambda i: (i, 0))
    return pl.pallas_call(body, name=name, grid=(S // tm,), in_specs=[row, row, _vspec(D)],
                          out_specs=[row, pl.BlockSpec((8, D), lambda i: (0, 0))],
                          out_shape=[jax.ShapeDtypeStruct((S, D), BF16), jax.ShapeDtypeStruct((8, D), F32)],
                          compiler_params=_cparams(("arbitrary",)))(dx, out, gate)


def loss_fwd_bwd(x, g, target, name):
    S, D = x.shape
    tm = _tile(S, 256, 8)

    def body(x_ref, g_ref, t_ref, dx_ref, sums_ref):
        xv = x_ref[...]
        r = lax.rsqrt(jnp.mean(xv * xv, axis=-1, keepdims=True) + EPS)
        xn = xv * r
        err = xn * g_ref[...] - t_ref[...]
        dy = err * (1.0 / D)
        dxn = dy * g_ref[...]
        dx_ref[...] = r * (dxn - xn * jnp.mean(dxn * xn, axis=-1, keepdims=True))
        part = jnp.concatenate([jnp.sum(dy * xn, axis=0, keepdims=True), jnp.sum(err * err, axis=0, keepdims=True),
                                jnp.zeros((6, D), F32)], axis=0)

        @pl.when(pl.program_id(0) == 0)
        def _():
            sums_ref[...] = jnp.zeros_like(sums_ref)
        sums_ref[...] += part

    row = pl.BlockSpec((tm, D), lambda i: (i, 0))
    return pl.pallas_call(body, name=name, grid=(S // tm,), in_specs=[row, _vspec(D), row],
                          out_specs=[row, pl.BlockSpec((8, D), lambda i: (0, 0))],
                          out_shape=[jax.ShapeDtypeStruct((S, D), F32), jax.ShapeDtypeStruct((8, D), F32)],
                          compiler_params=_cparams(("arbitrary",)))(x, g, target)


def pool_fwd(u, wgrp, layer, scale, name):
    S, D = u.shape
    G = len(POOL_WINDOWS)
    C = D // G
    tm = _tile(S, 256, 16)
    hb = tm // HALO

    def body(up_ref, uc_ref, w_ref, sc_ref, p_ref, z_ref, y_ref):
        i = pl.program_id(0)
        prev = jnp.where(i > 0, up_ref[...], 0.0)
        ext = jnp.concatenate([prev, uc_ref[...]], axis=0)
        t = i * tm + lax.broadcasted_iota(jnp.int32, (tm, 1), 0)
        for gi, w in enumerate(POOL_WINDOWS):
            cs = slice(gi * C, (gi + 1) * C)
            e = ext[:, cs]
            s, k = e, 1
            while k < w:
                s = s + pltpu.roll(s, k, 0)
                k *= 2
            cnt = jnp.minimum(t + 1, w).astype(F32)
            pooled = (s[HALO:] / cnt - e[HALO:]).astype(BF16)
            p_ref[:, cs] = pooled
            z = jnp.dot(pooled, w_ref[:, gi].reshape(C, C), preferred_element_type=F32)
            z_ref[:, cs] = z.astype(BF16)
            y_ref[:, cs] = (z * sc_ref[:, cs]).astype(BF16)

    row = pl.BlockSpec((tm, D), lambda i: (i, 0))
    return pl.pallas_call(
        body, name=name, grid=(S // tm,),
        in_specs=[pl.BlockSpec((HALO, D), lambda i: (jnp.maximum(i * hb - 1, 0), 0)), row,
                  pl.BlockSpec((None,) + wgrp.shape[1:], lambda i: (layer, 0, 0, 0, 0)), _vspec(D)],
        out_specs=[row, row, row], out_shape=[jax.ShapeDtypeStruct((S, D), BF16)] * 3,
        compiler_params=_cparams(("parallel",)))(u, u, wgrp, scale)


def pool_bwd(dys, z, pooled, wgrp, layer, scale, name):
    S, D = dys.shape
    G = len(POOL_WINDOWS)
    C = D // G
    tm = _tile(S, 256, 16)
    hb = tm // HALO
    nt = S // tm
    n_ext = tm + HALO

    def body(dc_ref, dn_ref, z_ref, p_ref, w_ref, sc_ref, du_ref, dw_ref, sums_ref):
        i = pl.program_id(0)

        @pl.when(i == 0)
        def _():
            dw_ref[...] = jnp.zeros_like(dw_ref)
            sums_ref[...] = jnp.zeros_like(sums_ref)
        dyc = dc_ref[...].astype(F32)
        nxt = jnp.where(i < nt - 1, dn_ref[...].astype(F32), 0.0)
        ext = jnp.concatenate([dyc, nxt], axis=0)
        sums_ref[...] += jnp.concatenate([jnp.sum(dyc * z_ref[...].astype(F32), axis=0, keepdims=True), jnp.zeros((7, D), F32)], axis=0)
        t = i * tm + lax.broadcasted_iota(jnp.int32, (n_ext, 1), 0)
        for gi, w in enumerate(POOL_WINDOWS):
            cs = slice(gi * C, (gi + 1) * C)
            wg = w_ref[:, gi].reshape(C, C)
            dz = (ext[:, cs] * sc_ref[:, cs]).astype(BF16)
            dpool = lax.dot_general(dz, wg, (((1,), (1,)), ((), ())), preferred_element_type=F32)
            dw_ref[gi] += lax.dot_general(p_ref[:, cs], dz[:tm], (((0,), (0,)), ((), ())), preferred_element_type=F32)
            cnt = jnp.minimum(t + 1, w).astype(F32)
            s, k = dpool / cnt, 1
            while k < w:
                s = s + pltpu.roll(s, n_ext - k, 0)
                k *= 2
            du_ref[:, cs] = (s[:tm] - dpool[:tm]).astype(BF16)

    row = pl.BlockSpec((tm, D), lambda i: (i, 0))
    return pl.pallas_call(
        body, name=name, grid=(nt,),
        in_specs=[row, pl.BlockSpec((HALO, D), lambda i: (jnp.minimum((i + 1) * hb, S // HALO - 1), 0)), row, row,
                  pl.BlockSpec((None,) + wgrp.shape[1:], lambda i: (layer, 0, 0, 0, 0)), _vspec(D)],
        out_specs=[row, pl.BlockSpec((G, C, C), lambda i: (0, 0, 0)), pl.BlockSpec((8, D), lambda i: (0, 0))],
        out_shape=[jax.ShapeDtypeStruct((S, D), BF16), jax.ShapeDtypeStruct((G, C, C), F32), jax.ShapeDtypeStruct((8, D), F32)],
        compiler_params=_cparams(("arbitrary",)))(dys, dys, z, pooled, wgrp, scale)


def ffn_act(hu, conv_w, conv_b, name):
    S, F2 = hu.shape
    F = F2 // 2
    tm = _tile(S, 256, 16)
    tn = _tile(F, 1408, LANES)
    nb = F // tn
    hb = tm // HALO

    def body(ap_ref, ac_ref, v_ref, w_ref, b_ref, o_ref):
        i = pl.program_id(0)
        prev = jnp.where(i > 0, ap_ref[...].astype(F32), 0.0)
        ext = jnp.concatenate([prev, ac_ref[...].astype(F32)], axis=0)
        conv = b_ref[...] + pltpu.roll(ext, 2, 0) * w_ref[0:1, :] + pltpu.roll(ext, 1, 0) * w_ref[1:2, :] + ext * w_ref[2:3, :]
        conv = conv[HALO:]
        o_ref[...] = (conv * _sigmoid(conv) * v_ref[...].astype(F32)).astype(o_ref.dtype)

    return pl.pallas_call(
        body, name=name, grid=(S // tm, nb),
        in_specs=[pl.BlockSpec((HALO, tn), lambda i, j: (jnp.maximum(i * hb - 1, 0), j)), pl.BlockSpec((tm, tn), lambda i, j: (i, j)),
                  pl.BlockSpec((tm, tn), lambda i, j: (i, j + nb)), pl.BlockSpec((3, tn), lambda i, j: (0, j)),
                  pl.BlockSpec((1, tn), lambda i, j: (0, j))],
        out_specs=pl.BlockSpec((tm, tn), lambda i, j: (i, j)), out_shape=jax.ShapeDtypeStruct((S, F), BF16),
        compiler_params=_cparams(("parallel", "parallel")))(hu, hu, hu, conv_w, conv_b)


def ffn_act_bwd(dg, hu, conv_w, conv_b, name):
    S, F = dg.shape
    tm = _tile(S, 256, 16)
    tn = _tile(F, 1408, LANES)
    nb = F // tn
    hb = tm // HALO
    nt = S // tm
    n_ext = tm + 2 * HALO

    def body(gc_ref, gn_ref, ap_ref, ac_ref, an_ref, vc_ref, vn_ref, w_ref, b_ref, o_ref, sums_ref):
        i = pl.program_id(1)

        @pl.when(i == 0)
        def _():
            sums_ref[...] = jnp.zeros_like(sums_ref)
        zeros = jnp.zeros((HALO, tn), F32)
        not_last = i < nt - 1
        a_ext = jnp.concatenate([jnp.where(i > 0, ap_ref[...].astype(F32), 0.0), ac_ref[...].astype(F32), an_ref[...].astype(F32)], axis=0)
        v_ext = jnp.concatenate([zeros, vc_ref[...].astype(F32), vn_ref[...].astype(F32)], axis=0)
        g_ext = jnp.concatenate([zeros, gc_ref[...].astype(F32), jnp.where(not_last, gn_ref[...].astype(F32), 0.0)], axis=0)
        w0, w1, w2 = w_ref[0:1, :], w_ref[1:2, :], w_ref[2:3, :]
        a_m2, a_m1 = pltpu.roll(a_ext, 2, 0), pltpu.roll(a_ext, 1, 0)
        conv = b_ref[...] + a_m2 * w0 + a_m1 * w1 + a_ext * w2
        sig = _sigmoid(conv)
        silu = conv * sig
        dsilu = sig * (1.0 + conv * (1.0 - sig))
        dconv = g_ext * v_ext * dsilu
        da = dconv * w2 + pltpu.roll(dconv, n_ext - 1, 0) * w1 + pltpu.roll(dconv, n_ext - 2, 0) * w0
        cur = slice(HALO, HALO + tm)
        o_ref[0] = da[cur].astype(o_ref.dtype)
        o_ref[1] = (g_ext * silu)[cur].astype(o_ref.dtype)
        dc = dconv[cur]
        part = jnp.concatenate([jnp.sum(dc * a_m2[cur], axis=0, keepdims=True), jnp.sum(dc * a_m1[cur], axis=0, keepdims=True),
                                jnp.sum(dc * a_ext[cur], axis=0, keepdims=True), jnp.sum(dc, axis=0, keepdims=True),
                                jnp.zeros((4, tn), F32)], axis=0)
        sums_ref[...] += part

    def prev(i):
        return jnp.maximum(i * hb - 1, 0)

    def nxt(i):
        return jnp.minimum((i + 1) * hb, S // HALO - 1)

    return pl.pallas_call(
        body, name=name, grid=(nb, nt),
        in_specs=[pl.BlockSpec((tm, tn), lambda j, i: (i, j)), pl.BlockSpec((HALO, tn), lambda j, i: (nxt(i), j)),
                  pl.BlockSpec((HALO, tn), lambda j, i: (prev(i), j)), pl.BlockSpec((tm, tn), lambda j, i: (i, j)),
                  pl.BlockSpec((HALO, tn), lambda j, i: (nxt(i), j)),
                  pl.BlockSpec((tm, tn), lambda j, i: (i, j + nb)), pl.BlockSpec((HALO, tn), lambda j, i: (nxt(i), j + nb)),
                  pl.BlockSpec((3, tn), lambda j, i: (0, j)), pl.BlockSpec((1, tn), lambda j, i: (0, j))],
        out_specs=[pl.BlockSpec((2, tm, tn), lambda j, i: (0, i, j)), pl.BlockSpec((8, tn), lambda j, i: (0, j))],
        out_shape=[jax.ShapeDtypeStruct((2, S, F), BF16), jax.ShapeDtypeStruct((8, F), F32)],
        compiler_params=_cparams(("parallel", "arbitrary")))(dg, dg, hu, hu, hu, hu, hu, conv_w, conv_b)


def _head_expander(n_heads, da):
    e = np.zeros((LANES, da), np.float32)
    for h in range(n_heads):
        e[h, h * HEAD_DIM:(h + 1) * HEAD_DIM] = 1.0
    return jnp.asarray(e, BF16)


def _split_dot(v, e, dims):
    hi = v.astype(BF16)
    lo = (v - hi.astype(F32)).astype(BF16)
    return (lax.dot_general(hi, e, dims, preferred_element_type=F32) + lax.dot_general(lo, e, dims, preferred_element_type=F32))


def _lane_col(tile, h):
    lane = lax.broadcasted_iota(jnp.int32, tile.shape, 1)
    return jnp.sum(jnp.where(lane == h, tile, 0.0), axis=1, keepdims=True)


def attn_branch_fwd(q, kv, gi, slopes, name):
    S = q.shape[0]
    DA = q.shape[1] // len(BRANCHES)
    H = DA // HEAD_DIM
    window, d = BRANCHES[gi]
    n_steps = window // d
    blk = ATTN_BLOCK
    assert n_steps == blk and (S // d) % blk == 0
    nbs = S // d // blk
    scale = HEAD_DIM ** -0.5
    qv = q.reshape(S // d, d * q.shape[1])
    kvv = kv.reshape(S // d, d * kv.shape[1])

    def body(q_ref, kp_ref, kc_ref, vp_ref, vc_ref, o_ref, l_ref):
        jb = pl.program_id(1)
        row = lax.broadcasted_iota(jnp.int32, (blk, 2 * blk), 0)
        col = lax.broadcasted_iota(jnp.int32, (blk, 2 * blk), 1)
        delta = row + blk - col
        valid = (delta >= 0) & (delta <= n_steps) & ((col >= blk) | (jb > 0))
        dist = (delta * d).astype(F32)
        lane = lax.broadcasted_iota(jnp.int32, (blk, LANES), 1)
        ltile = jnp.zeros((blk, LANES), F32)
        for h in range(H):
            hs = slice(h * HEAD_DIM, (h + 1) * HEAD_DIM)
            k2 = jnp.concatenate([kp_ref[:, hs], kc_ref[:, hs]], axis=0)
            v2 = jnp.concatenate([vp_ref[:, hs], vc_ref[:, hs]], axis=0)
            s = lax.dot_general(q_ref[:, hs], k2, (((1,), (1,)), ((), ())), preferred_element_type=F32) * scale
            s = jnp.where(valid, s - float(slopes[h]) * dist, NEG)
            m = jnp.max(s, axis=-1, keepdims=True)
            p = jnp.exp(s - m)
            l = jnp.sum(p, axis=-1, keepdims=True)
            o_ref[:, hs] = jnp.dot(p.astype(BF16), v2, preferred_element_type=F32) / l
            ltile = jnp.where(lane == h, m + jnp.log(l), ltile)
        l_ref[...] = ltile

    nq, nk = len(BRANCHES), 2 * len(BRANCHES)
    kb = pl.BlockSpec((blk, DA), lambda r, jb: (jb, r * nk + 2 * gi))
    kpb = pl.BlockSpec((blk, DA), lambda r, jb: (jnp.maximum(jb - 1, 0), r * nk + 2 * gi))
    vb = pl.BlockSpec((blk, DA), lambda r, jb: (jb, r * nk + 2 * gi + 1))
    vpb = pl.BlockSpec((blk, DA), lambda r, jb: (jnp.maximum(jb - 1, 0), r * nk + 2 * gi + 1))
    o, lse = pl.pallas_call(
        body, name=name, grid=(d, nbs),
        in_specs=[pl.BlockSpec((blk, DA), lambda r, jb: (jb, r * nq + gi)), kpb, kb, vpb, vb],
        out_specs=[pl.BlockSpec((blk, DA), lambda r, jb: (jb, r)), pl.BlockSpec((blk, LANES), lambda r, jb: (jb, r))],
        out_shape=[jax.ShapeDtypeStruct((S // d, d * DA), F32), jax.ShapeDtypeStruct((S // d, d * LANES), F32)],
        compiler_params=_cparams(("parallel", "parallel")))(qv, kvv, kvv, kvv, kvv)
    return o.reshape(S, DA), lse.reshape(S, LANES)


def attn_combine(os_, lses, name):
    S, DA = os_[0].shape
    H = DA // HEAD_DIM
    tm = _tile(S, 256, 16)
    expander = _head_expander(H, DA)
    nbr = len(os_)

    def body(*refs):
        o_refs, l_refs, e_ref = refs[:nbr], refs[nbr:2 * nbr], refs[2 * nbr]
        out_ref, lse_ref = refs[2 * nbr + 1:]
        ls = [r[...] for r in l_refs]
        lmax = functools.reduce(jnp.maximum, ls)
        es = [jnp.exp(l - lmax) for l in ls]
        den = functools.reduce(lambda a, b: a + b, es)
        lse_ref[...] = lmax + jnp.log(den)
        acc = jnp.zeros((tm, DA), F32)
        for e, o_ref in zip(es, o_refs):
            acc = acc + _split_dot(e / den, e_ref[...], (((1,), (0,)), ((), ()))) * o_ref[...]
        out_ref[...] = acc.astype(out_ref.dtype)

    row = pl.BlockSpec((tm, DA), lambda i: (i, 0))
    lrow = pl.BlockSpec((tm, LANES), lambda i: (i, 0))
    return pl.pallas_call(
        body, name=name, grid=(S // tm,),
        in_specs=[row] * nbr + [lrow] * nbr + [pl.BlockSpec((LANES, DA), lambda i: (0, 0))],
        out_specs=[row, lrow], out_shape=[jax.ShapeDtypeStruct((S, DA), BF16), jax.ShapeDtypeStruct((S, LANES), F32)],
        compiler_params=_cparams(("parallel",)))(*os_, *lses, expander)


def attn_delta(do, o, name):
    S, DA = o.shape
    H = DA // HEAD_DIM
    tm = _tile(S, 512, 16)
    expander = _head_expander(H, DA)

    def body(do_ref, o_ref, e_ref, d_ref):
        prod = do_ref[...].astype(F32) * o_ref[...].astype(F32)
        d_ref[...] = _split_dot(prod, e_ref[...], (((1,), (1,)), ((), ())))

    row = pl.BlockSpec((tm, DA), lambda i: (i, 0))
    return pl.pallas_call(body, name=name, grid=(S // tm,), in_specs=[row, row, pl.BlockSpec((LANES, DA), lambda i: (0, 0))],
                          out_specs=pl.BlockSpec((tm, LANES), lambda i: (i, 0)), out_shape=jax.ShapeDtypeStruct((S, LANES), F32),
                          compiler_params=_cparams(("parallel",)))(do, o, expander)


def attn_branch_dq(q, kv, do, lse, dlt, dq_prev, gi, slopes, name):
    S = q.shape[0]
    DA = q.shape[1] // len(BRANCHES)
    H = DA // HEAD_DIM
    window, d = BRANCHES[gi]
    n_steps = window // d
    blk = ATTN_BLOCK
    nbs = S // d // blk
    scale = HEAD_DIM ** -0.5
    qv = q.reshape(S // d, d * q.shape[1])
    kvv = kv.reshape(S // d, d * kv.shape[1])
    dov = do.reshape(S // d, d * DA)
    lv = lse.reshape(S // d, d * LANES)
    dv_ = dlt.reshape(S // d, d * LANES)

    def body(*refs):
        q_ref, kp_ref, kc_ref, vp_ref, vc_ref, do_ref, l_ref, d_ref = refs[:8]
        dq_ref = refs[-1]
        jb = pl.program_id(1)
        row = lax.broadcasted_iota(jnp.int32, (blk, 2 * blk), 0)
        col = lax.broadcasted_iota(jnp.int32, (blk, 2 * blk), 1)
        delta = row + blk - col
        valid = (delta >= 0) & (delta <= n_steps) & ((col >= blk) | (jb > 0))
        dist = (delta * d).astype(F32)
        ltile, dtile = l_ref[...], d_ref[...]
        for h in range(H):
            hs = slice(h * HEAD_DIM, (h + 1) * HEAD_DIM)
            k2 = jnp.concatenate([kp_ref[:, hs], kc_ref[:, hs]], axis=0)
            v2 = jnp.concatenate([vp_ref[:, hs], vc_ref[:, hs]], axis=0)
            s = lax.dot_general(q_ref[:, hs], k2, (((1,), (1,)), ((), ())), preferred_element_type=F32) * scale
            s = jnp.where(valid, s - float(slopes[h]) * dist - _lane_col(ltile, h), NEG)
            p = jnp.exp(s)
            dp = lax.dot_general(do_ref[:, hs], v2, (((1,), (1,)), ((), ())), preferred_element_type=F32)
            ds = (p * (dp - _lane_col(dtile, h))).astype(BF16)
            dq_ref[:, hs] = (jnp.dot(ds, k2, preferred_element_type=F32) * scale).astype(dq_ref.dtype)

    nq, nk = len(BRANCHES), 2 * len(BRANCHES)

    def cur(off, n):
        return pl.BlockSpec((blk, DA), lambda r, jb: (jb, r * n + off))

    def prv(off, n):
        return pl.BlockSpec((blk, DA), lambda r, jb: (jnp.maximum(jb - 1, 0), r * n + off))

    lspec = pl.BlockSpec((blk, LANES), lambda r, jb: (jb, r))
    in_specs = [cur(gi, nq), prv(2 * gi, nk), cur(2 * gi, nk), prv(2 * gi + 1, nk), cur(2 * gi + 1, nk),
                pl.BlockSpec((blk, DA), lambda r, jb: (jb, r)), lspec, lspec]
    args = [qv, kvv, kvv, kvv, kvv, dov, lv, dv_]
    aliases = {}
    if dq_prev is not None:
        in_specs.append(pl.BlockSpec(memory_space=pl.ANY))
        args.append(dq_prev.reshape(S // d, d * q.shape[1]))
        aliases = {8: 0}
    dq = pl.pallas_call(
        body, name=name, grid=(d, nbs), in_specs=in_specs, out_specs=cur(gi, nq),
        out_shape=jax.ShapeDtypeStruct(qv.shape, BF16), input_output_aliases=aliases,
        compiler_params=_cparams(("parallel", "parallel")))(*args)
    return dq.reshape(q.shape)


def attn_branch_dkv(q, kv, do, lse, dlt, dkv_prev, accumulate, gi, slopes, name):
    S = q.shape[0]
    DA = q.shape[1] // len(BRANCHES)
    H = DA // HEAD_DIM
    window, d = BRANCHES[gi]
    n_steps = window // d
    blk = ATTN_BLOCK
    nbs = S // d // blk
    scale = HEAD_DIM ** -0.5
    qv = q.reshape(S // d, d * q.shape[1])
    kvv = kv.reshape(S // d, d * kv.shape[1])
    dov = do.reshape(S // d, d * DA)
    lv = lse.reshape(S // d, d * LANES)
    dv_ = dlt.reshape(S // d, d * LANES)

    def body(*refs):
        k_ref, v_ref, qc_ref, qn_ref, doc_ref, don_ref, lc_ref, ln_ref, dc_ref, dn_ref = refs[:10]
        out_ref = refs[-1]
        kb = pl.program_id(1)
        row = lax.broadcasted_iota(jnp.int32, (2 * blk, blk), 0)
        col = lax.broadcasted_iota(jnp.int32, (2 * blk, blk), 1)
        delta = row - col
        valid = (delta >= 0) & (delta <= n_steps) & ((row < blk) | (kb < nbs - 1))
        dist = (delta * d).astype(F32)
        l2 = jnp.concatenate([lc_ref[...], ln_ref[...]], axis=0)
        d2 = jnp.concatenate([dc_ref[...], dn_ref[...]], axis=0)
        for h in range(H):
            hs = slice(h * HEAD_DIM, (h + 1) * HEAD_DIM)
            vs = slice(DA + h * HEAD_DIM, DA + (h + 1) * HEAD_DIM)
            q2 = jnp.concatenate([qc_ref[:, hs], qn_ref[:, hs]], axis=0)
            do2 = jnp.concatenate([doc_ref[:, hs], don_ref[:, hs]], axis=0)
            s = lax.dot_general(q2, k_ref[:, hs], (((1,), (1,)), ((), ())), preferred_element_type=F32) * scale
            s = jnp.where(valid, s - float(slopes[h]) * dist - _lane_col(l2, h), NEG)
            p = jnp.exp(s)
            dvh = lax.dot_general(p.astype(BF16), do2, (((0,), (0,)), ((), ())), preferred_element_type=F32)
            dp = lax.dot_general(do2, v_ref[:, hs], (((1,), (1,)), ((), ())), preferred_element_type=F32)
            ds = (p * (dp - _lane_col(d2, h))).astype(BF16)
            dkh = lax.dot_general(ds, q2, (((0,), (0,)), ((), ())), preferred_element_type=F32) * scale
            if accumulate:
                dkh = dkh + refs[10][:, hs]
                dvh = dvh + refs[10][:, vs]
            out_ref[:, hs] = dkh
            out_ref[:, vs] = dvh

    nq, nk = len(BRANCHES), 2 * len(BRANCHES)

    def cur(width, off, n):
        return pl.BlockSpec((blk, width), lambda r, kb: (kb, r * n + off))

    def nxt(width, off, n):
        return pl.BlockSpec((blk, width), lambda r, kb: (jnp.minimum(kb + 1, nbs - 1), r * n + off))

    pair = pl.BlockSpec((blk, 2 * DA), lambda r, kb: (kb, r * nq + gi))
    in_specs = [cur(DA, 2 * gi, nk), cur(DA, 2 * gi + 1, nk), cur(DA, gi, nq), nxt(DA, gi, nq), cur(DA, 0, 1), nxt(DA, 0, 1),
                cur(LANES, 0, 1), nxt(LANES, 0, 1), cur(LANES, 0, 1), nxt(LANES, 0, 1)]
    args = [kvv, kvv, qv, qv, dov, dov, lv, lv, dv_, dv_]
    aliases = {}
    if dkv_prev is not None:
        prev_v = dkv_prev.reshape(S // d, d * kv.shape[1])
        if accumulate:
            in_specs.append(pair)
        else:
            in_specs.append(pl.BlockSpec(memory_space=pl.ANY))
        args.append(prev_v)
        aliases = {10: 0}
    dkv = pl.pallas_call(
        body, name=name, grid=(d, nbs), in_specs=in_specs, out_specs=pair,
        out_shape=jax.ShapeDtypeStruct(kvv.shape, F32), input_output_aliases=aliases,
        compiler_params=_cparams(("parallel", "parallel")))(*args)
    return dkv.reshape(kv.shape)


def ada_project(c16, w3, b3, name):
    L, D, Ns = w3.shape
    tn = _tile(Ns, 512, LANES)

    def body(c_ref, w_ref, b_ref, o_ref):
        cv = c_ref[...]
        cond = (cv * _sigmoid(cv)).astype(BF16)
        o_ref[...] = jnp.dot(cond, w_ref[...].astype(BF16), preferred_element_type=F32) + b_ref[...]

    return pl.pallas_call(
        body, name=name, grid=(L, Ns // tn),
        in_specs=[pl.BlockSpec((16, D), lambda l, j: (0, 0)), pl.BlockSpec((None, D, tn), lambda l, j: (l, 0, j)),
                  pl.BlockSpec((None, 1, tn), lambda l, j: (l, 0, j))],
        out_specs=pl.BlockSpec((None, 16, tn), lambda l, j: (l, 0, j)), out_shape=jax.ShapeDtypeStruct((L, 16, Ns), F32),
        compiler_params=_cparams(("parallel", "parallel")))(c16, w3, b3)


def _adamw(w, g, m, v):
    m = B1 * m + (1.0 - B1) * g
    v = B2 * v + (1.0 - B2) * (g * g)
    m_hat = m / (1.0 - B1 ** STEP)
    v_hat = v / (1.0 - B2 ** STEP)
    delta = -LR * (m_hat / (jnp.sqrt(v_hat) + ADAM_EPS) + WD * w)
    return delta, m, v


def ada_grad_adamw(c16, d3, w3, m3, v3, name):
    L, D, Ns = w3.shape
    tk = _tile(D, 256, 8)

    def body(c_ref, d_ref, w_ref, m_ref, v_ref, g_out, dl_out, m_out, v_out):
        cv = c_ref[...]
        cond = (cv * _sigmoid(cv)).astype(BF16)
        g = lax.dot_general(cond, d_ref[...].astype(BF16), (((0,), (0,)), ((), ())), preferred_element_type=F32)
        g_out[...] = g
        dl_out[...], m_out[...], v_out[...] = _adamw(w_ref[...], g, m_ref[...], v_ref[...])

    wspec = pl.BlockSpec((None, tk, Ns), lambda l, kj: (l, kj, 0))
    return pl.pallas_call(
        body, name=name, grid=(L, D // tk),
        in_specs=[pl.BlockSpec((16, tk), lambda l, kj: (0, kj)), pl.BlockSpec((None, 16, Ns), lambda l, kj: (l, 0, 0)), wspec, wspec, wspec],
        out_specs=[wspec] * 4, out_shape=[jax.ShapeDtypeStruct((L, D, Ns), F32)] * 4,
        compiler_params=_cparams(("parallel", "parallel")))(c16, d3, w3, m3, v3)


def adamw(w, g, m, v, name):
    R, C = w.shape
    tr = _tile(R, 256, 8)

    def body(w_ref, g_ref, m_ref, v_ref, g_out, dl_out, m_out, v_out):
        g = g_ref[...]
        g_out[...] = g
        dl_out[...], m_out[...], v_out[...] = _adamw(w_ref[...], g, m_ref[...], v_ref[...])

    spec = pl.BlockSpec((tr, C), lambda i: (i, 0))
    return pl.pallas_call(body, name=name, grid=(R // tr,), in_specs=[spec] * 4, out_specs=[spec] * 4,
                          out_shape=[jax.ShapeDtypeStruct((R, C), F32)] * 4, compiler_params=_cparams(("parallel",)))(w, g, m, v)


def sum_partials(own, recv, chip_major, pos, name):
    T, _, R, C = recv.shape
    L = 2 * T
    tr = _tile(R, 256, 16)

    def body(pos_ref, own_ref, recv_ref, o_ref):
        acc = own_ref[...].astype(F32)
        for rel in range(7):
            acc = acc + recv_ref[rel].astype(F32)
        o_ref[...] = acc

    if chip_major:
        own_spec = pl.BlockSpec((None, None, tr, C), lambda t, r, pos: (pos[1], 2 * t + pos[0], r, 0))
    else:
        own_spec = pl.BlockSpec((None, None, tr, C), lambda t, r, pos: (2 * t + pos[0], pos[1], r, 0))
    return pl.pallas_call(
        body, name=name,
        grid_spec=pltpu.PrefetchScalarGridSpec(
            num_scalar_prefetch=1, grid=(T, R // tr),
            in_specs=[own_spec, pl.BlockSpec((None, 7, tr, C), lambda t, r, pos: (t, 0, r, 0))],
            out_specs=pl.BlockSpec((None, tr, C), lambda t, r, pos: (2 * t + pos[0], r, 0))),
        out_shape=jax.ShapeDtypeStruct((L, R, C), F32),
        compiler_params=_cparams(("parallel", "parallel")))(pos, own, recv)


def sum_rows8(g8, name):
    _, R, C = g8.shape

    def body(g_ref, o_ref):
        acc = g_ref[0]
        for i in range(1, N_DEV):
            acc = acc + g_ref[i]
        o_ref[...] = acc

    return pl.pallas_call(body, name=name, grid=(1,), in_specs=[pl.BlockSpec((N_DEV, R, C), lambda i: (0, 0, 0))],
                          out_specs=pl.BlockSpec((R, C), lambda i: (0, 0)), out_shape=jax.ShapeDtypeStruct((R, C), F32),
                          compiler_params=_cparams(("arbitrary",)))(g8)


def _pack(vecs):
    flat = [v.reshape(-1).astype(F32) for v in vecs]
    sizes = [f.shape[0] for f in flat]
    total = sum(sizes)
    padded = -(-total // (8 * PACK_W)) * (8 * PACK_W)
    buf = jnp.concatenate(flat + [jnp.zeros((padded - total,), F32)])
    offs = np.concatenate([[0], np.cumsum(sizes)])
    return buf.reshape(-1, PACK_W), offs


def _unpack(buf, offs, shapes):
    flat = buf.reshape(-1)
    return [flat[int(offs[i]):int(offs[i + 1])].reshape(s) for i, s in enumerate(shapes)]


def kernel(x, c, ada_w, ada_b, norm1_g, norm2_g, pool_w_in, pool_w_grp, pool_scale, pool_w_out, kv_norm_g, kv_ada_w, kv_ada_b, w_kv, attn_w_q, attn_w_o, ffn_w_up, ffn_conv_w, ffn_conv_b, ffn_w_down, final_g, loss_target, m_ada_w, m_ada_b, m_norm1_g, m_norm2_g, m_pool_w_in, m_pool_w_grp, m_pool_scale, m_pool_w_out, m_kv_norm_g, m_kv_ada_w, m_kv_ada_b, m_w_kv, m_attn_w_q, m_attn_w_o, m_ffn_w_up, m_ffn_conv_w, m_ffn_conv_b, m_ffn_w_down, m_final_g, v_ada_w, v_ada_b, v_norm1_g, v_norm2_g, v_pool_w_in, v_pool_w_grp, v_pool_scale, v_pool_w_out, v_kv_norm_g, v_kv_ada_w, v_kv_ada_b, v_w_kv, v_attn_w_q, v_attn_w_o, v_ffn_w_up, v_ffn_conv_w, v_ffn_conv_b, v_ffn_w_down, v_final_g):
    S, D = x.shape[1], x.shape[2]
    depth = ada_w.shape[0]
    n_pool = pool_w_in.shape[0]
    n_attn = attn_w_q.shape[0]
    G = len(POOL_WINDOWS)
    NB = len(BRANCHES)
    DA = attn_w_o.shape[1] * N_CHIPS
    H = DA // HEAD_DIM
    F = ffn_conv_b.shape[1]
    Fs = F // N_CHIPS
    Dq = D // N_CHIPS
    ada_ns = ada_w.shape[2]
    kvada_ns = kv_ada_w.shape[1]
    slopes = _alibi_slopes(NB * H).reshape(NB, H)

    ix, iy, ic = lax.axis_index("x"), lax.axis_index("y"), lax.axis_index("c")
    p_me = 2 * ix + iy
    b_me = 4 * ix + 2 * iy + ic
    pos = jnp.stack([ic, p_me]).astype(jnp.int32)
    xs, tgt = x[0], loss_target[0]

    pk, offs = _pack([c, pool_scale, ffn_conv_w])
    rows1 = pk.shape[0]
    got = all_gather8(pk, "gather_small_in").reshape(N_DEV, rows1, PACK_W)
    c8 = got.reshape(N_DEV, -1)[:, :D]
    c16 = jnp.concatenate([c8, jnp.zeros_like(c8)], axis=0)
    chip_rows = got[0::2].reshape(N_CHIPS, -1)
    scale_full = chip_rows[:, int(offs[1]):int(offs[2])].reshape(N_CHIPS, n_pool, Dq).transpose(1, 0, 2).reshape(n_pool, D)
    convw_full = chip_rows[:, int(offs[2]):int(offs[3])].reshape(N_CHIPS, depth, 3, Fs).transpose(1, 2, 0, 3).reshape(depth, 3, F)

    ada_b_loc = lax.dynamic_slice(ada_b, (0, p_me * ada_ns), (depth, ada_ns)).reshape(depth, 1, ada_ns)
    kvb_loc = lax.dynamic_slice(kv_ada_b, (p_me * kvada_ns,), (kvada_ns,)).reshape(1, 1, kvada_ns)
    mods_loc = ada_project(c16, ada_w, ada_b_loc, "ada_project")[:, :N_DEV]
    kvmod_loc = ada_project(c16, kv_ada_w.reshape(1, D, kvada_ns), kvb_loc, "kv_ada_project")[0, :N_DEV]
    mods_cat = jnp.concatenate([mods_loc.transpose(1, 0, 2).reshape(N_DEV, depth * ada_ns), kvmod_loc], axis=1)
    mods_all = all_gather8(mods_cat, "gather_mods").reshape(N_CHIPS, 2, N_DEV, -1)
    mine = lax.dynamic_index_in_dim(mods_all[:, 0], b_me, axis=1, keepdims=False)
    mod = mine[:, :depth * ada_ns].reshape(N_CHIPS, depth, ada_ns).transpose(1, 0, 2).reshape(depth, 6, 1, D)
    kvmod = mine[:, depth * ada_ns:].reshape(2, 1, D)

    big = [pool_w_in, pool_w_grp, pool_w_out, w_kv.reshape(2, D // 2, -1), attn_w_q, attn_w_o, ffn_w_up, ffn_w_down]
    chip_major = [False, False, False, True, False, False, False, False]
    g_in, g_grp, g_out, g_kv, g_q, g_o, g_up, g_down = gather_weights([w.astype(BF16) for w in big], chip_major, "gather_weights")
    w_in4 = g_in.reshape(n_pool, 1, D, D)
    w_out4 = g_out.reshape(n_pool, 1, D, D)
    w_kv4 = g_kv.reshape(1, N_CHIPS, D, -1)
    w_q4 = g_q
    w_o4 = g_o.reshape(n_attn, 1, DA, D)
    w_up4 = g_up
    w_down4 = g_down.reshape(depth, 1, F, D)

    kv_tn = DA // 2
    kv_nper = w_kv4.shape[3] // kv_tn

    def kv_colmap(cb):
        slot, half = cb // 2, cb % 2
        two, gi = slot // NB, slot % NB
        return (gi * 2 + two) * 2 + half

    q_tn = _tile(w_q4.shape[3], 768, 256)
    up_tn = w_up4.shape[3]
    up_per_half = F // up_tn

    def up_gmap(s):
        return s // up_per_half, s % up_per_half

    def vec(v):
        return v.reshape(1, -1)

    saved = []
    xcur = xs
    kv = None
    for l in range(depth):
        sh1, sc1, g1, sh2, sc2, g2 = [mod[l, i] for i in range(6)]
        st = {"x0": xcur}
        h1 = norm_mod(xcur, vec(norm1_g[l]), sh1, sc1, "norm_mod")
        st["h1"] = h1
        if l < n_pool:
            u = mm_nn(h1, w_in4, l, tn=D // 2, out_dtype=F32, name="pool_in_proj")
            pooled, z, ys = pool_fwd(u, g_grp, l, vec(scale_full[l]), "pool_mix")
            out, x1 = mm_nn(ys, w_out4, l, tn=D // 2, out_dtype=BF16, name="pool_out_proj", res=(xcur, g1))
            st.update(pooled=pooled, z=z, ys=ys, out1=out)
        else:
            j = l - n_pool
            if j == 0:
                hkv = norm_mod(xcur, vec(kv_norm_g), kvmod[0], kvmod[1], "norm_mod")
                kv = mm_nn(hkv, w_kv4, 0, tn=kv_tn, out_dtype=BF16, name="kv_proj", colmap=kv_colmap)
                kv_state = {"x": xcur, "hkv": hkv}
            q = mm_nn(h1, w_q4, j, tn=q_tn, out_dtype=BF16, name="q_proj")
            os_, lses = [], []
            for gi in range(NB):
                o_b, l_b = attn_branch_fwd(q, kv, gi, slopes[gi], f"attn_fwd_b{gi}")
                os_.append(o_b)
                lses.append(l_b)
            o, lse = attn_combine(os_, lses, "attn_combine")
            out, x1 = mm_nn(o, w_o4, j, tn=D // 2, out_dtype=BF16, name="attn_out_proj", res=(xcur, g1))
            st.update(q=q, o=o, lse=lse, out1=out)
        st["x1"] = x1
        h2 = norm_mod(x1, vec(norm2_g[l]), sh2, sc2, "norm_mod")
        hu = mm_nn(h2, w_up4, l, tn=up_tn, out_dtype=BF16, name="ffn_up_proj")
        gated = ffn_act(hu, convw_full[l], vec(ffn_conv_b[l]), "ffn_act")
        out2, x2 = mm_nn(gated, w_down4, l, tn=D // 2, out_dtype=BF16, name="ffn_down_proj", res=(x1, g2))
        st.update(h2=h2, hu=hu, gated=gated, out2=out2)
        saved.append(st)
        xcur = x2

    dx, fsums = loss_fwd_bwd(xcur, vec(final_g), tgt, "loss_head")
    loss = lax.psum(0.5 * jnp.sum(fsums[1]) / D, ("x", "y", "c"))
    d_final_g = fsums[0]

    dmods = [None] * depth
    d_n1 = [None] * depth
    d_n2 = [None] * depth
    d_convw = [None] * depth
    d_convb = [None] * depth
    d_scale = [None] * n_pool
    d_grp = [None] * n_pool
    dw_in = dw_out = dw_q = dw_o = dw_up = dw_down = None
    dkv = None
    for l in reversed(range(depth)):
        st = saved[l]
        sh1, sc1, g1, sh2, sc2, g2 = [mod[l, i] for i in range(6)]
        dout2, s_g2 = gate_bwd(dx, st["out2"], g2, "gate_bwd")
        dgated = mm_nt(dout2.reshape(1, S, D), w_down4, l, tn=D, tk=_tile(F, 1408, LANES), out_dtype=BF16, name="ffn_down_bwd")
        dw_down = mm_tn(st["gated"], dout2.reshape(1, S, D), dw_down, l, w_down4.shape, tn=D, tk=_tile(F, 1408, LANES), name="ffn_down_dw")
        dhu, s_conv = ffn_act_bwd(dgated, st["hu"], convw_full[l], vec(ffn_conv_b[l]), "ffn_act_bwd")
        dh2 = mm_nt(dhu, w_up4, l, tn=up_tn, tk=D, out_dtype=F32, name="ffn_up_bwd", gmap=up_gmap)
        dw_up = mm_tn(st["h2"], dhu, dw_up, l, w_up4.shape, tn=up_tn, tk=D, name="ffn_up_dw", gmap=up_gmap)
        dx, s_n2 = norm_mod_bwd(dh2, st["x1"], dx, vec(norm2_g[l]), sc2, "norm_mod_bwd")
        d_convw[l], d_convb[l] = s_conv[0:3], s_conv[3]
        d_n2[l] = s_n2[2]
        dout1, s_g1 = gate_bwd(dx, st["out1"], g1, "gate_bwd")
        if l < n_pool:
            dys = mm_nt(dout1.reshape(1, S, D), w_out4, l, tn=D, tk=D // 2, out_dtype=F32, name="pool_out_bwd")
            dw_out = mm_tn(st["ys"], dout1.reshape(1, S, D), dw_out, l, w_out4.shape, tn=D, tk=D, name="pool_out_dw")
            du, d_grp[l], s_sc = pool_bwd(dys, st["z"], st["pooled"], g_grp, l, vec(scale_full[l]), "pool_mix_bwd")
            d_scale[l] = s_sc[0]
            dh1 = mm_nt(du.reshape(1, S, D), w_in4, l, tn=D, tk=D // 2, out_dtype=F32, name="pool_in_bwd")
            dw_in = mm_tn(st["h1"], du.reshape(1, S, D), dw_in, l, w_in4.shape, tn=D, tk=D, name="pool_in_dw")
        else:
            j = l - n_pool
            do = mm_nt(dout1.reshape(1, S, D), w_o4, j, tn=D, tk=DA // 2, out_dtype=BF16, name="attn_out_bwd")
            dw_o = mm_tn(st["o"], dout1.reshape(1, S, D), dw_o, j, w_o4.shape, tn=D, tk=DA, name="attn_out_dw")
            dlt = attn_delta(do, st["o"], "attn_delta")
            dq = None
            for gi in range(NB):
                dq = attn_branch_dq(st["q"], kv, do, st["lse"], dlt, dq, gi, slopes[gi], f"attn_dq_b{gi}")
                dkv = attn_branch_dkv(st["q"], kv, do, st["lse"], dlt, dkv, j < n_attn - 1, gi, slopes[gi],
                                      f"attn_dkv_b{gi}" + ("_acc" if j < n_attn - 1 else ""))
            dh1 = mm_nt(dq.reshape(1, S, -1), w_q4, j, tn=q_tn, tk=D, out_dtype=F32, name="q_proj_bwd")
            dw_q = mm_tn(st["h1"], dq.reshape(1, S, -1), dw_q, j, w_q4.shape, tn=q_tn, tk=D, name="q_proj_dw")
        dx, s_n1 = norm_mod_bwd(dh1, st["x0"], dx, vec(norm1_g[l]), sc1, "norm_mod_bwd")
        d_n1[l] = s_n1[2]
        dmods[l] = jnp.stack([s_n1[0], s_n1[1], s_g1[0], s_n2[0], s_n2[1], s_g2[0]])
        if l == n_pool:
            def kv_gmap(s):
                return 0, kv_colmap(s)
            dhkv = mm_nt(dkv.reshape(1, S, -1), w_kv4, 0, tn=kv_tn, tk=D, out_dtype=F32, name="kv_proj_bwd", gmap=kv_gmap)
            dw_kv = mm_tn(kv_state["hkv"], dkv.reshape(1, S, -1), None, 0, w_kv4.shape, tn=kv_tn, tk=D, name="kv_proj_dw", gmap=kv_gmap)
            dx, s_kv = norm_mod_bwd(dhkv, kv_state["x"], dx, vec(kv_norm_g), kvmod[1], "norm_mod_bwd")
    grad_x = dx.reshape(1, S, D)

    smalls = [jnp.stack(dmods), jnp.stack([s_kv[0], s_kv[1]]), jnp.stack(d_n1), jnp.stack(d_n2), s_kv[2], jnp.stack(d_convb), d_final_g,
              jnp.stack(d_scale), jnp.stack(d_convw)]
    small_shapes = [s.shape for s in smalls]
    spk, soffs = _pack(smalls)
    srows = spk.shape[0]
    sgot = all_gather8(spk, "gather_small_grads").reshape(N_DEV, srows, PACK_W)
    ssum = sum_rows8(sgot, "sum_small_grads")
    g_mods, g_kvmod, g_n1, g_n2, g_kvn, g_convb, g_fg, g_scale_full, g_convw_full = _unpack(ssum, soffs, small_shapes)
    g_ada_b = g_mods.reshape(depth, 6 * D)
    g_kv_ada_b = g_kvmod.reshape(2 * D)
    g_scale = lax.dynamic_slice(g_scale_full, (0, p_me * Dq), (n_pool, Dq))
    g_convw = lax.dynamic_slice(g_convw_full, (0, 0, p_me * Fs), (depth, 3, Fs))

    small_w = [ada_b, norm1_g, norm2_g, kv_norm_g, kv_ada_b, ffn_conv_b, final_g, pool_scale, ffn_conv_w]
    small_m = [m_ada_b, m_norm1_g, m_norm2_g, m_kv_norm_g, m_kv_ada_b, m_ffn_conv_b, m_final_g, m_pool_scale, m_ffn_conv_w]
    small_v = [v_ada_b, v_norm1_g, v_norm2_g, v_kv_norm_g, v_kv_ada_b, v_ffn_conv_b, v_final_g, v_pool_scale, v_ffn_conv_w]
    small_g = [g_ada_b, g_n1, g_n2, g_kvn, g_kv_ada_b, g_convb, g_fg, g_scale, g_convw]
    sw_shapes = [w.shape for w in small_w]
    pw, woffs = _pack(small_w)
    s_res = adamw(pw, _pack(small_g)[0], _pack(small_m)[0], _pack(small_v)[0], "adamw_small")
    s_g, s_dl, s_m, s_v = [_unpack(r, woffs, sw_shapes) for r in s_res]

    per_dev = sgot.reshape(N_DEV, -1)
    dm_all = per_dev[:, int(soffs[0]):int(soffs[1])].reshape(N_DEV, depth, 6 * D)
    dkvm_all = per_dev[:, int(soffs[1]):int(soffs[2])].reshape(N_DEV, 1, 2 * D)

    def shard_cols(a, ns):
        sl = lax.dynamic_slice_in_dim(a, p_me * ns, ns, axis=2).transpose(1, 0, 2)
        return jnp.concatenate([sl, jnp.zeros_like(sl)], axis=1)

    ada_res = ada_grad_adamw(c16, shard_cols(dm_all, ada_ns), ada_w, m_ada_w, v_ada_w, "ada_grad_adamw")
    kvada_res = ada_grad_adamw(c16, shard_cols(dkvm_all, kvada_ns), kv_ada_w.reshape(1, D, kvada_ns), m_kv_ada_w.reshape(1, D, kvada_ns),
                               v_kv_ada_w.reshape(1, D, kvada_ns), "kv_ada_grad_adamw")
    kvada_res = [r.reshape(D, kvada_ns) for r in kvada_res]

    d_grp5 = jnp.stack(d_grp).astype(BF16).reshape(n_pool, G, N_CHIPS, -1, D // G).transpose(0, 2, 1, 3, 4)
    parts = [dw_in.reshape(n_pool, N_CHIPS, Dq, D), d_grp5.reshape(n_pool, N_CHIPS, -1, D // G), dw_out.reshape(n_pool, N_CHIPS, Dq, D),
             dw_kv.reshape(N_CHIPS, 2, D // 2, -1), dw_q, dw_o.reshape(n_attn, N_CHIPS, -1, D), dw_up,
             dw_down.reshape(depth, N_CHIPS, Fs, D)]
    recvs = exchange_grads(parts, chip_major, "exchange_grads")
    halves = [sum_partials(p, r, cm, pos, "sum_partials") for p, r, cm in zip(parts, recvs, chip_major)]
    fulls = swap_halves(halves, "swap_halves")
    big_m = [m_pool_w_in, m_pool_w_grp, m_pool_w_out, m_w_kv, m_attn_w_q, m_attn_w_o, m_ffn_w_up, m_ffn_w_down]
    big_v = [v_pool_w_in, v_pool_w_grp, v_pool_w_out, v_w_kv, v_attn_w_q, v_attn_w_o, v_ffn_w_up, v_ffn_w_down]
    big_w = [pool_w_in, pool_w_grp, pool_w_out, w_kv, attn_w_q, attn_w_o, ffn_w_up, ffn_w_down]
    big_res = []
    for w, gfull, m_, v_ in zip(big_w, fulls, big_m, big_v):
        cols = gfull.shape[-1]
        res = adamw(w.reshape(-1, cols), gfull.reshape(-1, cols), m_.reshape(-1, cols), v_.reshape(-1, cols), "adamw_big")
        big_res.append([r.reshape(w.shape) for r in res])

    order = ["ada_w", "ada_b", "norm1_g", "norm2_g", "pool_w_in", "pool_w_grp", "pool_scale", "pool_w_out", "kv_norm_g", "kv_ada_w",
             "kv_ada_b", "w_kv", "attn_w_q", "attn_w_o", "ffn_w_up", "ffn_conv_w", "ffn_conv_b", "ffn_w_down", "final_g"]
    small_names = ["ada_b", "norm1_g", "norm2_g", "kv_norm_g", "kv_ada_b", "ffn_conv_b", "final_g", "pool_scale", "ffn_conv_w"]
    big_names = ["pool_w_in", "pool_w_grp", "pool_w_out", "w_kv", "attn_w_q", "attn_w_o", "ffn_w_up", "ffn_w_down"]
    results = {"ada_w": ada_res, "kv_ada_w": kvada_res}
    for i, nm in enumerate(small_names):
        results[nm] = [s_g[i], s_dl[i], s_m[i], s_v[i]]
    for i, nm in enumerate(big_names):
        results[nm] = big_res[i]
    outs = [loss, grad_x]
    for kind in range(4):
        outs += [results[nm][kind] for nm in order]
    return tuple(outs)
```

```python
import functools
import math

import numpy as np
import jax
import jax.numpy as jnp
from jax import lax
from jax.experimental import pallas as pl
from jax.experimental.pallas import tpu as pltpu

F32 = jnp.float32
BF16 = jnp.bfloat16
MESH = pl.DeviceIdType.MESH

POOL_WINDOWS = (2, 4, 8, 16)
BRANCHES = ((128, 1), (512, 4), (2048, 16))
HEAD_DIM = 64
ATTN_BLOCK = 128
EPS = 1e-6
LR, B1, B2, ADAM_EPS, WD, STEP = 0.001, 0.9, 0.999, 1e-08, 0.01, 10

VMEM_LIMIT_BYTES = 56 * 1024 * 1024
LANES = 128
PACK_W = 1024
HALO = 16
NEG = -1e30
N_CHIPS = 4
N_DEV = 8


def _alibi_slopes(n):
    def pow2(m):
        start = 2.0 ** (-(2.0 ** -(math.log2(m) - 3)))
        return [start ** (i + 1) for i in range(m)]
    if math.log2(n).is_integer():
        s = pow2(n)
    else:
        c = 2 ** math.floor(math.log2(n))
        s = pow2(c) + pow2(2 * c)[0::2][: n - c]
    s = np.asarray(s, dtype=np.float32)
    return -np.sort(-s)


def _cparams(sem=None):
    return pltpu.CompilerParams(dimension_semantics=sem, vmem_limit_bytes=VMEM_LIMIT_BYTES)


def _tile(n, pref, unit):
    t = (min(pref, n) // unit) * unit
    while t >= unit:
        if n % t == 0:
            return t
        t -= unit
    return n


def _sigmoid(v):
    return 1.0 / (1.0 + jnp.exp(-v))


def all_gather8(xs, name):
    m_per, n = xs.shape

    def body(x_ref, out_ref, send_sems, recv_sems, local_sem):
        x, y, c = lax.axis_index("x"), lax.axis_index("y"), lax.axis_index("c")
        me, sibling = (x, y, c), (x, y, 1 - c)
        chips = [(1 - x, y), (x, 1 - y), (1 - x, 1 - y)]

        def rows(px, py, pc):
            return out_ref.at[pl.ds((4 * px + 2 * py + pc) * m_per, m_per), :]

        def copy(k, block, to, src=None):
            return pltpu.make_async_remote_copy(src_ref=rows(*block) if src is None else src, dst_ref=rows(*block),
                                                send_sem=send_sems.at[k], recv_sem=recv_sems.at[k], device_id=to, device_id_type=MESH)

        mine = pltpu.make_async_copy(x_ref, rows(*me), local_sem)
        mine.start()
        first = [copy(0, me, sibling, src=x_ref)]
        first += [copy(1 + j, me, (*chip, c), src=x_ref) for j, chip in enumerate(chips)]
        for cp in first:
            cp.start()
        passed = [copy(4 + j, (*chip, c), sibling) for j, chip in enumerate(chips)]
        for j, chip in enumerate(chips):
            copy(1 + j, (*chip, c), me).wait_recv()
            passed[j].start()
        copy(0, sibling, me).wait_recv()
        for j, chip in enumerate(chips):
            copy(4 + j, (*chip, 1 - c), me).wait_recv()
        for cp in first + passed:
            cp.wait_send()
        mine.wait()

    return pl.pallas_call(
        body, name=name,
        out_shape=jax.ShapeDtypeStruct((N_DEV * m_per, n), xs.dtype),
        in_specs=[pl.BlockSpec(memory_space=pltpu.VMEM)],
        out_specs=pl.BlockSpec(memory_space=pltpu.VMEM),
        scratch_shapes=[pltpu.SemaphoreType.DMA((7,)), pltpu.SemaphoreType.DMA((7,)), pltpu.SemaphoreType.DMA],
        compiler_params=pltpu.CompilerParams(vmem_limit_bytes=VMEM_LIMIT_BYTES),
    )(xs)


HBM_SPEC = pl.BlockSpec(memory_space=pltpu.HBM)


def _slot(ref, chip_major, piece, chip):
    return ref.at[chip, piece] if chip_major else ref.at[piece, chip]


def gather_weights(arrs, chip_major, name):
    n = len(arrs)
    n_mine = sum(a.shape[0] // 2 for a in arrs)
    n_pieces = sum(a.shape[0] for a in arrs)

    def body(*refs):
        ins, outs = refs[:n], refs[n:2 * n]
        send_sems, recv_sems, loc_sems = refs[2 * n:]
        x, y, c = lax.axis_index("x"), lax.axis_index("y"), lax.axis_index("c")
        p_me = 2 * x + y
        chips = [(1 - x, y), (x, 1 - y), (1 - x, 1 - y)]
        sib = (x, y, 1 - c)

        def rcopy(src, dst, k, to):
            return pltpu.make_async_remote_copy(src_ref=src, dst_ref=dst, send_sem=send_sems.at[k], recv_sem=recv_sems.at[k],
                                                device_id=to, device_id_type=MESH)

        locs, li = [], 0
        for a in range(n):
            for i in range(ins[a].shape[0]):
                cp = pltpu.make_async_copy(ins[a].at[i], _slot(outs[a], chip_major[a], i, p_me), loc_sems.at[li])
                cp.start()
                locs.append(cp)
                li += 1
        sends, plan, k = [], [], 0
        for a in range(n):
            for t in range(ins[a].shape[0] // 2):
                ii = 2 * t + c
                for j, chip in enumerate(chips):
                    cp = rcopy(ins[a].at[ii], _slot(outs[a], chip_major[a], ii, p_me), k + j, (*chip, c))
                    cp.start()
                    sends.append(cp)
                plan.append((a, t, k))
                k += 6
        for a, t, k in plan:
            ii = 2 * t + c
            for j, chip in enumerate(chips):
                blk = _slot(outs[a], chip_major[a], ii, 2 * chip[0] + chip[1])
                rcopy(blk, blk, k + j, (*chip, c)).wait_recv()
                fw = rcopy(blk, blk, k + 3 + j, sib)
                fw.start()
                sends.append(fw)
        for a, t, k in plan:
            io = 2 * t + 1 - c
            for j, chip in enumerate(chips):
                blk = _slot(outs[a], chip_major[a], io, 2 * chip[0] + chip[1])
                rcopy(blk, blk, k + 3 + j, sib).wait_recv()
        for cp in sends:
            cp.wait_send()
        for cp in locs:
            cp.wait()

    def oshape(a, cm):
        lead = (N_CHIPS, a.shape[0]) if cm else (a.shape[0], N_CHIPS)
        return jax.ShapeDtypeStruct(lead + a.shape[1:], a.dtype)

    return pl.pallas_call(
        body, name=name,
        out_shape=[oshape(a, cm) for a, cm in zip(arrs, chip_major)],
        in_specs=[HBM_SPEC] * n, out_specs=[HBM_SPEC] * n,
        scratch_shapes=[pltpu.SemaphoreType.DMA((6 * n_mine,)), pltpu.SemaphoreType.DMA((6 * n_mine,)), pltpu.SemaphoreType.DMA((n_pieces,))],
    )(*arrs)


def exchange_grads(arrs, chip_major, name):
    n = len(arrs)
    n_l = [a.shape[1] if cm else a.shape[0] for a, cm in zip(arrs, chip_major)]
    n_mine = sum(l // 2 for l in n_l)

    def body(*refs):
        ins, outs = refs[:n], refs[n:2 * n]
        send_sems, recv_sems = refs[2 * n:]
        x, y, c = lax.axis_index("x"), lax.axis_index("y"), lax.axis_index("c")
        chips = [(1 - x, y), (x, 1 - y), (1 - x, 1 - y)]
        sends, plan, k = [], [], 0
        for a in range(n):
            for t in range(n_l[a] // 2):
                ii, io = 2 * t + c, 2 * t + 1 - c
                targets = [(ii, chip, c, j) for j, chip in enumerate(chips)]
                targets += [(io, chip, 1 - c, 3 + j) for j, chip in enumerate([(x, y)] + chips)]
                for piece, chip, core, rel in targets:
                    cp = pltpu.make_async_remote_copy(
                        src_ref=_slot(ins[a], chip_major[a], piece, 2 * chip[0] + chip[1]), dst_ref=outs[a].at[t, rel],
                        send_sem=send_sems.at[k + rel], recv_sem=recv_sems.at[k + rel], device_id=(*chip, core), device_id_type=MESH)
                    cp.start()
                    sends.append(cp)
                plan.append((a, t, k))
                k += 7
        for a, t, k in plan:
            for rel in range(7):
                blk = outs[a].at[t, rel]
                pltpu.make_async_remote_copy(src_ref=blk, dst_ref=blk, send_sem=send_sems.at[k + rel], recv_sem=recv_sems.at[k + rel],
                                             device_id=(x, y, 1 - c), device_id_type=MESH).wait_recv()
        for cp in sends:
            cp.wait_send()

    def oshape(a, cm, l):
        return jax.ShapeDtypeStruct((l // 2, 7) + a.shape[2:], a.dtype)

    return pl.pallas_call(
        body, name=name,
        out_shape=[oshape(a, cm, l) for a, cm, l in zip(arrs, chip_major, n_l)],
        in_specs=[HBM_SPEC] * n, out_specs=[HBM_SPEC] * n,
        scratch_shapes=[pltpu.SemaphoreType.DMA((7 * n_mine,)), pltpu.SemaphoreType.DMA((7 * n_mine,))],
    )(*arrs)


def swap_halves(arrs, name):
    n = len(arrs)
    n_mine = sum(a.shape[0] // 2 for a in arrs)

    def body(*refs):
        ins, outs = refs[:n], refs[n:2 * n]
        send_sems, recv_sems = refs[2 * n:]
        x, y, c = lax.axis_index("x"), lax.axis_index("y"), lax.axis_index("c")
        sib = (x, y, 1 - c)
        sends, k = [], 0
        for a in range(n):
            for t in range(ins[a].shape[0] // 2):
                cp = pltpu.make_async_remote_copy(src_ref=ins[a].at[2 * t + c], dst_ref=outs[a].at[2 * t + c], send_sem=send_sems.at[k],
                                                  recv_sem=recv_sems.at[k], device_id=sib, device_id_type=MESH)
                cp.start()
                sends.append((cp, a, t, k))
                k += 1
        for cp, a, t, k in sends:
            blk = outs[a].at[2 * t + 1 - c]
            pltpu.make_async_remote_copy(src_ref=blk, dst_ref=blk, send_sem=send_sems.at[k], recv_sem=recv_sems.at[k],
                                         device_id=sib, device_id_type=MESH).wait_recv()
        for cp, a, t, k in sends:
            cp.wait_send()

    return pl.pallas_call(
        body, name=name,
        out_shape=[jax.ShapeDtypeStruct(a.shape, a.dtype) for a in arrs],
        in_specs=[HBM_SPEC] * n, out_specs=[HBM_SPEC] * n,
        input_output_aliases={a: a for a in range(n)},
        scratch_shapes=[pltpu.SemaphoreType.DMA((n_mine,)), pltpu.SemaphoreType.DMA((n_mine,))],
    )(*arrs)


def mm_nn(a, w4, layer, *, tn, out_dtype, name, ncb=None, cbmap=None, res=None, perm_d=1):
    M, K = a.shape
    _, P, _, Ns = w4.shape
    nper = Ns // tn
    ncb = P * nper if ncb is None else ncb
    tm = ATTN_BLOCK * perm_d if perm_d > 1 else _tile(M, 512, 8)
    cbm = cbmap if cbmap is not None else (lambda j: j)
    nch = tn // LANES

    def body(*refs):
        if res is None:
            a_ref, w_ref, o_ref = refs[:3]
        else:
            a_ref, w_ref, x_ref, g_ref, o_ref, xo_ref = refs
        acc = jnp.dot(a_ref[...].astype(BF16), w_ref[...], preferred_element_type=F32)
        if perm_d > 1:
            scr = refs[3]
            for cj in range(nch):
                scr[cj] = acc[:, cj * LANES:(cj + 1) * LANES]
            for r in range(perm_d):
                for cj in range(nch):
                    o_ref[r, :, cj * LANES:(cj + 1) * LANES] = scr.at[cj][pl.ds(r, ATTN_BLOCK, stride=perm_d), :].astype(o_ref.dtype)
        else:
            o_ref[...] = acc.astype(o_ref.dtype)
        if res is not None:
            xo_ref[...] = x_ref[...] + g_ref[...] * acc

    in_specs = [pl.BlockSpec((tm, K), lambda i, j: (i, 0)),
                pl.BlockSpec((None, None, K, tn), lambda i, j: (layer, cbm(j) // nper, 0, cbm(j) % nper))]
    scratch = []
    if perm_d > 1:
        out_specs = [pl.BlockSpec((perm_d, ATTN_BLOCK, tn), lambda i, j: (0, i, j))]
        out_shape = [jax.ShapeDtypeStruct((perm_d, M // perm_d, ncb * tn), out_dtype)]
        scratch = [pltpu.VMEM((nch, tm, LANES), F32)]
    else:
        out_specs = [pl.BlockSpec((tm, tn), lambda i, j: (i, j))]
        out_shape = [jax.ShapeDtypeStruct((M, ncb * tn), out_dtype)]
    args = [a, w4]
    if res is not None:
        in_specs += [pl.BlockSpec((tm, tn), lambda i, j: (i, j)), pl.BlockSpec((1, tn), lambda i, j: (0, j))]
        out_specs.append(pl.BlockSpec((tm, tn), lambda i, j: (i, j)))
        out_shape.append(jax.ShapeDtypeStruct((M, ncb * tn), F32))
        args += [res[0], res[1]]
    outs = pl.pallas_call(body, name=name, grid=(M // tm, ncb), in_specs=in_specs, out_specs=out_specs, out_shape=out_shape,
                          scratch_shapes=scratch, compiler_params=_cparams(("parallel", "arbitrary")))(*args)
    if perm_d > 1:
        return outs[0].reshape(M, ncb * tn)
    return outs[0] if res is None else (outs[0], outs[1])


def permute_rows(x, d, name):
    S, C = x.shape
    R = ATTN_BLOCK * d
    ct = _tile(C, 256, LANES)
    nch = ct // LANES

    def body(x_ref, o_ref, scr):
        xv = x_ref[...].astype(F32)
        for cj in range(nch):
            scr[cj] = xv[:, cj * LANES:(cj + 1) * LANES]
        for r in range(d):
            for cj in range(nch):
                o_ref[r, :, cj * LANES:(cj + 1) * LANES] = scr.at[cj][pl.ds(r, ATTN_BLOCK, stride=d), :].astype(o_ref.dtype)

    out = pl.pallas_call(body, name=name, grid=(S // R, C // ct), in_specs=[pl.BlockSpec((R, ct), lambda i, j: (i, j))],
                         out_specs=pl.BlockSpec((d, ATTN_BLOCK, ct), lambda i, j: (0, i, j)),
                         out_shape=jax.ShapeDtypeStruct((d, S // d, C), x.dtype), scratch_shapes=[pltpu.VMEM((nch, R, LANES), F32)],
                         compiler_params=_cparams(("parallel", "parallel")))(x)
    return out.reshape(S, C)


def unpermute_rows(p, d, name, into=None, slot=0, nslots=1):
    S, C = p.shape
    R = ATTN_BLOCK * d
    ct = _tile(C, 256, LANES)
    nch = ct // LANES

    def body(*refs):
        p_ref, o_ref, scr = refs[0], refs[-2], refs[-1]
        for r in range(d):
            for cj in range(nch):
                scr.at[cj][pl.ds(r, ATTN_BLOCK, stride=d), :] = p_ref[r, :, cj * LANES:(cj + 1) * LANES].astype(F32)
        for cj in range(nch):
            o_ref[:, cj * LANES:(cj + 1) * LANES] = scr[cj].astype(o_ref.dtype)

    in_specs = [pl.BlockSpec((d, ATTN_BLOCK, ct), lambda i, j: (0, i, j))]
    args = [p.reshape(d, S // d, C)]
    aliases = {}
    if into is not None:
        in_specs.append(pl.BlockSpec(memory_space=pl.ANY))
        args.append(into)
        aliases = {1: 0}
    return pl.pallas_call(body, name=name, grid=(S // R, C // ct), in_specs=in_specs,
                          out_specs=pl.BlockSpec((None, R, ct), lambda i, j: (slot, i, j)),
                          out_shape=jax.ShapeDtypeStruct((nslots, S, C), p.dtype), scratch_shapes=[pltpu.VMEM((nch, R, LANES), F32)],
                          input_output_aliases=aliases, compiler_params=_cparams(("parallel", "parallel")))(*args)


def mm_nt(g3, w4, layer, *, tn, tk, out_dtype, name, gmap=None):
    _, M, _ = g3.shape
    _, P, K, Ns = w4.shape
    nper = Ns // tn
    ns = P * nper
    tm = _tile(M, 512, 8)
    gm = gmap if gmap is not None else (lambda s: (0, s))

    def body(g_ref, w_ref, o_ref, acc):
        s = pl.program_id(2)

        @pl.when(s == 0)
        def _():
            acc[...] = jnp.zeros_like(acc)
        acc[...] += lax.dot_general(g_ref[...].astype(BF16), w_ref[...], (((1,), (1,)), ((), ())), preferred_element_type=F32)

        @pl.when(s == ns - 1)
        def _():
            o_ref[...] = acc[...].astype(o_ref.dtype)

    return pl.pallas_call(
        body, name=name, grid=(M // tm, K // tk, ns),
        in_specs=[pl.BlockSpec((None, tm, tn), lambda i, kj, s: (gm(s)[0], i, gm(s)[1])),
                  pl.BlockSpec((None, None, tk, tn), lambda i, kj, s: (layer, s // nper, kj, s % nper))],
        out_specs=pl.BlockSpec((tm, tk), lambda i, kj, s: (i, kj)),
        out_shape=jax.ShapeDtypeStruct((M, K), out_dtype),
        scratch_shapes=[pltpu.VMEM((tm, tk), F32)],
        compiler_params=_cparams(("parallel", "parallel", "arbitrary")))(g3, w4)


def mm_tn(a, g3, dw_prev, layer, wshape, *, tn, tk, name, gmap=None):
    M, K = a.shape
    L, P, _, Ns = wshape
    nper = Ns // tn
    ns = P * nper
    tm = _tile(M, 512, 16)
    nm = M // tm
    gm = gmap if gmap is not None else (lambda s: (0, s))

    def body(*refs):
        a_ref, g_ref = refs[0], refs[1]
        o_ref, acc = refs[-2], refs[-1]
        mi = pl.program_id(2)

        @pl.when(mi == 0)
        def _():
            acc[...] = jnp.zeros_like(acc)
        acc[...] += lax.dot_general(a_ref[...].astype(BF16), g_ref[...].astype(BF16), (((0,), (0,)), ((), ())), preferred_element_type=F32)

        @pl.when(mi == nm - 1)
        def _():
            o_ref[...] = acc[...].astype(o_ref.dtype)

    in_specs = [pl.BlockSpec((tm, tk), lambda s, kj, mi: (mi, kj)),
                pl.BlockSpec((None, tm, tn), lambda s, kj, mi: (gm(s)[0], mi, gm(s)[1]))]
    args = [a, g3]
    aliases = {}
    if dw_prev is not None:
        in_specs.append(pl.BlockSpec(memory_space=pl.ANY))
        args.append(dw_prev)
        aliases = {2: 0}
    return pl.pallas_call(
        body, name=name, grid=(ns, K // tk, nm), in_specs=in_specs,
        out_specs=pl.BlockSpec((None, None, tk, tn), lambda s, kj, mi: (layer, s // nper, kj, s % nper)),
        out_shape=jax.ShapeDtypeStruct((L, P, K, Ns), BF16),
        scratch_shapes=[pltpu.VMEM((tk, tn), F32)], input_output_aliases=aliases,
        compiler_params=_cparams(("parallel", "parallel", "arbitrary")))(*args)


def _vspec(d):
    return pl.BlockSpec((1, d), lambda i: (0, 0))


def norm_mod(x, g, sh, sc, name):
    S, D = x.shape
    tm = _tile(S, 512, 16)

    def body(x_ref, g_ref, sh_ref, sc_ref, o_ref):
        xv = x_ref[...]
        r = lax.rsqrt(jnp.mean(xv * xv, axis=-1, keepdims=True) + EPS)
        o_ref[...] = ((xv * r) * g_ref[...] * (1.0 + sc_ref[...]) + sh_ref[...]).astype(o_ref.dtype)

    return pl.pallas_call(body, name=name, grid=(S // tm,),
                          in_specs=[pl.BlockSpec((tm, D), lambda i: (i, 0)), _vspec(D), _vspec(D), _vspec(D)],
                          out_specs=pl.BlockSpec((tm, D), lambda i: (i, 0)), out_shape=jax.ShapeDtypeStruct((S, D), BF16),
                          compiler_params=_cparams(("parallel",)))(x, g, sh, sc)


def norm_mod_bwd(dh, x, dres, g, sc, name):
    S, D = x.shape
    tm = _tile(S, 256, 8)

    def body(dh_ref, x_ref, dr_ref, g_ref, sc_ref, dx_ref, sums_ref):
        xv = x_ref[...]
        dhv = dh_ref[...].astype(F32)
        r = lax.rsqrt(jnp.mean(xv * xv, axis=-1, keepdims=True) + EPS)
        xn = xv * r
        one_sc = 1.0 + sc_ref[...]
        dxn = dhv * g_ref[...] * one_sc
        dx = r * (dxn - xn * jnp.mean(dxn * xn, axis=-1, keepdims=True))
        dx_ref[...] = dx + dr_ref[...]
        part = jnp.concatenate([jnp.sum(dhv, axis=0, keepdims=True), jnp.sum(dhv * xn * g_ref[...], axis=0, keepdims=True),
                                jnp.sum(dhv * one_sc * xn, axis=0, keepdims=True), jnp.zeros((5, D), F32)], axis=0)

        @pl.when(pl.program_id(0) == 0)
        def _():
            sums_ref[...] = jnp.zeros_like(sums_ref)
        sums_ref[...] += part

    row = pl.BlockSpec((tm, D), lambda i: (i, 0))
    return pl.pallas_call(body, name=name, grid=(S // tm,), in_specs=[row, row, row, _vspec(D), _vspec(D)],
                          out_specs=[row, pl.BlockSpec((8, D), lambda i: (0, 0))],
                          out_shape=[jax.ShapeDtypeStruct((S, D), F32), jax.ShapeDtypeStruct((8, D), F32)],
                          compiler_params=_cparams(("arbitrary",)))(dh, x, dres, g, sc)


def gate_bwd(dx, out, gate, name):
    S, D = dx.shape
    tm = _tile(S, 512, 16)

    def body(dx_ref, o_ref, g_ref, do_ref, sums_ref):
        dxv = dx_ref[...]
        do_ref[...] = (g_ref[...] * dxv).astype(do_ref.dtype)
        part = jnp.concatenate([jnp.sum(dxv * o_ref[...].astype(F32), axis=0, keepdims=True), jnp.zeros((7, D), F32)], axis=0)

        @pl.when(pl.program_id(0) == 0)
        def _():
            sums_ref[...] = jnp.zeros_like(sums_ref)
        sums_ref[...] += part

    row = pl.BlockSpec((tm, D), lambda i: (i, 0))
    return pl.pallas_call(body, name=name, grid=(S // tm,), in_specs=[row, row, _vspec(D)],
                          out_specs=[row, pl.BlockSpec((8, D), lambda i: (0, 0))],
                          out_shape=[jax.ShapeDtypeStruct((S, D), BF16), jax.ShapeDtypeStruct((8, D), F32)],
                          compiler_params=_cparams(("arbitrary",)))(dx, out, gate)


def loss_fwd_bwd(x, g, target, name):
    S, D = x.shape
    tm = _tile(S, 256, 8)

    def body(x_ref, g_ref, t_ref, dx_ref, sums_ref):
        xv = x_ref[...]
        r = lax.rsqrt(jnp.mean(xv * xv, axis=-1, keepdims=True) + EPS)
        xn = xv * r
        err = xn * g_ref[...] - t_ref[...]
        dy = err * (1.0 / D)
        dxn = dy * g_ref[...]
        dx_ref[...] = r * (dxn - xn * jnp.mean(dxn * xn, axis=-1, keepdims=True))
        part = jnp.concatenate([jnp.sum(dy * xn, axis=0, keepdims=True), jnp.sum(err * err, axis=0, keepdims=True),
                                jnp.zeros((6, D), F32)], axis=0)

        @pl.when(pl.program_id(0) == 0)
        def _():
            sums_ref[...] = jnp.zeros_like(sums_ref)
        sums_ref[...] += part

    row = pl.BlockSpec((tm, D), lambda i: (i, 0))
    return pl.pallas_call(body, name=name, grid=(S // tm,), in_specs=[row, _vspec(D), row],
                          out_specs=[row, pl.BlockSpec((8, D), lambda i: (0, 0))],
                          out_shape=[jax.ShapeDtypeStruct((S, D), F32), jax.ShapeDtypeStruct((8, D), F32)],
                          compiler_params=_cparams(("arbitrary",)))(x, g, target)


def pool_fwd(u, wgrp, layer, scale, name):
    S, D = u.shape
    G = len(POOL_WINDOWS)
    C = D // G
    tm = _tile(S, 256, 16)
    hb = tm // HALO

    def body(up_ref, uc_ref, w_ref, sc_ref, p_ref, z_ref, y_ref):
        i = pl.program_id(0)
        prev = jnp.where(i > 0, up_ref[...], 0.0)
        ext = jnp.concatenate([prev, uc_ref[...]], axis=0)
        t = i * tm + lax.broadcasted_iota(jnp.int32, (tm, 1), 0)
        for gi, w in enumerate(POOL_WINDOWS):
            cs = slice(gi * C, (gi + 1) * C)
            e = ext[:, cs]
            s, k = e, 1
            while k < w:
                s = s + pltpu.roll(s, k, 0)
                k *= 2
            cnt = jnp.minimum(t + 1, w).astype(F32)
            pooled = (s[HALO:] / cnt - e[HALO:]).astype(BF16)
            p_ref[:, cs] = pooled
            z = jnp.dot(pooled, w_ref[:, gi].reshape(C, C), preferred_element_type=F32)
            z_ref[:, cs] = z.astype(BF16)
            y_ref[:, cs] = (z * sc_ref[:, cs]).astype(BF16)

    row = pl.BlockSpec((tm, D), lambda i: (i, 0))
    return pl.pallas_call(
        body, name=name, grid=(S // tm,),
        in_specs=[pl.BlockSpec((HALO, D), lambda i: (jnp.maximum(i * hb - 1, 0), 0)), row,
                  pl.BlockSpec((None,) + wgrp.shape[1:], lambda i: (layer, 0, 0, 0, 0)), _vspec(D)],
        out_specs=[row, row, row], out_shape=[jax.ShapeDtypeStruct((S, D), BF16)] * 3,
        compiler_params=_cparams(("parallel",)))(u, u, wgrp, scale)


def pool_bwd(dys, z, pooled, wgrp, layer, scale, name):
    S, D = dys.shape
    G = len(POOL_WINDOWS)
    C = D // G
    tm = _tile(S, 256, 16)
    hb = tm // HALO
    nt = S // tm
    n_ext = tm + HALO

    def body(dc_ref, dn_ref, z_ref, p_ref, w_ref, sc_ref, du_ref, dw_ref, sums_ref):
        i = pl.program_id(0)

        @pl.when(i == 0)
        def _():
            dw_ref[...] = jnp.zeros_like(dw_ref)
            sums_ref[...] = jnp.zeros_like(sums_ref)
        dyc = dc_ref[...].astype(F32)
        nxt = jnp.where(i < nt - 1, dn_ref[...].astype(F32), 0.0)
        ext = jnp.concatenate([dyc, nxt], axis=0)
        sums_ref[...] += jnp.concatenate([jnp.sum(dyc * z_ref[...].astype(F32), axis=0, keepdims=True), jnp.zeros((7, D), F32)], axis=0)
        t = i * tm + lax.broadcasted_iota(jnp.int32, (n_ext, 1), 0)
        for gi, w in enumerate(POOL_WINDOWS):
            cs = slice(gi * C, (gi + 1) * C)
            wg = w_ref[:, gi].reshape(C, C)
            dz = (ext[:, cs] * sc_ref[:, cs]).astype(BF16)
            dpool = lax.dot_general(dz, wg, (((1,), (1,)), ((), ())), preferred_element_type=F32)
            dw_ref[gi] += lax.dot_general(p_ref[:, cs], dz[:tm], (((0,), (0,)), ((), ())), preferred_element_type=F32)
            cnt = jnp.minimum(t + 1, w).astype(F32)
            s, k = dpool / cnt, 1
            while k < w:
                s = s + pltpu.roll(s, n_ext - k, 0)
                k *= 2
            du_ref[:, cs] = (s[:tm] - dpool[:tm]).astype(BF16)

    row = pl.BlockSpec((tm, D), lambda i: (i, 0))
    return pl.pallas_call(
        body, name=name, grid=(nt,),
        in_specs=[row, pl.BlockSpec((HALO, D), lambda i: (jnp.minimum((i + 1) * hb, S // HALO - 1), 0)), row, row,
                  pl.BlockSpec((None,) + wgrp.shape[1:], lambda i: (layer, 0, 0, 0, 0)), _vspec(D)],
        out_specs=[row, pl.BlockSpec((G, C, C), lambda i: (0, 0, 0)), pl.BlockSpec((8, D), lambda i: (0, 0))],
        out_shape=[jax.ShapeDtypeStruct((S, D), BF16), jax.ShapeDtypeStruct((G, C, C), F32), jax.ShapeDtypeStruct((8, D), F32)],
        compiler_params=_cparams(("arbitrary",)))(dys, dys, z, pooled, wgrp, scale)


def ffn_act(hu, conv_w, conv_b, name):
    S, F2 = hu.shape
    F = F2 // 2
    tm = _tile(S, 256, 16)
    tn = _tile(F, 1408, LANES)
    nb = F // tn
    hb = tm // HALO

    def body(ap_ref, ac_ref, v_ref, w_ref, b_ref, o_ref):
        i = pl.program_id(0)
        prev = jnp.where(i > 0, ap_ref[...].astype(F32), 0.0)
        ext = jnp.concatenate([prev, ac_ref[...].astype(F32)], axis=0)
        conv = b_ref[...] + pltpu.roll(ext, 2, 0) * w_ref[0:1, :] + pltpu.roll(ext, 1, 0) * w_ref[1:2, :] + ext * w_ref[2:3, :]
        conv = conv[HALO:]
        o_ref[...] = (conv * _sigmoid(conv) * v_ref[...].astype(F32)).astype(o_ref.dtype)

    return pl.pallas_call(
        body, name=name, grid=(S // tm, nb),
        in_specs=[pl.BlockSpec((HALO, tn), lambda i, j: (jnp.maximum(i * hb - 1, 0), j)), pl.BlockSpec((tm, tn), lambda i, j: (i, j)),
                  pl.BlockSpec((tm, tn), lambda i, j: (i, j + nb)), pl.BlockSpec((3, tn), lambda i, j: (0, j)),
                  pl.BlockSpec((1, tn), lambda i, j: (0, j))],
        out_specs=pl.BlockSpec((tm, tn), lambda i, j: (i, j)), out_shape=jax.ShapeDtypeStruct((S, F), BF16),
        compiler_params=_cparams(("parallel", "parallel")))(hu, hu, hu, conv_w, conv_b)


def ffn_act_bwd(dg, hu, conv_w, conv_b, name):
    S, F = dg.shape
    tm = _tile(S, 256, 16)
    tn = _tile(F, 1408, LANES)
    nb = F // tn
    hb = tm // HALO
    nt = S // tm
    n_ext = tm + 2 * HALO

    def body(gc_ref, gn_ref, ap_ref, ac_ref, an_ref, vc_ref, vn_ref, w_ref, b_ref, o_ref, sums_ref):
        i = pl.program_id(1)

        @pl.when(i == 0)
        def _():
            sums_ref[...] = jnp.zeros_like(sums_ref)
        zeros = jnp.zeros((HALO, tn), F32)
        not_last = i < nt - 1
        a_ext = jnp.concatenate([jnp.where(i > 0, ap_ref[...].astype(F32), 0.0), ac_ref[...].astype(F32), an_ref[...].astype(F32)], axis=0)
        v_ext = jnp.concatenate([zeros, vc_ref[...].astype(F32), vn_ref[...].astype(F32)], axis=0)
        g_ext = jnp.concatenate([zeros, gc_ref[...].astype(F32), jnp.where(not_last, gn_ref[...].astype(F32), 0.0)], axis=0)
        w0, w1, w2 = w_ref[0:1, :], w_ref[1:2, :], w_ref[2:3, :]
        a_m2, a_m1 = pltpu.roll(a_ext, 2, 0), pltpu.roll(a_ext, 1, 0)
        conv = b_ref[...] + a_m2 * w0 + a_m1 * w1 + a_ext * w2
        sig = _sigmoid(conv)
        silu = conv * sig
        dsilu = sig * (1.0 + conv * (1.0 - sig))
        dconv = g_ext * v_ext * dsilu
        da = dconv * w2 + pltpu.roll(dconv, n_ext - 1, 0) * w1 + pltpu.roll(dconv, n_ext - 2, 0) * w0
        cur = slice(HALO, HALO + tm)
        o_ref[0] = da[cur].astype(o_ref.dtype)
        o_ref[1] = (g_ext * silu)[cur].astype(o_ref.dtype)
        dc = dconv[cur]
        part = jnp.concatenate([jnp.sum(dc * a_m2[cur], axis=0, keepdims=True), jnp.sum(dc * a_m1[cur], axis=0, keepdims=True),
                                jnp.sum(dc * a_ext[cur], axis=0, keepdims=True), jnp.sum(dc, axis=0, keepdims=True),
                                jnp.zeros((4, tn), F32)], axis=0)
        sums_ref[...] += part

    def prev(i):
        return jnp.maximum(i * hb - 1, 0)

    def nxt(i):
        return jnp.minimum((i + 1) * hb, S // HALO - 1)

    return pl.pallas_call(
        body, name=name, grid=(nb, nt),
        in_specs=[pl.BlockSpec((tm, tn), lambda j, i: (i, j)), pl.BlockSpec((HALO, tn), lambda j, i: (nxt(i), j)),
                  pl.BlockSpec((HALO, tn), lambda j, i: (prev(i), j)), pl.BlockSpec((tm, tn), lambda j, i: (i, j)),
                  pl.BlockSpec((HALO, tn), lambda j, i: (nxt(i), j)),
                  pl.BlockSpec((tm, tn), lambda j, i: (i, j + nb)), pl.BlockSpec((HALO, tn), lambda j, i: (nxt(i), j + nb)),
                  pl.BlockSpec((3, tn), lambda j, i: (0, j)), pl.BlockSpec((1, tn), lambda j, i: (0, j))],
        out_specs=[pl.BlockSpec((2, tm, tn), lambda j, i: (0, i, j)), pl.BlockSpec((8, tn), lambda j, i: (0, j))],
        out_shape=[jax.ShapeDtypeStruct((2, S, F), BF16), jax.ShapeDtypeStruct((8, F), F32)],
        compiler_params=_cparams(("parallel", "arbitrary")))(dg, dg, hu, hu, hu, hu, hu, conv_w, conv_b)


def _head_expander(n_heads, da):
    e = np.zeros((LANES, da), np.float32)
    for h in range(n_heads):
        e[h, h * HEAD_DIM:(h + 1) * HEAD_DIM] = 1.0
    return jnp.asarray(e, BF16)


def _split_dot(v, e, dims):
    hi = v.astype(BF16)
    lo = (v - hi.astype(F32)).astype(BF16)
    return (lax.dot_general(hi, e, dims, preferred_element_type=F32) + lax.dot_general(lo, e, dims, preferred_element_type=F32))


def _lane_col(tile, h):
    lane = lax.broadcasted_iota(jnp.int32, tile.shape, 1)
    return jnp.sum(jnp.where(lane == h, tile, 0.0), axis=1, keepdims=True)


def attn_branch_fwd(q, kv, gi, slopes, name):
    S, DA = q.shape
    H = DA // HEAD_DIM
    window, d = BRANCHES[gi]
    n_steps = window // d
    blk = ATTN_BLOCK
    assert n_steps == blk and (S // d) % blk == 0
    nbs = S // d // blk
    scale = HEAD_DIM ** -0.5

    def body(q_ref, kp_ref, kc_ref, vp_ref, vc_ref, o_ref, l_ref):
        jb = pl.program_id(1)
        row = lax.broadcasted_iota(jnp.int32, (blk, 2 * blk), 0)
        col = lax.broadcasted_iota(jnp.int32, (blk, 2 * blk), 1)
        delta = row + blk - col
        valid = (delta >= 0) & (delta <= n_steps) & ((col >= blk) | (jb > 0))
        dist = (delta * d).astype(F32)
        lane = lax.broadcasted_iota(jnp.int32, (blk, LANES), 1)
        ltile = jnp.zeros((blk, LANES), F32)
        for h in range(H):
            hs = slice(h * HEAD_DIM, (h + 1) * HEAD_DIM)
            k2 = jnp.concatenate([kp_ref[:, hs], kc_ref[:, hs]], axis=0)
            v2 = jnp.concatenate([vp_ref[:, hs], vc_ref[:, hs]], axis=0)
            s = lax.dot_general(q_ref[:, hs], k2, (((1,), (1,)), ((), ())), preferred_element_type=F32) * scale
            s = jnp.where(valid, s - float(slopes[h]) * dist, NEG)
            m = jnp.max(s, axis=-1, keepdims=True)
            p = jnp.exp(s - m)
            l = jnp.sum(p, axis=-1, keepdims=True)
            o_ref[:, hs] = jnp.dot(p.astype(BF16), v2, preferred_element_type=F32) / l
            ltile = jnp.where(lane == h, m + jnp.log(l), ltile)
        l_ref[...] = ltile

    def cur(width, off):
        return pl.BlockSpec((blk, width), lambda r, jb: (r * nbs + jb, off))

    def prv(width, off):
        return pl.BlockSpec((blk, width), lambda r, jb: (r * nbs + jnp.maximum(jb - 1, 0), off))

    return pl.pallas_call(
        body, name=name, grid=(d, nbs),
        in_specs=[cur(DA, 0), prv(DA, 0), cur(DA, 0), prv(DA, 1), cur(DA, 1)],
        out_specs=[cur(DA, 0), cur(LANES, 0)],
        out_shape=[jax.ShapeDtypeStruct((S, DA), F32), jax.ShapeDtypeStruct((S, LANES), F32)],
        compiler_params=_cparams(("parallel", "parallel")))(q, kv, kv, kv, kv)


def attn_combine(os_, lses, name):
    S, DA = os_[0].shape
    H = DA // HEAD_DIM
    tm = _tile(S, 256, 16)
    expander = _head_expander(H, DA)
    nbr = len(os_)

    def body(*refs):
        o_refs, l_refs, e_ref = refs[:nbr], refs[nbr:2 * nbr], refs[2 * nbr]
        out_ref, lse_ref = refs[2 * nbr + 1:]
        ls = [r[...] for r in l_refs]
        lmax = functools.reduce(jnp.maximum, ls)
        es = [jnp.exp(l - lmax) for l in ls]
        den = functools.reduce(lambda a, b: a + b, es)
        lse_ref[...] = lmax + jnp.log(den)
        acc = jnp.zeros((tm, DA), F32)
        for e, o_ref in zip(es, o_refs):
            acc = acc + _split_dot(e / den, e_ref[...], (((1,), (0,)), ((), ()))) * o_ref[...]
        out_ref[...] = acc.astype(out_ref.dtype)

    row = pl.BlockSpec((tm, DA), lambda i: (i, 0))
    lrow = pl.BlockSpec((tm, LANES), lambda i: (i, 0))
    return pl.pallas_call(
        body, name=name, grid=(S // tm,),
        in_specs=[row] * nbr + [lrow] * nbr + [pl.BlockSpec((LANES, DA), lambda i: (0, 0))],
        out_specs=[row, lrow], out_shape=[jax.ShapeDtypeStruct((S, DA), BF16), jax.ShapeDtypeStruct((S, LANES), F32)],
        compiler_params=_cparams(("parallel",)))(*os_, *lses, expander)


def attn_delta(do, o, name):
    S, DA = o.shape
    H = DA // HEAD_DIM
    tm = _tile(S, 512, 16)
    expander = _head_expander(H, DA)

    def body(do_ref, o_ref, e_ref, d_ref):
        prod = do_ref[...].astype(F32) * o_ref[...].astype(F32)
        d_ref[...] = _split_dot(prod, e_ref[...], (((1,), (1,)), ((), ())))

    row = pl.BlockSpec((tm, DA), lambda i: (i, 0))
    return pl.pallas_call(body, name=name, grid=(S // tm,), in_specs=[row, row, pl.BlockSpec((LANES, DA), lambda i: (0, 0))],
                          out_specs=pl.BlockSpec((tm, LANES), lambda i: (i, 0)), out_shape=jax.ShapeDtypeStruct((S, LANES), F32),
                          compiler_params=_cparams(("parallel",)))(do, o, expander)


def attn_branch_dq(q, kv, do, lse, dlt, gi, slopes, name):
    S, DA = q.shape
    H = DA // HEAD_DIM
    window, d = BRANCHES[gi]
    n_steps = window // d
    blk = ATTN_BLOCK
    nbs = S // d // blk
    scale = HEAD_DIM ** -0.5

    def body(q_ref, kp_ref, kc_ref, vp_ref, vc_ref, do_ref, l_ref, d_ref, dq_ref):
        jb = pl.program_id(1)
        row = lax.broadcasted_iota(jnp.int32, (blk, 2 * blk), 0)
        col = lax.broadcasted_iota(jnp.int32, (blk, 2 * blk), 1)
        delta = row + blk - col
        valid = (delta >= 0) & (delta <= n_steps) & ((col >= blk) | (jb > 0))
        dist = (delta * d).astype(F32)
        ltile, dtile = l_ref[...], d_ref[...]
        for h in range(H):
            hs = slice(h * HEAD_DIM, (h + 1) * HEAD_DIM)
            k2 = jnp.concatenate([kp_ref[:, hs], kc_ref[:, hs]], axis=0)
            v2 = jnp.concatenate([vp_ref[:, hs], vc_ref[:, hs]], axis=0)
            s = lax.dot_general(q_ref[:, hs], k2, (((1,), (1,)), ((), ())), preferred_element_type=F32) * scale
            s = jnp.where(valid, s - float(slopes[h]) * dist - _lane_col(ltile, h), NEG)
            p = jnp.exp(s)
            dp = lax.dot_general(do_ref[:, hs], v2, (((1,), (1,)), ((), ())), preferred_element_type=F32)
            ds = (p * (dp - _lane_col(dtile, h))).astype(BF16)
            dq_ref[:, hs] = (jnp.dot(ds, k2, preferred_element_type=F32) * scale).astype(dq_ref.dtype)

    def cur(width, off):
        return pl.BlockSpec((blk, width), lambda r, jb: (r * nbs + jb, off))

    def prv(width, off):
        return pl.BlockSpec((blk, width), lambda r, jb: (r * nbs + jnp.maximum(jb - 1, 0), off))

    return pl.pallas_call(
        body, name=name, grid=(d, nbs),
        in_specs=[cur(DA, 0), prv(DA, 0), cur(DA, 0), prv(DA, 1), cur(DA, 1), cur(DA, 0), cur(LANES, 0), cur(LANES, 0)],
        out_specs=cur(DA, 0), out_shape=jax.ShapeDtypeStruct((S, DA), BF16),
        compiler_params=_cparams(("parallel", "parallel")))(q, kv, kv, kv, kv, do, lse, dlt)


def attn_branch_dkv(q, kv, do, lse, dlt, dkv_prev, gi, slopes, name):
    S, DA = q.shape
    H = DA // HEAD_DIM
    window, d = BRANCHES[gi]
    n_steps = window // d
    blk = ATTN_BLOCK
    nbs = S // d // blk
    scale = HEAD_DIM ** -0.5
    accumulate = dkv_prev is not None

    def body(*refs):
        k_ref, v_ref, qc_ref, qn_ref, doc_ref, don_ref, lc_ref, ln_ref, dc_ref, dn_ref = refs[:10]
        out_ref = refs[-1]
        kb = pl.program_id(1)
        row = lax.broadcasted_iota(jnp.int32, (2 * blk, blk), 0)
        col = lax.broadcasted_iota(jnp.int32, (2 * blk, blk), 1)
        delta = row - col
        valid = (delta >= 0) & (delta <= n_steps) & ((row < blk) | (kb < nbs - 1))
        dist = (delta * d).astype(F32)
        l2 = jnp.concatenate([lc_ref[...], ln_ref[...]], axis=0)
        d2 = jnp.concatenate([dc_ref[...], dn_ref[...]], axis=0)
        for h in range(H):
            hs = slice(h * HEAD_DIM, (h + 1) * HEAD_DIM)
            vs = slice(DA + h * HEAD_DIM, DA + (h + 1) * HEAD_DIM)
            q2 = jnp.concatenate([qc_ref[:, hs], qn_ref[:, hs]], axis=0)
            do2 = jnp.concatenate([doc_ref[:, hs], don_ref[:, hs]], axis=0)
            s = lax.dot_general(q2, k_ref[:, hs], (((1,), (1,)), ((), ())), preferred_element_type=F32) * scale
            s = jnp.where(valid, s - float(slopes[h]) * dist - _lane_col(l2, h), NEG)
            p = jnp.exp(s)
            dvh = lax.dot_general(p.astype(BF16), do2, (((0,), (0,)), ((), ())), preferred_element_type=F32)
            dp = lax.dot_general(do2, v_ref[:, hs], (((1,), (1,)), ((), ())), preferred_element_type=F32)
            ds = (p * (dp - _lane_col(d2, h))).astype(BF16)
            dkh = lax.dot_general(ds, q2, (((0,), (0,)), ((), ())), preferred_element_type=F32) * scale
            if accumulate:
                dkh = dkh + refs[10][:, hs]
                dvh = dvh + refs[10][:, vs]
            out_ref[:, hs] = dkh
            out_ref[:, vs] = dvh

    def cur(width, off):
        return pl.BlockSpec((blk, width), lambda r, kb: (r * nbs + kb, off))

    def nxt(width, off):
        return pl.BlockSpec((blk, width), lambda r, kb: (r * nbs + jnp.minimum(kb + 1, nbs - 1), off))

    in_specs = [cur(DA, 0), cur(DA, 1), cur(DA, 0), nxt(DA, 0), cur(DA, 0), nxt(DA, 0),
                cur(LANES, 0), nxt(LANES, 0), cur(LANES, 0), nxt(LANES, 0)]
    args = [kv, kv, q, q, do, do, lse, lse, dlt, dlt]
    aliases = {}
    if accumulate:
        in_specs.append(cur(2 * DA, 0))
        args.append(dkv_prev)
        aliases = {10: 0}
    return pl.pallas_call(
        body, name=name, grid=(d, nbs), in_specs=in_specs, out_specs=cur(2 * DA, 0),
        out_shape=jax.ShapeDtypeStruct((S, 2 * DA), F32), input_output_aliases=aliases,
        compiler_params=_cparams(("parallel", "parallel")))(*args)


def ada_project(c16, w3, b3, name):
    L, D, Ns = w3.shape
    tn = _tile(Ns, 512, LANES)

    def body(c_ref, w_ref, b_ref, o_ref):
        cv = c_ref[...]
        cond = (cv * _sigmoid(cv)).astype(BF16)
        o_ref[...] = jnp.dot(cond, w_ref[...].astype(BF16), preferred_element_type=F32) + b_ref[...]

    return pl.pallas_call(
        body, name=name, grid=(L, Ns // tn),
        in_specs=[pl.BlockSpec((16, D), lambda l, j: (0, 0)), pl.BlockSpec((None, D, tn), lambda l, j: (l, 0, j)),
                  pl.BlockSpec((None, 1, tn), lambda l, j: (l, 0, j))],
        out_specs=pl.BlockSpec((None, 16, tn), lambda l, j: (l, 0, j)), out_shape=jax.ShapeDtypeStruct((L, 16, Ns), F32),
        compiler_params=_cparams(("parallel", "parallel")))(c16, w3, b3)


def _adamw(w, g, m, v):
    m = B1 * m + (1.0 - B1) * g
    v = B2 * v + (1.0 - B2) * (g * g)
    m_hat = m / (1.0 - B1 ** STEP)
    v_hat = v / (1.0 - B2 ** STEP)
    delta = -LR * (m_hat / (jnp.sqrt(v_hat) + ADAM_EPS) + WD * w)
    return delta, m, v


def ada_grad_adamw(c16, d3, w3, m3, v3, name):
    L, D, Ns = w3.shape
    tk = _tile(D, 256, 8)

    def body(c_ref, d_ref, w_ref, m_ref, v_ref, g_out, dl_out, m_out, v_out):
        cv = c_ref[...]
        cond = (cv * _sigmoid(cv)).astype(BF16)
        g = lax.dot_general(cond, d_ref[...].astype(BF16), (((0,), (0,)), ((), ())), preferred_element_type=F32)
        g_out[...] = g
        dl_out[...], m_out[...], v_out[...] = _adamw(w_ref[...], g, m_ref[...], v_ref[...])

    wspec = pl.BlockSpec((None, tk, Ns), lambda l, kj: (l, kj, 0))
    return pl.pallas_call(
        body, name=name, grid=(L, D // tk),
        in_specs=[pl.BlockSpec((16, tk), lambda l, kj: (0, kj)), pl.BlockSpec((None, 16, Ns), lambda l, kj: (l, 0, 0)), wspec, wspec, wspec],
        out_specs=[wspec] * 4, out_shape=[jax.ShapeDtypeStruct((L, D, Ns), F32)] * 4,
        compiler_params=_cparams(("parallel", "parallel")))(c16, d3, w3, m3, v3)


def adamw(w, g, m, v, name):
    R, C = w.shape
    tr = _tile(R, 256, 8)

    def body(w_ref, g_ref, m_ref, v_ref, g_out, dl_out, m_out, v_out):
        g = g_ref[...]
        g_out[...] = g
        dl_out[...], m_out[...], v_out[...] = _adamw(w_ref[...], g, m_ref[...], v_ref[...])

    spec = pl.BlockSpec((tr, C), lambda i: (i, 0))
    return pl.pallas_call(body, name=name, grid=(R // tr,), in_specs=[spec] * 4, out_specs=[spec] * 4,
                          out_shape=[jax.ShapeDtypeStruct((R, C), F32)] * 4, compiler_params=_cparams(("parallel",)))(w, g, m, v)


def sum_partials(own, recv, chip_major, pos, name):
    T, _, R, C = recv.shape
    L = 2 * T
    tr = _tile(R, 256, 16)

    def body(pos_ref, own_ref, recv_ref, o_ref):
        acc = own_ref[...].astype(F32)
        for rel in range(7):
            acc = acc + recv_ref[rel].astype(F32)
        o_ref[...] = acc

    if chip_major:
        own_spec = pl.BlockSpec((None, None, tr, C), lambda t, r, pos: (pos[1], 2 * t + pos[0], r, 0))
    else:
        own_spec = pl.BlockSpec((None, None, tr, C), lambda t, r, pos: (2 * t + pos[0], pos[1], r, 0))
    return pl.pallas_call(
        body, name=name,
        grid_spec=pltpu.PrefetchScalarGridSpec(
            num_scalar_prefetch=1, grid=(T, R // tr),
            in_specs=[own_spec, pl.BlockSpec((None, 7, tr, C), lambda t, r, pos: (t, 0, r, 0))],
            out_specs=pl.BlockSpec((None, tr, C), lambda t, r, pos: (2 * t + pos[0], r, 0))),
        out_shape=jax.ShapeDtypeStruct((L, R, C), F32),
        compiler_params=_cparams(("parallel", "parallel")))(pos, own, recv)


def sum_rows8(g8, name):
    _, R, C = g8.shape

    def body(g_ref, o_ref):
        acc = g_ref[0]
        for i in range(1, N_DEV):
            acc = acc + g_ref[i]
        o_ref[...] = acc

    return pl.pallas_call(body, name=name, grid=(1,), in_specs=[pl.BlockSpec((N_DEV, R, C), lambda i: (0, 0, 0))],
                          out_specs=pl.BlockSpec((R, C), lambda i: (0, 0)), out_shape=jax.ShapeDtypeStruct((R, C), F32),
                          compiler_params=_cparams(("arbitrary",)))(g8)


def _pack(vecs):
    flat = [v.reshape(-1).astype(F32) for v in vecs]
    sizes = [f.shape[0] for f in flat]
    total = sum(sizes)
    padded = -(-total // (8 * PACK_W)) * (8 * PACK_W)
    buf = jnp.concatenate(flat + [jnp.zeros((padded - total,), F32)])
    offs = np.concatenate([[0], np.cumsum(sizes)])
    return buf.reshape(-1, PACK_W), offs


def _unpack(buf, offs, shapes):
    flat = buf.reshape(-1)
    return [flat[int(offs[i]):int(offs[i + 1])].reshape(s) for i, s in enumerate(shapes)]


def kernel(x, c, ada_w, ada_b, norm1_g, norm2_g, pool_w_in, pool_w_grp, pool_scale, pool_w_out, kv_norm_g, kv_ada_w, kv_ada_b, w_kv, attn_w_q, attn_w_o, ffn_w_up, ffn_conv_w, ffn_conv_b, ffn_w_down, final_g, loss_target, m_ada_w, m_ada_b, m_norm1_g, m_norm2_g, m_pool_w_in, m_pool_w_grp, m_pool_scale, m_pool_w_out, m_kv_norm_g, m_kv_ada_w, m_kv_ada_b, m_w_kv, m_attn_w_q, m_attn_w_o, m_ffn_w_up, m_ffn_conv_w, m_ffn_conv_b, m_ffn_w_down, m_final_g, v_ada_w, v_ada_b, v_norm1_g, v_norm2_g, v_pool_w_in, v_pool_w_grp, v_pool_scale, v_pool_w_out, v_kv_norm_g, v_kv_ada_w, v_kv_ada_b, v_w_kv, v_attn_w_q, v_attn_w_o, v_ffn_w_up, v_ffn_conv_w, v_ffn_conv_b, v_ffn_w_down, v_final_g):
    S, D = x.shape[1], x.shape[2]
    depth = ada_w.shape[0]
    n_pool = pool_w_in.shape[0]
    n_attn = attn_w_q.shape[0]
    G = len(POOL_WINDOWS)
    NB = len(BRANCHES)
    DA = attn_w_o.shape[1] * N_CHIPS
    H = DA // HEAD_DIM
    F = ffn_conv_b.shape[1]
    Fs = F // N_CHIPS
    Dq = D // N_CHIPS
    ada_ns = ada_w.shape[2]
    kvada_ns = kv_ada_w.shape[1]
    slopes = _alibi_slopes(NB * H).reshape(NB, H)

    ix, iy, ic = lax.axis_index("x"), lax.axis_index("y"), lax.axis_index("c")
    p_me = 2 * ix + iy
    b_me = 4 * ix + 2 * iy + ic
    pos = jnp.stack([ic, p_me]).astype(jnp.int32)
    xs, tgt = x[0], loss_target[0]

    pk, offs = _pack([c, pool_scale, ffn_conv_w])
    rows1 = pk.shape[0]
    got = all_gather8(pk, "gather_small_in").reshape(N_DEV, rows1, PACK_W)
    c8 = got.reshape(N_DEV, -1)[:, :D]
    c16 = jnp.concatenate([c8, jnp.zeros_like(c8)], axis=0)
    chip_rows = got[0::2].reshape(N_CHIPS, -1)
    scale_full = chip_rows[:, int(offs[1]):int(offs[2])].reshape(N_CHIPS, n_pool, Dq).transpose(1, 0, 2).reshape(n_pool, D)
    convw_full = chip_rows[:, int(offs[2]):int(offs[3])].reshape(N_CHIPS, depth, 3, Fs).transpose(1, 2, 0, 3).reshape(depth, 3, F)

    ada_b_loc = lax.dynamic_slice(ada_b, (0, p_me * ada_ns), (depth, ada_ns)).reshape(depth, 1, ada_ns)
    kvb_loc = lax.dynamic_slice(kv_ada_b, (p_me * kvada_ns,), (kvada_ns,)).reshape(1, 1, kvada_ns)
    mods_loc = ada_project(c16, ada_w, ada_b_loc, "ada_project")[:, :N_DEV]
    kvmod_loc = ada_project(c16, kv_ada_w.reshape(1, D, kvada_ns), kvb_loc, "kv_ada_project")[0, :N_DEV]
    mods_cat = jnp.concatenate([mods_loc.transpose(1, 0, 2).reshape(N_DEV, depth * ada_ns), kvmod_loc], axis=1)
    mods_all = all_gather8(mods_cat, "gather_mods").reshape(N_CHIPS, 2, N_DEV, -1)
    mine = lax.dynamic_index_in_dim(mods_all[:, 0], b_me, axis=1, keepdims=False)
    mod = mine[:, :depth * ada_ns].reshape(N_CHIPS, depth, ada_ns).transpose(1, 0, 2).reshape(depth, 6, 1, D)
    kvmod = mine[:, depth * ada_ns:].reshape(2, 1, D)

    big = [pool_w_in, pool_w_grp, pool_w_out, w_kv.reshape(2, D // 2, -1), attn_w_q, attn_w_o, ffn_w_up, ffn_w_down]
    chip_major = [False, False, False, True, False, False, False, False]
    g_in, g_grp, g_out, g_kv, g_q, g_o, g_up, g_down = gather_weights([w.astype(BF16) for w in big], chip_major, "gather_weights")
    w_in4 = g_in.reshape(n_pool, 1, D, D)
    w_out4 = g_out.reshape(n_pool, 1, D, D)
    w_kv4 = g_kv.reshape(1, N_CHIPS, D, -1)
    w_q4 = g_q
    w_o4 = g_o.reshape(n_attn, 1, DA, D)
    w_up4 = g_up
    w_down4 = g_down.reshape(depth, 1, F, D)

    dil = [d for _, d in BRANCHES]
    kv_tn = DA // 2
    q_tn = DA // 4

    def kv_gmap(s):
        return s // 2 % NB, (s // (2 * NB)) * 2 + s % 2

    def q_gmap(s):
        return s // 4, s % 4

    up_tn = w_up4.shape[3]
    up_per_half = F // up_tn

    def up_gmap(s):
        return s // up_per_half, s % up_per_half

    def vec(v):
        return v.reshape(1, -1)

    saved = []
    xcur = xs
    kvs = None
    for l in range(depth):
        sh1, sc1, g1, sh2, sc2, g2 = [mod[l, i] for i in range(6)]
        st = {"x0": xcur}
        h1 = norm_mod(xcur, vec(norm1_g[l]), sh1, sc1, "norm_mod")
        st["h1"] = h1
        if l < n_pool:
            u = mm_nn(h1, w_in4, l, tn=D // 2, out_dtype=F32, name="pool_in_proj")
            pooled, z, ys = pool_fwd(u, g_grp, l, vec(scale_full[l]), "pool_mix")
            out, x1 = mm_nn(ys, w_out4, l, tn=D // 2, out_dtype=BF16, name="pool_out_proj", res=(xcur, g1))
            st.update(pooled=pooled, z=z, ys=ys, out1=out)
        else:
            j = l - n_pool
            if j == 0:
                hkv = norm_mod(xcur, vec(kv_norm_g), kvmod[0], kvmod[1], "norm_mod")
                kvs = [mm_nn(hkv, w_kv4, 0, tn=kv_tn, out_dtype=BF16, name=f"kv_proj_b{gi}", ncb=4, perm_d=dil[gi],
                             cbmap=functools.partial(lambda jj, gi: 2 * gi + (jj // 2) * 2 * NB + jj % 2, gi=gi)) for gi in range(NB)]
                kv_state = {"x": xcur, "hkv": hkv}
            qs, os_, lses = [], [], []
            for gi in range(NB):
                q_b = mm_nn(h1, w_q4, j, tn=q_tn, out_dtype=BF16, name=f"q_proj_b{gi}", ncb=4, perm_d=dil[gi],
                            cbmap=functools.partial(lambda jj, gi: 4 * gi + jj, gi=gi))
                o_b, l_b = attn_branch_fwd(q_b, kvs[gi], gi, slopes[gi], f"attn_fwd_b{gi}")
                if dil[gi] > 1:
                    o_b = unpermute_rows(o_b, dil[gi], f"unpermute_o_b{gi}")[0]
                    l_b = unpermute_rows(l_b, dil[gi], f"unpermute_lse_b{gi}")[0]
                qs.append(q_b)
                os_.append(o_b)
                lses.append(l_b)
            o, lse = attn_combine(os_, lses, "attn_combine")
            out, x1 = mm_nn(o, w_o4, j, tn=D // 2, out_dtype=BF16, name="attn_out_proj", res=(xcur, g1))
            st.update(qs=qs, o=o, lse=lse, out1=out)
        st["x1"] = x1
        h2 = norm_mod(x1, vec(norm2_g[l]), sh2, sc2, "norm_mod")
        hu = mm_nn(h2, w_up4, l, tn=up_tn, out_dtype=BF16, name="ffn_up_proj")
        gated = ffn_act(hu, convw_full[l], vec(ffn_conv_b[l]), "ffn_act")
        out2, x2 = mm_nn(gated, w_down4, l, tn=D // 2, out_dtype=BF16, name="ffn_down_proj", res=(x1, g2))
        st.update(h2=h2, hu=hu, gated=gated, out2=out2)
        saved.append(st)
        xcur = x2

    dx, fsums = loss_fwd_bwd(xcur, vec(final_g), tgt, "loss_head")
    loss = lax.psum(0.5 * jnp.sum(fsums[1]) / D, ("x", "y", "c"))
    d_final_g = fsums[0]

    dmods = [None] * depth
    d_n1 = [None] * depth
    d_n2 = [None] * depth
    d_convw = [None] * depth
    d_convb = [None] * depth
    d_scale = [None] * n_pool
    d_grp = [None] * n_pool
    dw_in = dw_out = dw_q = dw_o = dw_up = dw_down = None
    dkvs = [None] * NB
    for l in reversed(range(depth)):
        st = saved[l]
        sh1, sc1, g1, sh2, sc2, g2 = [mod[l, i] for i in range(6)]
        dout2, s_g2 = gate_bwd(dx, st["out2"], g2, "gate_bwd")
        dgated = mm_nt(dout2.reshape(1, S, D), w_down4, l, tn=D, tk=_tile(F, 1408, LANES), out_dtype=BF16, name="ffn_down_bwd")
        dw_down = mm_tn(st["gated"], dout2.reshape(1, S, D), dw_down, l, w_down4.shape, tn=D, tk=_tile(F, 1408, LANES), name="ffn_down_dw")
        dhu, s_conv = ffn_act_bwd(dgated, st["hu"], convw_full[l], vec(ffn_conv_b[l]), "ffn_act_bwd")
        dh2 = mm_nt(dhu, w_up4, l, tn=up_tn, tk=D, out_dtype=F32, name="ffn_up_bwd", gmap=up_gmap)
        dw_up = mm_tn(st["h2"], dhu, dw_up, l, w_up4.shape, tn=up_tn, tk=D, name="ffn_up_dw", gmap=up_gmap)
        dx, s_n2 = norm_mod_bwd(dh2, st["x1"], dx, vec(norm2_g[l]), sc2, "norm_mod_bwd")
        d_convw[l], d_convb[l] = s_conv[0:3], s_conv[3]
        d_n2[l] = s_n2[2]
        dout1, s_g1 = gate_bwd(dx, st["out1"], g1, "gate_bwd")
        if l < n_pool:
            dys = mm_nt(dout1.reshape(1, S, D), w_out4, l, tn=D, tk=D // 2, out_dtype=F32, name="pool_out_bwd")
            dw_out = mm_tn(st["ys"], dout1.reshape(1, S, D), dw_out, l, w_out4.shape, tn=D, tk=D, name="pool_out_dw")
            du, d_grp[l], s_sc = pool_bwd(dys, st["z"], st["pooled"], g_grp, l, vec(scale_full[l]), "pool_mix_bwd")
            d_scale[l] = s_sc[0]
            dh1 = mm_nt(du.reshape(1, S, D), w_in4, l, tn=D, tk=D // 2, out_dtype=F32, name="pool_in_bwd")
            dw_in = mm_tn(st["h1"], du.reshape(1, S, D), dw_in, l, w_in4.shape, tn=D, tk=D, name="pool_in_dw")
        else:
            j = l - n_pool
            do = mm_nt(dout1.reshape(1, S, D), w_o4, j, tn=D, tk=DA // 2, out_dtype=BF16, name="attn_out_bwd")
            dw_o = mm_tn(st["o"], dout1.reshape(1, S, D), dw_o, j, w_o4.shape, tn=D, tk=DA, name="attn_out_dw")
            dlt = attn_delta(do, st["o"], "attn_delta")
            dq3 = None
            for gi in range(NB):
                d = dil[gi]
                do_b, l_b, dl_b = do, st["lse"], dlt
                if d > 1:
                    do_b = permute_rows(do, d, f"permute_do_b{gi}")
                    l_b = permute_rows(st["lse"], d, f"permute_lse_b{gi}")
                    dl_b = permute_rows(dlt, d, f"permute_delta_b{gi}")
                dq_b = attn_branch_dq(st["qs"][gi], kvs[gi], do_b, l_b, dl_b, gi, slopes[gi], f"attn_dq_b{gi}")
                dq3 = unpermute_rows(dq_b, d, f"unpermute_dq_b{gi}", into=dq3, slot=gi, nslots=NB)
                dkvs[gi] = attn_branch_dkv(st["qs"][gi], kvs[gi], do_b, l_b, dl_b, dkvs[gi], gi, slopes[gi],
                                           f"attn_dkv_b{gi}" + ("" if dkvs[gi] is None else "_acc"))
            dh1 = mm_nt(dq3, w_q4, j, tn=q_tn, tk=D, out_dtype=F32, name="q_proj_bwd", gmap=q_gmap)
            dw_q = mm_tn(st["h1"], dq3, dw_q, j, w_q4.shape, tn=q_tn, tk=D, name="q_proj_dw", gmap=q_gmap)
        dx, s_n1 = norm_mod_bwd(dh1, st["x0"], dx, vec(norm1_g[l]), sc1, "norm_mod_bwd")
        d_n1[l] = s_n1[2]
        dmods[l] = jnp.stack([s_n1[0], s_n1[1], s_g1[0], s_n2[0], s_n2[1], s_g2[0]])
        if l == n_pool:
            dkv3 = None
            for gi in range(NB):
                dkv3 = unpermute_rows(dkvs[gi], dil[gi], f"unpermute_dkv_b{gi}", into=dkv3, slot=gi, nslots=NB)
            dhkv = mm_nt(dkv3, w_kv4, 0, tn=kv_tn, tk=D, out_dtype=F32, name="kv_proj_bwd", gmap=kv_gmap)
            dw_kv = mm_tn(kv_state["hkv"], dkv3, None, 0, w_kv4.shape, tn=kv_tn, tk=D, name="kv_proj_dw", gmap=kv_gmap)
            dx, s_kv = norm_mod_bwd(dhkv, kv_state["x"], dx, vec(kv_norm_g), kvmod[1], "norm_mod_bwd")
    grad_x = dx.reshape(1, S, D)

    smalls = [jnp.stack(dmods), jnp.stack([s_kv[0], s_kv[1]]), jnp.stack(d_n1), jnp.stack(d_n2), s_kv[2], jnp.stack(d_convb), d_final_g,
              jnp.stack(d_scale), jnp.stack(d_convw)]
    small_shapes = [s.shape for s in smalls]
    spk, soffs = _pack(smalls)
    srows = spk.shape[0]
    sgot = all_gather8(spk, "gather_small_grads").reshape(N_DEV, srows, PACK_W)
    ssum = sum_rows8(sgot, "sum_small_grads")
    g_mods, g_kvmod, g_n1, g_n2, g_kvn, g_convb, g_fg, g_scale_full, g_convw_full = _unpack(ssum, soffs, small_shapes)
    g_ada_b = g_mods.reshape(depth, 6 * D)
    g_kv_ada_b = g_kvmod.reshape(2 * D)
    g_scale = lax.dynamic_slice(g_scale_full, (0, p_me * Dq), (n_pool, Dq))
    g_convw = lax.dynamic_slice(g_convw_full, (0, 0, p_me * Fs), (depth, 3, Fs))

    small_w = [ada_b, norm1_g, norm2_g, kv_norm_g, kv_ada_b, ffn_conv_b, final_g, pool_scale, ffn_conv_w]
    small_m = [m_ada_b, m_norm1_g, m_norm2_g, m_kv_norm_g, m_kv_ada_b, m_ffn_conv_b, m_final_g, m_pool_scale, m_ffn_conv_w]
    small_v = [v_ada_b, v_norm1_g, v_norm2_g, v_kv_norm_g, v_kv_ada_b, v_ffn_conv_b, v_final_g, v_pool_scale, v_ffn_conv_w]
    small_g = [g_ada_b, g_n1, g_n2, g_kvn, g_kv_ada_b, g_convb, g_fg, g_scale, g_convw]
    sw_shapes = [w.shape for w in small_w]
    pw, woffs = _pack(small_w)
    s_res = adamw(pw, _pack(small_g)[0], _pack(small_m)[0], _pack(small_v)[0], "adamw_small")
    s_g, s_dl, s_m, s_v = [_unpack(r, woffs, sw_shapes) for r in s_res]

    per_dev = sgot.reshape(N_DEV, -1)
    dm_all = per_dev[:, int(soffs[0]):int(soffs[1])].reshape(N_DEV, depth, 6 * D)
    dkvm_all = per_dev[:, int(soffs[1]):int(soffs[2])].reshape(N_DEV, 1, 2 * D)

    def shard_cols(a, ns):
        sl = lax.dynamic_slice_in_dim(a, p_me * ns, ns, axis=2).transpose(1, 0, 2)
        return jnp.concatenate([sl, jnp.zeros_like(sl)], axis=1)

    ada_res = ada_grad_adamw(c16, shard_cols(dm_all, ada_ns), ada_w, m_ada_w, v_ada_w, "ada_grad_adamw")
    kvada_res = ada_grad_adamw(c16, shard_cols(dkvm_all, kvada_ns), kv_ada_w.reshape(1, D, kvada_ns), m_kv_ada_w.reshape(1, D, kvada_ns),
                               v_kv_ada_w.reshape(1, D, kvada_ns), "kv_ada_grad_adamw")
    kvada_res = [r.reshape(D, kvada_ns) for r in kvada_res]

    d_grp5 = jnp.stack(d_grp).astype(BF16).reshape(n_pool, G, N_CHIPS, -1, D // G).transpose(0, 2, 1, 3, 4)
    parts = [dw_in.reshape(n_pool, N_CHIPS, Dq, D), d_grp5.reshape(n_pool, N_CHIPS, -1, D // G), dw_out.reshape(n_pool, N_CHIPS, Dq, D),
             dw_kv.reshape(N_CHIPS, 2, D // 2, -1), dw_q, dw_o.reshape(n_attn, N_CHIPS, -1, D), dw_up,
             dw_down.reshape(depth, N_CHIPS, Fs, D)]
    recvs = exchange_grads(parts, chip_major, "exchange_grads")
    halves = [sum_partials(p, r, cm, pos, "sum_partials") for p, r, cm in zip(parts, recvs, chip_major)]
    fulls = swap_halves(halves, "swap_halves")
    big_m = [m_pool_w_in, m_pool_w_grp, m_pool_w_out, m_w_kv, m_attn_w_q, m_attn_w_o, m_ffn_w_up, m_ffn_w_down]
    big_v = [v_pool_w_in, v_pool_w_grp, v_pool_w_out, v_w_kv, v_attn_w_q, v_attn_w_o, v_ffn_w_up, v_ffn_w_down]
    big_w = [pool_w_in, pool_w_grp, pool_w_out, w_kv, attn_w_q, attn_w_o, ffn_w_up, ffn_w_down]
    big_res = []
    for w, gfull, m_, v_ in zip(big_w, fulls, big_m, big_v):
        cols = gfull.shape[-1]
        res = adamw(w.reshape(-1, cols), gfull.reshape(-1, cols), m_.reshape(-1, cols), v_.reshape(-1, cols), "adamw_big")
        big_res.append([r.reshape(w.shape) for r in res])

    order = ["ada_w", "ada_b", "norm1_g", "norm2_g", "pool_w_in", "pool_w_grp", "pool_scale", "pool_w_out", "kv_norm_g", "kv_ada_w",
             "kv_ada_b", "w_kv", "attn_w_q", "attn_w_o", "ffn_w_up", "ffn_conv_w", "ffn_conv_b", "ffn_w_down", "final_g"]
    small_names = ["ada_b", "norm1_g", "norm2_g", "kv_norm_g", "kv_ada_b", "ffn_conv_b", "final_g", "pool_scale", "ffn_conv_w"]
    big_names = ["pool_w_in", "pool_w_grp", "pool_w_out", "w_kv", "attn_w_q", "attn_w_o", "ffn_w_up", "ffn_w_down"]
    results = {"ada_w": ada_res, "kv_ada_w": kvada_res}
    for i, nm in enumerate(small_names):
        results[nm] = [s_g[i], s_dl[i], s_m[i], s_v[i]]
    for i, nm in enumerate(big_names):
        results[nm] = big_res[i]
    outs = [loss, grad_x]
    for kind in range(4):
        outs += [results[nm][kind] for nm in order]
    return tuple(outs)
```

```python
import functools
import math

import numpy as np
import jax
import jax.numpy as jnp
from jax import lax
from jax.experimental import pallas as pl
from jax.experimental.pallas import tpu as pltpu

F32 = jnp.float32
BF16 = jnp.bfloat16
MESH = pl.DeviceIdType.MESH

POOL_WINDOWS = (2, 4, 8, 16)
BRANCHES = ((128, 1), (512, 4), (2048, 16))
HEAD_DIM = 64
ATTN_BLOCK = 128
EPS = 1e-6
LR, B1, B2, ADAM_EPS, WD, STEP = 0.001, 0.9, 0.999, 1e-08, 0.01, 10

VMEM_LIMIT_BYTES = 56 * 1024 * 1024
LANES = 128
PACK_W = 1024
HALO = 16
NEG = -1e30
N_CHIPS = 4
N_DEV = 8


def _alibi_slopes(n):
    def pow2(m):
        start = 2.0 ** (-(2.0 ** -(math.log2(m) - 3)))
        return [start ** (i + 1) for i in range(m)]
    if math.log2(n).is_integer():
        s = pow2(n)
    else:
        c = 2 ** math.floor(math.log2(n))
        s = pow2(c) + pow2(2 * c)[0::2][: n - c]
    s = np.asarray(s, dtype=np.float32)
    return -np.sort(-s)


def _cparams(sem=None):
    return pltpu.CompilerParams(dimension_semantics=sem, vmem_limit_bytes=VMEM_LIMIT_BYTES)


def _tile(n, pref, unit):
    t = (min(pref, n) // unit) * unit
    while t >= unit:
        if n % t == 0:
            return t
        t -= unit
    return n


def _sigmoid(v):
    return 1.0 / (1.0 + jnp.exp(-v))


def all_gather8(xs, name):
    m_per, n = xs.shape

    def body(x_ref, out_ref, send_sems, recv_sems, local_sem):
        x, y, c = lax.axis_index("x"), lax.axis_index("y"), lax.axis_index("c")
        me, sibling = (x, y, c), (x, y, 1 - c)
        chips = [(1 - x, y), (x, 1 - y), (1 - x, 1 - y)]

        def rows(px, py, pc):
            return out_ref.at[pl.ds((4 * px + 2 * py + pc) * m_per, m_per), :]

        def copy(k, block, to, src=None):
            return pltpu.make_async_remote_copy(src_ref=rows(*block) if src is None else src, dst_ref=rows(*block),
                                                send_sem=send_sems.at[k], recv_sem=recv_sems.at[k], device_id=to, device_id_type=MESH)

        mine = pltpu.make_async_copy(x_ref, rows(*me), local_sem)
        mine.start()
        first = [copy(0, me, sibling, src=x_ref)]
        first += [copy(1 + j, me, (*chip, c), src=x_ref) for j, chip in enumerate(chips)]
        for cp in first:
            cp.start()
        passed = [copy(4 + j, (*chip, c), sibling) for j, chip in enumerate(chips)]
        for j, chip in enumerate(chips):
            copy(1 + j, (*chip, c), me).wait_recv()
            passed[j].start()
        copy(0, sibling, me).wait_recv()
        for j, chip in enumerate(chips):
            copy(4 + j, (*chip, 1 - c), me).wait_recv()
        for cp in first + passed:
            cp.wait_send()
        mine.wait()

    return pl.pallas_call(
        body, name=name,
        out_shape=jax.ShapeDtypeStruct((N_DEV * m_per, n), xs.dtype),
        in_specs=[pl.BlockSpec(memory_space=pltpu.VMEM)],
        out_specs=pl.BlockSpec(memory_space=pltpu.VMEM),
        scratch_shapes=[pltpu.SemaphoreType.DMA((7,)), pltpu.SemaphoreType.DMA((7,)), pltpu.SemaphoreType.DMA],
        compiler_params=pltpu.CompilerParams(vmem_limit_bytes=VMEM_LIMIT_BYTES),
    )(xs)


HBM_SPEC = pl.BlockSpec(memory_space=pltpu.HBM)


def _mesh_pos():
    x, y, c = lax.axis_index("x"), lax.axis_index("y"), lax.axis_index("c")
    return x, y, c, [(1 - x, y), (x, 1 - y), (1 - x, 1 - y)]


class _Phase:
    def __init__(self, key, group, after, est_us, get_ins, get_inouts, new_outs, n_sems, n_local, build, on_done):
        self.key, self.group, self.after, self.est_us = key, group, after, est_us
        self.get_ins, self.get_inouts, self.new_outs = get_ins, get_inouts, new_outs
        self.n_sems, self.n_local, self.build, self.on_done = n_sems, n_local, build, on_done


def _rcopy(src, dst, send_sems, recv_sems, k, to):
    return pltpu.make_async_remote_copy(src_ref=src, dst_ref=dst, send_sem=send_sems.at[k], recv_sem=recv_sems.at[k],
                                        device_id=to, device_id_type=MESH)


def _build_fetch(in_refs, out_refs, send_sems, recv_sems, loc_sems, sem0, loc0):
    (shard,), (g,) = in_refs, out_refs
    x, y, c, chips = _mesh_pos()
    p_me = 2 * x + y
    locs = [pltpu.make_async_copy(shard.at[i], g.at[p_me, i], loc_sems.at[loc0 + i]) for i in range(2)]
    sends = [_rcopy(shard.at[c], g.at[p_me, c], send_sems, recv_sems, sem0 + j, (*chip, c)) for j, chip in enumerate(chips)]

    def recvs():
        blks = [g.at[2 * chip[0] + chip[1], c] for chip in chips]
        return [_rcopy(blk, blk, send_sems, recv_sems, sem0 + j, (*chip, c)) for j, (blk, chip) in enumerate(zip(blks, chips))]
    return sends, recvs, locs


def _build_pass(in_refs, out_refs, send_sems, recv_sems, loc_sems, sem0, loc0):
    (g,) = out_refs
    x, y, c, chips = _mesh_pos()
    sib = (x, y, 1 - c)
    slots = [2 * chip[0] + chip[1] for chip in chips]
    sends = [_rcopy(g.at[p, c], g.at[p, c], send_sems, recv_sems, sem0 + j, sib) for j, p in enumerate(slots)]

    def recvs():
        return [_rcopy(g.at[p, 1 - c], g.at[p, 1 - c], send_sems, recv_sems, sem0 + j, sib) for j, p in enumerate(slots)]
    return sends, recvs, []


def _build_exchange(in_refs, out_refs, send_sems, recv_sems, loc_sems, sem0, loc0):
    (dw,), (recv,) = in_refs, out_refs
    x, y, c, chips = _mesh_pos()
    targets = [(c, chip, c, j) for j, chip in enumerate(chips)]
    targets += [(1 - c, chip, 1 - c, 3 + j) for j, chip in enumerate([(x, y)] + chips)]
    sends = [_rcopy(dw.at[2 * chip[0] + chip[1], half], recv.at[rel], send_sems, recv_sems, sem0 + rel, (*chip, core))
             for half, chip, core, rel in targets]

    def recvs():
        return [_rcopy(recv.at[rel], recv.at[rel], send_sems, recv_sems, sem0 + rel, (x, y, 1 - c)) for rel in range(7)]
    return sends, recvs, []


def _build_swap(in_refs, out_refs, send_sems, recv_sems, loc_sems, sem0, loc0):
    (g,) = out_refs
    x, y, c, _ = _mesh_pos()
    sib = (x, y, 1 - c)
    sends = [_rcopy(g.at[l, c], g.at[l, c], send_sems, recv_sems, sem0 + l, sib) for l in range(g.shape[0])]

    def recvs():
        return [_rcopy(g.at[l, 1 - c], g.at[l, 1 - c], send_sems, recv_sems, sem0 + l, sib) for l in range(g.shape[0])]
    return sends, recvs, []


def _plan_refs(phases):
    xin, xout, alias, n_sems, n_loc, out_of = [], [], {}, 0, 0, {}
    for ph in phases:
        ph.sem0, ph.loc0 = n_sems, n_loc
        n_sems += ph.n_sems
        n_loc += ph.n_local
        if ph.after in out_of:
            ph.in0, ph.n_in, ph.out0, ph.n_out = 0, 0, out_of[ph.after], 1
            continue
        ins, inouts = ph.get_ins(), ph.get_inouts()
        ph.in0, ph.n_in = len(xin), len(ins)
        xin += ins
        ph.out0, ph.n_out = len(xout), len(inouts) + len(ph.new_outs)
        out_of[ph.key] = ph.out0
        for a in inouts:
            alias[len(xin)] = len(xout)
            xin.append(a)
            xout.append(jax.ShapeDtypeStruct(a.shape, a.dtype))
        xout += ph.new_outs
    return xin, xout, alias, max(n_sems, 1), max(n_loc, 1)


def _built(ph, xin_refs, xout_refs, sems):
    return ph.build(xin_refs[ph.in0:ph.in0 + ph.n_in], xout_refs[ph.out0:ph.out0 + ph.n_out], sems[0], sems[1], sems[2], ph.sem0, ph.loc0)


def _start(phases, xin_refs, xout_refs, sems):
    for ph in phases:
        sends, _, locs = _built(ph, xin_refs, xout_refs, sems)
        for cp in locs + sends:
            cp.start()


def _finish(phases, xin_refs, xout_refs, sems):
    for ph in phases:
        sends, recvs, locs = _built(ph, xin_refs, xout_refs, sems)
        for cp in recvs():
            cp.wait_recv()
        for cp in sends:
            cp.wait_send()
        for cp in locs:
            cp.wait()


class _Comm:
    def __init__(self):
        self.queue, self.store, self.n_alone = [], {}, 0

    def push(self, ph):
        self.queue.append(ph)

    def take(self, carry_us):
        taken, t = [], 0.0
        while True:
            pending = {ph.key for ph in self.queue} | {ph.key for ph in taken}
            room = 2.0 * carry_us if not taken else 0.8 * carry_us - t
            fits = [ph for ph in self.queue if ph.after not in pending and ph.est_us <= room]
            if not fits:
                return taken
            ph = max(fits, key=lambda p: p.est_us)
            self.queue.remove(ph)
            taken.append(ph)
            t += ph.est_us

    def require(self, group):
        phases = [ph for ph in self.queue if ph.group == group]
        if phases:
            self.queue = [ph for ph in self.queue if ph.group != group]
            self.run_alone(phases)
        return self.store[group]

    def flush(self):
        phases, self.queue = self.queue, []
        if phases:
            self.run_alone(phases)

    def run_alone(self, phases):
        phases = [ph for ph in phases if ph.after is None] + [ph for ph in phases if ph.after is not None]
        groups, keys = [[]], set()
        for ph in phases:
            if ph.after in keys:
                groups.append([])
                keys = set()
            groups[-1].append(ph)
            keys.add(ph.key)
        xin, xout, alias, n_sems, n_loc = _plan_refs(phases)
        n_xin, n_xout = len(xin), len(xout)

        def body(*refs):
            xin_refs, xout_refs, sems = refs[:n_xin], refs[n_xin:n_xin + n_xout], refs[n_xin + n_xout:]
            for grp in groups:
                _start(grp, xin_refs, xout_refs, sems)
                _finish(grp, xin_refs, xout_refs, sems)

        self.n_alone += 1
        outs = pl.pallas_call(
            body, name=f"comm_alone_{self.n_alone}", out_shape=xout, in_specs=[HBM_SPEC] * n_xin, out_specs=[HBM_SPEC] * n_xout,
            input_output_aliases=alias,
            scratch_shapes=[pltpu.SemaphoreType.DMA((n_sems,)), pltpu.SemaphoreType.DMA((n_sems,)), pltpu.SemaphoreType.DMA((n_loc,))],
        )(*xin)
        for ph in phases:
            ph.on_done(outs[ph.out0:ph.out0 + ph.n_out])


def _pcall(body, *, name, grid, in_specs, out_specs, out_shape, args, scratch_shapes=(), aliases=None, comm=None, carry_us=0.0):
    phases = comm.take(carry_us) if comm is not None else []
    n_in, n_out, n_scr = len(in_specs), len(out_specs), len(scratch_shapes)
    if not phases:
        return pl.pallas_call(body, name=name, grid=grid, in_specs=in_specs, out_specs=out_specs, out_shape=out_shape,
                              scratch_shapes=list(scratch_shapes), input_output_aliases=aliases or {},
                              compiler_params=_cparams(("arbitrary",) * len(grid)))(*args)
    xin, xout, xalias, n_sems, n_loc = _plan_refs(phases)
    n_xin, n_xout = len(xin), len(xout)
    all_alias = dict(aliases or {})
    all_alias.update({n_in + i: n_out + o for i, o in xalias.items()})

    def carrier(*refs):
        ins, xin_refs = refs[:n_in], refs[n_in:n_in + n_xin]
        outs = refs[n_in + n_xin:n_in + n_xin + n_out]
        xout_refs = refs[n_in + n_xin + n_out:n_in + n_xin + n_out + n_xout]
        rest = refs[n_in + n_xin + n_out + n_xout:]
        scr, sems = rest[:n_scr], rest[n_scr:]
        pids = [pl.program_id(k) for k in range(len(grid))]
        first = functools.reduce(jnp.logical_and, [p == 0 for p in pids])
        last = functools.reduce(jnp.logical_and, [p == n - 1 for p, n in zip(pids, grid)])

        @pl.when(first)
        def _():
            _start(phases, xin_refs, xout_refs, sems)
        body(*ins, *outs, *scr)

        @pl.when(last)
        def _():
            _finish(phases, xin_refs, xout_refs, sems)

    outs = pl.pallas_call(
        carrier, name=name, grid=grid, in_specs=list(in_specs) + [HBM_SPEC] * n_xin, out_specs=list(out_specs) + [HBM_SPEC] * n_xout,
        out_shape=list(out_shape) + xout,
        scratch_shapes=list(scratch_shapes) + [pltpu.SemaphoreType.DMA((n_sems,)), pltpu.SemaphoreType.DMA((n_sems,)),
                                               pltpu.SemaphoreType.DMA((n_loc,))],
        input_output_aliases=all_alias, compiler_params=_cparams(("arbitrary",) * len(grid)))(*args, *xin)
    for ph in phases:
        ph.on_done(outs[n_out + ph.out0:n_out + ph.out0 + ph.n_out])
    return outs[:n_out]


FETCH_US_PER_MB = 20.4
PASS_US_PER_MB = 3.3
EXCHANGE_US_PER_MB = 14.5


def push_gather(comm, keys_shards):
    prev = None
    for key, shard in keys_shards:
        r, c = shard.shape
        sh = shard.reshape(2, r // 2, c)
        half_mb = r // 2 * c * 2 / 1e6

        def done(outs, key=key):
            comm.store[key] = outs[0]

        comm.push(_Phase(("fetch", key), key, None, 3 * half_mb * FETCH_US_PER_MB, lambda sh=sh: [sh], lambda: [],
                         [jax.ShapeDtypeStruct((N_CHIPS, 2, r // 2, c), BF16)], 3, 2, _build_fetch, done))
        if prev is not None:
            comm.push(prev)
        prev = _Phase(("pass", key), key, ("fetch", key), 3 * half_mb * PASS_US_PER_MB + 3.0, lambda: [], lambda key=key: [comm.store[key]],
                      [], 3, 0, _build_pass, done)
    if prev is not None:
        comm.push(prev)


def push_exchange(comm, key, dw):
    _, r, c = dw.shape
    half_mb = r // 2 * c * 2 / 1e6

    def done(outs):
        comm.store[key] = outs[0]

    comm.push(_Phase(("exchange", key), key, None, 6 * half_mb * EXCHANGE_US_PER_MB, lambda: [dw.reshape(N_CHIPS, 2, r // 2, c)], lambda: [],
                     [jax.ShapeDtypeStruct((7, r // 2, c), BF16)], 7, 0, _build_exchange, done))


MM_FLOPS_PER_US = 6.0e8


def mm_nn(a, w3, *, tn, out_dtype, name, ncb=None, cbmap=None, res=None, perm_d=1, comm=None):
    M, K = a.shape
    P, _, Ns = w3.shape
    nper = Ns // tn
    ncb = P * nper if ncb is None else ncb
    tm = ATTN_BLOCK * perm_d if perm_d > 1 else _tile(M, 512, 8)
    cbm = cbmap if cbmap is not None else (lambda j: j)
    nch = tn // LANES

    def body(*refs):
        if res is None:
            a_ref, w_ref, o_ref = refs[:3]
        else:
            a_ref, w_ref, x_ref, g_ref, o_ref, xo_ref = refs
        acc = jnp.dot(a_ref[...].astype(BF16), w_ref[...], preferred_element_type=F32)
        if perm_d > 1:
            scr = refs[3]
            for cj in range(nch):
                scr[cj] = acc[:, cj * LANES:(cj + 1) * LANES]
            for r in range(perm_d):
                for cj in range(nch):
                    o_ref[r, :, cj * LANES:(cj + 1) * LANES] = scr.at[cj][pl.ds(r, ATTN_BLOCK, stride=perm_d), :].astype(o_ref.dtype)
        else:
            o_ref[...] = acc.astype(o_ref.dtype)
        if res is not None:
            xo_ref[...] = x_ref[...] + g_ref[...] * acc

    in_specs = [pl.BlockSpec((tm, K), lambda i, j: (i, 0)),
                pl.BlockSpec((None, K, tn), lambda i, j: (cbm(j) // nper, 0, cbm(j) % nper))]
    scratch = []
    if perm_d > 1:
        out_specs = [pl.BlockSpec((perm_d, ATTN_BLOCK, tn), lambda i, j: (0, i, j))]
        out_shape = [jax.ShapeDtypeStruct((perm_d, M // perm_d, ncb * tn), out_dtype)]
        scratch = [pltpu.VMEM((nch, tm, LANES), F32)]
    else:
        out_specs = [pl.BlockSpec((tm, tn), lambda i, j: (i, j))]
        out_shape = [jax.ShapeDtypeStruct((M, ncb * tn), out_dtype)]
    args = [a, w3]
    if res is not None:
        in_specs += [pl.BlockSpec((tm, tn), lambda i, j: (i, j)), pl.BlockSpec((1, tn), lambda i, j: (0, j))]
        out_specs.append(pl.BlockSpec((tm, tn), lambda i, j: (i, j)))
        out_shape.append(jax.ShapeDtypeStruct((M, ncb * tn), F32))
        args += [res[0], res[1]]
    outs = _pcall(body, name=name, grid=(M // tm, ncb), in_specs=in_specs, out_specs=out_specs, out_shape=out_shape, args=args,
                  scratch_shapes=scratch, comm=comm, carry_us=2.0 * M * K * ncb * tn / MM_FLOPS_PER_US)
    if perm_d > 1:
        return outs[0].reshape(M, ncb * tn)
    return outs[0] if res is None else (outs[0], outs[1])


def permute_rows(x, d, name):
    S, C = x.shape
    R = ATTN_BLOCK * d
    ct = _tile(C, 256, LANES)
    nch = ct // LANES

    def body(x_ref, o_ref, scr):
        xv = x_ref[...].astype(F32)
        for cj in range(nch):
            scr[cj] = xv[:, cj * LANES:(cj + 1) * LANES]
        for r in range(d):
            for cj in range(nch):
                o_ref[r, :, cj * LANES:(cj + 1) * LANES] = scr.at[cj][pl.ds(r, ATTN_BLOCK, stride=d), :].astype(o_ref.dtype)

    out = pl.pallas_call(body, name=name, grid=(S // R, C // ct), in_specs=[pl.BlockSpec((R, ct), lambda i, j: (i, j))],
                         out_specs=pl.BlockSpec((d, ATTN_BLOCK, ct), lambda i, j: (0, i, j)),
                         out_shape=jax.ShapeDtypeStruct((d, S // d, C), x.dtype), scratch_shapes=[pltpu.VMEM((nch, R, LANES), F32)],
                         compiler_params=_cparams(("parallel", "parallel")))(x)
    return out.reshape(S, C)


def unpermute_rows(p, d, name, into=None, total_cols=None, colmap=None):
    S, C = p.shape
    rpb = max(ATTN_BLOCK, 512 // d)
    R = rpb * d
    ct = _tile(C, 256, LANES)
    nch = ct // LANES
    total_cols = C if total_cols is None else total_cols
    cm = colmap if colmap is not None else (lambda j: j)

    def body(*refs):
        p_ref, o_ref, scr = refs[0], refs[-2], refs[-1]
        if d == 1:
            o_ref[...] = p_ref[0]
            return
        for r in range(d):
            for cj in range(nch):
                scr.at[cj][pl.ds(r, rpb, stride=d), :] = p_ref[r, :, cj * LANES:(cj + 1) * LANES].astype(F32)
        for cj in range(nch):
            o_ref[:, cj * LANES:(cj + 1) * LANES] = scr[cj].astype(o_ref.dtype)

    in_specs = [pl.BlockSpec((d, rpb, ct), lambda i, j: (0, i, j))]
    args = [p.reshape(d, S // d, C)]
    aliases = {}
    if into is not None:
        in_specs.append(pl.BlockSpec(memory_space=pl.ANY))
        args.append(into)
        aliases = {1: 0}
    return pl.pallas_call(body, name=name, grid=(S // R, C // ct), in_specs=in_specs,
                          out_specs=pl.BlockSpec((R, ct), lambda i, j: (i, cm(j))),
                          out_shape=jax.ShapeDtypeStruct((S, total_cols), p.dtype), scratch_shapes=[pltpu.VMEM((nch, R, LANES), F32)],
                          input_output_aliases=aliases, compiler_params=_cparams(("parallel", "parallel")))(*args)


def mm_nt(g3, w3, *, tn, tk, out_dtype, name, gmap=None, comm=None):
    _, M, _ = g3.shape
    P, K, Ns = w3.shape
    nper = Ns // tn
    ns = P * nper
    tm = _tile(M, 512, 8)
    gm = gmap if gmap is not None else (lambda s: (0, s))

    def body(g_ref, w_ref, o_ref, acc):
        s = pl.program_id(2)

        @pl.when(s == 0)
        def _():
            acc[...] = jnp.zeros_like(acc)
        acc[...] += lax.dot_general(g_ref[...].astype(BF16), w_ref[...], (((1,), (1,)), ((), ())), preferred_element_type=F32)

        @pl.when(s == ns - 1)
        def _():
            o_ref[...] = acc[...].astype(o_ref.dtype)

    return _pcall(
        body, name=name, grid=(M // tm, K // tk, ns),
        in_specs=[pl.BlockSpec((None, tm, tn), lambda i, kj, s: (gm(s)[0], i, gm(s)[1])),
                  pl.BlockSpec((None, tk, tn), lambda i, kj, s: (s // nper, kj, s % nper))],
        out_specs=[pl.BlockSpec((tm, tk), lambda i, kj, s: (i, kj))],
        out_shape=[jax.ShapeDtypeStruct((M, K), out_dtype)], args=[g3, w3],
        scratch_shapes=[pltpu.VMEM((tm, tk), F32)], comm=comm, carry_us=2.0 * M * K * P * Ns / MM_FLOPS_PER_US)[0]


def mm_tn(a, g3, wshape, *, tn, tk, name, gmap=None, comm=None):
    M, K = a.shape
    P, _, Ns = wshape
    nper = Ns // tn
    ns = P * nper
    tm = _tile(M, 512, 16)
    nm = M // tm
    gm = gmap if gmap is not None else (lambda s: (0, s))

    def body(a_ref, g_ref, o_ref, acc):
        mi = pl.program_id(2)

        @pl.when(mi == 0)
        def _():
            acc[...] = jnp.zeros_like(acc)
        acc[...] += lax.dot_general(a_ref[...].astype(BF16), g_ref[...].astype(BF16), (((0,), (0,)), ((), ())), preferred_element_type=F32)

        @pl.when(mi == nm - 1)
        def _():
            o_ref[...] = acc[...].astype(o_ref.dtype)

    return _pcall(
        body, name=name, grid=(ns, K // tk, nm),
        in_specs=[pl.BlockSpec((tm, tk), lambda s, kj, mi: (mi, kj)),
                  pl.BlockSpec((None, tm, tn), lambda s, kj, mi: (gm(s)[0], mi, gm(s)[1]))],
        out_specs=[pl.BlockSpec((None, tk, tn), lambda s, kj, mi: (s // nper, kj, s % nper))],
        out_shape=[jax.ShapeDtypeStruct((P, K, Ns), BF16)], args=[a, g3],
        scratch_shapes=[pltpu.VMEM((tk, tn), F32)], comm=comm, carry_us=2.0 * M * K * P * Ns / MM_FLOPS_PER_US)[0]


def _vspec(d):
    return pl.BlockSpec((1, d), lambda i: (0, 0))


def norm_mod(x, g, sh, sc, name):
    S, D = x.shape
    tm = _tile(S, 512, 16)

    def body(x_ref, g_ref, sh_ref, sc_ref, o_ref):
        xv = x_ref[...]
        r = lax.rsqrt(jnp.mean(xv * xv, axis=-1, keepdims=True) + EPS)
        o_ref[...] = ((xv * r) * g_ref[...] * (1.0 + sc_ref[...]) + sh_ref[...]).astype(o_ref.dtype)

    return pl.pallas_call(body, name=name, grid=(S // tm,),
                          in_specs=[pl.BlockSpec((tm, D), lambda i: (i, 0)), _vspec(D), _vspec(D), _vspec(D)],
                          out_specs=pl.BlockSpec((tm, D), lambda i: (i, 0)), out_shape=jax.ShapeDtypeStruct((S, D), BF16),
                          compiler_params=_cparams(("parallel",)))(x, g, sh, sc)


def norm_mod_bwd(dh, x, dres, g, sc, name):
    S, D = x.shape
    tm = _tile(S, 256, 8)

    def body(dh_ref, x_ref, dr_ref, g_ref, sc_ref, dx_ref, sums_ref):
        xv = x_ref[...]
        dhv = dh_ref[...].astype(F32)
        r = lax.rsqrt(jnp.mean(xv * xv, axis=-1, keepdims=True) + EPS)
        xn = xv * r
        one_sc = 1.0 + sc_ref[...]
        dxn = dhv * g_ref[...] * one_sc
        dx = r * (dxn - xn * jnp.mean(dxn * xn, axis=-1, keepdims=True))
        dx_ref[...] = dx + dr_ref[...]
        part = jnp.concatenate([jnp.sum(dhv, axis=0, keepdims=True), jnp.sum(dhv * xn * g_ref[...], axis=0, keepdims=True),
                                jnp.sum(dhv * one_sc * xn, axis=0, keepdims=True), jnp.zeros((5, D), F32)], axis=0)

        @pl.when(pl.program_id(0) == 0)
        def _():
            sums_ref[...] = jnp.zeros_like(sums_ref)
        sums_ref[...] += part

    row = pl.BlockSpec((tm, D), lambda i: (i, 0))
    return pl.pallas_call(body, name=name, grid=(S // tm,), in_specs=[row, row, row, _vspec(D), _vspec(D)],
                          out_specs=[row, pl.BlockSpec((8, D), lambda i: (0, 0))],
                          out_shape=[jax.ShapeDtypeStruct((S, D), F32), jax.ShapeDtypeStruct((8, D), F32)],
                          compiler_params=_cparams(("arbitrary",)))(dh, x, dres, g, sc)


def gate_bwd(dx, out, gate, name):
    S, D = dx.shape
    tm = _tile(S, 512, 16)

    def body(dx_ref, o_ref, g_ref, do_ref, sums_ref):
        dxv = dx_ref[...]
        do_ref[...] = (g_ref[...] * dxv).astype(do_ref.dtype)
        part = jnp.concatenate([jnp.sum(dxv * o_ref[...].astype(F32), axis=0, keepdims=True), jnp.zeros((7, D), F32)], axis=0)

        @pl.when(pl.program_id(0) == 0)
        def _():
            sums_ref[...] = jnp.zeros_like(sums_ref)
        sums_ref[...] += part

    row = pl.BlockSpec((tm, D), lambda i: (i, 0))
    return pl.pallas_call(body, name=name, grid=(S // tm,), in_specs=[row, row, _vspec(D)],
                          out_specs=[row, pl.BlockSpec((8, D), lambda i: (0, 0))],
                          out_shape=[jax.ShapeDtypeStruct((S, D), BF16), jax.ShapeDtypeStruct((8, D), F32)],
                          compiler_params=_cparams(("arbitrary",)))(dx, out, gate)


def loss_fwd_bwd(x, g, target, name):
    S, D = x.shape
    tm = _tile(S, 256, 8)

    def body(x_ref, g_ref, t_ref, dx_ref, sums_ref):
        xv = x_ref[...]
        r = lax.rsqrt(jnp.mean(xv * xv, axis=-1, keepdims=True) + EPS)
        xn = xv * r
        err = xn * g_ref[...] - t_ref[...]
        dy = err * (1.0 / D)
        dxn = dy * g_ref[...]
        dx_ref[...] = r * (dxn - xn * jnp.mean(dxn * xn, axis=-1, keepdims=True))
        part = jnp.concatenate([jnp.sum(dy * xn, axis=0, keepdims=True), jnp.sum(err * err, axis=0, keepdims=True),
                                jnp.zeros((6, D), F32)], axis=0)

        @pl.when(pl.program_id(0) == 0)
        def _():
            sums_ref[...] = jnp.zeros_like(sums_ref)
        sums_ref[...] += part

    row = pl.BlockSpec((tm, D), lambda i: (i, 0))
    return pl.pallas_call(body, name=name, grid=(S // tm,), in_specs=[row, _vspec(D), row],
                          out_specs=[row, pl.BlockSpec((8, D), lambda i: (0, 0))],
                          out_shape=[jax.ShapeDtypeStruct((S, D), F32), jax.ShapeDtypeStruct((8, D), F32)],
                          compiler_params=_cparams(("arbitrary",)))(x, g, target)


def pool_fwd(u, wgrp, scale, name):
    S, D = u.shape
    G = len(POOL_WINDOWS)
    C = D // G
    tm = _tile(S, 256, 16)
    hb = tm // HALO

    def body(up_ref, uc_ref, w_ref, sc_ref, p_ref, z_ref, y_ref):
        i = pl.program_id(0)
        prev = jnp.where(i > 0, up_ref[...], 0.0)
        ext = jnp.concatenate([prev, uc_ref[...]], axis=0)
        t = i * tm + lax.broadcasted_iota(jnp.int32, (tm, 1), 0)
        for gi, w in enumerate(POOL_WINDOWS):
            cs = slice(gi * C, (gi + 1) * C)
            e = ext[:, cs]
            s, k = e, 1
            while k < w:
                s = s + pltpu.roll(s, k, 0)
                k *= 2
            cnt = jnp.minimum(t + 1, w).astype(F32)
            pooled = (s[HALO:] / cnt - e[HALO:]).astype(BF16)
            p_ref[:, cs] = pooled
            z = jnp.dot(pooled, w_ref[:, gi].reshape(C, C), preferred_element_type=F32)
            z_ref[:, cs] = z.astype(BF16)
            y_ref[:, cs] = (z * sc_ref[:, cs]).astype(BF16)

    row = pl.BlockSpec((tm, D), lambda i: (i, 0))
    return pl.pallas_call(
        body, name=name, grid=(S // tm,),
        in_specs=[pl.BlockSpec((HALO, D), lambda i: (jnp.maximum(i * hb - 1, 0), 0)), row,
                  pl.BlockSpec(wgrp.shape, lambda i: (0, 0, 0, 0)), _vspec(D)],
        out_specs=[row, row, row], out_shape=[jax.ShapeDtypeStruct((S, D), BF16)] * 3,
        compiler_params=_cparams(("parallel",)))(u, u, wgrp, scale)


def pool_bwd(dys, z, pooled, wgrp, scale, name):
    S, D = dys.shape
    G = len(POOL_WINDOWS)
    C = D // G
    tm = _tile(S, 256, 16)
    hb = tm // HALO
    nt = S // tm
    n_ext = tm + HALO

    def body(dc_ref, dn_ref, z_ref, p_ref, w_ref, sc_ref, du_ref, dw_ref, sums_ref):
        i = pl.program_id(0)

        @pl.when(i == 0)
        def _():
            dw_ref[...] = jnp.zeros_like(dw_ref)
            sums_ref[...] = jnp.zeros_like(sums_ref)
        dyc = dc_ref[...].astype(F32)
        nxt = jnp.where(i < nt - 1, dn_ref[...].astype(F32), 0.0)
        ext = jnp.concatenate([dyc, nxt], axis=0)
        sums_ref[...] += jnp.concatenate([jnp.sum(dyc * z_ref[...].astype(F32), axis=0, keepdims=True), jnp.zeros((7, D), F32)], axis=0)
        t = i * tm + lax.broadcasted_iota(jnp.int32, (n_ext, 1), 0)
        for gi, w in enumerate(POOL_WINDOWS):
            cs = slice(gi * C, (gi + 1) * C)
            wg = w_ref[:, gi].reshape(C, C)
            dz = (ext[:, cs] * sc_ref[:, cs]).astype(BF16)
            dpool = lax.dot_general(dz, wg, (((1,), (1,)), ((), ())), preferred_element_type=F32)
            dw_ref[gi] += lax.dot_general(p_ref[:, cs], dz[:tm], (((0,), (0,)), ((), ())), preferred_element_type=F32)
            cnt = jnp.minimum(t + 1, w).astype(F32)
            s, k = dpool / cnt, 1
            while k < w:
                s = s + pltpu.roll(s, n_ext - k, 0)
                k *= 2
            du_ref[:, cs] = (s[:tm] - dpool[:tm]).astype(BF16)

    row = pl.BlockSpec((tm, D), lambda i: (i, 0))
    return pl.pallas_call(
        body, name=name, grid=(nt,),
        in_specs=[row, pl.BlockSpec((HALO, D), lambda i: (jnp.minimum((i + 1) * hb, S // HALO - 1), 0)), row, row,
                  pl.BlockSpec(wgrp.shape, lambda i: (0, 0, 0, 0)), _vspec(D)],
        out_specs=[row, pl.BlockSpec((G, C, C), lambda i: (0, 0, 0)), pl.BlockSpec((8, D), lambda i: (0, 0))],
        out_shape=[jax.ShapeDtypeStruct((S, D), BF16), jax.ShapeDtypeStruct((G, C, C), F32), jax.ShapeDtypeStruct((8, D), F32)],
        compiler_params=_cparams(("arbitrary",)))(dys, dys, z, pooled, wgrp, scale)


FFN_ACT_US_PER_ELEM = (48.0 / (4096 * 2816), 84.0 / (4096 * 2816))


def ffn_act(hu, conv_w, conv_b, name, comm=None):
    S, F2 = hu.shape
    F = F2 // 2
    tm = _tile(S, 256, 16)
    tn = _tile(F, 1408, LANES)
    nb = F // tn
    hb = tm // HALO

    def body(ap_ref, ac_ref, v_ref, w_ref, b_ref, o_ref):
        i = pl.program_id(0)
        prev = jnp.where(i > 0, ap_ref[...].astype(F32), 0.0)
        ext = jnp.concatenate([prev, ac_ref[...].astype(F32)], axis=0)
        conv = b_ref[...] + pltpu.roll(ext, 2, 0) * w_ref[0:1, :] + pltpu.roll(ext, 1, 0) * w_ref[1:2, :] + ext * w_ref[2:3, :]
        conv = conv[HALO:]
        o_ref[...] = (conv * _sigmoid(conv) * v_ref[...].astype(F32)).astype(o_ref.dtype)

    return _pcall(
        body, name=name, grid=(S // tm, nb),
        in_specs=[pl.BlockSpec((HALO, tn), lambda i, j: (jnp.maximum(i * hb - 1, 0), j)), pl.BlockSpec((tm, tn), lambda i, j: (i, j)),
                  pl.BlockSpec((tm, tn), lambda i, j: (i, j + nb)), pl.BlockSpec((3, tn), lambda i, j: (0, j)),
                  pl.BlockSpec((1, tn), lambda i, j: (0, j))],
        out_specs=[pl.BlockSpec((tm, tn), lambda i, j: (i, j))], out_shape=[jax.ShapeDtypeStruct((S, F), BF16)],
        args=[hu, hu, hu, conv_w, conv_b], comm=comm, carry_us=FFN_ACT_US_PER_ELEM[0] * S * F)[0]


def ffn_act_bwd(dg, hu, conv_w, conv_b, name, comm=None):
    S, F = dg.shape
    tm = _tile(S, 256, 16)
    tn = _tile(F, 1408, LANES)
    nb = F // tn
    hb = tm // HALO
    nt = S // tm
    n_ext = tm + 2 * HALO

    def body(gc_ref, gn_ref, ap_ref, ac_ref, an_ref, vc_ref, vn_ref, w_ref, b_ref, o_ref, sums_ref):
        i = pl.program_id(1)

        @pl.when(i == 0)
        def _():
            sums_ref[...] = jnp.zeros_like(sums_ref)
        zeros = jnp.zeros((HALO, tn), F32)
        not_last = i < nt - 1
        a_ext = jnp.concatenate([jnp.where(i > 0, ap_ref[...].astype(F32), 0.0), ac_ref[...].astype(F32), an_ref[...].astype(F32)], axis=0)
        v_ext = jnp.concatenate([zeros, vc_ref[...].astype(F32), vn_ref[...].astype(F32)], axis=0)
        g_ext = jnp.concatenate([zeros, gc_ref[...].astype(F32), jnp.where(not_last, gn_ref[...].astype(F32), 0.0)], axis=0)
        w0, w1, w2 = w_ref[0:1, :], w_ref[1:2, :], w_ref[2:3, :]
        a_m2, a_m1 = pltpu.roll(a_ext, 2, 0), pltpu.roll(a_ext, 1, 0)
        conv = b_ref[...] + a_m2 * w0 + a_m1 * w1 + a_ext * w2
        sig = _sigmoid(conv)
        silu = conv * sig
        dsilu = sig * (1.0 + conv * (1.0 - sig))
        dconv = g_ext * v_ext * dsilu
        da = dconv * w2 + pltpu.roll(dconv, n_ext - 1, 0) * w1 + pltpu.roll(dconv, n_ext - 2, 0) * w0
        cur = slice(HALO, HALO + tm)
        o_ref[0] = da[cur].astype(o_ref.dtype)
        o_ref[1] = (g_ext * silu)[cur].astype(o_ref.dtype)
        dc = dconv[cur]
        part = jnp.concatenate([jnp.sum(dc * a_m2[cur], axis=0, keepdims=True), jnp.sum(dc * a_m1[cur], axis=0, keepdims=True),
                                jnp.sum(dc * a_ext[cur], axis=0, keepdims=True), jnp.sum(dc, axis=0, keepdims=True),
                                jnp.zeros((4, tn), F32)], axis=0)
        sums_ref[...] += part

    def prev(i):
        return jnp.maximum(i * hb - 1, 0)

    def nxt(i):
        return jnp.minimum((i + 1) * hb, S // HALO - 1)

    return _pcall(
        body, name=name, grid=(nb, nt),
        in_specs=[pl.BlockSpec((tm, tn), lambda j, i: (i, j)), pl.BlockSpec((HALO, tn), lambda j, i: (nxt(i), j)),
                  pl.BlockSpec((HALO, tn), lambda j, i: (prev(i), j)), pl.BlockSpec((tm, tn), lambda j, i: (i, j)),
                  pl.BlockSpec((HALO, tn), lambda j, i: (nxt(i), j)),
                  pl.BlockSpec((tm, tn), lambda j, i: (i, j + nb)), pl.BlockSpec((HALO, tn), lambda j, i: (nxt(i), j + nb)),
                  pl.BlockSpec((3, tn), lambda j, i: (0, j)), pl.BlockSpec((1, tn), lambda j, i: (0, j))],
        out_specs=[pl.BlockSpec((2, tm, tn), lambda j, i: (0, i, j)), pl.BlockSpec((8, tn), lambda j, i: (0, j))],
        out_shape=[jax.ShapeDtypeStruct((2, S, F), BF16), jax.ShapeDtypeStruct((8, F), F32)],
        args=[dg, dg, hu, hu, hu, hu, hu, conv_w, conv_b], comm=comm, carry_us=FFN_ACT_US_PER_ELEM[1] * S * F)


def _head_expander(n_heads, da):
    e = np.zeros((LANES, da), np.float32)
    for h in range(n_heads):
        e[h, h * HEAD_DIM:(h + 1) * HEAD_DIM] = 1.0
    return jnp.asarray(e, BF16)


def _split_dot(v, e, dims):
    hi = v.astype(BF16)
    lo = (v - hi.astype(F32)).astype(BF16)
    return (lax.dot_general(hi, e, dims, preferred_element_type=F32) + lax.dot_general(lo, e, dims, preferred_element_type=F32))


def _lane_col(tile, h):
    lane = lax.broadcasted_iota(jnp.int32, tile.shape, 1)
    return jnp.sum(jnp.where(lane == h, tile, 0.0), axis=1, keepdims=True)


ATTN_US_PER_ELEM = (93.0 / (4096 * 1024), 113.0 / (4096 * 1024), 180.0 / (4096 * 1024))


def attn_branch_fwd(q, kv, gi, slopes, name, comm=None):
    S, DA = q.shape
    H = DA // HEAD_DIM
    window, d = BRANCHES[gi]
    n_steps = window // d
    blk = ATTN_BLOCK
    assert n_steps == blk and (S // d) % blk == 0
    nbs = S // d // blk
    scale = HEAD_DIM ** -0.5

    def body(q_ref, kp_ref, kc_ref, vp_ref, vc_ref, o_ref, l_ref):
        jb = pl.program_id(1)
        row = lax.broadcasted_iota(jnp.int32, (blk, 2 * blk), 0)
        col = lax.broadcasted_iota(jnp.int32, (blk, 2 * blk), 1)
        delta = row + blk - col
        valid = (delta >= 0) & (delta <= n_steps) & ((col >= blk) | (jb > 0))
        dist = (delta * d).astype(F32)
        lane = lax.broadcasted_iota(jnp.int32, (blk, LANES), 1)
        ltile = jnp.zeros((blk, LANES), F32)
        for h in range(H):
            hs = slice(h * HEAD_DIM, (h + 1) * HEAD_DIM)
            k2 = jnp.concatenate([kp_ref[:, hs], kc_ref[:, hs]], axis=0)
            v2 = jnp.concatenate([vp_ref[:, hs], vc_ref[:, hs]], axis=0)
            s = lax.dot_general(q_ref[:, hs], k2, (((1,), (1,)), ((), ())), preferred_element_type=F32) * scale
            s = jnp.where(valid, s - float(slopes[h]) * dist, NEG)
            m = jnp.max(s, axis=-1, keepdims=True)
            p = jnp.exp(s - m)
            l = jnp.sum(p, axis=-1, keepdims=True)
            o_ref[:, hs] = jnp.dot(p.astype(BF16), v2, preferred_element_type=F32) / l
            ltile = jnp.where(lane == h, m + jnp.log(l), ltile)
        l_ref[...] = ltile

    def cur(width, off):
        return pl.BlockSpec((blk, width), lambda r, jb: (r * nbs + jb, off))

    def prv(width, off):
        return pl.BlockSpec((blk, width), lambda r, jb: (r * nbs + jnp.maximum(jb - 1, 0), off))

    return _pcall(
        body, name=name, grid=(d, nbs),
        in_specs=[cur(DA, 0), prv(DA, 0), cur(DA, 0), prv(DA, 1), cur(DA, 1)],
        out_specs=[cur(DA, 0), cur(LANES, 0)],
        out_shape=[jax.ShapeDtypeStruct((S, DA), F32), jax.ShapeDtypeStruct((S, LANES), F32)],
        args=[q, kv, kv, kv, kv], comm=comm, carry_us=ATTN_US_PER_ELEM[0] * S * DA)


def attn_combine(os_, lses, name):
    S, DA = os_[0].shape
    H = DA // HEAD_DIM
    tm = _tile(S, 256, 16)
    expander = _head_expander(H, DA)
    nbr = len(os_)

    def body(*refs):
        o_refs, l_refs, e_ref = refs[:nbr], refs[nbr:2 * nbr], refs[2 * nbr]
        out_ref, lse_ref = refs[2 * nbr + 1:]
        ls = [r[...] for r in l_refs]
        lmax = functools.reduce(jnp.maximum, ls)
        es = [jnp.exp(l - lmax) for l in ls]
        den = functools.reduce(lambda a, b: a + b, es)
        lse_ref[...] = lmax + jnp.log(den)
        acc = jnp.zeros((tm, DA), F32)
        for e, o_ref in zip(es, o_refs):
            acc = acc + _split_dot(e / den, e_ref[...], (((1,), (0,)), ((), ()))) * o_ref[...]
        out_ref[...] = acc.astype(out_ref.dtype)

    row = pl.BlockSpec((tm, DA), lambda i: (i, 0))
    lrow = pl.BlockSpec((tm, LANES), lambda i: (i, 0))
    return pl.pallas_call(
        body, name=name, grid=(S // tm,),
        in_specs=[row] * nbr + [lrow] * nbr + [pl.BlockSpec((LANES, DA), lambda i: (0, 0))],
        out_specs=[row, lrow], out_shape=[jax.ShapeDtypeStruct((S, DA), BF16), jax.ShapeDtypeStruct((S, LANES), F32)],
        compiler_params=_cparams(("parallel",)))(*os_, *lses, expander)


def attn_delta(do, o, name):
    S, DA = o.shape
    H = DA // HEAD_DIM
    tm = _tile(S, 512, 16)
    expander = _head_expander(H, DA)

    def body(do_ref, o_ref, e_ref, d_ref):
        prod = do_ref[...].astype(F32) * o_ref[...].astype(F32)
        d_ref[...] = _split_dot(prod, e_ref[...], (((1,), (1,)), ((), ())))

    row = pl.BlockSpec((tm, DA), lambda i: (i, 0))
    return pl.pallas_call(body, name=name, grid=(S // tm,), in_specs=[row, row, pl.BlockSpec((LANES, DA), lambda i: (0, 0))],
                          out_specs=pl.BlockSpec((tm, LANES), lambda i: (i, 0)), out_shape=jax.ShapeDtypeStruct((S, LANES), F32),
                          compiler_params=_cparams(("parallel",)))(do, o, expander)


def attn_branch_dq(q, kv, do, lse, dlt, gi, slopes, name, out_cols=None, comm=None):
    S, DA = q.shape
    H = DA // HEAD_DIM
    window, d = BRANCHES[gi]
    n_steps = window // d
    blk = ATTN_BLOCK
    nbs = S // d // blk
    scale = HEAD_DIM ** -0.5

    def body(q_ref, kp_ref, kc_ref, vp_ref, vc_ref, do_ref, l_ref, d_ref, dq_ref):
        jb = pl.program_id(1)
        row = lax.broadcasted_iota(jnp.int32, (blk, 2 * blk), 0)
        col = lax.broadcasted_iota(jnp.int32, (blk, 2 * blk), 1)
        delta = row + blk - col
        valid = (delta >= 0) & (delta <= n_steps) & ((col >= blk) | (jb > 0))
        dist = (delta * d).astype(F32)
        ltile, dtile = l_ref[...], d_ref[...]
        for h in range(H):
            hs = slice(h * HEAD_DIM, (h + 1) * HEAD_DIM)
            k2 = jnp.concatenate([kp_ref[:, hs], kc_ref[:, hs]], axis=0)
            v2 = jnp.concatenate([vp_ref[:, hs], vc_ref[:, hs]], axis=0)
            s = lax.dot_general(q_ref[:, hs], k2, (((1,), (1,)), ((), ())), preferred_element_type=F32) * scale
            s = jnp.where(valid, s - float(slopes[h]) * dist - _lane_col(ltile, h), NEG)
            p = jnp.exp(s)
            dp = lax.dot_general(do_ref[:, hs], v2, (((1,), (1,)), ((), ())), preferred_element_type=F32)
            ds = (p * (dp - _lane_col(dtile, h))).astype(BF16)
            dq_ref[:, hs] = (jnp.dot(ds, k2, preferred_element_type=F32) * scale).astype(dq_ref.dtype)

    def cur(width, off):
        return pl.BlockSpec((blk, width), lambda r, jb: (r * nbs + jb, off))

    def prv(width, off):
        return pl.BlockSpec((blk, width), lambda r, jb: (r * nbs + jnp.maximum(jb - 1, 0), off))

    return _pcall(
        body, name=name, grid=(d, nbs),
        in_specs=[cur(DA, 0), prv(DA, 0), cur(DA, 0), prv(DA, 1), cur(DA, 1), cur(DA, 0), cur(LANES, 0), cur(LANES, 0)],
        out_specs=[cur(DA, 0)], out_shape=[jax.ShapeDtypeStruct((S, out_cols or DA), BF16)],
        args=[q, kv, kv, kv, kv, do, lse, dlt], comm=comm, carry_us=ATTN_US_PER_ELEM[1] * S * DA)[0]


def attn_branch_dkv(q, kv, do, lse, dlt, dkv_prev, gi, slopes, name, comm=None):
    S, DA = q.shape
    H = DA // HEAD_DIM
    window, d = BRANCHES[gi]
    n_steps = window // d
    blk = ATTN_BLOCK
    nbs = S // d // blk
    scale = HEAD_DIM ** -0.5
    accumulate = dkv_prev is not None

    def body(*refs):
        k_ref, v_ref, qc_ref, qn_ref, doc_ref, don_ref, lc_ref, ln_ref, dc_ref, dn_ref = refs[:10]
        out_ref = refs[-1]
        kb = pl.program_id(1)
        row = lax.broadcasted_iota(jnp.int32, (2 * blk, blk), 0)
        col = lax.broadcasted_iota(jnp.int32, (2 * blk, blk), 1)
        delta = row - col
        valid = (delta >= 0) & (delta <= n_steps) & ((row < blk) | (kb < nbs - 1))
        dist = (delta * d).astype(F32)
        l2 = jnp.concatenate([lc_ref[...], ln_ref[...]], axis=0)
        d2 = jnp.concatenate([dc_ref[...], dn_ref[...]], axis=0)
        for h in range(H):
            hs = slice(h * HEAD_DIM, (h + 1) * HEAD_DIM)
            vs = slice(DA + h * HEAD_DIM, DA + (h + 1) * HEAD_DIM)
            q2 = jnp.concatenate([qc_ref[:, hs], qn_ref[:, hs]], axis=0)
            do2 = jnp.concatenate([doc_ref[:, hs], don_ref[:, hs]], axis=0)
            s = lax.dot_general(q2, k_ref[:, hs], (((1,), (1,)), ((), ())), preferred_element_type=F32) * scale
            s = jnp.where(valid, s - float(slopes[h]) * dist - _lane_col(l2, h), NEG)
            p = jnp.exp(s)
            dvh = lax.dot_general(p.astype(BF16), do2, (((0,), (0,)), ((), ())), preferred_element_type=F32)
            dp = lax.dot_general(do2, v_ref[:, hs], (((1,), (1,)), ((), ())), preferred_element_type=F32)
            ds = (p * (dp - _lane_col(d2, h))).astype(BF16)
            dkh = lax.dot_general(ds, q2, (((0,), (0,)), ((), ())), preferred_element_type=F32) * scale
            if accumulate:
                dkh = dkh + refs[10][:, hs]
                dvh = dvh + refs[10][:, vs]
            out_ref[:, hs] = dkh
            out_ref[:, vs] = dvh

    def cur(width, off):
        return pl.BlockSpec((blk, width), lambda r, kb: (r * nbs + kb, off))

    def nxt(width, off):
        return pl.BlockSpec((blk, width), lambda r, kb: (r * nbs + jnp.minimum(kb + 1, nbs - 1), off))

    in_specs = [cur(DA, 0), cur(DA, 1), cur(DA, 0), nxt(DA, 0), cur(DA, 0), nxt(DA, 0),
                cur(LANES, 0), nxt(LANES, 0), cur(LANES, 0), nxt(LANES, 0)]
    args = [kv, kv, q, q, do, do, lse, lse, dlt, dlt]
    aliases = {}
    if accumulate:
        in_specs.append(cur(2 * DA, 0))
        args.append(dkv_prev)
        aliases = {10: 0}
    return _pcall(
        body, name=name, grid=(d, nbs), in_specs=in_specs, out_specs=[cur(2 * DA, 0)],
        out_shape=[jax.ShapeDtypeStruct((S, 2 * DA), F32)], aliases=aliases, args=args, comm=comm,
        carry_us=ATTN_US_PER_ELEM[2] * S * DA)[0]


def ada_project(c16, w3, b3, name):
    L, D, Ns = w3.shape
    tn = _tile(Ns, 512, LANES)

    def body(c_ref, w_ref, b_ref, o_ref):
        cv = c_ref[...]
        cond = (cv * _sigmoid(cv)).astype(BF16)
        o_ref[...] = jnp.dot(cond, w_ref[...].astype(BF16), preferred_element_type=F32) + b_ref[...]

    return pl.pallas_call(
        body, name=name, grid=(L, Ns // tn),
        in_specs=[pl.BlockSpec((16, D), lambda l, j: (0, 0)), pl.BlockSpec((None, D, tn), lambda l, j: (l, 0, j)),
                  pl.BlockSpec((None, 1, tn), lambda l, j: (l, 0, j))],
        out_specs=pl.BlockSpec((None, 16, tn), lambda l, j: (l, 0, j)), out_shape=jax.ShapeDtypeStruct((L, 16, Ns), F32),
        compiler_params=_cparams(("parallel", "parallel")))(c16, w3, b3)


def _adamw(w, g, m, v):
    m = B1 * m + (1.0 - B1) * g
    v = B2 * v + (1.0 - B2) * (g * g)
    m_hat = m / (1.0 - B1 ** STEP)
    v_hat = v / (1.0 - B2 ** STEP)
    delta = -LR * (m_hat / (jnp.sqrt(v_hat) + ADAM_EPS) + WD * w)
    return delta, m, v


def ada_grad_adamw(c16, d3, w3, m3, v3, name):
    L, D, Ns = w3.shape
    tk = _tile(D, 256, 8)

    def body(c_ref, d_ref, w_ref, m_ref, v_ref, g_out, dl_out, m_out, v_out):
        cv = c_ref[...]
        cond = (cv * _sigmoid(cv)).astype(BF16)
        g = lax.dot_general(cond, d_ref[...].astype(BF16), (((0,), (0,)), ((), ())), preferred_element_type=F32)
        g_out[...] = g
        dl_out[...], m_out[...], v_out[...] = _adamw(w_ref[...], g, m_ref[...], v_ref[...])

    wspec = pl.BlockSpec((None, tk, Ns), lambda l, kj: (l, kj, 0))
    return pl.pallas_call(
        body, name=name, grid=(L, D // tk),
        in_specs=[pl.BlockSpec((16, tk), lambda l, kj: (0, kj)), pl.BlockSpec((None, 16, Ns), lambda l, kj: (l, 0, 0)), wspec, wspec, wspec],
        out_specs=[wspec] * 4, out_shape=[jax.ShapeDtypeStruct((L, D, Ns), F32)] * 4,
        compiler_params=_cparams(("parallel", "parallel")))(c16, d3, w3, m3, v3)


def adamw(w, g, m, v, name):
    R, C = w.shape
    tr = _tile(R, 256, 8)

    def body(w_ref, g_ref, m_ref, v_ref, g_out, dl_out, m_out, v_out):
        g = g_ref[...]
        g_out[...] = g
        dl_out[...], m_out[...], v_out[...] = _adamw(w_ref[...], g, m_ref[...], v_ref[...])

    spec = pl.BlockSpec((tr, C), lambda i: (i, 0))
    return pl.pallas_call(body, name=name, grid=(R // tr,), in_specs=[spec] * 4, out_specs=[spec] * 4,
                          out_shape=[jax.ShapeDtypeStruct((R, C), F32)] * 4, compiler_params=_cparams(("parallel",)))(w, g, m, v)


def sum_partials(own, recv, g_prev, layer, n_layers, pos, name):
    _, Rh, C = recv.shape
    tr = _tile(Rh, 256, 16)

    def body(pos_ref, own_ref, recv_ref, *rest):
        acc = own_ref[...].astype(F32)
        for rel in range(7):
            acc = acc + recv_ref[rel].astype(F32)
        rest[-1][...] = acc

    in_specs = [pl.BlockSpec((None, None, tr, C), lambda r, pos: (pos[1], pos[0], r, 0)), pl.BlockSpec((7, tr, C), lambda r, pos: (0, r, 0))]
    args = [pos, own.reshape(N_CHIPS, 2, Rh, C), recv]
    aliases = {}
    if g_prev is not None:
        in_specs.append(pl.BlockSpec(memory_space=pl.ANY))
        args.append(g_prev)
        aliases = {3: 0}
    return pl.pallas_call(
        body, name=name,
        grid_spec=pltpu.PrefetchScalarGridSpec(
            num_scalar_prefetch=1, grid=(Rh // tr,), in_specs=in_specs,
            out_specs=pl.BlockSpec((None, None, tr, C), lambda r, pos: (layer, pos[0], r, 0))),
        out_shape=jax.ShapeDtypeStruct((n_layers, 2, Rh, C), F32), input_output_aliases=aliases,
        compiler_params=_cparams(("parallel",)))(*args)


def sum_rows8(g8, name):
    _, R, C = g8.shape

    def body(g_ref, o_ref):
        acc = g_ref[0]
        for i in range(1, N_DEV):
            acc = acc + g_ref[i]
        o_ref[...] = acc

    return pl.pallas_call(body, name=name, grid=(1,), in_specs=[pl.BlockSpec((N_DEV, R, C), lambda i: (0, 0, 0))],
                          out_specs=pl.BlockSpec((R, C), lambda i: (0, 0)), out_shape=jax.ShapeDtypeStruct((R, C), F32),
                          compiler_params=_cparams(("arbitrary",)))(g8)


def _pack(vecs):
    flat = [v.reshape(-1).astype(F32) for v in vecs]
    sizes = [f.shape[0] for f in flat]
    total = sum(sizes)
    padded = -(-total // (8 * PACK_W)) * (8 * PACK_W)
    buf = jnp.concatenate(flat + [jnp.zeros((padded - total,), F32)])
    offs = np.concatenate([[0], np.cumsum(sizes)])
    return buf.reshape(-1, PACK_W), offs


def _unpack(buf, offs, shapes):
    flat = buf.reshape(-1)
    return [flat[int(offs[i]):int(offs[i + 1])].reshape(s) for i, s in enumerate(shapes)]


def kernel(x, c, ada_w, ada_b, norm1_g, norm2_g, pool_w_in, pool_w_grp, pool_scale, pool_w_out, kv_norm_g, kv_ada_w, kv_ada_b, w_kv, attn_w_q, attn_w_o, ffn_w_up, ffn_conv_w, ffn_conv_b, ffn_w_down, final_g, loss_target, m_ada_w, m_ada_b, m_norm1_g, m_norm2_g, m_pool_w_in, m_pool_w_grp, m_pool_scale, m_pool_w_out, m_kv_norm_g, m_kv_ada_w, m_kv_ada_b, m_w_kv, m_attn_w_q, m_attn_w_o, m_ffn_w_up, m_ffn_conv_w, m_ffn_conv_b, m_ffn_w_down, m_final_g, v_ada_w, v_ada_b, v_norm1_g, v_norm2_g, v_pool_w_in, v_pool_w_grp, v_pool_scale, v_pool_w_out, v_kv_norm_g, v_kv_ada_w, v_kv_ada_b, v_w_kv, v_attn_w_q, v_attn_w_o, v_ffn_w_up, v_ffn_conv_w, v_ffn_conv_b, v_ffn_w_down, v_final_g):
    S, D = x.shape[1], x.shape[2]
    depth = ada_w.shape[0]
    n_pool = pool_w_in.shape[0]
    n_attn = attn_w_q.shape[0]
    G = len(POOL_WINDOWS)
    NB = len(BRANCHES)
    DA = attn_w_o.shape[1] * N_CHIPS
    H = DA // HEAD_DIM
    F = ffn_conv_b.shape[1]
    Fs = F // N_CHIPS
    Dq = D // N_CHIPS
    ada_ns = ada_w.shape[2]
    kvada_ns = kv_ada_w.shape[1]
    slopes = _alibi_slopes(NB * H).reshape(NB, H)

    ix, iy, ic = lax.axis_index("x"), lax.axis_index("y"), lax.axis_index("c")
    p_me = 2 * ix + iy
    b_me = 4 * ix + 2 * iy + ic
    pos = jnp.stack([ic, p_me]).astype(jnp.int32)
    xs, tgt = x[0], loss_target[0]

    pk, offs = _pack([c, pool_scale, ffn_conv_w])
    rows1 = pk.shape[0]
    got = all_gather8(pk, "gather_small_in").reshape(N_DEV, rows1, PACK_W)
    c8 = got.reshape(N_DEV, -1)[:, :D]
    c16 = jnp.concatenate([c8, jnp.zeros_like(c8)], axis=0)
    chip_rows = got[0::2].reshape(N_CHIPS, -1)
    scale_full = chip_rows[:, int(offs[1]):int(offs[2])].reshape(N_CHIPS, n_pool, Dq).transpose(1, 0, 2).reshape(n_pool, D)
    convw_full = chip_rows[:, int(offs[2]):int(offs[3])].reshape(N_CHIPS, depth, 3, Fs).transpose(1, 2, 0, 3).reshape(depth, 3, F)

    ada_b_loc = lax.dynamic_slice(ada_b, (0, p_me * ada_ns), (depth, ada_ns)).reshape(depth, 1, ada_ns)
    kvb_loc = lax.dynamic_slice(kv_ada_b, (p_me * kvada_ns,), (kvada_ns,)).reshape(1, 1, kvada_ns)
    mods_loc = ada_project(c16, ada_w, ada_b_loc, "ada_project")[:, :N_DEV]
    kvmod_loc = ada_project(c16, kv_ada_w.reshape(1, D, kvada_ns), kvb_loc, "kv_ada_project")[0, :N_DEV]
    mods_cat = jnp.concatenate([mods_loc.transpose(1, 0, 2).reshape(N_DEV, depth * ada_ns), kvmod_loc], axis=1)
    mods_all = all_gather8(mods_cat, "gather_mods").reshape(N_CHIPS, 2, N_DEV, -1)
    mine = lax.dynamic_index_in_dim(mods_all[:, 0], b_me, axis=1, keepdims=False)
    mod = mine[:, :depth * ada_ns].reshape(N_CHIPS, depth, ada_ns).transpose(1, 0, 2).reshape(depth, 6, 1, D)
    kvmod = mine[:, depth * ada_ns:].reshape(2, 1, D)

    comm = _Comm()
    C = D // G
    kv_ns, q_ns, up_ns = w_kv.shape[1], attn_w_q.shape[2], ffn_w_up.shape[2]

    def layer_shards(l):
        sh = []
        if l < n_pool:
            sh += [(("pin", l), pool_w_in[l]), (("pgrp", l), pool_w_grp[l].reshape(-1, C)), (("pout", l), pool_w_out[l])]
        else:
            if l == n_pool:
                sh.append((("kv", 0), w_kv))
            sh += [(("wq", l), attn_w_q[l - n_pool]), (("wo", l), attn_w_o[l - n_pool])]
        sh += [(("up", l), ffn_w_up[l]), (("down", l), ffn_w_down[l])]
        return [(k, w.astype(BF16)) for k, w in sh]

    def weight(key, shape):
        return comm.require(key).reshape(shape)

    dil = [d for _, d in BRANCHES]
    kv_tn = DA // 2
    q_tn = DA // 4
    q_bwd_tn = q_ns
    up_tn = up_ns
    up_per_half = F // up_tn

    def up_gmap(s):
        return s // up_per_half, s % up_per_half

    def vec(v):
        return v.reshape(1, -1)

    saved = []
    xcur = xs
    kvs = None
    wts = {}
    push_gather(comm, layer_shards(0))
    comm.flush()
    for l in range(depth):
        if l + 1 < depth:
            push_gather(comm, layer_shards(l + 1))
        sh1, sc1, g1, sh2, sc2, g2 = [mod[l, i] for i in range(6)]
        st = {"x0": xcur}
        h1 = norm_mod(xcur, vec(norm1_g[l]), sh1, sc1, "norm_mod")
        st["h1"] = h1
        if l < n_pool:
            wts["pin", l] = weight(("pin", l), (1, D, D))
            u = mm_nn(h1, wts["pin", l], tn=D // 2, out_dtype=F32, name="pool_in_proj", comm=comm)
            wts["pgrp", l] = weight(("pgrp", l), (N_CHIPS, G, C // N_CHIPS, C))
            pooled, z, ys = pool_fwd(u, wts["pgrp", l], vec(scale_full[l]), "pool_mix")
            wts["pout", l] = weight(("pout", l), (1, D, D))
            out, x1 = mm_nn(ys, wts["pout", l], tn=D // 2, out_dtype=BF16, name="pool_out_proj", res=(xcur, g1), comm=comm)
            st.update(pooled=pooled, z=z, ys=ys, out1=out)
        else:
            if l == n_pool:
                wts["kv", 0] = weight(("kv", 0), (N_CHIPS, D, kv_ns))
                hkv = norm_mod(xcur, vec(kv_norm_g), kvmod[0], kvmod[1], "norm_mod")
                kvs = [mm_nn(hkv, wts["kv", 0], tn=kv_tn, out_dtype=BF16, name=f"kv_proj_b{gi}", ncb=4, perm_d=dil[gi], comm=comm,
                             cbmap=functools.partial(lambda jj, gi: 2 * gi + (jj // 2) * 2 * NB + jj % 2, gi=gi)) for gi in range(NB)]
                kv_state = {"x": xcur, "hkv": hkv}
            wts["wq", l] = weight(("wq", l), (N_CHIPS, D, q_ns))
            qs, os_, lses = [], [], []
            for gi in range(NB):
                q_b = mm_nn(h1, wts["wq", l], tn=q_tn, out_dtype=BF16, name=f"q_proj_b{gi}", ncb=4, perm_d=dil[gi], comm=comm,
                            cbmap=functools.partial(lambda jj, gi: 4 * gi + jj, gi=gi))
                o_b, l_b = attn_branch_fwd(q_b, kvs[gi], gi, slopes[gi], f"attn_fwd_b{gi}", comm=comm)
                if dil[gi] > 1:
                    o_b = unpermute_rows(o_b, dil[gi], f"unpermute_o_b{gi}")
                    l_b = unpermute_rows(l_b, dil[gi], f"unpermute_lse_b{gi}")
                qs.append(q_b)
                os_.append(o_b)
                lses.append(l_b)
            o, lse = attn_combine(os_, lses, "attn_combine")
            wts["wo", l] = weight(("wo", l), (1, DA, D))
            out, x1 = mm_nn(o, wts["wo", l], tn=D // 2, out_dtype=BF16, name="attn_out_proj", res=(xcur, g1), comm=comm)
            st.update(qs=qs, o=o, lse=lse, out1=out)
        st["x1"] = x1
        h2 = norm_mod(x1, vec(norm2_g[l]), sh2, sc2, "norm_mod")
        wts["up", l] = weight(("up", l), (N_CHIPS, D, up_ns))
        hu = mm_nn(h2, wts["up", l], tn=up_tn, out_dtype=BF16, name="ffn_up_proj", comm=comm)
        gated = ffn_act(hu, convw_full[l], vec(ffn_conv_b[l]), "ffn_act", comm=comm)
        wts["down", l] = weight(("down", l), (1, F, D))
        out2, x2 = mm_nn(gated, wts["down", l], tn=D // 2, out_dtype=BF16, name="ffn_down_proj", res=(x1, g2), comm=comm)
        st.update(h2=h2, hu=hu, gated=gated, out2=out2)
        saved.append(st)
        xcur = x2
    comm.flush()

    dx, fsums = loss_fwd_bwd(xcur, vec(final_g), tgt, "loss_head")
    loss = lax.psum(0.5 * jnp.sum(fsums[1]) / D, ("x", "y", "c"))
    d_final_g = fsums[0]

    dmods = [None] * depth
    d_n1 = [None] * depth
    d_n2 = [None] * depth
    d_convw = [None] * depth
    d_convb = [None] * depth
    d_scale = [None] * n_pool
    d_grp = [None] * n_pool
    dkvs = [None] * NB
    exchanged = []
    f_tk = _tile(F, 1408, LANES)
    ct_blocks = DA // _tile(DA, 256, LANES)

    def exchange(name, idx, dw):
        dw4 = dw.reshape(N_CHIPS, -1, dw.shape[-1])
        exchanged.append((name, idx, dw4))
        push_exchange(comm, (name, idx), dw4)

    for l in reversed(range(depth)):
        st = saved[l]
        sh1, sc1, g1, sh2, sc2, g2 = [mod[l, i] for i in range(6)]
        dout2, s_g2 = gate_bwd(dx, st["out2"], g2, "gate_bwd")
        dout2_3 = dout2.reshape(1, S, D)
        dgated = mm_nt(dout2_3, wts["down", l], tn=D, tk=f_tk, out_dtype=BF16, name="ffn_down_bwd", comm=comm)
        exchange("down", l, mm_tn(st["gated"], dout2_3, (1, F, D), tn=D, tk=f_tk, name="ffn_down_dw", comm=comm))
        dhu, s_conv = ffn_act_bwd(dgated, st["hu"], convw_full[l], vec(ffn_conv_b[l]), "ffn_act_bwd", comm=comm)
        dh2 = mm_nt(dhu, wts["up", l], tn=up_tn, tk=D, out_dtype=F32, name="ffn_up_bwd", gmap=up_gmap, comm=comm)
        exchange("up", l, mm_tn(st["h2"], dhu, (N_CHIPS, D, up_ns), tn=up_tn, tk=D, name="ffn_up_dw", gmap=up_gmap, comm=comm))
        dx, s_n2 = norm_mod_bwd(dh2, st["x1"], dx, vec(norm2_g[l]), sc2, "norm_mod_bwd")
        d_convw[l], d_convb[l] = s_conv[0:3], s_conv[3]
        d_n2[l] = s_n2[2]
        dout1, s_g1 = gate_bwd(dx, st["out1"], g1, "gate_bwd")
        dout1_3 = dout1.reshape(1, S, D)
        if l < n_pool:
            dys = mm_nt(dout1_3, wts["pout", l], tn=D, tk=D // 2, out_dtype=F32, name="pool_out_bwd", comm=comm)
            exchange("pout", l, mm_tn(st["ys"], dout1_3, (1, D, D), tn=D, tk=D, name="pool_out_dw", comm=comm))
            du, d_grp, s_sc = pool_bwd(dys, st["z"], st["pooled"], wts["pgrp", l], vec(scale_full[l]), "pool_mix_bwd")
            exchange("pgrp", l, d_grp.astype(BF16).reshape(G, N_CHIPS, C // N_CHIPS, C).transpose(1, 0, 2, 3))
            d_scale[l] = s_sc[0]
            du_3 = du.reshape(1, S, D)
            dh1 = mm_nt(du_3, wts["pin", l], tn=D, tk=D // 2, out_dtype=F32, name="pool_in_bwd", comm=comm)
            exchange("pin", l, mm_tn(st["h1"], du_3, (1, D, D), tn=D, tk=D, name="pool_in_dw", comm=comm))
        else:
            j = l - n_pool
            do = mm_nt(dout1_3, wts["wo", l], tn=D, tk=DA // 2, out_dtype=BF16, name="attn_out_bwd", comm=comm)
            exchange("wo", j, mm_tn(st["o"], dout1_3, (1, DA, D), tn=D, tk=DA, name="attn_out_dw", comm=comm))
            dlt = attn_delta(do, st["o"], "attn_delta")
            dq = None
            for gi in range(NB):
                d = dil[gi]
                do_b, l_b, dl_b = do, st["lse"], dlt
                if d > 1:
                    do_b = permute_rows(do, d, f"permute_do_b{gi}")
                    l_b = permute_rows(st["lse"], d, f"permute_lse_b{gi}")
                    dl_b = permute_rows(dlt, d, f"permute_delta_b{gi}")
                    dq_b = attn_branch_dq(st["qs"][gi], kvs[gi], do_b, l_b, dl_b, gi, slopes[gi], f"attn_dq_b{gi}", comm=comm)
                    dq = unpermute_rows(dq_b, d, f"unpermute_dq_b{gi}", into=dq, total_cols=NB * DA,
                                        colmap=functools.partial(lambda jj, gi: gi * ct_blocks + jj, gi=gi))
                else:
                    dq = attn_branch_dq(st["qs"][gi], kvs[gi], do_b, l_b, dl_b, gi, slopes[gi], f"attn_dq_b{gi}", out_cols=NB * DA,
                                        comm=comm)
                dkvs[gi] = attn_branch_dkv(st["qs"][gi], kvs[gi], do_b, l_b, dl_b, dkvs[gi], gi, slopes[gi],
                                           f"attn_dkv_b{gi}" + ("" if dkvs[gi] is None else "_acc"), comm=comm)
            dq_3 = dq.reshape(1, S, NB * DA)
            dh1 = mm_nt(dq_3, wts["wq", l], tn=q_bwd_tn, tk=D, out_dtype=F32, name="q_proj_bwd", comm=comm)
            exchange("wq", j, mm_tn(st["h1"], dq_3, (N_CHIPS, D, q_ns), tn=q_bwd_tn, tk=D, name="q_proj_dw", comm=comm))
        dx, s_n1 = norm_mod_bwd(dh1, st["x0"], dx, vec(norm1_g[l]), sc1, "norm_mod_bwd")
        d_n1[l] = s_n1[2]
        dmods[l] = jnp.stack([s_n1[0], s_n1[1], s_g1[0], s_n2[0], s_n2[1], s_g2[0]])
        if l == n_pool:
            dkv = None
            for gi in range(NB):
                dkv = unpermute_rows(dkvs[gi], dil[gi], f"unpermute_dkv_b{gi}", into=dkv, total_cols=2 * NB * DA,
                                     colmap=functools.partial(lambda jj, gi: (jj // ct_blocks) * NB * ct_blocks + gi * ct_blocks + jj % ct_blocks,
                                                              gi=gi))
            dkv_3 = dkv.reshape(1, S, 2 * NB * DA)
            dhkv = mm_nt(dkv_3, wts["kv", 0], tn=kv_ns // 2, tk=D, out_dtype=F32, name="kv_proj_bwd", comm=comm)
            exchange("kv", 0, mm_tn(kv_state["hkv"], dkv_3, (N_CHIPS, D, kv_ns), tn=kv_ns // 2, tk=D, name="kv_proj_dw", comm=comm))
            dx, s_kv = norm_mod_bwd(dhkv, kv_state["x"], dx, vec(kv_norm_g), kvmod[1], "norm_mod_bwd")
    grad_x = dx.reshape(1, S, D)

    smalls = [jnp.stack(dmods), jnp.stack([s_kv[0], s_kv[1]]), jnp.stack(d_n1), jnp.stack(d_n2), s_kv[2], jnp.stack(d_convb), d_final_g,
              jnp.stack(d_scale), jnp.stack(d_convw)]
    small_shapes = [s.shape for s in smalls]
    spk, soffs = _pack(smalls)
    srows = spk.shape[0]
    sgot = all_gather8(spk, "gather_small_grads").reshape(N_DEV, srows, PACK_W)
    ssum = sum_rows8(sgot, "sum_small_grads")
    g_mods, g_kvmod, g_n1, g_n2, g_kvn, g_convb, g_fg, g_scale_full, g_convw_full = _unpack(ssum, soffs, small_shapes)
    g_ada_b = g_mods.reshape(depth, 6 * D)
    g_kv_ada_b = g_kvmod.reshape(2 * D)
    g_scale = lax.dynamic_slice(g_scale_full, (0, p_me * Dq), (n_pool, Dq))
    g_convw = lax.dynamic_slice(g_convw_full, (0, 0, p_me * Fs), (depth, 3, Fs))

    small_w = [ada_b, norm1_g, norm2_g, kv_norm_g, kv_ada_b, ffn_conv_b, final_g, pool_scale, ffn_conv_w]
    small_m = [m_ada_b, m_norm1_g, m_norm2_g, m_kv_norm_g, m_kv_ada_b, m_ffn_conv_b, m_final_g, m_pool_scale, m_ffn_conv_w]
    small_v = [v_ada_b, v_norm1_g, v_norm2_g, v_kv_norm_g, v_kv_ada_b, v_ffn_conv_b, v_final_g, v_pool_scale, v_ffn_conv_w]
    small_g = [g_ada_b, g_n1, g_n2, g_kvn, g_kv_ada_b, g_convb, g_fg, g_scale, g_convw]
    sw_shapes = [w.shape for w in small_w]
    pw, woffs = _pack(small_w)
    s_res = adamw(pw, _pack(small_g)[0], _pack(small_m)[0], _pack(small_v)[0], "adamw_small")
    s_g, s_dl, s_m, s_v = [_unpack(r, woffs, sw_shapes) for r in s_res]

    per_dev = sgot.reshape(N_DEV, -1)
    dm_all = per_dev[:, int(soffs[0]):int(soffs[1])].reshape(N_DEV, depth, 6 * D)
    dkvm_all = per_dev[:, int(soffs[1]):int(soffs[2])].reshape(N_DEV, 1, 2 * D)

    def shard_cols(a, ns):
        sl = lax.dynamic_slice_in_dim(a, p_me * ns, ns, axis=2).transpose(1, 0, 2)
        return jnp.concatenate([sl, jnp.zeros_like(sl)], axis=1)

    ada_res = ada_grad_adamw(c16, shard_cols(dm_all, ada_ns), ada_w, m_ada_w, v_ada_w, "ada_grad_adamw")
    kvada_res = ada_grad_adamw(c16, shard_cols(dkvm_all, kvada_ns), kv_ada_w.reshape(1, D, kvada_ns), m_kv_ada_w.reshape(1, D, kvada_ns),
                               v_kv_ada_w.reshape(1, D, kvada_ns), "kv_ada_grad_adamw")
    kvada_res = [r.reshape(D, kvada_ns) for r in kvada_res]

    comm.flush()
    big_names = ["pin", "pgrp", "pout", "kv", "wq", "wo", "up", "down"]
    n_stack = {"pin": n_pool, "pgrp": n_pool, "pout": n_pool, "kv": 1, "wq": n_attn, "wo": n_attn, "up": depth, "down": depth}
    gsum = {nm: None for nm in big_names}
    for nm, idx, dw4 in exchanged:
        gsum[nm] = sum_partials(dw4, comm.store[nm, idx], gsum[nm], idx, n_stack[nm], pos, "sum_partials")
    for nm in big_names:
        g = gsum[nm]

        def done(outs, nm=nm):
            gsum[nm] = outs[0]

        comm.push(_Phase(("swap", nm), nm, None, 0.0, lambda: [], lambda g=g: [g], [], g.shape[0], 0, _build_swap, done))
    comm.flush()
    big_m = [m_pool_w_in, m_pool_w_grp, m_pool_w_out, m_w_kv, m_attn_w_q, m_attn_w_o, m_ffn_w_up, m_ffn_w_down]
    big_v = [v_pool_w_in, v_pool_w_grp, v_pool_w_out, v_w_kv, v_attn_w_q, v_attn_w_o, v_ffn_w_up, v_ffn_w_down]
    big_w = [pool_w_in, pool_w_grp, pool_w_out, w_kv, attn_w_q, attn_w_o, ffn_w_up, ffn_w_down]
    big_res = []
    for nm, w, m_, v_ in zip(big_names, big_w, big_m, big_v):
        cols = gsum[nm].shape[-1]
        res = adamw(w.reshape(-1, cols), gsum[nm].reshape(-1, cols), m_.reshape(-1, cols), v_.reshape(-1, cols), "adamw_big")
        big_res.append([r.reshape(w.shape) for r in res])

    order = ["ada_w", "ada_b", "norm1_g", "norm2_g", "pool_w_in", "pool_w_grp", "pool_scale", "pool_w_out", "kv_norm_g", "kv_ada_w",
             "kv_ada_b", "w_kv", "attn_w_q", "attn_w_o", "ffn_w_up", "ffn_conv_w", "ffn_conv_b", "ffn_w_down", "final_g"]
    small_names = ["ada_b", "norm1_g", "norm2_g", "kv_norm_g", "kv_ada_b", "ffn_conv_b", "final_g", "pool_scale", "ffn_conv_w"]
    results = {"ada_w": ada_res, "kv_ada_w": kvada_res}
    for i, nm in enumerate(small_names):
        results[nm] = [s_g[i], s_dl[i], s_m[i], s_v[i]]
    for i, nm in enumerate(["pool_w_in", "pool_w_grp", "pool_w_out", "w_kv", "attn_w_q", "attn_w_o", "ffn_w_up", "ffn_w_down"]):
        results[nm] = big_res[i]
    outs = [loss, grad_x]
    for kind in range(4):
        outs += [results[nm][kind] for nm in order]
    return tuple(outs)
```

```python
import functools
import math

import numpy as np
import jax
import jax.numpy as jnp
from jax import lax
from jax.experimental import pallas as pl
from jax.experimental.pallas import tpu as pltpu

F32 = jnp.float32
BF16 = jnp.bfloat16
MESH = pl.DeviceIdType.MESH

POOL_WINDOWS = (2, 4, 8, 16)
BRANCHES = ((128, 1), (512, 4), (2048, 16))
HEAD_DIM = 64
ATTN_BLOCK = 128
EPS = 1e-6
LR, B1, B2, ADAM_EPS, WD, STEP = 0.001, 0.9, 0.999, 1e-08, 0.01, 10

VMEM_LIMIT_BYTES = 56 * 1024 * 1024
LANES = 128
PACK_W = 1024
HALO = 16
NEG = -1e30
N_CHIPS = 4
N_DEV = 8


def _alibi_slopes(n):
    def pow2(m):
        start = 2.0 ** (-(2.0 ** -(math.log2(m) - 3)))
        return [start ** (i + 1) for i in range(m)]
    if math.log2(n).is_integer():
        s = pow2(n)
    else:
        c = 2 ** math.floor(math.log2(n))
        s = pow2(c) + pow2(2 * c)[0::2][: n - c]
    s = np.asarray(s, dtype=np.float32)
    return -np.sort(-s)


def _cparams(sem=None):
    return pltpu.CompilerParams(dimension_semantics=sem, vmem_limit_bytes=VMEM_LIMIT_BYTES)


def _tile(n, pref, unit):
    t = (min(pref, n) // unit) * unit
    while t >= unit:
        if n % t == 0:
            return t
        t -= unit
    return n


def _sigmoid(v):
    return 1.0 / (1.0 + jnp.exp(-v))


def all_gather8(xs, name):
    m_per, n = xs.shape

    def body(x_ref, out_ref, send_sems, recv_sems, local_sem):
        x, y, c = lax.axis_index("x"), lax.axis_index("y"), lax.axis_index("c")
        me, sibling = (x, y, c), (x, y, 1 - c)
        chips = [(1 - x, y), (x, 1 - y), (1 - x, 1 - y)]

        def rows(px, py, pc):
            return out_ref.at[pl.ds((4 * px + 2 * py + pc) * m_per, m_per), :]

        def copy(k, block, to, src=None):
            return pltpu.make_async_remote_copy(src_ref=rows(*block) if src is None else src, dst_ref=rows(*block),
                                                send_sem=send_sems.at[k], recv_sem=recv_sems.at[k], device_id=to, device_id_type=MESH)

        mine = pltpu.make_async_copy(x_ref, rows(*me), local_sem)
        mine.start()
        first = [copy(0, me, sibling, src=x_ref)]
        first += [copy(1 + j, me, (*chip, c), src=x_ref) for j, chip in enumerate(chips)]
        for cp in first:
            cp.start()
        passed = [copy(4 + j, (*chip, c), sibling) for j, chip in enumerate(chips)]
        for j, chip in enumerate(chips):
            copy(1 + j, (*chip, c), me).wait_recv()
            passed[j].start()
        copy(0, sibling, me).wait_recv()
        for j, chip in enumerate(chips):
            copy(4 + j, (*chip, 1 - c), me).wait_recv()
        for cp in first + passed:
            cp.wait_send()
        mine.wait()

    return pl.pallas_call(
        body, name=name,
        out_shape=jax.ShapeDtypeStruct((N_DEV * m_per, n), xs.dtype),
        in_specs=[pl.BlockSpec(memory_space=pltpu.VMEM)],
        out_specs=pl.BlockSpec(memory_space=pltpu.VMEM),
        scratch_shapes=[pltpu.SemaphoreType.DMA((7,)), pltpu.SemaphoreType.DMA((7,)), pltpu.SemaphoreType.DMA],
        compiler_params=pltpu.CompilerParams(vmem_limit_bytes=VMEM_LIMIT_BYTES),
    )(xs)


HBM_SPEC = pl.BlockSpec(memory_space=pltpu.HBM)


def _mesh_pos():
    x, y, c = lax.axis_index("x"), lax.axis_index("y"), lax.axis_index("c")
    return x, y, c, [(1 - x, y), (x, 1 - y), (1 - x, 1 - y)]


class _Phase:
    def __init__(self, key, group, after, est_us, get_ins, get_inouts, new_outs, n_sems, n_local, build, on_done):
        self.key, self.group, self.after, self.est_us = key, group, after, est_us
        self.get_ins, self.get_inouts, self.new_outs = get_ins, get_inouts, new_outs
        self.n_sems, self.n_local, self.build, self.on_done = n_sems, n_local, build, on_done


def _rcopy(src, dst, send_sems, recv_sems, k, to):
    return pltpu.make_async_remote_copy(src_ref=src, dst_ref=dst, send_sem=send_sems.at[k], recv_sem=recv_sems.at[k],
                                        device_id=to, device_id_type=MESH)


def _build_fetch(in_refs, out_refs, send_sems, recv_sems, loc_sems, sem0, loc0):
    (shard,), (g,) = in_refs, out_refs
    x, y, c, chips = _mesh_pos()
    p_me = 2 * x + y
    locs = [pltpu.make_async_copy(shard.at[i], g.at[p_me, i], loc_sems.at[loc0 + i]) for i in range(2)]
    sends = [_rcopy(shard.at[c], g.at[p_me, c], send_sems, recv_sems, sem0 + j, (*chip, c)) for j, chip in enumerate(chips)]

    def recvs():
        blks = [g.at[2 * chip[0] + chip[1], c] for chip in chips]
        return [_rcopy(blk, blk, send_sems, recv_sems, sem0 + j, (*chip, c)) for j, (blk, chip) in enumerate(zip(blks, chips))]
    return sends, recvs, locs


def _build_pass(in_refs, out_refs, send_sems, recv_sems, loc_sems, sem0, loc0):
    (g,) = out_refs
    x, y, c, chips = _mesh_pos()
    sib = (x, y, 1 - c)
    slots = [2 * chip[0] + chip[1] for chip in chips]
    sends = [_rcopy(g.at[p, c], g.at[p, c], send_sems, recv_sems, sem0 + j, sib) for j, p in enumerate(slots)]

    def recvs():
        return [_rcopy(g.at[p, 1 - c], g.at[p, 1 - c], send_sems, recv_sems, sem0 + j, sib) for j, p in enumerate(slots)]
    return sends, recvs, []


def _build_exchange(in_refs, out_refs, send_sems, recv_sems, loc_sems, sem0, loc0):
    (dw,), (recv,) = in_refs, out_refs
    x, y, c, chips = _mesh_pos()
    targets = [(c, chip, c, j) for j, chip in enumerate(chips)]
    targets += [(1 - c, chip, 1 - c, 3 + j) for j, chip in enumerate([(x, y)] + chips)]
    sends = [_rcopy(dw.at[2 * chip[0] + chip[1], half], recv.at[rel], send_sems, recv_sems, sem0 + rel, (*chip, core))
             for half, chip, core, rel in targets]

    def recvs():
        return [_rcopy(recv.at[rel], recv.at[rel], send_sems, recv_sems, sem0 + rel, (x, y, 1 - c)) for rel in range(7)]
    return sends, recvs, []


def _build_swap(in_refs, out_refs, send_sems, recv_sems, loc_sems, sem0, loc0):
    (g,) = out_refs
    x, y, c, _ = _mesh_pos()
    sib = (x, y, 1 - c)
    sends = [_rcopy(g.at[l, c], g.at[l, c], send_sems, recv_sems, sem0 + l, sib) for l in range(g.shape[0])]

    def recvs():
        return [_rcopy(g.at[l, 1 - c], g.at[l, 1 - c], send_sems, recv_sems, sem0 + l, sib) for l in range(g.shape[0])]
    return sends, recvs, []


def _plan_refs(phases):
    xin, xout, alias, n_sems, n_loc, out_of = [], [], {}, 0, 0, {}
    for ph in phases:
        ph.sem0, ph.loc0 = n_sems, n_loc
        n_sems += ph.n_sems
        n_loc += ph.n_local
        if ph.after in out_of:
            ph.in0, ph.n_in, ph.out0, ph.n_out = 0, 0, out_of[ph.after], 1
            continue
        ins, inouts = ph.get_ins(), ph.get_inouts()
        ph.in0, ph.n_in = len(xin), len(ins)
        xin += ins
        ph.out0, ph.n_out = len(xout), len(inouts) + len(ph.new_outs)
        out_of[ph.key] = ph.out0
        for a in inouts:
            alias[len(xin)] = len(xout)
            xin.append(a)
            xout.append(jax.ShapeDtypeStruct(a.shape, a.dtype))
        xout += ph.new_outs
    return xin, xout, alias, max(n_sems, 1), max(n_loc, 1)


def _built(ph, xin_refs, xout_refs, sems):
    return ph.build(xin_refs[ph.in0:ph.in0 + ph.n_in], xout_refs[ph.out0:ph.out0 + ph.n_out], sems[0], sems[1], sems[2], ph.sem0, ph.loc0)


def _start(phases, xin_refs, xout_refs, sems):
    for ph in phases:
        sends, _, locs = _built(ph, xin_refs, xout_refs, sems)
        for cp in locs + sends:
            cp.start()


def _finish(phases, xin_refs, xout_refs, sems):
    for ph in phases:
        sends, recvs, locs = _built(ph, xin_refs, xout_refs, sems)
        for cp in recvs():
            cp.wait_recv()
        for cp in sends:
            cp.wait_send()
        for cp in locs:
            cp.wait()


class _Comm:
    def __init__(self):
        self.queue, self.store, self.n_alone = [], {}, 0

    def push(self, ph):
        self.queue.append(ph)

    def take(self, carry_us):
        taken, t = [], 0.0
        while True:
            pending = {ph.key for ph in self.queue} | {ph.key for ph in taken}
            room = 2.0 * carry_us if not taken else 0.8 * carry_us - t
            fits = [ph for ph in self.queue if ph.after not in pending and ph.est_us <= room]
            if not fits:
                return taken
            ph = max(fits, key=lambda p: p.est_us)
            self.queue.remove(ph)
            taken.append(ph)
            t += ph.est_us

    def require(self, group):
        phases = [ph for ph in self.queue if ph.group == group]
        if phases:
            self.queue = [ph for ph in self.queue if ph.group != group]
            self.run_alone(phases)
        return self.store[group]

    def flush(self):
        phases, self.queue = self.queue, []
        if phases:
            self.run_alone(phases)

    def run_alone(self, phases):
        phases = [ph for ph in phases if ph.after is None] + [ph for ph in phases if ph.after is not None]
        groups, keys = [[]], set()
        for ph in phases:
            if ph.after in keys:
                groups.append([])
                keys = set()
            groups[-1].append(ph)
            keys.add(ph.key)
        xin, xout, alias, n_sems, n_loc = _plan_refs(phases)
        n_xin, n_xout = len(xin), len(xout)

        def body(*refs):
            xin_refs, xout_refs, sems = refs[:n_xin], refs[n_xin:n_xin + n_xout], refs[n_xin + n_xout:]
            for grp in groups:
                _start(grp, xin_refs, xout_refs, sems)
                _finish(grp, xin_refs, xout_refs, sems)

        self.n_alone += 1
        outs = pl.pallas_call(
            body, name=f"comm_alone_{self.n_alone}", out_shape=xout, in_specs=[HBM_SPEC] * n_xin, out_specs=[HBM_SPEC] * n_xout,
            input_output_aliases=alias,
            scratch_shapes=[pltpu.SemaphoreType.DMA((n_sems,)), pltpu.SemaphoreType.DMA((n_sems,)), pltpu.SemaphoreType.DMA((n_loc,))],
        )(*xin)
        for ph in phases:
            ph.on_done(outs[ph.out0:ph.out0 + ph.n_out])


def _pcall(body, *, name, grid, in_specs, out_specs, out_shape, args, scratch_shapes=(), aliases=None, comm=None, carry_us=0.0):
    phases = comm.take(carry_us) if comm is not None else []
    n_in, n_out, n_scr = len(in_specs), len(out_specs), len(scratch_shapes)
    if not phases:
        return pl.pallas_call(body, name=name, grid=grid, in_specs=in_specs, out_specs=out_specs, out_shape=out_shape,
                              scratch_shapes=list(scratch_shapes), input_output_aliases=aliases or {},
                              compiler_params=_cparams(("arbitrary",) * len(grid)))(*args)
    xin, xout, xalias, n_sems, n_loc = _plan_refs(phases)
    n_xin, n_xout = len(xin), len(xout)
    all_alias = dict(aliases or {})
    all_alias.update({n_in + i: n_out + o for i, o in xalias.items()})

    def carrier(*refs):
        ins, xin_refs = refs[:n_in], refs[n_in:n_in + n_xin]
        outs = refs[n_in + n_xin:n_in + n_xin + n_out]
        xout_refs = refs[n_in + n_xin + n_out:n_in + n_xin + n_out + n_xout]
        rest = refs[n_in + n_xin + n_out + n_xout:]
        scr, sems = rest[:n_scr], rest[n_scr:]
        pids = [pl.program_id(k) for k in range(len(grid))]
        first = functools.reduce(jnp.logical_and, [p == 0 for p in pids])
        last = functools.reduce(jnp.logical_and, [p == n - 1 for p, n in zip(pids, grid)])

        @pl.when(first)
        def _():
            _start(phases, xin_refs, xout_refs, sems)
        body(*ins, *outs, *scr)

        @pl.when(last)
        def _():
            _finish(phases, xin_refs, xout_refs, sems)

    outs = pl.pallas_call(
        carrier, name=name, grid=grid, in_specs=list(in_specs) + [HBM_SPEC] * n_xin, out_specs=list(out_specs) + [HBM_SPEC] * n_xout,
        out_shape=list(out_shape) + xout,
        scratch_shapes=list(scratch_shapes) + [pltpu.SemaphoreType.DMA((n_sems,)), pltpu.SemaphoreType.DMA((n_sems,)),
                                               pltpu.SemaphoreType.DMA((n_loc,))],
        input_output_aliases=all_alias, compiler_params=_cparams(("arbitrary",) * len(grid)))(*args, *xin)
    for ph in phases:
        ph.on_done(outs[n_out + ph.out0:n_out + ph.out0 + ph.n_out])
    return outs[:n_out]


FETCH_US_PER_MB = 20.4
PASS_US_PER_MB = 3.3
EXCHANGE_US_PER_MB = 14.5


def push_gather(comm, keys_shards):
    prev = None
    for key, shard in keys_shards:
        r, c = shard.shape
        sh = shard.reshape(2, r // 2, c)
        half_mb = r // 2 * c * 2 / 1e6

        def done(outs, key=key):
            comm.store[key] = outs[0]

        comm.push(_Phase(("fetch", key), key, None, 3 * half_mb * FETCH_US_PER_MB, lambda sh=sh: [sh], lambda: [],
                         [jax.ShapeDtypeStruct((N_CHIPS, 2, r // 2, c), BF16)], 3, 2, _build_fetch, done))
        if prev is not None:
            comm.push(prev)
        prev = _Phase(("pass", key), key, ("fetch", key), 3 * half_mb * PASS_US_PER_MB + 3.0, lambda: [], lambda key=key: [comm.store[key]],
                      [], 3, 0, _build_pass, done)
    if prev is not None:
        comm.push(prev)


def push_exchange(comm, key, dw):
    _, r, c = dw.shape
    half_mb = r // 2 * c * 2 / 1e6

    def done(outs):
        comm.store[key] = outs[0]

    comm.push(_Phase(("exchange", key), key, None, 6 * half_mb * EXCHANGE_US_PER_MB, lambda: [dw.reshape(N_CHIPS, 2, r // 2, c)], lambda: [],
                     [jax.ShapeDtypeStruct((7, r // 2, c), BF16)], 7, 0, _build_exchange, done))


MM_FLOPS_PER_US = 6.0e8


def mm_nn(a, w3, *, tn, out_dtype, name, ncb=None, cbmap=None, res=None, perm_d=1, comm=None):
    M, K = a.shape
    P, _, Ns = w3.shape
    nper = Ns // tn
    ncb = P * nper if ncb is None else ncb
    tm = ATTN_BLOCK * perm_d if perm_d > 1 else _tile(M, 512, 8)
    cbm = cbmap if cbmap is not None else (lambda j: j)
    nch = tn // LANES

    def body(*refs):
        if res is None:
            a_ref, w_ref, o_ref = refs[:3]
        else:
            a_ref, w_ref, x_ref, g_ref, o_ref, xo_ref = refs
        acc = jnp.dot(a_ref[...].astype(BF16), w_ref[...], preferred_element_type=F32)
        if perm_d > 1:
            scr = refs[3]
            for cj in range(nch):
                scr[cj] = acc[:, cj * LANES:(cj + 1) * LANES]
            for r in range(perm_d):
                for cj in range(nch):
                    o_ref[r, :, cj * LANES:(cj + 1) * LANES] = scr.at[cj][pl.ds(r, ATTN_BLOCK, stride=perm_d), :].astype(o_ref.dtype)
        else:
            o_ref[...] = acc.astype(o_ref.dtype)
        if res is not None:
            xo_ref[...] = x_ref[...] + g_ref[...] * acc

    in_specs = [pl.BlockSpec((tm, K), lambda i, j: (i, 0)),
                pl.BlockSpec((None, K, tn), lambda i, j: (cbm(j) // nper, 0, cbm(j) % nper))]
    scratch = []
    if perm_d > 1:
        out_specs = [pl.BlockSpec((perm_d, ATTN_BLOCK, tn), lambda i, j: (0, i, j))]
        out_shape = [jax.ShapeDtypeStruct((perm_d, M // perm_d, ncb * tn), out_dtype)]
        scratch = [pltpu.VMEM((nch, tm, LANES), F32)]
    else:
        out_specs = [pl.BlockSpec((tm, tn), lambda i, j: (i, j))]
        out_shape = [jax.ShapeDtypeStruct((M, ncb * tn), out_dtype)]
    args = [a, w3]
    if res is not None:
        in_specs += [pl.BlockSpec((tm, tn), lambda i, j: (i, j)), pl.BlockSpec((1, tn), lambda i, j: (0, j))]
        out_specs.append(pl.BlockSpec((tm, tn), lambda i, j: (i, j)))
        out_shape.append(jax.ShapeDtypeStruct((M, ncb * tn), F32))
        args += [res[0], res[1]]
    outs = _pcall(body, name=name, grid=(M // tm, ncb), in_specs=in_specs, out_specs=out_specs, out_shape=out_shape, args=args,
                  scratch_shapes=scratch, comm=comm, carry_us=2.0 * M * K * ncb * tn / MM_FLOPS_PER_US)
    if perm_d > 1:
        return outs[0].reshape(M, ncb * tn)
    return outs[0] if res is None else (outs[0], outs[1])


def permute_rows(x, d, name):
    S, C = x.shape
    R = ATTN_BLOCK * d
    ct = _tile(C, 256, LANES)
    nch = ct // LANES

    def body(x_ref, o_ref, scr):
        xv = x_ref[...].astype(F32)
        for cj in range(nch):
            scr[cj] = xv[:, cj * LANES:(cj + 1) * LANES]
        for r in range(d):
            for cj in range(nch):
                o_ref[r, :, cj * LANES:(cj + 1) * LANES] = scr.at[cj][pl.ds(r, ATTN_BLOCK, stride=d), :].astype(o_ref.dtype)

    out = pl.pallas_call(body, name=name, grid=(S // R, C // ct), in_specs=[pl.BlockSpec((R, ct), lambda i, j: (i, j))],
                         out_specs=pl.BlockSpec((d, ATTN_BLOCK, ct), lambda i, j: (0, i, j)),
                         out_shape=jax.ShapeDtypeStruct((d, S // d, C), x.dtype), scratch_shapes=[pltpu.VMEM((nch, R, LANES), F32)],
                         compiler_params=_cparams(("parallel", "parallel")))(x)
    return out.reshape(S, C)


def unpermute_rows(p, d, name, into=None, total_cols=None, colmap=None):
    S, C = p.shape
    rpb = max(ATTN_BLOCK, 512 // d)
    R = rpb * d
    ct = _tile(C, 256, LANES)
    nch = ct // LANES
    total_cols = C if total_cols is None else total_cols
    cm = colmap if colmap is not None else (lambda j: j)

    def body(*refs):
        p_ref, o_ref, scr = refs[0], refs[-2], refs[-1]
        if d == 1:
            o_ref[...] = p_ref[0]
            return
        for r in range(d):
            for cj in range(nch):
                scr.at[cj][pl.ds(r, rpb, stride=d), :] = p_ref[r, :, cj * LANES:(cj + 1) * LANES].astype(F32)
        for cj in range(nch):
            o_ref[:, cj * LANES:(cj + 1) * LANES] = scr[cj].astype(o_ref.dtype)

    in_specs = [pl.BlockSpec((d, rpb, ct), lambda i, j: (0, i, j))]
    args = [p.reshape(d, S // d, C)]
    aliases = {}
    if into is not None:
        in_specs.append(pl.BlockSpec(memory_space=pl.ANY))
        args.append(into)
        aliases = {1: 0}
    return pl.pallas_call(body, name=name, grid=(S // R, C // ct), in_specs=in_specs,
                          out_specs=pl.BlockSpec((R, ct), lambda i, j: (i, cm(j))),
                          out_shape=jax.ShapeDtypeStruct((S, total_cols), p.dtype), scratch_shapes=[pltpu.VMEM((nch, R, LANES), F32)],
                          input_output_aliases=aliases, compiler_params=_cparams(("parallel", "parallel")))(*args)


def mm_nt(g3, w3, *, tn, tk, out_dtype, name, gmap=None, comm=None):
    _, M, _ = g3.shape
    P, K, Ns = w3.shape
    nper = Ns // tn
    ns = P * nper
    tm = _tile(M, 512, 8)
    gm = gmap if gmap is not None else (lambda s: (0, s))

    def body(g_ref, w_ref, o_ref, acc):
        s = pl.program_id(2)

        @pl.when(s == 0)
        def _():
            acc[...] = jnp.zeros_like(acc)
        acc[...] += lax.dot_general(g_ref[...].astype(BF16), w_ref[...], (((1,), (1,)), ((), ())), preferred_element_type=F32)

        @pl.when(s == ns - 1)
        def _():
            o_ref[...] = acc[...].astype(o_ref.dtype)

    return _pcall(
        body, name=name, grid=(M // tm, K // tk, ns),
        in_specs=[pl.BlockSpec((None, tm, tn), lambda i, kj, s: (gm(s)[0], i, gm(s)[1])),
                  pl.BlockSpec((None, tk, tn), lambda i, kj, s: (s // nper, kj, s % nper))],
        out_specs=[pl.BlockSpec((tm, tk), lambda i, kj, s: (i, kj))],
        out_shape=[jax.ShapeDtypeStruct((M, K), out_dtype)], args=[g3, w3],
        scratch_shapes=[pltpu.VMEM((tm, tk), F32)], comm=comm, carry_us=2.0 * M * K * P * Ns / MM_FLOPS_PER_US)[0]


def mm_tn(a, g3, wshape, *, tn, tk, name, gmap=None, comm=None):
    M, K = a.shape
    P, _, Ns = wshape
    nper = Ns // tn
    ns = P * nper
    tm = _tile(M, 512, 16)
    nm = M // tm
    gm = gmap if gmap is not None else (lambda s: (0, s))

    def body(a_ref, g_ref, o_ref, acc):
        mi = pl.program_id(2)

        @pl.when(mi == 0)
        def _():
            acc[...] = jnp.zeros_like(acc)
        acc[...] += lax.dot_general(a_ref[...].astype(BF16), g_ref[...].astype(BF16), (((0,), (0,)), ((), ())), preferred_element_type=F32)

        @pl.when(mi == nm - 1)
        def _():
            o_ref[...] = acc[...].astype(o_ref.dtype)

    return _pcall(
        body, name=name, grid=(ns, K // tk, nm),
        in_specs=[pl.BlockSpec((tm, tk), lambda s, kj, mi: (mi, kj)),
                  pl.BlockSpec((None, tm, tn), lambda s, kj, mi: (gm(s)[0], mi, gm(s)[1]))],
        out_specs=[pl.BlockSpec((None, tk, tn), lambda s, kj, mi: (s // nper, kj, s % nper))],
        out_shape=[jax.ShapeDtypeStruct((P, K, Ns), BF16)], args=[a, g3],
        scratch_shapes=[pltpu.VMEM((tk, tn), F32)], comm=comm, carry_us=2.0 * M * K * P * Ns / MM_FLOPS_PER_US)[0]


def _vspec(d):
    return pl.BlockSpec((1, d), lambda i: (0, 0))


def norm_mod(x, g, sh, sc, name):
    S, D = x.shape
    tm = _tile(S, 512, 16)

    def body(x_ref, g_ref, sh_ref, sc_ref, o_ref):
        xv = x_ref[...]
        r = lax.rsqrt(jnp.mean(xv * xv, axis=-1, keepdims=True) + EPS)
        o_ref[...] = ((xv * r) * g_ref[...] * (1.0 + sc_ref[...]) + sh_ref[...]).astype(o_ref.dtype)

    return pl.pallas_call(body, name=name, grid=(S // tm,),
                          in_specs=[pl.BlockSpec((tm, D), lambda i: (i, 0)), _vspec(D), _vspec(D), _vspec(D)],
                          out_specs=pl.BlockSpec((tm, D), lambda i: (i, 0)), out_shape=jax.ShapeDtypeStruct((S, D), BF16),
                          compiler_params=_cparams(("parallel",)))(x, g, sh, sc)


def norm_mod_bwd(dh, x, dres, g, sc, name):
    S, D = x.shape
    tm = _tile(S, 256, 8)

    def body(dh_ref, x_ref, dr_ref, g_ref, sc_ref, dx_ref, sums_ref):
        xv = x_ref[...]
        dhv = dh_ref[...].astype(F32)
        r = lax.rsqrt(jnp.mean(xv * xv, axis=-1, keepdims=True) + EPS)
        xn = xv * r
        one_sc = 1.0 + sc_ref[...]
        dxn = dhv * g_ref[...] * one_sc
        dx = r * (dxn - xn * jnp.mean(dxn * xn, axis=-1, keepdims=True))
        dx_ref[...] = dx + dr_ref[...]
        part = jnp.concatenate([jnp.sum(dhv, axis=0, keepdims=True), jnp.sum(dhv * xn * g_ref[...], axis=0, keepdims=True),
                                jnp.sum(dhv * one_sc * xn, axis=0, keepdims=True), jnp.zeros((5, D), F32)], axis=0)

        @pl.when(pl.program_id(0) == 0)
        def _():
            sums_ref[...] = jnp.zeros_like(sums_ref)
        sums_ref[...] += part

    row = pl.BlockSpec((tm, D), lambda i: (i, 0))
    return pl.pallas_call(body, name=name, grid=(S // tm,), in_specs=[row, row, row, _vspec(D), _vspec(D)],
                          out_specs=[row, pl.BlockSpec((8, D), lambda i: (0, 0))],
                          out_shape=[jax.ShapeDtypeStruct((S, D), F32), jax.ShapeDtypeStruct((8, D), F32)],
                          compiler_params=_cparams(("arbitrary",)))(dh, x, dres, g, sc)


def gate_bwd(dx, out, gate, name):
    S, D = dx.shape
    tm = _tile(S, 512, 16)

    def body(dx_ref, o_ref, g_ref, do_ref, sums_ref):
        dxv = dx_ref[...]
        do_ref[...] = (g_ref[...] * dxv).astype(do_ref.dtype)
        part = jnp.concatenate([jnp.sum(dxv * o_ref[...].astype(F32), axis=0, keepdims=True), jnp.zeros((7, D), F32)], axis=0)

        @pl.when(pl.program_id(0) == 0)
        def _():
            sums_ref[...] = jnp.zeros_like(sums_ref)
        sums_ref[...] += part

    row = pl.BlockSpec((tm, D), lambda i: (i, 0))
    return pl.pallas_call(body, name=name, grid=(S // tm,), in_specs=[row, row, _vspec(D)],
                          out_specs=[row, pl.BlockSpec((8, D), lambda i: (0, 0))],
                          out_shape=[jax.ShapeDtypeStruct((S, D), BF16), jax.ShapeDtypeStruct((8, D), F32)],
                          compiler_params=_cparams(("arbitrary",)))(dx, out, gate)


def loss_fwd_bwd(x, g, target, name):
    S, D = x.shape
    tm = _tile(S, 256, 8)

    def body(x_ref, g_ref, t_ref, dx_ref, sums_ref):
        xv = x_ref[...]
        r = lax.rsqrt(jnp.mean(xv * xv, axis=-1, keepdims=True) + EPS)
        xn = xv * r
        err = xn * g_ref[...] - t_ref[...]
        dy = err * (1.0 / D)
        dxn = dy * g_ref[...]
        dx_ref[...] = r * (dxn - xn * jnp.mean(dxn * xn, axis=-1, keepdims=True))
        part = jnp.concatenate([jnp.sum(dy * xn, axis=0, keepdims=True), jnp.sum(err * err, axis=0, keepdims=True),
                                jnp.zeros((6, D), F32)], axis=0)

        @pl.when(pl.program_id(0) == 0)
        def _():
            sums_ref[...] = jnp.zeros_like(sums_ref)
        sums_ref[...] += part

    row = pl.BlockSpec((tm, D), lambda i: (i, 0))
    return pl.pallas_call(body, name=name, grid=(S // tm,), in_specs=[row, _vspec(D), row],
                          out_specs=[row, pl.BlockSpec((8, D), lambda i: (0, 0))],
                          out_shape=[jax.ShapeDtypeStruct((S, D), F32), jax.ShapeDtypeStruct((8, D), F32)],
                          compiler_params=_cparams(("arbitrary",)))(x, g, target)


def pool_fwd(u, wgrp, scale, name):
    S, D = u.shape
    G = len(POOL_WINDOWS)
    C = D // G
    tm = _tile(S, 256, 16)
    hb = tm // HALO

    def body(up_ref, uc_ref, w_ref, sc_ref, p_ref, z_ref, y_ref):
        i = pl.program_id(0)
        prev = jnp.where(i > 0, up_ref[...], 0.0)
        ext = jnp.concatenate([prev, uc_ref[...]], axis=0)
        t = i * tm + lax.broadcasted_iota(jnp.int32, (tm, 1), 0)
        for gi, w in enumerate(POOL_WINDOWS):
            cs = slice(gi * C, (gi + 1) * C)
            e = ext[:, cs]
            s, k = e, 1
            while k < w:
                s = s + pltpu.roll(s, k, 0)
                k *= 2
            cnt = jnp.minimum(t + 1, w).astype(F32)
            pooled = (s[HALO:] / cnt - e[HALO:]).astype(BF16)
            p_ref[:, cs] = pooled
            z = jnp.dot(pooled, w_ref[:, gi].reshape(C, C), preferred_element_type=F32)
            z_ref[:, cs] = z.astype(BF16)
            y_ref[:, cs] = (z * sc_ref[:, cs]).astype(BF16)

    row = pl.BlockSpec((tm, D), lambda i: (i, 0))
    return pl.pallas_call(
        body, name=name, grid=(S // tm,),
        in_specs=[pl.BlockSpec((HALO, D), lambda i: (jnp.maximum(i * hb - 1, 0), 0)), row,
                  pl.BlockSpec(wgrp.shape, lambda i: (0, 0, 0, 0)), _vspec(D)],
        out_specs=[row, row, row], out_shape=[jax.ShapeDtypeStruct((S, D), BF16)] * 3,
        compiler_params=_cparams(("parallel",)))(u, u, wgrp, scale)


def pool_bwd(dys, z, pooled, wgrp, scale, name):
    S, D = dys.shape
    G = len(POOL_WINDOWS)
    C = D // G
    tm = _tile(S, 256, 16)
    hb = tm // HALO
    nt = S // tm
    n_ext = tm + HALO

    def body(dc_ref, dn_ref, z_ref, p_ref, w_ref, sc_ref, du_ref, dw_ref, sums_ref):
        i = pl.program_id(0)

        @pl.when(i == 0)
        def _():
            dw_ref[...] = jnp.zeros_like(dw_ref)
            sums_ref[...] = jnp.zeros_like(sums_ref)
        dyc = dc_ref[...].astype(F32)
        nxt = jnp.where(i < nt - 1, dn_ref[...].astype(F32), 0.0)
        ext = jnp.concatenate([dyc, nxt], axis=0)
        sums_ref[...] += jnp.concatenate([jnp.sum(dyc * z_ref[...].astype(F32), axis=0, keepdims=True), jnp.zeros((7, D), F32)], axis=0)
        t = i * tm + lax.broadcasted_iota(jnp.int32, (n_ext, 1), 0)
        for gi, w in enumerate(POOL_WINDOWS):
            cs = slice(gi * C, (gi + 1) * C)
            wg = w_ref[:, gi].reshape(C, C)
            dz = (ext[:, cs] * sc_ref[:, cs]).astype(BF16)
            dpool = lax.dot_general(dz, wg, (((1,), (1,)), ((), ())), preferred_element_type=F32)
            dw_ref[gi] += lax.dot_general(p_ref[:, cs], dz[:tm], (((0,), (0,)), ((), ())), preferred_element_type=F32)
            cnt = jnp.minimum(t + 1, w).astype(F32)
            s, k = dpool / cnt, 1
            while k < w:
                s = s + pltpu.roll(s, n_ext - k, 0)
                k *= 2
            du_ref[:, cs] = (s[:tm] - dpool[:tm]).astype(BF16)

    row = pl.BlockSpec((tm, D), lambda i: (i, 0))
    return pl.pallas_call(
        body, name=name, grid=(nt,),
        in_specs=[row, pl.BlockSpec((HALO, D), lambda i: (jnp.minimum((i + 1) * hb, S // HALO - 1), 0)), row, row,
                  pl.BlockSpec(wgrp.shape, lambda i: (0, 0, 0, 0)), _vspec(D)],
        out_specs=[row, pl.BlockSpec((G, C, C), lambda i: (0, 0, 0)), pl.BlockSpec((8, D), lambda i: (0, 0))],
        out_shape=[jax.ShapeDtypeStruct((S, D), BF16), jax.ShapeDtypeStruct((G, C, C), F32), jax.ShapeDtypeStruct((8, D), F32)],
        compiler_params=_cparams(("arbitrary",)))(dys, dys, z, pooled, wgrp, scale)


FFN_ACT_US_PER_ELEM = (48.0 / (4096 * 2816), 84.0 / (4096 * 2816))


def ffn_act(hu, conv_w, conv_b, name, comm=None):
    S, F2 = hu.shape
    F = F2 // 2
    tm = _tile(S, 256, 16)
    tn = _tile(F, 1408, LANES)
    nb = F // tn
    hb = tm // HALO

    def body(ap_ref, ac_ref, v_ref, w_ref, b_ref, o_ref):
        i = pl.program_id(0)
        prev = jnp.where(i > 0, ap_ref[...].astype(F32), 0.0)
        ext = jnp.concatenate([prev, ac_ref[...].astype(F32)], axis=0)
        conv = b_ref[...] + pltpu.roll(ext, 2, 0) * w_ref[0:1, :] + pltpu.roll(ext, 1, 0) * w_ref[1:2, :] + ext * w_ref[2:3, :]
        conv = conv[HALO:]
        o_ref[...] = (conv * _sigmoid(conv) * v_ref[...].astype(F32)).astype(o_ref.dtype)

    return _pcall(
        body, name=name, grid=(S // tm, nb),
        in_specs=[pl.BlockSpec((HALO, tn), lambda i, j: (jnp.maximum(i * hb - 1, 0), j)), pl.BlockSpec((tm, tn), lambda i, j: (i, j)),
                  pl.BlockSpec((tm, tn), lambda i, j: (i, j + nb)), pl.BlockSpec((3, tn), lambda i, j: (0, j)),
                  pl.BlockSpec((1, tn), lambda i, j: (0, j))],
        out_specs=[pl.BlockSpec((tm, tn), lambda i, j: (i, j))], out_shape=[jax.ShapeDtypeStruct((S, F), BF16)],
        args=[hu, hu, hu, conv_w, conv_b], comm=comm, carry_us=FFN_ACT_US_PER_ELEM[0] * S * F)[0]


def ffn_act_bwd(dg, hu, conv_w, conv_b, name, comm=None):
    S, F = dg.shape
    tm = _tile(S, 256, 16)
    tn = _tile(F, 1408, LANES)
    nb = F // tn
    hb = tm // HALO
    nt = S // tm
    n_ext = tm + 2 * HALO

    def body(gc_ref, gn_ref, ap_ref, ac_ref, an_ref, vc_ref, vn_ref, w_ref, b_ref, o_ref, sums_ref):
        i = pl.program_id(1)

        @pl.when(i == 0)
        def _():
            sums_ref[...] = jnp.zeros_like(sums_ref)
        zeros = jnp.zeros((HALO, tn), F32)
        not_last = i < nt - 1
        a_ext = jnp.concatenate([jnp.where(i > 0, ap_ref[...].astype(F32), 0.0), ac_ref[...].astype(F32), an_ref[...].astype(F32)], axis=0)
        v_ext = jnp.concatenate([zeros, vc_ref[...].astype(F32), vn_ref[...].astype(F32)], axis=0)
        g_ext = jnp.concatenate([zeros, gc_ref[...].astype(F32), jnp.where(not_last, gn_ref[...].astype(F32), 0.0)], axis=0)
        w0, w1, w2 = w_ref[0:1, :], w_ref[1:2, :], w_ref[2:3, :]
        a_m2, a_m1 = pltpu.roll(a_ext, 2, 0), pltpu.roll(a_ext, 1, 0)
        conv = b_ref[...] + a_m2 * w0 + a_m1 * w1 + a_ext * w2
        sig = _sigmoid(conv)
        silu = conv * sig
        dsilu = sig * (1.0 + conv * (1.0 - sig))
        dconv = g_ext * v_ext * dsilu
        da = dconv * w2 + pltpu.roll(dconv, n_ext - 1, 0) * w1 + pltpu.roll(dconv, n_ext - 2, 0) * w0
        cur = slice(HALO, HALO + tm)
        o_ref[0] = da[cur].astype(o_ref.dtype)
        o_ref[1] = (g_ext * silu)[cur].astype(o_ref.dtype)
        dc = dconv[cur]
        part = jnp.concatenate([jnp.sum(dc * a_m2[cur], axis=0, keepdims=True), jnp.sum(dc * a_m1[cur], axis=0, keepdims=True),
                                jnp.sum(dc * a_ext[cur], axis=0, keepdims=True), jnp.sum(dc, axis=0, keepdims=True),
                                jnp.zeros((4, tn), F32)], axis=0)
        sums_ref[...] += part

    def prev(i):
        return jnp.maximum(i * hb - 1, 0)

    def nxt(i):
        return jnp.minimum((i + 1) * hb, S // HALO - 1)

    return _pcall(
        body, name=name, grid=(nb, nt),
        in_specs=[pl.BlockSpec((tm, tn), lambda j, i: (i, j)), pl.BlockSpec((HALO, tn), lambda j, i: (nxt(i), j)),
                  pl.BlockSpec((HALO, tn), lambda j, i: (prev(i), j)), pl.BlockSpec((tm, tn), lambda j, i: (i, j)),
                  pl.BlockSpec((HALO, tn), lambda j, i: (nxt(i), j)),
                  pl.BlockSpec((tm, tn), lambda j, i: (i, j + nb)), pl.BlockSpec((HALO, tn), lambda j, i: (nxt(i), j + nb)),
                  pl.BlockSpec((3, tn), lambda j, i: (0, j)), pl.BlockSpec((1, tn), lambda j, i: (0, j))],
        out_specs=[pl.BlockSpec((2, tm, tn), lambda j, i: (0, i, j)), pl.BlockSpec((8, tn), lambda j, i: (0, j))],
        out_shape=[jax.ShapeDtypeStruct((2, S, F), BF16), jax.ShapeDtypeStruct((8, F), F32)],
        args=[dg, dg, hu, hu, hu, hu, hu, conv_w, conv_b], comm=comm, carry_us=FFN_ACT_US_PER_ELEM[1] * S * F)


def _head_expander(n_heads, da):
    e = np.zeros((LANES, da), np.float32)
    for h in range(n_heads):
        e[h, h * HEAD_DIM:(h + 1) * HEAD_DIM] = 1.0
    return jnp.asarray(e, BF16)


def _split_dot(v, e, dims):
    hi = v.astype(BF16)
    lo = (v - hi.astype(F32)).astype(BF16)
    return (lax.dot_general(hi, e, dims, preferred_element_type=F32) + lax.dot_general(lo, e, dims, preferred_element_type=F32))


def _lane_col(tile, h):
    lane = lax.broadcasted_iota(jnp.int32, tile.shape, 1)
    return jnp.sum(jnp.where(lane == h, tile, 0.0), axis=1, keepdims=True)


ATTN_US_PER_ELEM = (80.0 / (4096 * 1024), 230.0 / (4096 * 1024))


def attn_branch_fwd(q, kv, gi, slopes, name, comm=None):
    S, DA = q.shape
    H = DA // HEAD_DIM
    window, d = BRANCHES[gi]
    n_steps = window // d
    blk = ATTN_BLOCK
    assert n_steps == blk and (S // d) % blk == 0
    nbs = S // d // blk
    scale = HEAD_DIM ** -0.5

    def body(q_ref, kp_ref, kc_ref, vp_ref, vc_ref, o_ref, l_ref, s_scr, p_scr):
        jb = pl.program_id(1)
        row = lax.broadcasted_iota(jnp.int32, (blk, 2 * blk), 0)
        col = lax.broadcasted_iota(jnp.int32, (blk, 2 * blk), 1)
        delta = row + blk - col
        valid = (delta >= 0) & (delta <= n_steps) & ((col >= blk) | (jb > 0))
        dist = jnp.where(valid, (delta * d).astype(F32), -NEG)
        lane = lax.broadcasted_iota(jnp.int32, (blk, LANES), 1)
        ltile = jnp.zeros((blk, LANES), F32)
        for h in range(H):
            hs = slice(h * HEAD_DIM, (h + 1) * HEAD_DIM)
            k2 = jnp.concatenate([kp_ref[:, hs], kc_ref[:, hs]], axis=0)
            s_scr[h] = lax.dot_general(q_ref[:, hs], k2, (((1,), (1,)), ((), ())), preferred_element_type=F32)
        for h in range(H):
            s = s_scr[h] * scale - float(slopes[h]) * dist
            m = jnp.max(s, axis=-1, keepdims=True)
            p = jnp.exp(s - m)
            l = jnp.sum(p, axis=-1, keepdims=True)
            p_scr[h] = (p / l).astype(BF16)
            ltile = jnp.where(lane == h, m + jnp.log(l), ltile)
        for h in range(H):
            hs = slice(h * HEAD_DIM, (h + 1) * HEAD_DIM)
            v2 = jnp.concatenate([vp_ref[:, hs], vc_ref[:, hs]], axis=0)
            o_ref[:, hs] = jnp.dot(p_scr[h], v2, preferred_element_type=F32)
        l_ref[...] = ltile

    def cur(width, off):
        return pl.BlockSpec((blk, width), lambda r, jb: (r * nbs + jb, off))

    def prv(width, off):
        return pl.BlockSpec((blk, width), lambda r, jb: (r * nbs + jnp.maximum(jb - 1, 0), off))

    return _pcall(
        body, name=name, grid=(d, nbs),
        in_specs=[cur(DA, 0), prv(DA, 0), cur(DA, 0), prv(DA, 1), cur(DA, 1)],
        out_specs=[cur(DA, 0), cur(LANES, 0)],
        out_shape=[jax.ShapeDtypeStruct((S, DA), F32), jax.ShapeDtypeStruct((S, LANES), F32)],
        scratch_shapes=[pltpu.VMEM((H, blk, 2 * blk), F32), pltpu.VMEM((H, blk, 2 * blk), BF16)],
        args=[q, kv, kv, kv, kv], comm=comm, carry_us=ATTN_US_PER_ELEM[0] * S * DA)


def attn_combine(os_, lses, name):
    S, DA = os_[0].shape
    H = DA // HEAD_DIM
    tm = _tile(S, 256, 16)
    expander = _head_expander(H, DA)
    nbr = len(os_)

    def body(*refs):
        o_refs, l_refs, e_ref = refs[:nbr], refs[nbr:2 * nbr], refs[2 * nbr]
        out_ref, lse_ref = refs[2 * nbr + 1:]
        ls = [r[...] for r in l_refs]
        lmax = functools.reduce(jnp.maximum, ls)
        es = [jnp.exp(l - lmax) for l in ls]
        den = functools.reduce(lambda a, b: a + b, es)
        lse_ref[...] = lmax + jnp.log(den)
        acc = jnp.zeros((tm, DA), F32)
        for e, o_ref in zip(es, o_refs):
            acc = acc + _split_dot(e / den, e_ref[...], (((1,), (0,)), ((), ()))) * o_ref[...]
        out_ref[...] = acc.astype(out_ref.dtype)

    row = pl.BlockSpec((tm, DA), lambda i: (i, 0))
    lrow = pl.BlockSpec((tm, LANES), lambda i: (i, 0))
    return pl.pallas_call(
        body, name=name, grid=(S // tm,),
        in_specs=[row] * nbr + [lrow] * nbr + [pl.BlockSpec((LANES, DA), lambda i: (0, 0))],
        out_specs=[row, lrow], out_shape=[jax.ShapeDtypeStruct((S, DA), BF16), jax.ShapeDtypeStruct((S, LANES), F32)],
        compiler_params=_cparams(("parallel",)))(*os_, *lses, expander)


def attn_delta(do, o, name):
    S, DA = o.shape
    H = DA // HEAD_DIM
    tm = _tile(S, 512, 16)
    expander = _head_expander(H, DA)

    def body(do_ref, o_ref, e_ref, d_ref):
        prod = do_ref[...].astype(F32) * o_ref[...].astype(F32)
        d_ref[...] = _split_dot(prod, e_ref[...], (((1,), (1,)), ((), ())))

    row = pl.BlockSpec((tm, DA), lambda i: (i, 0))
    return pl.pallas_call(body, name=name, grid=(S // tm,), in_specs=[row, row, pl.BlockSpec((LANES, DA), lambda i: (0, 0))],
                          out_specs=pl.BlockSpec((tm, LANES), lambda i: (i, 0)), out_shape=jax.ShapeDtypeStruct((S, LANES), F32),
                          compiler_params=_cparams(("parallel",)))(do, o, expander)


def attn_branch_bwd(q, kv, do, lse, dlt, dkv_prev, gi, slopes, name, out_cols=None, comm=None):
    S, DA = q.shape
    H = DA // HEAD_DIM
    window, d = BRANCHES[gi]
    n_steps = window // d
    blk = ATTN_BLOCK
    nbs = S // d // blk
    scale = HEAD_DIM ** -0.5
    accumulate = dkv_prev is not None
    nt, tn = (((1,), (1,)), ((), ())), (((0,), (0,)), ((), ()))

    def body(*refs):
        k_ref, v_ref, qc_ref, qn_ref, doc_ref, don_ref, lc_ref, ln_ref, dc_ref, dn_ref = refs[:10]
        dq_ref, dkv_ref, carry, s_scr, dp_scr, p_scr, ds_scr = refs[-7:]
        kb = pl.program_id(1)

        @pl.when(kb == 0)
        def _():
            carry[...] = jnp.zeros_like(carry)
        row = lax.broadcasted_iota(jnp.int32, (2 * blk, blk), 0)
        col = lax.broadcasted_iota(jnp.int32, (2 * blk, blk), 1)
        delta = row - col
        valid = (delta >= 0) & (delta <= n_steps) & ((row < blk) | (kb < nbs - 1))
        dist = jnp.where(valid, (delta * d).astype(F32), -NEG)
        l2 = jnp.concatenate([lc_ref[...], ln_ref[...]], axis=0)
        d2 = jnp.concatenate([dc_ref[...], dn_ref[...]], axis=0)
        for h in range(H):
            hs = slice(h * HEAD_DIM, (h + 1) * HEAD_DIM)
            q2 = jnp.concatenate([qc_ref[:, hs], qn_ref[:, hs]], axis=0)
            do2 = jnp.concatenate([doc_ref[:, hs], don_ref[:, hs]], axis=0)
            s_scr[h] = lax.dot_general(q2, k_ref[:, hs], nt, preferred_element_type=F32)
            dp_scr[h] = lax.dot_general(do2, v_ref[:, hs], nt, preferred_element_type=F32)
        for h in range(H):
            p = jnp.exp(s_scr[h] * scale - float(slopes[h]) * dist - _lane_col(l2, h))
            p_scr[h] = p.astype(BF16)
            ds_scr[h] = (p * (dp_scr[h] - _lane_col(d2, h))).astype(BF16)
        for h in range(H):
            hs = slice(h * HEAD_DIM, (h + 1) * HEAD_DIM)
            vs = slice(DA + h * HEAD_DIM, DA + (h + 1) * HEAD_DIM)
            q2 = jnp.concatenate([qc_ref[:, hs], qn_ref[:, hs]], axis=0)
            do2 = jnp.concatenate([doc_ref[:, hs], don_ref[:, hs]], axis=0)
            dvh = lax.dot_general(p_scr[h], do2, tn, preferred_element_type=F32)
            dkh = lax.dot_general(ds_scr[h], q2, tn, preferred_element_type=F32) * scale
            dq2 = jnp.dot(ds_scr[h], k_ref[:, hs], preferred_element_type=F32) * scale
            dq_ref[:, hs] = (carry[:, hs] + dq2[:blk]).astype(dq_ref.dtype)
            carry[:, hs] = dq2[blk:]
            if accumulate:
                dkh = dkh + refs[10][:, hs]
                dvh = dvh + refs[10][:, vs]
            dkv_ref[:, hs] = dkh
            dkv_ref[:, vs] = dvh

    def cur(width, off):
        return pl.BlockSpec((blk, width), lambda r, kb: (r * nbs + kb, off))

    def nxt(width, off):
        return pl.BlockSpec((blk, width), lambda r, kb: (r * nbs + jnp.minimum(kb + 1, nbs - 1), off))

    in_specs = [cur(DA, 0), cur(DA, 1), cur(DA, 0), nxt(DA, 0), cur(DA, 0), nxt(DA, 0),
                cur(LANES, 0), nxt(LANES, 0), cur(LANES, 0), nxt(LANES, 0)]
    args = [kv, kv, q, q, do, do, lse, lse, dlt, dlt]
    aliases = {}
    if accumulate:
        in_specs.append(cur(2 * DA, 0))
        args.append(dkv_prev)
        aliases = {10: 1}
    return _pcall(
        body, name=name, grid=(d, nbs), in_specs=in_specs, out_specs=[cur(DA, 0), cur(2 * DA, 0)],
        out_shape=[jax.ShapeDtypeStruct((S, out_cols or DA), BF16), jax.ShapeDtypeStruct((S, 2 * DA), F32)],
        scratch_shapes=[pltpu.VMEM((blk, DA), F32), pltpu.VMEM((H, 2 * blk, blk), F32), pltpu.VMEM((H, 2 * blk, blk), F32),
                        pltpu.VMEM((H, 2 * blk, blk), BF16), pltpu.VMEM((H, 2 * blk, blk), BF16)],
        aliases=aliases, args=args, comm=comm, carry_us=ATTN_US_PER_ELEM[1] * S * DA)


def ada_project(c16, w3, b3, name):
    L, D, Ns = w3.shape
    tn = _tile(Ns, 512, LANES)

    def body(c_ref, w_ref, b_ref, o_ref):
        cv = c_ref[...]
        cond = (cv * _sigmoid(cv)).astype(BF16)
        o_ref[...] = jnp.dot(cond, w_ref[...].astype(BF16), preferred_element_type=F32) + b_ref[...]

    return pl.pallas_call(
        body, name=name, grid=(L, Ns // tn),
        in_specs=[pl.BlockSpec((16, D), lambda l, j: (0, 0)), pl.BlockSpec((None, D, tn), lambda l, j: (l, 0, j)),
                  pl.BlockSpec((None, 1, tn), lambda l, j: (l, 0, j))],
        out_specs=pl.BlockSpec((None, 16, tn), lambda l, j: (l, 0, j)), out_shape=jax.ShapeDtypeStruct((L, 16, Ns), F32),
        compiler_params=_cparams(("parallel", "parallel")))(c16, w3, b3)


def _adamw(w, g, m, v):
    m = B1 * m + (1.0 - B1) * g
    v = B2 * v + (1.0 - B2) * (g * g)
    m_hat = m / (1.0 - B1 ** STEP)
    v_hat = v / (1.0 - B2 ** STEP)
    delta = -LR * (m_hat / (jnp.sqrt(v_hat) + ADAM_EPS) + WD * w)
    return delta, m, v


def ada_grad_adamw(c16, d3, w3, m3, v3, name):
    L, D, Ns = w3.shape
    tk = _tile(D, 256, 8)

    def body(c_ref, d_ref, w_ref, m_ref, v_ref, g_out, dl_out, m_out, v_out):
        cv = c_ref[...]
        cond = (cv * _sigmoid(cv)).astype(BF16)
        g = lax.dot_general(cond, d_ref[...].astype(BF16), (((0,), (0,)), ((), ())), preferred_element_type=F32)
        g_out[...] = g
        dl_out[...], m_out[...], v_out[...] = _adamw(w_ref[...], g, m_ref[...], v_ref[...])

    wspec = pl.BlockSpec((None, tk, Ns), lambda l, kj: (l, kj, 0))
    return pl.pallas_call(
        body, name=name, grid=(L, D // tk),
        in_specs=[pl.BlockSpec((16, tk), lambda l, kj: (0, kj)), pl.BlockSpec((None, 16, Ns), lambda l, kj: (l, 0, 0)), wspec, wspec, wspec],
        out_specs=[wspec] * 4, out_shape=[jax.ShapeDtypeStruct((L, D, Ns), F32)] * 4,
        compiler_params=_cparams(("parallel", "parallel")))(c16, d3, w3, m3, v3)


def adamw(w, g, m, v, name):
    R, C = w.shape
    tr = _tile(R, 256, 8)

    def body(w_ref, g_ref, m_ref, v_ref, g_out, dl_out, m_out, v_out):
        g = g_ref[...]
        g_out[...] = g
        dl_out[...], m_out[...], v_out[...] = _adamw(w_ref[...], g, m_ref[...], v_ref[...])

    spec = pl.BlockSpec((tr, C), lambda i: (i, 0))
    return pl.pallas_call(body, name=name, grid=(R // tr,), in_specs=[spec] * 4, out_specs=[spec] * 4,
                          out_shape=[jax.ShapeDtypeStruct((R, C), F32)] * 4, compiler_params=_cparams(("parallel",)))(w, g, m, v)


def sum_partials(own, recv, g_prev, layer, n_layers, pos, name):
    _, Rh, C = recv.shape
    tr = _tile(Rh, 256, 16)

    def body(pos_ref, own_ref, recv_ref, *rest):
        acc = own_ref[...].astype(F32)
        for rel in range(7):
            acc = acc + recv_ref[rel].astype(F32)
        rest[-1][...] = acc

    in_specs = [pl.BlockSpec((None, None, tr, C), lambda r, pos: (pos[1], pos[0], r, 0)), pl.BlockSpec((7, tr, C), lambda r, pos: (0, r, 0))]
    args = [pos, own.reshape(N_CHIPS, 2, Rh, C), recv]
    aliases = {}
    if g_prev is not None:
        in_specs.append(pl.BlockSpec(memory_space=pl.ANY))
        args.append(g_prev)
        aliases = {3: 0}
    return pl.pallas_call(
        body, name=name,
        grid_spec=pltpu.PrefetchScalarGridSpec(
            num_scalar_prefetch=1, grid=(Rh // tr,), in_specs=in_specs,
            out_specs=pl.BlockSpec((None, None, tr, C), lambda r, pos: (layer, pos[0], r, 0))),
        out_shape=jax.ShapeDtypeStruct((n_layers, 2, Rh, C), F32), input_output_aliases=aliases,
        compiler_params=_cparams(("parallel",)))(*args)


def sum_rows8(g8, name):
    _, R, C = g8.shape

    def body(g_ref, o_ref):
        acc = g_ref[0]
        for i in range(1, N_DEV):
            acc = acc + g_ref[i]
        o_ref[...] = acc

    return pl.pallas_call(body, name=name, grid=(1,), in_specs=[pl.BlockSpec((N_DEV, R, C), lambda i: (0, 0, 0))],
                          out_specs=pl.BlockSpec((R, C), lambda i: (0, 0)), out_shape=jax.ShapeDtypeStruct((R, C), F32),
                          compiler_params=_cparams(("arbitrary",)))(g8)


def _pack(vecs):
    flat = [v.reshape(-1).astype(F32) for v in vecs]
    sizes = [f.shape[0] for f in flat]
    total = sum(sizes)
    padded = -(-total // (8 * PACK_W)) * (8 * PACK_W)
    buf = jnp.concatenate(flat + [jnp.zeros((padded - total,), F32)])
    offs = np.concatenate([[0], np.cumsum(sizes)])
    return buf.reshape(-1, PACK_W), offs


def _unpack(buf, offs, shapes):
    flat = buf.reshape(-1)
    return [flat[int(offs[i]):int(offs[i + 1])].reshape(s) for i, s in enumerate(shapes)]


def kernel(x, c, ada_w, ada_b, norm1_g, norm2_g, pool_w_in, pool_w_grp, pool_scale, pool_w_out, kv_norm_g, kv_ada_w, kv_ada_b, w_kv, attn_w_q, attn_w_o, ffn_w_up, ffn_conv_w, ffn_conv_b, ffn_w_down, final_g, loss_target, m_ada_w, m_ada_b, m_norm1_g, m_norm2_g, m_pool_w_in, m_pool_w_grp, m_pool_scale, m_pool_w_out, m_kv_norm_g, m_kv_ada_w, m_kv_ada_b, m_w_kv, m_attn_w_q, m_attn_w_o, m_ffn_w_up, m_ffn_conv_w, m_ffn_conv_b, m_ffn_w_down, m_final_g, v_ada_w, v_ada_b, v_norm1_g, v_norm2_g, v_pool_w_in, v_pool_w_grp, v_pool_scale, v_pool_w_out, v_kv_norm_g, v_kv_ada_w, v_kv_ada_b, v_w_kv, v_attn_w_q, v_attn_w_o, v_ffn_w_up, v_ffn_conv_w, v_ffn_conv_b, v_ffn_w_down, v_final_g):
    S, D = x.shape[1], x.shape[2]
    depth = ada_w.shape[0]
    n_pool = pool_w_in.shape[0]
    n_attn = attn_w_q.shape[0]
    G = len(POOL_WINDOWS)
    NB = len(BRANCHES)
    DA = attn_w_o.shape[1] * N_CHIPS
    H = DA // HEAD_DIM
    F = ffn_conv_b.shape[1]
    Fs = F // N_CHIPS
    Dq = D // N_CHIPS
    ada_ns = ada_w.shape[2]
    kvada_ns = kv_ada_w.shape[1]
    slopes = _alibi_slopes(NB * H).reshape(NB, H)

    ix, iy, ic = lax.axis_index("x"), lax.axis_index("y"), lax.axis_index("c")
    p_me = 2 * ix + iy
    b_me = 4 * ix + 2 * iy + ic
    pos = jnp.stack([ic, p_me]).astype(jnp.int32)
    xs, tgt = x[0], loss_target[0]

    pk, offs = _pack([c, pool_scale, ffn_conv_w])
    rows1 = pk.shape[0]
    got = all_gather8(pk, "gather_small_in").reshape(N_DEV, rows1, PACK_W)
    c8 = got.reshape(N_DEV, -1)[:, :D]
    c16 = jnp.concatenate([c8, jnp.zeros_like(c8)], axis=0)
    chip_rows = got[0::2].reshape(N_CHIPS, -1)
    scale_full = chip_rows[:, int(offs[1]):int(offs[2])].reshape(N_CHIPS, n_pool, Dq).transpose(1, 0, 2).reshape(n_pool, D)
    convw_full = chip_rows[:, int(offs[2]):int(offs[3])].reshape(N_CHIPS, depth, 3, Fs).transpose(1, 2, 0, 3).reshape(depth, 3, F)

    ada_b_loc = lax.dynamic_slice(ada_b, (0, p_me * ada_ns), (depth, ada_ns)).reshape(depth, 1, ada_ns)
    kvb_loc = lax.dynamic_slice(kv_ada_b, (p_me * kvada_ns,), (kvada_ns,)).reshape(1, 1, kvada_ns)
    mods_loc = ada_project(c16, ada_w, ada_b_loc, "ada_project")[:, :N_DEV]
    kvmod_loc = ada_project(c16, kv_ada_w.reshape(1, D, kvada_ns), kvb_loc, "kv_ada_project")[0, :N_DEV]
    mods_cat = jnp.concatenate([mods_loc.transpose(1, 0, 2).reshape(N_DEV, depth * ada_ns), kvmod_loc], axis=1)
    mods_all = all_gather8(mods_cat, "gather_mods").reshape(N_CHIPS, 2, N_DEV, -1)
    mine = lax.dynamic_index_in_dim(mods_all[:, 0], b_me, axis=1, keepdims=False)
    mod = mine[:, :depth * ada_ns].reshape(N_CHIPS, depth, ada_ns).transpose(1, 0, 2).reshape(depth, 6, 1, D)
    kvmod = mine[:, depth * ada_ns:].reshape(2, 1, D)

    comm = _Comm()
    C = D // G
    kv_ns, q_ns, up_ns = w_kv.shape[1], attn_w_q.shape[2], ffn_w_up.shape[2]

    def layer_shards(l):
        sh = []
        if l < n_pool:
            sh += [(("pin", l), pool_w_in[l]), (("pgrp", l), pool_w_grp[l].reshape(-1, C)), (("pout", l), pool_w_out[l])]
        else:
            if l == n_pool:
                sh.append((("kv", 0), w_kv))
            sh += [(("wq", l), attn_w_q[l - n_pool]), (("wo", l), attn_w_o[l - n_pool])]
        sh += [(("up", l), ffn_w_up[l]), (("down", l), ffn_w_down[l])]
        return [(k, w.astype(BF16)) for k, w in sh]

    def weight(key, shape):
        return comm.require(key).reshape(shape)

    dil = [d for _, d in BRANCHES]
    kv_tn = DA // 2
    q_tn = DA // 4
    q_bwd_tn = q_ns
    up_tn = up_ns
    up_per_half = F // up_tn

    def up_gmap(s):
        return s // up_per_half, s % up_per_half

    def vec(v):
        return v.reshape(1, -1)

    saved = []
    xcur = xs
    kvs = None
    wts = {}
    push_gather(comm, layer_shards(0))
    comm.flush()
    for l in range(depth):
        if l + 1 < depth:
            push_gather(comm, layer_shards(l + 1))
        sh1, sc1, g1, sh2, sc2, g2 = [mod[l, i] for i in range(6)]
        st = {"x0": xcur}
        h1 = norm_mod(xcur, vec(norm1_g[l]), sh1, sc1, "norm_mod")
        st["h1"] = h1
        if l < n_pool:
            wts["pin", l] = weight(("pin", l), (1, D, D))
            u = mm_nn(h1, wts["pin", l], tn=D // 2, out_dtype=F32, name="pool_in_proj", comm=comm)
            wts["pgrp", l] = weight(("pgrp", l), (N_CHIPS, G, C // N_CHIPS, C))
            pooled, z, ys = pool_fwd(u, wts["pgrp", l], vec(scale_full[l]), "pool_mix")
            wts["pout", l] = weight(("pout", l), (1, D, D))
            out, x1 = mm_nn(ys, wts["pout", l], tn=D // 2, out_dtype=BF16, name="pool_out_proj", res=(xcur, g1), comm=comm)
            st.update(pooled=pooled, z=z, ys=ys, out1=out)
        else:
            if l == n_pool:
                wts["kv", 0] = weight(("kv", 0), (N_CHIPS, D, kv_ns))
                hkv = norm_mod(xcur, vec(kv_norm_g), kvmod[0], kvmod[1], "norm_mod")
                kvs = [mm_nn(hkv, wts["kv", 0], tn=kv_tn, out_dtype=BF16, name=f"kv_proj_b{gi}", ncb=4, perm_d=dil[gi], comm=comm,
                             cbmap=functools.partial(lambda jj, gi: 2 * gi + (jj // 2) * 2 * NB + jj % 2, gi=gi)) for gi in range(NB)]
                kv_state = {"x": xcur, "hkv": hkv}
            wts["wq", l] = weight(("wq", l), (N_CHIPS, D, q_ns))
            qs, os_, lses = [], [], []
            for gi in range(NB):
                q_b = mm_nn(h1, wts["wq", l], tn=q_tn, out_dtype=BF16, name=f"q_proj_b{gi}", ncb=4, perm_d=dil[gi], comm=comm,
                            cbmap=functools.partial(lambda jj, gi: 4 * gi + jj, gi=gi))
                o_b, l_b = attn_branch_fwd(q_b, kvs[gi], gi, slopes[gi], f"attn_fwd_b{gi}", comm=comm)
                if dil[gi] > 1:
                    o_b = unpermute_rows(o_b, dil[gi], f"unpermute_o_b{gi}")
                    l_b = unpermute_rows(l_b, dil[gi], f"unpermute_lse_b{gi}")
                qs.append(q_b)
                os_.append(o_b)
                lses.append(l_b)
            o, lse = attn_combine(os_, lses, "attn_combine")
            wts["wo", l] = weight(("wo", l), (1, DA, D))
            out, x1 = mm_nn(o, wts["wo", l], tn=D // 2, out_dtype=BF16, name="attn_out_proj", res=(xcur, g1), comm=comm)
            st.update(qs=qs, o=o, lse=lse, out1=out)
        st["x1"] = x1
        h2 = norm_mod(x1, vec(norm2_g[l]), sh2, sc2, "norm_mod")
        wts["up", l] = weight(("up", l), (N_CHIPS, D, up_ns))
        hu = mm_nn(h2, wts["up", l], tn=up_tn, out_dtype=BF16, name="ffn_up_proj", comm=comm)
        gated = ffn_act(hu, convw_full[l], vec(ffn_conv_b[l]), "ffn_act", comm=comm)
        wts["down", l] = weight(("down", l), (1, F, D))
        out2, x2 = mm_nn(gated, wts["down", l], tn=D // 2, out_dtype=BF16, name="ffn_down_proj", res=(x1, g2), comm=comm)
        st.update(h2=h2, hu=hu, gated=gated, out2=out2)
        saved.append(st)
        xcur = x2
    comm.flush()

    dx, fsums = loss_fwd_bwd(xcur, vec(final_g), tgt, "loss_head")
    loss = lax.psum(0.5 * jnp.sum(fsums[1]) / D, ("x", "y", "c"))
    d_final_g = fsums[0]

    dmods = [None] * depth
    d_n1 = [None] * depth
    d_n2 = [None] * depth
    d_convw = [None] * depth
    d_convb = [None] * depth
    d_scale = [None] * n_pool
    d_grp = [None] * n_pool
    dkvs = [None] * NB
    exchanged = []
    f_tk = _tile(F, 1408, LANES)
    ct_blocks = DA // _tile(DA, 256, LANES)

    def exchange(name, idx, dw):
        dw4 = dw.reshape(N_CHIPS, -1, dw.shape[-1])
        exchanged.append((name, idx, dw4))
        push_exchange(comm, (name, idx), dw4)

    for l in reversed(range(depth)):
        st = saved[l]
        sh1, sc1, g1, sh2, sc2, g2 = [mod[l, i] for i in range(6)]
        dout2, s_g2 = gate_bwd(dx, st["out2"], g2, "gate_bwd")
        dout2_3 = dout2.reshape(1, S, D)
        dgated = mm_nt(dout2_3, wts["down", l], tn=D, tk=f_tk, out_dtype=BF16, name="ffn_down_bwd", comm=comm)
        exchange("down", l, mm_tn(st["gated"], dout2_3, (1, F, D), tn=D, tk=f_tk, name="ffn_down_dw", comm=comm))
        dhu, s_conv = ffn_act_bwd(dgated, st["hu"], convw_full[l], vec(ffn_conv_b[l]), "ffn_act_bwd", comm=comm)
        dh2 = mm_nt(dhu, wts["up", l], tn=up_tn, tk=D, out_dtype=F32, name="ffn_up_bwd", gmap=up_gmap, comm=comm)
        exchange("up", l, mm_tn(st["h2"], dhu, (N_CHIPS, D, up_ns), tn=up_tn, tk=D, name="ffn_up_dw", gmap=up_gmap, comm=comm))
        dx, s_n2 = norm_mod_bwd(dh2, st["x1"], dx, vec(norm2_g[l]), sc2, "norm_mod_bwd")
        d_convw[l], d_convb[l] = s_conv[0:3], s_conv[3]
        d_n2[l] = s_n2[2]
        dout1, s_g1 = gate_bwd(dx, st["out1"], g1, "gate_bwd")
        dout1_3 = dout1.reshape(1, S, D)
        if l < n_pool:
            dys = mm_nt(dout1_3, wts["pout", l], tn=D, tk=D // 2, out_dtype=F32, name="pool_out_bwd", comm=comm)
            exchange("pout", l, mm_tn(st["ys"], dout1_3, (1, D, D), tn=D, tk=D, name="pool_out_dw", comm=comm))
            du, d_grp, s_sc = pool_bwd(dys, st["z"], st["pooled"], wts["pgrp", l], vec(scale_full[l]), "pool_mix_bwd")
            exchange("pgrp", l, d_grp.astype(BF16).reshape(G, N_CHIPS, C // N_CHIPS, C).transpose(1, 0, 2, 3))
            d_scale[l] = s_sc[0]
            du_3 = du.reshape(1, S, D)
            dh1 = mm_nt(du_3, wts["pin", l], tn=D, tk=D // 2, out_dtype=F32, name="pool_in_bwd", comm=comm)
            exchange("pin", l, mm_tn(st["h1"], du_3, (1, D, D), tn=D, tk=D, name="pool_in_dw", comm=comm))
        else:
            j = l - n_pool
            do = mm_nt(dout1_3, wts["wo", l], tn=D, tk=DA // 2, out_dtype=BF16, name="attn_out_bwd", comm=comm)
            exchange("wo", j, mm_tn(st["o"], dout1_3, (1, DA, D), tn=D, tk=DA, name="attn_out_dw", comm=comm))
            dlt = attn_delta(do, st["o"], "attn_delta")
            dq = None
            for gi in range(NB):
                d = dil[gi]
                do_b, l_b, dl_b = do, st["lse"], dlt
                if d > 1:
                    do_b = permute_rows(do, d, f"permute_do_b{gi}")
                    l_b = permute_rows(st["lse"], d, f"permute_lse_b{gi}")
                    dl_b = permute_rows(dlt, d, f"permute_delta_b{gi}")
                bwd_name = f"attn_bwd_b{gi}" + ("" if dkvs[gi] is None else "_acc")
                if d > 1:
                    dq_b, dkvs[gi] = attn_branch_bwd(st["qs"][gi], kvs[gi], do_b, l_b, dl_b, dkvs[gi], gi, slopes[gi], bwd_name, comm=comm)
                    dq = unpermute_rows(dq_b, d, f"unpermute_dq_b{gi}", into=dq, total_cols=NB * DA,
                                        colmap=functools.partial(lambda jj, gi: gi * ct_blocks + jj, gi=gi))
                else:
                    dq, dkvs[gi] = attn_branch_bwd(st["qs"][gi], kvs[gi], do_b, l_b, dl_b, dkvs[gi], gi, slopes[gi], bwd_name,
                                                   out_cols=NB * DA, comm=comm)
            dq_3 = dq.reshape(1, S, NB * DA)
            dh1 = mm_nt(dq_3, wts["wq", l], tn=q_bwd_tn, tk=D, out_dtype=F32, name="q_proj_bwd", comm=comm)
            exchange("wq", j, mm_tn(st["h1"], dq_3, (N_CHIPS, D, q_ns), tn=q_bwd_tn, tk=D, name="q_proj_dw", comm=comm))
        dx, s_n1 = norm_mod_bwd(dh1, st["x0"], dx, vec(norm1_g[l]), sc1, "norm_mod_bwd")
        d_n1[l] = s_n1[2]
        dmods[l] = jnp.stack([s_n1[0], s_n1[1], s_g1[0], s_n2[0], s_n2[1], s_g2[0]])
        if l == n_pool:
            dkv = None
            for gi in range(NB):
                dkv = unpermute_rows(dkvs[gi], dil[gi], f"unpermute_dkv_b{gi}", into=dkv, total_cols=2 * NB * DA,
                                     colmap=functools.partial(lambda jj, gi: (jj // ct_blocks) * NB * ct_blocks + gi * ct_blocks + jj % ct_blocks,
                                                              gi=gi))
            dkv_3 = dkv.reshape(1, S, 2 * NB * DA)
            dhkv = mm_nt(dkv_3, wts["kv", 0], tn=kv_ns // 2, tk=D, out_dtype=F32, name="kv_proj_bwd", comm=comm)
            exchange("kv", 0, mm_tn(kv_state["hkv"], dkv_3, (N_CHIPS, D, kv_ns), tn=kv_ns // 2, tk=D, name="kv_proj_dw", comm=comm))
            dx, s_kv = norm_mod_bwd(dhkv, kv_state["x"], dx, vec(kv_norm_g), kvmod[1], "norm_mod_bwd")
    grad_x = dx.reshape(1, S, D)

    smalls = [jnp.stack(dmods), jnp.stack([s_kv[0], s_kv[1]]), jnp.stack(d_n1), jnp.stack(d_n2), s_kv[2], jnp.stack(d_convb), d_final_g,
              jnp.stack(d_scale), jnp.stack(d_convw)]
    small_shapes = [s.shape for s in smalls]
    spk, soffs = _pack(smalls)
    srows = spk.shape[0]
    sgot = all_gather8(spk, "gather_small_grads").reshape(N_DEV, srows, PACK_W)
    ssum = sum_rows8(sgot, "sum_small_grads")
    g_mods, g_kvmod, g_n1, g_n2, g_kvn, g_convb, g_fg, g_scale_full, g_convw_full = _unpack(ssum, soffs, small_shapes)
    g_ada_b = g_mods.reshape(depth, 6 * D)
    g_kv_ada_b = g_kvmod.reshape(2 * D)
    g_scale = lax.dynamic_slice(g_scale_full, (0, p_me * Dq), (n_pool, Dq))
    g_convw = lax.dynamic_slice(g_convw_full, (0, 0, p_me * Fs), (depth, 3, Fs))

    small_w = [ada_b, norm1_g, norm2_g, kv_norm_g, kv_ada_b, ffn_conv_b, final_g, pool_scale, ffn_conv_w]
    small_m = [m_ada_b, m_norm1_g, m_norm2_g, m_kv_norm_g, m_kv_ada_b, m_ffn_conv_b, m_final_g, m_pool_scale, m_ffn_conv_w]
    small_v = [v_ada_b, v_norm1_g, v_norm2_g, v_kv_norm_g, v_kv_ada_b, v_ffn_conv_b, v_final_g, v_pool_scale, v_ffn_conv_w]
    small_g = [g_ada_b, g_n1, g_n2, g_kvn, g_kv_ada_b, g_convb, g_fg, g_scale, g_convw]
    sw_shapes = [w.shape for w in small_w]
    pw, woffs = _pack(small_w)
    s_res = adamw(pw, _pack(small_g)[0], _pack(small_m)[0], _pack(small_v)[0], "adamw_small")
    s_g, s_dl, s_m, s_v = [_unpack(r, woffs, sw_shapes) for r in s_res]

    per_dev = sgot.reshape(N_DEV, -1)
    dm_all = per_dev[:, int(soffs[0]):int(soffs[1])].reshape(N_DEV, depth, 6 * D)
    dkvm_all = per_dev[:, int(soffs[1]):int(soffs[2])].reshape(N_DEV, 1, 2 * D)

    def shard_cols(a, ns):
        sl = lax.dynamic_slice_in_dim(a, p_me * ns, ns, axis=2).transpose(1, 0, 2)
        return jnp.concatenate([sl, jnp.zeros_like(sl)], axis=1)

    ada_res = ada_grad_adamw(c16, shard_cols(dm_all, ada_ns), ada_w, m_ada_w, v_ada_w, "ada_grad_adamw")
    kvada_res = ada_grad_adamw(c16, shard_cols(dkvm_all, kvada_ns), kv_ada_w.reshape(1, D, kvada_ns), m_kv_ada_w.reshape(1, D, kvada_ns),
                               v_kv_ada_w.reshape(1, D, kvada_ns), "kv_ada_grad_adamw")
    kvada_res = [r.reshape(D, kvada_ns) for r in kvada_res]

    comm.flush()
    big_names = ["pin", "pgrp", "pout", "kv", "wq", "wo", "up", "down"]
    n_stack = {"pin": n_pool, "pgrp": n_pool, "pout": n_pool, "kv": 1, "wq": n_attn, "wo": n_attn, "up": depth, "down": depth}
    gsum = {nm: None for nm in big_names}
    for nm, idx, dw4 in exchanged:
        gsum[nm] = sum_partials(dw4, comm.store[nm, idx], gsum[nm], idx, n_stack[nm], pos, "sum_partials")
    for nm in big_names:
        g = gsum[nm]

        def done(outs, nm=nm):
            gsum[nm] = outs[0]

        comm.push(_Phase(("swap", nm), nm, None, 0.0, lambda: [], lambda g=g: [g], [], g.shape[0], 0, _build_swap, done))
    comm.flush()
    big_m = [m_pool_w_in, m_pool_w_grp, m_pool_w_out, m_w_kv, m_attn_w_q, m_attn_w_o, m_ffn_w_up, m_ffn_w_down]
    big_v = [v_pool_w_in, v_pool_w_grp, v_pool_w_out, v_w_kv, v_attn_w_q, v_attn_w_o, v_ffn_w_up, v_ffn_w_down]
    big_w = [pool_w_in, pool_w_grp, pool_w_out, w_kv, attn_w_q, attn_w_o, ffn_w_up, ffn_w_down]
    big_res = []
    for nm, w, m_, v_ in zip(big_names, big_w, big_m, big_v):
        cols = gsum[nm].shape[-1]
        res = adamw(w.reshape(-1, cols), gsum[nm].reshape(-1, cols), m_.reshape(-1, cols), v_.reshape(-1, cols), "adamw_big")
        big_res.append([r.reshape(w.shape) for r in res])

    order = ["ada_w", "ada_b", "norm1_g", "norm2_g", "pool_w_in", "pool_w_grp", "pool_scale", "pool_w_out", "kv_norm_g", "kv_ada_w",
             "kv_ada_b", "w_kv", "attn_w_q", "attn_w_o", "ffn_w_up", "ffn_conv_w", "ffn_conv_b", "ffn_w_down", "final_g"]
    small_names = ["ada_b", "norm1_g", "norm2_g", "kv_norm_g", "kv_ada_b", "ffn_conv_b", "final_g", "pool_scale", "ffn_conv_w"]
    results = {"ada_w": ada_res, "kv_ada_w": kvada_res}
    for i, nm in enumerate(small_names):
        results[nm] = [s_g[i], s_dl[i], s_m[i], s_v[i]]
    for i, nm in enumerate(["pool_w_in", "pool_w_grp", "pool_w_out", "w_kv", "attn_w_q", "attn_w_o", "ffn_w_up", "ffn_w_down"]):
        results[nm] = big_res[i]
    outs = [loss, grad_x]
    for kind in range(4):
        outs += [results[nm][kind] for nm in order]
    return tuple(outs)
```

```python
import functools
import math

import numpy as np
import jax
import jax.numpy as jnp
from jax import lax
from jax.experimental import pallas as pl
from jax.experimental.pallas import tpu as pltpu

F32 = jnp.float32
BF16 = jnp.bfloat16
MESH = pl.DeviceIdType.MESH

POOL_WINDOWS = (2, 4, 8, 16)
BRANCHES = ((128, 1), (512, 4), (2048, 16))
HEAD_DIM = 64
ATTN_BLOCK = 128
EPS = 1e-6
LR, B1, B2, ADAM_EPS, WD, STEP = 0.001, 0.9, 0.999, 1e-08, 0.01, 10

VMEM_LIMIT_BYTES = 56 * 1024 * 1024
LANES = 128
PACK_W = 1024
HALO = 16
NEG = -1e30
N_CHIPS = 4
N_DEV = 8


def _alibi_slopes(n):
    def pow2(m):
        start = 2.0 ** (-(2.0 ** -(math.log2(m) - 3)))
        return [start ** (i + 1) for i in range(m)]
    if math.log2(n).is_integer():
        s = pow2(n)
    else:
        c = 2 ** math.floor(math.log2(n))
        s = pow2(c) + pow2(2 * c)[0::2][: n - c]
    s = np.asarray(s, dtype=np.float32)
    return -np.sort(-s)


def _cparams(sem=None):
    return pltpu.CompilerParams(dimension_semantics=sem, vmem_limit_bytes=VMEM_LIMIT_BYTES)


def _tile(n, pref, unit):
    t = (min(pref, n) // unit) * unit
    while t >= unit:
        if n % t == 0:
            return t
        t -= unit
    return n


def _sigmoid(v):
    return 1.0 / (1.0 + jnp.exp(-v))


def all_gather8(xs, name):
    m_per, n = xs.shape

    def body(x_ref, out_ref, send_sems, recv_sems, local_sem):
        x, y, c = lax.axis_index("x"), lax.axis_index("y"), lax.axis_index("c")
        me, sibling = (x, y, c), (x, y, 1 - c)
        chips = [(1 - x, y), (x, 1 - y), (1 - x, 1 - y)]

        def rows(px, py, pc):
            return out_ref.at[pl.ds((4 * px + 2 * py + pc) * m_per, m_per), :]

        def copy(k, block, to, src=None):
            return pltpu.make_async_remote_copy(src_ref=rows(*block) if src is None else src, dst_ref=rows(*block),
                                                send_sem=send_sems.at[k], recv_sem=recv_sems.at[k], device_id=to, device_id_type=MESH)

        mine = pltpu.make_async_copy(x_ref, rows(*me), local_sem)
        mine.start()
        first = [copy(0, me, sibling, src=x_ref)]
        first += [copy(1 + j, me, (*chip, c), src=x_ref) for j, chip in enumerate(chips)]
        for cp in first:
            cp.start()
        passed = [copy(4 + j, (*chip, c), sibling) for j, chip in enumerate(chips)]
        for j, chip in enumerate(chips):
            copy(1 + j, (*chip, c), me).wait_recv()
            passed[j].start()
        copy(0, sibling, me).wait_recv()
        for j, chip in enumerate(chips):
            copy(4 + j, (*chip, 1 - c), me).wait_recv()
        for cp in first + passed:
            cp.wait_send()
        mine.wait()

    return pl.pallas_call(
        body, name=name,
        out_shape=jax.ShapeDtypeStruct((N_DEV * m_per, n), xs.dtype),
        in_specs=[pl.BlockSpec(memory_space=pltpu.VMEM)],
        out_specs=pl.BlockSpec(memory_space=pltpu.VMEM),
        scratch_shapes=[pltpu.SemaphoreType.DMA((7,)), pltpu.SemaphoreType.DMA((7,)), pltpu.SemaphoreType.DMA],
        compiler_params=pltpu.CompilerParams(vmem_limit_bytes=VMEM_LIMIT_BYTES),
    )(xs)


HBM_SPEC = pl.BlockSpec(memory_space=pltpu.HBM)


def _mesh_pos():
    x, y, c = lax.axis_index("x"), lax.axis_index("y"), lax.axis_index("c")
    return x, y, c, [(1 - x, y), (x, 1 - y), (1 - x, 1 - y)]


class _Phase:
    def __init__(self, key, group, after, est_us, get_ins, get_inouts, new_outs, n_sems, n_local, build, on_done):
        self.key, self.group, self.after, self.est_us = key, group, after, est_us
        self.get_ins, self.get_inouts, self.new_outs = get_ins, get_inouts, new_outs
        self.n_sems, self.n_local, self.build, self.on_done = n_sems, n_local, build, on_done


def _rcopy(src, dst, send_sems, recv_sems, k, to):
    return pltpu.make_async_remote_copy(src_ref=src, dst_ref=dst, send_sem=send_sems.at[k], recv_sem=recv_sems.at[k],
                                        device_id=to, device_id_type=MESH)


def _build_fetch(in_refs, out_refs, send_sems, recv_sems, loc_sems, sem0, loc0):
    (shard,), (g,) = in_refs, out_refs
    x, y, c, chips = _mesh_pos()
    p_me = 2 * x + y
    locs = [pltpu.make_async_copy(shard.at[i], g.at[p_me, i], loc_sems.at[loc0 + i]) for i in range(2)]
    sends = [_rcopy(shard.at[c], g.at[p_me, c], send_sems, recv_sems, sem0 + j, (*chip, c)) for j, chip in enumerate(chips)]

    def recvs():
        blks = [g.at[2 * chip[0] + chip[1], c] for chip in chips]
        return [_rcopy(blk, blk, send_sems, recv_sems, sem0 + j, (*chip, c)) for j, (blk, chip) in enumerate(zip(blks, chips))]
    return sends, recvs, locs


def _build_pass(in_refs, out_refs, send_sems, recv_sems, loc_sems, sem0, loc0):
    (g,) = out_refs
    x, y, c, chips = _mesh_pos()
    sib = (x, y, 1 - c)
    slots = [2 * chip[0] + chip[1] for chip in chips]
    sends = [_rcopy(g.at[p, c], g.at[p, c], send_sems, recv_sems, sem0 + j, sib) for j, p in enumerate(slots)]

    def recvs():
        return [_rcopy(g.at[p, 1 - c], g.at[p, 1 - c], send_sems, recv_sems, sem0 + j, sib) for j, p in enumerate(slots)]
    return sends, recvs, []


def _build_exchange(in_refs, out_refs, send_sems, recv_sems, loc_sems, sem0, loc0):
    (dw,), (recv,) = in_refs, out_refs
    x, y, c, chips = _mesh_pos()
    targets = [(c, chip, c, j) for j, chip in enumerate(chips)]
    targets += [(1 - c, chip, 1 - c, 3 + j) for j, chip in enumerate([(x, y)] + chips)]
    sends = [_rcopy(dw.at[2 * chip[0] + chip[1], half], recv.at[rel], send_sems, recv_sems, sem0 + rel, (*chip, core))
             for half, chip, core, rel in targets]

    def recvs():
        return [_rcopy(recv.at[rel], recv.at[rel], send_sems, recv_sems, sem0 + rel, (x, y, 1 - c)) for rel in range(7)]
    return sends, recvs, []


def _build_swap(in_refs, out_refs, send_sems, recv_sems, loc_sems, sem0, loc0):
    (g,) = out_refs
    x, y, c, _ = _mesh_pos()
    sib = (x, y, 1 - c)
    sends = [_rcopy(g.at[l, c], g.at[l, c], send_sems, recv_sems, sem0 + l, sib) for l in range(g.shape[0])]

    def recvs():
        return [_rcopy(g.at[l, 1 - c], g.at[l, 1 - c], send_sems, recv_sems, sem0 + l, sib) for l in range(g.shape[0])]
    return sends, recvs, []


def _plan_refs(phases):
    xin, xout, alias, n_sems, n_loc, out_of = [], [], {}, 0, 0, {}
    for ph in phases:
        ph.sem0, ph.loc0 = n_sems, n_loc
        n_sems += ph.n_sems
        n_loc += ph.n_local
        if ph.after in out_of:
            ph.in0, ph.n_in, ph.out0, ph.n_out = 0, 0, out_of[ph.after], 1
            continue
        ins, inouts = ph.get_ins(), ph.get_inouts()
        ph.in0, ph.n_in = len(xin), len(ins)
        xin += ins
        ph.out0, ph.n_out = len(xout), len(inouts) + len(ph.new_outs)
        out_of[ph.key] = ph.out0
        for a in inouts:
            alias[len(xin)] = len(xout)
            xin.append(a)
            xout.append(jax.ShapeDtypeStruct(a.shape, a.dtype))
        xout += ph.new_outs
    return xin, xout, alias, max(n_sems, 1), max(n_loc, 1)


def _built(ph, xin_refs, xout_refs, sems):
    return ph.build(xin_refs[ph.in0:ph.in0 + ph.n_in], xout_refs[ph.out0:ph.out0 + ph.n_out], sems[0], sems[1], sems[2], ph.sem0, ph.loc0)


def _start(phases, xin_refs, xout_refs, sems):
    for ph in phases:
        sends, _, locs = _built(ph, xin_refs, xout_refs, sems)
        for cp in locs + sends:
            cp.start()


def _finish(phases, xin_refs, xout_refs, sems):
    for ph in phases:
        sends, recvs, locs = _built(ph, xin_refs, xout_refs, sems)
        for cp in recvs():
            cp.wait_recv()
        for cp in sends:
            cp.wait_send()
        for cp in locs:
            cp.wait()


class _Comm:
    def __init__(self):
        self.queue, self.store, self.n_alone = [], {}, 0

    def push(self, ph):
        self.queue.append(ph)

    def take(self, carry_us):
        taken, t = [], 0.0
        while True:
            pending = {ph.key for ph in self.queue} | {ph.key for ph in taken}
            room = 2.2 * carry_us if not taken else 0.8 * carry_us - t
            fits = [ph for ph in self.queue if ph.after not in pending and ph.est_us <= room]
            if not fits:
                return taken
            ph = max(fits, key=lambda p: p.est_us)
            self.queue.remove(ph)
            taken.append(ph)
            t += ph.est_us

    def require(self, group):
        phases = [ph for ph in self.queue if ph.group == group]
        if phases:
            self.queue = [ph for ph in self.queue if ph.group != group]
            self.run_alone(phases)
        return self.store[group]

    def flush(self):
        phases, self.queue = self.queue, []
        if phases:
            self.run_alone(phases)

    def run_alone(self, phases):
        phases = [ph for ph in phases if ph.after is None] + [ph for ph in phases if ph.after is not None]
        groups, keys = [[]], set()
        for ph in phases:
            if ph.after in keys:
                groups.append([])
                keys = set()
            groups[-1].append(ph)
            keys.add(ph.key)
        xin, xout, alias, n_sems, n_loc = _plan_refs(phases)
        n_xin, n_xout = len(xin), len(xout)

        def body(*refs):
            xin_refs, xout_refs, sems = refs[:n_xin], refs[n_xin:n_xin + n_xout], refs[n_xin + n_xout:]
            for grp in groups:
                _start(grp, xin_refs, xout_refs, sems)
                _finish(grp, xin_refs, xout_refs, sems)

        self.n_alone += 1
        outs = pl.pallas_call(
            body, name=f"comm_alone_{self.n_alone}", out_shape=xout, in_specs=[HBM_SPEC] * n_xin, out_specs=[HBM_SPEC] * n_xout,
            input_output_aliases=alias,
            scratch_shapes=[pltpu.SemaphoreType.DMA((n_sems,)), pltpu.SemaphoreType.DMA((n_sems,)), pltpu.SemaphoreType.DMA((n_loc,))],
        )(*xin)
        for ph in phases:
            ph.on_done(outs[ph.out0:ph.out0 + ph.n_out])


def _pcall(body, *, name, grid, in_specs, out_specs, out_shape, args, scratch_shapes=(), aliases=None, comm=None, carry_us=0.0):
    phases = comm.take(carry_us) if comm is not None else []
    n_in, n_out, n_scr = len(in_specs), len(out_specs), len(scratch_shapes)
    if not phases:
        return pl.pallas_call(body, name=name, grid=grid, in_specs=in_specs, out_specs=out_specs, out_shape=out_shape,
                              scratch_shapes=list(scratch_shapes), input_output_aliases=aliases or {},
                              compiler_params=_cparams(("arbitrary",) * len(grid)))(*args)
    xin, xout, xalias, n_sems, n_loc = _plan_refs(phases)
    n_xin, n_xout = len(xin), len(xout)
    all_alias = dict(aliases or {})
    all_alias.update({n_in + i: n_out + o for i, o in xalias.items()})

    def carrier(*refs):
        ins, xin_refs = refs[:n_in], refs[n_in:n_in + n_xin]
        outs = refs[n_in + n_xin:n_in + n_xin + n_out]
        xout_refs = refs[n_in + n_xin + n_out:n_in + n_xin + n_out + n_xout]
        rest = refs[n_in + n_xin + n_out + n_xout:]
        scr, sems = rest[:n_scr], rest[n_scr:]
        pids = [pl.program_id(k) for k in range(len(grid))]
        first = functools.reduce(jnp.logical_and, [p == 0 for p in pids])
        last = functools.reduce(jnp.logical_and, [p == n - 1 for p, n in zip(pids, grid)])

        @pl.when(first)
        def _():
            _start(phases, xin_refs, xout_refs, sems)
        body(*ins, *outs, *scr)

        @pl.when(last)
        def _():
            _finish(phases, xin_refs, xout_refs, sems)

    outs = pl.pallas_call(
        carrier, name=name, grid=grid, in_specs=list(in_specs) + [HBM_SPEC] * n_xin, out_specs=list(out_specs) + [HBM_SPEC] * n_xout,
        out_shape=list(out_shape) + xout,
        scratch_shapes=list(scratch_shapes) + [pltpu.SemaphoreType.DMA((n_sems,)), pltpu.SemaphoreType.DMA((n_sems,)),
                                               pltpu.SemaphoreType.DMA((n_loc,))],
        input_output_aliases=all_alias, compiler_params=_cparams(("arbitrary",) * len(grid)))(*args, *xin)
    for ph in phases:
        ph.on_done(outs[n_out + ph.out0:n_out + ph.out0 + ph.n_out])
    return outs[:n_out]


FETCH_US_PER_MB = 20.4
PASS_US_PER_MB = 3.3
EXCHANGE_US_PER_MB = 14.5


def push_gather(comm, keys_shards):
    prev = None
    for key, shard in keys_shards:
        r, c = shard.shape
        sh = shard.reshape(2, r // 2, c)
        half_mb = r // 2 * c * 2 / 1e6

        def done(outs, key=key):
            comm.store[key] = outs[0]

        comm.push(_Phase(("fetch", key), key, None, 3 * half_mb * FETCH_US_PER_MB, lambda sh=sh: [sh], lambda: [],
                         [jax.ShapeDtypeStruct((N_CHIPS, 2, r // 2, c), BF16)], 3, 2, _build_fetch, done))
        if prev is not None:
            comm.push(prev)
        prev = _Phase(("pass", key), key, ("fetch", key), 3 * half_mb * PASS_US_PER_MB + 3.0, lambda: [], lambda key=key: [comm.store[key]],
                      [], 3, 0, _build_pass, done)
    if prev is not None:
        comm.push(prev)


def push_exchange(comm, key, dw):
    _, r, c = dw.shape
    half_mb = r // 2 * c * 2 / 1e6

    def done(outs):
        comm.store[key] = outs[0]

    comm.push(_Phase(("exchange", key), key, None, 6 * half_mb * EXCHANGE_US_PER_MB, lambda: [dw.reshape(N_CHIPS, 2, r // 2, c)], lambda: [],
                     [jax.ShapeDtypeStruct((7, r // 2, c), BF16)], 7, 0, _build_exchange, done))


MM_FLOPS_PER_US = 6.0e8


def mm_nn(a, w3, *, tn, out_dtype, name, ncb=None, cbmap=None, res=None, perm_d=1, comm=None):
    M, K = a.shape
    P, _, Ns = w3.shape
    nper = Ns // tn
    ncb = P * nper if ncb is None else ncb
    tm = ATTN_BLOCK * perm_d if perm_d > 1 else _tile(M, 512, 8)
    cbm = cbmap if cbmap is not None else (lambda j: j)
    nch = tn // LANES

    def body(*refs):
        if res is None:
            a_ref, w_ref, o_ref = refs[:3]
        else:
            a_ref, w_ref, x_ref, g_ref, o_ref, xo_ref = refs
        acc = jnp.dot(a_ref[...].astype(BF16), w_ref[...], preferred_element_type=F32)
        if perm_d > 1:
            scr = refs[3]
            for cj in range(nch):
                scr[cj] = acc[:, cj * LANES:(cj + 1) * LANES]
            for r in range(perm_d):
                for cj in range(nch):
                    o_ref[r, :, cj * LANES:(cj + 1) * LANES] = scr.at[cj][pl.ds(r, ATTN_BLOCK, stride=perm_d), :].astype(o_ref.dtype)
        else:
            o_ref[...] = acc.astype(o_ref.dtype)
        if res is not None:
            xo_ref[...] = x_ref[...] + g_ref[...] * acc

    in_specs = [pl.BlockSpec((tm, K), lambda i, j: (i, 0)),
                pl.BlockSpec((None, K, tn), lambda i, j: (cbm(j) // nper, 0, cbm(j) % nper))]
    scratch = []
    if perm_d > 1:
        out_specs = [pl.BlockSpec((perm_d, ATTN_BLOCK, tn), lambda i, j: (0, i, j))]
        out_shape = [jax.ShapeDtypeStruct((perm_d, M // perm_d, ncb * tn), out_dtype)]
        scratch = [pltpu.VMEM((nch, tm, LANES), F32)]
    else:
        out_specs = [pl.BlockSpec((tm, tn), lambda i, j: (i, j))]
        out_shape = [jax.ShapeDtypeStruct((M, ncb * tn), out_dtype)]
    args = [a, w3]
    if res is not None:
        in_specs += [pl.BlockSpec((tm, tn), lambda i, j: (i, j)), pl.BlockSpec((1, tn), lambda i, j: (0, j))]
        out_specs.append(pl.BlockSpec((tm, tn), lambda i, j: (i, j)))
        out_shape.append(jax.ShapeDtypeStruct((M, ncb * tn), F32))
        args += [res[0], res[1]]
    outs = _pcall(body, name=name, grid=(M // tm, ncb), in_specs=in_specs, out_specs=out_specs, out_shape=out_shape, args=args,
                  scratch_shapes=scratch, comm=comm, carry_us=2.0 * M * K * ncb * tn / MM_FLOPS_PER_US)
    if perm_d > 1:
        return outs[0].reshape(M, ncb * tn)
    return outs[0] if res is None else (outs[0], outs[1])


def permute_rows(x, d, name):
    S, C = x.shape
    R = ATTN_BLOCK * d
    ct = _tile(C, 256, LANES)
    nch = ct // LANES

    def body(x_ref, o_ref, scr):
        xv = x_ref[...].astype(F32)
        for cj in range(nch):
            scr[cj] = xv[:, cj * LANES:(cj + 1) * LANES]
        for r in range(d):
            for cj in range(nch):
                o_ref[r, :, cj * LANES:(cj + 1) * LANES] = scr.at[cj][pl.ds(r, ATTN_BLOCK, stride=d), :].astype(o_ref.dtype)

    out = pl.pallas_call(body, name=name, grid=(S // R, C // ct), in_specs=[pl.BlockSpec((R, ct), lambda i, j: (i, j))],
                         out_specs=pl.BlockSpec((d, ATTN_BLOCK, ct), lambda i, j: (0, i, j)),
                         out_shape=jax.ShapeDtypeStruct((d, S // d, C), x.dtype), scratch_shapes=[pltpu.VMEM((nch, R, LANES), F32)],
                         compiler_params=_cparams(("parallel", "parallel")))(x)
    return out.reshape(S, C)


def unpermute_rows(ps, d, name, into=None, total_cols=None, colmap=None):
    ps = list(ps) if isinstance(ps, (list, tuple)) else [ps]
    n_p = len(ps)
    p = ps[0]
    S, C = p.shape
    rpb = max(ATTN_BLOCK, 512 // d)
    R = rpb * d
    ct = _tile(C, 256, LANES)
    nch = ct // LANES
    total_cols = C if total_cols is None else total_cols
    cm = colmap if colmap is not None else (lambda j: j)

    def body(*refs):
        p_refs, o_ref, scr = refs[:n_p], refs[-2], refs[-1]

        def summed(idx):
            return functools.reduce(lambda a, b: a + b, [r[idx] for r in p_refs])
        if d == 1:
            o_ref[...] = summed(0)
            return
        for r in range(d):
            for cj in range(nch):
                scr.at[cj][pl.ds(r, rpb, stride=d), :] = summed((r, slice(None), slice(cj * LANES, (cj + 1) * LANES))).astype(F32)
        for cj in range(nch):
            o_ref[:, cj * LANES:(cj + 1) * LANES] = scr[cj].astype(o_ref.dtype)

    in_specs = [pl.BlockSpec((d, rpb, ct), lambda i, j: (0, i, j))] * n_p
    args = [a.reshape(d, S // d, C) for a in ps]
    aliases = {}
    if into is not None:
        in_specs.append(pl.BlockSpec(memory_space=pl.ANY))
        args.append(into)
        aliases = {n_p: 0}
    return pl.pallas_call(body, name=name, grid=(S // R, C // ct), in_specs=in_specs,
                          out_specs=pl.BlockSpec((R, ct), lambda i, j: (i, cm(j))),
                          out_shape=jax.ShapeDtypeStruct((S, total_cols), p.dtype), scratch_shapes=[pltpu.VMEM((nch, R, LANES), F32)],
                          input_output_aliases=aliases, compiler_params=_cparams(("parallel", "parallel")))(*args)


def mm_nt(g3, w3, *, tn, tk, out_dtype, name, gmap=None, comm=None):
    _, M, _ = g3.shape
    P, K, Ns = w3.shape
    nper = Ns // tn
    ns = P * nper
    tm = _tile(M, 512, 8)
    gm = gmap if gmap is not None else (lambda s: (0, s))

    def body(g_ref, w_ref, o_ref, acc):
        s = pl.program_id(2)

        @pl.when(s == 0)
        def _():
            acc[...] = jnp.zeros_like(acc)
        acc[...] += lax.dot_general(g_ref[...].astype(BF16), w_ref[...], (((1,), (1,)), ((), ())), preferred_element_type=F32)

        @pl.when(s == ns - 1)
        def _():
            o_ref[...] = acc[...].astype(o_ref.dtype)

    return _pcall(
        body, name=name, grid=(M // tm, K // tk, ns),
        in_specs=[pl.BlockSpec((None, tm, tn), lambda i, kj, s: (gm(s)[0], i, gm(s)[1])),
                  pl.BlockSpec((None, tk, tn), lambda i, kj, s: (s // nper, kj, s % nper))],
        out_specs=[pl.BlockSpec((tm, tk), lambda i, kj, s: (i, kj))],
        out_shape=[jax.ShapeDtypeStruct((M, K), out_dtype)], args=[g3, w3],
        scratch_shapes=[pltpu.VMEM((tm, tk), F32)], comm=comm, carry_us=2.0 * M * K * P * Ns / MM_FLOPS_PER_US)[0]


def mm_tn(a, g3, wshape, *, tn, tk, name, gmap=None, comm=None):
    M, K = a.shape
    P, _, Ns = wshape
    nper = Ns // tn
    ns = P * nper
    tm = _tile(M, 512, 16)
    nm = M // tm
    gm = gmap if gmap is not None else (lambda s: (0, s))

    def body(a_ref, g_ref, o_ref, acc):
        mi = pl.program_id(2)

        @pl.when(mi == 0)
        def _():
            acc[...] = jnp.zeros_like(acc)
        acc[...] += lax.dot_general(a_ref[...].astype(BF16), g_ref[...].astype(BF16), (((0,), (0,)), ((), ())), preferred_element_type=F32)

        @pl.when(mi == nm - 1)
        def _():
            o_ref[...] = acc[...].astype(o_ref.dtype)

    return _pcall(
        body, name=name, grid=(ns, K // tk, nm),
        in_specs=[pl.BlockSpec((tm, tk), lambda s, kj, mi: (mi, kj)),
                  pl.BlockSpec((None, tm, tn), lambda s, kj, mi: (gm(s)[0], mi, gm(s)[1]))],
        out_specs=[pl.BlockSpec((None, tk, tn), lambda s, kj, mi: (s // nper, kj, s % nper))],
        out_shape=[jax.ShapeDtypeStruct((P, K, Ns), BF16)], args=[a, g3],
        scratch_shapes=[pltpu.VMEM((tk, tn), F32)], comm=comm, carry_us=2.0 * M * K * P * Ns / MM_FLOPS_PER_US)[0]


def _vspec(d):
    return pl.BlockSpec((1, d), lambda i: (0, 0))


NORM_US_PER_ELEM = 12.0 / (4096 * 1024)


def norm_mod(x, g, sh, sc, name, comm=None):
    S, D = x.shape
    tm = _tile(S, 512, 16)

    def body(x_ref, g_ref, sh_ref, sc_ref, o_ref):
        xv = x_ref[...]
        r = lax.rsqrt(jnp.mean(xv * xv, axis=-1, keepdims=True) + EPS)
        o_ref[...] = ((xv * r) * g_ref[...] * (1.0 + sc_ref[...]) + sh_ref[...]).astype(o_ref.dtype)

    return _pcall(body, name=name, grid=(S // tm,),
                  in_specs=[pl.BlockSpec((tm, D), lambda i: (i, 0)), _vspec(D), _vspec(D), _vspec(D)],
                  out_specs=[pl.BlockSpec((tm, D), lambda i: (i, 0))], out_shape=[jax.ShapeDtypeStruct((S, D), BF16)],
                  args=[x, g, sh, sc], comm=comm, carry_us=NORM_US_PER_ELEM * S * D)[0]


def norm_mod_bwd(dh, x, dres, g, sc, name):
    S, D = x.shape
    tm = _tile(S, 256, 8)

    def body(dh_ref, x_ref, dr_ref, g_ref, sc_ref, dx_ref, sums_ref):
        xv = x_ref[...]
        dhv = dh_ref[...].astype(F32)
        r = lax.rsqrt(jnp.mean(xv * xv, axis=-1, keepdims=True) + EPS)
        xn = xv * r
        one_sc = 1.0 + sc_ref[...]
        dxn = dhv * g_ref[...] * one_sc
        dx = r * (dxn - xn * jnp.mean(dxn * xn, axis=-1, keepdims=True))
        dx_ref[...] = dx + dr_ref[...]
        part = jnp.concatenate([jnp.sum(dhv, axis=0, keepdims=True), jnp.sum(dhv * xn * g_ref[...], axis=0, keepdims=True),
                                jnp.sum(dhv * one_sc * xn, axis=0, keepdims=True), jnp.zeros((5, D), F32)], axis=0)

        @pl.when(pl.program_id(0) == 0)
        def _():
            sums_ref[...] = jnp.zeros_like(sums_ref)
        sums_ref[...] += part

    row = pl.BlockSpec((tm, D), lambda i: (i, 0))
    return pl.pallas_call(body, name=name, grid=(S // tm,), in_specs=[row, row, row, _vspec(D), _vspec(D)],
                          out_specs=[row, pl.BlockSpec((8, D), lambda i: (0, 0))],
                          out_shape=[jax.ShapeDtypeStruct((S, D), F32), jax.ShapeDtypeStruct((8, D), F32)],
                          compiler_params=_cparams(("arbitrary",)))(dh, x, dres, g, sc)


def gate_bwd(dx, out, gate, name):
    S, D = dx.shape
    tm = _tile(S, 512, 16)

    def body(dx_ref, o_ref, g_ref, do_ref, sums_ref):
        dxv = dx_ref[...]
        do_ref[...] = (g_ref[...] * dxv).astype(do_ref.dtype)
        part = jnp.concatenate([jnp.sum(dxv * o_ref[...].astype(F32), axis=0, keepdims=True), jnp.zeros((7, D), F32)], axis=0)

        @pl.when(pl.program_id(0) == 0)
        def _():
            sums_ref[...] = jnp.zeros_like(sums_ref)
        sums_ref[...] += part

    row = pl.BlockSpec((tm, D), lambda i: (i, 0))
    return pl.pallas_call(body, name=name, grid=(S // tm,), in_specs=[row, row, _vspec(D)],
                          out_specs=[row, pl.BlockSpec((8, D), lambda i: (0, 0))],
                          out_shape=[jax.ShapeDtypeStruct((S, D), BF16), jax.ShapeDtypeStruct((8, D), F32)],
                          compiler_params=_cparams(("arbitrary",)))(dx, out, gate)


def loss_fwd_bwd(x, g, target, name):
    S, D = x.shape
    tm = _tile(S, 256, 8)

    def body(x_ref, g_ref, t_ref, dx_ref, sums_ref):
        xv = x_ref[...]
        r = lax.rsqrt(jnp.mean(xv * xv, axis=-1, keepdims=True) + EPS)
        xn = xv * r
        err = xn * g_ref[...] - t_ref[...]
        dy = err * (1.0 / D)
        dxn = dy * g_ref[...]
        dx_ref[...] = r * (dxn - xn * jnp.mean(dxn * xn, axis=-1, keepdims=True))
        part = jnp.concatenate([jnp.sum(dy * xn, axis=0, keepdims=True), jnp.sum(err * err, axis=0, keepdims=True),
                                jnp.zeros((6, D), F32)], axis=0)

        @pl.when(pl.program_id(0) == 0)
        def _():
            sums_ref[...] = jnp.zeros_like(sums_ref)
        sums_ref[...] += part

    row = pl.BlockSpec((tm, D), lambda i: (i, 0))
    return pl.pallas_call(body, name=name, grid=(S // tm,), in_specs=[row, _vspec(D), row],
                          out_specs=[row, pl.BlockSpec((8, D), lambda i: (0, 0))],
                          out_shape=[jax.ShapeDtypeStruct((S, D), F32), jax.ShapeDtypeStruct((8, D), F32)],
                          compiler_params=_cparams(("arbitrary",)))(x, g, target)


def pool_fwd(u, wgrp, scale, name):
    S, D = u.shape
    G = len(POOL_WINDOWS)
    C = D // G
    tm = _tile(S, 256, 16)
    hb = tm // HALO

    def body(up_ref, uc_ref, w_ref, sc_ref, p_ref, z_ref, y_ref):
        i = pl.program_id(0)
        prev = jnp.where(i > 0, up_ref[...], 0.0)
        ext = jnp.concatenate([prev, uc_ref[...]], axis=0)
        t = i * tm + lax.broadcasted_iota(jnp.int32, (tm, 1), 0)
        for gi, w in enumerate(POOL_WINDOWS):
            cs = slice(gi * C, (gi + 1) * C)
            e = ext[:, cs]
            s, k = e, 1
            while k < w:
                s = s + pltpu.roll(s, k, 0)
                k *= 2
            cnt = jnp.minimum(t + 1, w).astype(F32)
            pooled = (s[HALO:] / cnt - e[HALO:]).astype(BF16)
            p_ref[:, cs] = pooled
            z = jnp.dot(pooled, w_ref[:, gi].reshape(C, C), preferred_element_type=F32)
            z_ref[:, cs] = z.astype(BF16)
            y_ref[:, cs] = (z * sc_ref[:, cs]).astype(BF16)

    row = pl.BlockSpec((tm, D), lambda i: (i, 0))
    return pl.pallas_call(
        body, name=name, grid=(S // tm,),
        in_specs=[pl.BlockSpec((HALO, D), lambda i: (jnp.maximum(i * hb - 1, 0), 0)), row,
                  pl.BlockSpec(wgrp.shape, lambda i: (0, 0, 0, 0)), _vspec(D)],
        out_specs=[row, row, row], out_shape=[jax.ShapeDtypeStruct((S, D), BF16)] * 3,
        compiler_params=_cparams(("parallel",)))(u, u, wgrp, scale)


def pool_bwd(dys, z, pooled, wgrp, scale, name):
    S, D = dys.shape
    G = len(POOL_WINDOWS)
    C = D // G
    tm = _tile(S, 256, 16)
    hb = tm // HALO
    nt = S // tm
    n_ext = tm + HALO

    def body(dc_ref, dn_ref, z_ref, p_ref, w_ref, sc_ref, du_ref, dw_ref, sums_ref):
        i = pl.program_id(0)

        @pl.when(i == 0)
        def _():
            dw_ref[...] = jnp.zeros_like(dw_ref)
            sums_ref[...] = jnp.zeros_like(sums_ref)
        dyc = dc_ref[...].astype(F32)
        nxt = jnp.where(i < nt - 1, dn_ref[...].astype(F32), 0.0)
        ext = jnp.concatenate([dyc, nxt], axis=0)
        sums_ref[...] += jnp.concatenate([jnp.sum(dyc * z_ref[...].astype(F32), axis=0, keepdims=True), jnp.zeros((7, D), F32)], axis=0)
        t = i * tm + lax.broadcasted_iota(jnp.int32, (n_ext, 1), 0)
        for gi, w in enumerate(POOL_WINDOWS):
            cs = slice(gi * C, (gi + 1) * C)
            wg = w_ref[:, gi].reshape(C, C)
            dz = (ext[:, cs] * sc_ref[:, cs]).astype(BF16)
            dpool = lax.dot_general(dz, wg, (((1,), (1,)), ((), ())), preferred_element_type=F32)
            dw_ref[gi] += lax.dot_general(p_ref[:, cs], dz[:tm], (((0,), (0,)), ((), ())), preferred_element_type=F32)
            cnt = jnp.minimum(t + 1, w).astype(F32)
            s, k = dpool / cnt, 1
            while k < w:
                s = s + pltpu.roll(s, n_ext - k, 0)
                k *= 2
            du_ref[:, cs] = (s[:tm] - dpool[:tm]).astype(BF16)

    row = pl.BlockSpec((tm, D), lambda i: (i, 0))
    return pl.pallas_call(
        body, name=name, grid=(nt,),
        in_specs=[row, pl.BlockSpec((HALO, D), lambda i: (jnp.minimum((i + 1) * hb, S // HALO - 1), 0)), row, row,
                  pl.BlockSpec(wgrp.shape, lambda i: (0, 0, 0, 0)), _vspec(D)],
        out_specs=[row, pl.BlockSpec((G, C, C), lambda i: (0, 0, 0)), pl.BlockSpec((8, D), lambda i: (0, 0))],
        out_shape=[jax.ShapeDtypeStruct((S, D), BF16), jax.ShapeDtypeStruct((G, C, C), F32), jax.ShapeDtypeStruct((8, D), F32)],
        compiler_params=_cparams(("arbitrary",)))(dys, dys, z, pooled, wgrp, scale)


FFN_ACT_BWD_US_PER_ELEM = 84.0 / (4096 * 2816)


def ffn_up_act(h, w3, conv_w, conv_b, name, comm=None):
    S, D = h.shape
    P, _, Ns = w3.shape
    nh = P // 2
    tm = _tile(S, 512, 16)

    def body(h_ref, w_ref, cw_ref, cb_ref, hu_ref, g_ref, stash, halo):
        i, j = pl.program_id(0), pl.program_id(1)
        acc = jnp.dot(h_ref[...], w_ref[...], preferred_element_type=F32).astype(BF16)
        hu_ref[...] = acc

        @pl.when(j < nh)
        def _():
            stash[j] = acc.astype(F32)

        @pl.when(j >= nh)
        def _():
            c = j - nh
            a = stash[c]
            ext = jnp.concatenate([jnp.where(i > 0, halo[c], 0.0), a], axis=0)
            conv = cb_ref[...] + pltpu.roll(ext, 2, 0) * cw_ref[0:1, :] + pltpu.roll(ext, 1, 0) * cw_ref[1:2, :] + ext * cw_ref[2:3, :]
            conv = conv[HALO:]
            g_ref[...] = (conv * _sigmoid(conv) * acc.astype(F32)).astype(g_ref.dtype)
            halo[c] = a[tm - HALO:]

    def gcol(j):
        return jnp.maximum(j - nh, 0)

    return _pcall(
        body, name=name, grid=(S // tm, P),
        in_specs=[pl.BlockSpec((tm, D), lambda i, j: (i, 0)), pl.BlockSpec((None, D, Ns), lambda i, j: (j, 0, 0)),
                  pl.BlockSpec((3, Ns), lambda i, j: (0, gcol(j))), pl.BlockSpec((1, Ns), lambda i, j: (0, gcol(j)))],
        out_specs=[pl.BlockSpec((tm, Ns), lambda i, j: (i, j)), pl.BlockSpec((tm, Ns), lambda i, j: (i, gcol(j)))],
        out_shape=[jax.ShapeDtypeStruct((S, P * Ns), BF16), jax.ShapeDtypeStruct((S, nh * Ns), BF16)],
        scratch_shapes=[pltpu.VMEM((nh, tm, Ns), F32), pltpu.VMEM((nh, HALO, Ns), F32)],
        args=[h, w3, conv_w, conv_b], comm=comm, carry_us=2.0 * S * D * P * Ns / MM_FLOPS_PER_US)


def ffn_act_bwd(dg, hu, conv_w, conv_b, name, comm=None):
    S, F = dg.shape
    tm = _tile(S, 256, 16)
    tn = _tile(F, 1408, LANES)
    nb = F // tn
    hb = tm // HALO
    nt = S // tm
    n_ext = tm + 2 * HALO

    def body(gc_ref, gn_ref, ap_ref, ac_ref, an_ref, vc_ref, vn_ref, w_ref, b_ref, o_ref, sums_ref):
        i = pl.program_id(1)

        @pl.when(i == 0)
        def _():
            sums_ref[...] = jnp.zeros_like(sums_ref)
        zeros = jnp.zeros((HALO, tn), F32)
        not_last = i < nt - 1
        a_ext = jnp.concatenate([jnp.where(i > 0, ap_ref[...].astype(F32), 0.0), ac_ref[...].astype(F32), an_ref[...].astype(F32)], axis=0)
        v_ext = jnp.concatenate([zeros, vc_ref[...].astype(F32), vn_ref[...].astype(F32)], axis=0)
        g_ext = jnp.concatenate([zeros, gc_ref[...].astype(F32), jnp.where(not_last, gn_ref[...].astype(F32), 0.0)], axis=0)
        w0, w1, w2 = w_ref[0:1, :], w_ref[1:2, :], w_ref[2:3, :]
        a_m2, a_m1 = pltpu.roll(a_ext, 2, 0), pltpu.roll(a_ext, 1, 0)
        conv = b_ref[...] + a_m2 * w0 + a_m1 * w1 + a_ext * w2
        sig = _sigmoid(conv)
        silu = conv * sig
        dsilu = sig * (1.0 + conv * (1.0 - sig))
        dconv = g_ext * v_ext * dsilu
        da = dconv * w2 + pltpu.roll(dconv, n_ext - 1, 0) * w1 + pltpu.roll(dconv, n_ext - 2, 0) * w0
        cur = slice(HALO, HALO + tm)
        o_ref[0] = da[cur].astype(o_ref.dtype)
        o_ref[1] = (g_ext * silu)[cur].astype(o_ref.dtype)
        dc = dconv[cur]
        part = jnp.concatenate([jnp.sum(dc * a_m2[cur], axis=0, keepdims=True), jnp.sum(dc * a_m1[cur], axis=0, keepdims=True),
                                jnp.sum(dc * a_ext[cur], axis=0, keepdims=True), jnp.sum(dc, axis=0, keepdims=True),
                                jnp.zeros((4, tn), F32)], axis=0)
        sums_ref[...] += part

    def prev(i):
        return jnp.maximum(i * hb - 1, 0)

    def nxt(i):
        return jnp.minimum((i + 1) * hb, S // HALO - 1)

    return _pcall(
        body, name=name, grid=(nb, nt),
        in_specs=[pl.BlockSpec((tm, tn), lambda j, i: (i, j)), pl.BlockSpec((HALO, tn), lambda j, i: (nxt(i), j)),
                  pl.BlockSpec((HALO, tn), lambda j, i: (prev(i), j)), pl.BlockSpec((tm, tn), lambda j, i: (i, j)),
                  pl.BlockSpec((HALO, tn), lambda j, i: (nxt(i), j)),
                  pl.BlockSpec((tm, tn), lambda j, i: (i, j + nb)), pl.BlockSpec((HALO, tn), lambda j, i: (nxt(i), j + nb)),
                  pl.BlockSpec((3, tn), lambda j, i: (0, j)), pl.BlockSpec((1, tn), lambda j, i: (0, j))],
        out_specs=[pl.BlockSpec((2, tm, tn), lambda j, i: (0, i, j)), pl.BlockSpec((8, tn), lambda j, i: (0, j))],
        out_shape=[jax.ShapeDtypeStruct((2, S, F), BF16), jax.ShapeDtypeStruct((8, F), F32)],
        args=[dg, dg, hu, hu, hu, hu, hu, conv_w, conv_b], comm=comm, carry_us=FFN_ACT_BWD_US_PER_ELEM * S * F)


def _head_expander(n_heads, da):
    e = np.zeros((LANES, da), np.float32)
    for h in range(n_heads):
        e[h, h * HEAD_DIM:(h + 1) * HEAD_DIM] = 1.0
    return jnp.asarray(e, BF16)


def _split_dot(v, e, dims):
    hi = v.astype(BF16)
    lo = (v - hi.astype(F32)).astype(BF16)
    return (lax.dot_general(hi, e, dims, preferred_element_type=F32) + lax.dot_general(lo, e, dims, preferred_element_type=F32))


def _lane_col(tile, h):
    lane = lax.broadcasted_iota(jnp.int32, tile.shape, 1)
    return jnp.sum(jnp.where(lane == h, tile, 0.0), axis=1, keepdims=True)


ATTN_US_PER_ELEM = (80.0 / (4096 * 1024), 230.0 / (4096 * 1024))


def attn_branch_fwd(q, kv, gi, slopes, name, comm=None):
    S, DA = q.shape
    H = DA // HEAD_DIM
    window, d = BRANCHES[gi]
    n_steps = window // d
    blk = ATTN_BLOCK
    assert n_steps == blk and (S // d) % blk == 0
    nbs = S // d // blk
    scale = HEAD_DIM ** -0.5

    def body(q_ref, kp_ref, kc_ref, vp_ref, vc_ref, o_ref, l_ref, s_scr, p_scr):
        jb = pl.program_id(1)
        row = lax.broadcasted_iota(jnp.int32, (blk, 2 * blk), 0)
        col = lax.broadcasted_iota(jnp.int32, (blk, 2 * blk), 1)
        delta = row + blk - col
        valid = (delta >= 0) & (delta <= n_steps) & ((col >= blk) | (jb > 0))
        dist = jnp.where(valid, (delta * d).astype(F32), -NEG)
        lane = lax.broadcasted_iota(jnp.int32, (blk, LANES), 1)
        ltile = jnp.zeros((blk, LANES), F32)
        for h in range(H):
            hs = slice(h * HEAD_DIM, (h + 1) * HEAD_DIM)
            k2 = jnp.concatenate([kp_ref[:, hs], kc_ref[:, hs]], axis=0)
            s_scr[h] = lax.dot_general(q_ref[:, hs], k2, (((1,), (1,)), ((), ())), preferred_element_type=F32)
        for h in range(H):
            s = s_scr[h] * scale - float(slopes[h]) * dist
            m = jnp.max(s, axis=-1, keepdims=True)
            p = jnp.exp(s - m)
            l = jnp.sum(p, axis=-1, keepdims=True)
            p_scr[h] = (p / l).astype(BF16)
            ltile = jnp.where(lane == h, m + jnp.log(l), ltile)
        for h in range(H):
            hs = slice(h * HEAD_DIM, (h + 1) * HEAD_DIM)
            v2 = jnp.concatenate([vp_ref[:, hs], vc_ref[:, hs]], axis=0)
            o_ref[:, hs] = jnp.dot(p_scr[h], v2, preferred_element_type=F32)
        l_ref[...] = ltile

    def cur(width, off):
        return pl.BlockSpec((blk, width), lambda r, jb: (r * nbs + jb, off))

    def prv(width, off):
        return pl.BlockSpec((blk, width), lambda r, jb: (r * nbs + jnp.maximum(jb - 1, 0), off))

    return _pcall(
        body, name=name, grid=(d, nbs),
        in_specs=[cur(DA, 0), prv(DA, 0), cur(DA, 0), prv(DA, 1), cur(DA, 1)],
        out_specs=[cur(DA, 0), cur(LANES, 0)],
        out_shape=[jax.ShapeDtypeStruct((S, DA), F32), jax.ShapeDtypeStruct((S, LANES), F32)],
        scratch_shapes=[pltpu.VMEM((H, blk, 2 * blk), F32), pltpu.VMEM((H, blk, 2 * blk), BF16)],
        args=[q, kv, kv, kv, kv], comm=comm, carry_us=ATTN_US_PER_ELEM[0] * S * DA)


def attn_combine(os_, lses, name):
    S, DA = os_[0].shape
    H = DA // HEAD_DIM
    tm = _tile(S, 256, 16)
    expander = _head_expander(H, DA)
    nbr = len(os_)

    def body(*refs):
        o_refs, l_refs, e_ref = refs[:nbr], refs[nbr:2 * nbr], refs[2 * nbr]
        out_ref, lse_ref = refs[2 * nbr + 1:]
        ls = [r[...] for r in l_refs]
        lmax = functools.reduce(jnp.maximum, ls)
        es = [jnp.exp(l - lmax) for l in ls]
        den = functools.reduce(lambda a, b: a + b, es)
        lse_ref[...] = lmax + jnp.log(den)
        acc = jnp.zeros((tm, DA), F32)
        for e, o_ref in zip(es, o_refs):
            acc = acc + _split_dot(e / den, e_ref[...], (((1,), (0,)), ((), ()))) * o_ref[...]
        out_ref[...] = acc.astype(out_ref.dtype)

    row = pl.BlockSpec((tm, DA), lambda i: (i, 0))
    lrow = pl.BlockSpec((tm, LANES), lambda i: (i, 0))
    return pl.pallas_call(
        body, name=name, grid=(S // tm,),
        in_specs=[row] * nbr + [lrow] * nbr + [pl.BlockSpec((LANES, DA), lambda i: (0, 0))],
        out_specs=[row, lrow], out_shape=[jax.ShapeDtypeStruct((S, DA), BF16), jax.ShapeDtypeStruct((S, LANES), F32)],
        compiler_params=_cparams(("parallel",)))(*os_, *lses, expander)


def attn_delta(do, o, name):
    S, DA = o.shape
    H = DA // HEAD_DIM
    tm = _tile(S, 512, 16)
    expander = _head_expander(H, DA)

    def body(do_ref, o_ref, e_ref, d_ref):
        prod = do_ref[...].astype(F32) * o_ref[...].astype(F32)
        d_ref[...] = _split_dot(prod, e_ref[...], (((1,), (1,)), ((), ())))

    row = pl.BlockSpec((tm, DA), lambda i: (i, 0))
    return pl.pallas_call(body, name=name, grid=(S // tm,), in_specs=[row, row, pl.BlockSpec((LANES, DA), lambda i: (0, 0))],
                          out_specs=pl.BlockSpec((tm, LANES), lambda i: (i, 0)), out_shape=jax.ShapeDtypeStruct((S, LANES), F32),
                          compiler_params=_cparams(("parallel",)))(do, o, expander)


def attn_branch_bwd(q, kv, do, lse, dlt, gi, slopes, name, out_cols=None, comm=None):
    S, DA = q.shape
    H = DA // HEAD_DIM
    window, d = BRANCHES[gi]
    n_steps = window // d
    blk = ATTN_BLOCK
    nbs = S // d // blk
    scale = HEAD_DIM ** -0.5
    nt, tn = (((1,), (1,)), ((), ())), (((0,), (0,)), ((), ()))

    def body(*refs):
        k_ref, v_ref, qc_ref, qn_ref, doc_ref, don_ref, lc_ref, ln_ref, dc_ref, dn_ref = refs[:10]
        dq_ref, dkv_ref, carry, s_scr, dp_scr, p_scr, ds_scr = refs[-7:]
        kb = pl.program_id(1)

        @pl.when(kb == 0)
        def _():
            carry[...] = jnp.zeros_like(carry)
        row = lax.broadcasted_iota(jnp.int32, (2 * blk, blk), 0)
        col = lax.broadcasted_iota(jnp.int32, (2 * blk, blk), 1)
        delta = row - col
        valid = (delta >= 0) & (delta <= n_steps) & ((row < blk) | (kb < nbs - 1))
        dist = jnp.where(valid, (delta * d).astype(F32), -NEG)
        l2 = jnp.concatenate([lc_ref[...], ln_ref[...]], axis=0)
        d2 = jnp.concatenate([dc_ref[...], dn_ref[...]], axis=0)
        for h in range(H):
            hs = slice(h * HEAD_DIM, (h + 1) * HEAD_DIM)
            q2 = jnp.concatenate([qc_ref[:, hs], qn_ref[:, hs]], axis=0)
            do2 = jnp.concatenate([doc_ref[:, hs], don_ref[:, hs]], axis=0)
            s_scr[h] = lax.dot_general(q2, k_ref[:, hs], nt, preferred_element_type=F32)
            dp_scr[h] = lax.dot_general(do2, v_ref[:, hs], nt, preferred_element_type=F32)
        for h in range(H):
            p = jnp.exp(s_scr[h] * scale - float(slopes[h]) * dist - _lane_col(l2, h))
            p_scr[h] = p.astype(BF16)
            ds_scr[h] = (p * (dp_scr[h] - _lane_col(d2, h))).astype(BF16)
        for h in range(H):
            hs = slice(h * HEAD_DIM, (h + 1) * HEAD_DIM)
            vs = slice(DA + h * HEAD_DIM, DA + (h + 1) * HEAD_DIM)
            q2 = jnp.concatenate([qc_ref[:, hs], qn_ref[:, hs]], axis=0)
            do2 = jnp.concatenate([doc_ref[:, hs], don_ref[:, hs]], axis=0)
            dvh = lax.dot_general(p_scr[h], do2, tn, preferred_element_type=F32)
            dkh = lax.dot_general(ds_scr[h], q2, tn, preferred_element_type=F32) * scale
            dq2 = jnp.dot(ds_scr[h], k_ref[:, hs], preferred_element_type=F32) * scale
            dq_ref[:, hs] = (carry[:, hs] + dq2[:blk]).astype(dq_ref.dtype)
            carry[:, hs] = dq2[blk:]
            dkv_ref[:, hs] = dkh
            dkv_ref[:, vs] = dvh

    def cur(width, off):
        return pl.BlockSpec((blk, width), lambda r, kb: (r * nbs + kb, off))

    def nxt(width, off):
        return pl.BlockSpec((blk, width), lambda r, kb: (r * nbs + jnp.minimum(kb + 1, nbs - 1), off))

    in_specs = [cur(DA, 0), cur(DA, 1), cur(DA, 0), nxt(DA, 0), cur(DA, 0), nxt(DA, 0),
                cur(LANES, 0), nxt(LANES, 0), cur(LANES, 0), nxt(LANES, 0)]
    args = [kv, kv, q, q, do, do, lse, lse, dlt, dlt]
    return _pcall(
        body, name=name, grid=(d, nbs), in_specs=in_specs, out_specs=[cur(DA, 0), cur(2 * DA, 0)],
        out_shape=[jax.ShapeDtypeStruct((S, out_cols or DA), BF16), jax.ShapeDtypeStruct((S, 2 * DA), F32)],
        scratch_shapes=[pltpu.VMEM((blk, DA), F32), pltpu.VMEM((H, 2 * blk, blk), F32), pltpu.VMEM((H, 2 * blk, blk), F32),
                        pltpu.VMEM((H, 2 * blk, blk), BF16), pltpu.VMEM((H, 2 * blk, blk), BF16)],
        args=args, comm=comm, carry_us=ATTN_US_PER_ELEM[1] * S * DA)


def ada_project(c16, w3, b3, name):
    L, D, Ns = w3.shape
    tn = _tile(Ns, 512, LANES)

    def body(c_ref, w_ref, b_ref, o_ref):
        cv = c_ref[...]
        cond = (cv * _sigmoid(cv)).astype(BF16)
        o_ref[...] = jnp.dot(cond, w_ref[...].astype(BF16), preferred_element_type=F32) + b_ref[...]

    return pl.pallas_call(
        body, name=name, grid=(L, Ns // tn),
        in_specs=[pl.BlockSpec((16, D), lambda l, j: (0, 0)), pl.BlockSpec((None, D, tn), lambda l, j: (l, 0, j)),
                  pl.BlockSpec((None, 1, tn), lambda l, j: (l, 0, j))],
        out_specs=pl.BlockSpec((None, 16, tn), lambda l, j: (l, 0, j)), out_shape=jax.ShapeDtypeStruct((L, 16, Ns), F32),
        compiler_params=_cparams(("parallel", "parallel")))(c16, w3, b3)


def _adamw(w, g, m, v):
    m = B1 * m + (1.0 - B1) * g
    v = B2 * v + (1.0 - B2) * (g * g)
    m_hat = m / (1.0 - B1 ** STEP)
    v_hat = v / (1.0 - B2 ** STEP)
    delta = -LR * (m_hat / (jnp.sqrt(v_hat) + ADAM_EPS) + WD * w)
    return delta, m, v


ADAMW_US_PER_ELEM = 60.0 / (4 * 1024 * 1536)


def ada_grad_adamw(c16, d3, w3, m3, v3, name, comm=None):
    L, D, Ns = w3.shape
    tk = _tile(D, 256, 8)

    def body(c_ref, d_ref, w_ref, m_ref, v_ref, g_out, dl_out, m_out, v_out):
        cv = c_ref[...]
        cond = (cv * _sigmoid(cv)).astype(BF16)
        g = lax.dot_general(cond, d_ref[...].astype(BF16), (((0,), (0,)), ((), ())), preferred_element_type=F32)
        g_out[...] = g
        dl_out[...], m_out[...], v_out[...] = _adamw(w_ref[...], g, m_ref[...], v_ref[...])

    wspec = pl.BlockSpec((None, tk, Ns), lambda l, kj: (l, kj, 0))
    return _pcall(
        body, name=name, grid=(L, D // tk),
        in_specs=[pl.BlockSpec((16, tk), lambda l, kj: (0, kj)), pl.BlockSpec((None, 16, Ns), lambda l, kj: (l, 0, 0)), wspec, wspec, wspec],
        out_specs=[wspec] * 4, out_shape=[jax.ShapeDtypeStruct((L, D, Ns), F32)] * 4,
        args=[c16, d3, w3, m3, v3], comm=comm, carry_us=ADAMW_US_PER_ELEM * L * D * Ns)


def adamw(w, g, m, v, name):
    R, C = w.shape
    tr = _tile(R, 256, 8)

    def body(w_ref, g_ref, m_ref, v_ref, g_out, dl_out, m_out, v_out):
        g = g_ref[...]
        g_out[...] = g
        dl_out[...], m_out[...], v_out[...] = _adamw(w_ref[...], g, m_ref[...], v_ref[...])

    spec = pl.BlockSpec((tr, C), lambda i: (i, 0))
    return pl.pallas_call(body, name=name, grid=(R // tr,), in_specs=[spec] * 4, out_specs=[spec] * 4,
                          out_shape=[jax.ShapeDtypeStruct((R, C), F32)] * 4, compiler_params=_cparams(("parallel",)))(w, g, m, v)


def sum_partials(own, recv, g_prev, layer, n_layers, pos, name):
    _, Rh, C = recv.shape
    tr = _tile(Rh, 256, 16)

    def body(pos_ref, own_ref, recv_ref, *rest):
        acc = own_ref[...].astype(F32)
        for rel in range(7):
            acc = acc + recv_ref[rel].astype(F32)
        rest[-1][...] = acc

    in_specs = [pl.BlockSpec((None, None, tr, C), lambda r, pos: (pos[1], pos[0], r, 0)), pl.BlockSpec((7, tr, C), lambda r, pos: (0, r, 0))]
    args = [pos, own.reshape(N_CHIPS, 2, Rh, C), recv]
    aliases = {}
    if g_prev is not None:
        in_specs.append(pl.BlockSpec(memory_space=pl.ANY))
        args.append(g_prev)
        aliases = {3: 0}
    return pl.pallas_call(
        body, name=name,
        grid_spec=pltpu.PrefetchScalarGridSpec(
            num_scalar_prefetch=1, grid=(Rh // tr,), in_specs=in_specs,
            out_specs=pl.BlockSpec((None, None, tr, C), lambda r, pos: (layer, pos[0], r, 0))),
        out_shape=jax.ShapeDtypeStruct((n_layers, 2, Rh, C), F32), input_output_aliases=aliases,
        compiler_params=_cparams(("parallel",)))(*args)


def sum_rows8(g8, name):
    _, R, C = g8.shape

    def body(g_ref, o_ref):
        acc = g_ref[0]
        for i in range(1, N_DEV):
            acc = acc + g_ref[i]
        o_ref[...] = acc

    return pl.pallas_call(body, name=name, grid=(1,), in_specs=[pl.BlockSpec((N_DEV, R, C), lambda i: (0, 0, 0))],
                          out_specs=pl.BlockSpec((R, C), lambda i: (0, 0)), out_shape=jax.ShapeDtypeStruct((R, C), F32),
                          compiler_params=_cparams(("arbitrary",)))(g8)


def _pack(vecs):
    flat = [v.reshape(-1).astype(F32) for v in vecs]
    sizes = [f.shape[0] for f in flat]
    total = sum(sizes)
    padded = -(-total // (8 * PACK_W)) * (8 * PACK_W)
    buf = jnp.concatenate(flat + [jnp.zeros((padded - total,), F32)])
    offs = np.concatenate([[0], np.cumsum(sizes)])
    return buf.reshape(-1, PACK_W), offs


def _unpack(buf, offs, shapes):
    flat = buf.reshape(-1)
    return [flat[int(offs[i]):int(offs[i + 1])].reshape(s) for i, s in enumerate(shapes)]


def kernel(x, c, ada_w, ada_b, norm1_g, norm2_g, pool_w_in, pool_w_grp, pool_scale, pool_w_out, kv_norm_g, kv_ada_w, kv_ada_b, w_kv, attn_w_q, attn_w_o, ffn_w_up, ffn_conv_w, ffn_conv_b, ffn_w_down, final_g, loss_target, m_ada_w, m_ada_b, m_norm1_g, m_norm2_g, m_pool_w_in, m_pool_w_grp, m_pool_scale, m_pool_w_out, m_kv_norm_g, m_kv_ada_w, m_kv_ada_b, m_w_kv, m_attn_w_q, m_attn_w_o, m_ffn_w_up, m_ffn_conv_w, m_ffn_conv_b, m_ffn_w_down, m_final_g, v_ada_w, v_ada_b, v_norm1_g, v_norm2_g, v_pool_w_in, v_pool_w_grp, v_pool_scale, v_pool_w_out, v_kv_norm_g, v_kv_ada_w, v_kv_ada_b, v_w_kv, v_attn_w_q, v_attn_w_o, v_ffn_w_up, v_ffn_conv_w, v_ffn_conv_b, v_ffn_w_down, v_final_g):
    S, D = x.shape[1], x.shape[2]
    depth = ada_w.shape[0]
    n_pool = pool_w_in.shape[0]
    n_attn = attn_w_q.shape[0]
    G = len(POOL_WINDOWS)
    NB = len(BRANCHES)
    DA = attn_w_o.shape[1] * N_CHIPS
    H = DA // HEAD_DIM
    F = ffn_conv_b.shape[1]
    Fs = F // N_CHIPS
    Dq = D // N_CHIPS
    ada_ns = ada_w.shape[2]
    kvada_ns = kv_ada_w.shape[1]
    slopes = _alibi_slopes(NB * H).reshape(NB, H)

    ix, iy, ic = lax.axis_index("x"), lax.axis_index("y"), lax.axis_index("c")
    p_me = 2 * ix + iy
    b_me = 4 * ix + 2 * iy + ic
    pos = jnp.stack([ic, p_me]).astype(jnp.int32)
    xs, tgt = x[0], loss_target[0]

    pk, offs = _pack([c, pool_scale, ffn_conv_w])
    rows1 = pk.shape[0]
    got = all_gather8(pk, "gather_small_in").reshape(N_DEV, rows1, PACK_W)
    c8 = got.reshape(N_DEV, -1)[:, :D]
    c16 = jnp.concatenate([c8, jnp.zeros_like(c8)], axis=0)
    chip_rows = got[0::2].reshape(N_CHIPS, -1)
    scale_full = chip_rows[:, int(offs[1]):int(offs[2])].reshape(N_CHIPS, n_pool, Dq).transpose(1, 0, 2).reshape(n_pool, D)
    convw_full = chip_rows[:, int(offs[2]):int(offs[3])].reshape(N_CHIPS, depth, 3, Fs).transpose(1, 2, 0, 3).reshape(depth, 3, F)

    ada_b_loc = lax.dynamic_slice(ada_b, (0, p_me * ada_ns), (depth, ada_ns)).reshape(depth, 1, ada_ns)
    kvb_loc = lax.dynamic_slice(kv_ada_b, (p_me * kvada_ns,), (kvada_ns,)).reshape(1, 1, kvada_ns)
    mods_loc = ada_project(c16, ada_w, ada_b_loc, "ada_project")[:, :N_DEV]
    kvmod_loc = ada_project(c16, kv_ada_w.reshape(1, D, kvada_ns), kvb_loc, "kv_ada_project")[0, :N_DEV]
    mods_cat = jnp.concatenate([mods_loc.transpose(1, 0, 2).reshape(N_DEV, depth * ada_ns), kvmod_loc], axis=1)
    mods_all = all_gather8(mods_cat, "gather_mods").reshape(N_CHIPS, 2, N_DEV, -1)
    mine = lax.dynamic_index_in_dim(mods_all[:, 0], b_me, axis=1, keepdims=False)
    mod = mine[:, :depth * ada_ns].reshape(N_CHIPS, depth, ada_ns).transpose(1, 0, 2).reshape(depth, 6, 1, D)
    kvmod = mine[:, depth * ada_ns:].reshape(2, 1, D)

    comm = _Comm()
    C = D // G
    kv_ns, q_ns, up_ns = w_kv.shape[1], attn_w_q.shape[2], ffn_w_up.shape[2]

    def layer_shards(l):
        sh = []
        if l < n_pool:
            sh += [(("pin", l), pool_w_in[l]), (("pgrp", l), pool_w_grp[l].reshape(-1, C)), (("pout", l), pool_w_out[l])]
        else:
            if l == n_pool:
                sh.append((("kv", 0), w_kv))
            sh += [(("wq", l), attn_w_q[l - n_pool]), (("wo", l), attn_w_o[l - n_pool])]
        sh += [(("up", l), ffn_w_up[l]), (("down", l), ffn_w_down[l])]
        return [(k, w.astype(BF16)) for k, w in sh]

    def weight(key, shape):
        return comm.require(key).reshape(shape)

    dil = [d for _, d in BRANCHES]
    kv_tn = DA // 2
    q_tn = DA // 4
    q_bwd_tn = q_ns
    up_tn = up_ns
    up_per_half = F // up_tn

    def up_gmap(s):
        return s // up_per_half, s % up_per_half

    def vec(v):
        return v.reshape(1, -1)

    saved = []
    xcur = xs
    kvs = None
    wts = {}
    push_gather(comm, layer_shards(0))
    comm.flush()
    for l in range(depth):
        if l + 1 < depth:
            push_gather(comm, layer_shards(l + 1))
        sh1, sc1, g1, sh2, sc2, g2 = [mod[l, i] for i in range(6)]
        st = {"x0": xcur}
        h1 = norm_mod(xcur, vec(norm1_g[l]), sh1, sc1, "norm_mod", comm=comm)
        st["h1"] = h1
        if l < n_pool:
            wts["pin", l] = weight(("pin", l), (1, D, D))
            u = mm_nn(h1, wts["pin", l], tn=D // 2, out_dtype=F32, name="pool_in_proj", comm=comm)
            wts["pgrp", l] = weight(("pgrp", l), (N_CHIPS, G, C // N_CHIPS, C))
            pooled, z, ys = pool_fwd(u, wts["pgrp", l], vec(scale_full[l]), "pool_mix")
            wts["pout", l] = weight(("pout", l), (1, D, D))
            out, x1 = mm_nn(ys, wts["pout", l], tn=D // 2, out_dtype=BF16, name="pool_out_proj", res=(xcur, g1), comm=comm)
            st.update(pooled=pooled, z=z, ys=ys, out1=out)
        else:
            if l == n_pool:
                wts["kv", 0] = weight(("kv", 0), (N_CHIPS, D, kv_ns))
                hkv = norm_mod(xcur, vec(kv_norm_g), kvmod[0], kvmod[1], "norm_mod", comm=comm)
                kvs = [mm_nn(hkv, wts["kv", 0], tn=kv_tn, out_dtype=BF16, name=f"kv_proj_b{gi}", ncb=4, perm_d=dil[gi], comm=comm,
                             cbmap=functools.partial(lambda jj, gi: 2 * gi + (jj // 2) * 2 * NB + jj % 2, gi=gi)) for gi in range(NB)]
                kv_state = {"x": xcur, "hkv": hkv}
            wts["wq", l] = weight(("wq", l), (N_CHIPS, D, q_ns))
            qs, os_, lses = [], [], []
            for gi in range(NB):
                q_b = mm_nn(h1, wts["wq", l], tn=q_tn, out_dtype=BF16, name=f"q_proj_b{gi}", ncb=4, perm_d=dil[gi], comm=comm,
                            cbmap=functools.partial(lambda jj, gi: 4 * gi + jj, gi=gi))
                o_b, l_b = attn_branch_fwd(q_b, kvs[gi], gi, slopes[gi], f"attn_fwd_b{gi}", comm=comm)
                if dil[gi] > 1:
                    o_b = unpermute_rows(o_b, dil[gi], f"unpermute_o_b{gi}")
                    l_b = unpermute_rows(l_b, dil[gi], f"unpermute_lse_b{gi}")
                qs.append(q_b)
                os_.append(o_b)
                lses.append(l_b)
            o, lse = attn_combine(os_, lses, "attn_combine")
            wts["wo", l] = weight(("wo", l), (1, DA, D))
            out, x1 = mm_nn(o, wts["wo", l], tn=D // 2, out_dtype=BF16, name="attn_out_proj", res=(xcur, g1), comm=comm)
            st.update(qs=qs, o=o, lse=lse, out1=out)
        st["x1"] = x1
        h2 = norm_mod(x1, vec(norm2_g[l]), sh2, sc2, "norm_mod", comm=comm)
        wts["up", l] = weight(("up", l), (N_CHIPS, D, up_ns))
        hu, gated = ffn_up_act(h2, wts["up", l], convw_full[l], vec(ffn_conv_b[l]), "ffn_up_act", comm=comm)
        wts["down", l] = weight(("down", l), (1, F, D))
        out2, x2 = mm_nn(gated, wts["down", l], tn=D // 2, out_dtype=BF16, name="ffn_down_proj", res=(x1, g2), comm=comm)
        st.update(h2=h2, hu=hu, gated=gated, out2=out2)
        saved.append(st)
        xcur = x2
    comm.flush()

    dx, fsums = loss_fwd_bwd(xcur, vec(final_g), tgt, "loss_head")
    loss = lax.psum(0.5 * jnp.sum(fsums[1]) / D, ("x", "y", "c"))
    d_final_g = fsums[0]

    dmods = [None] * depth
    d_n1 = [None] * depth
    d_n2 = [None] * depth
    d_convw = [None] * depth
    d_convb = [None] * depth
    d_scale = [None] * n_pool
    d_grp = [None] * n_pool
    dkvs = [[] for _ in range(NB)]
    exchanged = []
    f_tk = _tile(F, 1408, LANES)
    ct_blocks = DA // _tile(DA, 256, LANES)

    def exchange(name, idx, dw):
        dw4 = dw.reshape(N_CHIPS, -1, dw.shape[-1])
        exchanged.append((name, idx, dw4))
        push_exchange(comm, (name, idx), dw4)

    for l in reversed(range(depth)):
        st = saved[l]
        sh1, sc1, g1, sh2, sc2, g2 = [mod[l, i] for i in range(6)]
        dout2, s_g2 = gate_bwd(dx, st["out2"], g2, "gate_bwd")
        dout2_3 = dout2.reshape(1, S, D)
        dgated = mm_nt(dout2_3, wts["down", l], tn=D, tk=f_tk, out_dtype=BF16, name="ffn_down_bwd", comm=comm)
        exchange("down", l, mm_tn(st["gated"], dout2_3, (1, F, D), tn=D, tk=f_tk, name="ffn_down_dw", comm=comm))
        dhu, s_conv = ffn_act_bwd(dgated, st["hu"], convw_full[l], vec(ffn_conv_b[l]), "ffn_act_bwd", comm=comm)
        dh2 = mm_nt(dhu, wts["up", l], tn=up_tn, tk=D, out_dtype=F32, name="ffn_up_bwd", gmap=up_gmap, comm=comm)
        exchange("up", l, mm_tn(st["h2"], dhu, (N_CHIPS, D, up_ns), tn=up_tn, tk=D, name="ffn_up_dw", gmap=up_gmap, comm=comm))
        dx, s_n2 = norm_mod_bwd(dh2, st["x1"], dx, vec(norm2_g[l]), sc2, "norm_mod_bwd")
        d_convw[l], d_convb[l] = s_conv[0:3], s_conv[3]
        d_n2[l] = s_n2[2]
        dout1, s_g1 = gate_bwd(dx, st["out1"], g1, "gate_bwd")
        dout1_3 = dout1.reshape(1, S, D)
        if l < n_pool:
            dys = mm_nt(dout1_3, wts["pout", l], tn=D, tk=D // 2, out_dtype=F32, name="pool_out_bwd", comm=comm)
            exchange("pout", l, mm_tn(st["ys"], dout1_3, (1, D, D), tn=D, tk=D, name="pool_out_dw", comm=comm))
            du, d_grp, s_sc = pool_bwd(dys, st["z"], st["pooled"], wts["pgrp", l], vec(scale_full[l]), "pool_mix_bwd")
            exchange("pgrp", l, d_grp.astype(BF16).reshape(G, N_CHIPS, C // N_CHIPS, C).transpose(1, 0, 2, 3))
            d_scale[l] = s_sc[0]
            du_3 = du.reshape(1, S, D)
            dh1 = mm_nt(du_3, wts["pin", l], tn=D, tk=D // 2, out_dtype=F32, name="pool_in_bwd", comm=comm)
            exchange("pin", l, mm_tn(st["h1"], du_3, (1, D, D), tn=D, tk=D, name="pool_in_dw", comm=comm))
        else:
            j = l - n_pool
            do = mm_nt(dout1_3, wts["wo", l], tn=D, tk=DA // 2, out_dtype=BF16, name="attn_out_bwd", comm=comm)
            exchange("wo", j, mm_tn(st["o"], dout1_3, (1, DA, D), tn=D, tk=DA, name="attn_out_dw", comm=comm))
            dlt = attn_delta(do, st["o"], "attn_delta")
            dq = None
            for gi in range(NB):
                d = dil[gi]
                do_b, l_b, dl_b = do, st["lse"], dlt
                if d > 1:
                    do_b = permute_rows(do, d, f"permute_do_b{gi}")
                    l_b = permute_rows(st["lse"], d, f"permute_lse_b{gi}")
                    dl_b = permute_rows(dlt, d, f"permute_delta_b{gi}")
                bwd_name = f"attn_bwd_b{gi}"
                if d > 1:
                    dq_b, dkv_b = attn_branch_bwd(st["qs"][gi], kvs[gi], do_b, l_b, dl_b, gi, slopes[gi], bwd_name, comm=comm)
                    dq = unpermute_rows(dq_b, d, f"unpermute_dq_b{gi}", into=dq, total_cols=NB * DA,
                                        colmap=functools.partial(lambda jj, gi: gi * ct_blocks + jj, gi=gi))
                else:
                    dq, dkv_b = attn_branch_bwd(st["qs"][gi], kvs[gi], do_b, l_b, dl_b, gi, slopes[gi], bwd_name,
                                                out_cols=NB * DA, comm=comm)
                dkvs[gi].append(dkv_b)
            dq_3 = dq.reshape(1, S, NB * DA)
            dh1 = mm_nt(dq_3, wts["wq", l], tn=q_bwd_tn, tk=D, out_dtype=F32, name="q_proj_bwd", comm=comm)
            exchange("wq", j, mm_tn(st["h1"], dq_3, (N_CHIPS, D, q_ns), tn=q_bwd_tn, tk=D, name="q_proj_dw", comm=comm))
        dx, s_n1 = norm_mod_bwd(dh1, st["x0"], dx, vec(norm1_g[l]), sc1, "norm_mod_bwd")
        d_n1[l] = s_n1[2]
        dmods[l] = jnp.stack([s_n1[0], s_n1[1], s_g1[0], s_n2[0], s_n2[1], s_g2[0]])
        if l == n_pool:
            dkv = None
            for gi in range(NB):
                dkv = unpermute_rows(dkvs[gi], dil[gi], f"unpermute_dkv_b{gi}", into=dkv, total_cols=2 * NB * DA,
                                     colmap=functools.partial(lambda jj, gi: (jj // ct_blocks) * NB * ct_blocks + gi * ct_blocks + jj % ct_blocks,
                                                              gi=gi))
            dkv_3 = dkv.reshape(1, S, 2 * NB * DA)
            dhkv = mm_nt(dkv_3, wts["kv", 0], tn=kv_ns // 2, tk=D, out_dtype=F32, name="kv_proj_bwd", comm=comm)
            exchange("kv", 0, mm_tn(kv_state["hkv"], dkv_3, (N_CHIPS, D, kv_ns), tn=kv_ns // 2, tk=D, name="kv_proj_dw", comm=comm))
            dx, s_kv = norm_mod_bwd(dhkv, kv_state["x"], dx, vec(kv_norm_g), kvmod[1], "norm_mod_bwd")
    grad_x = dx.reshape(1, S, D)

    smalls = [jnp.stack(dmods), jnp.stack([s_kv[0], s_kv[1]]), jnp.stack(d_n1), jnp.stack(d_n2), s_kv[2], jnp.stack(d_convb), d_final_g,
              jnp.stack(d_scale), jnp.stack(d_convw)]
    small_shapes = [s.shape for s in smalls]
    spk, soffs = _pack(smalls)
    srows = spk.shape[0]
    sgot = all_gather8(spk, "gather_small_grads").reshape(N_DEV, srows, PACK_W)
    ssum = sum_rows8(sgot, "sum_small_grads")
    g_mods, g_kvmod, g_n1, g_n2, g_kvn, g_convb, g_fg, g_scale_full, g_convw_full = _unpack(ssum, soffs, small_shapes)
    g_ada_b = g_mods.reshape(depth, 6 * D)
    g_kv_ada_b = g_kvmod.reshape(2 * D)
    g_scale = lax.dynamic_slice(g_scale_full, (0, p_me * Dq), (n_pool, Dq))
    g_convw = lax.dynamic_slice(g_convw_full, (0, 0, p_me * Fs), (depth, 3, Fs))

    small_w = [ada_b, norm1_g, norm2_g, kv_norm_g, kv_ada_b, ffn_conv_b, final_g, pool_scale, ffn_conv_w]
    small_m = [m_ada_b, m_norm1_g, m_norm2_g, m_kv_norm_g, m_kv_ada_b, m_ffn_conv_b, m_final_g, m_pool_scale, m_ffn_conv_w]
    small_v = [v_ada_b, v_norm1_g, v_norm2_g, v_kv_norm_g, v_kv_ada_b, v_ffn_conv_b, v_final_g, v_pool_scale, v_ffn_conv_w]
    small_g = [g_ada_b, g_n1, g_n2, g_kvn, g_kv_ada_b, g_convb, g_fg, g_scale, g_convw]
    sw_shapes = [w.shape for w in small_w]
    pw, woffs = _pack(small_w)
    s_res = adamw(pw, _pack(small_g)[0], _pack(small_m)[0], _pack(small_v)[0], "adamw_small")
    s_g, s_dl, s_m, s_v = [_unpack(r, woffs, sw_shapes) for r in s_res]

    per_dev = sgot.reshape(N_DEV, -1)
    dm_all = per_dev[:, int(soffs[0]):int(soffs[1])].reshape(N_DEV, depth, 6 * D)
    dkvm_all = per_dev[:, int(soffs[1]):int(soffs[2])].reshape(N_DEV, 1, 2 * D)

    def shard_cols(a, ns):
        sl = lax.dynamic_slice_in_dim(a, p_me * ns, ns, axis=2).transpose(1, 0, 2)
        return jnp.concatenate([sl, jnp.zeros_like(sl)], axis=1)

    ada_res = ada_grad_adamw(c16, shard_cols(dm_all, ada_ns), ada_w, m_ada_w, v_ada_w, "ada_grad_adamw", comm=comm)
    kvada_res = ada_grad_adamw(c16, shard_cols(dkvm_all, kvada_ns), kv_ada_w.reshape(1, D, kvada_ns), m_kv_ada_w.reshape(1, D, kvada_ns),
                               v_kv_ada_w.reshape(1, D, kvada_ns), "kv_ada_grad_adamw")
    kvada_res = [r.reshape(D, kvada_ns) for r in kvada_res]

    comm.flush()
    big_names = ["pin", "pgrp", "pout", "kv", "wq", "wo", "up", "down"]
    n_stack = {"pin": n_pool, "pgrp": n_pool, "pout": n_pool, "kv": 1, "wq": n_attn, "wo": n_attn, "up": depth, "down": depth}
    gsum = {nm: None for nm in big_names}
    for nm, idx, dw4 in exchanged:
        gsum[nm] = sum_partials(dw4, comm.store[nm, idx], gsum[nm], idx, n_stack[nm], pos, "sum_partials")
    for nm in big_names:
        g = gsum[nm]

        def done(outs, nm=nm):
            gsum[nm] = outs[0]

        comm.push(_Phase(("swap", nm), nm, None, 0.0, lambda: [], lambda g=g: [g], [], g.shape[0], 0, _build_swap, done))
    comm.flush()
    big_m = [m_pool_w_in, m_pool_w_grp, m_pool_w_out, m_w_kv, m_attn_w_q, m_attn_w_o, m_ffn_w_up, m_ffn_w_down]
    big_v = [v_pool_w_in, v_pool_w_grp, v_pool_w_out, v_w_kv, v_attn_w_q, v_attn_w_o, v_ffn_w_up, v_ffn_w_down]
    big_w = [pool_w_in, pool_w_grp, pool_w_out, w_kv, attn_w_q, attn_w_o, ffn_w_up, ffn_w_down]
    big_res = []
    for nm, w, m_, v_ in zip(big_names, big_w, big_m, big_v):
        cols = gsum[nm].shape[-1]
        res = adamw(w.reshape(-1, cols), gsum[nm].reshape(-1, cols), m_.reshape(-1, cols), v_.reshape(-1, cols), "adamw_big")
        big_res.append([r.reshape(w.shape) for r in res])

    order = ["ada_w", "ada_b", "norm1_g", "norm2_g", "pool_w_in", "pool_w_grp", "pool_scale", "pool_w_out", "kv_norm_g", "kv_ada_w",
             "kv_ada_b", "w_kv", "attn_w_q", "attn_w_o", "ffn_w_up", "ffn_conv_w", "ffn_conv_b", "ffn_w_down", "final_g"]
    small_names = ["ada_b", "norm1_g", "norm2_g", "kv_norm_g", "kv_ada_b", "ffn_conv_b", "final_g", "pool_scale", "ffn_conv_w"]
    results = {"ada_w": ada_res, "kv_ada_w": kvada_res}
    for i, nm in enumerate(small_names):
        results[nm] = [s_g[i], s_dl[i], s_m[i], s_v[i]]
    for i, nm in enumerate(["pool_w_in", "pool_w_grp", "pool_w_out", "w_kv", "attn_w_q", "attn_w_o", "ffn_w_up", "ffn_w_down"]):
        results[nm] = big_res[i]
    outs = [loss, grad_x]
    for kind in range(4):
        outs += [results[nm][kind] for nm in order]
    return tuple(outs)
```

```python
import functools
import math

import numpy as np
import jax
import jax.numpy as jnp
from jax import lax
from jax.experimental import pallas as pl
from jax.experimental.pallas import tpu as pltpu

F32 = jnp.float32
BF16 = jnp.bfloat16
MESH = pl.DeviceIdType.MESH

POOL_WINDOWS = (2, 4, 8, 16)
BRANCHES = ((128, 1), (512, 4), (2048, 16))
HEAD_DIM = 64
ATTN_BLOCK = 128
EPS = 1e-6
LR, B1, B2, ADAM_EPS, WD, STEP = 0.001, 0.9, 0.999, 1e-08, 0.01, 10

VMEM_LIMIT_BYTES = 56 * 1024 * 1024
LANES = 128
PACK_W = 1024
HALO = 16
NEG = -1e30
N_CHIPS = 4
N_DEV = 8


def _alibi_slopes(n):
    def pow2(m):
        start = 2.0 ** (-(2.0 ** -(math.log2(m) - 3)))
        return [start ** (i + 1) for i in range(m)]
    if math.log2(n).is_integer():
        s = pow2(n)
    else:
        c = 2 ** math.floor(math.log2(n))
        s = pow2(c) + pow2(2 * c)[0::2][: n - c]
    s = np.asarray(s, dtype=np.float32)
    return -np.sort(-s)


def _cparams(sem=None):
    return pltpu.CompilerParams(dimension_semantics=sem, vmem_limit_bytes=VMEM_LIMIT_BYTES)


def _tile(n, pref, unit):
    t = (min(pref, n) // unit) * unit
    while t >= unit:
        if n % t == 0:
            return t
        t -= unit
    return n


def _sigmoid(v):
    return 1.0 / (1.0 + jnp.exp(-v))


def all_gather8(xs, name):
    m_per, n = xs.shape

    def body(x_ref, out_ref, send_sems, recv_sems, local_sem):
        x, y, c = lax.axis_index("x"), lax.axis_index("y"), lax.axis_index("c")
        me, sibling = (x, y, c), (x, y, 1 - c)
        chips = [(1 - x, y), (x, 1 - y), (1 - x, 1 - y)]

        def rows(px, py, pc):
            return out_ref.at[pl.ds((4 * px + 2 * py + pc) * m_per, m_per), :]

        def copy(k, block, to, src=None):
            return pltpu.make_async_remote_copy(src_ref=rows(*block) if src is None else src, dst_ref=rows(*block),
                                                send_sem=send_sems.at[k], recv_sem=recv_sems.at[k], device_id=to, device_id_type=MESH)

        mine = pltpu.make_async_copy(x_ref, rows(*me), local_sem)
        mine.start()
        first = [copy(0, me, sibling, src=x_ref)]
        first += [copy(1 + j, me, (*chip, c), src=x_ref) for j, chip in enumerate(chips)]
        for cp in first:
            cp.start()
        passed = [copy(4 + j, (*chip, c), sibling) for j, chip in enumerate(chips)]
        for j, chip in enumerate(chips):
            copy(1 + j, (*chip, c), me).wait_recv()
            passed[j].start()
        copy(0, sibling, me).wait_recv()
        for j, chip in enumerate(chips):
            copy(4 + j, (*chip, 1 - c), me).wait_recv()
        for cp in first + passed:
            cp.wait_send()
        mine.wait()

    return pl.pallas_call(
        body, name=name,
        out_shape=jax.ShapeDtypeStruct((N_DEV * m_per, n), xs.dtype),
        in_specs=[pl.BlockSpec(memory_space=pltpu.VMEM)],
        out_specs=pl.BlockSpec(memory_space=pltpu.VMEM),
        scratch_shapes=[pltpu.SemaphoreType.DMA((7,)), pltpu.SemaphoreType.DMA((7,)), pltpu.SemaphoreType.DMA],
        compiler_params=pltpu.CompilerParams(vmem_limit_bytes=VMEM_LIMIT_BYTES),
    )(xs)


HBM_SPEC = pl.BlockSpec(memory_space=pltpu.HBM)


def _mesh_pos():
    x, y, c = lax.axis_index("x"), lax.axis_index("y"), lax.axis_index("c")
    return x, y, c, [(1 - x, y), (x, 1 - y), (1 - x, 1 - y)]


class _Phase:
    def __init__(self, key, group, after, est_us, get_ins, get_inouts, new_outs, n_sems, n_local, build, on_done):
        self.key, self.group, self.after, self.est_us = key, group, after, est_us
        self.get_ins, self.get_inouts, self.new_outs = get_ins, get_inouts, new_outs
        self.n_sems, self.n_local, self.build, self.on_done = n_sems, n_local, build, on_done


def _rcopy(src, dst, send_sems, recv_sems, k, to):
    return pltpu.make_async_remote_copy(src_ref=src, dst_ref=dst, send_sem=send_sems.at[k], recv_sem=recv_sems.at[k],
                                        device_id=to, device_id_type=MESH)


def _build_fetch(in_refs, out_refs, send_sems, recv_sems, loc_sems, sem0, loc0):
    (shard,), (g,) = in_refs, out_refs
    x, y, c, chips = _mesh_pos()
    p_me = 2 * x + y
    locs = [pltpu.make_async_copy(shard.at[i], g.at[p_me, i], loc_sems.at[loc0 + i]) for i in range(2)]
    sends = [_rcopy(shard.at[c], g.at[p_me, c], send_sems, recv_sems, sem0 + j, (*chip, c)) for j, chip in enumerate(chips)]

    def recvs():
        blks = [g.at[2 * chip[0] + chip[1], c] for chip in chips]
        return [_rcopy(blk, blk, send_sems, recv_sems, sem0 + j, (*chip, c)) for j, (blk, chip) in enumerate(zip(blks, chips))]
    return sends, recvs, locs


def _build_pass(in_refs, out_refs, send_sems, recv_sems, loc_sems, sem0, loc0):
    (g,) = out_refs
    x, y, c, chips = _mesh_pos()
    sib = (x, y, 1 - c)
    slots = [2 * chip[0] + chip[1] for chip in chips]
    sends = [_rcopy(g.at[p, c], g.at[p, c], send_sems, recv_sems, sem0 + j, sib) for j, p in enumerate(slots)]

    def recvs():
        return [_rcopy(g.at[p, 1 - c], g.at[p, 1 - c], send_sems, recv_sems, sem0 + j, sib) for j, p in enumerate(slots)]
    return sends, recvs, []


def _build_exchange(in_refs, out_refs, send_sems, recv_sems, loc_sems, sem0, loc0):
    (dw,), (recv,) = in_refs, out_refs
    x, y, c, chips = _mesh_pos()
    targets = [(c, chip, c, j) for j, chip in enumerate(chips)]
    targets += [(1 - c, chip, 1 - c, 3 + j) for j, chip in enumerate([(x, y)] + chips)]
    sends = [_rcopy(dw.at[2 * chip[0] + chip[1], half], recv.at[rel], send_sems, recv_sems, sem0 + rel, (*chip, core))
             for half, chip, core, rel in targets]

    def recvs():
        return [_rcopy(recv.at[rel], recv.at[rel], send_sems, recv_sems, sem0 + rel, (x, y, 1 - c)) for rel in range(7)]
    return sends, recvs, []


def _build_swap(in_refs, out_refs, send_sems, recv_sems, loc_sems, sem0, loc0):
    (g,) = out_refs
    x, y, c, _ = _mesh_pos()
    sib = (x, y, 1 - c)
    sends = [_rcopy(g.at[l, c], g.at[l, c], send_sems, recv_sems, sem0 + l, sib) for l in range(g.shape[0])]

    def recvs():
        return [_rcopy(g.at[l, 1 - c], g.at[l, 1 - c], send_sems, recv_sems, sem0 + l, sib) for l in range(g.shape[0])]
    return sends, recvs, []


def _plan_refs(phases):
    xin, xout, alias, n_sems, n_loc, out_of = [], [], {}, 0, 0, {}
    for ph in phases:
        ph.sem0, ph.loc0 = n_sems, n_loc
        n_sems += ph.n_sems
        n_loc += ph.n_local
        if ph.after in out_of:
            ph.in0, ph.n_in, ph.out0, ph.n_out = 0, 0, out_of[ph.after], 1
            continue
        ins, inouts = ph.get_ins(), ph.get_inouts()
        ph.in0, ph.n_in = len(xin), len(ins)
        xin += ins
        ph.out0, ph.n_out = len(xout), len(inouts) + len(ph.new_outs)
        out_of[ph.key] = ph.out0
        for a in inouts:
            alias[len(xin)] = len(xout)
            xin.append(a)
            xout.append(jax.ShapeDtypeStruct(a.shape, a.dtype))
        xout += ph.new_outs
    return xin, xout, alias, max(n_sems, 1), max(n_loc, 1)


def _built(ph, xin_refs, xout_refs, sems):
    return ph.build(xin_refs[ph.in0:ph.in0 + ph.n_in], xout_refs[ph.out0:ph.out0 + ph.n_out], sems[0], sems[1], sems[2], ph.sem0, ph.loc0)


def _start(phases, xin_refs, xout_refs, sems):
    for ph in phases:
        sends, _, locs = _built(ph, xin_refs, xout_refs, sems)
        for cp in locs + sends:
            cp.start()


def _finish(phases, xin_refs, xout_refs, sems):
    for ph in phases:
        sends, recvs, locs = _built(ph, xin_refs, xout_refs, sems)
        for cp in recvs():
            cp.wait_recv()
        for cp in sends:
            cp.wait_send()
        for cp in locs:
            cp.wait()


class _Comm:
    def __init__(self):
        self.queue, self.store, self.n_alone = [], {}, 0

    def push(self, ph):
        self.queue.append(ph)

    def take(self, carry_us):
        taken, t = [], 0.0
        while True:
            pending = {ph.key for ph in self.queue} | {ph.key for ph in taken}
            room = 2.2 * carry_us if not taken else 0.8 * carry_us - t
            fits = [ph for ph in self.queue if ph.after not in pending and ph.est_us <= room]
            if not fits:
                return taken
            ph = max(fits, key=lambda p: p.est_us)
            self.queue.remove(ph)
            taken.append(ph)
            t += ph.est_us

    def require(self, group):
        phases = [ph for ph in self.queue if ph.group == group]
        if phases:
            self.queue = [ph for ph in self.queue if ph.group != group]
            self.run_alone(phases)
        return self.store[group]

    def flush(self):
        phases, self.queue = self.queue, []
        if phases:
            self.run_alone(phases)

    def run_alone(self, phases):
        phases = [ph for ph in phases if ph.after is None] + [ph for ph in phases if ph.after is not None]
        groups, keys = [[]], set()
        for ph in phases:
            if ph.after in keys:
                groups.append([])
                keys = set()
            groups[-1].append(ph)
            keys.add(ph.key)
        xin, xout, alias, n_sems, n_loc = _plan_refs(phases)
        n_xin, n_xout = len(xin), len(xout)

        def body(*refs):
            xin_refs, xout_refs, sems = refs[:n_xin], refs[n_xin:n_xin + n_xout], refs[n_xin + n_xout:]
            for grp in groups:
                _start(grp, xin_refs, xout_refs, sems)
                _finish(grp, xin_refs, xout_refs, sems)

        self.n_alone += 1
        outs = pl.pallas_call(
            body, name=f"comm_alone_{self.n_alone}", out_shape=xout, in_specs=[HBM_SPEC] * n_xin, out_specs=[HBM_SPEC] * n_xout,
            input_output_aliases=alias,
            scratch_shapes=[pltpu.SemaphoreType.DMA((n_sems,)), pltpu.SemaphoreType.DMA((n_sems,)), pltpu.SemaphoreType.DMA((n_loc,))],
        )(*xin)
        for ph in phases:
            ph.on_done(outs[ph.out0:ph.out0 + ph.n_out])


def _pcall(body, *, name, grid, in_specs, out_specs, out_shape, args, scratch_shapes=(), aliases=None, comm=None, carry_us=0.0):
    phases = comm.take(carry_us) if comm is not None else []
    n_in, n_out, n_scr = len(in_specs), len(out_specs), len(scratch_shapes)
    if not phases:
        return pl.pallas_call(body, name=name, grid=grid, in_specs=in_specs, out_specs=out_specs, out_shape=out_shape,
                              scratch_shapes=list(scratch_shapes), input_output_aliases=aliases or {},
                              compiler_params=_cparams(("arbitrary",) * len(grid)))(*args)
    xin, xout, xalias, n_sems, n_loc = _plan_refs(phases)
    n_xin, n_xout = len(xin), len(xout)
    all_alias = dict(aliases or {})
    all_alias.update({n_in + i: n_out + o for i, o in xalias.items()})

    def carrier(*refs):
        ins, xin_refs = refs[:n_in], refs[n_in:n_in + n_xin]
        outs = refs[n_in + n_xin:n_in + n_xin + n_out]
        xout_refs = refs[n_in + n_xin + n_out:n_in + n_xin + n_out + n_xout]
        rest = refs[n_in + n_xin + n_out + n_xout:]
        scr, sems = rest[:n_scr], rest[n_scr:]
        pids = [pl.program_id(k) for k in range(len(grid))]
        first = functools.reduce(jnp.logical_and, [p == 0 for p in pids])
        last = functools.reduce(jnp.logical_and, [p == n - 1 for p, n in zip(pids, grid)])

        @pl.when(first)
        def _():
            _start(phases, xin_refs, xout_refs, sems)
        body(*ins, *outs, *scr)

        @pl.when(last)
        def _():
            _finish(phases, xin_refs, xout_refs, sems)

    outs = pl.pallas_call(
        carrier, name=name, grid=grid, in_specs=list(in_specs) + [HBM_SPEC] * n_xin, out_specs=list(out_specs) + [HBM_SPEC] * n_xout,
        out_shape=list(out_shape) + xout,
        scratch_shapes=list(scratch_shapes) + [pltpu.SemaphoreType.DMA((n_sems,)), pltpu.SemaphoreType.DMA((n_sems,)),
                                               pltpu.SemaphoreType.DMA((n_loc,))],
        input_output_aliases=all_alias, compiler_params=_cparams(("arbitrary",) * len(grid)))(*args, *xin)
    for ph in phases:
        ph.on_done(outs[n_out + ph.out0:n_out + ph.out0 + ph.n_out])
    return outs[:n_out]


FETCH_US_PER_MB = 20.4
PASS_US_PER_MB = 3.3
EXCHANGE_US_PER_MB = 14.5


def push_gather(comm, keys_shards):
    prev = None
    for key, shard in keys_shards:
        r, c = shard.shape
        sh = shard.reshape(2, r // 2, c)
        half_mb = r // 2 * c * 2 / 1e6

        def done(outs, key=key):
            comm.store[key] = outs[0]

        comm.push(_Phase(("fetch", key), key, None, 3 * half_mb * FETCH_US_PER_MB, lambda sh=sh: [sh], lambda: [],
                         [jax.ShapeDtypeStruct((N_CHIPS, 2, r // 2, c), BF16)], 3, 2, _build_fetch, done))
        if prev is not None:
            comm.push(prev)
        prev = _Phase(("pass", key), key, ("fetch", key), 3 * half_mb * PASS_US_PER_MB + 3.0, lambda: [], lambda key=key: [comm.store[key]],
                      [], 3, 0, _build_pass, done)
    if prev is not None:
        comm.push(prev)


def push_exchange(comm, key, dw):
    _, r, c = dw.shape
    half_mb = r // 2 * c * 2 / 1e6

    def done(outs):
        comm.store[key] = outs[0]

    comm.push(_Phase(("exchange", key), key, None, 6 * half_mb * EXCHANGE_US_PER_MB, lambda: [dw.reshape(N_CHIPS, 2, r // 2, c)], lambda: [],
                     [jax.ShapeDtypeStruct((7, r // 2, c), BF16)], 7, 0, _build_exchange, done))


MM_FLOPS_PER_US = 6.0e8


def mm_nn(a, w3, *, tn, out_dtype, name, ncb=None, cbmap=None, res=None, perm_d=1, comm=None):
    M, K = a.shape
    P, _, Ns = w3.shape
    nper = Ns // tn
    ncb = P * nper if ncb is None else ncb
    tm = ATTN_BLOCK * perm_d if perm_d > 1 else _tile(M, 512, 8)
    cbm = cbmap if cbmap is not None else (lambda j: j)
    nch = tn // LANES

    def body(*refs):
        if res is None:
            a_ref, w_ref, o_ref = refs[:3]
        else:
            a_ref, w_ref, x_ref, g_ref, o_ref, xo_ref = refs
        acc = jnp.dot(a_ref[...].astype(BF16), w_ref[...], preferred_element_type=F32)
        if perm_d > 1:
            scr = refs[3]
            for cj in range(nch):
                scr[cj] = acc[:, cj * LANES:(cj + 1) * LANES]
            for r in range(perm_d):
                for cj in range(nch):
                    o_ref[r, :, cj * LANES:(cj + 1) * LANES] = scr.at[cj][pl.ds(r, ATTN_BLOCK, stride=perm_d), :].astype(o_ref.dtype)
        else:
            o_ref[...] = acc.astype(o_ref.dtype)
        if res is not None:
            xo_ref[...] = x_ref[...] + g_ref[...] * acc

    in_specs = [pl.BlockSpec((tm, K), lambda i, j: (i, 0)),
                pl.BlockSpec((None, K, tn), lambda i, j: (cbm(j) // nper, 0, cbm(j) % nper))]
    scratch = []
    if perm_d > 1:
        out_specs = [pl.BlockSpec((perm_d, ATTN_BLOCK, tn), lambda i, j: (0, i, j))]
        out_shape = [jax.ShapeDtypeStruct((perm_d, M // perm_d, ncb * tn), out_dtype)]
        scratch = [pltpu.VMEM((nch, tm, LANES), F32)]
    else:
        out_specs = [pl.BlockSpec((tm, tn), lambda i, j: (i, j))]
        out_shape = [jax.ShapeDtypeStruct((M, ncb * tn), out_dtype)]
    args = [a, w3]
    if res is not None:
        in_specs += [pl.BlockSpec((tm, tn), lambda i, j: (i, j)), pl.BlockSpec((1, tn), lambda i, j: (0, j))]
        out_specs.append(pl.BlockSpec((tm, tn), lambda i, j: (i, j)))
        out_shape.append(jax.ShapeDtypeStruct((M, ncb * tn), F32))
        args += [res[0], res[1]]
    outs = _pcall(body, name=name, grid=(M // tm, ncb), in_specs=in_specs, out_specs=out_specs, out_shape=out_shape, args=args,
                  scratch_shapes=scratch, comm=comm, carry_us=2.0 * M * K * ncb * tn / MM_FLOPS_PER_US)
    if perm_d > 1:
        return outs[0].reshape(M, ncb * tn)
    return outs[0] if res is None else (outs[0], outs[1])


def permute_rows(x, d, name):
    S, C = x.shape
    R = ATTN_BLOCK * d
    ct = _tile(C, 256, LANES)
    nch = ct // LANES

    def body(x_ref, o_ref, scr):
        xv = x_ref[...].astype(F32)
        for cj in range(nch):
            scr[cj] = xv[:, cj * LANES:(cj + 1) * LANES]
        for r in range(d):
            for cj in range(nch):
                o_ref[r, :, cj * LANES:(cj + 1) * LANES] = scr.at[cj][pl.ds(r, ATTN_BLOCK, stride=d), :].astype(o_ref.dtype)

    out = pl.pallas_call(body, name=name, grid=(S // R, C // ct), in_specs=[pl.BlockSpec((R, ct), lambda i, j: (i, j))],
                         out_specs=pl.BlockSpec((d, ATTN_BLOCK, ct), lambda i, j: (0, i, j)),
                         out_shape=jax.ShapeDtypeStruct((d, S // d, C), x.dtype), scratch_shapes=[pltpu.VMEM((nch, R, LANES), F32)],
                         compiler_params=_cparams(("parallel", "parallel")))(x)
    return out.reshape(S, C)


def unpermute_rows(ps, d, name, into=None, total_cols=None, colmap=None):
    ps = list(ps) if isinstance(ps, (list, tuple)) else [ps]
    n_p = len(ps)
    p = ps[0]
    S, C = p.shape
    rpb = max(ATTN_BLOCK, 512 // d)
    R = rpb * d
    ct = _tile(C, 256, LANES)
    nch = ct // LANES
    total_cols = C if total_cols is None else total_cols
    cm = colmap if colmap is not None else (lambda j: j)

    def body(*refs):
        p_refs, o_ref, scr = refs[:n_p], refs[-2], refs[-1]

        def summed(idx):
            return functools.reduce(lambda a, b: a + b, [r[idx] for r in p_refs])
        if d == 1:
            o_ref[...] = summed(0)
            return
        for r in range(d):
            for cj in range(nch):
                scr.at[cj][pl.ds(r, rpb, stride=d), :] = summed((r, slice(None), slice(cj * LANES, (cj + 1) * LANES))).astype(F32)
        for cj in range(nch):
            o_ref[:, cj * LANES:(cj + 1) * LANES] = scr[cj].astype(o_ref.dtype)

    in_specs = [pl.BlockSpec((d, rpb, ct), lambda i, j: (0, i, j))] * n_p
    args = [a.reshape(d, S // d, C) for a in ps]
    aliases = {}
    if into is not None:
        in_specs.append(pl.BlockSpec(memory_space=pl.ANY))
        args.append(into)
        aliases = {n_p: 0}
    return pl.pallas_call(body, name=name, grid=(S // R, C // ct), in_specs=in_specs,
                          out_specs=pl.BlockSpec((R, ct), lambda i, j: (i, cm(j))),
                          out_shape=jax.ShapeDtypeStruct((S, total_cols), p.dtype), scratch_shapes=[pltpu.VMEM((nch, R, LANES), F32)],
                          input_output_aliases=aliases, compiler_params=_cparams(("parallel", "parallel")))(*args)


def mm_nt(g3, w3, *, tn, tk, out_dtype, name, gmap=None, comm=None):
    _, M, _ = g3.shape
    P, K, Ns = w3.shape
    nper = Ns // tn
    ns = P * nper
    tm = _tile(M, 512, 8)
    gm = gmap if gmap is not None else (lambda s: (0, s))

    def body(g_ref, w_ref, o_ref, acc):
        s = pl.program_id(2)

        @pl.when(s == 0)
        def _():
            acc[...] = jnp.zeros_like(acc)
        acc[...] += lax.dot_general(g_ref[...].astype(BF16), w_ref[...], (((1,), (1,)), ((), ())), preferred_element_type=F32)

        @pl.when(s == ns - 1)
        def _():
            o_ref[...] = acc[...].astype(o_ref.dtype)

    return _pcall(
        body, name=name, grid=(M // tm, K // tk, ns),
        in_specs=[pl.BlockSpec((None, tm, tn), lambda i, kj, s: (gm(s)[0], i, gm(s)[1])),
                  pl.BlockSpec((None, tk, tn), lambda i, kj, s: (s // nper, kj, s % nper))],
        out_specs=[pl.BlockSpec((tm, tk), lambda i, kj, s: (i, kj))],
        out_shape=[jax.ShapeDtypeStruct((M, K), out_dtype)], args=[g3, w3],
        scratch_shapes=[pltpu.VMEM((tm, tk), F32)], comm=comm, carry_us=2.0 * M * K * P * Ns / MM_FLOPS_PER_US)[0]


def mm_tn(a, g3, wshape, *, tn, tk, name, gmap=None, comm=None):
    M, K = a.shape
    P, _, Ns = wshape
    nper = Ns // tn
    ns = P * nper
    tm = _tile(M, 512, 16)
    nm = M // tm
    gm = gmap if gmap is not None else (lambda s: (0, s))

    def body(a_ref, g_ref, o_ref, acc):
        mi = pl.program_id(2)

        @pl.when(mi == 0)
        def _():
            acc[...] = jnp.zeros_like(acc)
        acc[...] += lax.dot_general(a_ref[...].astype(BF16), g_ref[...].astype(BF16), (((0,), (0,)), ((), ())), preferred_element_type=F32)

        @pl.when(mi == nm - 1)
        def _():
            o_ref[...] = acc[...].astype(o_ref.dtype)

    return _pcall(
        body, name=name, grid=(ns, K // tk, nm),
        in_specs=[pl.BlockSpec((tm, tk), lambda s, kj, mi: (mi, kj)),
                  pl.BlockSpec((None, tm, tn), lambda s, kj, mi: (gm(s)[0], mi, gm(s)[1]))],
        out_specs=[pl.BlockSpec((None, tk, tn), lambda s, kj, mi: (s // nper, kj, s % nper))],
        out_shape=[jax.ShapeDtypeStruct((P, K, Ns), BF16)], args=[a, g3],
        scratch_shapes=[pltpu.VMEM((tk, tn), F32)], comm=comm, carry_us=2.0 * M * K * P * Ns / MM_FLOPS_PER_US)[0]


def _vspec(d):
    return pl.BlockSpec((1, d), lambda i: (0, 0))


NORM_US_PER_ELEM = 12.0 / (4096 * 1024)


def norm_mod(x, g, sh, sc, name, comm=None):
    S, D = x.shape
    tm = _tile(S, 512, 16)

    def body(x_ref, g_ref, sh_ref, sc_ref, o_ref):
        xv = x_ref[...]
        r = lax.rsqrt(jnp.mean(xv * xv, axis=-1, keepdims=True) + EPS)
        o_ref[...] = ((xv * r) * g_ref[...] * (1.0 + sc_ref[...]) + sh_ref[...]).astype(o_ref.dtype)

    return _pcall(body, name=name, grid=(S // tm,),
                  in_specs=[pl.BlockSpec((tm, D), lambda i: (i, 0)), _vspec(D), _vspec(D), _vspec(D)],
                  out_specs=[pl.BlockSpec((tm, D), lambda i: (i, 0))], out_shape=[jax.ShapeDtypeStruct((S, D), BF16)],
                  args=[x, g, sh, sc], comm=comm, carry_us=NORM_US_PER_ELEM * S * D)[0]


def _gate_outputs(dx, gate_ref, out_ref, dout_ref):
    dout_ref[...] = (gate_ref[...] * dx).astype(dout_ref.dtype)
    return jnp.sum(dx * out_ref[...].astype(F32), axis=0, keepdims=True)


NORM_BWD_US_PER_ELEM = 28.0 / (4096 * 1024)


def norm_mod_bwd(dh, x, dres, g, sc, name, below=None, comm=None):
    S, D = x.shape
    tm = _tile(S, 256, 16)

    def body(dh_ref, x_ref, dr_ref, g_ref, sc_ref, *rest):
        dx_ref, sums_ref = (rest[2], rest[3]) if below is not None else (rest[0], rest[1])
        xv = x_ref[...]
        dhv = dh_ref[...].astype(F32)
        r = lax.rsqrt(jnp.mean(xv * xv, axis=-1, keepdims=True) + EPS)
        xn = xv * r
        one_sc = 1.0 + sc_ref[...]
        dxn = dhv * g_ref[...] * one_sc
        dx = r * (dxn - xn * jnp.mean(dxn * xn, axis=-1, keepdims=True)) + dr_ref[...]
        dx_ref[...] = dx
        rows = [jnp.sum(dhv, axis=0, keepdims=True), jnp.sum(dhv * xn * g_ref[...], axis=0, keepdims=True),
                jnp.sum(dhv * one_sc * xn, axis=0, keepdims=True)]
        if below is not None:
            rows.append(_gate_outputs(dx, rest[0], rest[1], rest[4]))
        part = jnp.concatenate(rows + [jnp.zeros((8 - len(rows), D), F32)], axis=0)

        @pl.when(pl.program_id(0) == 0)
        def _():
            sums_ref[...] = jnp.zeros_like(sums_ref)
        sums_ref[...] += part

    row = pl.BlockSpec((tm, D), lambda i: (i, 0))
    in_specs, args = [row, row, row, _vspec(D), _vspec(D)], [dh, x, dres, g, sc]
    out_specs = [row, pl.BlockSpec((8, D), lambda i: (0, 0))]
    out_shape = [jax.ShapeDtypeStruct((S, D), F32), jax.ShapeDtypeStruct((8, D), F32)]
    if below is not None:
        in_specs += [_vspec(D), row]
        args += [below[0], below[1]]
        out_specs.append(row)
        out_shape.append(jax.ShapeDtypeStruct((S, D), BF16))
    return _pcall(body, name=name, grid=(S // tm,), in_specs=in_specs, out_specs=out_specs, out_shape=out_shape, args=args,
                  comm=comm, carry_us=NORM_BWD_US_PER_ELEM * S * D)


def loss_fwd_bwd(x, g, target, below, name):
    S, D = x.shape
    tm = _tile(S, 256, 16)

    def body(x_ref, g_ref, t_ref, gate_ref, out_ref, dx_ref, sums_ref, dout_ref):
        xv = x_ref[...]
        r = lax.rsqrt(jnp.mean(xv * xv, axis=-1, keepdims=True) + EPS)
        xn = xv * r
        err = xn * g_ref[...] - t_ref[...]
        dy = err * (1.0 / D)
        dxn = dy * g_ref[...]
        dx = r * (dxn - xn * jnp.mean(dxn * xn, axis=-1, keepdims=True))
        dx_ref[...] = dx
        part = jnp.concatenate([jnp.sum(dy * xn, axis=0, keepdims=True), jnp.sum(err * err, axis=0, keepdims=True),
                                _gate_outputs(dx, gate_ref, out_ref, dout_ref), jnp.zeros((5, D), F32)], axis=0)

        @pl.when(pl.program_id(0) == 0)
        def _():
            sums_ref[...] = jnp.zeros_like(sums_ref)
        sums_ref[...] += part

    row = pl.BlockSpec((tm, D), lambda i: (i, 0))
    return pl.pallas_call(body, name=name, grid=(S // tm,), in_specs=[row, _vspec(D), row, _vspec(D), row],
                          out_specs=[row, pl.BlockSpec((8, D), lambda i: (0, 0)), row],
                          out_shape=[jax.ShapeDtypeStruct((S, D), F32), jax.ShapeDtypeStruct((8, D), F32), jax.ShapeDtypeStruct((S, D), BF16)],
                          compiler_params=_cparams(("arbitrary",)))(x, g, target, below[0], below[1])


def pool_fwd(u, wgrp, scale, name):
    S, D = u.shape
    G = len(POOL_WINDOWS)
    C = D // G
    tm = _tile(S, 256, 16)
    hb = tm // HALO

    def body(up_ref, uc_ref, w_ref, sc_ref, p_ref, z_ref, y_ref):
        i = pl.program_id(0)
        prev = jnp.where(i > 0, up_ref[...], 0.0)
        ext = jnp.concatenate([prev, uc_ref[...]], axis=0)
        t = i * tm + lax.broadcasted_iota(jnp.int32, (tm, 1), 0)
        for gi, w in enumerate(POOL_WINDOWS):
            cs = slice(gi * C, (gi + 1) * C)
            e = ext[:, cs]
            s, k = e, 1
            while k < w:
                s = s + pltpu.roll(s, k, 0)
                k *= 2
            cnt = jnp.minimum(t + 1, w).astype(F32)
            pooled = (s[HALO:] / cnt - e[HALO:]).astype(BF16)
            p_ref[:, cs] = pooled
            z = jnp.dot(pooled, w_ref[:, gi].reshape(C, C), preferred_element_type=F32)
            z_ref[:, cs] = z.astype(BF16)
            y_ref[:, cs] = (z * sc_ref[:, cs]).astype(BF16)

    row = pl.BlockSpec((tm, D), lambda i: (i, 0))
    return pl.pallas_call(
        body, name=name, grid=(S // tm,),
        in_specs=[pl.BlockSpec((HALO, D), lambda i: (jnp.maximum(i * hb - 1, 0), 0)), row,
                  pl.BlockSpec(wgrp.shape, lambda i: (0, 0, 0, 0)), _vspec(D)],
        out_specs=[row, row, row], out_shape=[jax.ShapeDtypeStruct((S, D), BF16)] * 3,
        compiler_params=_cparams(("parallel",)))(u, u, wgrp, scale)


def pool_bwd(dys, z, pooled, wgrp, scale, name):
    S, D = dys.shape
    G = len(POOL_WINDOWS)
    C = D // G
    tm = _tile(S, 256, 16)
    hb = tm // HALO
    nt = S // tm
    n_ext = tm + HALO

    def body(dc_ref, dn_ref, z_ref, p_ref, w_ref, sc_ref, du_ref, dw_ref, sums_ref):
        i = pl.program_id(0)

        @pl.when(i == 0)
        def _():
            dw_ref[...] = jnp.zeros_like(dw_ref)
            sums_ref[...] = jnp.zeros_like(sums_ref)
        dyc = dc_ref[...].astype(F32)
        nxt = jnp.where(i < nt - 1, dn_ref[...].astype(F32), 0.0)
        ext = jnp.concatenate([dyc, nxt], axis=0)
        sums_ref[...] += jnp.concatenate([jnp.sum(dyc * z_ref[...].astype(F32), axis=0, keepdims=True), jnp.zeros((7, D), F32)], axis=0)
        t = i * tm + lax.broadcasted_iota(jnp.int32, (n_ext, 1), 0)
        for gi, w in enumerate(POOL_WINDOWS):
            cs = slice(gi * C, (gi + 1) * C)
            wg = w_ref[:, gi].reshape(C, C)
            dz = (ext[:, cs] * sc_ref[:, cs]).astype(BF16)
            dpool = lax.dot_general(dz, wg, (((1,), (1,)), ((), ())), preferred_element_type=F32)
            dw_ref[gi] += lax.dot_general(p_ref[:, cs], dz[:tm], (((0,), (0,)), ((), ())), preferred_element_type=F32)
            cnt = jnp.minimum(t + 1, w).astype(F32)
            s, k = dpool / cnt, 1
            while k < w:
                s = s + pltpu.roll(s, n_ext - k, 0)
                k *= 2
            du_ref[:, cs] = (s[:tm] - dpool[:tm]).astype(BF16)

    row = pl.BlockSpec((tm, D), lambda i: (i, 0))
    return pl.pallas_call(
        body, name=name, grid=(nt,),
        in_specs=[row, pl.BlockSpec((HALO, D), lambda i: (jnp.minimum((i + 1) * hb, S // HALO - 1), 0)), row, row,
                  pl.BlockSpec(wgrp.shape, lambda i: (0, 0, 0, 0)), _vspec(D)],
        out_specs=[row, pl.BlockSpec((G, C, C), lambda i: (0, 0, 0)), pl.BlockSpec((8, D), lambda i: (0, 0))],
        out_shape=[jax.ShapeDtypeStruct((S, D), BF16), jax.ShapeDtypeStruct((G, C, C), F32), jax.ShapeDtypeStruct((8, D), F32)],
        compiler_params=_cparams(("arbitrary",)))(dys, dys, z, pooled, wgrp, scale)


FFN_ACT_BWD_US_PER_ELEM = 84.0 / (4096 * 2816)


def ffn_up_act(h, w3, conv_w, conv_b, name, comm=None):
    S, D = h.shape
    P, _, Ns = w3.shape
    nh = P // 2
    tm = _tile(S, 512, 16)

    def body(h_ref, w_ref, cw_ref, cb_ref, hu_ref, g_ref, stash, halo):
        i, j = pl.program_id(0), pl.program_id(1)
        acc = jnp.dot(h_ref[...], w_ref[...], preferred_element_type=F32).astype(BF16)
        hu_ref[...] = acc

        @pl.when(j < nh)
        def _():
            stash[j] = acc.astype(F32)

        @pl.when(j >= nh)
        def _():
            c = j - nh
            a = stash[c]
            ext = jnp.concatenate([jnp.where(i > 0, halo[c], 0.0), a], axis=0)
            conv = cb_ref[...] + pltpu.roll(ext, 2, 0) * cw_ref[0:1, :] + pltpu.roll(ext, 1, 0) * cw_ref[1:2, :] + ext * cw_ref[2:3, :]
            conv = conv[HALO:]
            g_ref[...] = (conv * _sigmoid(conv) * acc.astype(F32)).astype(g_ref.dtype)
            halo[c] = a[tm - HALO:]

    def gcol(j):
        return jnp.maximum(j - nh, 0)

    return _pcall(
        body, name=name, grid=(S // tm, P),
        in_specs=[pl.BlockSpec((tm, D), lambda i, j: (i, 0)), pl.BlockSpec((None, D, Ns), lambda i, j: (j, 0, 0)),
                  pl.BlockSpec((3, Ns), lambda i, j: (0, gcol(j))), pl.BlockSpec((1, Ns), lambda i, j: (0, gcol(j)))],
        out_specs=[pl.BlockSpec((tm, Ns), lambda i, j: (i, j)), pl.BlockSpec((tm, Ns), lambda i, j: (i, gcol(j)))],
        out_shape=[jax.ShapeDtypeStruct((S, P * Ns), BF16), jax.ShapeDtypeStruct((S, nh * Ns), BF16)],
        scratch_shapes=[pltpu.VMEM((nh, tm, Ns), F32), pltpu.VMEM((nh, HALO, Ns), F32)],
        args=[h, w3, conv_w, conv_b], comm=comm, carry_us=2.0 * S * D * P * Ns / MM_FLOPS_PER_US)


def ffn_act_bwd(dout, w_down, hu, conv_w, conv_b, name, comm=None):
    S, D = dout.shape
    F = w_down.shape[1]
    tm = _tile(S, 256, 16)
    tn = _tile(F, 1408, LANES)
    nb = F // tn
    hb = tm // HALO
    nt = S // tm
    n_ext = tm + 2 * HALO
    nt_dims = (((1,), (1,)), ((), ()))

    def body(dc_ref, dn_ref, wd_ref, ap_ref, ac_ref, an_ref, vc_ref, vn_ref, w_ref, b_ref, o_ref, sums_ref):
        i = pl.program_id(1)

        @pl.when(i == 0)
        def _():
            sums_ref[...] = jnp.zeros_like(sums_ref)
        zeros = jnp.zeros((HALO, tn), F32)
        not_last = i < nt - 1
        a_ext = jnp.concatenate([jnp.where(i > 0, ap_ref[...].astype(F32), 0.0), ac_ref[...].astype(F32), an_ref[...].astype(F32)], axis=0)
        v_ext = jnp.concatenate([zeros, vc_ref[...].astype(F32), vn_ref[...].astype(F32)], axis=0)
        g_cur = lax.dot_general(dc_ref[...], wd_ref[...], nt_dims, preferred_element_type=F32)
        g_nxt = lax.dot_general(dn_ref[...], wd_ref[...], nt_dims, preferred_element_type=F32)
        g_ext = jnp.concatenate([zeros, g_cur, jnp.where(not_last, g_nxt, 0.0)], axis=0)
        w0, w1, w2 = w_ref[0:1, :], w_ref[1:2, :], w_ref[2:3, :]
        a_m2, a_m1 = pltpu.roll(a_ext, 2, 0), pltpu.roll(a_ext, 1, 0)
        conv = b_ref[...] + a_m2 * w0 + a_m1 * w1 + a_ext * w2
        sig = _sigmoid(conv)
        silu = conv * sig
        dsilu = sig * (1.0 + conv * (1.0 - sig))
        dconv = g_ext * v_ext * dsilu
        da = dconv * w2 + pltpu.roll(dconv, n_ext - 1, 0) * w1 + pltpu.roll(dconv, n_ext - 2, 0) * w0
        cur = slice(HALO, HALO + tm)
        o_ref[0] = da[cur].astype(o_ref.dtype)
        o_ref[1] = (g_ext * silu)[cur].astype(o_ref.dtype)
        dc = dconv[cur]
        part = jnp.concatenate([jnp.sum(dc * a_m2[cur], axis=0, keepdims=True), jnp.sum(dc * a_m1[cur], axis=0, keepdims=True),
                                jnp.sum(dc * a_ext[cur], axis=0, keepdims=True), jnp.sum(dc, axis=0, keepdims=True),
                                jnp.zeros((4, tn), F32)], axis=0)
        sums_ref[...] += part

    def prev(i):
        return jnp.maximum(i * hb - 1, 0)

    def nxt(i):
        return jnp.minimum((i + 1) * hb, S // HALO - 1)

    return _pcall(
        body, name=name, grid=(nb, nt),
        in_specs=[pl.BlockSpec((tm, D), lambda j, i: (i, 0)), pl.BlockSpec((HALO, D), lambda j, i: (nxt(i), 0)),
                  pl.BlockSpec((None, tn, D), lambda j, i: (0, j, 0)),
                  pl.BlockSpec((HALO, tn), lambda j, i: (prev(i), j)), pl.BlockSpec((tm, tn), lambda j, i: (i, j)),
                  pl.BlockSpec((HALO, tn), lambda j, i: (nxt(i), j)),
                  pl.BlockSpec((tm, tn), lambda j, i: (i, j + nb)), pl.BlockSpec((HALO, tn), lambda j, i: (nxt(i), j + nb)),
                  pl.BlockSpec((3, tn), lambda j, i: (0, j)), pl.BlockSpec((1, tn), lambda j, i: (0, j))],
        out_specs=[pl.BlockSpec((2, tm, tn), lambda j, i: (0, i, j)), pl.BlockSpec((8, tn), lambda j, i: (0, j))],
        out_shape=[jax.ShapeDtypeStruct((2, S, F), BF16), jax.ShapeDtypeStruct((8, F), F32)],
        args=[dout, dout, w_down, hu, hu, hu, hu, hu, conv_w, conv_b], comm=comm, carry_us=FFN_ACT_BWD_US_PER_ELEM * S * F)


def _head_expander(n_heads, da):
    e = np.zeros((LANES, da), np.float32)
    for h in range(n_heads):
        e[h, h * HEAD_DIM:(h + 1) * HEAD_DIM] = 1.0
    return jnp.asarray(e, BF16)


def _split_dot(v, e, dims):
    hi = v.astype(BF16)
    lo = (v - hi.astype(F32)).astype(BF16)
    return (lax.dot_general(hi, e, dims, preferred_element_type=F32) + lax.dot_general(lo, e, dims, preferred_element_type=F32))


def _lane_col(tile, h):
    lane = lax.broadcasted_iota(jnp.int32, tile.shape, 1)
    return jnp.sum(jnp.where(lane == h, tile, 0.0), axis=1, keepdims=True)


ATTN_US_PER_ELEM = (80.0 / (4096 * 1024), 230.0 / (4096 * 1024))


def attn_branch_fwd(q, kv, gi, slopes, name, comm=None):
    S, DA = q.shape
    H = DA // HEAD_DIM
    window, d = BRANCHES[gi]
    n_steps = window // d
    blk = ATTN_BLOCK
    assert n_steps == blk and (S // d) % blk == 0
    nbs = S // d // blk
    scale = HEAD_DIM ** -0.5

    def body(q_ref, kp_ref, kc_ref, vp_ref, vc_ref, o_ref, l_ref, s_scr, p_scr):
        jb = pl.program_id(1)
        row = lax.broadcasted_iota(jnp.int32, (blk, 2 * blk), 0)
        col = lax.broadcasted_iota(jnp.int32, (blk, 2 * blk), 1)
        delta = row + blk - col
        valid = (delta >= 0) & (delta <= n_steps) & ((col >= blk) | (jb > 0))
        dist = jnp.where(valid, (delta * d).astype(F32), -NEG)
        lane = lax.broadcasted_iota(jnp.int32, (blk, LANES), 1)
        ltile = jnp.zeros((blk, LANES), F32)
        for h in range(H):
            hs = slice(h * HEAD_DIM, (h + 1) * HEAD_DIM)
            k2 = jnp.concatenate([kp_ref[:, hs], kc_ref[:, hs]], axis=0)
            s_scr[h] = lax.dot_general(q_ref[:, hs], k2, (((1,), (1,)), ((), ())), preferred_element_type=F32)
        for h in range(H):
            s = s_scr[h] * scale - float(slopes[h]) * dist
            m = jnp.max(s, axis=-1, keepdims=True)
            p = jnp.exp(s - m)
            l = jnp.sum(p, axis=-1, keepdims=True)
            p_scr[h] = (p / l).astype(BF16)
            ltile = jnp.where(lane == h, m + jnp.log(l), ltile)
        for h in range(H):
            hs = slice(h * HEAD_DIM, (h + 1) * HEAD_DIM)
            v2 = jnp.concatenate([vp_ref[:, hs], vc_ref[:, hs]], axis=0)
            o_ref[:, hs] = jnp.dot(p_scr[h], v2, preferred_element_type=F32)
        l_ref[...] = ltile

    def cur(width, off):
        return pl.BlockSpec((blk, width), lambda r, jb: (r * nbs + jb, off))

    def prv(width, off):
        return pl.BlockSpec((blk, width), lambda r, jb: (r * nbs + jnp.maximum(jb - 1, 0), off))

    return _pcall(
        body, name=name, grid=(d, nbs),
        in_specs=[cur(DA, 0), prv(DA, 0), cur(DA, 0), prv(DA, 1), cur(DA, 1)],
        out_specs=[cur(DA, 0), cur(LANES, 0)],
        out_shape=[jax.ShapeDtypeStruct((S, DA), F32), jax.ShapeDtypeStruct((S, LANES), F32)],
        scratch_shapes=[pltpu.VMEM((H, blk, 2 * blk), F32), pltpu.VMEM((H, blk, 2 * blk), BF16)],
        args=[q, kv, kv, kv, kv], comm=comm, carry_us=ATTN_US_PER_ELEM[0] * S * DA)


def attn_combine(os_, lses, name):
    S, DA = os_[0].shape
    H = DA // HEAD_DIM
    tm = _tile(S, 256, 16)
    expander = _head_expander(H, DA)
    nbr = len(os_)

    def body(*refs):
        o_refs, l_refs, e_ref = refs[:nbr], refs[nbr:2 * nbr], refs[2 * nbr]
        out_ref, lse_ref = refs[2 * nbr + 1:]
        ls = [r[...] for r in l_refs]
        lmax = functools.reduce(jnp.maximum, ls)
        es = [jnp.exp(l - lmax) for l in ls]
        den = functools.reduce(lambda a, b: a + b, es)
        lse_ref[...] = lmax + jnp.log(den)
        acc = jnp.zeros((tm, DA), F32)
        for e, o_ref in zip(es, o_refs):
            acc = acc + _split_dot(e / den, e_ref[...], (((1,), (0,)), ((), ()))) * o_ref[...]
        out_ref[...] = acc.astype(out_ref.dtype)

    row = pl.BlockSpec((tm, DA), lambda i: (i, 0))
    lrow = pl.BlockSpec((tm, LANES), lambda i: (i, 0))
    return pl.pallas_call(
        body, name=name, grid=(S // tm,),
        in_specs=[row] * nbr + [lrow] * nbr + [pl.BlockSpec((LANES, DA), lambda i: (0, 0))],
        out_specs=[row, lrow], out_shape=[jax.ShapeDtypeStruct((S, DA), BF16), jax.ShapeDtypeStruct((S, LANES), F32)],
        compiler_params=_cparams(("parallel",)))(*os_, *lses, expander)


def attn_delta(do, o, name):
    S, DA = o.shape
    H = DA // HEAD_DIM
    tm = _tile(S, 512, 16)
    expander = _head_expander(H, DA)

    def body(do_ref, o_ref, e_ref, d_ref):
        prod = do_ref[...].astype(F32) * o_ref[...].astype(F32)
        d_ref[...] = _split_dot(prod, e_ref[...], (((1,), (1,)), ((), ())))

    row = pl.BlockSpec((tm, DA), lambda i: (i, 0))
    return pl.pallas_call(body, name=name, grid=(S // tm,), in_specs=[row, row, pl.BlockSpec((LANES, DA), lambda i: (0, 0))],
                          out_specs=pl.BlockSpec((tm, LANES), lambda i: (i, 0)), out_shape=jax.ShapeDtypeStruct((S, LANES), F32),
                          compiler_params=_cparams(("parallel",)))(do, o, expander)


def attn_branch_bwd(q, kv, do, lse, dlt, gi, slopes, name, out_cols=None, comm=None):
    S, DA = q.shape
    H = DA // HEAD_DIM
    window, d = BRANCHES[gi]
    n_steps = window // d
    blk = ATTN_BLOCK
    nbs = S // d // blk
    scale = HEAD_DIM ** -0.5
    nt, tn = (((1,), (1,)), ((), ())), (((0,), (0,)), ((), ()))

    def body(*refs):
        k_ref, v_ref, qc_ref, qn_ref, doc_ref, don_ref, lc_ref, ln_ref, dc_ref, dn_ref = refs[:10]
        dq_ref, dkv_ref, carry, s_scr, dp_scr, p_scr, ds_scr = refs[-7:]
        kb = pl.program_id(1)

        @pl.when(kb == 0)
        def _():
            carry[...] = jnp.zeros_like(carry)
        row = lax.broadcasted_iota(jnp.int32, (2 * blk, blk), 0)
        col = lax.broadcasted_iota(jnp.int32, (2 * blk, blk), 1)
        delta = row - col
        valid = (delta >= 0) & (delta <= n_steps) & ((row < blk) | (kb < nbs - 1))
        dist = jnp.where(valid, (delta * d).astype(F32), -NEG)
        l2 = jnp.concatenate([lc_ref[...], ln_ref[...]], axis=0)
        d2 = jnp.concatenate([dc_ref[...], dn_ref[...]], axis=0)
        for h in range(H):
            hs = slice(h * HEAD_DIM, (h + 1) * HEAD_DIM)
            q2 = jnp.concatenate([qc_ref[:, hs], qn_ref[:, hs]], axis=0)
            do2 = jnp.concatenate([doc_ref[:, hs], don_ref[:, hs]], axis=0)
            s_scr[h] = lax.dot_general(q2, k_ref[:, hs], nt, preferred_element_type=F32)
            dp_scr[h] = lax.dot_general(do2, v_ref[:, hs], nt, preferred_element_type=F32)
        for h in range(H):
            p = jnp.exp(s_scr[h] * scale - float(slopes[h]) * dist - _lane_col(l2, h))
            p_scr[h] = p.astype(BF16)
            ds_scr[h] = (p * (dp_scr[h] - _lane_col(d2, h))).astype(BF16)
        for h in range(H):
            hs = slice(h * HEAD_DIM, (h + 1) * HEAD_DIM)
            vs = slice(DA + h * HEAD_DIM, DA + (h + 1) * HEAD_DIM)
            q2 = jnp.concatenate([qc_ref[:, hs], qn_ref[:, hs]], axis=0)
            do2 = jnp.concatenate([doc_ref[:, hs], don_ref[:, hs]], axis=0)
            dvh = lax.dot_general(p_scr[h], do2, tn, preferred_element_type=F32)
            dkh = lax.dot_general(ds_scr[h], q2, tn, preferred_element_type=F32) * scale
            dq2 = jnp.dot(ds_scr[h], k_ref[:, hs], preferred_element_type=F32) * scale
            dq_ref[:, hs] = (carry[:, hs] + dq2[:blk]).astype(dq_ref.dtype)
            carry[:, hs] = dq2[blk:]
            dkv_ref[:, hs] = dkh
            dkv_ref[:, vs] = dvh

    def cur(width, off):
        return pl.BlockSpec((blk, width), lambda r, kb: (r * nbs + kb, off))

    def nxt(width, off):
        return pl.BlockSpec((blk, width), lambda r, kb: (r * nbs + jnp.minimum(kb + 1, nbs - 1), off))

    in_specs = [cur(DA, 0), cur(DA, 1), cur(DA, 0), nxt(DA, 0), cur(DA, 0), nxt(DA, 0),
                cur(LANES, 0), nxt(LANES, 0), cur(LANES, 0), nxt(LANES, 0)]
    args = [kv, kv, q, q, do, do, lse, lse, dlt, dlt]
    return _pcall(
        body, name=name, grid=(d, nbs), in_specs=in_specs, out_specs=[cur(DA, 0), cur(2 * DA, 0)],
        out_shape=[jax.ShapeDtypeStruct((S, out_cols or DA), BF16), jax.ShapeDtypeStruct((S, 2 * DA), F32)],
        scratch_shapes=[pltpu.VMEM((blk, DA), F32), pltpu.VMEM((H, 2 * blk, blk), F32), pltpu.VMEM((H, 2 * blk, blk), F32),
                        pltpu.VMEM((H, 2 * blk, blk), BF16), pltpu.VMEM((H, 2 * blk, blk), BF16)],
        args=args, comm=comm, carry_us=ATTN_US_PER_ELEM[1] * S * DA)


def ada_project(c16, w3, b3, name):
    L, D, Ns = w3.shape
    tn = _tile(Ns, 512, LANES)

    def body(c_ref, w_ref, b_ref, o_ref):
        cv = c_ref[...]
        cond = (cv * _sigmoid(cv)).astype(BF16)
        o_ref[...] = jnp.dot(cond, w_ref[...].astype(BF16), preferred_element_type=F32) + b_ref[...]

    return pl.pallas_call(
        body, name=name, grid=(L, Ns // tn),
        in_specs=[pl.BlockSpec((16, D), lambda l, j: (0, 0)), pl.BlockSpec((None, D, tn), lambda l, j: (l, 0, j)),
                  pl.BlockSpec((None, 1, tn), lambda l, j: (l, 0, j))],
        out_specs=pl.BlockSpec((None, 16, tn), lambda l, j: (l, 0, j)), out_shape=jax.ShapeDtypeStruct((L, 16, Ns), F32),
        compiler_params=_cparams(("parallel", "parallel")))(c16, w3, b3)


def _adamw(w, g, m, v):
    m = B1 * m + (1.0 - B1) * g
    v = B2 * v + (1.0 - B2) * (g * g)
    m_hat = m / (1.0 - B1 ** STEP)
    v_hat = v / (1.0 - B2 ** STEP)
    delta = -LR * (m_hat / (jnp.sqrt(v_hat) + ADAM_EPS) + WD * w)
    return delta, m, v


def ada_grad_adamw(c16, d3, w3, m3, v3, name):
    L, D, Ns = w3.shape
    tk = _tile(D, 256, 8)

    def body(c_ref, d_ref, w_ref, m_ref, v_ref, g_out, dl_out, m_out, v_out):
        cv = c_ref[...]
        cond = (cv * _sigmoid(cv)).astype(BF16)
        g = lax.dot_general(cond, d_ref[...].astype(BF16), (((0,), (0,)), ((), ())), preferred_element_type=F32)
        g_out[...] = g
        dl_out[...], m_out[...], v_out[...] = _adamw(w_ref[...], g, m_ref[...], v_ref[...])

    wspec = pl.BlockSpec((None, tk, Ns), lambda l, kj: (l, kj, 0))
    return pl.pallas_call(
        body, name=name, grid=(L, D // tk),
        in_specs=[pl.BlockSpec((16, tk), lambda l, kj: (0, kj)), pl.BlockSpec((None, 16, Ns), lambda l, kj: (l, 0, 0)), wspec, wspec, wspec],
        out_specs=[wspec] * 4, out_shape=[jax.ShapeDtypeStruct((L, D, Ns), F32)] * 4,
        compiler_params=_cparams(("parallel", "parallel")))(c16, d3, w3, m3, v3)


def adamw(w, g, m, v, name):
    R, C = w.shape
    tr = _tile(R, 256, 8)

    def body(w_ref, g_ref, m_ref, v_ref, g_out, dl_out, m_out, v_out):
        g = g_ref[...]
        g_out[...] = g
        dl_out[...], m_out[...], v_out[...] = _adamw(w_ref[...], g, m_ref[...], v_ref[...])

    spec = pl.BlockSpec((tr, C), lambda i: (i, 0))
    return pl.pallas_call(body, name=name, grid=(R // tr,), in_specs=[spec] * 4, out_specs=[spec] * 4,
                          out_shape=[jax.ShapeDtypeStruct((R, C), F32)] * 4, compiler_params=_cparams(("parallel",)))(w, g, m, v)


def sum_partials(own, recv, g_prev, layer, n_layers, pos, name):
    _, Rh, C = recv.shape
    tr = _tile(Rh, 256, 16)

    def body(pos_ref, own_ref, recv_ref, *rest):
        acc = own_ref[...].astype(F32)
        for rel in range(7):
            acc = acc + recv_ref[rel].astype(F32)
        rest[-1][...] = acc

    in_specs = [pl.BlockSpec((None, None, tr, C), lambda r, pos: (pos[1], pos[0], r, 0)), pl.BlockSpec((7, tr, C), lambda r, pos: (0, r, 0))]
    args = [pos, own.reshape(N_CHIPS, 2, Rh, C), recv]
    aliases = {}
    if g_prev is not None:
        in_specs.append(pl.BlockSpec(memory_space=pl.ANY))
        args.append(g_prev)
        aliases = {3: 0}
    return pl.pallas_call(
        body, name=name,
        grid_spec=pltpu.PrefetchScalarGridSpec(
            num_scalar_prefetch=1, grid=(Rh // tr,), in_specs=in_specs,
            out_specs=pl.BlockSpec((None, None, tr, C), lambda r, pos: (layer, pos[0], r, 0))),
        out_shape=jax.ShapeDtypeStruct((n_layers, 2, Rh, C), F32), input_output_aliases=aliases,
        compiler_params=_cparams(("parallel",)))(*args)


def sum_rows8(g8, name):
    _, R, C = g8.shape

    def body(g_ref, o_ref):
        acc = g_ref[0]
        for i in range(1, N_DEV):
            acc = acc + g_ref[i]
        o_ref[...] = acc

    return pl.pallas_call(body, name=name, grid=(1,), in_specs=[pl.BlockSpec((N_DEV, R, C), lambda i: (0, 0, 0))],
                          out_specs=pl.BlockSpec((R, C), lambda i: (0, 0)), out_shape=jax.ShapeDtypeStruct((R, C), F32),
                          compiler_params=_cparams(("arbitrary",)))(g8)


def _pack(vecs):
    flat = [v.reshape(-1).astype(F32) for v in vecs]
    sizes = [f.shape[0] for f in flat]
    total = sum(sizes)
    padded = -(-total // (8 * PACK_W)) * (8 * PACK_W)
    buf = jnp.concatenate(flat + [jnp.zeros((padded - total,), F32)])
    offs = np.concatenate([[0], np.cumsum(sizes)])
    return buf.reshape(-1, PACK_W), offs


def _unpack(buf, offs, shapes):
    flat = buf.reshape(-1)
    return [flat[int(offs[i]):int(offs[i + 1])].reshape(s) for i, s in enumerate(shapes)]


def kernel(x, c, ada_w, ada_b, norm1_g, norm2_g, pool_w_in, pool_w_grp, pool_scale, pool_w_out, kv_norm_g, kv_ada_w, kv_ada_b, w_kv, attn_w_q, attn_w_o, ffn_w_up, ffn_conv_w, ffn_conv_b, ffn_w_down, final_g, loss_target, m_ada_w, m_ada_b, m_norm1_g, m_norm2_g, m_pool_w_in, m_pool_w_grp, m_pool_scale, m_pool_w_out, m_kv_norm_g, m_kv_ada_w, m_kv_ada_b, m_w_kv, m_attn_w_q, m_attn_w_o, m_ffn_w_up, m_ffn_conv_w, m_ffn_conv_b, m_ffn_w_down, m_final_g, v_ada_w, v_ada_b, v_norm1_g, v_norm2_g, v_pool_w_in, v_pool_w_grp, v_pool_scale, v_pool_w_out, v_kv_norm_g, v_kv_ada_w, v_kv_ada_b, v_w_kv, v_attn_w_q, v_attn_w_o, v_ffn_w_up, v_ffn_conv_w, v_ffn_conv_b, v_ffn_w_down, v_final_g):
    S, D = x.shape[1], x.shape[2]
    depth = ada_w.shape[0]
    n_pool = pool_w_in.shape[0]
    n_attn = attn_w_q.shape[0]
    G = len(POOL_WINDOWS)
    NB = len(BRANCHES)
    DA = attn_w_o.shape[1] * N_CHIPS
    H = DA // HEAD_DIM
    F = ffn_conv_b.shape[1]
    Fs = F // N_CHIPS
    Dq = D // N_CHIPS
    ada_ns = ada_w.shape[2]
    kvada_ns = kv_ada_w.shape[1]
    slopes = _alibi_slopes(NB * H).reshape(NB, H)

    ix, iy, ic = lax.axis_index("x"), lax.axis_index("y"), lax.axis_index("c")
    p_me = 2 * ix + iy
    b_me = 4 * ix + 2 * iy + ic
    pos = jnp.stack([ic, p_me]).astype(jnp.int32)
    xs, tgt = x[0], loss_target[0]

    pk, offs = _pack([c, pool_scale, ffn_conv_w])
    rows1 = pk.shape[0]
    got = all_gather8(pk, "gather_small_in").reshape(N_DEV, rows1, PACK_W)
    c8 = got.reshape(N_DEV, -1)[:, :D]
    c16 = jnp.concatenate([c8, jnp.zeros_like(c8)], axis=0)
    chip_rows = got[0::2].reshape(N_CHIPS, -1)
    scale_full = chip_rows[:, int(offs[1]):int(offs[2])].reshape(N_CHIPS, n_pool, Dq).transpose(1, 0, 2).reshape(n_pool, D)
    convw_full = chip_rows[:, int(offs[2]):int(offs[3])].reshape(N_CHIPS, depth, 3, Fs).transpose(1, 2, 0, 3).reshape(depth, 3, F)

    ada_b_loc = lax.dynamic_slice(ada_b, (0, p_me * ada_ns), (depth, ada_ns)).reshape(depth, 1, ada_ns)
    kvb_loc = lax.dynamic_slice(kv_ada_b, (p_me * kvada_ns,), (kvada_ns,)).reshape(1, 1, kvada_ns)
    mods_loc = ada_project(c16, ada_w, ada_b_loc, "ada_project")[:, :N_DEV]
    kvmod_loc = ada_project(c16, kv_ada_w.reshape(1, D, kvada_ns), kvb_loc, "kv_ada_project")[0, :N_DEV]
    mods_cat = jnp.concatenate([mods_loc.transpose(1, 0, 2).reshape(N_DEV, depth * ada_ns), kvmod_loc], axis=1)
    mods_all = all_gather8(mods_cat, "gather_mods").reshape(N_CHIPS, 2, N_DEV, -1)
    mine = lax.dynamic_index_in_dim(mods_all[:, 0], b_me, axis=1, keepdims=False)
    mod = mine[:, :depth * ada_ns].reshape(N_CHIPS, depth, ada_ns).transpose(1, 0, 2).reshape(depth, 6, 1, D)
    kvmod = mine[:, depth * ada_ns:].reshape(2, 1, D)

    comm = _Comm()
    C = D // G
    kv_ns, q_ns, up_ns = w_kv.shape[1], attn_w_q.shape[2], ffn_w_up.shape[2]

    def layer_shards(l):
        sh = []
        if l < n_pool:
            sh += [(("pin", l), pool_w_in[l]), (("pgrp", l), pool_w_grp[l].reshape(-1, C)), (("pout", l), pool_w_out[l])]
        else:
            if l == n_pool:
                sh.append((("kv", 0), w_kv))
            sh += [(("wq", l), attn_w_q[l - n_pool]), (("wo", l), attn_w_o[l - n_pool])]
        sh += [(("up", l), ffn_w_up[l]), (("down", l), ffn_w_down[l])]
        return [(k, w.astype(BF16)) for k, w in sh]

    def weight(key, shape):
        return comm.require(key).reshape(shape)

    dil = [d for _, d in BRANCHES]
    kv_tn = DA // 2
    q_tn = DA // 4
    q_bwd_tn = q_ns
    up_tn = up_ns
    up_per_half = F // up_tn

    def up_gmap(s):
        return s // up_per_half, s % up_per_half

    def vec(v):
        return v.reshape(1, -1)

    saved = []
    xcur = xs
    kvs = None
    wts = {}
    push_gather(comm, layer_shards(0))
    comm.flush()
    for l in range(depth):
        if l + 1 < depth:
            push_gather(comm, layer_shards(l + 1))
        sh1, sc1, g1, sh2, sc2, g2 = [mod[l, i] for i in range(6)]
        st = {"x0": xcur}
        h1 = norm_mod(xcur, vec(norm1_g[l]), sh1, sc1, "norm_mod", comm=comm)
        st["h1"] = h1
        if l < n_pool:
            wts["pin", l] = weight(("pin", l), (1, D, D))
            u = mm_nn(h1, wts["pin", l], tn=D // 2, out_dtype=F32, name="pool_in_proj", comm=comm)
            wts["pgrp", l] = weight(("pgrp", l), (N_CHIPS, G, C // N_CHIPS, C))
            pooled, z, ys = pool_fwd(u, wts["pgrp", l], vec(scale_full[l]), "pool_mix")
            wts["pout", l] = weight(("pout", l), (1, D, D))
            out, x1 = mm_nn(ys, wts["pout", l], tn=D // 2, out_dtype=BF16, name="pool_out_proj", res=(xcur, g1), comm=comm)
            st.update(pooled=pooled, z=z, ys=ys, out1=out)
        else:
            if l == n_pool:
                wts["kv", 0] = weight(("kv", 0), (N_CHIPS, D, kv_ns))
                hkv = norm_mod(xcur, vec(kv_norm_g), kvmod[0], kvmod[1], "norm_mod", comm=comm)
                kvs = [mm_nn(hkv, wts["kv", 0], tn=kv_tn, out_dtype=BF16, name=f"kv_proj_b{gi}", ncb=4, perm_d=dil[gi], comm=comm,
                             cbmap=functools.partial(lambda jj, gi: 2 * gi + (jj // 2) * 2 * NB + jj % 2, gi=gi)) for gi in range(NB)]
                kv_state = {"x": xcur, "hkv": hkv}
            wts["wq", l] = weight(("wq", l), (N_CHIPS, D, q_ns))
            qs, os_, lses = [], [], []
            for gi in range(NB):
                q_b = mm_nn(h1, wts["wq", l], tn=q_tn, out_dtype=BF16, name=f"q_proj_b{gi}", ncb=4, perm_d=dil[gi], comm=comm,
                            cbmap=functools.partial(lambda jj, gi: 4 * gi + jj, gi=gi))
                o_b, l_b = attn_branch_fwd(q_b, kvs[gi], gi, slopes[gi], f"attn_fwd_b{gi}", comm=comm)
                if dil[gi] > 1:
                    o_b = unpermute_rows(o_b, dil[gi], f"unpermute_o_b{gi}")
                    l_b = unpermute_rows(l_b, dil[gi], f"unpermute_lse_b{gi}")
                qs.append(q_b)
                os_.append(o_b)
                lses.append(l_b)
            o, lse = attn_combine(os_, lses, "attn_combine")
            wts["wo", l] = weight(("wo", l), (1, DA, D))
            out, x1 = mm_nn(o, wts["wo", l], tn=D // 2, out_dtype=BF16, name="attn_out_proj", res=(xcur, g1), comm=comm)
            st.update(qs=qs, o=o, lse=lse, out1=out)
        st["x1"] = x1
        h2 = norm_mod(x1, vec(norm2_g[l]), sh2, sc2, "norm_mod", comm=comm)
        wts["up", l] = weight(("up", l), (N_CHIPS, D, up_ns))
        hu, gated = ffn_up_act(h2, wts["up", l], convw_full[l], vec(ffn_conv_b[l]), "ffn_up_act", comm=comm)
        wts["down", l] = weight(("down", l), (1, F, D))
        out2, x2 = mm_nn(gated, wts["down", l], tn=D // 2, out_dtype=BF16, name="ffn_down_proj", res=(x1, g2), comm=comm)
        st.update(h2=h2, hu=hu, gated=gated, out2=out2)
        saved.append(st)
        xcur = x2
    comm.flush()

    dx, fsums, dout2 = loss_fwd_bwd(xcur, vec(final_g), tgt, (mod[depth - 1, 5], saved[depth - 1]["out2"]), "loss_head")
    loss = lax.psum(0.5 * jnp.sum(fsums[1]) / D, ("x", "y", "c"))
    d_final_g, s_g2 = fsums[0], fsums[2]

    dmods = [None] * depth
    d_n1 = [None] * depth
    d_n2 = [None] * depth
    d_convw = [None] * depth
    d_convb = [None] * depth
    d_scale = [None] * n_pool
    d_grp = [None] * n_pool
    dkvs = [[] for _ in range(NB)]
    exchanged = []
    f_tk = _tile(F, 1408, LANES)
    ct_blocks = DA // _tile(DA, 256, LANES)

    def exchange(name, idx, dw):
        dw4 = dw.reshape(N_CHIPS, -1, dw.shape[-1])
        exchanged.append((name, idx, dw4))
        push_exchange(comm, (name, idx), dw4)

    for l in reversed(range(depth)):
        st = saved[l]
        sh1, sc1, g1, sh2, sc2, g2 = [mod[l, i] for i in range(6)]
        dout2_3 = dout2.reshape(1, S, D)
        exchange("down", l, mm_tn(st["gated"], dout2_3, (1, F, D), tn=D, tk=f_tk, name="ffn_down_dw", comm=comm))
        dhu, s_conv = ffn_act_bwd(dout2, wts["down", l], st["hu"], convw_full[l], vec(ffn_conv_b[l]), "ffn_act_bwd", comm=comm)
        dh2 = mm_nt(dhu, wts["up", l], tn=up_tn, tk=D, out_dtype=F32, name="ffn_up_bwd", gmap=up_gmap, comm=comm)
        exchange("up", l, mm_tn(st["h2"], dhu, (N_CHIPS, D, up_ns), tn=up_tn, tk=D, name="ffn_up_dw", gmap=up_gmap, comm=comm))
        dx, s_n2, dout1 = norm_mod_bwd(dh2, st["x1"], dx, vec(norm2_g[l]), sc2, "norm_mod_bwd_gate", below=(g1, st["out1"]), comm=comm)
        d_convw[l], d_convb[l] = s_conv[0:3], s_conv[3]
        d_n2[l] = s_n2[2]
        dout1_3 = dout1.reshape(1, S, D)
        if l < n_pool:
            dys = mm_nt(dout1_3, wts["pout", l], tn=D, tk=D // 2, out_dtype=F32, name="pool_out_bwd", comm=comm)
            exchange("pout", l, mm_tn(st["ys"], dout1_3, (1, D, D), tn=D, tk=D, name="pool_out_dw", comm=comm))
            du, d_grp, s_sc = pool_bwd(dys, st["z"], st["pooled"], wts["pgrp", l], vec(scale_full[l]), "pool_mix_bwd")
            exchange("pgrp", l, d_grp.astype(BF16).reshape(G, N_CHIPS, C // N_CHIPS, C).transpose(1, 0, 2, 3))
            d_scale[l] = s_sc[0]
            du_3 = du.reshape(1, S, D)
            dh1 = mm_nt(du_3, wts["pin", l], tn=D, tk=D // 2, out_dtype=F32, name="pool_in_bwd", comm=comm)
            exchange("pin", l, mm_tn(st["h1"], du_3, (1, D, D), tn=D, tk=D, name="pool_in_dw", comm=comm))
        else:
            j = l - n_pool
            do = mm_nt(dout1_3, wts["wo", l], tn=D, tk=DA // 2, out_dtype=BF16, name="attn_out_bwd", comm=comm)
            exchange("wo", j, mm_tn(st["o"], dout1_3, (1, DA, D), tn=D, tk=DA, name="attn_out_dw", comm=comm))
            dlt = attn_delta(do, st["o"], "attn_delta")
            dq = None
            for gi in range(NB):
                d = dil[gi]
                do_b, l_b, dl_b = do, st["lse"], dlt
                if d > 1:
                    do_b = permute_rows(do, d, f"permute_do_b{gi}")
                    l_b = permute_rows(st["lse"], d, f"permute_lse_b{gi}")
                    dl_b = permute_rows(dlt, d, f"permute_delta_b{gi}")
                bwd_name = f"attn_bwd_b{gi}"
                if d > 1:
                    dq_b, dkv_b = attn_branch_bwd(st["qs"][gi], kvs[gi], do_b, l_b, dl_b, gi, slopes[gi], bwd_name, comm=comm)
                    dq = unpermute_rows(dq_b, d, f"unpermute_dq_b{gi}", into=dq, total_cols=NB * DA,
                                        colmap=functools.partial(lambda jj, gi: gi * ct_blocks + jj, gi=gi))
                else:
                    dq, dkv_b = attn_branch_bwd(st["qs"][gi], kvs[gi], do_b, l_b, dl_b, gi, slopes[gi], bwd_name,
                                                out_cols=NB * DA, comm=comm)
                dkvs[gi].append(dkv_b)
            dq_3 = dq.reshape(1, S, NB * DA)
            dh1 = mm_nt(dq_3, wts["wq", l], tn=q_bwd_tn, tk=D, out_dtype=F32, name="q_proj_bwd", comm=comm)
            exchange("wq", j, mm_tn(st["h1"], dq_3, (N_CHIPS, D, q_ns), tn=q_bwd_tn, tk=D, name="q_proj_dw", comm=comm))
        below = (mod[l - 1, 5], saved[l - 1]["out2"]) if l > 0 else None
        if l == n_pool or below is None:
            dx, s_n1 = norm_mod_bwd(dh1, st["x0"], dx, vec(norm1_g[l]), sc1, "norm_mod_bwd", comm=comm if l > 0 else None)
        else:
            dx, s_n1, dout2 = norm_mod_bwd(dh1, st["x0"], dx, vec(norm1_g[l]), sc1, "norm_mod_bwd_gate", below=below, comm=comm)
        d_n1[l] = s_n1[2]
        dmods[l] = jnp.stack([s_n1[0], s_n1[1], s_n2[3], s_n2[0], s_n2[1], s_g2])
        if l > 0 and l != n_pool:
            s_g2 = s_n1[3]
        if l == n_pool:
            dkv = None
            for gi in range(NB):
                dkv = unpermute_rows(dkvs[gi], dil[gi], f"unpermute_dkv_b{gi}", into=dkv, total_cols=2 * NB * DA,
                                     colmap=functools.partial(lambda jj, gi: (jj // ct_blocks) * NB * ct_blocks + gi * ct_blocks + jj % ct_blocks,
                                                              gi=gi))
            dkv_3 = dkv.reshape(1, S, 2 * NB * DA)
            dhkv = mm_nt(dkv_3, wts["kv", 0], tn=kv_ns // 2, tk=D, out_dtype=F32, name="kv_proj_bwd", comm=comm)
            exchange("kv", 0, mm_tn(kv_state["hkv"], dkv_3, (N_CHIPS, D, kv_ns), tn=kv_ns // 2, tk=D, name="kv_proj_dw", comm=comm))
            dx, s_kv, dout2 = norm_mod_bwd(dhkv, kv_state["x"], dx, vec(kv_norm_g), kvmod[1], "norm_mod_bwd_gate", below=below, comm=comm)
            s_g2 = s_kv[3]
    grad_x = dx.reshape(1, S, D)

    smalls = [jnp.stack(dmods), jnp.stack([s_kv[0], s_kv[1]]), jnp.stack(d_n1), jnp.stack(d_n2), s_kv[2], jnp.stack(d_convb), d_final_g,
              jnp.stack(d_scale), jnp.stack(d_convw)]
    small_shapes = [s.shape for s in smalls]
    spk, soffs = _pack(smalls)
    srows = spk.shape[0]
    sgot = all_gather8(spk, "gather_small_grads").reshape(N_DEV, srows, PACK_W)
    ssum = sum_rows8(sgot, "sum_small_grads")
    g_mods, g_kvmod, g_n1, g_n2, g_kvn, g_convb, g_fg, g_scale_full, g_convw_full = _unpack(ssum, soffs, small_shapes)
    g_ada_b = g_mods.reshape(depth, 6 * D)
    g_kv_ada_b = g_kvmod.reshape(2 * D)
    g_scale = lax.dynamic_slice(g_scale_full, (0, p_me * Dq), (n_pool, Dq))
    g_convw = lax.dynamic_slice(g_convw_full, (0, 0, p_me * Fs), (depth, 3, Fs))

    small_w = [ada_b, norm1_g, norm2_g, kv_norm_g, kv_ada_b, ffn_conv_b, final_g, pool_scale, ffn_conv_w]
    small_m = [m_ada_b, m_norm1_g, m_norm2_g, m_kv_norm_g, m_kv_ada_b, m_ffn_conv_b, m_final_g, m_pool_scale, m_ffn_conv_w]
    small_v = [v_ada_b, v_norm1_g, v_norm2_g, v_kv_norm_g, v_kv_ada_b, v_ffn_conv_b, v_final_g, v_pool_scale, v_ffn_conv_w]
    small_g = [g_ada_b, g_n1, g_n2, g_kvn, g_kv_ada_b, g_convb, g_fg, g_scale, g_convw]
    sw_shapes = [w.shape for w in small_w]
    pw, woffs = _pack(small_w)
    s_res = adamw(pw, _pack(small_g)[0], _pack(small_m)[0], _pack(small_v)[0], "adamw_small")
    s_g, s_dl, s_m, s_v = [_unpack(r, woffs, sw_shapes) for r in s_res]

    per_dev = sgot.reshape(N_DEV, -1)
    dm_all = per_dev[:, int(soffs[0]):int(soffs[1])].reshape(N_DEV, depth, 6 * D)
    dkvm_all = per_dev[:, int(soffs[1]):int(soffs[2])].reshape(N_DEV, 1, 2 * D)

    def shard_cols(a, ns):
        sl = lax.dynamic_slice_in_dim(a, p_me * ns, ns, axis=2).transpose(1, 0, 2)
        return jnp.concatenate([sl, jnp.zeros_like(sl)], axis=1)

    ada_res = ada_grad_adamw(c16, shard_cols(dm_all, ada_ns), ada_w, m_ada_w, v_ada_w, "ada_grad_adamw")
    kvada_res = ada_grad_adamw(c16, shard_cols(dkvm_all, kvada_ns), kv_ada_w.reshape(1, D, kvada_ns), m_kv_ada_w.reshape(1, D, kvada_ns),
                               v_kv_ada_w.reshape(1, D, kvada_ns), "kv_ada_grad_adamw")
    kvada_res = [r.reshape(D, kvada_ns) for r in kvada_res]

    comm.flush()
    big_names = ["pin", "pgrp", "pout", "kv", "wq", "wo", "up", "down"]
    n_stack = {"pin": n_pool, "pgrp": n_pool, "pout": n_pool, "kv": 1, "wq": n_attn, "wo": n_attn, "up": depth, "down": depth}
    gsum = {nm: None for nm in big_names}
    for nm, idx, dw4 in exchanged:
        gsum[nm] = sum_partials(dw4, comm.store[nm, idx], gsum[nm], idx, n_stack[nm], pos, "sum_partials")
    for nm in big_names:
        g = gsum[nm]

        def done(outs, nm=nm):
            gsum[nm] = outs[0]

        comm.push(_Phase(("swap", nm), nm, None, 0.0, lambda: [], lambda g=g: [g], [], g.shape[0], 0, _build_swap, done))
    comm.flush()
    big_m = [m_pool_w_in, m_pool_w_grp, m_pool_w_out, m_w_kv, m_attn_w_q, m_attn_w_o, m_ffn_w_up, m_ffn_w_down]
    big_v = [v_pool_w_in, v_pool_w_grp, v_pool_w_out, v_w_kv, v_attn_w_q, v_attn_w_o, v_ffn_w_up, v_ffn_w_down]
    big_w = [pool_w_in, pool_w_grp, pool_w_out, w_kv, attn_w_q, attn_w_o, ffn_w_up, ffn_w_down]
    big_res = []
    for nm, w, m_, v_ in zip(big_names, big_w, big_m, big_v):
        cols = gsum[nm].shape[-1]
        res = adamw(w.reshape(-1, cols), gsum[nm].reshape(-1, cols), m_.reshape(-1, cols), v_.reshape(-1, cols), "adamw_big")
        big_res.append([r.reshape(w.shape) for r in res])

    order = ["ada_w", "ada_b", "norm1_g", "norm2_g", "pool_w_in", "pool_w_grp", "pool_scale", "pool_w_out", "kv_norm_g", "kv_ada_w",
             "kv_ada_b", "w_kv", "attn_w_q", "attn_w_o", "ffn_w_up", "ffn_conv_w", "ffn_conv_b", "ffn_w_down", "final_g"]
    small_names = ["ada_b", "norm1_g", "norm2_g", "kv_norm_g", "kv_ada_b", "ffn_conv_b", "final_g", "pool_scale", "ffn_conv_w"]
    results = {"ada_w": ada_res, "kv_ada_w": kvada_res}
    for i, nm in enumerate(small_names):
        results[nm] = [s_g[i], s_dl[i], s_m[i], s_v[i]]
    for i, nm in enumerate(["pool_w_in", "pool_w_grp", "pool_w_out", "w_kv", "attn_w_q", "attn_w_o", "ffn_w_up", "ffn_w_down"]):
        results[nm] = big_res[i]
    outs = [loss, grad_x]
    for kind in range(4):
        outs += [results[nm][kind] for nm in order]
    return tuple(outs)
```

```python
import functools
import math

import numpy as np
import jax
import jax.numpy as jnp
from jax import lax
from jax.experimental import pallas as pl
from jax.experimental.pallas import tpu as pltpu

F32 = jnp.float32
BF16 = jnp.bfloat16
MESH = pl.DeviceIdType.MESH

POOL_WINDOWS = (2, 4, 8, 16)
BRANCHES = ((128, 1), (512, 4), (2048, 16))
HEAD_DIM = 64
ATTN_BLOCK = 128
EPS = 1e-6
LR, B1, B2, ADAM_EPS, WD, STEP = 0.001, 0.9, 0.999, 1e-08, 0.01, 10

VMEM_LIMIT_BYTES = 56 * 1024 * 1024
LANES = 128
PACK_W = 1024
HALO = 16
NEG = -1e30
N_CHIPS = 4
N_DEV = 8


def _alibi_slopes(n):
    def pow2(m):
        start = 2.0 ** (-(2.0 ** -(math.log2(m) - 3)))
        return [start ** (i + 1) for i in range(m)]
    if math.log2(n).is_integer():
        s = pow2(n)
    else:
        c = 2 ** math.floor(math.log2(n))
        s = pow2(c) + pow2(2 * c)[0::2][: n - c]
    s = np.asarray(s, dtype=np.float32)
    return -np.sort(-s)


def _cparams(sem=None):
    return pltpu.CompilerParams(dimension_semantics=sem, vmem_limit_bytes=VMEM_LIMIT_BYTES)


def _tile(n, pref, unit):
    t = (min(pref, n) // unit) * unit
    while t >= unit:
        if n % t == 0:
            return t
        t -= unit
    return n


def _sigmoid(v):
    return 1.0 / (1.0 + jnp.exp(-v))


def all_gather8(xs, name):
    m_per, n = xs.shape

    def body(x_ref, out_ref, send_sems, recv_sems, local_sem):
        x, y, c = lax.axis_index("x"), lax.axis_index("y"), lax.axis_index("c")
        me, sibling = (x, y, c), (x, y, 1 - c)
        chips = [(1 - x, y), (x, 1 - y), (1 - x, 1 - y)]

        def rows(px, py, pc):
            return out_ref.at[pl.ds((4 * px + 2 * py + pc) * m_per, m_per), :]

        def copy(k, block, to, src=None):
            return pltpu.make_async_remote_copy(src_ref=rows(*block) if src is None else src, dst_ref=rows(*block),
                                                send_sem=send_sems.at[k], recv_sem=recv_sems.at[k], device_id=to, device_id_type=MESH)

        mine = pltpu.make_async_copy(x_ref, rows(*me), local_sem)
        mine.start()
        first = [copy(0, me, sibling, src=x_ref)]
        first += [copy(1 + j, me, (*chip, c), src=x_ref) for j, chip in enumerate(chips)]
        for cp in first:
            cp.start()
        passed = [copy(4 + j, (*chip, c), sibling) for j, chip in enumerate(chips)]
        for j, chip in enumerate(chips):
            copy(1 + j, (*chip, c), me).wait_recv()
            passed[j].start()
        copy(0, sibling, me).wait_recv()
        for j, chip in enumerate(chips):
            copy(4 + j, (*chip, 1 - c), me).wait_recv()
        for cp in first + passed:
            cp.wait_send()
        mine.wait()

    return pl.pallas_call(
        body, name=name,
        out_shape=jax.ShapeDtypeStruct((N_DEV * m_per, n), xs.dtype),
        in_specs=[pl.BlockSpec(memory_space=pltpu.VMEM)],
        out_specs=pl.BlockSpec(memory_space=pltpu.VMEM),
        scratch_shapes=[pltpu.SemaphoreType.DMA((7,)), pltpu.SemaphoreType.DMA((7,)), pltpu.SemaphoreType.DMA],
        compiler_params=pltpu.CompilerParams(vmem_limit_bytes=VMEM_LIMIT_BYTES),
    )(xs)


HBM_SPEC = pl.BlockSpec(memory_space=pltpu.HBM)


def _mesh_pos():
    x, y, c = lax.axis_index("x"), lax.axis_index("y"), lax.axis_index("c")
    return x, y, c, [(1 - x, y), (x, 1 - y), (1 - x, 1 - y)]


class _Phase:
    def __init__(self, key, group, after, owner, est_us, ins, out_shape, n_sems, n_local, build, buffer=None):
        self.key, self.group, self.after, self.owner, self.est_us = key, group, after, owner, est_us
        self.ins, self.out_shape, self.buffer = ins, out_shape, buffer
        self.n_sems, self.n_local, self.build = n_sems, n_local, build


def _rcopy(src, dst, send_sems, recv_sems, k, to):
    return pltpu.make_async_remote_copy(src_ref=src, dst_ref=dst, send_sem=send_sems.at[k], recv_sem=recv_sems.at[k],
                                        device_id=to, device_id_type=MESH)


def _build_fetch(in_refs, g, send_sems, recv_sems, loc_sems, sem0, loc0, rows, whole):
    (shard,) = in_refs
    x, y, c, chips = _mesh_pos()
    p_me = 2 * x + y
    locs = [pltpu.make_async_copy(shard.at[i], g.at[p_me, i], loc_sems.at[loc0 + i]) for i in range(2)] if whole else []
    sends = [_rcopy(shard.at[c, rows], g.at[p_me, c, rows], send_sems, recv_sems, sem0 + j, (*chip, c)) for j, chip in enumerate(chips)]

    def recvs():
        blks = [g.at[2 * chip[0] + chip[1], c, rows] for chip in chips]
        return [_rcopy(blk, blk, send_sems, recv_sems, sem0 + j, (*chip, c)) for j, (blk, chip) in enumerate(zip(blks, chips))]
    return sends, recvs, locs


def _build_pass(in_refs, g, send_sems, recv_sems, loc_sems, sem0, loc0, rows):
    x, y, c, chips = _mesh_pos()
    sib = (x, y, 1 - c)
    slots = [2 * chip[0] + chip[1] for chip in chips]
    sends = [_rcopy(g.at[p, c, rows], g.at[p, c, rows], send_sems, recv_sems, sem0 + j, sib) for j, p in enumerate(slots)]

    def recvs():
        return [_rcopy(g.at[p, 1 - c, rows], g.at[p, 1 - c, rows], send_sems, recv_sems, sem0 + j, sib) for j, p in enumerate(slots)]
    return sends, recvs, []


def _build_exchange(in_refs, recv, send_sems, recv_sems, loc_sems, sem0, loc0, rows):
    (dw,) = in_refs
    x, y, c, chips = _mesh_pos()
    targets = [(c, chip, c, j) for j, chip in enumerate(chips)]
    targets += [(1 - c, chip, 1 - c, 3 + j) for j, chip in enumerate([(x, y)] + chips)]
    sends = [_rcopy(dw.at[2 * chip[0] + chip[1], half, rows], recv.at[rel, rows], send_sems, recv_sems, sem0 + rel, (*chip, core))
             for half, chip, core, rel in targets]

    def recvs():
        return [_rcopy(recv.at[rel, rows], recv.at[rel, rows], send_sems, recv_sems, sem0 + rel, (x, y, 1 - c)) for rel in range(7)]
    return sends, recvs, []


def _build_swap(in_refs, g, send_sems, recv_sems, loc_sems, sem0, loc0):
    x, y, c, _ = _mesh_pos()
    sib = (x, y, 1 - c)
    sends = [_rcopy(g.at[l, c], g.at[l, c], send_sems, recv_sems, sem0 + l, sib) for l in range(g.shape[0])]

    def recvs():
        return [_rcopy(g.at[l, 1 - c], g.at[l, 1 - c], send_sems, recv_sems, sem0 + l, sib) for l in range(g.shape[0])]
    return sends, recvs, []


def _plan_refs(phases, store):
    xin, xout, alias, n_sems, n_loc, out_of = [], [], {}, 0, 0, {}
    for ph in phases:
        ph.sem0, ph.loc0 = n_sems, n_loc
        n_sems += ph.n_sems
        n_loc += ph.n_local
        ph.in0, ph.n_in = len(xin), len(ph.ins)
        xin += ph.ins
        if ph.owner not in out_of:
            out_of[ph.owner] = len(xout)
            if ph.owner == ph.key and ph.buffer is None:
                xout.append(ph.out_shape)
            else:
                buf = ph.buffer if ph.buffer is not None else store[ph.group]
                alias[len(xin)] = len(xout)
                xin.append(buf)
                xout.append(jax.ShapeDtypeStruct(buf.shape, buf.dtype))
        ph.out0 = out_of[ph.owner]
    return xin, xout, alias, max(n_sems, 1), max(n_loc, 1)


def _built(ph, xin_refs, xout_refs, sems):
    return ph.build(xin_refs[ph.in0:ph.in0 + ph.n_in], xout_refs[ph.out0], sems[0], sems[1], sems[2], ph.sem0, ph.loc0)


def _start(phases, xin_refs, xout_refs, sems):
    for ph in phases:
        sends, _, locs = _built(ph, xin_refs, xout_refs, sems)
        for cp in locs + sends:
            cp.start()


def _finish(phases, xin_refs, xout_refs, sems):
    for ph in phases:
        sends, recvs, locs = _built(ph, xin_refs, xout_refs, sems)
        for cp in recvs():
            cp.wait_recv()
        for cp in sends:
            cp.wait_send()
        for cp in locs:
            cp.wait()


class _Comm:
    def __init__(self):
        self.queue, self.store, self.n_alone = [], {}, 0

    def push(self, ph):
        self.queue.append(ph)

    def take(self, carry_us):
        taken, t = [], 0.0
        while True:
            queued = {ph.key for ph in self.queue}
            pending = queued | {ph.key for ph in taken}
            room = 1.5 * carry_us if not taken else 1.2 * carry_us - t
            fits = [ph for ph in self.queue if ph.after not in pending and (ph.owner == ph.key or ph.owner not in queued)
                    and ph.est_us <= room]
            if not fits:
                return taken
            ph = max(fits, key=lambda p: p.est_us)
            self.queue.remove(ph)
            taken.append(ph)
            t += ph.est_us

    def require(self, group):
        phases = [ph for ph in self.queue if ph.group == group]
        if phases:
            self.queue = [ph for ph in self.queue if ph.group != group]
            self.run_alone(phases)
        return self.store[group]

    def flush(self):
        phases, self.queue = self.queue, []
        if phases:
            self.run_alone(phases)

    def run_alone(self, phases):
        phases = [ph for ph in phases if ph.after is None] + [ph for ph in phases if ph.after is not None]
        groups, keys = [[]], set()
        for ph in phases:
            if ph.after in keys:
                groups.append([])
                keys = set()
            groups[-1].append(ph)
            keys.add(ph.key)
        xin, xout, alias, n_sems, n_loc = _plan_refs(phases, self.store)
        n_xin, n_xout = len(xin), len(xout)

        def body(*refs):
            xin_refs, xout_refs, sems = refs[:n_xin], refs[n_xin:n_xin + n_xout], refs[n_xin + n_xout:]
            for grp in groups:
                _start(grp, xin_refs, xout_refs, sems)
                _finish(grp, xin_refs, xout_refs, sems)

        self.n_alone += 1
        outs = pl.pallas_call(
            body, name=f"comm_alone_{self.n_alone}", out_shape=xout, in_specs=[HBM_SPEC] * n_xin, out_specs=[HBM_SPEC] * n_xout,
            input_output_aliases=alias,
            scratch_shapes=[pltpu.SemaphoreType.DMA((n_sems,)), pltpu.SemaphoreType.DMA((n_sems,)), pltpu.SemaphoreType.DMA((n_loc,))],
        )(*xin)
        for ph in phases:
            self.store[ph.group] = outs[ph.out0]


def _pcall(body, *, name, grid, in_specs, out_specs, out_shape, args, scratch_shapes=(), aliases=None, comm=None, carry_us=0.0):
    phases = comm.take(carry_us) if comm is not None else []
    n_in, n_out, n_scr = len(in_specs), len(out_specs), len(scratch_shapes)
    if not phases:
        return pl.pallas_call(body, name=name, grid=grid, in_specs=in_specs, out_specs=out_specs, out_shape=out_shape,
                              scratch_shapes=list(scratch_shapes), input_output_aliases=aliases or {},
                              compiler_params=_cparams(("arbitrary",) * len(grid)))(*args)
    xin, xout, xalias, n_sems, n_loc = _plan_refs(phases, comm.store)
    n_xin, n_xout = len(xin), len(xout)
    all_alias = dict(aliases or {})
    all_alias.update({n_in + i: n_out + o for i, o in xalias.items()})

    def carrier(*refs):
        ins, xin_refs = refs[:n_in], refs[n_in:n_in + n_xin]
        outs = refs[n_in + n_xin:n_in + n_xin + n_out]
        xout_refs = refs[n_in + n_xin + n_out:n_in + n_xin + n_out + n_xout]
        rest = refs[n_in + n_xin + n_out + n_xout:]
        scr, sems = rest[:n_scr], rest[n_scr:]
        pids = [pl.program_id(k) for k in range(len(grid))]
        first = functools.reduce(jnp.logical_and, [p == 0 for p in pids])
        last = functools.reduce(jnp.logical_and, [p == n - 1 for p, n in zip(pids, grid)])

        @pl.when(first)
        def _():
            _start(phases, xin_refs, xout_refs, sems)
        body(*ins, *outs, *scr)

        @pl.when(last)
        def _():
            _finish(phases, xin_refs, xout_refs, sems)

    outs = pl.pallas_call(
        carrier, name=name, grid=grid, in_specs=list(in_specs) + [HBM_SPEC] * n_xin, out_specs=list(out_specs) + [HBM_SPEC] * n_xout,
        out_shape=list(out_shape) + xout,
        scratch_shapes=list(scratch_shapes) + [pltpu.SemaphoreType.DMA((n_sems,)), pltpu.SemaphoreType.DMA((n_sems,)),
                                               pltpu.SemaphoreType.DMA((n_loc,))],
        input_output_aliases=all_alias, compiler_params=_cparams(("arbitrary",) * len(grid)))(*args, *xin)
    for ph in phases:
        comm.store[ph.group] = outs[n_out + ph.out0]
    return outs[:n_out]


FETCH_US_PER_MB = 20.4
PASS_US_PER_MB = 3.3
EXCHANGE_US_PER_MB = 14.5


FETCH_PHASE_US = 35.0
EXCHANGE_PHASE_US = 20.0


def _row_chunks(rows, est_us, phase_us):
    n = 1
    while est_us / n > phase_us and rows % (2 * n) == 0 and (rows // (2 * n)) % 16 == 0:
        n *= 2
    return [pl.ds(k * (rows // n), rows // n) for k in range(n)]


def push_gather(comm, keys_shards):
    prev = []
    for key, shard in keys_shards:
        r, c = shard.shape
        sh = shard.reshape(2, r // 2, c)
        half_mb = r // 2 * c * 2 / 1e6
        chunks = _row_chunks(r // 2, 3 * half_mb * FETCH_US_PER_MB, FETCH_PHASE_US)
        n = len(chunks)
        shape = jax.ShapeDtypeStruct((N_CHIPS, 2, r // 2, c), BF16)
        for k, rows in enumerate(chunks):
            comm.push(_Phase(("fetch", key, k), key, None, ("fetch", key, 0), 3 * half_mb * FETCH_US_PER_MB / n, [sh], shape, 3,
                             2 if k == 0 else 0, functools.partial(_build_fetch, rows=rows, whole=k == 0)))
        for ph in prev:
            comm.push(ph)
        prev = [_Phase(("pass", key, k), key, ("fetch", key, k), ("fetch", key, 0), 3 * half_mb * PASS_US_PER_MB / n + 3.0, [], shape, 3, 0,
                       functools.partial(_build_pass, rows=rows)) for k, rows in enumerate(chunks)]
    for ph in prev:
        comm.push(ph)


def push_exchange(comm, key, dw):
    _, r, c = dw.shape
    half_mb = r // 2 * c * 2 / 1e6
    chunks = _row_chunks(r // 2, 6 * half_mb * EXCHANGE_US_PER_MB, EXCHANGE_PHASE_US)
    dw5 = dw.reshape(N_CHIPS, 2, r // 2, c)
    for k, rows in enumerate(chunks):
        comm.push(_Phase(("exchange", key, k), key, None, ("exchange", key, 0), 6 * half_mb * EXCHANGE_US_PER_MB / len(chunks), [dw5],
                         jax.ShapeDtypeStruct((7, r // 2, c), BF16), 7, 0, functools.partial(_build_exchange, rows=rows)))


MM_FLOPS_PER_US = 6.0e8


def mm_nn(a, w3, *, tn, out_dtype, name, ncb=None, cbmap=None, res=None, perm_d=1, comm=None):
    M, K = a.shape
    P, _, Ns = w3.shape
    nper = Ns // tn
    ncb = P * nper if ncb is None else ncb
    tm = ATTN_BLOCK * perm_d if perm_d > 1 else _tile(M, 512, 8)
    cbm = cbmap if cbmap is not None else (lambda j: j)
    nch = tn // LANES

    def body(*refs):
        if res is None:
            a_ref, w_ref, o_ref = refs[:3]
        else:
            a_ref, w_ref, x_ref, g_ref, o_ref, xo_ref = refs
        acc = jnp.dot(a_ref[...].astype(BF16), w_ref[...], preferred_element_type=F32)
        if perm_d > 1:
            scr = refs[3]
            for cj in range(nch):
                scr[cj] = acc[:, cj * LANES:(cj + 1) * LANES]
            for r in range(perm_d):
                for cj in range(nch):
                    o_ref[r, :, cj * LANES:(cj + 1) * LANES] = scr.at[cj][pl.ds(r, ATTN_BLOCK, stride=perm_d), :].astype(o_ref.dtype)
        else:
            o_ref[...] = acc.astype(o_ref.dtype)
        if res is not None:
            xo_ref[...] = x_ref[...] + g_ref[...] * acc

    in_specs = [pl.BlockSpec((tm, K), lambda i, j: (i, 0)),
                pl.BlockSpec((None, K, tn), lambda i, j: (cbm(j) // nper, 0, cbm(j) % nper))]
    scratch = []
    if perm_d > 1:
        out_specs = [pl.BlockSpec((perm_d, ATTN_BLOCK, tn), lambda i, j: (0, i, j))]
        out_shape = [jax.ShapeDtypeStruct((perm_d, M // perm_d, ncb * tn), out_dtype)]
        scratch = [pltpu.VMEM((nch, tm, LANES), F32)]
    else:
        out_specs = [pl.BlockSpec((tm, tn), lambda i, j: (i, j))]
        out_shape = [jax.ShapeDtypeStruct((M, ncb * tn), out_dtype)]
    args = [a, w3]
    if res is not None:
        in_specs += [pl.BlockSpec((tm, tn), lambda i, j: (i, j)), pl.BlockSpec((1, tn), lambda i, j: (0, j))]
        out_specs.append(pl.BlockSpec((tm, tn), lambda i, j: (i, j)))
        out_shape.append(jax.ShapeDtypeStruct((M, ncb * tn), F32))
        args += [res[0], res[1]]
    outs = _pcall(body, name=name, grid=(M // tm, ncb), in_specs=in_specs, out_specs=out_specs, out_shape=out_shape, args=args,
                  scratch_shapes=scratch, comm=comm, carry_us=2.0 * M * K * ncb * tn / MM_FLOPS_PER_US)
    if perm_d > 1:
        return outs[0].reshape(M, ncb * tn)
    return outs[0] if res is None else (outs[0], outs[1])


def permute_rows(x, d, name):
    S, C = x.shape
    R = ATTN_BLOCK * d
    ct = _tile(C, 256, LANES)
    nch = ct // LANES

    def body(x_ref, o_ref, scr):
        xv = x_ref[...].astype(F32)
        for cj in range(nch):
            scr[cj] = xv[:, cj * LANES:(cj + 1) * LANES]
        for r in range(d):
            for cj in range(nch):
                o_ref[r, :, cj * LANES:(cj + 1) * LANES] = scr.at[cj][pl.ds(r, ATTN_BLOCK, stride=d), :].astype(o_ref.dtype)

    out = pl.pallas_call(body, name=name, grid=(S // R, C // ct), in_specs=[pl.BlockSpec((R, ct), lambda i, j: (i, j))],
                         out_specs=pl.BlockSpec((d, ATTN_BLOCK, ct), lambda i, j: (0, i, j)),
                         out_shape=jax.ShapeDtypeStruct((d, S // d, C), x.dtype), scratch_shapes=[pltpu.VMEM((nch, R, LANES), F32)],
                         compiler_params=_cparams(("parallel", "parallel")))(x)
    return out.reshape(S, C)


def unpermute_rows(ps, d, name, into=None, total_cols=None, colmap=None):
    ps = list(ps) if isinstance(ps, (list, tuple)) else [ps]
    n_p = len(ps)
    p = ps[0]
    S, C = p.shape
    rpb = max(ATTN_BLOCK, 512 // d)
    R = rpb * d
    ct = _tile(C, 256, LANES)
    nch = ct // LANES
    total_cols = C if total_cols is None else total_cols
    cm = colmap if colmap is not None else (lambda j: j)

    def body(*refs):
        p_refs, o_ref, scr = refs[:n_p], refs[-2], refs[-1]

        def summed(idx):
            return functools.reduce(lambda a, b: a + b, [r[idx] for r in p_refs])
        if d == 1:
            o_ref[...] = summed(0)
            return
        for r in range(d):
            for cj in range(nch):
                scr.at[cj][pl.ds(r, rpb, stride=d), :] = summed((r, slice(None), slice(cj * LANES, (cj + 1) * LANES))).astype(F32)
        for cj in range(nch):
            o_ref[:, cj * LANES:(cj + 1) * LANES] = scr[cj].astype(o_ref.dtype)

    in_specs = [pl.BlockSpec((d, rpb, ct), lambda i, j: (0, i, j))] * n_p
    args = [a.reshape(d, S // d, C) for a in ps]
    aliases = {}
    if into is not None:
        in_specs.append(pl.BlockSpec(memory_space=pl.ANY))
        args.append(into)
        aliases = {n_p: 0}
    return pl.pallas_call(body, name=name, grid=(S // R, C // ct), in_specs=in_specs,
                          out_specs=pl.BlockSpec((R, ct), lambda i, j: (i, cm(j))),
                          out_shape=jax.ShapeDtypeStruct((S, total_cols), p.dtype), scratch_shapes=[pltpu.VMEM((nch, R, LANES), F32)],
                          input_output_aliases=aliases, compiler_params=_cparams(("parallel", "parallel")))(*args)


def mm_nt(g3, w3, *, tn, tk, out_dtype, name, gmap=None, comm=None):
    _, M, _ = g3.shape
    P, K, Ns = w3.shape
    nper = Ns // tn
    ns = P * nper
    tm = _tile(M, 512, 8)
    gm = gmap if gmap is not None else (lambda s: (0, s))

    def body(g_ref, w_ref, o_ref, acc):
        s = pl.program_id(2)

        @pl.when(s == 0)
        def _():
            acc[...] = jnp.zeros_like(acc)
        acc[...] += lax.dot_general(g_ref[...].astype(BF16), w_ref[...], (((1,), (1,)), ((), ())), preferred_element_type=F32)

        @pl.when(s == ns - 1)
        def _():
            o_ref[...] = acc[...].astype(o_ref.dtype)

    return _pcall(
        body, name=name, grid=(M // tm, K // tk, ns),
        in_specs=[pl.BlockSpec((None, tm, tn), lambda i, kj, s: (gm(s)[0], i, gm(s)[1])),
                  pl.BlockSpec((None, tk, tn), lambda i, kj, s: (s // nper, kj, s % nper))],
        out_specs=[pl.BlockSpec((tm, tk), lambda i, kj, s: (i, kj))],
        out_shape=[jax.ShapeDtypeStruct((M, K), out_dtype)], args=[g3, w3],
        scratch_shapes=[pltpu.VMEM((tm, tk), F32)], comm=comm, carry_us=2.0 * M * K * P * Ns / MM_FLOPS_PER_US)[0]


def mm_tn(a, g3, wshape, *, tn, tk, name, gmap=None, comm=None):
    M, K = a.shape
    P, _, Ns = wshape
    nper = Ns // tn
    ns = P * nper
    tm = _tile(M, 512, 16)
    nm = M // tm
    gm = gmap if gmap is not None else (lambda s: (0, s))

    def body(a_ref, g_ref, o_ref, acc):
        mi = pl.program_id(2)

        @pl.when(mi == 0)
        def _():
            acc[...] = jnp.zeros_like(acc)
        acc[...] += lax.dot_general(a_ref[...].astype(BF16), g_ref[...].astype(BF16), (((0,), (0,)), ((), ())), preferred_element_type=F32)

        @pl.when(mi == nm - 1)
        def _():
            o_ref[...] = acc[...].astype(o_ref.dtype)

    return _pcall(
        body, name=name, grid=(ns, K // tk, nm),
        in_specs=[pl.BlockSpec((tm, tk), lambda s, kj, mi: (mi, kj)),
                  pl.BlockSpec((None, tm, tn), lambda s, kj, mi: (gm(s)[0], mi, gm(s)[1]))],
        out_specs=[pl.BlockSpec((None, tk, tn), lambda s, kj, mi: (s // nper, kj, s % nper))],
        out_shape=[jax.ShapeDtypeStruct((P, K, Ns), BF16)], args=[a, g3],
        scratch_shapes=[pltpu.VMEM((tk, tn), F32)], comm=comm, carry_us=2.0 * M * K * P * Ns / MM_FLOPS_PER_US)[0]


def _vspec(d):
    return pl.BlockSpec((1, d), lambda i: (0, 0))


NORM_US_PER_ELEM = 12.0 / (4096 * 1024)


def norm_mod(x, g, sh, sc, name, comm=None):
    S, D = x.shape
    tm = _tile(S, 512, 16)

    def body(x_ref, g_ref, sh_ref, sc_ref, o_ref):
        xv = x_ref[...]
        r = lax.rsqrt(jnp.mean(xv * xv, axis=-1, keepdims=True) + EPS)
        o_ref[...] = ((xv * r) * g_ref[...] * (1.0 + sc_ref[...]) + sh_ref[...]).astype(o_ref.dtype)

    return _pcall(body, name=name, grid=(S // tm,),
                  in_specs=[pl.BlockSpec((tm, D), lambda i: (i, 0)), _vspec(D), _vspec(D), _vspec(D)],
                  out_specs=[pl.BlockSpec((tm, D), lambda i: (i, 0))], out_shape=[jax.ShapeDtypeStruct((S, D), BF16)],
                  args=[x, g, sh, sc], comm=comm, carry_us=NORM_US_PER_ELEM * S * D)[0]


def _gate_outputs(dx, gate_ref, out_ref, dout_ref):
    dout_ref[...] = (gate_ref[...] * dx).astype(dout_ref.dtype)
    return jnp.sum(dx * out_ref[...].astype(F32), axis=0, keepdims=True)


NORM_BWD_US_PER_ELEM = 28.0 / (4096 * 1024)


def norm_mod_bwd(dh, x, dres, g, sc, name, below=None, comm=None):
    S, D = x.shape
    tm = _tile(S, 256, 16)

    def body(dh_ref, x_ref, dr_ref, g_ref, sc_ref, *rest):
        dx_ref, sums_ref = (rest[2], rest[3]) if below is not None else (rest[0], rest[1])
        xv = x_ref[...]
        dhv = dh_ref[...].astype(F32)
        r = lax.rsqrt(jnp.mean(xv * xv, axis=-1, keepdims=True) + EPS)
        xn = xv * r
        one_sc = 1.0 + sc_ref[...]
        dxn = dhv * g_ref[...] * one_sc
        dx = r * (dxn - xn * jnp.mean(dxn * xn, axis=-1, keepdims=True)) + dr_ref[...]
        dx_ref[...] = dx
        rows = [jnp.sum(dhv, axis=0, keepdims=True), jnp.sum(dhv * xn * g_ref[...], axis=0, keepdims=True),
                jnp.sum(dhv * one_sc * xn, axis=0, keepdims=True)]
        if below is not None:
            rows.append(_gate_outputs(dx, rest[0], rest[1], rest[4]))
        part = jnp.concatenate(rows + [jnp.zeros((8 - len(rows), D), F32)], axis=0)

        @pl.when(pl.program_id(0) == 0)
        def _():
            sums_ref[...] = jnp.zeros_like(sums_ref)
        sums_ref[...] += part

    row = pl.BlockSpec((tm, D), lambda i: (i, 0))
    in_specs, args = [row, row, row, _vspec(D), _vspec(D)], [dh, x, dres, g, sc]
    out_specs = [row, pl.BlockSpec((8, D), lambda i: (0, 0))]
    out_shape = [jax.ShapeDtypeStruct((S, D), F32), jax.ShapeDtypeStruct((8, D), F32)]
    if below is not None:
        in_specs += [_vspec(D), row]
        args += [below[0], below[1]]
        out_specs.append(row)
        out_shape.append(jax.ShapeDtypeStruct((S, D), BF16))
    return _pcall(body, name=name, grid=(S // tm,), in_specs=in_specs, out_specs=out_specs, out_shape=out_shape, args=args,
                  comm=comm, carry_us=NORM_BWD_US_PER_ELEM * S * D)


def loss_fwd_bwd(x, g, target, below, name):
    S, D = x.shape
    tm = _tile(S, 256, 16)

    def body(x_ref, g_ref, t_ref, gate_ref, out_ref, dx_ref, sums_ref, dout_ref):
        xv = x_ref[...]
        r = lax.rsqrt(jnp.mean(xv * xv, axis=-1, keepdims=True) + EPS)
        xn = xv * r
        err = xn * g_ref[...] - t_ref[...]
        dy = err * (1.0 / D)
        dxn = dy * g_ref[...]
        dx = r * (dxn - xn * jnp.mean(dxn * xn, axis=-1, keepdims=True))
        dx_ref[...] = dx
        part = jnp.concatenate([jnp.sum(dy * xn, axis=0, keepdims=True), jnp.sum(err * err, axis=0, keepdims=True),
                                _gate_outputs(dx, gate_ref, out_ref, dout_ref), jnp.zeros((5, D), F32)], axis=0)

        @pl.when(pl.program_id(0) == 0)
        def _():
            sums_ref[...] = jnp.zeros_like(sums_ref)
        sums_ref[...] += part

    row = pl.BlockSpec((tm, D), lambda i: (i, 0))
    return pl.pallas_call(body, name=name, grid=(S // tm,), in_specs=[row, _vspec(D), row, _vspec(D), row],
                          out_specs=[row, pl.BlockSpec((8, D), lambda i: (0, 0)), row],
                          out_shape=[jax.ShapeDtypeStruct((S, D), F32), jax.ShapeDtypeStruct((8, D), F32), jax.ShapeDtypeStruct((S, D), BF16)],
                          compiler_params=_cparams(("arbitrary",)))(x, g, target, below[0], below[1])


def pool_fwd(u, wgrp, scale, name):
    S, D = u.shape
    G = len(POOL_WINDOWS)
    C = D // G
    tm = _tile(S, 256, 16)
    hb = tm // HALO

    def body(up_ref, uc_ref, w_ref, sc_ref, p_ref, z_ref, y_ref):
        i = pl.program_id(0)
        prev = jnp.where(i > 0, up_ref[...], 0.0)
        ext = jnp.concatenate([prev, uc_ref[...]], axis=0)
        t = i * tm + lax.broadcasted_iota(jnp.int32, (tm, 1), 0)
        for gi, w in enumerate(POOL_WINDOWS):
            cs = slice(gi * C, (gi + 1) * C)
            e = ext[:, cs]
            s, k = e, 1
            while k < w:
                s = s + pltpu.roll(s, k, 0)
                k *= 2
            cnt = jnp.minimum(t + 1, w).astype(F32)
            pooled = (s[HALO:] / cnt - e[HALO:]).astype(BF16)
            p_ref[:, cs] = pooled
            z = jnp.dot(pooled, w_ref[:, gi].reshape(C, C), preferred_element_type=F32)
            z_ref[:, cs] = z.astype(BF16)
            y_ref[:, cs] = (z * sc_ref[:, cs]).astype(BF16)

    row = pl.BlockSpec((tm, D), lambda i: (i, 0))
    return pl.pallas_call(
        body, name=name, grid=(S // tm,),
        in_specs=[pl.BlockSpec((HALO, D), lambda i: (jnp.maximum(i * hb - 1, 0), 0)), row,
                  pl.BlockSpec(wgrp.shape, lambda i: (0, 0, 0, 0)), _vspec(D)],
        out_specs=[row, row, row], out_shape=[jax.ShapeDtypeStruct((S, D), BF16)] * 3,
        compiler_params=_cparams(("parallel",)))(u, u, wgrp, scale)


def pool_bwd(dys, z, pooled, wgrp, scale, name):
    S, D = dys.shape
    G = len(POOL_WINDOWS)
    C = D // G
    tm = _tile(S, 256, 16)
    hb = tm // HALO
    nt = S // tm
    n_ext = tm + HALO

    def body(dc_ref, dn_ref, z_ref, p_ref, w_ref, sc_ref, du_ref, dw_ref, sums_ref):
        i = pl.program_id(0)

        @pl.when(i == 0)
        def _():
            dw_ref[...] = jnp.zeros_like(dw_ref)
            sums_ref[...] = jnp.zeros_like(sums_ref)
        dyc = dc_ref[...].astype(F32)
        nxt = jnp.where(i < nt - 1, dn_ref[...].astype(F32), 0.0)
        ext = jnp.concatenate([dyc, nxt], axis=0)
        sums_ref[...] += jnp.concatenate([jnp.sum(dyc * z_ref[...].astype(F32), axis=0, keepdims=True), jnp.zeros((7, D), F32)], axis=0)
        t = i * tm + lax.broadcasted_iota(jnp.int32, (n_ext, 1), 0)
        for gi, w in enumerate(POOL_WINDOWS):
            cs = slice(gi * C, (gi + 1) * C)
            wg = w_ref[:, gi].reshape(C, C)
            dz = (ext[:, cs] * sc_ref[:, cs]).astype(BF16)
            dpool = lax.dot_general(dz, wg, (((1,), (1,)), ((), ())), preferred_element_type=F32)
            dw_ref[gi] += lax.dot_general(p_ref[:, cs], dz[:tm], (((0,), (0,)), ((), ())), preferred_element_type=F32)
            cnt = jnp.minimum(t + 1, w).astype(F32)
            s, k = dpool / cnt, 1
            while k < w:
                s = s + pltpu.roll(s, n_ext - k, 0)
                k *= 2
            du_ref[:, cs] = (s[:tm] - dpool[:tm]).astype(BF16)

    row = pl.BlockSpec((tm, D), lambda i: (i, 0))
    return pl.pallas_call(
        body, name=name, grid=(nt,),
        in_specs=[row, pl.BlockSpec((HALO, D), lambda i: (jnp.minimum((i + 1) * hb, S // HALO - 1), 0)), row, row,
                  pl.BlockSpec(wgrp.shape, lambda i: (0, 0, 0, 0)), _vspec(D)],
        out_specs=[row, pl.BlockSpec((G, C, C), lambda i: (0, 0, 0)), pl.BlockSpec((8, D), lambda i: (0, 0))],
        out_shape=[jax.ShapeDtypeStruct((S, D), BF16), jax.ShapeDtypeStruct((G, C, C), F32), jax.ShapeDtypeStruct((8, D), F32)],
        compiler_params=_cparams(("arbitrary",)))(dys, dys, z, pooled, wgrp, scale)


FFN_ACT_BWD_US_PER_ELEM = 84.0 / (4096 * 2816)


def ffn_up_act(h, w3, conv_w, conv_b, name, comm=None):
    S, D = h.shape
    P, _, Ns = w3.shape
    nh = P // 2
    tm = _tile(S, 512, 16)

    def body(h_ref, w_ref, cw_ref, cb_ref, hu_ref, g_ref, stash, halo):
        i, j = pl.program_id(0), pl.program_id(1)
        acc = jnp.dot(h_ref[...], w_ref[...], preferred_element_type=F32).astype(BF16)
        hu_ref[...] = acc

        @pl.when(j < nh)
        def _():
            stash[j] = acc.astype(F32)

        @pl.when(j >= nh)
        def _():
            c = j - nh
            a = stash[c]
            ext = jnp.concatenate([jnp.where(i > 0, halo[c], 0.0), a], axis=0)
            conv = cb_ref[...] + pltpu.roll(ext, 2, 0) * cw_ref[0:1, :] + pltpu.roll(ext, 1, 0) * cw_ref[1:2, :] + ext * cw_ref[2:3, :]
            conv = conv[HALO:]
            g_ref[...] = (conv * _sigmoid(conv) * acc.astype(F32)).astype(g_ref.dtype)
            halo[c] = a[tm - HALO:]

    def gcol(j):
        return jnp.maximum(j - nh, 0)

    return _pcall(
        body, name=name, grid=(S // tm, P),
        in_specs=[pl.BlockSpec((tm, D), lambda i, j: (i, 0)), pl.BlockSpec((None, D, Ns), lambda i, j: (j, 0, 0)),
                  pl.BlockSpec((3, Ns), lambda i, j: (0, gcol(j))), pl.BlockSpec((1, Ns), lambda i, j: (0, gcol(j)))],
        out_specs=[pl.BlockSpec((tm, Ns), lambda i, j: (i, j)), pl.BlockSpec((tm, Ns), lambda i, j: (i, gcol(j)))],
        out_shape=[jax.ShapeDtypeStruct((S, P * Ns), BF16), jax.ShapeDtypeStruct((S, nh * Ns), BF16)],
        scratch_shapes=[pltpu.VMEM((nh, tm, Ns), F32), pltpu.VMEM((nh, HALO, Ns), F32)],
        args=[h, w3, conv_w, conv_b], comm=comm, carry_us=2.0 * S * D * P * Ns / MM_FLOPS_PER_US)


def ffn_act_bwd(dout, w_down, hu, conv_w, conv_b, name, comm=None):
    S, D = dout.shape
    F = w_down.shape[1]
    tm = _tile(S, 256, 16)
    tn = _tile(F, 1408, LANES)
    nb = F // tn
    hb = tm // HALO
    nt = S // tm
    n_ext = tm + 2 * HALO
    nt_dims = (((1,), (1,)), ((), ()))

    def body(dc_ref, dn_ref, wd_ref, ap_ref, ac_ref, an_ref, vc_ref, vn_ref, w_ref, b_ref, o_ref, sums_ref):
        i = pl.program_id(1)

        @pl.when(i == 0)
        def _():
            sums_ref[...] = jnp.zeros_like(sums_ref)
        zeros = jnp.zeros((HALO, tn), F32)
        not_last = i < nt - 1
        a_ext = jnp.concatenate([jnp.where(i > 0, ap_ref[...].astype(F32), 0.0), ac_ref[...].astype(F32), an_ref[...].astype(F32)], axis=0)
        v_ext = jnp.concatenate([zeros, vc_ref[...].astype(F32), vn_ref[...].astype(F32)], axis=0)
        g_cur = lax.dot_general(dc_ref[...], wd_ref[...], nt_dims, preferred_element_type=F32)
        g_nxt = lax.dot_general(dn_ref[...], wd_ref[...], nt_dims, preferred_element_type=F32)
        g_ext = jnp.concatenate([zeros, g_cur, jnp.where(not_last, g_nxt, 0.0)], axis=0)
        w0, w1, w2 = w_ref[0:1, :], w_ref[1:2, :], w_ref[2:3, :]
        a_m2, a_m1 = pltpu.roll(a_ext, 2, 0), pltpu.roll(a_ext, 1, 0)
        conv = b_ref[...] + a_m2 * w0 + a_m1 * w1 + a_ext * w2
        sig = _sigmoid(conv)
        silu = conv * sig
        dsilu = sig * (1.0 + conv * (1.0 - sig))
        dconv = g_ext * v_ext * dsilu
        da = dconv * w2 + pltpu.roll(dconv, n_ext - 1, 0) * w1 + pltpu.roll(dconv, n_ext - 2, 0) * w0
        cur = slice(HALO, HALO + tm)
        o_ref[0] = da[cur].astype(o_ref.dtype)
        o_ref[1] = (g_ext * silu)[cur].astype(o_ref.dtype)
        dc = dconv[cur]
        part = jnp.concatenate([jnp.sum(dc * a_m2[cur], axis=0, keepdims=True), jnp.sum(dc * a_m1[cur], axis=0, keepdims=True),
                                jnp.sum(dc * a_ext[cur], axis=0, keepdims=True), jnp.sum(dc, axis=0, keepdims=True),
                                jnp.zeros((4, tn), F32)], axis=0)
        sums_ref[...] += part

    def prev(i):
        return jnp.maximum(i * hb - 1, 0)

    def nxt(i):
        return jnp.minimum((i + 1) * hb, S // HALO - 1)

    return _pcall(
        body, name=name, grid=(nb, nt),
        in_specs=[pl.BlockSpec((tm, D), lambda j, i: (i, 0)), pl.BlockSpec((HALO, D), lambda j, i: (nxt(i), 0)),
                  pl.BlockSpec((None, tn, D), lambda j, i: (0, j, 0)),
                  pl.BlockSpec((HALO, tn), lambda j, i: (prev(i), j)), pl.BlockSpec((tm, tn), lambda j, i: (i, j)),
                  pl.BlockSpec((HALO, tn), lambda j, i: (nxt(i), j)),
                  pl.BlockSpec((tm, tn), lambda j, i: (i, j + nb)), pl.BlockSpec((HALO, tn), lambda j, i: (nxt(i), j + nb)),
                  pl.BlockSpec((3, tn), lambda j, i: (0, j)), pl.BlockSpec((1, tn), lambda j, i: (0, j))],
        out_specs=[pl.BlockSpec((2, tm, tn), lambda j, i: (0, i, j)), pl.BlockSpec((8, tn), lambda j, i: (0, j))],
        out_shape=[jax.ShapeDtypeStruct((2, S, F), BF16), jax.ShapeDtypeStruct((8, F), F32)],
        args=[dout, dout, w_down, hu, hu, hu, hu, hu, conv_w, conv_b], comm=comm, carry_us=FFN_ACT_BWD_US_PER_ELEM * S * F)


def _head_expander(n_heads, da):
    e = np.zeros((LANES, da), np.float32)
    for h in range(n_heads):
        e[h, h * HEAD_DIM:(h + 1) * HEAD_DIM] = 1.0
    return jnp.asarray(e, BF16)


def _split_dot(v, e, dims):
    hi = v.astype(BF16)
    lo = (v - hi.astype(F32)).astype(BF16)
    return (lax.dot_general(hi, e, dims, preferred_element_type=F32) + lax.dot_general(lo, e, dims, preferred_element_type=F32))


def _lane_col(tile, h):
    lane = lax.broadcasted_iota(jnp.int32, tile.shape, 1)
    return jnp.sum(jnp.where(lane == h, tile, 0.0), axis=1, keepdims=True)


ATTN_US_PER_ELEM = (80.0 / (4096 * 1024), 230.0 / (4096 * 1024))


def attn_branch_fwd(q, kv, gi, slopes, name, comm=None):
    S, DA = q.shape
    H = DA // HEAD_DIM
    window, d = BRANCHES[gi]
    n_steps = window // d
    blk = ATTN_BLOCK
    assert n_steps == blk and (S // d) % blk == 0
    nbs = S // d // blk
    scale = HEAD_DIM ** -0.5

    def body(q_ref, kp_ref, kc_ref, vp_ref, vc_ref, o_ref, l_ref, s_scr, p_scr):
        jb = pl.program_id(1)
        row = lax.broadcasted_iota(jnp.int32, (blk, 2 * blk), 0)
        col = lax.broadcasted_iota(jnp.int32, (blk, 2 * blk), 1)
        delta = row + blk - col
        valid = (delta >= 0) & (delta <= n_steps) & ((col >= blk) | (jb > 0))
        dist = jnp.where(valid, (delta * d).astype(F32), -NEG)
        lane = lax.broadcasted_iota(jnp.int32, (blk, LANES), 1)
        ltile = jnp.zeros((blk, LANES), F32)
        for h in range(H):
            hs = slice(h * HEAD_DIM, (h + 1) * HEAD_DIM)
            k2 = jnp.concatenate([kp_ref[:, hs], kc_ref[:, hs]], axis=0)
            s_scr[h] = lax.dot_general(q_ref[:, hs], k2, (((1,), (1,)), ((), ())), preferred_element_type=F32)
        for h in range(H):
            s = s_scr[h] * scale - float(slopes[h]) * dist
            m = jnp.max(s, axis=-1, keepdims=True)
            p = jnp.exp(s - m)
            l = jnp.sum(p, axis=-1, keepdims=True)
            p_scr[h] = (p / l).astype(BF16)
            ltile = jnp.where(lane == h, m + jnp.log(l), ltile)
        for h in range(H):
            hs = slice(h * HEAD_DIM, (h + 1) * HEAD_DIM)
            v2 = jnp.concatenate([vp_ref[:, hs], vc_ref[:, hs]], axis=0)
            o_ref[:, hs] = jnp.dot(p_scr[h], v2, preferred_element_type=F32)
        l_ref[...] = ltile

    def cur(width, off):
        return pl.BlockSpec((blk, width), lambda r, jb: (r * nbs + jb, off))

    def prv(width, off):
        return pl.BlockSpec((blk, width), lambda r, jb: (r * nbs + jnp.maximum(jb - 1, 0), off))

    return _pcall(
        body, name=name, grid=(d, nbs),
        in_specs=[cur(DA, 0), prv(DA, 0), cur(DA, 0), prv(DA, 1), cur(DA, 1)],
        out_specs=[cur(DA, 0), cur(LANES, 0)],
        out_shape=[jax.ShapeDtypeStruct((S, DA), F32), jax.ShapeDtypeStruct((S, LANES), F32)],
        scratch_shapes=[pltpu.VMEM((H, blk, 2 * blk), F32), pltpu.VMEM((H, blk, 2 * blk), BF16)],
        args=[q, kv, kv, kv, kv], comm=comm, carry_us=ATTN_US_PER_ELEM[0] * S * DA)


def attn_combine(os_, lses, name):
    S, DA = os_[0].shape
    H = DA // HEAD_DIM
    tm = _tile(S, 256, 16)
    expander = _head_expander(H, DA)
    nbr = len(os_)

    def body(*refs):
        o_refs, l_refs, e_ref = refs[:nbr], refs[nbr:2 * nbr], refs[2 * nbr]
        out_ref, lse_ref = refs[2 * nbr + 1:]
        ls = [r[...] for r in l_refs]
        lmax = functools.reduce(jnp.maximum, ls)
        es = [jnp.exp(l - lmax) for l in ls]
        den = functools.reduce(lambda a, b: a + b, es)
        lse_ref[...] = lmax + jnp.log(den)
        acc = jnp.zeros((tm, DA), F32)
        for e, o_ref in zip(es, o_refs):
            acc = acc + _split_dot(e / den, e_ref[...], (((1,), (0,)), ((), ()))) * o_ref[...]
        out_ref[...] = acc.astype(out_ref.dtype)

    row = pl.BlockSpec((tm, DA), lambda i: (i, 0))
    lrow = pl.BlockSpec((tm, LANES), lambda i: (i, 0))
    return pl.pallas_call(
        body, name=name, grid=(S // tm,),
        in_specs=[row] * nbr + [lrow] * nbr + [pl.BlockSpec((LANES, DA), lambda i: (0, 0))],
        out_specs=[row, lrow], out_shape=[jax.ShapeDtypeStruct((S, DA), BF16), jax.ShapeDtypeStruct((S, LANES), F32)],
        compiler_params=_cparams(("parallel",)))(*os_, *lses, expander)


def attn_delta(do, o, name):
    S, DA = o.shape
    H = DA // HEAD_DIM
    tm = _tile(S, 512, 16)
    expander = _head_expander(H, DA)

    def body(do_ref, o_ref, e_ref, d_ref):
        prod = do_ref[...].astype(F32) * o_ref[...].astype(F32)
        d_ref[...] = _split_dot(prod, e_ref[...], (((1,), (1,)), ((), ())))

    row = pl.BlockSpec((tm, DA), lambda i: (i, 0))
    return pl.pallas_call(body, name=name, grid=(S // tm,), in_specs=[row, row, pl.BlockSpec((LANES, DA), lambda i: (0, 0))],
                          out_specs=pl.BlockSpec((tm, LANES), lambda i: (i, 0)), out_shape=jax.ShapeDtypeStruct((S, LANES), F32),
                          compiler_params=_cparams(("parallel",)))(do, o, expander)


def attn_branch_bwd(q, kv, do, lse, dlt, gi, slopes, name, out_cols=None, comm=None):
    S, DA = q.shape
    H = DA // HEAD_DIM
    window, d = BRANCHES[gi]
    n_steps = window // d
    blk = ATTN_BLOCK
    nbs = S // d // blk
    scale = HEAD_DIM ** -0.5
    nt, tn = (((1,), (1,)), ((), ())), (((0,), (0,)), ((), ()))

    def body(*refs):
        k_ref, v_ref, qc_ref, qn_ref, doc_ref, don_ref, lc_ref, ln_ref, dc_ref, dn_ref = refs[:10]
        dq_ref, dkv_ref, carry, s_scr, dp_scr, p_scr, ds_scr = refs[-7:]
        kb = pl.program_id(1)

        @pl.when(kb == 0)
        def _():
            carry[...] = jnp.zeros_like(carry)
        row = lax.broadcasted_iota(jnp.int32, (2 * blk, blk), 0)
        col = lax.broadcasted_iota(jnp.int32, (2 * blk, blk), 1)
        delta = row - col
        valid = (delta >= 0) & (delta <= n_steps) & ((row < blk) | (kb < nbs - 1))
        dist = jnp.where(valid, (delta * d).astype(F32), -NEG)
        l2 = jnp.concatenate([lc_ref[...], ln_ref[...]], axis=0)
        d2 = jnp.concatenate([dc_ref[...], dn_ref[...]], axis=0)
        for h in range(H):
            hs = slice(h * HEAD_DIM, (h + 1) * HEAD_DIM)
            q2 = jnp.concatenate([qc_ref[:, hs], qn_ref[:, hs]], axis=0)
            do2 = jnp.concatenate([doc_ref[:, hs], don_ref[:, hs]], axis=0)
            s_scr[h] = lax.dot_general(q2, k_ref[:, hs], nt, preferred_element_type=F32)
            dp_scr[h] = lax.dot_general(do2, v_ref[:, hs], nt, preferred_element_type=F32)
        for h in range(H):
            p = jnp.exp(s_scr[h] * scale - float(slopes[h]) * dist - _lane_col(l2, h))
            p_scr[h] = p.astype(BF16)
            ds_scr[h] = (p * (dp_scr[h] - _lane_col(d2, h))).astype(BF16)
        for h in range(H):
            hs = slice(h * HEAD_DIM, (h + 1) * HEAD_DIM)
            vs = slice(DA + h * HEAD_DIM, DA + (h + 1) * HEAD_DIM)
            q2 = jnp.concatenate([qc_ref[:, hs], qn_ref[:, hs]], axis=0)
            do2 = jnp.concatenate([doc_ref[:, hs], don_ref[:, hs]], axis=0)
            dvh = lax.dot_general(p_scr[h], do2, tn, preferred_element_type=F32)
            dkh = lax.dot_general(ds_scr[h], q2, tn, preferred_element_type=F32) * scale
            dq2 = jnp.dot(ds_scr[h], k_ref[:, hs], preferred_element_type=F32) * scale
            dq_ref[:, hs] = (carry[:, hs] + dq2[:blk]).astype(dq_ref.dtype)
            carry[:, hs] = dq2[blk:]
            dkv_ref[:, hs] = dkh
            dkv_ref[:, vs] = dvh

    def cur(width, off):
        return pl.BlockSpec((blk, width), lambda r, kb: (r * nbs + kb, off))

    def nxt(width, off):
        return pl.BlockSpec((blk, width), lambda r, kb: (r * nbs + jnp.minimum(kb + 1, nbs - 1), off))

    in_specs = [cur(DA, 0), cur(DA, 1), cur(DA, 0), nxt(DA, 0), cur(DA, 0), nxt(DA, 0),
                cur(LANES, 0), nxt(LANES, 0), cur(LANES, 0), nxt(LANES, 0)]
    args = [kv, kv, q, q, do, do, lse, lse, dlt, dlt]
    return _pcall(
        body, name=name, grid=(d, nbs), in_specs=in_specs, out_specs=[cur(DA, 0), cur(2 * DA, 0)],
        out_shape=[jax.ShapeDtypeStruct((S, out_cols or DA), BF16), jax.ShapeDtypeStruct((S, 2 * DA), F32)],
        scratch_shapes=[pltpu.VMEM((blk, DA), F32), pltpu.VMEM((H, 2 * blk, blk), F32), pltpu.VMEM((H, 2 * blk, blk), F32),
                        pltpu.VMEM((H, 2 * blk, blk), BF16), pltpu.VMEM((H, 2 * blk, blk), BF16)],
        args=args, comm=comm, carry_us=ATTN_US_PER_ELEM[1] * S * DA)


def ada_project(c16, w3, b3, name):
    L, D, Ns = w3.shape
    tn = _tile(Ns, 512, LANES)

    def body(c_ref, w_ref, b_ref, o_ref):
        cv = c_ref[...]
        cond = (cv * _sigmoid(cv)).astype(BF16)
        o_ref[...] = jnp.dot(cond, w_ref[...].astype(BF16), preferred_element_type=F32) + b_ref[...]

    return pl.pallas_call(
        body, name=name, grid=(L, Ns // tn),
        in_specs=[pl.BlockSpec((16, D), lambda l, j: (0, 0)), pl.BlockSpec((None, D, tn), lambda l, j: (l, 0, j)),
                  pl.BlockSpec((None, 1, tn), lambda l, j: (l, 0, j))],
        out_specs=pl.BlockSpec((None, 16, tn), lambda l, j: (l, 0, j)), out_shape=jax.ShapeDtypeStruct((L, 16, Ns), F32),
        compiler_params=_cparams(("parallel", "parallel")))(c16, w3, b3)


def _adamw(w, g, m, v):
    m = B1 * m + (1.0 - B1) * g
    v = B2 * v + (1.0 - B2) * (g * g)
    m_hat = m / (1.0 - B1 ** STEP)
    v_hat = v / (1.0 - B2 ** STEP)
    delta = -LR * (m_hat / (jnp.sqrt(v_hat) + ADAM_EPS) + WD * w)
    return delta, m, v


def ada_grad_adamw(c16, d3, w3, m3, v3, name):
    L, D, Ns = w3.shape
    tk = _tile(D, 256, 8)

    def body(c_ref, d_ref, w_ref, m_ref, v_ref, g_out, dl_out, m_out, v_out):
        cv = c_ref[...]
        cond = (cv * _sigmoid(cv)).astype(BF16)
        g = lax.dot_general(cond, d_ref[...].astype(BF16), (((0,), (0,)), ((), ())), preferred_element_type=F32)
        g_out[...] = g
        dl_out[...], m_out[...], v_out[...] = _adamw(w_ref[...], g, m_ref[...], v_ref[...])

    wspec = pl.BlockSpec((None, tk, Ns), lambda l, kj: (l, kj, 0))
    return pl.pallas_call(
        body, name=name, grid=(L, D // tk),
        in_specs=[pl.BlockSpec((16, tk), lambda l, kj: (0, kj)), pl.BlockSpec((None, 16, Ns), lambda l, kj: (l, 0, 0)), wspec, wspec, wspec],
        out_specs=[wspec] * 4, out_shape=[jax.ShapeDtypeStruct((L, D, Ns), F32)] * 4,
        compiler_params=_cparams(("parallel", "parallel")))(c16, d3, w3, m3, v3)


def adamw(w, g, m, v, name):
    R, C = w.shape
    tr = _tile(R, 256, 8)

    def body(w_ref, g_ref, m_ref, v_ref, g_out, dl_out, m_out, v_out):
        g = g_ref[...]
        g_out[...] = g
        dl_out[...], m_out[...], v_out[...] = _adamw(w_ref[...], g, m_ref[...], v_ref[...])

    spec = pl.BlockSpec((tr, C), lambda i: (i, 0))
    return pl.pallas_call(body, name=name, grid=(R // tr,), in_specs=[spec] * 4, out_specs=[spec] * 4,
                          out_shape=[jax.ShapeDtypeStruct((R, C), F32)] * 4, compiler_params=_cparams(("parallel",)))(w, g, m, v)


def sum_partials(own, recv, g_prev, layer, n_layers, pos, name):
    _, Rh, C = recv.shape
    tr = _tile(Rh, 256, 16)

    def body(pos_ref, own_ref, recv_ref, *rest):
        acc = own_ref[...].astype(F32)
        for rel in range(7):
            acc = acc + recv_ref[rel].astype(F32)
        rest[-1][...] = acc

    in_specs = [pl.BlockSpec((None, None, tr, C), lambda r, pos: (pos[1], pos[0], r, 0)), pl.BlockSpec((7, tr, C), lambda r, pos: (0, r, 0))]
    args = [pos, own.reshape(N_CHIPS, 2, Rh, C), recv]
    aliases = {}
    if g_prev is not None:
        in_specs.append(pl.BlockSpec(memory_space=pl.ANY))
        args.append(g_prev)
        aliases = {3: 0}
    return pl.pallas_call(
        body, name=name,
        grid_spec=pltpu.PrefetchScalarGridSpec(
            num_scalar_prefetch=1, grid=(Rh // tr,), in_specs=in_specs,
            out_specs=pl.BlockSpec((None, None, tr, C), lambda r, pos: (layer, pos[0], r, 0))),
        out_shape=jax.ShapeDtypeStruct((n_layers, 2, Rh, C), F32), input_output_aliases=aliases,
        compiler_params=_cparams(("parallel",)))(*args)


def sum_rows8(g8, name):
    _, R, C = g8.shape

    def body(g_ref, o_ref):
        acc = g_ref[0]
        for i in range(1, N_DEV):
            acc = acc + g_ref[i]
        o_ref[...] = acc

    return pl.pallas_call(body, name=name, grid=(1,), in_specs=[pl.BlockSpec((N_DEV, R, C), lambda i: (0, 0, 0))],
                          out_specs=pl.BlockSpec((R, C), lambda i: (0, 0)), out_shape=jax.ShapeDtypeStruct((R, C), F32),
                          compiler_params=_cparams(("arbitrary",)))(g8)


def _pack(vecs):
    flat = [v.reshape(-1).astype(F32) for v in vecs]
    sizes = [f.shape[0] for f in flat]
    total = sum(sizes)
    padded = -(-total // (8 * PACK_W)) * (8 * PACK_W)
    buf = jnp.concatenate(flat + [jnp.zeros((padded - total,), F32)])
    offs = np.concatenate([[0], np.cumsum(sizes)])
    return buf.reshape(-1, PACK_W), offs


def _unpack(buf, offs, shapes):
    flat = buf.reshape(-1)
    return [flat[int(offs[i]):int(offs[i + 1])].reshape(s) for i, s in enumerate(shapes)]


def kernel(x, c, ada_w, ada_b, norm1_g, norm2_g, pool_w_in, pool_w_grp, pool_scale, pool_w_out, kv_norm_g, kv_ada_w, kv_ada_b, w_kv, attn_w_q, attn_w_o, ffn_w_up, ffn_conv_w, ffn_conv_b, ffn_w_down, final_g, loss_target, m_ada_w, m_ada_b, m_norm1_g, m_norm2_g, m_pool_w_in, m_pool_w_grp, m_pool_scale, m_pool_w_out, m_kv_norm_g, m_kv_ada_w, m_kv_ada_b, m_w_kv, m_attn_w_q, m_attn_w_o, m_ffn_w_up, m_ffn_conv_w, m_ffn_conv_b, m_ffn_w_down, m_final_g, v_ada_w, v_ada_b, v_norm1_g, v_norm2_g, v_pool_w_in, v_pool_w_grp, v_pool_scale, v_pool_w_out, v_kv_norm_g, v_kv_ada_w, v_kv_ada_b, v_w_kv, v_attn_w_q, v_attn_w_o, v_ffn_w_up, v_ffn_conv_w, v_ffn_conv_b, v_ffn_w_down, v_final_g):
    S, D = x.shape[1], x.shape[2]
    depth = ada_w.shape[0]
    n_pool = pool_w_in.shape[0]
    n_attn = attn_w_q.shape[0]
    G = len(POOL_WINDOWS)
    NB = len(BRANCHES)
    DA = attn_w_o.shape[1] * N_CHIPS
    H = DA // HEAD_DIM
    F = ffn_conv_b.shape[1]
    Fs = F // N_CHIPS
    Dq = D // N_CHIPS
    ada_ns = ada_w.shape[2]
    kvada_ns = kv_ada_w.shape[1]
    slopes = _alibi_slopes(NB * H).reshape(NB, H)

    ix, iy, ic = lax.axis_index("x"), lax.axis_index("y"), lax.axis_index("c")
    p_me = 2 * ix + iy
    b_me = 4 * ix + 2 * iy + ic
    pos = jnp.stack([ic, p_me]).astype(jnp.int32)
    xs, tgt = x[0], loss_target[0]

    pk, offs = _pack([c, pool_scale, ffn_conv_w])
    rows1 = pk.shape[0]
    got = all_gather8(pk, "gather_small_in").reshape(N_DEV, rows1, PACK_W)
    c8 = got.reshape(N_DEV, -1)[:, :D]
    c16 = jnp.concatenate([c8, jnp.zeros_like(c8)], axis=0)
    chip_rows = got[0::2].reshape(N_CHIPS, -1)
    scale_full = chip_rows[:, int(offs[1]):int(offs[2])].reshape(N_CHIPS, n_pool, Dq).transpose(1, 0, 2).reshape(n_pool, D)
    convw_full = chip_rows[:, int(offs[2]):int(offs[3])].reshape(N_CHIPS, depth, 3, Fs).transpose(1, 2, 0, 3).reshape(depth, 3, F)

    ada_b_loc = lax.dynamic_slice(ada_b, (0, p_me * ada_ns), (depth, ada_ns)).reshape(depth, 1, ada_ns)
    kvb_loc = lax.dynamic_slice(kv_ada_b, (p_me * kvada_ns,), (kvada_ns,)).reshape(1, 1, kvada_ns)
    mods_loc = ada_project(c16, ada_w, ada_b_loc, "ada_project")[:, :N_DEV]
    kvmod_loc = ada_project(c16, kv_ada_w.reshape(1, D, kvada_ns), kvb_loc, "kv_ada_project")[0, :N_DEV]
    mods_cat = jnp.concatenate([mods_loc.transpose(1, 0, 2).reshape(N_DEV, depth * ada_ns), kvmod_loc], axis=1)
    mods_all = all_gather8(mods_cat, "gather_mods").reshape(N_CHIPS, 2, N_DEV, -1)
    mine = lax.dynamic_index_in_dim(mods_all[:, 0], b_me, axis=1, keepdims=False)
    mod = mine[:, :depth * ada_ns].reshape(N_CHIPS, depth, ada_ns).transpose(1, 0, 2).reshape(depth, 6, 1, D)
    kvmod = mine[:, depth * ada_ns:].reshape(2, 1, D)

    comm = _Comm()
    C = D // G
    kv_ns, q_ns, up_ns = w_kv.shape[1], attn_w_q.shape[2], ffn_w_up.shape[2]

    def layer_shards(l):
        sh = []
        if l < n_pool:
            sh += [(("pin", l), pool_w_in[l]), (("pgrp", l), pool_w_grp[l].reshape(-1, C)), (("pout", l), pool_w_out[l])]
        else:
            if l == n_pool:
                sh.append((("kv", 0), w_kv))
            sh += [(("wq", l), attn_w_q[l - n_pool]), (("wo", l), attn_w_o[l - n_pool])]
        sh += [(("up", l), ffn_w_up[l]), (("down", l), ffn_w_down[l])]
        return [(k, w.astype(BF16)) for k, w in sh]

    def weight(key, shape):
        return comm.require(key).reshape(shape)

    dil = [d for _, d in BRANCHES]
    kv_tn = DA // 2
    q_tn = DA // 4
    q_bwd_tn = q_ns
    up_tn = up_ns
    up_per_half = F // up_tn

    def up_gmap(s):
        return s // up_per_half, s % up_per_half

    def vec(v):
        return v.reshape(1, -1)

    saved = []
    xcur = xs
    kvs = None
    wts = {}
    push_gather(comm, layer_shards(0))
    for l in range(depth):
        if l + 1 < depth:
            push_gather(comm, layer_shards(l + 1))
        sh1, sc1, g1, sh2, sc2, g2 = [mod[l, i] for i in range(6)]
        st = {"x0": xcur}
        h1 = norm_mod(xcur, vec(norm1_g[l]), sh1, sc1, "norm_mod", comm=comm)
        st["h1"] = h1
        if l < n_pool:
            wts["pin", l] = weight(("pin", l), (1, D, D))
            u = mm_nn(h1, wts["pin", l], tn=D, out_dtype=F32, name="pool_in_proj", comm=comm)
            wts["pgrp", l] = weight(("pgrp", l), (N_CHIPS, G, C // N_CHIPS, C))
            pooled, z, ys = pool_fwd(u, wts["pgrp", l], vec(scale_full[l]), "pool_mix")
            wts["pout", l] = weight(("pout", l), (1, D, D))
            out, x1 = mm_nn(ys, wts["pout", l], tn=D, out_dtype=BF16, name="pool_out_proj", res=(xcur, g1), comm=comm)
            st.update(pooled=pooled, z=z, ys=ys, out1=out)
        else:
            if l == n_pool:
                wts["kv", 0] = weight(("kv", 0), (N_CHIPS, D, kv_ns))
                hkv = norm_mod(xcur, vec(kv_norm_g), kvmod[0], kvmod[1], "norm_mod", comm=comm)
                kvs = [mm_nn(hkv, wts["kv", 0], tn=kv_tn, out_dtype=BF16, name=f"kv_proj_b{gi}", ncb=4, perm_d=dil[gi], comm=comm,
                             cbmap=functools.partial(lambda jj, gi: 2 * gi + (jj // 2) * 2 * NB + jj % 2, gi=gi)) for gi in range(NB)]
                kv_state = {"x": xcur, "hkv": hkv}
            wts["wq", l] = weight(("wq", l), (N_CHIPS, D, q_ns))
            qs, os_, lses = [], [], []
            for gi in range(NB):
                q_b = mm_nn(h1, wts["wq", l], tn=q_tn, out_dtype=BF16, name=f"q_proj_b{gi}", ncb=4, perm_d=dil[gi], comm=comm,
                            cbmap=functools.partial(lambda jj, gi: 4 * gi + jj, gi=gi))
                o_b, l_b = attn_branch_fwd(q_b, kvs[gi], gi, slopes[gi], f"attn_fwd_b{gi}", comm=comm)
                if dil[gi] > 1:
                    o_b = unpermute_rows(o_b, dil[gi], f"unpermute_o_b{gi}")
                    l_b = unpermute_rows(l_b, dil[gi], f"unpermute_lse_b{gi}")
                qs.append(q_b)
                os_.append(o_b)
                lses.append(l_b)
            o, lse = attn_combine(os_, lses, "attn_combine")
            wts["wo", l] = weight(("wo", l), (1, DA, D))
            out, x1 = mm_nn(o, wts["wo", l], tn=D, out_dtype=BF16, name="attn_out_proj", res=(xcur, g1), comm=comm)
            st.update(qs=qs, o=o, lse=lse, out1=out)
        st["x1"] = x1
        h2 = norm_mod(x1, vec(norm2_g[l]), sh2, sc2, "norm_mod", comm=comm)
        wts["up", l] = weight(("up", l), (N_CHIPS, D, up_ns))
        hu, gated = ffn_up_act(h2, wts["up", l], convw_full[l], vec(ffn_conv_b[l]), "ffn_up_act", comm=comm)
        wts["down", l] = weight(("down", l), (1, F, D))
        out2, x2 = mm_nn(gated, wts["down", l], tn=D, out_dtype=BF16, name="ffn_down_proj", res=(x1, g2), comm=comm)
        st.update(h2=h2, hu=hu, gated=gated, out2=out2)
        saved.append(st)
        xcur = x2
    comm.flush()

    dx, fsums, dout2 = loss_fwd_bwd(xcur, vec(final_g), tgt, (mod[depth - 1, 5], saved[depth - 1]["out2"]), "loss_head")
    loss = lax.psum(0.5 * jnp.sum(fsums[1]) / D, ("x", "y", "c"))
    d_final_g, s_g2 = fsums[0], fsums[2]

    dmods = [None] * depth
    d_n1 = [None] * depth
    d_n2 = [None] * depth
    d_convw = [None] * depth
    d_convb = [None] * depth
    d_scale = [None] * n_pool
    d_grp = [None] * n_pool
    dkvs = [[] for _ in range(NB)]
    exchanged = []
    f_tk = _tile(F, 1408, LANES)
    ct_blocks = DA // _tile(DA, 256, LANES)

    def exchange(name, idx, dw):
        dw4 = dw.reshape(N_CHIPS, -1, dw.shape[-1])
        exchanged.append((name, idx, dw4))
        push_exchange(comm, (name, idx), dw4)

    for l in reversed(range(depth)):
        st = saved[l]
        sh1, sc1, g1, sh2, sc2, g2 = [mod[l, i] for i in range(6)]
        dout2_3 = dout2.reshape(1, S, D)
        exchange("down", l, mm_tn(st["gated"], dout2_3, (1, F, D), tn=D, tk=f_tk, name="ffn_down_dw", comm=comm))
        dhu, s_conv = ffn_act_bwd(dout2, wts["down", l], st["hu"], convw_full[l], vec(ffn_conv_b[l]), "ffn_act_bwd", comm=comm)
        dh2 = mm_nt(dhu, wts["up", l], tn=up_tn, tk=D, out_dtype=F32, name="ffn_up_bwd", gmap=up_gmap, comm=comm)
        exchange("up", l, mm_tn(st["h2"], dhu, (N_CHIPS, D, up_ns), tn=up_tn, tk=D, name="ffn_up_dw", gmap=up_gmap, comm=comm))
        dx, s_n2, dout1 = norm_mod_bwd(dh2, st["x1"], dx, vec(norm2_g[l]), sc2, "norm_mod_bwd_gate", below=(g1, st["out1"]), comm=comm)
        d_convw[l], d_convb[l] = s_conv[0:3], s_conv[3]
        d_n2[l] = s_n2[2]
        dout1_3 = dout1.reshape(1, S, D)
        if l < n_pool:
            dys = mm_nt(dout1_3, wts["pout", l], tn=D, tk=D // 2, out_dtype=F32, name="pool_out_bwd", comm=comm)
            exchange("pout", l, mm_tn(st["ys"], dout1_3, (1, D, D), tn=D, tk=D, name="pool_out_dw", comm=comm))
            du, d_grp, s_sc = pool_bwd(dys, st["z"], st["pooled"], wts["pgrp", l], vec(scale_full[l]), "pool_mix_bwd")
            exchange("pgrp", l, d_grp.astype(BF16).reshape(G, N_CHIPS, C // N_CHIPS, C).transpose(1, 0, 2, 3))
            d_scale[l] = s_sc[0]
            du_3 = du.reshape(1, S, D)
            dh1 = mm_nt(du_3, wts["pin", l], tn=D, tk=D // 2, out_dtype=F32, name="pool_in_bwd", comm=comm)
            exchange("pin", l, mm_tn(st["h1"], du_3, (1, D, D), tn=D, tk=D, name="pool_in_dw", comm=comm))
        else:
            j = l - n_pool
            do = mm_nt(dout1_3, wts["wo", l], tn=D, tk=DA // 2, out_dtype=BF16, name="attn_out_bwd", comm=comm)
            exchange("wo", j, mm_tn(st["o"], dout1_3, (1, DA, D), tn=D, tk=DA, name="attn_out_dw", comm=comm))
            dlt = attn_delta(do, st["o"], "attn_delta")
            dq = None
            for gi in range(NB):
                d = dil[gi]
                do_b, l_b, dl_b = do, st["lse"], dlt
                if d > 1:
                    do_b = permute_rows(do, d, f"permute_do_b{gi}")
                    l_b = permute_rows(st["lse"], d, f"permute_lse_b{gi}")
                    dl_b = permute_rows(dlt, d, f"permute_delta_b{gi}")
                bwd_name = f"attn_bwd_b{gi}"
                if d > 1:
                    dq_b, dkv_b = attn_branch_bwd(st["qs"][gi], kvs[gi], do_b, l_b, dl_b, gi, slopes[gi], bwd_name, comm=comm)
                    dq = unpermute_rows(dq_b, d, f"unpermute_dq_b{gi}", into=dq, total_cols=NB * DA,
                                        colmap=functools.partial(lambda jj, gi: gi * ct_blocks + jj, gi=gi))
                else:
                    dq, dkv_b = attn_branch_bwd(st["qs"][gi], kvs[gi], do_b, l_b, dl_b, gi, slopes[gi], bwd_name,
                                                out_cols=NB * DA, comm=comm)
                dkvs[gi].append(dkv_b)
            dq_3 = dq.reshape(1, S, NB * DA)
            dh1 = mm_nt(dq_3, wts["wq", l], tn=q_bwd_tn, tk=D, out_dtype=F32, name="q_proj_bwd", comm=comm)
            exchange("wq", j, mm_tn(st["h1"], dq_3, (N_CHIPS, D, q_ns), tn=q_bwd_tn, tk=D, name="q_proj_dw", comm=comm))
        below = (mod[l - 1, 5], saved[l - 1]["out2"]) if l > 0 else None
        if l == n_pool or below is None:
            dx, s_n1 = norm_mod_bwd(dh1, st["x0"], dx, vec(norm1_g[l]), sc1, "norm_mod_bwd", comm=comm if l > 0 else None)
        else:
            dx, s_n1, dout2 = norm_mod_bwd(dh1, st["x0"], dx, vec(norm1_g[l]), sc1, "norm_mod_bwd_gate", below=below, comm=comm)
        d_n1[l] = s_n1[2]
        dmods[l] = jnp.stack([s_n1[0], s_n1[1], s_n2[3], s_n2[0], s_n2[1], s_g2])
        if l > 0 and l != n_pool:
            s_g2 = s_n1[3]
        if l == n_pool:
            dkv = None
            for gi in range(NB):
                dkv = unpermute_rows(dkvs[gi], dil[gi], f"unpermute_dkv_b{gi}", into=dkv, total_cols=2 * NB * DA,
                                     colmap=functools.partial(lambda jj, gi: (jj // ct_blocks) * NB * ct_blocks + gi * ct_blocks + jj % ct_blocks,
                                                              gi=gi))
            dkv_3 = dkv.reshape(1, S, 2 * NB * DA)
            dhkv = mm_nt(dkv_3, wts["kv", 0], tn=kv_ns // 2, tk=D, out_dtype=F32, name="kv_proj_bwd", comm=comm)
            exchange("kv", 0, mm_tn(kv_state["hkv"], dkv_3, (N_CHIPS, D, kv_ns), tn=kv_ns // 2, tk=D, name="kv_proj_dw", comm=comm))
            dx, s_kv, dout2 = norm_mod_bwd(dhkv, kv_state["x"], dx, vec(kv_norm_g), kvmod[1], "norm_mod_bwd_gate", below=below, comm=comm)
            s_g2 = s_kv[3]
    grad_x = dx.reshape(1, S, D)

    smalls = [jnp.stack(dmods), jnp.stack([s_kv[0], s_kv[1]]), jnp.stack(d_n1), jnp.stack(d_n2), s_kv[2], jnp.stack(d_convb), d_final_g,
              jnp.stack(d_scale), jnp.stack(d_convw)]
    small_shapes = [s.shape for s in smalls]
    spk, soffs = _pack(smalls)
    srows = spk.shape[0]
    sgot = all_gather8(spk, "gather_small_grads").reshape(N_DEV, srows, PACK_W)
    ssum = sum_rows8(sgot, "sum_small_grads")
    g_mods, g_kvmod, g_n1, g_n2, g_kvn, g_convb, g_fg, g_scale_full, g_convw_full = _unpack(ssum, soffs, small_shapes)
    g_ada_b = g_mods.reshape(depth, 6 * D)
    g_kv_ada_b = g_kvmod.reshape(2 * D)
    g_scale = lax.dynamic_slice(g_scale_full, (0, p_me * Dq), (n_pool, Dq))
    g_convw = lax.dynamic_slice(g_convw_full, (0, 0, p_me * Fs), (depth, 3, Fs))

    small_w = [ada_b, norm1_g, norm2_g, kv_norm_g, kv_ada_b, ffn_conv_b, final_g, pool_scale, ffn_conv_w]
    small_m = [m_ada_b, m_norm1_g, m_norm2_g, m_kv_norm_g, m_kv_ada_b, m_ffn_conv_b, m_final_g, m_pool_scale, m_ffn_conv_w]
    small_v = [v_ada_b, v_norm1_g, v_norm2_g, v_kv_norm_g, v_kv_ada_b, v_ffn_conv_b, v_final_g, v_pool_scale, v_ffn_conv_w]
    small_g = [g_ada_b, g_n1, g_n2, g_kvn, g_kv_ada_b, g_convb, g_fg, g_scale, g_convw]
    sw_shapes = [w.shape for w in small_w]
    pw, woffs = _pack(small_w)
    s_res = adamw(pw, _pack(small_g)[0], _pack(small_m)[0], _pack(small_v)[0], "adamw_small")
    s_g, s_dl, s_m, s_v = [_unpack(r, woffs, sw_shapes) for r in s_res]

    per_dev = sgot.reshape(N_DEV, -1)
    dm_all = per_dev[:, int(soffs[0]):int(soffs[1])].reshape(N_DEV, depth, 6 * D)
    dkvm_all = per_dev[:, int(soffs[1]):int(soffs[2])].reshape(N_DEV, 1, 2 * D)

    def shard_cols(a, ns):
        sl = lax.dynamic_slice_in_dim(a, p_me * ns, ns, axis=2).transpose(1, 0, 2)
        return jnp.concatenate([sl, jnp.zeros_like(sl)], axis=1)

    ada_res = ada_grad_adamw(c16, shard_cols(dm_all, ada_ns), ada_w, m_ada_w, v_ada_w, "ada_grad_adamw")
    kvada_res = ada_grad_adamw(c16, shard_cols(dkvm_all, kvada_ns), kv_ada_w.reshape(1, D, kvada_ns), m_kv_ada_w.reshape(1, D, kvada_ns),
                               v_kv_ada_w.reshape(1, D, kvada_ns), "kv_ada_grad_adamw")
    kvada_res = [r.reshape(D, kvada_ns) for r in kvada_res]

    comm.flush()
    big_names = ["pin", "pgrp", "pout", "kv", "wq", "wo", "up", "down"]
    n_stack = {"pin": n_pool, "pgrp": n_pool, "pout": n_pool, "kv": 1, "wq": n_attn, "wo": n_attn, "up": depth, "down": depth}
    gsum = {nm: None for nm in big_names}
    for nm, idx, dw4 in exchanged:
        gsum[nm] = sum_partials(dw4, comm.store[nm, idx], gsum[nm], idx, n_stack[nm], pos, "sum_partials")
    for nm in big_names:
        comm.push(_Phase(("swap", nm), ("swap", nm), None, ("swap", nm), 0.0, [], None, gsum[nm].shape[0], 0, _build_swap, buffer=gsum[nm]))
    comm.flush()
    gsum = {nm: comm.store["swap", nm] for nm in big_names}
    big_m = [m_pool_w_in, m_pool_w_grp, m_pool_w_out, m_w_kv, m_attn_w_q, m_attn_w_o, m_ffn_w_up, m_ffn_w_down]
    big_v = [v_pool_w_in, v_pool_w_grp, v_pool_w_out, v_w_kv, v_attn_w_q, v_attn_w_o, v_ffn_w_up, v_ffn_w_down]
    big_w = [pool_w_in, pool_w_grp, pool_w_out, w_kv, attn_w_q, attn_w_o, ffn_w_up, ffn_w_down]
    big_res = []
    for nm, w, m_, v_ in zip(big_names, big_w, big_m, big_v):
        cols = gsum[nm].shape[-1]
        res = adamw(w.reshape(-1, cols), gsum[nm].reshape(-1, cols), m_.reshape(-1, cols), v_.reshape(-1, cols), "adamw_big")
        big_res.append([r.reshape(w.shape) for r in res])

    order = ["ada_w", "ada_b", "norm1_g", "norm2_g", "pool_w_in", "pool_w_grp", "pool_scale", "pool_w_out", "kv_norm_g", "kv_ada_w",
             "kv_ada_b", "w_kv", "attn_w_q", "attn_w_o", "ffn_w_up", "ffn_conv_w", "ffn_conv_b", "ffn_w_down", "final_g"]
    small_names = ["ada_b", "norm1_g", "norm2_g", "kv_norm_g", "kv_ada_b", "ffn_conv_b", "final_g", "pool_scale", "ffn_conv_w"]
    results = {"ada_w": ada_res, "kv_ada_w": kvada_res}
    for i, nm in enumerate(small_names):
        results[nm] = [s_g[i], s_dl[i], s_m[i], s_v[i]]
    for i, nm in enumerate(["pool_w_in", "pool_w_grp", "pool_w_out", "w_kv", "attn_w_q", "attn_w_o", "ffn_w_up", "ffn_w_down"]):
        results[nm] = big_res[i]
    outs = [loss, grad_x]
    for kind in range(4):
        outs += [results[nm][kind] for nm in order]
    return tuple(outs)
```

```python
import functools
import math

import numpy as np
import jax
import jax.numpy as jnp
from jax import lax
from jax.experimental import pallas as pl
from jax.experimental.pallas import tpu as pltpu

F32 = jnp.float32
BF16 = jnp.bfloat16
MESH = pl.DeviceIdType.MESH

POOL_WINDOWS = (2, 4, 8, 16)
BRANCHES = ((128, 1), (512, 4), (2048, 16))
HEAD_DIM = 64
ATTN_BLOCK = 128
EPS = 1e-6
LR, B1, B2, ADAM_EPS, WD, STEP = 0.001, 0.9, 0.999, 1e-08, 0.01, 10

VMEM_LIMIT_BYTES = 56 * 1024 * 1024
LANES = 128
PACK_W = 1024
HALO = 16
NEG = -1e30
N_CHIPS = 4
N_DEV = 8


def _alibi_slopes(n):
    def pow2(m):
        start = 2.0 ** (-(2.0 ** -(math.log2(m) - 3)))
        return [start ** (i + 1) for i in range(m)]
    if math.log2(n).is_integer():
        s = pow2(n)
    else:
        c = 2 ** math.floor(math.log2(n))
        s = pow2(c) + pow2(2 * c)[0::2][: n - c]
    s = np.asarray(s, dtype=np.float32)
    return -np.sort(-s)


def _cparams(sem=None):
    return pltpu.CompilerParams(dimension_semantics=sem, vmem_limit_bytes=VMEM_LIMIT_BYTES)


def _tile(n, pref, unit):
    t = (min(pref, n) // unit) * unit
    while t >= unit:
        if n % t == 0:
            return t
        t -= unit
    return n


def _sigmoid(v):
    return 1.0 / (1.0 + jnp.exp(-v))


def all_gather8(xs, name):
    m_per, n = xs.shape

    def body(x_ref, out_ref, send_sems, recv_sems, local_sem):
        x, y, c = lax.axis_index("x"), lax.axis_index("y"), lax.axis_index("c")
        me, sibling = (x, y, c), (x, y, 1 - c)
        chips = [(1 - x, y), (x, 1 - y), (1 - x, 1 - y)]

        def rows(px, py, pc):
            return out_ref.at[pl.ds((4 * px + 2 * py + pc) * m_per, m_per), :]

        def copy(k, block, to, src=None):
            return pltpu.make_async_remote_copy(src_ref=rows(*block) if src is None else src, dst_ref=rows(*block),
                                                send_sem=send_sems.at[k], recv_sem=recv_sems.at[k], device_id=to, device_id_type=MESH)

        mine = pltpu.make_async_copy(x_ref, rows(*me), local_sem)
        mine.start()
        first = [copy(0, me, sibling, src=x_ref)]
        first += [copy(1 + j, me, (*chip, c), src=x_ref) for j, chip in enumerate(chips)]
        for cp in first:
            cp.start()
        passed = [copy(4 + j, (*chip, c), sibling) for j, chip in enumerate(chips)]
        for j, chip in enumerate(chips):
            copy(1 + j, (*chip, c), me).wait_recv()
            passed[j].start()
        copy(0, sibling, me).wait_recv()
        for j, chip in enumerate(chips):
            copy(4 + j, (*chip, 1 - c), me).wait_recv()
        for cp in first + passed:
            cp.wait_send()
        mine.wait()

    return pl.pallas_call(
        body, name=name,
        out_shape=jax.ShapeDtypeStruct((N_DEV * m_per, n), xs.dtype),
        in_specs=[pl.BlockSpec(memory_space=pltpu.VMEM)],
        out_specs=pl.BlockSpec(memory_space=pltpu.VMEM),
        scratch_shapes=[pltpu.SemaphoreType.DMA((7,)), pltpu.SemaphoreType.DMA((7,)), pltpu.SemaphoreType.DMA],
        compiler_params=pltpu.CompilerParams(vmem_limit_bytes=VMEM_LIMIT_BYTES),
    )(xs)


HBM_SPEC = pl.BlockSpec(memory_space=pltpu.HBM)


def _mesh_pos():
    x, y, c = lax.axis_index("x"), lax.axis_index("y"), lax.axis_index("c")
    return x, y, c, [(1 - x, y), (x, 1 - y), (1 - x, 1 - y)]


class _Phase:
    def __init__(self, key, group, after, owner, est_us, ins, out_shape, n_sems, n_local, build, buffer=None):
        self.key, self.group, self.after, self.owner, self.est_us = key, group, after, owner, est_us
        self.ins, self.out_shape, self.buffer = ins, out_shape, buffer
        self.n_sems, self.n_local, self.build = n_sems, n_local, build


def _rcopy(src, dst, send_sems, recv_sems, k, to):
    return pltpu.make_async_remote_copy(src_ref=src, dst_ref=dst, send_sem=send_sems.at[k], recv_sem=recv_sems.at[k],
                                        device_id=to, device_id_type=MESH)


def _build_fetch(in_refs, g, send_sems, recv_sems, loc_sems, sem0, loc0, rows, whole):
    (shard,) = in_refs
    x, y, c, chips = _mesh_pos()
    p_me = 2 * x + y
    locs = [pltpu.make_async_copy(shard.at[i], g.at[p_me, i], loc_sems.at[loc0 + i]) for i in range(2)] if whole else []
    sends = [_rcopy(shard.at[c, rows], g.at[p_me, c, rows], send_sems, recv_sems, sem0 + j, (*chip, c)) for j, chip in enumerate(chips)]

    def recvs():
        blks = [g.at[2 * chip[0] + chip[1], c, rows] for chip in chips]
        return [_rcopy(blk, blk, send_sems, recv_sems, sem0 + j, (*chip, c)) for j, (blk, chip) in enumerate(zip(blks, chips))]
    return sends, recvs, locs


def _build_pass(in_refs, g, send_sems, recv_sems, loc_sems, sem0, loc0, rows):
    x, y, c, chips = _mesh_pos()
    sib = (x, y, 1 - c)
    slots = [2 * chip[0] + chip[1] for chip in chips]
    sends = [_rcopy(g.at[p, c, rows], g.at[p, c, rows], send_sems, recv_sems, sem0 + j, sib) for j, p in enumerate(slots)]

    def recvs():
        return [_rcopy(g.at[p, 1 - c, rows], g.at[p, 1 - c, rows], send_sems, recv_sems, sem0 + j, sib) for j, p in enumerate(slots)]
    return sends, recvs, []


def _build_exchange(in_refs, recv, send_sems, recv_sems, loc_sems, sem0, loc0, rows):
    (dw,) = in_refs
    x, y, c, chips = _mesh_pos()
    targets = [(c, chip, c, j) for j, chip in enumerate(chips)]
    targets += [(1 - c, chip, 1 - c, 3 + j) for j, chip in enumerate([(x, y)] + chips)]
    sends = [_rcopy(dw.at[2 * chip[0] + chip[1], half, rows], recv.at[rel, rows], send_sems, recv_sems, sem0 + rel, (*chip, core))
             for half, chip, core, rel in targets]

    def recvs():
        return [_rcopy(recv.at[rel, rows], recv.at[rel, rows], send_sems, recv_sems, sem0 + rel, (x, y, 1 - c)) for rel in range(7)]
    return sends, recvs, []


def _build_swap(in_refs, g, send_sems, recv_sems, loc_sems, sem0, loc0):
    x, y, c, _ = _mesh_pos()
    sib = (x, y, 1 - c)
    sends = [_rcopy(g.at[l, c], g.at[l, c], send_sems, recv_sems, sem0 + l, sib) for l in range(g.shape[0])]

    def recvs():
        return [_rcopy(g.at[l, 1 - c], g.at[l, 1 - c], send_sems, recv_sems, sem0 + l, sib) for l in range(g.shape[0])]
    return sends, recvs, []


def _plan_refs(phases, store):
    xin, xout, alias, n_sems, n_loc, out_of = [], [], {}, 0, 0, {}
    for ph in phases:
        ph.sem0, ph.loc0 = n_sems, n_loc
        n_sems += ph.n_sems
        n_loc += ph.n_local
        ph.in0, ph.n_in = len(xin), len(ph.ins)
        xin += ph.ins
        if ph.owner not in out_of:
            out_of[ph.owner] = len(xout)
            if ph.owner == ph.key and ph.buffer is None:
                xout.append(ph.out_shape)
            else:
                buf = ph.buffer if ph.buffer is not None else store[ph.group]
                alias[len(xin)] = len(xout)
                xin.append(buf)
                xout.append(jax.ShapeDtypeStruct(buf.shape, buf.dtype))
        ph.out0 = out_of[ph.owner]
    return xin, xout, alias, max(n_sems, 1), max(n_loc, 1)


def _built(ph, xin_refs, xout_refs, sems):
    return ph.build(xin_refs[ph.in0:ph.in0 + ph.n_in], xout_refs[ph.out0], sems[0], sems[1], sems[2], ph.sem0, ph.loc0)


def _start(phases, xin_refs, xout_refs, sems):
    for ph in phases:
        sends, _, locs = _built(ph, xin_refs, xout_refs, sems)
        for cp in locs + sends:
            cp.start()


def _finish(phases, xin_refs, xout_refs, sems):
    for ph in phases:
        sends, recvs, locs = _built(ph, xin_refs, xout_refs, sems)
        for cp in recvs():
            cp.wait_recv()
        for cp in sends:
            cp.wait_send()
        for cp in locs:
            cp.wait()


class _Comm:
    def __init__(self):
        self.queue, self.store, self.n_alone = [], {}, 0

    def push(self, ph):
        self.queue.append(ph)

    def take(self, carry_us):
        taken, t = [], 0.0
        while True:
            queued = {ph.key for ph in self.queue}
            pending = queued | {ph.key for ph in taken}
            room = 1.5 * carry_us if not taken else carry_us - t
            fits = [ph for ph in self.queue if ph.after not in pending and (ph.owner == ph.key or ph.owner not in queued)
                    and ph.est_us <= room]
            if not fits:
                return taken
            ph = max(fits, key=lambda p: p.est_us)
            self.queue.remove(ph)
            taken.append(ph)
            t += ph.est_us

    def require(self, group):
        phases = [ph for ph in self.queue if ph.group == group]
        if phases:
            self.queue = [ph for ph in self.queue if ph.group != group]
            self.run_alone(phases)
        return self.store[group]

    def flush(self):
        phases, self.queue = self.queue, []
        if phases:
            self.run_alone(phases)

    def run_alone(self, phases):
        phases = [ph for ph in phases if ph.after is None] + [ph for ph in phases if ph.after is not None]
        groups, keys = [[]], set()
        for ph in phases:
            if ph.after in keys:
                groups.append([])
                keys = set()
            groups[-1].append(ph)
            keys.add(ph.key)
        xin, xout, alias, n_sems, n_loc = _plan_refs(phases, self.store)
        n_xin, n_xout = len(xin), len(xout)

        def body(*refs):
            xin_refs, xout_refs, sems = refs[:n_xin], refs[n_xin:n_xin + n_xout], refs[n_xin + n_xout:]
            for grp in groups:
                _start(grp, xin_refs, xout_refs, sems)
                _finish(grp, xin_refs, xout_refs, sems)

        self.n_alone += 1
        outs = pl.pallas_call(
            body, name=f"comm_alone_{self.n_alone}", out_shape=xout, in_specs=[HBM_SPEC] * n_xin, out_specs=[HBM_SPEC] * n_xout,
            input_output_aliases=alias,
            scratch_shapes=[pltpu.SemaphoreType.DMA((n_sems,)), pltpu.SemaphoreType.DMA((n_sems,)), pltpu.SemaphoreType.DMA((n_loc,))],
        )(*xin)
        for ph in phases:
            self.store[ph.group] = outs[ph.out0]


def _pcall(body, *, name, grid, in_specs, out_specs, out_shape, args, scratch_shapes=(), aliases=None, comm=None, carry_us=0.0):
    phases = comm.take(carry_us) if comm is not None else []
    n_in, n_out, n_scr = len(in_specs), len(out_specs), len(scratch_shapes)
    if not phases:
        return pl.pallas_call(body, name=name, grid=grid, in_specs=in_specs, out_specs=out_specs, out_shape=out_shape,
                              scratch_shapes=list(scratch_shapes), input_output_aliases=aliases or {},
                              compiler_params=_cparams(("arbitrary",) * len(grid)))(*args)
    xin, xout, xalias, n_sems, n_loc = _plan_refs(phases, comm.store)
    n_xin, n_xout = len(xin), len(xout)
    all_alias = dict(aliases or {})
    all_alias.update({n_in + i: n_out + o for i, o in xalias.items()})

    def carrier(*refs):
        ins, xin_refs = refs[:n_in], refs[n_in:n_in + n_xin]
        outs = refs[n_in + n_xin:n_in + n_xin + n_out]
        xout_refs = refs[n_in + n_xin + n_out:n_in + n_xin + n_out + n_xout]
        rest = refs[n_in + n_xin + n_out + n_xout:]
        scr, sems = rest[:n_scr], rest[n_scr:]
        pids = [pl.program_id(k) for k in range(len(grid))]
        first = functools.reduce(jnp.logical_and, [p == 0 for p in pids])
        last = functools.reduce(jnp.logical_and, [p == n - 1 for p, n in zip(pids, grid)])

        @pl.when(first)
        def _():
            _start(phases, xin_refs, xout_refs, sems)
        body(*ins, *outs, *scr)

        @pl.when(last)
        def _():
            _finish(phases, xin_refs, xout_refs, sems)

    outs = pl.pallas_call(
        carrier, name=name, grid=grid, in_specs=list(in_specs) + [HBM_SPEC] * n_xin, out_specs=list(out_specs) + [HBM_SPEC] * n_xout,
        out_shape=list(out_shape) + xout,
        scratch_shapes=list(scratch_shapes) + [pltpu.SemaphoreType.DMA((n_sems,)), pltpu.SemaphoreType.DMA((n_sems,)),
                                               pltpu.SemaphoreType.DMA((n_loc,))],
        input_output_aliases=all_alias, compiler_params=_cparams(("arbitrary",) * len(grid)))(*args, *xin)
    for ph in phases:
        comm.store[ph.group] = outs[n_out + ph.out0]
    return outs[:n_out]


FETCH_US_PER_MB = 20.4
PASS_US_PER_MB = 3.3
EXCHANGE_US_PER_MB = 14.5


FETCH_PHASE_US = 35.0
EXCHANGE_PHASE_US = 20.0


def _row_chunks(rows, est_us, phase_us):
    n = 1
    while est_us / n > phase_us and rows % (2 * n) == 0 and (rows // (2 * n)) % 16 == 0:
        n *= 2
    return [pl.ds(k * (rows // n), rows // n) for k in range(n)]


def push_gather(comm, keys_shards):
    prev = []
    for key, shard in keys_shards:
        r, c = shard.shape
        sh = shard.reshape(2, r // 2, c)
        half_mb = r // 2 * c * 2 / 1e6
        chunks = _row_chunks(r // 2, 3 * half_mb * FETCH_US_PER_MB, FETCH_PHASE_US)
        n = len(chunks)
        shape = jax.ShapeDtypeStruct((N_CHIPS, 2, r // 2, c), BF16)
        for k, rows in enumerate(chunks):
            comm.push(_Phase(("fetch", key, k), key, None, ("fetch", key, 0), 3 * half_mb * FETCH_US_PER_MB / n, [sh], shape, 3,
                             2 if k == 0 else 0, functools.partial(_build_fetch, rows=rows, whole=k == 0)))
        for ph in prev:
            comm.push(ph)
        prev = [_Phase(("pass", key, k), key, ("fetch", key, k), ("fetch", key, 0), 3 * half_mb * PASS_US_PER_MB / n + 3.0, [], shape, 3, 0,
                       functools.partial(_build_pass, rows=rows)) for k, rows in enumerate(chunks)]
    for ph in prev:
        comm.push(ph)


def push_exchange(comm, key, dw):
    _, r, c = dw.shape
    half_mb = r // 2 * c * 2 / 1e6
    chunks = _row_chunks(r // 2, 6 * half_mb * EXCHANGE_US_PER_MB, EXCHANGE_PHASE_US)
    dw5 = dw.reshape(N_CHIPS, 2, r // 2, c)
    for k, rows in enumerate(chunks):
        comm.push(_Phase(("exchange", key, k), key, None, ("exchange", key, 0), 6 * half_mb * EXCHANGE_US_PER_MB / len(chunks), [dw5],
                         jax.ShapeDtypeStruct((7, r // 2, c), BF16), 7, 0, functools.partial(_build_exchange, rows=rows)))


MM_FLOPS_PER_US = 6.0e8
MM_ROWS = 1024


def mm_nn(a, w3, *, tn, out_dtype, name, ncb=None, cbmap=None, res=None, perm_d=1, comm=None):
    M, K = a.shape
    P, _, Ns = w3.shape
    nper = Ns // tn
    ncb = P * nper if ncb is None else ncb
    tm = ATTN_BLOCK * perm_d if perm_d > 1 else _tile(M, MM_ROWS, 16)
    cbm = cbmap if cbmap is not None else (lambda j: j)
    nch = tn // LANES

    def body(*refs):
        if res is None:
            a_ref, w_ref, o_ref = refs[:3]
        else:
            a_ref, w_ref, x_ref, g_ref, o_ref, xo_ref = refs
        acc = jnp.dot(a_ref[...].astype(BF16), w_ref[...], preferred_element_type=F32)
        if perm_d > 1:
            scr = refs[3]
            for cj in range(nch):
                scr[cj] = acc[:, cj * LANES:(cj + 1) * LANES]
            for r in range(perm_d):
                for cj in range(nch):
                    o_ref[r, :, cj * LANES:(cj + 1) * LANES] = scr.at[cj][pl.ds(r, ATTN_BLOCK, stride=perm_d), :].astype(o_ref.dtype)
        else:
            o_ref[...] = acc.astype(o_ref.dtype)
        if res is not None:
            xo_ref[...] = x_ref[...] + g_ref[...] * acc

    in_specs = [pl.BlockSpec((tm, K), lambda i, j: (i, 0)),
                pl.BlockSpec((None, K, tn), lambda i, j: (cbm(j) // nper, 0, cbm(j) % nper))]
    scratch = []
    if perm_d > 1:
        out_specs = [pl.BlockSpec((perm_d, ATTN_BLOCK, tn), lambda i, j: (0, i, j))]
        out_shape = [jax.ShapeDtypeStruct((perm_d, M // perm_d, ncb * tn), out_dtype)]
        scratch = [pltpu.VMEM((nch, tm, LANES), F32)]
    else:
        out_specs = [pl.BlockSpec((tm, tn), lambda i, j: (i, j))]
        out_shape = [jax.ShapeDtypeStruct((M, ncb * tn), out_dtype)]
    args = [a, w3]
    if res is not None:
        in_specs += [pl.BlockSpec((tm, tn), lambda i, j: (i, j)), pl.BlockSpec((1, tn), lambda i, j: (0, j))]
        out_specs.append(pl.BlockSpec((tm, tn), lambda i, j: (i, j)))
        out_shape.append(jax.ShapeDtypeStruct((M, ncb * tn), F32))
        args += [res[0], res[1]]
    outs = _pcall(body, name=name, grid=(M // tm, ncb), in_specs=in_specs, out_specs=out_specs, out_shape=out_shape, args=args,
                  scratch_shapes=scratch, comm=comm, carry_us=2.0 * M * K * ncb * tn / MM_FLOPS_PER_US)
    if perm_d > 1:
        return outs[0].reshape(M, ncb * tn)
    return outs[0] if res is None else (outs[0], outs[1])


def permute_rows(x, d, name):
    S, C = x.shape
    R = ATTN_BLOCK * d
    ct = _tile(C, 256, LANES)
    nch = ct // LANES

    def body(x_ref, o_ref, scr):
        xv = x_ref[...].astype(F32)
        for cj in range(nch):
            scr[cj] = xv[:, cj * LANES:(cj + 1) * LANES]
        for r in range(d):
            for cj in range(nch):
                o_ref[r, :, cj * LANES:(cj + 1) * LANES] = scr.at[cj][pl.ds(r, ATTN_BLOCK, stride=d), :].astype(o_ref.dtype)

    out = pl.pallas_call(body, name=name, grid=(S // R, C // ct), in_specs=[pl.BlockSpec((R, ct), lambda i, j: (i, j))],
                         out_specs=pl.BlockSpec((d, ATTN_BLOCK, ct), lambda i, j: (0, i, j)),
                         out_shape=jax.ShapeDtypeStruct((d, S // d, C), x.dtype), scratch_shapes=[pltpu.VMEM((nch, R, LANES), F32)],
                         compiler_params=_cparams(("parallel", "parallel")))(x)
    return out.reshape(S, C)


def unpermute_rows(ps, d, name, into=None, total_cols=None, colmap=None):
    ps = list(ps) if isinstance(ps, (list, tuple)) else [ps]
    n_p = len(ps)
    p = ps[0]
    S, C = p.shape
    rpb = max(ATTN_BLOCK, 512 // d)
    R = rpb * d
    ct = _tile(C, 256, LANES)
    nch = ct // LANES
    total_cols = C if total_cols is None else total_cols
    cm = colmap if colmap is not None else (lambda j: j)

    def body(*refs):
        p_refs, o_ref, scr = refs[:n_p], refs[-2], refs[-1]

        def summed(idx):
            return functools.reduce(lambda a, b: a + b, [r[idx] for r in p_refs])
        if d == 1:
            o_ref[...] = summed(0)
            return
        for r in range(d):
            for cj in range(nch):
                scr.at[cj][pl.ds(r, rpb, stride=d), :] = summed((r, slice(None), slice(cj * LANES, (cj + 1) * LANES))).astype(F32)
        for cj in range(nch):
            o_ref[:, cj * LANES:(cj + 1) * LANES] = scr[cj].astype(o_ref.dtype)

    in_specs = [pl.BlockSpec((d, rpb, ct), lambda i, j: (0, i, j))] * n_p
    args = [a.reshape(d, S // d, C) for a in ps]
    aliases = {}
    if into is not None:
        in_specs.append(pl.BlockSpec(memory_space=pl.ANY))
        args.append(into)
        aliases = {n_p: 0}
    return pl.pallas_call(body, name=name, grid=(S // R, C // ct), in_specs=in_specs,
                          out_specs=pl.BlockSpec((R, ct), lambda i, j: (i, cm(j))),
                          out_shape=jax.ShapeDtypeStruct((S, total_cols), p.dtype), scratch_shapes=[pltpu.VMEM((nch, R, LANES), F32)],
                          input_output_aliases=aliases, compiler_params=_cparams(("parallel", "parallel")))(*args)


def mm_nt(g3, w3, *, tn, tk, out_dtype, name, gmap=None, comm=None):
    _, M, _ = g3.shape
    P, K, Ns = w3.shape
    nper = Ns // tn
    ns = P * nper
    tm = _tile(M, MM_ROWS, 16)
    gm = gmap if gmap is not None else (lambda s: (0, s))

    def body(g_ref, w_ref, o_ref, acc):
        s = pl.program_id(2)

        @pl.when(s == 0)
        def _():
            acc[...] = jnp.zeros_like(acc)
        acc[...] += lax.dot_general(g_ref[...].astype(BF16), w_ref[...], (((1,), (1,)), ((), ())), preferred_element_type=F32)

        @pl.when(s == ns - 1)
        def _():
            o_ref[...] = acc[...].astype(o_ref.dtype)

    return _pcall(
        body, name=name, grid=(M // tm, K // tk, ns),
        in_specs=[pl.BlockSpec((None, tm, tn), lambda i, kj, s: (gm(s)[0], i, gm(s)[1])),
                  pl.BlockSpec((None, tk, tn), lambda i, kj, s: (s // nper, kj, s % nper))],
        out_specs=[pl.BlockSpec((tm, tk), lambda i, kj, s: (i, kj))],
        out_shape=[jax.ShapeDtypeStruct((M, K), out_dtype)], args=[g3, w3],
        scratch_shapes=[pltpu.VMEM((tm, tk), F32)], comm=comm, carry_us=2.0 * M * K * P * Ns / MM_FLOPS_PER_US)[0]


def mm_tn(a, g3, wshape, *, tn, tk, name, gmap=None, comm=None):
    M, K = a.shape
    P, _, Ns = wshape
    nper = Ns // tn
    ns = P * nper
    tm = _tile(M, MM_ROWS, 16)
    nm = M // tm
    gm = gmap if gmap is not None else (lambda s: (0, s))

    def body(a_ref, g_ref, o_ref, acc):
        mi = pl.program_id(2)

        @pl.when(mi == 0)
        def _():
            acc[...] = jnp.zeros_like(acc)
        acc[...] += lax.dot_general(a_ref[...].astype(BF16), g_ref[...].astype(BF16), (((0,), (0,)), ((), ())), preferred_element_type=F32)

        @pl.when(mi == nm - 1)
        def _():
            o_ref[...] = acc[...].astype(o_ref.dtype)

    return _pcall(
        body, name=name, grid=(ns, K // tk, nm),
        in_specs=[pl.BlockSpec((tm, tk), lambda s, kj, mi: (mi, kj)),
                  pl.BlockSpec((None, tm, tn), lambda s, kj, mi: (gm(s)[0], mi, gm(s)[1]))],
        out_specs=[pl.BlockSpec((None, tk, tn), lambda s, kj, mi: (s // nper, kj, s % nper))],
        out_shape=[jax.ShapeDtypeStruct((P, K, Ns), BF16)], args=[a, g3],
        scratch_shapes=[pltpu.VMEM((tk, tn), F32)], comm=comm, carry_us=2.0 * M * K * P * Ns / MM_FLOPS_PER_US)[0]


def _vspec(d):
    return pl.BlockSpec((1, d), lambda i: (0, 0))


NORM_US_PER_ELEM = 12.0 / (4096 * 1024)


def norm_mod(x, g, sh, sc, name, comm=None):
    S, D = x.shape
    tm = _tile(S, 512, 16)

    def body(x_ref, g_ref, sh_ref, sc_ref, o_ref):
        xv = x_ref[...]
        r = lax.rsqrt(jnp.mean(xv * xv, axis=-1, keepdims=True) + EPS)
        o_ref[...] = ((xv * r) * g_ref[...] * (1.0 + sc_ref[...]) + sh_ref[...]).astype(o_ref.dtype)

    return _pcall(body, name=name, grid=(S // tm,),
                  in_specs=[pl.BlockSpec((tm, D), lambda i: (i, 0)), _vspec(D), _vspec(D), _vspec(D)],
                  out_specs=[pl.BlockSpec((tm, D), lambda i: (i, 0))], out_shape=[jax.ShapeDtypeStruct((S, D), BF16)],
                  args=[x, g, sh, sc], comm=comm, carry_us=NORM_US_PER_ELEM * S * D)[0]


def _gate_outputs(dx, gate_ref, out_ref, dout_ref):
    dout_ref[...] = (gate_ref[...] * dx).astype(dout_ref.dtype)
    return jnp.sum(dx * out_ref[...].astype(F32), axis=0, keepdims=True)


NORM_BWD_US_PER_ELEM = 28.0 / (4096 * 1024)


def norm_mod_bwd(dh, x, dres, g, sc, name, below=None, comm=None):
    S, D = x.shape
    tm = _tile(S, 256, 16)

    def body(dh_ref, x_ref, dr_ref, g_ref, sc_ref, *rest):
        dx_ref, sums_ref = (rest[2], rest[3]) if below is not None else (rest[0], rest[1])
        xv = x_ref[...]
        dhv = dh_ref[...].astype(F32)
        r = lax.rsqrt(jnp.mean(xv * xv, axis=-1, keepdims=True) + EPS)
        xn = xv * r
        one_sc = 1.0 + sc_ref[...]
        dxn = dhv * g_ref[...] * one_sc
        dx = r * (dxn - xn * jnp.mean(dxn * xn, axis=-1, keepdims=True)) + dr_ref[...]
        dx_ref[...] = dx
        rows = [jnp.sum(dhv, axis=0, keepdims=True), jnp.sum(dhv * xn * g_ref[...], axis=0, keepdims=True),
                jnp.sum(dhv * one_sc * xn, axis=0, keepdims=True)]
        if below is not None:
            rows.append(_gate_outputs(dx, rest[0], rest[1], rest[4]))
        part = jnp.concatenate(rows + [jnp.zeros((8 - len(rows), D), F32)], axis=0)

        @pl.when(pl.program_id(0) == 0)
        def _():
            sums_ref[...] = jnp.zeros_like(sums_ref)
        sums_ref[...] += part

    row = pl.BlockSpec((tm, D), lambda i: (i, 0))
    in_specs, args = [row, row, row, _vspec(D), _vspec(D)], [dh, x, dres, g, sc]
    out_specs = [row, pl.BlockSpec((8, D), lambda i: (0, 0))]
    out_shape = [jax.ShapeDtypeStruct((S, D), F32), jax.ShapeDtypeStruct((8, D), F32)]
    if below is not None:
        in_specs += [_vspec(D), row]
        args += [below[0], below[1]]
        out_specs.append(row)
        out_shape.append(jax.ShapeDtypeStruct((S, D), BF16))
    return _pcall(body, name=name, grid=(S // tm,), in_specs=in_specs, out_specs=out_specs, out_shape=out_shape, args=args,
                  comm=comm, carry_us=NORM_BWD_US_PER_ELEM * S * D)


def loss_fwd_bwd(x, g, target, below, name):
    S, D = x.shape
    tm = _tile(S, 256, 16)

    def body(x_ref, g_ref, t_ref, gate_ref, out_ref, dx_ref, sums_ref, dout_ref):
        xv = x_ref[...]
        r = lax.rsqrt(jnp.mean(xv * xv, axis=-1, keepdims=True) + EPS)
        xn = xv * r
        err = xn * g_ref[...] - t_ref[...]
        dy = err * (1.0 / D)
        dxn = dy * g_ref[...]
        dx = r * (dxn - xn * jnp.mean(dxn * xn, axis=-1, keepdims=True))
        dx_ref[...] = dx
        part = jnp.concatenate([jnp.sum(dy * xn, axis=0, keepdims=True), jnp.sum(err * err, axis=0, keepdims=True),
                                _gate_outputs(dx, gate_ref, out_ref, dout_ref), jnp.zeros((5, D), F32)], axis=0)

        @pl.when(pl.program_id(0) == 0)
        def _():
            sums_ref[...] = jnp.zeros_like(sums_ref)
        sums_ref[...] += part

    row = pl.BlockSpec((tm, D), lambda i: (i, 0))
    return pl.pallas_call(body, name=name, grid=(S // tm,), in_specs=[row, _vspec(D), row, _vspec(D), row],
                          out_specs=[row, pl.BlockSpec((8, D), lambda i: (0, 0)), row],
                          out_shape=[jax.ShapeDtypeStruct((S, D), F32), jax.ShapeDtypeStruct((8, D), F32), jax.ShapeDtypeStruct((S, D), BF16)],
                          compiler_params=_cparams(("arbitrary",)))(x, g, target, below[0], below[1])


def pool_fwd(u, wgrp, scale, name):
    S, D = u.shape
    G = len(POOL_WINDOWS)
    C = D // G
    tm = _tile(S, 256, 16)
    hb = tm // HALO

    def body(up_ref, uc_ref, w_ref, sc_ref, p_ref, z_ref, y_ref):
        i = pl.program_id(0)
        prev = jnp.where(i > 0, up_ref[...], 0.0)
        ext = jnp.concatenate([prev, uc_ref[...]], axis=0)
        t = i * tm + lax.broadcasted_iota(jnp.int32, (tm, 1), 0)
        for gi, w in enumerate(POOL_WINDOWS):
            cs = slice(gi * C, (gi + 1) * C)
            e = ext[:, cs]
            s, k = e, 1
            while k < w:
                s = s + pltpu.roll(s, k, 0)
                k *= 2
            cnt = jnp.minimum(t + 1, w).astype(F32)
            pooled = (s[HALO:] / cnt - e[HALO:]).astype(BF16)
            p_ref[:, cs] = pooled
            z = jnp.dot(pooled, w_ref[:, gi].reshape(C, C), preferred_element_type=F32)
            z_ref[:, cs] = z.astype(BF16)
            y_ref[:, cs] = (z * sc_ref[:, cs]).astype(BF16)

    row = pl.BlockSpec((tm, D), lambda i: (i, 0))
    return pl.pallas_call(
        body, name=name, grid=(S // tm,),
        in_specs=[pl.BlockSpec((HALO, D), lambda i: (jnp.maximum(i * hb - 1, 0), 0)), row,
                  pl.BlockSpec(wgrp.shape, lambda i: (0, 0, 0, 0)), _vspec(D)],
        out_specs=[row, row, row], out_shape=[jax.ShapeDtypeStruct((S, D), BF16)] * 3,
        compiler_params=_cparams(("parallel",)))(u, u, wgrp, scale)


def pool_bwd(dys, z, pooled, wgrp, scale, name):
    S, D = dys.shape
    G = len(POOL_WINDOWS)
    C = D // G
    tm = _tile(S, 256, 16)
    hb = tm // HALO
    nt = S // tm
    n_ext = tm + HALO

    def body(dc_ref, dn_ref, z_ref, p_ref, w_ref, sc_ref, du_ref, dw_ref, sums_ref):
        i = pl.program_id(0)

        @pl.when(i == 0)
        def _():
            dw_ref[...] = jnp.zeros_like(dw_ref)
            sums_ref[...] = jnp.zeros_like(sums_ref)
        dyc = dc_ref[...].astype(F32)
        nxt = jnp.where(i < nt - 1, dn_ref[...].astype(F32), 0.0)
        ext = jnp.concatenate([dyc, nxt], axis=0)
        sums_ref[...] += jnp.concatenate([jnp.sum(dyc * z_ref[...].astype(F32), axis=0, keepdims=True), jnp.zeros((7, D), F32)], axis=0)
        t = i * tm + lax.broadcasted_iota(jnp.int32, (n_ext, 1), 0)
        for gi, w in enumerate(POOL_WINDOWS):
            cs = slice(gi * C, (gi + 1) * C)
            wg = w_ref[:, gi].reshape(C, C)
            dz = (ext[:, cs] * sc_ref[:, cs]).astype(BF16)
            dpool = lax.dot_general(dz, wg, (((1,), (1,)), ((), ())), preferred_element_type=F32)
            dw_ref[gi] += lax.dot_general(p_ref[:, cs], dz[:tm], (((0,), (0,)), ((), ())), preferred_element_type=F32)
            cnt = jnp.minimum(t + 1, w).astype(F32)
            s, k = dpool / cnt, 1
            while k < w:
                s = s + pltpu.roll(s, n_ext - k, 0)
                k *= 2
            du_ref[:, cs] = (s[:tm] - dpool[:tm]).astype(BF16)

    row = pl.BlockSpec((tm, D), lambda i: (i, 0))
    return pl.pallas_call(
        body, name=name, grid=(nt,),
        in_specs=[row, pl.BlockSpec((HALO, D), lambda i: (jnp.minimum((i + 1) * hb, S // HALO - 1), 0)), row, row,
                  pl.BlockSpec(wgrp.shape, lambda i: (0, 0, 0, 0)), _vspec(D)],
        out_specs=[row, pl.BlockSpec((G, C, C), lambda i: (0, 0, 0)), pl.BlockSpec((8, D), lambda i: (0, 0))],
        out_shape=[jax.ShapeDtypeStruct((S, D), BF16), jax.ShapeDtypeStruct((G, C, C), F32), jax.ShapeDtypeStruct((8, D), F32)],
        compiler_params=_cparams(("arbitrary",)))(dys, dys, z, pooled, wgrp, scale)


FFN_ACT_BWD_US_PER_ELEM = 84.0 / (4096 * 2816)


def ffn_up_act(h, w3, conv_w, conv_b, name, comm=None):
    S, D = h.shape
    P, _, Ns = w3.shape
    nh = P // 2
    tm = _tile(S, MM_ROWS, 16)

    def body(h_ref, w_ref, cw_ref, cb_ref, hu_ref, g_ref, stash, halo):
        i, j = pl.program_id(0), pl.program_id(1)
        acc = jnp.dot(h_ref[...], w_ref[...], preferred_element_type=F32).astype(BF16)
        hu_ref[...] = acc

        @pl.when(j < nh)
        def _():
            stash[j] = acc.astype(F32)

        @pl.when(j >= nh)
        def _():
            c = j - nh
            a = stash[c]
            ext = jnp.concatenate([jnp.where(i > 0, halo[c], 0.0), a], axis=0)
            conv = cb_ref[...] + pltpu.roll(ext, 2, 0) * cw_ref[0:1, :] + pltpu.roll(ext, 1, 0) * cw_ref[1:2, :] + ext * cw_ref[2:3, :]
            conv = conv[HALO:]
            g_ref[...] = (conv * _sigmoid(conv) * acc.astype(F32)).astype(g_ref.dtype)
            halo[c] = a[tm - HALO:]

    def gcol(j):
        return jnp.maximum(j - nh, 0)

    return _pcall(
        body, name=name, grid=(S // tm, P),
        in_specs=[pl.BlockSpec((tm, D), lambda i, j: (i, 0)), pl.BlockSpec((None, D, Ns), lambda i, j: (j, 0, 0)),
                  pl.BlockSpec((3, Ns), lambda i, j: (0, gcol(j))), pl.BlockSpec((1, Ns), lambda i, j: (0, gcol(j)))],
        out_specs=[pl.BlockSpec((tm, Ns), lambda i, j: (i, j)), pl.BlockSpec((tm, Ns), lambda i, j: (i, gcol(j)))],
        out_shape=[jax.ShapeDtypeStruct((S, P * Ns), BF16), jax.ShapeDtypeStruct((S, nh * Ns), BF16)],
        scratch_shapes=[pltpu.VMEM((nh, tm, Ns), F32), pltpu.VMEM((nh, HALO, Ns), F32)],
        args=[h, w3, conv_w, conv_b], comm=comm, carry_us=2.0 * S * D * P * Ns / MM_FLOPS_PER_US)


def ffn_act_bwd(dout, w_down, hu, conv_w, conv_b, name, comm=None):
    S, D = dout.shape
    F = w_down.shape[1]
    tm = _tile(S, 256, 16)
    tn = _tile(F, 1408, LANES)
    nb = F // tn
    hb = tm // HALO
    nt = S // tm
    n_ext = tm + 2 * HALO
    nt_dims = (((1,), (1,)), ((), ()))

    def body(dc_ref, dn_ref, wd_ref, ap_ref, ac_ref, an_ref, vc_ref, vn_ref, w_ref, b_ref, o_ref, sums_ref):
        i = pl.program_id(1)

        @pl.when(i == 0)
        def _():
            sums_ref[...] = jnp.zeros_like(sums_ref)
        zeros = jnp.zeros((HALO, tn), F32)
        not_last = i < nt - 1
        a_ext = jnp.concatenate([jnp.where(i > 0, ap_ref[...].astype(F32), 0.0), ac_ref[...].astype(F32), an_ref[...].astype(F32)], axis=0)
        v_ext = jnp.concatenate([zeros, vc_ref[...].astype(F32), vn_ref[...].astype(F32)], axis=0)
        g_cur = lax.dot_general(dc_ref[...], wd_ref[...], nt_dims, preferred_element_type=F32)
        g_nxt = lax.dot_general(dn_ref[...], wd_ref[...], nt_dims, preferred_element_type=F32)
        g_ext = jnp.concatenate([zeros, g_cur, jnp.where(not_last, g_nxt, 0.0)], axis=0)
        w0, w1, w2 = w_ref[0:1, :], w_ref[1:2, :], w_ref[2:3, :]
        a_m2, a_m1 = pltpu.roll(a_ext, 2, 0), pltpu.roll(a_ext, 1, 0)
        conv = b_ref[...] + a_m2 * w0 + a_m1 * w1 + a_ext * w2
        sig = _sigmoid(conv)
        silu = conv * sig
        dsilu = sig * (1.0 + conv * (1.0 - sig))
        dconv = g_ext * v_ext * dsilu
        da = dconv * w2 + pltpu.roll(dconv, n_ext - 1, 0) * w1 + pltpu.roll(dconv, n_ext - 2, 0) * w0
        cur = slice(HALO, HALO + tm)
        o_ref[0] = da[cur].astype(o_ref.dtype)
        o_ref[1] = (g_ext * silu)[cur].astype(o_ref.dtype)
        dc = dconv[cur]
        part = jnp.concatenate([jnp.sum(dc * a_m2[cur], axis=0, keepdims=True), jnp.sum(dc * a_m1[cur], axis=0, keepdims=True),
                                jnp.sum(dc * a_ext[cur], axis=0, keepdims=True), jnp.sum(dc, axis=0, keepdims=True),
                                jnp.zeros((4, tn), F32)], axis=0)
        sums_ref[...] += part

    def prev(i):
        return jnp.maximum(i * hb - 1, 0)

    def nxt(i):
        return jnp.minimum((i + 1) * hb, S // HALO - 1)

    return _pcall(
        body, name=name, grid=(nb, nt),
        in_specs=[pl.BlockSpec((tm, D), lambda j, i: (i, 0)), pl.BlockSpec((HALO, D), lambda j, i: (nxt(i), 0)),
                  pl.BlockSpec((None, tn, D), lambda j, i: (0, j, 0)),
                  pl.BlockSpec((HALO, tn), lambda j, i: (prev(i), j)), pl.BlockSpec((tm, tn), lambda j, i: (i, j)),
                  pl.BlockSpec((HALO, tn), lambda j, i: (nxt(i), j)),
                  pl.BlockSpec((tm, tn), lambda j, i: (i, j + nb)), pl.BlockSpec((HALO, tn), lambda j, i: (nxt(i), j + nb)),
                  pl.BlockSpec((3, tn), lambda j, i: (0, j)), pl.BlockSpec((1, tn), lambda j, i: (0, j))],
        out_specs=[pl.BlockSpec((2, tm, tn), lambda j, i: (0, i, j)), pl.BlockSpec((8, tn), lambda j, i: (0, j))],
        out_shape=[jax.ShapeDtypeStruct((2, S, F), BF16), jax.ShapeDtypeStruct((8, F), F32)],
        args=[dout, dout, w_down, hu, hu, hu, hu, hu, conv_w, conv_b], comm=comm, carry_us=FFN_ACT_BWD_US_PER_ELEM * S * F)


def _head_expander(n_heads, da):
    e = np.zeros((LANES, da), np.float32)
    for h in range(n_heads):
        e[h, h * HEAD_DIM:(h + 1) * HEAD_DIM] = 1.0
    return jnp.asarray(e, BF16)


def _split_dot(v, e, dims):
    hi = v.astype(BF16)
    lo = (v - hi.astype(F32)).astype(BF16)
    return (lax.dot_general(hi, e, dims, preferred_element_type=F32) + lax.dot_general(lo, e, dims, preferred_element_type=F32))


def _lane_col(tile, h):
    lane = lax.broadcasted_iota(jnp.int32, tile.shape, 1)
    return jnp.sum(jnp.where(lane == h, tile, 0.0), axis=1, keepdims=True)


ATTN_US_PER_ELEM = (80.0 / (4096 * 1024), 230.0 / (4096 * 1024))


def attn_branch_fwd(q, kv, gi, slopes, name, comm=None):
    S, DA = q.shape
    H = DA // HEAD_DIM
    window, d = BRANCHES[gi]
    n_steps = window // d
    blk = ATTN_BLOCK
    assert n_steps == blk and (S // d) % blk == 0
    nbs = S // d // blk
    scale = HEAD_DIM ** -0.5

    def body(q_ref, kp_ref, kc_ref, vp_ref, vc_ref, o_ref, l_ref, s_scr, p_scr):
        jb = pl.program_id(1)
        row = lax.broadcasted_iota(jnp.int32, (blk, 2 * blk), 0)
        col = lax.broadcasted_iota(jnp.int32, (blk, 2 * blk), 1)
        delta = row + blk - col
        valid = (delta >= 0) & (delta <= n_steps) & ((col >= blk) | (jb > 0))
        dist = jnp.where(valid, (delta * d).astype(F32), -NEG)
        lane = lax.broadcasted_iota(jnp.int32, (blk, LANES), 1)
        ltile = jnp.zeros((blk, LANES), F32)
        for h in range(H):
            hs = slice(h * HEAD_DIM, (h + 1) * HEAD_DIM)
            k2 = jnp.concatenate([kp_ref[:, hs], kc_ref[:, hs]], axis=0)
            s_scr[h] = lax.dot_general(q_ref[:, hs], k2, (((1,), (1,)), ((), ())), preferred_element_type=F32)
        for h in range(H):
            s = s_scr[h] * scale - float(slopes[h]) * dist
            m = jnp.max(s, axis=-1, keepdims=True)
            p = jnp.exp(s - m)
            l = jnp.sum(p, axis=-1, keepdims=True)
            p_scr[h] = (p / l).astype(BF16)
            ltile = jnp.where(lane == h, m + jnp.log(l), ltile)
        for h in range(H):
            hs = slice(h * HEAD_DIM, (h + 1) * HEAD_DIM)
            v2 = jnp.concatenate([vp_ref[:, hs], vc_ref[:, hs]], axis=0)
            o_ref[:, hs] = jnp.dot(p_scr[h], v2, preferred_element_type=F32)
        l_ref[...] = ltile

    def cur(width, off):
        return pl.BlockSpec((blk, width), lambda r, jb: (r * nbs + jb, off))

    def prv(width, off):
        return pl.BlockSpec((blk, width), lambda r, jb: (r * nbs + jnp.maximum(jb - 1, 0), off))

    return _pcall(
        body, name=name, grid=(d, nbs),
        in_specs=[cur(DA, 0), prv(DA, 0), cur(DA, 0), prv(DA, 1), cur(DA, 1)],
        out_specs=[cur(DA, 0), cur(LANES, 0)],
        out_shape=[jax.ShapeDtypeStruct((S, DA), F32), jax.ShapeDtypeStruct((S, LANES), F32)],
        scratch_shapes=[pltpu.VMEM((H, blk, 2 * blk), F32), pltpu.VMEM((H, blk, 2 * blk), BF16)],
        args=[q, kv, kv, kv, kv], comm=comm, carry_us=ATTN_US_PER_ELEM[0] * S * DA)


def attn_combine(os_, lses, name):
    S, DA = os_[0].shape
    H = DA // HEAD_DIM
    tm = _tile(S, 256, 16)
    expander = _head_expander(H, DA)
    nbr = len(os_)

    def body(*refs):
        o_refs, l_refs, e_ref = refs[:nbr], refs[nbr:2 * nbr], refs[2 * nbr]
        out_ref, lse_ref = refs[2 * nbr + 1:]
        ls = [r[...] for r in l_refs]
        lmax = functools.reduce(jnp.maximum, ls)
        es = [jnp.exp(l - lmax) for l in ls]
        den = functools.reduce(lambda a, b: a + b, es)
        lse_ref[...] = lmax + jnp.log(den)
        acc = jnp.zeros((tm, DA), F32)
        for e, o_ref in zip(es, o_refs):
            acc = acc + _split_dot(e / den, e_ref[...], (((1,), (0,)), ((), ()))) * o_ref[...]
        out_ref[...] = acc.astype(out_ref.dtype)

    row = pl.BlockSpec((tm, DA), lambda i: (i, 0))
    lrow = pl.BlockSpec((tm, LANES), lambda i: (i, 0))
    return pl.pallas_call(
        body, name=name, grid=(S // tm,),
        in_specs=[row] * nbr + [lrow] * nbr + [pl.BlockSpec((LANES, DA), lambda i: (0, 0))],
        out_specs=[row, lrow], out_shape=[jax.ShapeDtypeStruct((S, DA), BF16), jax.ShapeDtypeStruct((S, LANES), F32)],
        compiler_params=_cparams(("parallel",)))(*os_, *lses, expander)


def attn_delta(do, o, name):
    S, DA = o.shape
    H = DA // HEAD_DIM
    tm = _tile(S, 512, 16)
    expander = _head_expander(H, DA)

    def body(do_ref, o_ref, e_ref, d_ref):
        prod = do_ref[...].astype(F32) * o_ref[...].astype(F32)
        d_ref[...] = _split_dot(prod, e_ref[...], (((1,), (1,)), ((), ())))

    row = pl.BlockSpec((tm, DA), lambda i: (i, 0))
    return pl.pallas_call(body, name=name, grid=(S // tm,), in_specs=[row, row, pl.BlockSpec((LANES, DA), lambda i: (0, 0))],
                          out_specs=pl.BlockSpec((tm, LANES), lambda i: (i, 0)), out_shape=jax.ShapeDtypeStruct((S, LANES), F32),
                          compiler_params=_cparams(("parallel",)))(do, o, expander)


def attn_branch_bwd(q, kv, do, lse, dlt, gi, slopes, name, out_cols=None, comm=None):
    S, DA = q.shape
    H = DA // HEAD_DIM
    window, d = BRANCHES[gi]
    n_steps = window // d
    blk = ATTN_BLOCK
    nbs = S // d // blk
    scale = HEAD_DIM ** -0.5
    nt, tn = (((1,), (1,)), ((), ())), (((0,), (0,)), ((), ()))

    def body(*refs):
        k_ref, v_ref, qc_ref, qn_ref, doc_ref, don_ref, lc_ref, ln_ref, dc_ref, dn_ref = refs[:10]
        dq_ref, dkv_ref, carry, s_scr, dp_scr, p_scr, ds_scr = refs[-7:]
        kb = pl.program_id(1)

        @pl.when(kb == 0)
        def _():
            carry[...] = jnp.zeros_like(carry)
        row = lax.broadcasted_iota(jnp.int32, (2 * blk, blk), 0)
        col = lax.broadcasted_iota(jnp.int32, (2 * blk, blk), 1)
        delta = row - col
        valid = (delta >= 0) & (delta <= n_steps) & ((row < blk) | (kb < nbs - 1))
        dist = jnp.where(valid, (delta * d).astype(F32), -NEG)
        l2 = jnp.concatenate([lc_ref[...], ln_ref[...]], axis=0)
        d2 = jnp.concatenate([dc_ref[...], dn_ref[...]], axis=0)
        for h in range(H):
            hs = slice(h * HEAD_DIM, (h + 1) * HEAD_DIM)
            q2 = jnp.concatenate([qc_ref[:, hs], qn_ref[:, hs]], axis=0)
            do2 = jnp.concatenate([doc_ref[:, hs], don_ref[:, hs]], axis=0)
            s_scr[h] = lax.dot_general(q2, k_ref[:, hs], nt, preferred_element_type=F32)
            dp_scr[h] = lax.dot_general(do2, v_ref[:, hs], nt, preferred_element_type=F32)
        for h in range(H):
            p = jnp.exp(s_scr[h] * scale - float(slopes[h]) * dist - _lane_col(l2, h))
            p_scr[h] = p.astype(BF16)
            ds_scr[h] = (p * (dp_scr[h] - _lane_col(d2, h))).astype(BF16)
        for h in range(H):
            hs = slice(h * HEAD_DIM, (h + 1) * HEAD_DIM)
            vs = slice(DA + h * HEAD_DIM, DA + (h + 1) * HEAD_DIM)
            q2 = jnp.concatenate([qc_ref[:, hs], qn_ref[:, hs]], axis=0)
            do2 = jnp.concatenate([doc_ref[:, hs], don_ref[:, hs]], axis=0)
            dvh = lax.dot_general(p_scr[h], do2, tn, preferred_element_type=F32)
            dkh = lax.dot_general(ds_scr[h], q2, tn, preferred_element_type=F32) * scale
            dq2 = jnp.dot(ds_scr[h], k_ref[:, hs], preferred_element_type=F32) * scale
            dq_ref[:, hs] = (carry[:, hs] + dq2[:blk]).astype(dq_ref.dtype)
            carry[:, hs] = dq2[blk:]
            dkv_ref[:, hs] = dkh
            dkv_ref[:, vs] = dvh

    def cur(width, off):
        return pl.BlockSpec((blk, width), lambda r, kb: (r * nbs + kb, off))

    def nxt(width, off):
        return pl.BlockSpec((blk, width), lambda r, kb: (r * nbs + jnp.minimum(kb + 1, nbs - 1), off))

    in_specs = [cur(DA, 0), cur(DA, 1), cur(DA, 0), nxt(DA, 0), cur(DA, 0), nxt(DA, 0),
                cur(LANES, 0), nxt(LANES, 0), cur(LANES, 0), nxt(LANES, 0)]
    args = [kv, kv, q, q, do, do, lse, lse, dlt, dlt]
    return _pcall(
        body, name=name, grid=(d, nbs), in_specs=in_specs, out_specs=[cur(DA, 0), cur(2 * DA, 0)],
        out_shape=[jax.ShapeDtypeStruct((S, out_cols or DA), BF16), jax.ShapeDtypeStruct((S, 2 * DA), F32)],
        scratch_shapes=[pltpu.VMEM((blk, DA), F32), pltpu.VMEM((H, 2 * blk, blk), F32), pltpu.VMEM((H, 2 * blk, blk), F32),
                        pltpu.VMEM((H, 2 * blk, blk), BF16), pltpu.VMEM((H, 2 * blk, blk), BF16)],
        args=args, comm=comm, carry_us=ATTN_US_PER_ELEM[1] * S * DA)


def ada_project(c16, w3, b3, name):
    L, D, Ns = w3.shape
    tn = _tile(Ns, 512, LANES)

    def body(c_ref, w_ref, b_ref, o_ref):
        cv = c_ref[...]
        cond = (cv * _sigmoid(cv)).astype(BF16)
        o_ref[...] = jnp.dot(cond, w_ref[...].astype(BF16), preferred_element_type=F32) + b_ref[...]

    return pl.pallas_call(
        body, name=name, grid=(L, Ns // tn),
        in_specs=[pl.BlockSpec((16, D), lambda l, j: (0, 0)), pl.BlockSpec((None, D, tn), lambda l, j: (l, 0, j)),
                  pl.BlockSpec((None, 1, tn), lambda l, j: (l, 0, j))],
        out_specs=pl.BlockSpec((None, 16, tn), lambda l, j: (l, 0, j)), out_shape=jax.ShapeDtypeStruct((L, 16, Ns), F32),
        compiler_params=_cparams(("parallel", "parallel")))(c16, w3, b3)


def _adamw(w, g, m, v):
    m = B1 * m + (1.0 - B1) * g
    v = B2 * v + (1.0 - B2) * (g * g)
    m_hat = m / (1.0 - B1 ** STEP)
    v_hat = v / (1.0 - B2 ** STEP)
    delta = -LR * (m_hat / (jnp.sqrt(v_hat) + ADAM_EPS) + WD * w)
    return delta, m, v


def ada_grad_adamw(c16, d3, w3, m3, v3, name):
    L, D, Ns = w3.shape
    tk = _tile(D, 256, 8)

    def body(c_ref, d_ref, w_ref, m_ref, v_ref, g_out, dl_out, m_out, v_out):
        cv = c_ref[...]
        cond = (cv * _sigmoid(cv)).astype(BF16)
        g = lax.dot_general(cond, d_ref[...].astype(BF16), (((0,), (0,)), ((), ())), preferred_element_type=F32)
        g_out[...] = g
        dl_out[...], m_out[...], v_out[...] = _adamw(w_ref[...], g, m_ref[...], v_ref[...])

    wspec = pl.BlockSpec((None, tk, Ns), lambda l, kj: (l, kj, 0))
    return pl.pallas_call(
        body, name=name, grid=(L, D // tk),
        in_specs=[pl.BlockSpec((16, tk), lambda l, kj: (0, kj)), pl.BlockSpec((None, 16, Ns), lambda l, kj: (l, 0, 0)), wspec, wspec, wspec],
        out_specs=[wspec] * 4, out_shape=[jax.ShapeDtypeStruct((L, D, Ns), F32)] * 4,
        compiler_params=_cparams(("parallel", "parallel")))(c16, d3, w3, m3, v3)


def adamw(w, g, m, v, name):
    R, C = w.shape
    tr = _tile(R, 256, 8)

    def body(w_ref, g_ref, m_ref, v_ref, g_out, dl_out, m_out, v_out):
        g = g_ref[...]
        g_out[...] = g
        dl_out[...], m_out[...], v_out[...] = _adamw(w_ref[...], g, m_ref[...], v_ref[...])

    spec = pl.BlockSpec((tr, C), lambda i: (i, 0))
    return pl.pallas_call(body, name=name, grid=(R // tr,), in_specs=[spec] * 4, out_specs=[spec] * 4,
                          out_shape=[jax.ShapeDtypeStruct((R, C), F32)] * 4, compiler_params=_cparams(("parallel",)))(w, g, m, v)


def sum_partials(own, recv, g_prev, layer, n_layers, pos, name):
    _, Rh, C = recv.shape
    tr = _tile(Rh, 256, 16)

    def body(pos_ref, own_ref, recv_ref, *rest):
        acc = own_ref[...].astype(F32)
        for rel in range(7):
            acc = acc + recv_ref[rel].astype(F32)
        rest[-1][...] = acc

    in_specs = [pl.BlockSpec((None, None, tr, C), lambda r, pos: (pos[1], pos[0], r, 0)), pl.BlockSpec((7, tr, C), lambda r, pos: (0, r, 0))]
    args = [pos, own.reshape(N_CHIPS, 2, Rh, C), recv]
    aliases = {}
    if g_prev is not None:
        in_specs.append(pl.BlockSpec(memory_space=pl.ANY))
        args.append(g_prev)
        aliases = {3: 0}
    return pl.pallas_call(
        body, name=name,
        grid_spec=pltpu.PrefetchScalarGridSpec(
            num_scalar_prefetch=1, grid=(Rh // tr,), in_specs=in_specs,
            out_specs=pl.BlockSpec((None, None, tr, C), lambda r, pos: (layer, pos[0], r, 0))),
        out_shape=jax.ShapeDtypeStruct((n_layers, 2, Rh, C), F32), input_output_aliases=aliases,
        compiler_params=_cparams(("parallel",)))(*args)


def sum_rows8(g8, name):
    _, R, C = g8.shape

    def body(g_ref, o_ref):
        acc = g_ref[0]
        for i in range(1, N_DEV):
            acc = acc + g_ref[i]
        o_ref[...] = acc

    return pl.pallas_call(body, name=name, grid=(1,), in_specs=[pl.BlockSpec((N_DEV, R, C), lambda i: (0, 0, 0))],
                          out_specs=pl.BlockSpec((R, C), lambda i: (0, 0)), out_shape=jax.ShapeDtypeStruct((R, C), F32),
                          compiler_params=_cparams(("arbitrary",)))(g8)


def _pack(vecs):
    flat = [v.reshape(-1).astype(F32) for v in vecs]
    sizes = [f.shape[0] for f in flat]
    total = sum(sizes)
    padded = -(-total // (8 * PACK_W)) * (8 * PACK_W)
    buf = jnp.concatenate(flat + [jnp.zeros((padded - total,), F32)])
    offs = np.concatenate([[0], np.cumsum(sizes)])
    return buf.reshape(-1, PACK_W), offs


def _unpack(buf, offs, shapes):
    flat = buf.reshape(-1)
    return [flat[int(offs[i]):int(offs[i + 1])].reshape(s) for i, s in enumerate(shapes)]


def kernel(x, c, ada_w, ada_b, norm1_g, norm2_g, pool_w_in, pool_w_grp, pool_scale, pool_w_out, kv_norm_g, kv_ada_w, kv_ada_b, w_kv, attn_w_q, attn_w_o, ffn_w_up, ffn_conv_w, ffn_conv_b, ffn_w_down, final_g, loss_target, m_ada_w, m_ada_b, m_norm1_g, m_norm2_g, m_pool_w_in, m_pool_w_grp, m_pool_scale, m_pool_w_out, m_kv_norm_g, m_kv_ada_w, m_kv_ada_b, m_w_kv, m_attn_w_q, m_attn_w_o, m_ffn_w_up, m_ffn_conv_w, m_ffn_conv_b, m_ffn_w_down, m_final_g, v_ada_w, v_ada_b, v_norm1_g, v_norm2_g, v_pool_w_in, v_pool_w_grp, v_pool_scale, v_pool_w_out, v_kv_norm_g, v_kv_ada_w, v_kv_ada_b, v_w_kv, v_attn_w_q, v_attn_w_o, v_ffn_w_up, v_ffn_conv_w, v_ffn_conv_b, v_ffn_w_down, v_final_g):
    S, D = x.shape[1], x.shape[2]
    depth = ada_w.shape[0]
    n_pool = pool_w_in.shape[0]
    n_attn = attn_w_q.shape[0]
    G = len(POOL_WINDOWS)
    NB = len(BRANCHES)
    DA = attn_w_o.shape[1] * N_CHIPS
    H = DA // HEAD_DIM
    F = ffn_conv_b.shape[1]
    Fs = F // N_CHIPS
    Dq = D // N_CHIPS
    ada_ns = ada_w.shape[2]
    kvada_ns = kv_ada_w.shape[1]
    slopes = _alibi_slopes(NB * H).reshape(NB, H)

    ix, iy, ic = lax.axis_index("x"), lax.axis_index("y"), lax.axis_index("c")
    p_me = 2 * ix + iy
    b_me = 4 * ix + 2 * iy + ic
    pos = jnp.stack([ic, p_me]).astype(jnp.int32)
    xs, tgt = x[0], loss_target[0]

    pk, offs = _pack([c, pool_scale, ffn_conv_w])
    rows1 = pk.shape[0]
    got = all_gather8(pk, "gather_small_in").reshape(N_DEV, rows1, PACK_W)
    c8 = got.reshape(N_DEV, -1)[:, :D]
    c16 = jnp.concatenate([c8, jnp.zeros_like(c8)], axis=0)
    chip_rows = got[0::2].reshape(N_CHIPS, -1)
    scale_full = chip_rows[:, int(offs[1]):int(offs[2])].reshape(N_CHIPS, n_pool, Dq).transpose(1, 0, 2).reshape(n_pool, D)
    convw_full = chip_rows[:, int(offs[2]):int(offs[3])].reshape(N_CHIPS, depth, 3, Fs).transpose(1, 2, 0, 3).reshape(depth, 3, F)

    ada_b_loc = lax.dynamic_slice(ada_b, (0, p_me * ada_ns), (depth, ada_ns)).reshape(depth, 1, ada_ns)
    kvb_loc = lax.dynamic_slice(kv_ada_b, (p_me * kvada_ns,), (kvada_ns,)).reshape(1, 1, kvada_ns)
    mods_loc = ada_project(c16, ada_w, ada_b_loc, "ada_project")[:, :N_DEV]
    kvmod_loc = ada_project(c16, kv_ada_w.reshape(1, D, kvada_ns), kvb_loc, "kv_ada_project")[0, :N_DEV]
    mods_cat = jnp.concatenate([mods_loc.transpose(1, 0, 2).reshape(N_DEV, depth * ada_ns), kvmod_loc], axis=1)
    mods_all = all_gather8(mods_cat, "gather_mods").reshape(N_CHIPS, 2, N_DEV, -1)
    mine = lax.dynamic_index_in_dim(mods_all[:, 0], b_me, axis=1, keepdims=False)
    mod = mine[:, :depth * ada_ns].reshape(N_CHIPS, depth, ada_ns).transpose(1, 0, 2).reshape(depth, 6, 1, D)
    kvmod = mine[:, depth * ada_ns:].reshape(2, 1, D)

    comm = _Comm()
    C = D // G
    kv_ns, q_ns, up_ns = w_kv.shape[1], attn_w_q.shape[2], ffn_w_up.shape[2]

    def layer_shards(l):
        sh = []
        if l < n_pool:
            sh += [(("pin", l), pool_w_in[l]), (("pgrp", l), pool_w_grp[l].reshape(-1, C)), (("pout", l), pool_w_out[l])]
        else:
            if l == n_pool:
                sh.append((("kv", 0), w_kv))
            sh += [(("wq", l), attn_w_q[l - n_pool]), (("wo", l), attn_w_o[l - n_pool])]
        sh += [(("up", l), ffn_w_up[l]), (("down", l), ffn_w_down[l])]
        return [(k, w.astype(BF16)) for k, w in sh]

    def weight(key, shape):
        return comm.require(key).reshape(shape)

    dil = [d for _, d in BRANCHES]
    kv_tn = DA // 2
    q_tn = DA // 4
    q_bwd_tn = q_ns
    up_tn = up_ns
    up_per_half = F // up_tn

    def up_gmap(s):
        return s // up_per_half, s % up_per_half

    def vec(v):
        return v.reshape(1, -1)

    saved = []
    xcur = xs
    kvs = None
    wts = {}
    push_gather(comm, layer_shards(0))
    for l in range(depth):
        if l + 1 < depth:
            push_gather(comm, layer_shards(l + 1))
        sh1, sc1, g1, sh2, sc2, g2 = [mod[l, i] for i in range(6)]
        st = {"x0": xcur}
        h1 = norm_mod(xcur, vec(norm1_g[l]), sh1, sc1, "norm_mod", comm=comm)
        st["h1"] = h1
        if l < n_pool:
            wts["pin", l] = weight(("pin", l), (1, D, D))
            u = mm_nn(h1, wts["pin", l], tn=D, out_dtype=F32, name="pool_in_proj", comm=comm)
            wts["pgrp", l] = weight(("pgrp", l), (N_CHIPS, G, C // N_CHIPS, C))
            pooled, z, ys = pool_fwd(u, wts["pgrp", l], vec(scale_full[l]), "pool_mix")
            wts["pout", l] = weight(("pout", l), (1, D, D))
            out, x1 = mm_nn(ys, wts["pout", l], tn=D, out_dtype=BF16, name="pool_out_proj", res=(xcur, g1), comm=comm)
            st.update(pooled=pooled, z=z, ys=ys, out1=out)
        else:
            if l == n_pool:
                wts["kv", 0] = weight(("kv", 0), (N_CHIPS, D, kv_ns))
                hkv = norm_mod(xcur, vec(kv_norm_g), kvmod[0], kvmod[1], "norm_mod", comm=comm)
                kvs = [mm_nn(hkv, wts["kv", 0], tn=kv_tn, out_dtype=BF16, name=f"kv_proj_b{gi}", ncb=4, perm_d=dil[gi], comm=comm,
                             cbmap=functools.partial(lambda jj, gi: 2 * gi + (jj // 2) * 2 * NB + jj % 2, gi=gi)) for gi in range(NB)]
                kv_state = {"x": xcur, "hkv": hkv}
            wts["wq", l] = weight(("wq", l), (N_CHIPS, D, q_ns))
            qs, os_, lses = [], [], []
            for gi in range(NB):
                q_b = mm_nn(h1, wts["wq", l], tn=q_tn, out_dtype=BF16, name=f"q_proj_b{gi}", ncb=4, perm_d=dil[gi], comm=comm,
                            cbmap=functools.partial(lambda jj, gi: 4 * gi + jj, gi=gi))
                o_b, l_b = attn_branch_fwd(q_b, kvs[gi], gi, slopes[gi], f"attn_fwd_b{gi}", comm=comm)
                if dil[gi] > 1:
                    o_b = unpermute_rows(o_b, dil[gi], f"unpermute_o_b{gi}")
                    l_b = unpermute_rows(l_b, dil[gi], f"unpermute_lse_b{gi}")
                qs.append(q_b)
                os_.append(o_b)
                lses.append(l_b)
            o, lse = attn_combine(os_, lses, "attn_combine")
            wts["wo", l] = weight(("wo", l), (1, DA, D))
            out, x1 = mm_nn(o, wts["wo", l], tn=D, out_dtype=BF16, name="attn_out_proj", res=(xcur, g1), comm=comm)
            st.update(qs=qs, o=o, lse=lse, out1=out)
        st["x1"] = x1
        h2 = norm_mod(x1, vec(norm2_g[l]), sh2, sc2, "norm_mod", comm=comm)
        wts["up", l] = weight(("up", l), (N_CHIPS, D, up_ns))
        hu, gated = ffn_up_act(h2, wts["up", l], convw_full[l], vec(ffn_conv_b[l]), "ffn_up_act", comm=comm)
        wts["down", l] = weight(("down", l), (1, F, D))
        out2, x2 = mm_nn(gated, wts["down", l], tn=D, out_dtype=BF16, name="ffn_down_proj", res=(x1, g2), comm=comm)
        st.update(h2=h2, hu=hu, gated=gated, out2=out2)
        saved.append(st)
        xcur = x2
    comm.flush()

    dx, fsums, dout2 = loss_fwd_bwd(xcur, vec(final_g), tgt, (mod[depth - 1, 5], saved[depth - 1]["out2"]), "loss_head")
    loss = lax.psum(0.5 * jnp.sum(fsums[1]) / D, ("x", "y", "c"))
    d_final_g, s_g2 = fsums[0], fsums[2]

    dmods = [None] * depth
    d_n1 = [None] * depth
    d_n2 = [None] * depth
    d_convw = [None] * depth
    d_convb = [None] * depth
    d_scale = [None] * n_pool
    d_grp = [None] * n_pool
    dkvs = [[] for _ in range(NB)]
    exchanged = []
    f_tk = _tile(F, 1408, LANES)
    ct_blocks = DA // _tile(DA, 256, LANES)

    def exchange(name, idx, dw):
        dw4 = dw.reshape(N_CHIPS, -1, dw.shape[-1])
        exchanged.append((name, idx, dw4))
        push_exchange(comm, (name, idx), dw4)

    for l in reversed(range(depth)):
        st = saved[l]
        sh1, sc1, g1, sh2, sc2, g2 = [mod[l, i] for i in range(6)]
        dout2_3 = dout2.reshape(1, S, D)
        exchange("down", l, mm_tn(st["gated"], dout2_3, (1, F, D), tn=D, tk=f_tk, name="ffn_down_dw", comm=comm))
        dhu, s_conv = ffn_act_bwd(dout2, wts["down", l], st["hu"], convw_full[l], vec(ffn_conv_b[l]), "ffn_act_bwd", comm=comm)
        dh2 = mm_nt(dhu, wts["up", l], tn=up_tn, tk=D, out_dtype=F32, name="ffn_up_bwd", gmap=up_gmap, comm=comm)
        exchange("up", l, mm_tn(st["h2"], dhu, (N_CHIPS, D, up_ns), tn=up_tn, tk=D, name="ffn_up_dw", gmap=up_gmap, comm=comm))
        dx, s_n2, dout1 = norm_mod_bwd(dh2, st["x1"], dx, vec(norm2_g[l]), sc2, "norm_mod_bwd_gate", below=(g1, st["out1"]), comm=comm)
        d_convw[l], d_convb[l] = s_conv[0:3], s_conv[3]
        d_n2[l] = s_n2[2]
        dout1_3 = dout1.reshape(1, S, D)
        if l < n_pool:
            dys = mm_nt(dout1_3, wts["pout", l], tn=D, tk=D // 2, out_dtype=F32, name="pool_out_bwd", comm=comm)
            exchange("pout", l, mm_tn(st["ys"], dout1_3, (1, D, D), tn=D, tk=D, name="pool_out_dw", comm=comm))
            du, d_grp, s_sc = pool_bwd(dys, st["z"], st["pooled"], wts["pgrp", l], vec(scale_full[l]), "pool_mix_bwd")
            exchange("pgrp", l, d_grp.astype(BF16).reshape(G, N_CHIPS, C // N_CHIPS, C).transpose(1, 0, 2, 3))
            d_scale[l] = s_sc[0]
            du_3 = du.reshape(1, S, D)
            dh1 = mm_nt(du_3, wts["pin", l], tn=D, tk=D // 2, out_dtype=F32, name="pool_in_bwd", comm=comm)
            exchange("pin", l, mm_tn(st["h1"], du_3, (1, D, D), tn=D, tk=D, name="pool_in_dw", comm=comm))
        else:
            j = l - n_pool
            do = mm_nt(dout1_3, wts["wo", l], tn=D, tk=DA // 2, out_dtype=BF16, name="attn_out_bwd", comm=comm)
            exchange("wo", j, mm_tn(st["o"], dout1_3, (1, DA, D), tn=D, tk=DA, name="attn_out_dw", comm=comm))
            dlt = attn_delta(do, st["o"], "attn_delta")
            dq = None
            for gi in range(NB):
                d = dil[gi]
                do_b, l_b, dl_b = do, st["lse"], dlt
                if d > 1:
                    do_b = permute_rows(do, d, f"permute_do_b{gi}")
                    l_b = permute_rows(st["lse"], d, f"permute_lse_b{gi}")
                    dl_b = permute_rows(dlt, d, f"permute_delta_b{gi}")
                bwd_name = f"attn_bwd_b{gi}"
                if d > 1:
                    dq_b, dkv_b = attn_branch_bwd(st["qs"][gi], kvs[gi], do_b, l_b, dl_b, gi, slopes[gi], bwd_name, comm=comm)
                    dq = unpermute_rows(dq_b, d, f"unpermute_dq_b{gi}", into=dq, total_cols=NB * DA,
                                        colmap=functools.partial(lambda jj, gi: gi * ct_blocks + jj, gi=gi))
                else:
                    dq, dkv_b = attn_branch_bwd(st["qs"][gi], kvs[gi], do_b, l_b, dl_b, gi, slopes[gi], bwd_name,
                                                out_cols=NB * DA, comm=comm)
                dkvs[gi].append(dkv_b)
            dq_3 = dq.reshape(1, S, NB * DA)
            dh1 = mm_nt(dq_3, wts["wq", l], tn=q_bwd_tn, tk=D, out_dtype=F32, name="q_proj_bwd", comm=comm)
            exchange("wq", j, mm_tn(st["h1"], dq_3, (N_CHIPS, D, q_ns), tn=q_bwd_tn, tk=D, name="q_proj_dw", comm=comm))
        below = (mod[l - 1, 5], saved[l - 1]["out2"]) if l > 0 else None
        if l == n_pool or below is None:
            dx, s_n1 = norm_mod_bwd(dh1, st["x0"], dx, vec(norm1_g[l]), sc1, "norm_mod_bwd", comm=comm if l > 0 else None)
        else:
            dx, s_n1, dout2 = norm_mod_bwd(dh1, st["x0"], dx, vec(norm1_g[l]), sc1, "norm_mod_bwd_gate", below=below, comm=comm)
        d_n1[l] = s_n1[2]
        dmods[l] = jnp.stack([s_n1[0], s_n1[1], s_n2[3], s_n2[0], s_n2[1], s_g2])
        if l > 0 and l != n_pool:
            s_g2 = s_n1[3]
        if l == n_pool:
            dkv = None
            for gi in range(NB):
                dkv = unpermute_rows(dkvs[gi], dil[gi], f"unpermute_dkv_b{gi}", into=dkv, total_cols=2 * NB * DA,
                                     colmap=functools.partial(lambda jj, gi: (jj // ct_blocks) * NB * ct_blocks + gi * ct_blocks + jj % ct_blocks,
                                                              gi=gi))
            dkv_3 = dkv.reshape(1, S, 2 * NB * DA)
            dhkv = mm_nt(dkv_3, wts["kv", 0], tn=kv_ns // 2, tk=D, out_dtype=F32, name="kv_proj_bwd", comm=comm)
            exchange("kv", 0, mm_tn(kv_state["hkv"], dkv_3, (N_CHIPS, D, kv_ns), tn=kv_ns // 2, tk=D, name="kv_proj_dw", comm=comm))
            dx, s_kv, dout2 = norm_mod_bwd(dhkv, kv_state["x"], dx, vec(kv_norm_g), kvmod[1], "norm_mod_bwd_gate", below=below, comm=comm)
            s_g2 = s_kv[3]
    grad_x = dx.reshape(1, S, D)

    smalls = [jnp.stack(dmods), jnp.stack([s_kv[0], s_kv[1]]), jnp.stack(d_n1), jnp.stack(d_n2), s_kv[2], jnp.stack(d_convb), d_final_g,
              jnp.stack(d_scale), jnp.stack(d_convw)]
    small_shapes = [s.shape for s in smalls]
    spk, soffs = _pack(smalls)
    srows = spk.shape[0]
    sgot = all_gather8(spk, "gather_small_grads").reshape(N_DEV, srows, PACK_W)
    ssum = sum_rows8(sgot, "sum_small_grads")
    g_mods, g_kvmod, g_n1, g_n2, g_kvn, g_convb, g_fg, g_scale_full, g_convw_full = _unpack(ssum, soffs, small_shapes)
    g_ada_b = g_mods.reshape(depth, 6 * D)
    g_kv_ada_b = g_kvmod.reshape(2 * D)
    g_scale = lax.dynamic_slice(g_scale_full, (0, p_me * Dq), (n_pool, Dq))
    g_convw = lax.dynamic_slice(g_convw_full, (0, 0, p_me * Fs), (depth, 3, Fs))

    small_w = [ada_b, norm1_g, norm2_g, kv_norm_g, kv_ada_b, ffn_conv_b, final_g, pool_scale, ffn_conv_w]
    small_m = [m_ada_b, m_norm1_g, m_norm2_g, m_kv_norm_g, m_kv_ada_b, m_ffn_conv_b, m_final_g, m_pool_scale, m_ffn_conv_w]
    small_v = [v_ada_b, v_norm1_g, v_norm2_g, v_kv_norm_g, v_kv_ada_b, v_ffn_conv_b, v_final_g, v_pool_scale, v_ffn_conv_w]
    small_g = [g_ada_b, g_n1, g_n2, g_kvn, g_kv_ada_b, g_convb, g_fg, g_scale, g_convw]
    sw_shapes = [w.shape for w in small_w]
    pw, woffs = _pack(small_w)
    s_res = adamw(pw, _pack(small_g)[0], _pack(small_m)[0], _pack(small_v)[0], "adamw_small")
    s_g, s_dl, s_m, s_v = [_unpack(r, woffs, sw_shapes) for r in s_res]

    per_dev = sgot.reshape(N_DEV, -1)
    dm_all = per_dev[:, int(soffs[0]):int(soffs[1])].reshape(N_DEV, depth, 6 * D)
    dkvm_all = per_dev[:, int(soffs[1]):int(soffs[2])].reshape(N_DEV, 1, 2 * D)

    def shard_cols(a, ns):
        sl = lax.dynamic_slice_in_dim(a, p_me * ns, ns, axis=2).transpose(1, 0, 2)
        return jnp.concatenate([sl, jnp.zeros_like(sl)], axis=1)

    ada_res = ada_grad_adamw(c16, shard_cols(dm_all, ada_ns), ada_w, m_ada_w, v_ada_w, "ada_grad_adamw")
    kvada_res = ada_grad_adamw(c16, shard_cols(dkvm_all, kvada_ns), kv_ada_w.reshape(1, D, kvada_ns), m_kv_ada_w.reshape(1, D, kvada_ns),
                               v_kv_ada_w.reshape(1, D, kvada_ns), "kv_ada_grad_adamw")
    kvada_res = [r.reshape(D, kvada_ns) for r in kvada_res]

    comm.flush()
    big_names = ["pin", "pgrp", "pout", "kv", "wq", "wo", "up", "down"]
    n_stack = {"pin": n_pool, "pgrp": n_pool, "pout": n_pool, "kv": 1, "wq": n_attn, "wo": n_attn, "up": depth, "down": depth}
    gsum = {nm: None for nm in big_names}
    for nm, idx, dw4 in exchanged:
        gsum[nm] = sum_partials(dw4, comm.store[nm, idx], gsum[nm], idx, n_stack[nm], pos, "sum_partials")
    for nm in big_names:
        comm.push(_Phase(("swap", nm), ("swap", nm), None, ("swap", nm), 0.0, [], None, gsum[nm].shape[0], 0, _build_swap, buffer=gsum[nm]))
    comm.flush()
    gsum = {nm: comm.store["swap", nm] for nm in big_names}
    big_m = [m_pool_w_in, m_pool_w_grp, m_pool_w_out, m_w_kv, m_attn_w_q, m_attn_w_o, m_ffn_w_up, m_ffn_w_down]
    big_v = [v_pool_w_in, v_pool_w_grp, v_pool_w_out, v_w_kv, v_attn_w_q, v_attn_w_o, v_ffn_w_up, v_ffn_w_down]
    big_w = [pool_w_in, pool_w_grp, pool_w_out, w_kv, attn_w_q, attn_w_o, ffn_w_up, ffn_w_down]
    big_res = []
    for nm, w, m_, v_ in zip(big_names, big_w, big_m, big_v):
        cols = gsum[nm].shape[-1]
        res = adamw(w.reshape(-1, cols), gsum[nm].reshape(-1, cols), m_.reshape(-1, cols), v_.reshape(-1, cols), "adamw_big")
        big_res.append([r.reshape(w.shape) for r in res])

    order = ["ada_w", "ada_b", "norm1_g", "norm2_g", "pool_w_in", "pool_w_grp", "pool_scale", "pool_w_out", "kv_norm_g", "kv_ada_w",
             "kv_ada_b", "w_kv", "attn_w_q", "attn_w_o", "ffn_w_up", "ffn_conv_w", "ffn_conv_b", "ffn_w_down", "final_g"]
    small_names = ["ada_b", "norm1_g", "norm2_g", "kv_norm_g", "kv_ada_b", "ffn_conv_b", "final_g", "pool_scale", "ffn_conv_w"]
    results = {"ada_w": ada_res, "kv_ada_w": kvada_res}
    for i, nm in enumerate(small_names):
        results[nm] = [s_g[i], s_dl[i], s_m[i], s_v[i]]
    for i, nm in enumerate(["pool_w_in", "pool_w_grp", "pool_w_out", "w_kv", "attn_w_q", "attn_w_o", "ffn_w_up", "ffn_w_down"]):
        results[nm] = big_res[i]
    outs = [loss, grad_x]
    for kind in range(4):
        outs += [results[nm][kind] for nm in order]
    return tuple(outs)
```

```python
import functools
import math

import numpy as np
import jax
import jax.numpy as jnp
from jax import lax
from jax.experimental import pallas as pl
from jax.experimental.pallas import tpu as pltpu

F32 = jnp.float32
BF16 = jnp.bfloat16
MESH = pl.DeviceIdType.MESH

POOL_WINDOWS = (2, 4, 8, 16)
BRANCHES = ((128, 1), (512, 4), (2048, 16))
HEAD_DIM = 64
ATTN_BLOCK = 128
EPS = 1e-6
LR, B1, B2, ADAM_EPS, WD, STEP = 0.001, 0.9, 0.999, 1e-08, 0.01, 10

VMEM_LIMIT_BYTES = 56 * 1024 * 1024
LANES = 128
PACK_W = 1024
HALO = 16
NEG = -1e30
N_CHIPS = 4
N_DEV = 8


def _alibi_slopes(n):
    def pow2(m):
        start = 2.0 ** (-(2.0 ** -(math.log2(m) - 3)))
        return [start ** (i + 1) for i in range(m)]
    if math.log2(n).is_integer():
        s = pow2(n)
    else:
        c = 2 ** math.floor(math.log2(n))
        s = pow2(c) + pow2(2 * c)[0::2][: n - c]
    s = np.asarray(s, dtype=np.float32)
    return -np.sort(-s)


def _cparams(sem=None):
    return pltpu.CompilerParams(dimension_semantics=sem, vmem_limit_bytes=VMEM_LIMIT_BYTES)


def _tile(n, pref, unit):
    t = (min(pref, n) // unit) * unit
    while t >= unit:
        if n % t == 0:
            return t
        t -= unit
    return n


def _sigmoid(v):
    return 1.0 / (1.0 + jnp.exp(-v))


def all_gather8(xs, name):
    m_per, n = xs.shape

    def body(x_ref, out_ref, send_sems, recv_sems, local_sem):
        x, y, c = lax.axis_index("x"), lax.axis_index("y"), lax.axis_index("c")
        me, sibling = (x, y, c), (x, y, 1 - c)
        chips = [(1 - x, y), (x, 1 - y), (1 - x, 1 - y)]

        def rows(px, py, pc):
            return out_ref.at[pl.ds((4 * px + 2 * py + pc) * m_per, m_per), :]

        def copy(k, block, to, src=None):
            return pltpu.make_async_remote_copy(src_ref=rows(*block) if src is None else src, dst_ref=rows(*block),
                                                send_sem=send_sems.at[k], recv_sem=recv_sems.at[k], device_id=to, device_id_type=MESH)

        mine = pltpu.make_async_copy(x_ref, rows(*me), local_sem)
        mine.start()
        first = [copy(0, me, sibling, src=x_ref)]
        first += [copy(1 + j, me, (*chip, c), src=x_ref) for j, chip in enumerate(chips)]
        for cp in first:
            cp.start()
        passed = [copy(4 + j, (*chip, c), sibling) for j, chip in enumerate(chips)]
        for j, chip in enumerate(chips):
            copy(1 + j, (*chip, c), me).wait_recv()
            passed[j].start()
        copy(0, sibling, me).wait_recv()
        for j, chip in enumerate(chips):
            copy(4 + j, (*chip, 1 - c), me).wait_recv()
        for cp in first + passed:
            cp.wait_send()
        mine.wait()

    return pl.pallas_call(
        body, name=name,
        out_shape=jax.ShapeDtypeStruct((N_DEV * m_per, n), xs.dtype),
        in_specs=[pl.BlockSpec(memory_space=pltpu.VMEM)],
        out_specs=pl.BlockSpec(memory_space=pltpu.VMEM),
        scratch_shapes=[pltpu.SemaphoreType.DMA((7,)), pltpu.SemaphoreType.DMA((7,)), pltpu.SemaphoreType.DMA],
        compiler_params=pltpu.CompilerParams(vmem_limit_bytes=VMEM_LIMIT_BYTES),
    )(xs)


HBM_SPEC = pl.BlockSpec(memory_space=pltpu.HBM)


def _mesh_pos():
    x, y, c = lax.axis_index("x"), lax.axis_index("y"), lax.axis_index("c")
    return x, y, c, [(1 - x, y), (x, 1 - y), (1 - x, 1 - y)]


class _Phase:
    def __init__(self, key, group, after, owner, est_us, ins, out_shape, n_sems, n_local, build, buffer=None):
        self.key, self.group, self.after, self.owner, self.est_us = key, group, after, owner, est_us
        self.ins, self.out_shape, self.buffer = ins, out_shape, buffer
        self.n_sems, self.n_local, self.build = n_sems, n_local, build


def _rcopy(src, dst, send_sems, recv_sems, k, to):
    return pltpu.make_async_remote_copy(src_ref=src, dst_ref=dst, send_sem=send_sems.at[k], recv_sem=recv_sems.at[k],
                                        device_id=to, device_id_type=MESH)


def _build_fetch(in_refs, g, send_sems, recv_sems, loc_sems, sem0, loc0, rows, whole):
    (shard,) = in_refs
    x, y, c, chips = _mesh_pos()
    p_me = 2 * x + y
    locs = [pltpu.make_async_copy(shard.at[i], g.at[p_me, i], loc_sems.at[loc0 + i]) for i in range(2)] if whole else []
    sends = [_rcopy(shard.at[c, rows], g.at[p_me, c, rows], send_sems, recv_sems, sem0 + j, (*chip, c)) for j, chip in enumerate(chips)]

    def recvs():
        blks = [g.at[2 * chip[0] + chip[1], c, rows] for chip in chips]
        return [_rcopy(blk, blk, send_sems, recv_sems, sem0 + j, (*chip, c)) for j, (blk, chip) in enumerate(zip(blks, chips))]
    return sends, recvs, locs


def _build_pass(in_refs, g, send_sems, recv_sems, loc_sems, sem0, loc0, rows):
    x, y, c, chips = _mesh_pos()
    sib = (x, y, 1 - c)
    slots = [2 * chip[0] + chip[1] for chip in chips]
    sends = [_rcopy(g.at[p, c, rows], g.at[p, c, rows], send_sems, recv_sems, sem0 + j, sib) for j, p in enumerate(slots)]

    def recvs():
        return [_rcopy(g.at[p, 1 - c, rows], g.at[p, 1 - c, rows], send_sems, recv_sems, sem0 + j, sib) for j, p in enumerate(slots)]
    return sends, recvs, []


def _build_exchange(in_refs, recv, send_sems, recv_sems, loc_sems, sem0, loc0, rows):
    (dw,) = in_refs
    x, y, c, chips = _mesh_pos()
    targets = [(c, chip, c, j) for j, chip in enumerate(chips)]
    targets += [(1 - c, chip, 1 - c, 3 + j) for j, chip in enumerate([(x, y)] + chips)]
    sends = [_rcopy(dw.at[2 * chip[0] + chip[1], half, rows], recv.at[rel, rows], send_sems, recv_sems, sem0 + rel, (*chip, core))
             for half, chip, core, rel in targets]

    def recvs():
        return [_rcopy(recv.at[rel, rows], recv.at[rel, rows], send_sems, recv_sems, sem0 + rel, (x, y, 1 - c)) for rel in range(7)]
    return sends, recvs, []


def _build_swap(in_refs, g, send_sems, recv_sems, loc_sems, sem0, loc0):
    x, y, c, _ = _mesh_pos()
    sib = (x, y, 1 - c)
    sends = [_rcopy(g.at[l, c], g.at[l, c], send_sems, recv_sems, sem0 + l, sib) for l in range(g.shape[0])]

    def recvs():
        return [_rcopy(g.at[l, 1 - c], g.at[l, 1 - c], send_sems, recv_sems, sem0 + l, sib) for l in range(g.shape[0])]
    return sends, recvs, []


def _plan_refs(phases, store):
    xin, xout, alias, n_sems, n_loc, out_of = [], [], {}, 0, 0, {}
    for ph in phases:
        ph.sem0, ph.loc0 = n_sems, n_loc
        n_sems += ph.n_sems
        n_loc += ph.n_local
        ph.in0, ph.n_in = len(xin), len(ph.ins)
        xin += ph.ins
        if ph.owner not in out_of:
            out_of[ph.owner] = len(xout)
            if ph.owner == ph.key and ph.buffer is None:
                xout.append(ph.out_shape)
            else:
                buf = ph.buffer if ph.buffer is not None else store[ph.group]
                alias[len(xin)] = len(xout)
                xin.append(buf)
                xout.append(jax.ShapeDtypeStruct(buf.shape, buf.dtype))
        ph.out0 = out_of[ph.owner]
    return xin, xout, alias, max(n_sems, 1), max(n_loc, 1)


def _built(ph, xin_refs, xout_refs, sems):
    return ph.build(xin_refs[ph.in0:ph.in0 + ph.n_in], xout_refs[ph.out0], sems[0], sems[1], sems[2], ph.sem0, ph.loc0)


def _start(phases, xin_refs, xout_refs, sems):
    for ph in phases:
        sends, _, locs = _built(ph, xin_refs, xout_refs, sems)
        for cp in locs + sends:
            cp.start()


def _finish(phases, xin_refs, xout_refs, sems):
    for ph in phases:
        sends, recvs, locs = _built(ph, xin_refs, xout_refs, sems)
        for cp in recvs():
            cp.wait_recv()
        for cp in sends:
            cp.wait_send()
        for cp in locs:
            cp.wait()


class _Comm:
    def __init__(self):
        self.queue, self.store, self.n_alone = [], {}, 0

    def push(self, ph):
        self.queue.append(ph)

    def take(self, carry_us):
        taken, t = [], 0.0
        while True:
            queued = {ph.key for ph in self.queue}
            pending = queued | {ph.key for ph in taken}
            room = 1.5 * carry_us if not taken else carry_us - t
            fits = [ph for ph in self.queue if ph.after not in pending and (ph.owner == ph.key or ph.owner not in queued)
                    and ph.est_us <= room]
            if not fits:
                return taken
            ph = max(fits, key=lambda p: p.est_us)
            self.queue.remove(ph)
            taken.append(ph)
            t += ph.est_us

    def require(self, group):
        phases = [ph for ph in self.queue if ph.group == group]
        if phases:
            self.queue = [ph for ph in self.queue if ph.group != group]
            self.run_alone(phases)
        return self.store[group]

    def flush(self):
        phases, self.queue = self.queue, []
        if phases:
            self.run_alone(phases)

    def run_alone(self, phases):
        phases = [ph for ph in phases if ph.after is None] + [ph for ph in phases if ph.after is not None]
        groups, keys = [[]], set()
        for ph in phases:
            if ph.after in keys:
                groups.append([])
                keys = set()
            groups[-1].append(ph)
            keys.add(ph.key)
        xin, xout, alias, n_sems, n_loc = _plan_refs(phases, self.store)
        n_xin, n_xout = len(xin), len(xout)

        def body(*refs):
            xin_refs, xout_refs, sems = refs[:n_xin], refs[n_xin:n_xin + n_xout], refs[n_xin + n_xout:]
            for grp in groups:
                _start(grp, xin_refs, xout_refs, sems)
                _finish(grp, xin_refs, xout_refs, sems)

        self.n_alone += 1
        outs = pl.pallas_call(
            body, name=f"comm_alone_{self.n_alone}", out_shape=xout, in_specs=[HBM_SPEC] * n_xin, out_specs=[HBM_SPEC] * n_xout,
            input_output_aliases=alias,
            scratch_shapes=[pltpu.SemaphoreType.DMA((n_sems,)), pltpu.SemaphoreType.DMA((n_sems,)), pltpu.SemaphoreType.DMA((n_loc,))],
        )(*xin)
        for ph in phases:
            self.store[ph.group] = outs[ph.out0]


def _pcall(body, *, name, grid, in_specs, out_specs, out_shape, args, scratch_shapes=(), aliases=None, comm=None, carry_us=0.0):
    phases = comm.take(carry_us) if comm is not None else []
    n_in, n_out, n_scr = len(in_specs), len(out_specs), len(scratch_shapes)
    if not phases:
        return pl.pallas_call(body, name=name, grid=grid, in_specs=in_specs, out_specs=out_specs, out_shape=out_shape,
                              scratch_shapes=list(scratch_shapes), input_output_aliases=aliases or {},
                              compiler_params=_cparams(("arbitrary",) * len(grid)))(*args)
    xin, xout, xalias, n_sems, n_loc = _plan_refs(phases, comm.store)
    n_xin, n_xout = len(xin), len(xout)
    all_alias = dict(aliases or {})
    all_alias.update({n_in + i: n_out + o for i, o in xalias.items()})

    def carrier(*refs):
        ins, xin_refs = refs[:n_in], refs[n_in:n_in + n_xin]
        outs = refs[n_in + n_xin:n_in + n_xin + n_out]
        xout_refs = refs[n_in + n_xin + n_out:n_in + n_xin + n_out + n_xout]
        rest = refs[n_in + n_xin + n_out + n_xout:]
        scr, sems = rest[:n_scr], rest[n_scr:]
        pids = [pl.program_id(k) for k in range(len(grid))]
        first = functools.reduce(jnp.logical_and, [p == 0 for p in pids])
        last = functools.reduce(jnp.logical_and, [p == n - 1 for p, n in zip(pids, grid)])

        @pl.when(first)
        def _():
            _start(phases, xin_refs, xout_refs, sems)
        body(*ins, *outs, *scr)

        @pl.when(last)
        def _():
            _finish(phases, xin_refs, xout_refs, sems)

    outs = pl.pallas_call(
        carrier, name=name, grid=grid, in_specs=list(in_specs) + [HBM_SPEC] * n_xin, out_specs=list(out_specs) + [HBM_SPEC] * n_xout,
        out_shape=list(out_shape) + xout,
        scratch_shapes=list(scratch_shapes) + [pltpu.SemaphoreType.DMA((n_sems,)), pltpu.SemaphoreType.DMA((n_sems,)),
                                               pltpu.SemaphoreType.DMA((n_loc,))],
        input_output_aliases=all_alias, compiler_params=_cparams(("arbitrary",) * len(grid)))(*args, *xin)
    for ph in phases:
        comm.store[ph.group] = outs[n_out + ph.out0]
    return outs[:n_out]


FETCH_US_PER_MB = 20.4
PASS_US_PER_MB = 3.3
EXCHANGE_US_PER_MB = 14.5


FETCH_PHASE_US = 35.0
EXCHANGE_PHASE_US = 20.0


def _row_chunks(rows, est_us, phase_us):
    n = 1
    while est_us / n > phase_us and rows % (2 * n) == 0 and (rows // (2 * n)) % 16 == 0:
        n *= 2
    return [pl.ds(k * (rows // n), rows // n) for k in range(n)]


def push_gather(comm, keys_shards):
    prev = []
    for key, shard in keys_shards:
        r, c = shard.shape
        sh = shard.reshape(2, r // 2, c)
        half_mb = r // 2 * c * 2 / 1e6
        chunks = _row_chunks(r // 2, 3 * half_mb * FETCH_US_PER_MB, FETCH_PHASE_US)
        n = len(chunks)
        shape = jax.ShapeDtypeStruct((N_CHIPS, 2, r // 2, c), BF16)
        for k, rows in enumerate(chunks):
            comm.push(_Phase(("fetch", key, k), key, None, ("fetch", key, 0), 3 * half_mb * FETCH_US_PER_MB / n, [sh], shape, 3,
                             2 if k == 0 else 0, functools.partial(_build_fetch, rows=rows, whole=k == 0)))
        for ph in prev:
            comm.push(ph)
        prev = [_Phase(("pass", key, k), key, ("fetch", key, k), ("fetch", key, 0), 3 * half_mb * PASS_US_PER_MB / n + 3.0, [], shape, 3, 0,
                       functools.partial(_build_pass, rows=rows)) for k, rows in enumerate(chunks)]
    for ph in prev:
        comm.push(ph)


def push_exchange(comm, key, dw):
    _, r, c = dw.shape
    half_mb = r // 2 * c * 2 / 1e6
    chunks = _row_chunks(r // 2, 6 * half_mb * EXCHANGE_US_PER_MB, EXCHANGE_PHASE_US)
    dw5 = dw.reshape(N_CHIPS, 2, r // 2, c)
    for k, rows in enumerate(chunks):
        comm.push(_Phase(("exchange", key, k), key, None, ("exchange", key, 0), 6 * half_mb * EXCHANGE_US_PER_MB / len(chunks), [dw5],
                         jax.ShapeDtypeStruct((7, r // 2, c), BF16), 7, 0, functools.partial(_build_exchange, rows=rows)))


MM_FLOPS_PER_US = 6.0e8
MM_ROWS = 1024
MM_ROWS_WIDE = 2048


def mm_nn(a, w3, *, tn, out_dtype, name, ncb=None, cbmap=None, res=None, perm_d=1, comm=None):
    M, K = a.shape
    P, _, Ns = w3.shape
    nper = Ns // tn
    ncb = P * nper if ncb is None else ncb
    tm = max(ATTN_BLOCK * perm_d, _tile(M, MM_ROWS_WIDE, 16)) if perm_d > 1 else _tile(M, MM_ROWS, 16)
    rpb = tm // perm_d
    cbm = cbmap if cbmap is not None else (lambda j: j)
    nch = tn // LANES

    def body(*refs):
        if res is None:
            a_ref, w_ref, o_ref = refs[:3]
        else:
            a_ref, w_ref, x_ref, g_ref, o_ref, xo_ref = refs
        acc = jnp.dot(a_ref[...].astype(BF16), w_ref[...], preferred_element_type=F32)
        if perm_d > 1:
            scr = refs[3]
            for cj in range(nch):
                scr[cj] = acc[:, cj * LANES:(cj + 1) * LANES]
            for r in range(perm_d):
                for cj in range(nch):
                    o_ref[r, :, cj * LANES:(cj + 1) * LANES] = scr.at[cj][pl.ds(r, rpb, stride=perm_d), :].astype(o_ref.dtype)
        else:
            o_ref[...] = acc.astype(o_ref.dtype)
        if res is not None:
            xo_ref[...] = x_ref[...] + g_ref[...] * acc

    in_specs = [pl.BlockSpec((tm, K), lambda i, j: (i, 0)),
                pl.BlockSpec((None, K, tn), lambda i, j: (cbm(j) // nper, 0, cbm(j) % nper))]
    scratch = []
    if perm_d > 1:
        out_specs = [pl.BlockSpec((perm_d, rpb, tn), lambda i, j: (0, i, j))]
        out_shape = [jax.ShapeDtypeStruct((perm_d, M // perm_d, ncb * tn), out_dtype)]
        scratch = [pltpu.VMEM((nch, tm, LANES), F32)]
    else:
        out_specs = [pl.BlockSpec((tm, tn), lambda i, j: (i, j))]
        out_shape = [jax.ShapeDtypeStruct((M, ncb * tn), out_dtype)]
    args = [a, w3]
    if res is not None:
        in_specs += [pl.BlockSpec((tm, tn), lambda i, j: (i, j)), pl.BlockSpec((1, tn), lambda i, j: (0, j))]
        out_specs.append(pl.BlockSpec((tm, tn), lambda i, j: (i, j)))
        out_shape.append(jax.ShapeDtypeStruct((M, ncb * tn), F32))
        args += [res[0], res[1]]
    outs = _pcall(body, name=name, grid=(M // tm, ncb), in_specs=in_specs, out_specs=out_specs, out_shape=out_shape, args=args,
                  scratch_shapes=scratch, comm=comm, carry_us=2.0 * M * K * ncb * tn / MM_FLOPS_PER_US)
    if perm_d > 1:
        return outs[0].reshape(M, ncb * tn)
    return outs[0] if res is None else (outs[0], outs[1])


def permute_rows(x, d, name):
    S, C = x.shape
    R = ATTN_BLOCK * d
    ct = _tile(C, 256, LANES)
    nch = ct // LANES

    def body(x_ref, o_ref, scr):
        xv = x_ref[...].astype(F32)
        for cj in range(nch):
            scr[cj] = xv[:, cj * LANES:(cj + 1) * LANES]
        for r in range(d):
            for cj in range(nch):
                o_ref[r, :, cj * LANES:(cj + 1) * LANES] = scr.at[cj][pl.ds(r, ATTN_BLOCK, stride=d), :].astype(o_ref.dtype)

    out = pl.pallas_call(body, name=name, grid=(S // R, C // ct), in_specs=[pl.BlockSpec((R, ct), lambda i, j: (i, j))],
                         out_specs=pl.BlockSpec((d, ATTN_BLOCK, ct), lambda i, j: (0, i, j)),
                         out_shape=jax.ShapeDtypeStruct((d, S // d, C), x.dtype), scratch_shapes=[pltpu.VMEM((nch, R, LANES), F32)],
                         compiler_params=_cparams(("parallel", "parallel")))(x)
    return out.reshape(S, C)


def unpermute_rows(ps, d, name, into=None, total_cols=None, colmap=None):
    ps = list(ps) if isinstance(ps, (list, tuple)) else [ps]
    n_p = len(ps)
    p = ps[0]
    S, C = p.shape
    rpb = max(ATTN_BLOCK, 512 // d)
    R = rpb * d
    ct = _tile(C, 256, LANES)
    nch = ct // LANES
    total_cols = C if total_cols is None else total_cols
    cm = colmap if colmap is not None else (lambda j: j)

    def body(*refs):
        p_refs, o_ref, scr = refs[:n_p], refs[-2], refs[-1]

        def summed(idx):
            return functools.reduce(lambda a, b: a + b, [r[idx].astype(F32) for r in p_refs])
        if d == 1:
            o_ref[...] = summed(0).astype(o_ref.dtype)
            return
        for r in range(d):
            for cj in range(nch):
                scr.at[cj][pl.ds(r, rpb, stride=d), :] = summed((r, slice(None), slice(cj * LANES, (cj + 1) * LANES)))
        for cj in range(nch):
            o_ref[:, cj * LANES:(cj + 1) * LANES] = scr[cj].astype(o_ref.dtype)

    in_specs = [pl.BlockSpec((d, rpb, ct), lambda i, j: (0, i, j))] * n_p
    args = [a.reshape(d, S // d, C) for a in ps]
    aliases = {}
    if into is not None:
        in_specs.append(pl.BlockSpec(memory_space=pl.ANY))
        args.append(into)
        aliases = {n_p: 0}
    return pl.pallas_call(body, name=name, grid=(S // R, C // ct), in_specs=in_specs,
                          out_specs=pl.BlockSpec((R, ct), lambda i, j: (i, cm(j))),
                          out_shape=jax.ShapeDtypeStruct((S, total_cols), p.dtype), scratch_shapes=[pltpu.VMEM((nch, R, LANES), F32)],
                          input_output_aliases=aliases, compiler_params=_cparams(("parallel", "parallel")))(*args)


def mm_nt(g3, w3, *, tn, tk, out_dtype, name, gmap=None, comm=None):
    _, M, _ = g3.shape
    P, K, Ns = w3.shape
    nper = Ns // tn
    ns = P * nper
    tm = _tile(M, MM_ROWS_WIDE, 16)
    gm = gmap if gmap is not None else (lambda s: (0, s))

    def body(g_ref, w_ref, o_ref, acc):
        s = pl.program_id(2)

        @pl.when(s == 0)
        def _():
            acc[...] = jnp.zeros_like(acc)
        acc[...] += lax.dot_general(g_ref[...].astype(BF16), w_ref[...], (((1,), (1,)), ((), ())), preferred_element_type=F32)

        @pl.when(s == ns - 1)
        def _():
            o_ref[...] = acc[...].astype(o_ref.dtype)

    return _pcall(
        body, name=name, grid=(M // tm, K // tk, ns),
        in_specs=[pl.BlockSpec((None, tm, tn), lambda i, kj, s: (gm(s)[0], i, gm(s)[1])),
                  pl.BlockSpec((None, tk, tn), lambda i, kj, s: (s // nper, kj, s % nper))],
        out_specs=[pl.BlockSpec((tm, tk), lambda i, kj, s: (i, kj))],
        out_shape=[jax.ShapeDtypeStruct((M, K), out_dtype)], args=[g3, w3],
        scratch_shapes=[pltpu.VMEM((tm, tk), F32)], comm=comm, carry_us=2.0 * M * K * P * Ns / MM_FLOPS_PER_US)[0]


def mm_tn(a, g3, wshape, *, tn, tk, name, gmap=None, comm=None):
    M, K = a.shape
    P, _, Ns = wshape
    nper = Ns // tn
    ns = P * nper
    tm = _tile(M, MM_ROWS_WIDE, 16)
    nm = M // tm
    gm = gmap if gmap is not None else (lambda s: (0, s))

    def body(a_ref, g_ref, o_ref, acc):
        mi = pl.program_id(2)

        @pl.when(mi == 0)
        def _():
            acc[...] = jnp.zeros_like(acc)
        acc[...] += lax.dot_general(a_ref[...].astype(BF16), g_ref[...].astype(BF16), (((0,), (0,)), ((), ())), preferred_element_type=F32)

        @pl.when(mi == nm - 1)
        def _():
            o_ref[...] = acc[...].astype(o_ref.dtype)

    return _pcall(
        body, name=name, grid=(ns, K // tk, nm),
        in_specs=[pl.BlockSpec((tm, tk), lambda s, kj, mi: (mi, kj)),
                  pl.BlockSpec((None, tm, tn), lambda s, kj, mi: (gm(s)[0], mi, gm(s)[1]))],
        out_specs=[pl.BlockSpec((None, tk, tn), lambda s, kj, mi: (s // nper, kj, s % nper))],
        out_shape=[jax.ShapeDtypeStruct((P, K, Ns), BF16)], args=[a, g3],
        scratch_shapes=[pltpu.VMEM((tk, tn), F32)], comm=comm, carry_us=2.0 * M * K * P * Ns / MM_FLOPS_PER_US)[0]


def _vspec(d):
    return pl.BlockSpec((1, d), lambda i: (0, 0))


NORM_US_PER_ELEM = 12.0 / (4096 * 1024)


def norm_mod(x, g, sh, sc, name, comm=None):
    S, D = x.shape
    tm = _tile(S, 512, 16)

    def body(x_ref, g_ref, sh_ref, sc_ref, o_ref):
        xv = x_ref[...]
        r = lax.rsqrt(jnp.mean(xv * xv, axis=-1, keepdims=True) + EPS)
        o_ref[...] = ((xv * r) * g_ref[...] * (1.0 + sc_ref[...]) + sh_ref[...]).astype(o_ref.dtype)

    return _pcall(body, name=name, grid=(S // tm,),
                  in_specs=[pl.BlockSpec((tm, D), lambda i: (i, 0)), _vspec(D), _vspec(D), _vspec(D)],
                  out_specs=[pl.BlockSpec((tm, D), lambda i: (i, 0))], out_shape=[jax.ShapeDtypeStruct((S, D), BF16)],
                  args=[x, g, sh, sc], comm=comm, carry_us=NORM_US_PER_ELEM * S * D)[0]


def _gate_outputs(dx, gate_ref, out_ref, dout_ref):
    dout_ref[...] = (gate_ref[...] * dx).astype(dout_ref.dtype)
    return jnp.sum(dx * out_ref[...].astype(F32), axis=0, keepdims=True)


NORM_BWD_US_PER_ELEM = 28.0 / (4096 * 1024)


def norm_mod_bwd(dh, x, dres, g, sc, name, below=None, comm=None):
    S, D = x.shape
    tm = _tile(S, 256, 16)

    def body(dh_ref, x_ref, dr_ref, g_ref, sc_ref, *rest):
        dx_ref, sums_ref = (rest[2], rest[3]) if below is not None else (rest[0], rest[1])
        xv = x_ref[...]
        dhv = dh_ref[...].astype(F32)
        r = lax.rsqrt(jnp.mean(xv * xv, axis=-1, keepdims=True) + EPS)
        xn = xv * r
        one_sc = 1.0 + sc_ref[...]
        dxn = dhv * g_ref[...] * one_sc
        dx = r * (dxn - xn * jnp.mean(dxn * xn, axis=-1, keepdims=True)) + dr_ref[...]
        dx_ref[...] = dx
        rows = [jnp.sum(dhv, axis=0, keepdims=True), jnp.sum(dhv * xn * g_ref[...], axis=0, keepdims=True),
                jnp.sum(dhv * one_sc * xn, axis=0, keepdims=True)]
        if below is not None:
            rows.append(_gate_outputs(dx, rest[0], rest[1], rest[4]))
        part = jnp.concatenate(rows + [jnp.zeros((8 - len(rows), D), F32)], axis=0)

        @pl.when(pl.program_id(0) == 0)
        def _():
            sums_ref[...] = jnp.zeros_like(sums_ref)
        sums_ref[...] += part

    row = pl.BlockSpec((tm, D), lambda i: (i, 0))
    in_specs, args = [row, row, row, _vspec(D), _vspec(D)], [dh, x, dres, g, sc]
    out_specs = [row, pl.BlockSpec((8, D), lambda i: (0, 0))]
    out_shape = [jax.ShapeDtypeStruct((S, D), F32), jax.ShapeDtypeStruct((8, D), F32)]
    if below is not None:
        in_specs += [_vspec(D), row]
        args += [below[0], below[1]]
        out_specs.append(row)
        out_shape.append(jax.ShapeDtypeStruct((S, D), BF16))
    return _pcall(body, name=name, grid=(S // tm,), in_specs=in_specs, out_specs=out_specs, out_shape=out_shape, args=args,
                  comm=comm, carry_us=NORM_BWD_US_PER_ELEM * S * D)


def loss_fwd_bwd(x, g, target, below, name):
    S, D = x.shape
    tm = _tile(S, 256, 16)

    def body(x_ref, g_ref, t_ref, gate_ref, out_ref, dx_ref, sums_ref, dout_ref):
        xv = x_ref[...]
        r = lax.rsqrt(jnp.mean(xv * xv, axis=-1, keepdims=True) + EPS)
        xn = xv * r
        err = xn * g_ref[...] - t_ref[...]
        dy = err * (1.0 / D)
        dxn = dy * g_ref[...]
        dx = r * (dxn - xn * jnp.mean(dxn * xn, axis=-1, keepdims=True))
        dx_ref[...] = dx
        part = jnp.concatenate([jnp.sum(dy * xn, axis=0, keepdims=True), jnp.sum(err * err, axis=0, keepdims=True),
                                _gate_outputs(dx, gate_ref, out_ref, dout_ref), jnp.zeros((5, D), F32)], axis=0)

        @pl.when(pl.program_id(0) == 0)
        def _():
            sums_ref[...] = jnp.zeros_like(sums_ref)
        sums_ref[...] += part

    row = pl.BlockSpec((tm, D), lambda i: (i, 0))
    return pl.pallas_call(body, name=name, grid=(S // tm,), in_specs=[row, _vspec(D), row, _vspec(D), row],
                          out_specs=[row, pl.BlockSpec((8, D), lambda i: (0, 0)), row],
                          out_shape=[jax.ShapeDtypeStruct((S, D), F32), jax.ShapeDtypeStruct((8, D), F32), jax.ShapeDtypeStruct((S, D), BF16)],
                          compiler_params=_cparams(("arbitrary",)))(x, g, target, below[0], below[1])


def pool_fwd(u, wgrp, scale, name):
    S, D = u.shape
    G = len(POOL_WINDOWS)
    C = D // G
    tm = _tile(S, 256, 16)
    hb = tm // HALO

    def body(up_ref, uc_ref, w_ref, sc_ref, p_ref, z_ref, y_ref):
        i = pl.program_id(0)
        prev = jnp.where(i > 0, up_ref[...], 0.0)
        ext = jnp.concatenate([prev, uc_ref[...]], axis=0)
        t = i * tm + lax.broadcasted_iota(jnp.int32, (tm, 1), 0)
        for gi, w in enumerate(POOL_WINDOWS):
            cs = slice(gi * C, (gi + 1) * C)
            e = ext[:, cs]
            s, k = e, 1
            while k < w:
                s = s + pltpu.roll(s, k, 0)
                k *= 2
            cnt = jnp.minimum(t + 1, w).astype(F32)
            pooled = (s[HALO:] / cnt - e[HALO:]).astype(BF16)
            p_ref[:, cs] = pooled
            z = jnp.dot(pooled, w_ref[:, gi].reshape(C, C), preferred_element_type=F32)
            z_ref[:, cs] = z.astype(BF16)
            y_ref[:, cs] = (z * sc_ref[:, cs]).astype(BF16)

    row = pl.BlockSpec((tm, D), lambda i: (i, 0))
    return pl.pallas_call(
        body, name=name, grid=(S // tm,),
        in_specs=[pl.BlockSpec((HALO, D), lambda i: (jnp.maximum(i * hb - 1, 0), 0)), row,
                  pl.BlockSpec(wgrp.shape, lambda i: (0, 0, 0, 0)), _vspec(D)],
        out_specs=[row, row, row], out_shape=[jax.ShapeDtypeStruct((S, D), BF16)] * 3,
        compiler_params=_cparams(("parallel",)))(u, u, wgrp, scale)


def pool_bwd(dys, z, pooled, wgrp, scale, name):
    S, D = dys.shape
    G = len(POOL_WINDOWS)
    C = D // G
    tm = _tile(S, 256, 16)
    hb = tm // HALO
    nt = S // tm
    n_ext = tm + HALO

    def body(dc_ref, dn_ref, z_ref, p_ref, w_ref, sc_ref, du_ref, dw_ref, sums_ref):
        i = pl.program_id(0)

        @pl.when(i == 0)
        def _():
            dw_ref[...] = jnp.zeros_like(dw_ref)
            sums_ref[...] = jnp.zeros_like(sums_ref)
        dyc = dc_ref[...].astype(F32)
        nxt = jnp.where(i < nt - 1, dn_ref[...].astype(F32), 0.0)
        ext = jnp.concatenate([dyc, nxt], axis=0)
        sums_ref[...] += jnp.concatenate([jnp.sum(dyc * z_ref[...].astype(F32), axis=0, keepdims=True), jnp.zeros((7, D), F32)], axis=0)
        t = i * tm + lax.broadcasted_iota(jnp.int32, (n_ext, 1), 0)
        for gi, w in enumerate(POOL_WINDOWS):
            cs = slice(gi * C, (gi + 1) * C)
            wg = w_ref[:, gi].reshape(C, C)
            dz = (ext[:, cs] * sc_ref[:, cs]).astype(BF16)
            dpool = lax.dot_general(dz, wg, (((1,), (1,)), ((), ())), preferred_element_type=F32)
            dw_ref[gi] += lax.dot_general(p_ref[:, cs], dz[:tm], (((0,), (0,)), ((), ())), preferred_element_type=F32)
            cnt = jnp.minimum(t + 1, w).astype(F32)
            s, k = dpool / cnt, 1
            while k < w:
                s = s + pltpu.roll(s, n_ext - k, 0)
                k *= 2
            du_ref[:, cs] = (s[:tm] - dpool[:tm]).astype(BF16)

    row = pl.BlockSpec((tm, D), lambda i: (i, 0))
    return pl.pallas_call(
        body, name=name, grid=(nt,),
        in_specs=[row, pl.BlockSpec((HALO, D), lambda i: (jnp.minimum((i + 1) * hb, S // HALO - 1), 0)), row, row,
                  pl.BlockSpec(wgrp.shape, lambda i: (0, 0, 0, 0)), _vspec(D)],
        out_specs=[row, pl.BlockSpec((G, C, C), lambda i: (0, 0, 0)), pl.BlockSpec((8, D), lambda i: (0, 0))],
        out_shape=[jax.ShapeDtypeStruct((S, D), BF16), jax.ShapeDtypeStruct((G, C, C), F32), jax.ShapeDtypeStruct((8, D), F32)],
        compiler_params=_cparams(("arbitrary",)))(dys, dys, z, pooled, wgrp, scale)


FFN_ACT_BWD_US_PER_ELEM = 84.0 / (4096 * 2816)


def ffn_up_act(h, w3, conv_w, conv_b, name, comm=None):
    S, D = h.shape
    P, _, Ns = w3.shape
    nh = P // 2
    tm = _tile(S, MM_ROWS, 16)

    def body(h_ref, w_ref, cw_ref, cb_ref, hu_ref, g_ref, stash, halo):
        i, j = pl.program_id(0), pl.program_id(1)
        acc = jnp.dot(h_ref[...], w_ref[...], preferred_element_type=F32).astype(BF16)
        hu_ref[...] = acc

        @pl.when(j < nh)
        def _():
            stash[j] = acc.astype(F32)

        @pl.when(j >= nh)
        def _():
            c = j - nh
            a = stash[c]
            ext = jnp.concatenate([jnp.where(i > 0, halo[c], 0.0), a], axis=0)
            conv = cb_ref[...] + pltpu.roll(ext, 2, 0) * cw_ref[0:1, :] + pltpu.roll(ext, 1, 0) * cw_ref[1:2, :] + ext * cw_ref[2:3, :]
            conv = conv[HALO:]
            g_ref[...] = (conv * _sigmoid(conv) * acc.astype(F32)).astype(g_ref.dtype)
            halo[c] = a[tm - HALO:]

    def gcol(j):
        return jnp.maximum(j - nh, 0)

    return _pcall(
        body, name=name, grid=(S // tm, P),
        in_specs=[pl.BlockSpec((tm, D), lambda i, j: (i, 0)), pl.BlockSpec((None, D, Ns), lambda i, j: (j, 0, 0)),
                  pl.BlockSpec((3, Ns), lambda i, j: (0, gcol(j))), pl.BlockSpec((1, Ns), lambda i, j: (0, gcol(j)))],
        out_specs=[pl.BlockSpec((tm, Ns), lambda i, j: (i, j)), pl.BlockSpec((tm, Ns), lambda i, j: (i, gcol(j)))],
        out_shape=[jax.ShapeDtypeStruct((S, P * Ns), BF16), jax.ShapeDtypeStruct((S, nh * Ns), BF16)],
        scratch_shapes=[pltpu.VMEM((nh, tm, Ns), F32), pltpu.VMEM((nh, HALO, Ns), F32)],
        args=[h, w3, conv_w, conv_b], comm=comm, carry_us=2.0 * S * D * P * Ns / MM_FLOPS_PER_US)


def ffn_act_bwd(dout, w_down, hu, conv_w, conv_b, name, comm=None):
    S, D = dout.shape
    F = w_down.shape[1]
    tm = _tile(S, 256, 16)
    tn = _tile(F, 1408, LANES)
    nb = F // tn
    hb = tm // HALO
    nt = S // tm
    n_ext = tm + 2 * HALO
    nt_dims = (((1,), (1,)), ((), ()))

    def body(dc_ref, dn_ref, wd_ref, ap_ref, ac_ref, an_ref, vc_ref, vn_ref, w_ref, b_ref, o_ref, sums_ref):
        i = pl.program_id(1)

        @pl.when(i == 0)
        def _():
            sums_ref[...] = jnp.zeros_like(sums_ref)
        zeros = jnp.zeros((HALO, tn), F32)
        not_last = i < nt - 1
        a_ext = jnp.concatenate([jnp.where(i > 0, ap_ref[...].astype(F32), 0.0), ac_ref[...].astype(F32), an_ref[...].astype(F32)], axis=0)
        v_ext = jnp.concatenate([zeros, vc_ref[...].astype(F32), vn_ref[...].astype(F32)], axis=0)
        g_cur = lax.dot_general(dc_ref[...], wd_ref[...], nt_dims, preferred_element_type=F32)
        g_nxt = lax.dot_general(dn_ref[...], wd_ref[...], nt_dims, preferred_element_type=F32)
        g_ext = jnp.concatenate([zeros, g_cur, jnp.where(not_last, g_nxt, 0.0)], axis=0)
        w0, w1, w2 = w_ref[0:1, :], w_ref[1:2, :], w_ref[2:3, :]
        a_m2, a_m1 = pltpu.roll(a_ext, 2, 0), pltpu.roll(a_ext, 1, 0)
        conv = b_ref[...] + a_m2 * w0 + a_m1 * w1 + a_ext * w2
        sig = _sigmoid(conv)
        silu = conv * sig
        dsilu = sig * (1.0 + conv * (1.0 - sig))
        dconv = g_ext * v_ext * dsilu
        da = dconv * w2 + pltpu.roll(dconv, n_ext - 1, 0) * w1 + pltpu.roll(dconv, n_ext - 2, 0) * w0
        cur = slice(HALO, HALO + tm)
        o_ref[0] = da[cur].astype(o_ref.dtype)
        o_ref[1] = (g_ext * silu)[cur].astype(o_ref.dtype)
        dc = dconv[cur]
        part = jnp.concatenate([jnp.sum(dc * a_m2[cur], axis=0, keepdims=True), jnp.sum(dc * a_m1[cur], axis=0, keepdims=True),
                                jnp.sum(dc * a_ext[cur], axis=0, keepdims=True), jnp.sum(dc, axis=0, keepdims=True),
                                jnp.zeros((4, tn), F32)], axis=0)
        sums_ref[...] += part

    def prev(i):
        return jnp.maximum(i * hb - 1, 0)

    def nxt(i):
        return jnp.minimum((i + 1) * hb, S // HALO - 1)

    return _pcall(
        body, name=name, grid=(nb, nt),
        in_specs=[pl.BlockSpec((tm, D), lambda j, i: (i, 0)), pl.BlockSpec((HALO, D), lambda j, i: (nxt(i), 0)),
                  pl.BlockSpec((None, tn, D), lambda j, i: (0, j, 0)),
                  pl.BlockSpec((HALO, tn), lambda j, i: (prev(i), j)), pl.BlockSpec((tm, tn), lambda j, i: (i, j)),
                  pl.BlockSpec((HALO, tn), lambda j, i: (nxt(i), j)),
                  pl.BlockSpec((tm, tn), lambda j, i: (i, j + nb)), pl.BlockSpec((HALO, tn), lambda j, i: (nxt(i), j + nb)),
                  pl.BlockSpec((3, tn), lambda j, i: (0, j)), pl.BlockSpec((1, tn), lambda j, i: (0, j))],
        out_specs=[pl.BlockSpec((2, tm, tn), lambda j, i: (0, i, j)), pl.BlockSpec((8, tn), lambda j, i: (0, j))],
        out_shape=[jax.ShapeDtypeStruct((2, S, F), BF16), jax.ShapeDtypeStruct((8, F), F32)],
        args=[dout, dout, w_down, hu, hu, hu, hu, hu, conv_w, conv_b], comm=comm, carry_us=FFN_ACT_BWD_US_PER_ELEM * S * F)


def _head_expander(n_heads, da):
    e = np.zeros((LANES, da), np.float32)
    for h in range(n_heads):
        e[h, h * HEAD_DIM:(h + 1) * HEAD_DIM] = 1.0
    return jnp.asarray(e, BF16)


def _split_dot(v, e, dims):
    hi = v.astype(BF16)
    lo = (v - hi.astype(F32)).astype(BF16)
    return (lax.dot_general(hi, e, dims, preferred_element_type=F32) + lax.dot_general(lo, e, dims, preferred_element_type=F32))


def _lane_col(tile, h):
    lane = lax.broadcasted_iota(jnp.int32, tile.shape, 1)
    return jnp.sum(jnp.where(lane == h, tile, 0.0), axis=1, keepdims=True)


ATTN_US_PER_ELEM = (80.0 / (4096 * 1024), 230.0 / (4096 * 1024))


def attn_branch_fwd(q, kv, gi, slopes, name, comm=None):
    S, DA = q.shape
    H = DA // HEAD_DIM
    window, d = BRANCHES[gi]
    n_steps = window // d
    blk = ATTN_BLOCK
    assert n_steps == blk and (S // d) % blk == 0
    nbs = S // d // blk
    scale = HEAD_DIM ** -0.5

    def body(q_ref, kp_ref, kc_ref, vp_ref, vc_ref, o_ref, l_ref, s_scr, p_scr):
        jb = pl.program_id(1)
        row = lax.broadcasted_iota(jnp.int32, (blk, 2 * blk), 0)
        col = lax.broadcasted_iota(jnp.int32, (blk, 2 * blk), 1)
        delta = row + blk - col
        valid = (delta >= 0) & (delta <= n_steps) & ((col >= blk) | (jb > 0))
        dist = jnp.where(valid, (delta * d).astype(F32), -NEG)
        lane = lax.broadcasted_iota(jnp.int32, (blk, LANES), 1)
        ltile = jnp.zeros((blk, LANES), F32)
        for h in range(H):
            hs = slice(h * HEAD_DIM, (h + 1) * HEAD_DIM)
            k2 = jnp.concatenate([kp_ref[:, hs], kc_ref[:, hs]], axis=0)
            s_scr[h] = lax.dot_general(q_ref[:, hs], k2, (((1,), (1,)), ((), ())), preferred_element_type=F32)
        for h in range(H):
            s = s_scr[h] * scale - float(slopes[h]) * dist
            m = jnp.max(s, axis=-1, keepdims=True)
            p = jnp.exp(s - m)
            l = jnp.sum(p, axis=-1, keepdims=True)
            p_scr[h] = (p / l).astype(BF16)
            ltile = jnp.where(lane == h, m + jnp.log(l), ltile)
        for h in range(H):
            hs = slice(h * HEAD_DIM, (h + 1) * HEAD_DIM)
            v2 = jnp.concatenate([vp_ref[:, hs], vc_ref[:, hs]], axis=0)
            o_ref[:, hs] = jnp.dot(p_scr[h], v2, preferred_element_type=F32).astype(o_ref.dtype)
        l_ref[...] = ltile

    def cur(width, off):
        return pl.BlockSpec((blk, width), lambda r, jb: (r * nbs + jb, off))

    def prv(width, off):
        return pl.BlockSpec((blk, width), lambda r, jb: (r * nbs + jnp.maximum(jb - 1, 0), off))

    return _pcall(
        body, name=name, grid=(d, nbs),
        in_specs=[cur(DA, 0), prv(DA, 0), cur(DA, 0), prv(DA, 1), cur(DA, 1)],
        out_specs=[cur(DA, 0), cur(LANES, 0)],
        out_shape=[jax.ShapeDtypeStruct((S, DA), BF16), jax.ShapeDtypeStruct((S, LANES), F32)],
        scratch_shapes=[pltpu.VMEM((H, blk, 2 * blk), F32), pltpu.VMEM((H, blk, 2 * blk), BF16)],
        args=[q, kv, kv, kv, kv], comm=comm, carry_us=ATTN_US_PER_ELEM[0] * S * DA)


def attn_combine(os_, lses, name):
    S, DA = os_[0].shape
    H = DA // HEAD_DIM
    tm = _tile(S, 256, 16)
    expander = _head_expander(H, DA)
    nbr = len(os_)

    def body(*refs):
        o_refs, l_refs, e_ref = refs[:nbr], refs[nbr:2 * nbr], refs[2 * nbr]
        out_ref, lse_ref = refs[2 * nbr + 1:]
        ls = [r[...] for r in l_refs]
        lmax = functools.reduce(jnp.maximum, ls)
        es = [jnp.exp(l - lmax) for l in ls]
        den = functools.reduce(lambda a, b: a + b, es)
        lse_ref[...] = lmax + jnp.log(den)
        acc = jnp.zeros((tm, DA), F32)
        for e, o_ref in zip(es, o_refs):
            acc = acc + _split_dot(e / den, e_ref[...], (((1,), (0,)), ((), ()))) * o_ref[...]
        out_ref[...] = acc.astype(out_ref.dtype)

    row = pl.BlockSpec((tm, DA), lambda i: (i, 0))
    lrow = pl.BlockSpec((tm, LANES), lambda i: (i, 0))
    return pl.pallas_call(
        body, name=name, grid=(S // tm,),
        in_specs=[row] * nbr + [lrow] * nbr + [pl.BlockSpec((LANES, DA), lambda i: (0, 0))],
        out_specs=[row, lrow], out_shape=[jax.ShapeDtypeStruct((S, DA), BF16), jax.ShapeDtypeStruct((S, LANES), F32)],
        compiler_params=_cparams(("parallel",)))(*os_, *lses, expander)


def attn_delta(do, o, name):
    S, DA = o.shape
    H = DA // HEAD_DIM
    tm = _tile(S, 512, 16)
    expander = _head_expander(H, DA)

    def body(do_ref, o_ref, e_ref, d_ref):
        prod = do_ref[...].astype(F32) * o_ref[...].astype(F32)
        d_ref[...] = _split_dot(prod, e_ref[...], (((1,), (1,)), ((), ())))

    row = pl.BlockSpec((tm, DA), lambda i: (i, 0))
    return pl.pallas_call(body, name=name, grid=(S // tm,), in_specs=[row, row, pl.BlockSpec((LANES, DA), lambda i: (0, 0))],
                          out_specs=pl.BlockSpec((tm, LANES), lambda i: (i, 0)), out_shape=jax.ShapeDtypeStruct((S, LANES), F32),
                          compiler_params=_cparams(("parallel",)))(do, o, expander)


def attn_branch_bwd(q, kv, do, lse, dlt, gi, slopes, name, out_cols=None, comm=None):
    S, DA = q.shape
    H = DA // HEAD_DIM
    window, d = BRANCHES[gi]
    n_steps = window // d
    blk = ATTN_BLOCK
    nbs = S // d // blk
    scale = HEAD_DIM ** -0.5
    nt, tn = (((1,), (1,)), ((), ())), (((0,), (0,)), ((), ()))

    def body(*refs):
        k_ref, v_ref, qc_ref, qn_ref, doc_ref, don_ref, lc_ref, ln_ref, dc_ref, dn_ref = refs[:10]
        dq_ref, dkv_ref, carry, s_scr, dp_scr, p_scr, ds_scr = refs[-7:]
        kb = pl.program_id(1)

        @pl.when(kb == 0)
        def _():
            carry[...] = jnp.zeros_like(carry)
        row = lax.broadcasted_iota(jnp.int32, (2 * blk, blk), 0)
        col = lax.broadcasted_iota(jnp.int32, (2 * blk, blk), 1)
        delta = row - col
        valid = (delta >= 0) & (delta <= n_steps) & ((row < blk) | (kb < nbs - 1))
        dist = jnp.where(valid, (delta * d).astype(F32), -NEG)
        l2 = jnp.concatenate([lc_ref[...], ln_ref[...]], axis=0)
        d2 = jnp.concatenate([dc_ref[...], dn_ref[...]], axis=0)
        for h in range(H):
            hs = slice(h * HEAD_DIM, (h + 1) * HEAD_DIM)
            q2 = jnp.concatenate([qc_ref[:, hs], qn_ref[:, hs]], axis=0)
            do2 = jnp.concatenate([doc_ref[:, hs], don_ref[:, hs]], axis=0)
            s_scr[h] = lax.dot_general(q2, k_ref[:, hs], nt, preferred_element_type=F32)
            dp_scr[h] = lax.dot_general(do2, v_ref[:, hs], nt, preferred_element_type=F32)
        for h in range(H):
            p = jnp.exp(s_scr[h] * scale - float(slopes[h]) * dist - _lane_col(l2, h))
            p_scr[h] = p.astype(BF16)
            ds_scr[h] = (p * (dp_scr[h] - _lane_col(d2, h))).astype(BF16)
        for h in range(H):
            hs = slice(h * HEAD_DIM, (h + 1) * HEAD_DIM)
            vs = slice(DA + h * HEAD_DIM, DA + (h + 1) * HEAD_DIM)
            q2 = jnp.concatenate([qc_ref[:, hs], qn_ref[:, hs]], axis=0)
            do2 = jnp.concatenate([doc_ref[:, hs], don_ref[:, hs]], axis=0)
            dvh = lax.dot_general(p_scr[h], do2, tn, preferred_element_type=F32)
            dkh = lax.dot_general(ds_scr[h], q2, tn, preferred_element_type=F32) * scale
            dq2 = jnp.dot(ds_scr[h], k_ref[:, hs], preferred_element_type=F32) * scale
            dq_ref[:, hs] = (carry[:, hs] + dq2[:blk]).astype(dq_ref.dtype)
            carry[:, hs] = dq2[blk:]
            dkv_ref[:, hs] = dkh.astype(dkv_ref.dtype)
            dkv_ref[:, vs] = dvh.astype(dkv_ref.dtype)

    def cur(width, off):
        return pl.BlockSpec((blk, width), lambda r, kb: (r * nbs + kb, off))

    def nxt(width, off):
        return pl.BlockSpec((blk, width), lambda r, kb: (r * nbs + jnp.minimum(kb + 1, nbs - 1), off))

    in_specs = [cur(DA, 0), cur(DA, 1), cur(DA, 0), nxt(DA, 0), cur(DA, 0), nxt(DA, 0),
                cur(LANES, 0), nxt(LANES, 0), cur(LANES, 0), nxt(LANES, 0)]
    args = [kv, kv, q, q, do, do, lse, lse, dlt, dlt]
    return _pcall(
        body, name=name, grid=(d, nbs), in_specs=in_specs, out_specs=[cur(DA, 0), cur(2 * DA, 0)],
        out_shape=[jax.ShapeDtypeStruct((S, out_cols or DA), BF16), jax.ShapeDtypeStruct((S, 2 * DA), BF16)],
        scratch_shapes=[pltpu.VMEM((blk, DA), F32), pltpu.VMEM((H, 2 * blk, blk), F32), pltpu.VMEM((H, 2 * blk, blk), F32),
                        pltpu.VMEM((H, 2 * blk, blk), BF16), pltpu.VMEM((H, 2 * blk, blk), BF16)],
        args=args, comm=comm, carry_us=ATTN_US_PER_ELEM[1] * S * DA)


def ada_project(c16, w3, b3, name):
    L, D, Ns = w3.shape
    tn = _tile(Ns, 512, LANES)

    def body(c_ref, w_ref, b_ref, o_ref):
        cv = c_ref[...]
        cond = (cv * _sigmoid(cv)).astype(BF16)
        o_ref[...] = jnp.dot(cond, w_ref[...].astype(BF16), preferred_element_type=F32) + b_ref[...]

    return pl.pallas_call(
        body, name=name, grid=(L, Ns // tn),
        in_specs=[pl.BlockSpec((16, D), lambda l, j: (0, 0)), pl.BlockSpec((None, D, tn), lambda l, j: (l, 0, j)),
                  pl.BlockSpec((None, 1, tn), lambda l, j: (l, 0, j))],
        out_specs=pl.BlockSpec((None, 16, tn), lambda l, j: (l, 0, j)), out_shape=jax.ShapeDtypeStruct((L, 16, Ns), F32),
        compiler_params=_cparams(("parallel", "parallel")))(c16, w3, b3)


def _adamw(w, g, m, v):
    m = B1 * m + (1.0 - B1) * g
    v = B2 * v + (1.0 - B2) * (g * g)
    m_hat = m / (1.0 - B1 ** STEP)
    v_hat = v / (1.0 - B2 ** STEP)
    delta = -LR * (m_hat / (jnp.sqrt(v_hat) + ADAM_EPS) + WD * w)
    return delta, m, v


def ada_grad_adamw(c16, d3, w3, m3, v3, name):
    L, D, Ns = w3.shape
    tk = _tile(D, 256, 8)

    def body(c_ref, d_ref, w_ref, m_ref, v_ref, g_out, dl_out, m_out, v_out):
        cv = c_ref[...]
        cond = (cv * _sigmoid(cv)).astype(BF16)
        g = lax.dot_general(cond, d_ref[...].astype(BF16), (((0,), (0,)), ((), ())), preferred_element_type=F32)
        g_out[...] = g
        dl_out[...], m_out[...], v_out[...] = _adamw(w_ref[...], g, m_ref[...], v_ref[...])

    wspec = pl.BlockSpec((None, tk, Ns), lambda l, kj: (l, kj, 0))
    return pl.pallas_call(
        body, name=name, grid=(L, D // tk),
        in_specs=[pl.BlockSpec((16, tk), lambda l, kj: (0, kj)), pl.BlockSpec((None, 16, Ns), lambda l, kj: (l, 0, 0)), wspec, wspec, wspec],
        out_specs=[wspec] * 4, out_shape=[jax.ShapeDtypeStruct((L, D, Ns), F32)] * 4,
        compiler_params=_cparams(("parallel", "parallel")))(c16, d3, w3, m3, v3)


def adamw(w, g, m, v, name):
    R, C = w.shape
    tr = _tile(R, 256, 8)

    def body(w_ref, g_ref, m_ref, v_ref, g_out, dl_out, m_out, v_out):
        g = g_ref[...]
        g_out[...] = g
        dl_out[...], m_out[...], v_out[...] = _adamw(w_ref[...], g, m_ref[...], v_ref[...])

    spec = pl.BlockSpec((tr, C), lambda i: (i, 0))
    return pl.pallas_call(body, name=name, grid=(R // tr,), in_specs=[spec] * 4, out_specs=[spec] * 4,
                          out_shape=[jax.ShapeDtypeStruct((R, C), F32)] * 4, compiler_params=_cparams(("parallel",)))(w, g, m, v)


def sum_partials(own, recv, g_prev, layer, n_layers, pos, name):
    _, Rh, C = recv.shape
    tr = _tile(Rh, 256, 16)

    def body(pos_ref, own_ref, recv_ref, *rest):
        acc = own_ref[...].astype(F32)
        for rel in range(7):
            acc = acc + recv_ref[rel].astype(F32)
        rest[-1][...] = acc

    in_specs = [pl.BlockSpec((None, None, tr, C), lambda r, pos: (pos[1], pos[0], r, 0)), pl.BlockSpec((7, tr, C), lambda r, pos: (0, r, 0))]
    args = [pos, own.reshape(N_CHIPS, 2, Rh, C), recv]
    aliases = {}
    if g_prev is not None:
        in_specs.append(pl.BlockSpec(memory_space=pl.ANY))
        args.append(g_prev)
        aliases = {3: 0}
    return pl.pallas_call(
        body, name=name,
        grid_spec=pltpu.PrefetchScalarGridSpec(
            num_scalar_prefetch=1, grid=(Rh // tr,), in_specs=in_specs,
            out_specs=pl.BlockSpec((None, None, tr, C), lambda r, pos: (layer, pos[0], r, 0))),
        out_shape=jax.ShapeDtypeStruct((n_layers, 2, Rh, C), F32), input_output_aliases=aliases,
        compiler_params=_cparams(("parallel",)))(*args)


def sum_rows8(g8, name):
    _, R, C = g8.shape

    def body(g_ref, o_ref):
        acc = g_ref[0]
        for i in range(1, N_DEV):
            acc = acc + g_ref[i]
        o_ref[...] = acc

    return pl.pallas_call(body, name=name, grid=(1,), in_specs=[pl.BlockSpec((N_DEV, R, C), lambda i: (0, 0, 0))],
                          out_specs=pl.BlockSpec((R, C), lambda i: (0, 0)), out_shape=jax.ShapeDtypeStruct((R, C), F32),
                          compiler_params=_cparams(("arbitrary",)))(g8)


def _pack(vecs):
    flat = [v.reshape(-1).astype(F32) for v in vecs]
    sizes = [f.shape[0] for f in flat]
    total = sum(sizes)
    padded = -(-total // (8 * PACK_W)) * (8 * PACK_W)
    buf = jnp.concatenate(flat + [jnp.zeros((padded - total,), F32)])
    offs = np.concatenate([[0], np.cumsum(sizes)])
    return buf.reshape(-1, PACK_W), offs


def _unpack(buf, offs, shapes):
    flat = buf.reshape(-1)
    return [flat[int(offs[i]):int(offs[i + 1])].reshape(s) for i, s in enumerate(shapes)]


def kernel(x, c, ada_w, ada_b, norm1_g, norm2_g, pool_w_in, pool_w_grp, pool_scale, pool_w_out, kv_norm_g, kv_ada_w, kv_ada_b, w_kv, attn_w_q, attn_w_o, ffn_w_up, ffn_conv_w, ffn_conv_b, ffn_w_down, final_g, loss_target, m_ada_w, m_ada_b, m_norm1_g, m_norm2_g, m_pool_w_in, m_pool_w_grp, m_pool_scale, m_pool_w_out, m_kv_norm_g, m_kv_ada_w, m_kv_ada_b, m_w_kv, m_attn_w_q, m_attn_w_o, m_ffn_w_up, m_ffn_conv_w, m_ffn_conv_b, m_ffn_w_down, m_final_g, v_ada_w, v_ada_b, v_norm1_g, v_norm2_g, v_pool_w_in, v_pool_w_grp, v_pool_scale, v_pool_w_out, v_kv_norm_g, v_kv_ada_w, v_kv_ada_b, v_w_kv, v_attn_w_q, v_attn_w_o, v_ffn_w_up, v_ffn_conv_w, v_ffn_conv_b, v_ffn_w_down, v_final_g):
    S, D = x.shape[1], x.shape[2]
    depth = ada_w.shape[0]
    n_pool = pool_w_in.shape[0]
    n_attn = attn_w_q.shape[0]
    G = len(POOL_WINDOWS)
    NB = len(BRANCHES)
    DA = attn_w_o.shape[1] * N_CHIPS
    H = DA // HEAD_DIM
    F = ffn_conv_b.shape[1]
    Fs = F // N_CHIPS
    Dq = D // N_CHIPS
    ada_ns = ada_w.shape[2]
    kvada_ns = kv_ada_w.shape[1]
    slopes = _alibi_slopes(NB * H).reshape(NB, H)

    ix, iy, ic = lax.axis_index("x"), lax.axis_index("y"), lax.axis_index("c")
    p_me = 2 * ix + iy
    b_me = 4 * ix + 2 * iy + ic
    pos = jnp.stack([ic, p_me]).astype(jnp.int32)
    xs, tgt = x[0], loss_target[0]

    pk, offs = _pack([c, pool_scale, ffn_conv_w])
    rows1 = pk.shape[0]
    got = all_gather8(pk, "gather_small_in").reshape(N_DEV, rows1, PACK_W)
    c8 = got.reshape(N_DEV, -1)[:, :D]
    c16 = jnp.concatenate([c8, jnp.zeros_like(c8)], axis=0)
    chip_rows = got[0::2].reshape(N_CHIPS, -1)
    scale_full = chip_rows[:, int(offs[1]):int(offs[2])].reshape(N_CHIPS, n_pool, Dq).transpose(1, 0, 2).reshape(n_pool, D)
    convw_full = chip_rows[:, int(offs[2]):int(offs[3])].reshape(N_CHIPS, depth, 3, Fs).transpose(1, 2, 0, 3).reshape(depth, 3, F)

    ada_b_loc = lax.dynamic_slice(ada_b, (0, p_me * ada_ns), (depth, ada_ns)).reshape(depth, 1, ada_ns)
    kvb_loc = lax.dynamic_slice(kv_ada_b, (p_me * kvada_ns,), (kvada_ns,)).reshape(1, 1, kvada_ns)
    mods_loc = ada_project(c16, ada_w, ada_b_loc, "ada_project")[:, :N_DEV]
    kvmod_loc = ada_project(c16, kv_ada_w.reshape(1, D, kvada_ns), kvb_loc, "kv_ada_project")[0, :N_DEV]
    mods_cat = jnp.concatenate([mods_loc.transpose(1, 0, 2).reshape(N_DEV, depth * ada_ns), kvmod_loc], axis=1)
    mods_all = all_gather8(mods_cat, "gather_mods").reshape(N_CHIPS, 2, N_DEV, -1)
    mine = lax.dynamic_index_in_dim(mods_all[:, 0], b_me, axis=1, keepdims=False)
    mod = mine[:, :depth * ada_ns].reshape(N_CHIPS, depth, ada_ns).transpose(1, 0, 2).reshape(depth, 6, 1, D)
    kvmod = mine[:, depth * ada_ns:].reshape(2, 1, D)

    comm = _Comm()
    C = D // G
    kv_ns, q_ns, up_ns = w_kv.shape[1], attn_w_q.shape[2], ffn_w_up.shape[2]

    def layer_shards(l):
        sh = []
        if l < n_pool:
            sh += [(("pin", l), pool_w_in[l]), (("pgrp", l), pool_w_grp[l].reshape(-1, C)), (("pout", l), pool_w_out[l])]
        else:
            if l == n_pool:
                sh.append((("kv", 0), w_kv))
            sh += [(("wq", l), attn_w_q[l - n_pool]), (("wo", l), attn_w_o[l - n_pool])]
        sh += [(("up", l), ffn_w_up[l]), (("down", l), ffn_w_down[l])]
        return [(k, w.astype(BF16)) for k, w in sh]

    def weight(key, shape):
        return comm.require(key).reshape(shape)

    dil = [d for _, d in BRANCHES]
    kv_tn = DA // 2
    q_tn = DA // 4
    q_bwd_tn = q_ns
    up_tn = up_ns
    up_per_half = F // up_tn

    def up_gmap(s):
        return s // up_per_half, s % up_per_half

    def vec(v):
        return v.reshape(1, -1)

    saved = []
    xcur = xs
    kvs = None
    wts = {}
    push_gather(comm, layer_shards(0))
    for l in range(depth):
        if l + 1 < depth:
            push_gather(comm, layer_shards(l + 1))
        sh1, sc1, g1, sh2, sc2, g2 = [mod[l, i] for i in range(6)]
        st = {"x0": xcur}
        h1 = norm_mod(xcur, vec(norm1_g[l]), sh1, sc1, "norm_mod", comm=comm)
        st["h1"] = h1
        if l < n_pool:
            wts["pin", l] = weight(("pin", l), (1, D, D))
            u = mm_nn(h1, wts["pin", l], tn=D, out_dtype=F32, name="pool_in_proj", comm=comm)
            wts["pgrp", l] = weight(("pgrp", l), (N_CHIPS, G, C // N_CHIPS, C))
            pooled, z, ys = pool_fwd(u, wts["pgrp", l], vec(scale_full[l]), "pool_mix")
            wts["pout", l] = weight(("pout", l), (1, D, D))
            out, x1 = mm_nn(ys, wts["pout", l], tn=D, out_dtype=BF16, name="pool_out_proj", res=(xcur, g1), comm=comm)
            st.update(pooled=pooled, z=z, ys=ys, out1=out)
        else:
            if l == n_pool:
                wts["kv", 0] = weight(("kv", 0), (N_CHIPS, D, kv_ns))
                hkv = norm_mod(xcur, vec(kv_norm_g), kvmod[0], kvmod[1], "norm_mod", comm=comm)
                kvs = [mm_nn(hkv, wts["kv", 0], tn=kv_tn, out_dtype=BF16, name=f"kv_proj_b{gi}", ncb=4, perm_d=dil[gi], comm=comm,
                             cbmap=functools.partial(lambda jj, gi: 2 * gi + (jj // 2) * 2 * NB + jj % 2, gi=gi)) for gi in range(NB)]
                kv_state = {"x": xcur, "hkv": hkv}
            wts["wq", l] = weight(("wq", l), (N_CHIPS, D, q_ns))
            qs, os_, lses = [], [], []
            for gi in range(NB):
                q_b = mm_nn(h1, wts["wq", l], tn=q_tn, out_dtype=BF16, name=f"q_proj_b{gi}", ncb=4, perm_d=dil[gi], comm=comm,
                            cbmap=functools.partial(lambda jj, gi: 4 * gi + jj, gi=gi))
                o_b, l_b = attn_branch_fwd(q_b, kvs[gi], gi, slopes[gi], f"attn_fwd_b{gi}", comm=comm)
                if dil[gi] > 1:
                    o_b = unpermute_rows(o_b, dil[gi], f"unpermute_o_b{gi}")
                    l_b = unpermute_rows(l_b, dil[gi], f"unpermute_lse_b{gi}")
                qs.append(q_b)
                os_.append(o_b)
                lses.append(l_b)
            o, lse = attn_combine(os_, lses, "attn_combine")
            wts["wo", l] = weight(("wo", l), (1, DA, D))
            out, x1 = mm_nn(o, wts["wo", l], tn=D, out_dtype=BF16, name="attn_out_proj", res=(xcur, g1), comm=comm)
            st.update(qs=qs, o=o, lse=lse, out1=out)
        st["x1"] = x1
        h2 = norm_mod(x1, vec(norm2_g[l]), sh2, sc2, "norm_mod", comm=comm)
        wts["up", l] = weight(("up", l), (N_CHIPS, D, up_ns))
        hu, gated = ffn_up_act(h2, wts["up", l], convw_full[l], vec(ffn_conv_b[l]), "ffn_up_act", comm=comm)
        wts["down", l] = weight(("down", l), (1, F, D))
        out2, x2 = mm_nn(gated, wts["down", l], tn=D, out_dtype=BF16, name="ffn_down_proj", res=(x1, g2), comm=comm)
        st.update(h2=h2, hu=hu, gated=gated, out2=out2)
        saved.append(st)
        xcur = x2
    comm.flush()

    dx, fsums, dout2 = loss_fwd_bwd(xcur, vec(final_g), tgt, (mod[depth - 1, 5], saved[depth - 1]["out2"]), "loss_head")
    loss = lax.psum(0.5 * jnp.sum(fsums[1]) / D, ("x", "y", "c"))
    d_final_g, s_g2 = fsums[0], fsums[2]

    dmods = [None] * depth
    d_n1 = [None] * depth
    d_n2 = [None] * depth
    d_convw = [None] * depth
    d_convb = [None] * depth
    d_scale = [None] * n_pool
    d_grp = [None] * n_pool
    dkvs = [[] for _ in range(NB)]
    exchanged = []
    f_tk = _tile(F, 1408, LANES)
    ct_blocks = DA // _tile(DA, 256, LANES)

    def exchange(name, idx, dw):
        dw4 = dw.reshape(N_CHIPS, -1, dw.shape[-1])
        exchanged.append((name, idx, dw4))
        push_exchange(comm, (name, idx), dw4)

    for l in reversed(range(depth)):
        st = saved[l]
        sh1, sc1, g1, sh2, sc2, g2 = [mod[l, i] for i in range(6)]
        dout2_3 = dout2.reshape(1, S, D)
        exchange("down", l, mm_tn(st["gated"], dout2_3, (1, F, D), tn=D, tk=f_tk, name="ffn_down_dw", comm=comm))
        dhu, s_conv = ffn_act_bwd(dout2, wts["down", l], st["hu"], convw_full[l], vec(ffn_conv_b[l]), "ffn_act_bwd", comm=comm)
        dh2 = mm_nt(dhu, wts["up", l], tn=up_tn, tk=D, out_dtype=F32, name="ffn_up_bwd", gmap=up_gmap, comm=comm)
        exchange("up", l, mm_tn(st["h2"], dhu, (N_CHIPS, D, up_ns), tn=up_tn, tk=D, name="ffn_up_dw", gmap=up_gmap, comm=comm))
        dx, s_n2, dout1 = norm_mod_bwd(dh2, st["x1"], dx, vec(norm2_g[l]), sc2, "norm_mod_bwd_gate", below=(g1, st["out1"]), comm=comm)
        d_convw[l], d_convb[l] = s_conv[0:3], s_conv[3]
        d_n2[l] = s_n2[2]
        dout1_3 = dout1.reshape(1, S, D)
        if l < n_pool:
            dys = mm_nt(dout1_3, wts["pout", l], tn=D, tk=D // 2, out_dtype=F32, name="pool_out_bwd", comm=comm)
            exchange("pout", l, mm_tn(st["ys"], dout1_3, (1, D, D), tn=D, tk=D, name="pool_out_dw", comm=comm))
            du, d_grp, s_sc = pool_bwd(dys, st["z"], st["pooled"], wts["pgrp", l], vec(scale_full[l]), "pool_mix_bwd")
            exchange("pgrp", l, d_grp.astype(BF16).reshape(G, N_CHIPS, C // N_CHIPS, C).transpose(1, 0, 2, 3))
            d_scale[l] = s_sc[0]
            du_3 = du.reshape(1, S, D)
            dh1 = mm_nt(du_3, wts["pin", l], tn=D, tk=D // 2, out_dtype=F32, name="pool_in_bwd", comm=comm)
            exchange("pin", l, mm_tn(st["h1"], du_3, (1, D, D), tn=D, tk=D, name="pool_in_dw", comm=comm))
        else:
            j = l - n_pool
            do = mm_nt(dout1_3, wts["wo", l], tn=D, tk=DA // 2, out_dtype=BF16, name="attn_out_bwd", comm=comm)
            exchange("wo", j, mm_tn(st["o"], dout1_3, (1, DA, D), tn=D, tk=DA, name="attn_out_dw", comm=comm))
            dlt = attn_delta(do, st["o"], "attn_delta")
            dq = None
            for gi in range(NB):
                d = dil[gi]
                do_b, l_b, dl_b = do, st["lse"], dlt
                if d > 1:
                    do_b = permute_rows(do, d, f"permute_do_b{gi}")
                    l_b = permute_rows(st["lse"], d, f"permute_lse_b{gi}")
                    dl_b = permute_rows(dlt, d, f"permute_delta_b{gi}")
                bwd_name = f"attn_bwd_b{gi}"
                if d > 1:
                    dq_b, dkv_b = attn_branch_bwd(st["qs"][gi], kvs[gi], do_b, l_b, dl_b, gi, slopes[gi], bwd_name, comm=comm)
                    dq = unpermute_rows(dq_b, d, f"unpermute_dq_b{gi}", into=dq, total_cols=NB * DA,
                                        colmap=functools.partial(lambda jj, gi: gi * ct_blocks + jj, gi=gi))
                else:
                    dq, dkv_b = attn_branch_bwd(st["qs"][gi], kvs[gi], do_b, l_b, dl_b, gi, slopes[gi], bwd_name,
                                                out_cols=NB * DA, comm=comm)
                dkvs[gi].append(dkv_b)
            dq_3 = dq.reshape(1, S, NB * DA)
            dh1 = mm_nt(dq_3, wts["wq", l], tn=q_bwd_tn, tk=D, out_dtype=F32, name="q_proj_bwd", comm=comm)
            exchange("wq", j, mm_tn(st["h1"], dq_3, (N_CHIPS, D, q_ns), tn=q_bwd_tn, tk=D, name="q_proj_dw", comm=comm))
        below = (mod[l - 1, 5], saved[l - 1]["out2"]) if l > 0 else None
        if l == n_pool or below is None:
            dx, s_n1 = norm_mod_bwd(dh1, st["x0"], dx, vec(norm1_g[l]), sc1, "norm_mod_bwd", comm=comm if l > 0 else None)
        else:
            dx, s_n1, dout2 = norm_mod_bwd(dh1, st["x0"], dx, vec(norm1_g[l]), sc1, "norm_mod_bwd_gate", below=below, comm=comm)
        d_n1[l] = s_n1[2]
        dmods[l] = jnp.stack([s_n1[0], s_n1[1], s_n2[3], s_n2[0], s_n2[1], s_g2])
        if l > 0 and l != n_pool:
            s_g2 = s_n1[3]
        if l == n_pool:
            dkv = None
            for gi in range(NB):
                dkv = unpermute_rows(dkvs[gi], dil[gi], f"unpermute_dkv_b{gi}", into=dkv, total_cols=2 * NB * DA,
                                     colmap=functools.partial(lambda jj, gi: (jj // ct_blocks) * NB * ct_blocks + gi * ct_blocks + jj % ct_blocks,
                                                              gi=gi))
            dkv_3 = dkv.reshape(1, S, 2 * NB * DA)
            dhkv = mm_nt(dkv_3, wts["kv", 0], tn=kv_ns // 2, tk=D, out_dtype=F32, name="kv_proj_bwd", comm=comm)
            exchange("kv", 0, mm_tn(kv_state["hkv"], dkv_3, (N_CHIPS, D, kv_ns), tn=kv_ns // 2, tk=D, name="kv_proj_dw", comm=comm))
            dx, s_kv, dout2 = norm_mod_bwd(dhkv, kv_state["x"], dx, vec(kv_norm_g), kvmod[1], "norm_mod_bwd_gate", below=below, comm=comm)
            s_g2 = s_kv[3]
    grad_x = dx.reshape(1, S, D)

    smalls = [jnp.stack(dmods), jnp.stack([s_kv[0], s_kv[1]]), jnp.stack(d_n1), jnp.stack(d_n2), s_kv[2], jnp.stack(d_convb), d_final_g,
              jnp.stack(d_scale), jnp.stack(d_convw)]
    small_shapes = [s.shape for s in smalls]
    spk, soffs = _pack(smalls)
    srows = spk.shape[0]
    sgot = all_gather8(spk, "gather_small_grads").reshape(N_DEV, srows, PACK_W)
    ssum = sum_rows8(sgot, "sum_small_grads")
    g_mods, g_kvmod, g_n1, g_n2, g_kvn, g_convb, g_fg, g_scale_full, g_convw_full = _unpack(ssum, soffs, small_shapes)
    g_ada_b = g_mods.reshape(depth, 6 * D)
    g_kv_ada_b = g_kvmod.reshape(2 * D)
    g_scale = lax.dynamic_slice(g_scale_full, (0, p_me * Dq), (n_pool, Dq))
    g_convw = lax.dynamic_slice(g_convw_full, (0, 0, p_me * Fs), (depth, 3, Fs))

    small_w = [ada_b, norm1_g, norm2_g, kv_norm_g, kv_ada_b, ffn_conv_b, final_g, pool_scale, ffn_conv_w]
    small_m = [m_ada_b, m_norm1_g, m_norm2_g, m_kv_norm_g, m_kv_ada_b, m_ffn_conv_b, m_final_g, m_pool_scale, m_ffn_conv_w]
    small_v = [v_ada_b, v_norm1_g, v_norm2_g, v_kv_norm_g, v_kv_ada_b, v_ffn_conv_b, v_final_g, v_pool_scale, v_ffn_conv_w]
    small_g = [g_ada_b, g_n1, g_n2, g_kvn, g_kv_ada_b, g_convb, g_fg, g_scale, g_convw]
    sw_shapes = [w.shape for w in small_w]
    pw, woffs = _pack(small_w)
    s_res = adamw(pw, _pack(small_g)[0], _pack(small_m)[0], _pack(small_v)[0], "adamw_small")
    s_g, s_dl, s_m, s_v = [_unpack(r, woffs, sw_shapes) for r in s_res]

    per_dev = sgot.reshape(N_DEV, -1)
    dm_all = per_dev[:, int(soffs[0]):int(soffs[1])].reshape(N_DEV, depth, 6 * D)
    dkvm_all = per_dev[:, int(soffs[1]):int(soffs[2])].reshape(N_DEV, 1, 2 * D)

    def shard_cols(a, ns):
        sl = lax.dynamic_slice_in_dim(a, p_me * ns, ns, axis=2).transpose(1, 0, 2)
        return jnp.concatenate([sl, jnp.zeros_like(sl)], axis=1)

    ada_res = ada_grad_adamw(c16, shard_cols(dm_all, ada_ns), ada_w, m_ada_w, v_ada_w, "ada_grad_adamw")
    kvada_res = ada_grad_adamw(c16, shard_cols(dkvm_all, kvada_ns), kv_ada_w.reshape(1, D, kvada_ns), m_kv_ada_w.reshape(1, D, kvada_ns),
                               v_kv_ada_w.reshape(1, D, kvada_ns), "kv_ada_grad_adamw")
    kvada_res = [r.reshape(D, kvada_ns) for r in kvada_res]

    comm.flush()
    big_names = ["pin", "pgrp", "pout", "kv", "wq", "wo", "up", "down"]
    n_stack = {"pin": n_pool, "pgrp": n_pool, "pout": n_pool, "kv": 1, "wq": n_attn, "wo": n_attn, "up": depth, "down": depth}
    gsum = {nm: None for nm in big_names}
    for nm, idx, dw4 in exchanged:
        gsum[nm] = sum_partials(dw4, comm.store[nm, idx], gsum[nm], idx, n_stack[nm], pos, "sum_partials")
    for nm in big_names:
        comm.push(_Phase(("swap", nm), ("swap", nm), None, ("swap", nm), 0.0, [], None, gsum[nm].shape[0], 0, _build_swap, buffer=gsum[nm]))
    comm.flush()
    gsum = {nm: comm.store["swap", nm] for nm in big_names}
    big_m = [m_pool_w_in, m_pool_w_grp, m_pool_w_out, m_w_kv, m_attn_w_q, m_attn_w_o, m_ffn_w_up, m_ffn_w_down]
    big_v = [v_pool_w_in, v_pool_w_grp, v_pool_w_out, v_w_kv, v_attn_w_q, v_attn_w_o, v_ffn_w_up, v_ffn_w_down]
    big_w = [pool_w_in, pool_w_grp, pool_w_out, w_kv, attn_w_q, attn_w_o, ffn_w_up, ffn_w_down]
    big_res = []
    for nm, w, m_, v_ in zip(big_names, big_w, big_m, big_v):
        cols = gsum[nm].shape[-1]
        res = adamw(w.reshape(-1, cols), gsum[nm].reshape(-1, cols), m_.reshape(-1, cols), v_.reshape(-1, cols), "adamw_big")
        big_res.append([r.reshape(w.shape) for r in res])

    order = ["ada_w", "ada_b", "norm1_g", "norm2_g", "pool_w_in", "pool_w_grp", "pool_scale", "pool_w_out", "kv_norm_g", "kv_ada_w",
             "kv_ada_b", "w_kv", "attn_w_q", "attn_w_o", "ffn_w_up", "ffn_conv_w", "ffn_conv_b", "ffn_w_down", "final_g"]
    small_names = ["ada_b", "norm1_g", "norm2_g", "kv_norm_g", "kv_ada_b", "ffn_conv_b", "final_g", "pool_scale", "ffn_conv_w"]
    results = {"ada_w": ada_res, "kv_ada_w": kvada_res}
    for i, nm in enumerate(small_names):
        results[nm] = [s_g[i], s_dl[i], s_m[i], s_v[i]]
    for i, nm in enumerate(["pool_w_in", "pool_w_grp", "pool_w_out", "w_kv", "attn_w_q", "attn_w_o", "ffn_w_up", "ffn_w_down"]):
        results[nm] = big_res[i]
    outs = [loss, grad_x]
    for kind in range(4):
        outs += [results[nm][kind] for nm in order]
    return tuple(outs)
```

```python
import functools
import math

import numpy as np
import jax
import jax.numpy as jnp
from jax import lax
from jax.experimental import pallas as pl
from jax.experimental.pallas import tpu as pltpu

F32 = jnp.float32
BF16 = jnp.bfloat16
MESH = pl.DeviceIdType.MESH

POOL_WINDOWS = (2, 4, 8, 16)
BRANCHES = ((128, 1), (512, 4), (2048, 16))
HEAD_DIM = 64
ATTN_BLOCK = 128
EPS = 1e-6
LR, B1, B2, ADAM_EPS, WD, STEP = 0.001, 0.9, 0.999, 1e-08, 0.01, 10

VMEM_LIMIT_BYTES = 56 * 1024 * 1024
LANES = 128
PACK_W = 1024
HALO = 16
NEG = -1e30
N_CHIPS = 4
N_DEV = 8


def _alibi_slopes(n):
    def pow2(m):
        start = 2.0 ** (-(2.0 ** -(math.log2(m) - 3)))
        return [start ** (i + 1) for i in range(m)]
    if math.log2(n).is_integer():
        s = pow2(n)
    else:
        c = 2 ** math.floor(math.log2(n))
        s = pow2(c) + pow2(2 * c)[0::2][: n - c]
    s = np.asarray(s, dtype=np.float32)
    return -np.sort(-s)


def _cparams(sem=None):
    return pltpu.CompilerParams(dimension_semantics=sem, vmem_limit_bytes=VMEM_LIMIT_BYTES)


def _tile(n, pref, unit):
    t = (min(pref, n) // unit) * unit
    while t >= unit:
        if n % t == 0:
            return t
        t -= unit
    return n


def _sigmoid(v):
    return 1.0 / (1.0 + jnp.exp(-v))


def all_gather8(xs, name):
    m_per, n = xs.shape

    def body(x_ref, out_ref, send_sems, recv_sems, local_sem):
        x, y, c = lax.axis_index("x"), lax.axis_index("y"), lax.axis_index("c")
        me, sibling = (x, y, c), (x, y, 1 - c)
        chips = [(1 - x, y), (x, 1 - y), (1 - x, 1 - y)]

        def rows(px, py, pc):
            return out_ref.at[pl.ds((4 * px + 2 * py + pc) * m_per, m_per), :]

        def copy(k, block, to, src=None):
            return pltpu.make_async_remote_copy(src_ref=rows(*block) if src is None else src, dst_ref=rows(*block),
                                                send_sem=send_sems.at[k], recv_sem=recv_sems.at[k], device_id=to, device_id_type=MESH)

        mine = pltpu.make_async_copy(x_ref, rows(*me), local_sem)
        mine.start()
        first = [copy(0, me, sibling, src=x_ref)]
        first += [copy(1 + j, me, (*chip, c), src=x_ref) for j, chip in enumerate(chips)]
        for cp in first:
            cp.start()
        passed = [copy(4 + j, (*chip, c), sibling) for j, chip in enumerate(chips)]
        for j, chip in enumerate(chips):
            copy(1 + j, (*chip, c), me).wait_recv()
            passed[j].start()
        copy(0, sibling, me).wait_recv()
        for j, chip in enumerate(chips):
            copy(4 + j, (*chip, 1 - c), me).wait_recv()
        for cp in first + passed:
            cp.wait_send()
        mine.wait()

    return pl.pallas_call(
        body, name=name,
        out_shape=jax.ShapeDtypeStruct((N_DEV * m_per, n), xs.dtype),
        in_specs=[pl.BlockSpec(memory_space=pltpu.VMEM)],
        out_specs=pl.BlockSpec(memory_space=pltpu.VMEM),
        scratch_shapes=[pltpu.SemaphoreType.DMA((7,)), pltpu.SemaphoreType.DMA((7,)), pltpu.SemaphoreType.DMA],
        compiler_params=pltpu.CompilerParams(vmem_limit_bytes=VMEM_LIMIT_BYTES),
    )(xs)


HBM_SPEC = pl.BlockSpec(memory_space=pltpu.HBM)


def _mesh_pos():
    x, y, c = lax.axis_index("x"), lax.axis_index("y"), lax.axis_index("c")
    return x, y, c, [(1 - x, y), (x, 1 - y), (1 - x, 1 - y)]


class _Phase:
    def __init__(self, key, group, after, owner, est_us, ins, out_shape, n_sems, n_local, build, buffer=None):
        self.key, self.group, self.after, self.owner, self.est_us = key, group, after, owner, est_us
        self.ins, self.out_shape, self.buffer = ins, out_shape, buffer
        self.n_sems, self.n_local, self.build = n_sems, n_local, build


def _rcopy(src, dst, send_sems, recv_sems, k, to):
    return pltpu.make_async_remote_copy(src_ref=src, dst_ref=dst, send_sem=send_sems.at[k], recv_sem=recv_sems.at[k],
                                        device_id=to, device_id_type=MESH)


def _build_fetch(in_refs, g, send_sems, recv_sems, loc_sems, sem0, loc0, rows, whole):
    (shard,) = in_refs
    x, y, c, chips = _mesh_pos()
    p_me = 2 * x + y
    locs = [pltpu.make_async_copy(shard.at[i], g.at[p_me, i], loc_sems.at[loc0 + i]) for i in range(2)] if whole else []
    sends = [_rcopy(shard.at[c, rows], g.at[p_me, c, rows], send_sems, recv_sems, sem0 + j, (*chip, c)) for j, chip in enumerate(chips)]

    def recvs():
        blks = [g.at[2 * chip[0] + chip[1], c, rows] for chip in chips]
        return [_rcopy(blk, blk, send_sems, recv_sems, sem0 + j, (*chip, c)) for j, (blk, chip) in enumerate(zip(blks, chips))]
    return sends, recvs, locs


def _build_pass(in_refs, g, send_sems, recv_sems, loc_sems, sem0, loc0, rows):
    x, y, c, chips = _mesh_pos()
    sib = (x, y, 1 - c)
    slots = [2 * chip[0] + chip[1] for chip in chips]
    sends = [_rcopy(g.at[p, c, rows], g.at[p, c, rows], send_sems, recv_sems, sem0 + j, sib) for j, p in enumerate(slots)]

    def recvs():
        return [_rcopy(g.at[p, 1 - c, rows], g.at[p, 1 - c, rows], send_sems, recv_sems, sem0 + j, sib) for j, p in enumerate(slots)]
    return sends, recvs, []


def _build_exchange(in_refs, recv, send_sems, recv_sems, loc_sems, sem0, loc0, rows):
    (dw,) = in_refs
    x, y, c, chips = _mesh_pos()
    targets = [(c, chip, c, j) for j, chip in enumerate(chips)]
    targets += [(1 - c, chip, 1 - c, 3 + j) for j, chip in enumerate([(x, y)] + chips)]
    sends = [_rcopy(dw.at[2 * chip[0] + chip[1], half, rows], recv.at[rel, rows], send_sems, recv_sems, sem0 + rel, (*chip, core))
             for half, chip, core, rel in targets]

    def recvs():
        return [_rcopy(recv.at[rel, rows], recv.at[rel, rows], send_sems, recv_sems, sem0 + rel, (x, y, 1 - c)) for rel in range(7)]
    return sends, recvs, []


def _build_swap(in_refs, g, send_sems, recv_sems, loc_sems, sem0, loc0):
    x, y, c, _ = _mesh_pos()
    sib = (x, y, 1 - c)
    sends = [_rcopy(g.at[l, c], g.at[l, c], send_sems, recv_sems, sem0 + l, sib) for l in range(g.shape[0])]

    def recvs():
        return [_rcopy(g.at[l, 1 - c], g.at[l, 1 - c], send_sems, recv_sems, sem0 + l, sib) for l in range(g.shape[0])]
    return sends, recvs, []


def _plan_refs(phases, store):
    xin, xout, alias, n_sems, n_loc, out_of = [], [], {}, 0, 0, {}
    for ph in phases:
        ph.sem0, ph.loc0 = n_sems, n_loc
        n_sems += ph.n_sems
        n_loc += ph.n_local
        ph.in0, ph.n_in = len(xin), len(ph.ins)
        xin += ph.ins
        if ph.owner not in out_of:
            out_of[ph.owner] = len(xout)
            if ph.owner == ph.key and ph.buffer is None:
                xout.append(ph.out_shape)
            else:
                buf = ph.buffer if ph.buffer is not None else store[ph.group]
                alias[len(xin)] = len(xout)
                xin.append(buf)
                xout.append(jax.ShapeDtypeStruct(buf.shape, buf.dtype))
        ph.out0 = out_of[ph.owner]
    return xin, xout, alias, max(n_sems, 1), max(n_loc, 1)


def _built(ph, xin_refs, xout_refs, sems):
    return ph.build(xin_refs[ph.in0:ph.in0 + ph.n_in], xout_refs[ph.out0], sems[0], sems[1], sems[2], ph.sem0, ph.loc0)


def _start(phases, xin_refs, xout_refs, sems):
    for ph in phases:
        sends, _, locs = _built(ph, xin_refs, xout_refs, sems)
        for cp in locs + sends:
            cp.start()


def _finish(phases, xin_refs, xout_refs, sems):
    for ph in phases:
        sends, recvs, locs = _built(ph, xin_refs, xout_refs, sems)
        for cp in recvs():
            cp.wait_recv()
        for cp in sends:
            cp.wait_send()
        for cp in locs:
            cp.wait()


class _Comm:
    def __init__(self):
        self.queue, self.store, self.n_alone = [], {}, 0

    def push(self, ph):
        self.queue.append(ph)

    def take(self, carry_us):
        taken, t = [], 0.0
        while True:
            queued = {ph.key for ph in self.queue}
            pending = queued | {ph.key for ph in taken}
            room = 1.5 * carry_us if not taken else carry_us - t
            fits = [ph for ph in self.queue if ph.after not in pending and (ph.owner == ph.key or ph.owner not in queued)
                    and ph.est_us <= room]
            if not fits:
                return taken
            ph = max(fits, key=lambda p: p.est_us)
            self.queue.remove(ph)
            taken.append(ph)
            t += ph.est_us

    def require(self, group):
        phases = [ph for ph in self.queue if ph.group == group]
        if phases:
            self.queue = [ph for ph in self.queue if ph.group != group]
            self.run_alone(phases)
        return self.store[group]

    def flush(self):
        phases, self.queue = self.queue, []
        if phases:
            self.run_alone(phases)

    def run_alone(self, phases):
        phases = [ph for ph in phases if ph.after is None] + [ph for ph in phases if ph.after is not None]
        groups, keys = [[]], set()
        for ph in phases:
            if ph.after in keys:
                groups.append([])
                keys = set()
            groups[-1].append(ph)
            keys.add(ph.key)
        xin, xout, alias, n_sems, n_loc = _plan_refs(phases, self.store)
        n_xin, n_xout = len(xin), len(xout)

        def body(*refs):
            xin_refs, xout_refs, sems = refs[:n_xin], refs[n_xin:n_xin + n_xout], refs[n_xin + n_xout:]
            for grp in groups:
                _start(grp, xin_refs, xout_refs, sems)
                _finish(grp, xin_refs, xout_refs, sems)

        self.n_alone += 1
        outs = pl.pallas_call(
            body, name=f"comm_alone_{self.n_alone}", out_shape=xout, in_specs=[HBM_SPEC] * n_xin, out_specs=[HBM_SPEC] * n_xout,
            input_output_aliases=alias,
            scratch_shapes=[pltpu.SemaphoreType.DMA((n_sems,)), pltpu.SemaphoreType.DMA((n_sems,)), pltpu.SemaphoreType.DMA((n_loc,))],
        )(*xin)
        for ph in phases:
            self.store[ph.group] = outs[ph.out0]


def _pcall(body, *, name, grid, in_specs, out_specs, out_shape, args, scratch_shapes=(), aliases=None, comm=None, carry_us=0.0):
    phases = comm.take(carry_us) if comm is not None else []
    n_in, n_out, n_scr = len(in_specs), len(out_specs), len(scratch_shapes)
    if not phases:
        return pl.pallas_call(body, name=name, grid=grid, in_specs=in_specs, out_specs=out_specs, out_shape=out_shape,
                              scratch_shapes=list(scratch_shapes), input_output_aliases=aliases or {},
                              compiler_params=_cparams(("arbitrary",) * len(grid)))(*args)
    xin, xout, xalias, n_sems, n_loc = _plan_refs(phases, comm.store)
    n_xin, n_xout = len(xin), len(xout)
    all_alias = dict(aliases or {})
    all_alias.update({n_in + i: n_out + o for i, o in xalias.items()})

    def carrier(*refs):
        ins, xin_refs = refs[:n_in], refs[n_in:n_in + n_xin]
        outs = refs[n_in + n_xin:n_in + n_xin + n_out]
        xout_refs = refs[n_in + n_xin + n_out:n_in + n_xin + n_out + n_xout]
        rest = refs[n_in + n_xin + n_out + n_xout:]
        scr, sems = rest[:n_scr], rest[n_scr:]
        pids = [pl.program_id(k) for k in range(len(grid))]
        first = functools.reduce(jnp.logical_and, [p == 0 for p in pids])
        last = functools.reduce(jnp.logical_and, [p == n - 1 for p, n in zip(pids, grid)])

        @pl.when(first)
        def _():
            _start(phases, xin_refs, xout_refs, sems)
        body(*ins, *outs, *scr)

        @pl.when(last)
        def _():
            _finish(phases, xin_refs, xout_refs, sems)

    outs = pl.pallas_call(
        carrier, name=name, grid=grid, in_specs=list(in_specs) + [HBM_SPEC] * n_xin, out_specs=list(out_specs) + [HBM_SPEC] * n_xout,
        out_shape=list(out_shape) + xout,
        scratch_shapes=list(scratch_shapes) + [pltpu.SemaphoreType.DMA((n_sems,)), pltpu.SemaphoreType.DMA((n_sems,)),
                                               pltpu.SemaphoreType.DMA((n_loc,))],
        input_output_aliases=all_alias, compiler_params=_cparams(("arbitrary",) * len(grid)))(*args, *xin)
    for ph in phases:
        comm.store[ph.group] = outs[n_out + ph.out0]
    return outs[:n_out]


FETCH_US_PER_MB = 20.4
PASS_US_PER_MB = 3.3
EXCHANGE_US_PER_MB = 14.5


FETCH_PHASE_US = 35.0
EXCHANGE_PHASE_US = 20.0


def _row_chunks(rows, est_us, phase_us):
    n = 1
    while est_us / n > phase_us and rows % (2 * n) == 0 and (rows // (2 * n)) % 16 == 0:
        n *= 2
    return [pl.ds(k * (rows // n), rows // n) for k in range(n)]


def push_gather(comm, keys_shards):
    prev = []
    for key, shard in keys_shards:
        r, c = shard.shape
        sh = shard.reshape(2, r // 2, c)
        half_mb = r // 2 * c * 2 / 1e6
        chunks = _row_chunks(r // 2, 3 * half_mb * FETCH_US_PER_MB, FETCH_PHASE_US)
        n = len(chunks)
        shape = jax.ShapeDtypeStruct((N_CHIPS, 2, r // 2, c), BF16)
        for k, rows in enumerate(chunks):
            comm.push(_Phase(("fetch", key, k), key, None, ("fetch", key, 0), 3 * half_mb * FETCH_US_PER_MB / n, [sh], shape, 3,
                             2 if k == 0 else 0, functools.partial(_build_fetch, rows=rows, whole=k == 0)))
        for ph in prev:
            comm.push(ph)
        prev = [_Phase(("pass", key, k), key, ("fetch", key, k), ("fetch", key, 0), 3 * half_mb * PASS_US_PER_MB / n + 3.0, [], shape, 3, 0,
                       functools.partial(_build_pass, rows=rows)) for k, rows in enumerate(chunks)]
    for ph in prev:
        comm.push(ph)


def push_exchange(comm, key, dw):
    _, r, c = dw.shape
    half_mb = r // 2 * c * 2 / 1e6
    chunks = _row_chunks(r // 2, 6 * half_mb * EXCHANGE_US_PER_MB, EXCHANGE_PHASE_US)
    dw5 = dw.reshape(N_CHIPS, 2, r // 2, c)
    for k, rows in enumerate(chunks):
        comm.push(_Phase(("exchange", key, k), key, None, ("exchange", key, 0), 6 * half_mb * EXCHANGE_US_PER_MB / len(chunks), [dw5],
                         jax.ShapeDtypeStruct((7, r // 2, c), BF16), 7, 0, functools.partial(_build_exchange, rows=rows)))


MM_FLOPS_PER_US = 6.0e8
MM_ROWS = 1024
MM_ROWS_WIDE = 2048


def mm_nn(a, w3, *, tn, out_dtype, name, ncb=None, cbmap=None, res=None, perm_d=1, comm=None):
    M, K = a.shape
    P, _, Ns = w3.shape
    nper = Ns // tn
    ncb = P * nper if ncb is None else ncb
    tm = max(ATTN_BLOCK * perm_d, _tile(M, MM_ROWS_WIDE, 16)) if perm_d > 1 else _tile(M, MM_ROWS, 16)
    rpb = tm // perm_d
    cbm = cbmap if cbmap is not None else (lambda j: j)
    nch = tn // LANES

    def body(*refs):
        if res is None:
            a_ref, w_ref, o_ref = refs[:3]
        else:
            a_ref, w_ref, x_ref, g_ref, o_ref, xo_ref = refs
        acc = jnp.dot(a_ref[...].astype(BF16), w_ref[...], preferred_element_type=F32)
        if perm_d > 1:
            scr = refs[3]
            for cj in range(nch):
                scr[cj] = acc[:, cj * LANES:(cj + 1) * LANES]
            for r in range(perm_d):
                for cj in range(nch):
                    o_ref[r, :, cj * LANES:(cj + 1) * LANES] = scr.at[cj][pl.ds(r, rpb, stride=perm_d), :].astype(o_ref.dtype)
        else:
            o_ref[...] = acc.astype(o_ref.dtype)
        if res is not None:
            xo_ref[...] = x_ref[...] + g_ref[...] * acc

    in_specs = [pl.BlockSpec((tm, K), lambda i, j: (i, 0)),
                pl.BlockSpec((None, K, tn), lambda i, j: (cbm(j) // nper, 0, cbm(j) % nper))]
    scratch = []
    if perm_d > 1:
        out_specs = [pl.BlockSpec((perm_d, rpb, tn), lambda i, j: (0, i, j))]
        out_shape = [jax.ShapeDtypeStruct((perm_d, M // perm_d, ncb * tn), out_dtype)]
        scratch = [pltpu.VMEM((nch, tm, LANES), F32)]
    else:
        out_specs = [pl.BlockSpec((tm, tn), lambda i, j: (i, j))]
        out_shape = [jax.ShapeDtypeStruct((M, ncb * tn), out_dtype)]
    args = [a, w3]
    if res is not None:
        in_specs += [pl.BlockSpec((tm, tn), lambda i, j: (i, j)), pl.BlockSpec((1, tn), lambda i, j: (0, j))]
        out_specs.append(pl.BlockSpec((tm, tn), lambda i, j: (i, j)))
        out_shape.append(jax.ShapeDtypeStruct((M, ncb * tn), F32))
        args += [res[0], res[1]]
    outs = _pcall(body, name=name, grid=(M // tm, ncb), in_specs=in_specs, out_specs=out_specs, out_shape=out_shape, args=args,
                  scratch_shapes=scratch, comm=comm, carry_us=2.0 * M * K * ncb * tn / MM_FLOPS_PER_US)
    if perm_d > 1:
        return outs[0].reshape(M, ncb * tn)
    return outs[0] if res is None else (outs[0], outs[1])


def permute_rows(x, d, name):
    S, C = x.shape
    R = ATTN_BLOCK * d
    ct = _tile(C, 256, LANES)
    nch = ct // LANES

    def body(x_ref, o_ref, scr):
        xv = x_ref[...].astype(F32)
        for cj in range(nch):
            scr[cj] = xv[:, cj * LANES:(cj + 1) * LANES]
        for r in range(d):
            for cj in range(nch):
                o_ref[r, :, cj * LANES:(cj + 1) * LANES] = scr.at[cj][pl.ds(r, ATTN_BLOCK, stride=d), :].astype(o_ref.dtype)

    out = pl.pallas_call(body, name=name, grid=(S // R, C // ct), in_specs=[pl.BlockSpec((R, ct), lambda i, j: (i, j))],
                         out_specs=pl.BlockSpec((d, ATTN_BLOCK, ct), lambda i, j: (0, i, j)),
                         out_shape=jax.ShapeDtypeStruct((d, S // d, C), x.dtype), scratch_shapes=[pltpu.VMEM((nch, R, LANES), F32)],
                         compiler_params=_cparams(("parallel", "parallel")))(x)
    return out.reshape(S, C)


def unpermute_rows(ps, d, name, into=None, total_cols=None, colmap=None):
    ps = list(ps) if isinstance(ps, (list, tuple)) else [ps]
    n_p = len(ps)
    p = ps[0]
    S, C = p.shape
    rpb = max(ATTN_BLOCK, 512 // d)
    R = rpb * d
    ct = _tile(C, 256, LANES)
    nch = ct // LANES
    total_cols = C if total_cols is None else total_cols
    cm = colmap if colmap is not None else (lambda j: j)

    def body(*refs):
        p_refs, o_ref, scr = refs[:n_p], refs[-2], refs[-1]

        def summed(idx):
            return functools.reduce(lambda a, b: a + b, [r[idx].astype(F32) for r in p_refs])
        if d == 1:
            o_ref[...] = summed(0).astype(o_ref.dtype)
            return
        for r in range(d):
            for cj in range(nch):
                scr.at[cj][pl.ds(r, rpb, stride=d), :] = summed((r, slice(None), slice(cj * LANES, (cj + 1) * LANES)))
        for cj in range(nch):
            o_ref[:, cj * LANES:(cj + 1) * LANES] = scr[cj].astype(o_ref.dtype)

    in_specs = [pl.BlockSpec((d, rpb, ct), lambda i, j: (0, i, j))] * n_p
    args = [a.reshape(d, S // d, C) for a in ps]
    aliases = {}
    if into is not None:
        in_specs.append(pl.BlockSpec(memory_space=pl.ANY))
        args.append(into)
        aliases = {n_p: 0}
    return pl.pallas_call(body, name=name, grid=(S // R, C // ct), in_specs=in_specs,
                          out_specs=pl.BlockSpec((R, ct), lambda i, j: (i, cm(j))),
                          out_shape=jax.ShapeDtypeStruct((S, total_cols), p.dtype), scratch_shapes=[pltpu.VMEM((nch, R, LANES), F32)],
                          input_output_aliases=aliases, compiler_params=_cparams(("parallel", "parallel")))(*args)


def mm_nt(g3, w3, *, tn, tk, out_dtype, name, gmap=None, comm=None):
    _, M, _ = g3.shape
    P, K, Ns = w3.shape
    nper = Ns // tn
    ns = P * nper
    tm = _tile(M, MM_ROWS_WIDE, 16)
    gm = gmap if gmap is not None else (lambda s: (0, s))

    def body(g_ref, w_ref, o_ref, acc):
        s = pl.program_id(2)

        @pl.when(s == 0)
        def _():
            acc[...] = jnp.zeros_like(acc)
        acc[...] += lax.dot_general(g_ref[...].astype(BF16), w_ref[...], (((1,), (1,)), ((), ())), preferred_element_type=F32)

        @pl.when(s == ns - 1)
        def _():
            o_ref[...] = acc[...].astype(o_ref.dtype)

    return _pcall(
        body, name=name, grid=(M // tm, K // tk, ns),
        in_specs=[pl.BlockSpec((None, tm, tn), lambda i, kj, s: (gm(s)[0], i, gm(s)[1])),
                  pl.BlockSpec((None, tk, tn), lambda i, kj, s: (s // nper, kj, s % nper))],
        out_specs=[pl.BlockSpec((tm, tk), lambda i, kj, s: (i, kj))],
        out_shape=[jax.ShapeDtypeStruct((M, K), out_dtype)], args=[g3, w3],
        scratch_shapes=[pltpu.VMEM((tm, tk), F32)], comm=comm, carry_us=2.0 * M * K * P * Ns / MM_FLOPS_PER_US)[0]


def mm_tn(a, g3, wshape, *, tn, tk, name, gmap=None, comm=None):
    M, K = a.shape
    P, _, Ns = wshape
    nper = Ns // tn
    ns = P * nper
    tm = _tile(M, MM_ROWS_WIDE, 16)
    nm = M // tm
    gm = gmap if gmap is not None else (lambda s: (0, s))

    def body(a_ref, g_ref, o_ref, acc):
        mi = pl.program_id(2)

        @pl.when(mi == 0)
        def _():
            acc[...] = jnp.zeros_like(acc)
        acc[...] += lax.dot_general(a_ref[...].astype(BF16), g_ref[...].astype(BF16), (((0,), (0,)), ((), ())), preferred_element_type=F32)

        @pl.when(mi == nm - 1)
        def _():
            o_ref[...] = acc[...].astype(o_ref.dtype)

    return _pcall(
        body, name=name, grid=(ns, K // tk, nm),
        in_specs=[pl.BlockSpec((tm, tk), lambda s, kj, mi: (mi, kj)),
                  pl.BlockSpec((None, tm, tn), lambda s, kj, mi: (gm(s)[0], mi, gm(s)[1]))],
        out_specs=[pl.BlockSpec((None, tk, tn), lambda s, kj, mi: (s // nper, kj, s % nper))],
        out_shape=[jax.ShapeDtypeStruct((P, K, Ns), BF16)], args=[a, g3],
        scratch_shapes=[pltpu.VMEM((tk, tn), F32)], comm=comm, carry_us=2.0 * M * K * P * Ns / MM_FLOPS_PER_US)[0]


def _vspec(d):
    return pl.BlockSpec((1, d), lambda i: (0, 0))


NORM_US_PER_ELEM = 12.0 / (4096 * 1024)


def norm_mod(x, g, sh, sc, name, comm=None):
    S, D = x.shape
    tm = _tile(S, 512, 16)

    def body(x_ref, g_ref, sh_ref, sc_ref, o_ref):
        xv = x_ref[...]
        r = lax.rsqrt(jnp.mean(xv * xv, axis=-1, keepdims=True) + EPS)
        o_ref[...] = ((xv * r) * g_ref[...] * (1.0 + sc_ref[...]) + sh_ref[...]).astype(o_ref.dtype)

    return _pcall(body, name=name, grid=(S // tm,),
                  in_specs=[pl.BlockSpec((tm, D), lambda i: (i, 0)), _vspec(D), _vspec(D), _vspec(D)],
                  out_specs=[pl.BlockSpec((tm, D), lambda i: (i, 0))], out_shape=[jax.ShapeDtypeStruct((S, D), BF16)],
                  args=[x, g, sh, sc], comm=comm, carry_us=NORM_US_PER_ELEM * S * D)[0]


def _gate_outputs(dx, gate_ref, out_ref, dout_ref):
    dout_ref[...] = (gate_ref[...] * dx).astype(dout_ref.dtype)
    return jnp.sum(dx * out_ref[...].astype(F32), axis=0, keepdims=True)


NORM_BWD_US_PER_ELEM = 28.0 / (4096 * 1024)


def norm_mod_bwd(dh, x, dres, g, sc, name, below=None, comm=None):
    S, D = x.shape
    tm = _tile(S, 256, 16)

    def body(dh_ref, x_ref, dr_ref, g_ref, sc_ref, *rest):
        dx_ref, sums_ref = (rest[2], rest[3]) if below is not None else (rest[0], rest[1])
        xv = x_ref[...]
        dhv = dh_ref[...].astype(F32)
        r = lax.rsqrt(jnp.mean(xv * xv, axis=-1, keepdims=True) + EPS)
        xn = xv * r
        one_sc = 1.0 + sc_ref[...]
        dxn = dhv * g_ref[...] * one_sc
        dx = r * (dxn - xn * jnp.mean(dxn * xn, axis=-1, keepdims=True)) + dr_ref[...]
        dx_ref[...] = dx
        rows = [jnp.sum(dhv, axis=0, keepdims=True), jnp.sum(dhv * xn * g_ref[...], axis=0, keepdims=True),
                jnp.sum(dhv * one_sc * xn, axis=0, keepdims=True)]
        if below is not None:
            rows.append(_gate_outputs(dx, rest[0], rest[1], rest[4]))
        part = jnp.concatenate(rows + [jnp.zeros((8 - len(rows), D), F32)], axis=0)

        @pl.when(pl.program_id(0) == 0)
        def _():
            sums_ref[...] = jnp.zeros_like(sums_ref)
        sums_ref[...] += part

    row = pl.BlockSpec((tm, D), lambda i: (i, 0))
    in_specs, args = [row, row, row, _vspec(D), _vspec(D)], [dh, x, dres, g, sc]
    out_specs = [row, pl.BlockSpec((8, D), lambda i: (0, 0))]
    out_shape = [jax.ShapeDtypeStruct((S, D), F32), jax.ShapeDtypeStruct((8, D), F32)]
    if below is not None:
        in_specs += [_vspec(D), row]
        args += [below[0], below[1]]
        out_specs.append(row)
        out_shape.append(jax.ShapeDtypeStruct((S, D), BF16))
    return _pcall(body, name=name, grid=(S // tm,), in_specs=in_specs, out_specs=out_specs, out_shape=out_shape, args=args,
                  comm=comm, carry_us=NORM_BWD_US_PER_ELEM * S * D)


def loss_fwd_bwd(x, g, target, below, name):
    S, D = x.shape
    tm = _tile(S, 256, 16)

    def body(x_ref, g_ref, t_ref, gate_ref, out_ref, dx_ref, sums_ref, dout_ref):
        xv = x_ref[...]
        r = lax.rsqrt(jnp.mean(xv * xv, axis=-1, keepdims=True) + EPS)
        xn = xv * r
        err = xn * g_ref[...] - t_ref[...]
        dy = err * (1.0 / D)
        dxn = dy * g_ref[...]
        dx = r * (dxn - xn * jnp.mean(dxn * xn, axis=-1, keepdims=True))
        dx_ref[...] = dx
        part = jnp.concatenate([jnp.sum(dy * xn, axis=0, keepdims=True), jnp.sum(err * err, axis=0, keepdims=True),
                                _gate_outputs(dx, gate_ref, out_ref, dout_ref), jnp.zeros((5, D), F32)], axis=0)

        @pl.when(pl.program_id(0) == 0)
        def _():
            sums_ref[...] = jnp.zeros_like(sums_ref)
        sums_ref[...] += part

    row = pl.BlockSpec((tm, D), lambda i: (i, 0))
    return pl.pallas_call(body, name=name, grid=(S // tm,), in_specs=[row, _vspec(D), row, _vspec(D), row],
                          out_specs=[row, pl.BlockSpec((8, D), lambda i: (0, 0)), row],
                          out_shape=[jax.ShapeDtypeStruct((S, D), F32), jax.ShapeDtypeStruct((8, D), F32), jax.ShapeDtypeStruct((S, D), BF16)],
                          compiler_params=_cparams(("arbitrary",)))(x, g, target, below[0], below[1])


def pool_fwd(u, wgrp, scale, name):
    S, D = u.shape
    G = len(POOL_WINDOWS)
    C = D // G
    tm = _tile(S, 256, 16)
    hb = tm // HALO

    def body(up_ref, uc_ref, w_ref, sc_ref, p_ref, z_ref, y_ref):
        i = pl.program_id(0)
        prev = jnp.where(i > 0, up_ref[...], 0.0)
        ext = jnp.concatenate([prev, uc_ref[...]], axis=0)
        t = i * tm + lax.broadcasted_iota(jnp.int32, (tm, 1), 0)
        for gi, w in enumerate(POOL_WINDOWS):
            cs = slice(gi * C, (gi + 1) * C)
            e = ext[:, cs]
            s, k = e, 1
            while k < w:
                s = s + pltpu.roll(s, k, 0)
                k *= 2
            cnt = jnp.minimum(t + 1, w).astype(F32)
            pooled = (s[HALO:] / cnt - e[HALO:]).astype(BF16)
            p_ref[:, cs] = pooled
            z = jnp.dot(pooled, w_ref[:, gi].reshape(C, C), preferred_element_type=F32)
            z_ref[:, cs] = z.astype(BF16)
            y_ref[:, cs] = (z * sc_ref[:, cs]).astype(BF16)

    row = pl.BlockSpec((tm, D), lambda i: (i, 0))
    return pl.pallas_call(
        body, name=name, grid=(S // tm,),
        in_specs=[pl.BlockSpec((HALO, D), lambda i: (jnp.maximum(i * hb - 1, 0), 0)), row,
                  pl.BlockSpec(wgrp.shape, lambda i: (0, 0, 0, 0)), _vspec(D)],
        out_specs=[row, row, row], out_shape=[jax.ShapeDtypeStruct((S, D), BF16)] * 3,
        compiler_params=_cparams(("parallel",)))(u, u, wgrp, scale)


def pool_bwd(dys, z, pooled, wgrp, scale, name):
    S, D = dys.shape
    G = len(POOL_WINDOWS)
    C = D // G
    tm = _tile(S, 256, 16)
    hb = tm // HALO
    nt = S // tm
    n_ext = tm + HALO

    def body(dc_ref, dn_ref, z_ref, p_ref, w_ref, sc_ref, du_ref, dw_ref, sums_ref):
        i = pl.program_id(0)

        @pl.when(i == 0)
        def _():
            dw_ref[...] = jnp.zeros_like(dw_ref)
            sums_ref[...] = jnp.zeros_like(sums_ref)
        dyc = dc_ref[...].astype(F32)
        nxt = jnp.where(i < nt - 1, dn_ref[...].astype(F32), 0.0)
        ext = jnp.concatenate([dyc, nxt], axis=0)
        sums_ref[...] += jnp.concatenate([jnp.sum(dyc * z_ref[...].astype(F32), axis=0, keepdims=True), jnp.zeros((7, D), F32)], axis=0)
        t = i * tm + lax.broadcasted_iota(jnp.int32, (n_ext, 1), 0)
        for gi, w in enumerate(POOL_WINDOWS):
            cs = slice(gi * C, (gi + 1) * C)
            wg = w_ref[:, gi].reshape(C, C)
            dz = (ext[:, cs] * sc_ref[:, cs]).astype(BF16)
            dpool = lax.dot_general(dz, wg, (((1,), (1,)), ((), ())), preferred_element_type=F32)
            dw_ref[gi] += lax.dot_general(p_ref[:, cs], dz[:tm], (((0,), (0,)), ((), ())), preferred_element_type=F32)
            cnt = jnp.minimum(t + 1, w).astype(F32)
            s, k = dpool / cnt, 1
            while k < w:
                s = s + pltpu.roll(s, n_ext - k, 0)
                k *= 2
            du_ref[:, cs] = (s[:tm] - dpool[:tm]).astype(BF16)

    row = pl.BlockSpec((tm, D), lambda i: (i, 0))
    return pl.pallas_call(
        body, name=name, grid=(nt,),
        in_specs=[row, pl.BlockSpec((HALO, D), lambda i: (jnp.minimum((i + 1) * hb, S // HALO - 1), 0)), row, row,
                  pl.BlockSpec(wgrp.shape, lambda i: (0, 0, 0, 0)), _vspec(D)],
        out_specs=[row, pl.BlockSpec((G, C, C), lambda i: (0, 0, 0)), pl.BlockSpec((8, D), lambda i: (0, 0))],
        out_shape=[jax.ShapeDtypeStruct((S, D), BF16), jax.ShapeDtypeStruct((G, C, C), F32), jax.ShapeDtypeStruct((8, D), F32)],
        compiler_params=_cparams(("arbitrary",)))(dys, dys, z, pooled, wgrp, scale)


FFN_ACT_BWD_US_PER_ELEM = 84.0 / (4096 * 2816)


def ffn_up_act(h, w3, conv_w, conv_b, name, comm=None):
    S, D = h.shape
    P, _, Ns = w3.shape
    nh = P // 2
    tm = _tile(S, MM_ROWS, 16)

    def body(h_ref, w_ref, cw_ref, cb_ref, hu_ref, g_ref, c_ref, stash, halo):
        i, j = pl.program_id(0), pl.program_id(1)
        acc = jnp.dot(h_ref[...], w_ref[...], preferred_element_type=F32).astype(BF16)
        hu_ref[...] = acc

        @pl.when(j < nh)
        def _():
            stash[j] = acc.astype(F32)

        @pl.when(j >= nh)
        def _():
            c = j - nh
            a = stash[c]
            ext = jnp.concatenate([jnp.where(i > 0, halo[c], 0.0), a], axis=0)
            conv = cb_ref[...] + pltpu.roll(ext, 2, 0) * cw_ref[0:1, :] + pltpu.roll(ext, 1, 0) * cw_ref[1:2, :] + ext * cw_ref[2:3, :]
            conv = conv[HALO:]
            c_ref[...] = conv.astype(c_ref.dtype)
            g_ref[...] = (conv * _sigmoid(conv) * acc.astype(F32)).astype(g_ref.dtype)
            halo[c] = a[tm - HALO:]

    def gcol(j):
        return jnp.maximum(j - nh, 0)

    gspec = pl.BlockSpec((tm, Ns), lambda i, j: (i, gcol(j)))
    return _pcall(
        body, name=name, grid=(S // tm, P),
        in_specs=[pl.BlockSpec((tm, D), lambda i, j: (i, 0)), pl.BlockSpec((None, D, Ns), lambda i, j: (j, 0, 0)),
                  pl.BlockSpec((3, Ns), lambda i, j: (0, gcol(j))), pl.BlockSpec((1, Ns), lambda i, j: (0, gcol(j)))],
        out_specs=[pl.BlockSpec((tm, Ns), lambda i, j: (i, j)), gspec, gspec],
        out_shape=[jax.ShapeDtypeStruct((S, P * Ns), BF16), jax.ShapeDtypeStruct((S, nh * Ns), BF16), jax.ShapeDtypeStruct((S, nh * Ns), BF16)],
        scratch_shapes=[pltpu.VMEM((nh, tm, Ns), F32), pltpu.VMEM((nh, HALO, Ns), F32)],
        args=[h, w3, conv_w, conv_b], comm=comm, carry_us=2.0 * S * D * P * Ns / MM_FLOPS_PER_US)


def ffn_act_bwd(dout, w_down, hu, conv, conv_w, name, comm=None):
    S, D = dout.shape
    F = w_down.shape[1]
    tm = _tile(S, 256, 16)
    tn = _tile(F, 1408, LANES)
    nb = F // tn
    hb = tm // HALO
    nt = S // tm
    n_ext = tm + HALO
    nt_dims = (((1,), (1,)), ((), ()))

    def body(dc_ref, dn_ref, wd_ref, cc_ref, cn_ref, ac_ref, vc_ref, vn_ref, w_ref, o_ref, sums_ref):
        i = pl.program_id(1)

        @pl.when(i == 0)
        def _():
            sums_ref[...] = jnp.zeros_like(sums_ref)
        cv = jnp.concatenate([cc_ref[...], cn_ref[...]], axis=0).astype(F32)
        v_ext = jnp.concatenate([vc_ref[...], vn_ref[...]], axis=0).astype(F32)
        g_cur = lax.dot_general(dc_ref[...], wd_ref[...], nt_dims, preferred_element_type=F32)
        g_nxt = lax.dot_general(dn_ref[...], wd_ref[...], nt_dims, preferred_element_type=F32)
        g_ext = jnp.concatenate([g_cur, jnp.where(i < nt - 1, g_nxt, 0.0)], axis=0)
        w0, w1, w2 = w_ref[0:1, :], w_ref[1:2, :], w_ref[2:3, :]
        sig = _sigmoid(cv)
        silu = cv * sig
        dconv = g_ext * v_ext * (sig + silu * (1.0 - sig))
        d_p1, d_p2 = pltpu.roll(dconv, n_ext - 1, 0), pltpu.roll(dconv, n_ext - 2, 0)
        da = dconv * w2 + d_p1 * w1 + d_p2 * w0
        o_ref[0] = da[:tm].astype(o_ref.dtype)
        o_ref[1] = (g_ext * silu)[:tm].astype(o_ref.dtype)
        a = ac_ref[...].astype(F32)
        part = jnp.concatenate([jnp.sum(a * d_p2[:tm], axis=0, keepdims=True), jnp.sum(a * d_p1[:tm], axis=0, keepdims=True),
                                jnp.sum(a * dconv[:tm], axis=0, keepdims=True), jnp.sum(dconv[:tm], axis=0, keepdims=True),
                                jnp.zeros((4, tn), F32)], axis=0)
        sums_ref[...] += part

    def nxt(i):
        return jnp.minimum((i + 1) * hb, S // HALO - 1)

    return _pcall(
        body, name=name, grid=(nb, nt),
        in_specs=[pl.BlockSpec((tm, D), lambda j, i: (i, 0)), pl.BlockSpec((HALO, D), lambda j, i: (nxt(i), 0)),
                  pl.BlockSpec((None, tn, D), lambda j, i: (0, j, 0)),
                  pl.BlockSpec((tm, tn), lambda j, i: (i, j)), pl.BlockSpec((HALO, tn), lambda j, i: (nxt(i), j)),
                  pl.BlockSpec((tm, tn), lambda j, i: (i, j)),
                  pl.BlockSpec((tm, tn), lambda j, i: (i, j + nb)), pl.BlockSpec((HALO, tn), lambda j, i: (nxt(i), j + nb)),
                  pl.BlockSpec((3, tn), lambda j, i: (0, j))],
        out_specs=[pl.BlockSpec((2, tm, tn), lambda j, i: (0, i, j)), pl.BlockSpec((8, tn), lambda j, i: (0, j))],
        out_shape=[jax.ShapeDtypeStruct((2, S, F), BF16), jax.ShapeDtypeStruct((8, F), F32)],
        args=[dout, dout, w_down, conv, conv, hu, hu, hu, conv_w], comm=comm, carry_us=FFN_ACT_BWD_US_PER_ELEM * S * F)


def _head_expander(n_heads, da):
    e = np.zeros((LANES, da), np.float32)
    for h in range(n_heads):
        e[h, h * HEAD_DIM:(h + 1) * HEAD_DIM] = 1.0
    return jnp.asarray(e, BF16)


def _split_dot(v, e, dims):
    hi = v.astype(BF16)
    lo = (v - hi.astype(F32)).astype(BF16)
    return (lax.dot_general(hi, e, dims, preferred_element_type=F32) + lax.dot_general(lo, e, dims, preferred_element_type=F32))


def _lane_col(tile, h):
    lane = lax.broadcasted_iota(jnp.int32, tile.shape, 1)
    return jnp.sum(jnp.where(lane == h, tile, 0.0), axis=1, keepdims=True)


ATTN_US_PER_ELEM = (80.0 / (4096 * 1024), 230.0 / (4096 * 1024))


def attn_branch_fwd(q, kv, gi, slopes, name, comm=None):
    S, DA = q.shape
    H = DA // HEAD_DIM
    window, d = BRANCHES[gi]
    n_steps = window // d
    blk = ATTN_BLOCK
    assert n_steps == blk and (S // d) % blk == 0
    nbs = S // d // blk
    scale = HEAD_DIM ** -0.5

    def body(q_ref, kp_ref, kc_ref, vp_ref, vc_ref, o_ref, l_ref, s_scr, p_scr):
        jb = pl.program_id(1)
        row = lax.broadcasted_iota(jnp.int32, (blk, 2 * blk), 0)
        col = lax.broadcasted_iota(jnp.int32, (blk, 2 * blk), 1)
        delta = row + blk - col
        valid = (delta >= 0) & (delta <= n_steps) & ((col >= blk) | (jb > 0))
        dist = jnp.where(valid, (delta * d).astype(F32), -NEG)
        lane = lax.broadcasted_iota(jnp.int32, (blk, LANES), 1)
        ltile = jnp.zeros((blk, LANES), F32)
        for h in range(H):
            hs = slice(h * HEAD_DIM, (h + 1) * HEAD_DIM)
            k2 = jnp.concatenate([kp_ref[:, hs], kc_ref[:, hs]], axis=0)
            s_scr[h] = lax.dot_general(q_ref[:, hs], k2, (((1,), (1,)), ((), ())), preferred_element_type=F32)
        for h in range(H):
            s = s_scr[h] * scale - float(slopes[h]) * dist
            m = jnp.max(s, axis=-1, keepdims=True)
            p = jnp.exp(s - m)
            l = jnp.sum(p, axis=-1, keepdims=True)
            p_scr[h] = (p / l).astype(BF16)
            ltile = jnp.where(lane == h, m + jnp.log(l), ltile)
        for h in range(H):
            hs = slice(h * HEAD_DIM, (h + 1) * HEAD_DIM)
            v2 = jnp.concatenate([vp_ref[:, hs], vc_ref[:, hs]], axis=0)
            o_ref[:, hs] = jnp.dot(p_scr[h], v2, preferred_element_type=F32).astype(o_ref.dtype)
        l_ref[...] = ltile

    def cur(width, off):
        return pl.BlockSpec((blk, width), lambda r, jb: (r * nbs + jb, off))

    def prv(width, off):
        return pl.BlockSpec((blk, width), lambda r, jb: (r * nbs + jnp.maximum(jb - 1, 0), off))

    return _pcall(
        body, name=name, grid=(d, nbs),
        in_specs=[cur(DA, 0), prv(DA, 0), cur(DA, 0), prv(DA, 1), cur(DA, 1)],
        out_specs=[cur(DA, 0), cur(LANES, 0)],
        out_shape=[jax.ShapeDtypeStruct((S, DA), BF16), jax.ShapeDtypeStruct((S, LANES), F32)],
        scratch_shapes=[pltpu.VMEM((H, blk, 2 * blk), F32), pltpu.VMEM((H, blk, 2 * blk), BF16)],
        args=[q, kv, kv, kv, kv], comm=comm, carry_us=ATTN_US_PER_ELEM[0] * S * DA)


def attn_combine(os_, lses, name):
    S, DA = os_[0].shape
    H = DA // HEAD_DIM
    tm = _tile(S, 256, 16)
    expander = _head_expander(H, DA)
    nbr = len(os_)

    def body(*refs):
        o_refs, l_refs, e_ref = refs[:nbr], refs[nbr:2 * nbr], refs[2 * nbr]
        out_ref, lse_ref = refs[2 * nbr + 1:]
        ls = [r[...] for r in l_refs]
        lmax = functools.reduce(jnp.maximum, ls)
        es = [jnp.exp(l - lmax) for l in ls]
        den = functools.reduce(lambda a, b: a + b, es)
        lse_ref[...] = lmax + jnp.log(den)
        acc = jnp.zeros((tm, DA), F32)
        for e, o_ref in zip(es, o_refs):
            acc = acc + _split_dot(e / den, e_ref[...], (((1,), (0,)), ((), ()))) * o_ref[...]
        out_ref[...] = acc.astype(out_ref.dtype)

    row = pl.BlockSpec((tm, DA), lambda i: (i, 0))
    lrow = pl.BlockSpec((tm, LANES), lambda i: (i, 0))
    return pl.pallas_call(
        body, name=name, grid=(S // tm,),
        in_specs=[row] * nbr + [lrow] * nbr + [pl.BlockSpec((LANES, DA), lambda i: (0, 0))],
        out_specs=[row, lrow], out_shape=[jax.ShapeDtypeStruct((S, DA), BF16), jax.ShapeDtypeStruct((S, LANES), F32)],
        compiler_params=_cparams(("parallel",)))(*os_, *lses, expander)


def attn_delta(do, o, name):
    S, DA = o.shape
    H = DA // HEAD_DIM
    tm = _tile(S, 512, 16)
    expander = _head_expander(H, DA)

    def body(do_ref, o_ref, e_ref, d_ref):
        prod = do_ref[...].astype(F32) * o_ref[...].astype(F32)
        d_ref[...] = _split_dot(prod, e_ref[...], (((1,), (1,)), ((), ())))

    row = pl.BlockSpec((tm, DA), lambda i: (i, 0))
    return pl.pallas_call(body, name=name, grid=(S // tm,), in_specs=[row, row, pl.BlockSpec((LANES, DA), lambda i: (0, 0))],
                          out_specs=pl.BlockSpec((tm, LANES), lambda i: (i, 0)), out_shape=jax.ShapeDtypeStruct((S, LANES), F32),
                          compiler_params=_cparams(("parallel",)))(do, o, expander)


def attn_branch_bwd(q, kv, do, lse, dlt, gi, slopes, name, out_cols=None, comm=None):
    S, DA = q.shape
    H = DA // HEAD_DIM
    window, d = BRANCHES[gi]
    n_steps = window // d
    blk = ATTN_BLOCK
    nbs = S // d // blk
    scale = HEAD_DIM ** -0.5
    nt, tn = (((1,), (1,)), ((), ())), (((0,), (0,)), ((), ()))

    def body(*refs):
        k_ref, v_ref, qc_ref, qn_ref, doc_ref, don_ref, lc_ref, ln_ref, dc_ref, dn_ref = refs[:10]
        dq_ref, dkv_ref, carry, s_scr, dp_scr, p_scr, ds_scr = refs[-7:]
        kb = pl.program_id(1)

        @pl.when(kb == 0)
        def _():
            carry[...] = jnp.zeros_like(carry)
        row = lax.broadcasted_iota(jnp.int32, (2 * blk, blk), 0)
        col = lax.broadcasted_iota(jnp.int32, (2 * blk, blk), 1)
        delta = row - col
        valid = (delta >= 0) & (delta <= n_steps) & ((row < blk) | (kb < nbs - 1))
        dist = jnp.where(valid, (delta * d).astype(F32), -NEG)
        l2 = jnp.concatenate([lc_ref[...], ln_ref[...]], axis=0)
        d2 = jnp.concatenate([dc_ref[...], dn_ref[...]], axis=0)
        for h in range(H):
            hs = slice(h * HEAD_DIM, (h + 1) * HEAD_DIM)
            q2 = jnp.concatenate([qc_ref[:, hs], qn_ref[:, hs]], axis=0)
            do2 = jnp.concatenate([doc_ref[:, hs], don_ref[:, hs]], axis=0)
            s_scr[h] = lax.dot_general(q2, k_ref[:, hs], nt, preferred_element_type=F32)
            dp_scr[h] = lax.dot_general(do2, v_ref[:, hs], nt, preferred_element_type=F32)
        for h in range(H):
            p = jnp.exp(s_scr[h] * scale - float(slopes[h]) * dist - _lane_col(l2, h))
            p_scr[h] = p.astype(BF16)
            ds_scr[h] = (p * (dp_scr[h] - _lane_col(d2, h))).astype(BF16)
        for h in range(H):
            hs = slice(h * HEAD_DIM, (h + 1) * HEAD_DIM)
            vs = slice(DA + h * HEAD_DIM, DA + (h + 1) * HEAD_DIM)
            q2 = jnp.concatenate([qc_ref[:, hs], qn_ref[:, hs]], axis=0)
            do2 = jnp.concatenate([doc_ref[:, hs], don_ref[:, hs]], axis=0)
            dvh = lax.dot_general(p_scr[h], do2, tn, preferred_element_type=F32)
            dkh = lax.dot_general(ds_scr[h], q2, tn, preferred_element_type=F32) * scale
            dq2 = jnp.dot(ds_scr[h], k_ref[:, hs], preferred_element_type=F32) * scale
            dq_ref[:, hs] = (carry[:, hs] + dq2[:blk]).astype(dq_ref.dtype)
            carry[:, hs] = dq2[blk:]
            dkv_ref[:, hs] = dkh.astype(dkv_ref.dtype)
            dkv_ref[:, vs] = dvh.astype(dkv_ref.dtype)

    def cur(width, off):
        return pl.BlockSpec((blk, width), lambda r, kb: (r * nbs + kb, off))

    def nxt(width, off):
        return pl.BlockSpec((blk, width), lambda r, kb: (r * nbs + jnp.minimum(kb + 1, nbs - 1), off))

    in_specs = [cur(DA, 0), cur(DA, 1), cur(DA, 0), nxt(DA, 0), cur(DA, 0), nxt(DA, 0),
                cur(LANES, 0), nxt(LANES, 0), cur(LANES, 0), nxt(LANES, 0)]
    args = [kv, kv, q, q, do, do, lse, lse, dlt, dlt]
    return _pcall(
        body, name=name, grid=(d, nbs), in_specs=in_specs, out_specs=[cur(DA, 0), cur(2 * DA, 0)],
        out_shape=[jax.ShapeDtypeStruct((S, out_cols or DA), BF16), jax.ShapeDtypeStruct((S, 2 * DA), BF16)],
        scratch_shapes=[pltpu.VMEM((blk, DA), F32), pltpu.VMEM((H, 2 * blk, blk), F32), pltpu.VMEM((H, 2 * blk, blk), F32),
                        pltpu.VMEM((H, 2 * blk, blk), BF16), pltpu.VMEM((H, 2 * blk, blk), BF16)],
        args=args, comm=comm, carry_us=ATTN_US_PER_ELEM[1] * S * DA)


def ada_project(c16, w3, b3, name):
    L, D, Ns = w3.shape
    tn = _tile(Ns, 512, LANES)

    def body(c_ref, w_ref, b_ref, o_ref):
        cv = c_ref[...]
        cond = (cv * _sigmoid(cv)).astype(BF16)
        o_ref[...] = jnp.dot(cond, w_ref[...].astype(BF16), preferred_element_type=F32) + b_ref[...]

    return pl.pallas_call(
        body, name=name, grid=(L, Ns // tn),
        in_specs=[pl.BlockSpec((16, D), lambda l, j: (0, 0)), pl.BlockSpec((None, D, tn), lambda l, j: (l, 0, j)),
                  pl.BlockSpec((None, 1, tn), lambda l, j: (l, 0, j))],
        out_specs=pl.BlockSpec((None, 16, tn), lambda l, j: (l, 0, j)), out_shape=jax.ShapeDtypeStruct((L, 16, Ns), F32),
        compiler_params=_cparams(("parallel", "parallel")))(c16, w3, b3)


def _adamw(w, g, m, v):
    m = B1 * m + (1.0 - B1) * g
    v = B2 * v + (1.0 - B2) * (g * g)
    m_hat = m / (1.0 - B1 ** STEP)
    v_hat = v / (1.0 - B2 ** STEP)
    delta = -LR * (m_hat / (jnp.sqrt(v_hat) + ADAM_EPS) + WD * w)
    return delta, m, v


def ada_grad_adamw(c16, d3, w3, m3, v3, name):
    L, D, Ns = w3.shape
    tk = _tile(D, 256, 8)

    def body(c_ref, d_ref, w_ref, m_ref, v_ref, g_out, dl_out, m_out, v_out):
        cv = c_ref[...]
        cond = (cv * _sigmoid(cv)).astype(BF16)
        g = lax.dot_general(cond, d_ref[...].astype(BF16), (((0,), (0,)), ((), ())), preferred_element_type=F32)
        g_out[...] = g
        dl_out[...], m_out[...], v_out[...] = _adamw(w_ref[...], g, m_ref[...], v_ref[...])

    wspec = pl.BlockSpec((None, tk, Ns), lambda l, kj: (l, kj, 0))
    return pl.pallas_call(
        body, name=name, grid=(L, D // tk),
        in_specs=[pl.BlockSpec((16, tk), lambda l, kj: (0, kj)), pl.BlockSpec((None, 16, Ns), lambda l, kj: (l, 0, 0)), wspec, wspec, wspec],
        out_specs=[wspec] * 4, out_shape=[jax.ShapeDtypeStruct((L, D, Ns), F32)] * 4,
        compiler_params=_cparams(("parallel", "parallel")))(c16, d3, w3, m3, v3)


def adamw(w, g, m, v, name):
    R, C = w.shape
    tr = _tile(R, 256, 8)

    def body(w_ref, g_ref, m_ref, v_ref, g_out, dl_out, m_out, v_out):
        g = g_ref[...]
        g_out[...] = g
        dl_out[...], m_out[...], v_out[...] = _adamw(w_ref[...], g, m_ref[...], v_ref[...])

    spec = pl.BlockSpec((tr, C), lambda i: (i, 0))
    return pl.pallas_call(body, name=name, grid=(R // tr,), in_specs=[spec] * 4, out_specs=[spec] * 4,
                          out_shape=[jax.ShapeDtypeStruct((R, C), F32)] * 4, compiler_params=_cparams(("parallel",)))(w, g, m, v)


def sum_partials(own, recv, g_prev, layer, n_layers, pos, name):
    _, Rh, C = recv.shape
    tr = _tile(Rh, 256, 16)

    def body(pos_ref, own_ref, recv_ref, *rest):
        acc = own_ref[...].astype(F32)
        for rel in range(7):
            acc = acc + recv_ref[rel].astype(F32)
        rest[-1][...] = acc

    in_specs = [pl.BlockSpec((None, None, tr, C), lambda r, pos: (pos[1], pos[0], r, 0)), pl.BlockSpec((7, tr, C), lambda r, pos: (0, r, 0))]
    args = [pos, own.reshape(N_CHIPS, 2, Rh, C), recv]
    aliases = {}
    if g_prev is not None:
        in_specs.append(pl.BlockSpec(memory_space=pl.ANY))
        args.append(g_prev)
        aliases = {3: 0}
    return pl.pallas_call(
        body, name=name,
        grid_spec=pltpu.PrefetchScalarGridSpec(
            num_scalar_prefetch=1, grid=(Rh // tr,), in_specs=in_specs,
            out_specs=pl.BlockSpec((None, None, tr, C), lambda r, pos: (layer, pos[0], r, 0))),
        out_shape=jax.ShapeDtypeStruct((n_layers, 2, Rh, C), F32), input_output_aliases=aliases,
        compiler_params=_cparams(("parallel",)))(*args)


def sum_rows8(g8, name):
    _, R, C = g8.shape

    def body(g_ref, o_ref):
        acc = g_ref[0]
        for i in range(1, N_DEV):
            acc = acc + g_ref[i]
        o_ref[...] = acc

    return pl.pallas_call(body, name=name, grid=(1,), in_specs=[pl.BlockSpec((N_DEV, R, C), lambda i: (0, 0, 0))],
                          out_specs=pl.BlockSpec((R, C), lambda i: (0, 0)), out_shape=jax.ShapeDtypeStruct((R, C), F32),
                          compiler_params=_cparams(("arbitrary",)))(g8)


def _pack(vecs):
    flat = [v.reshape(-1).astype(F32) for v in vecs]
    sizes = [f.shape[0] for f in flat]
    total = sum(sizes)
    padded = -(-total // (8 * PACK_W)) * (8 * PACK_W)
    buf = jnp.concatenate(flat + [jnp.zeros((padded - total,), F32)])
    offs = np.concatenate([[0], np.cumsum(sizes)])
    return buf.reshape(-1, PACK_W), offs


def _unpack(buf, offs, shapes):
    flat = buf.reshape(-1)
    return [flat[int(offs[i]):int(offs[i + 1])].reshape(s) for i, s in enumerate(shapes)]


def kernel(x, c, ada_w, ada_b, norm1_g, norm2_g, pool_w_in, pool_w_grp, pool_scale, pool_w_out, kv_norm_g, kv_ada_w, kv_ada_b, w_kv, attn_w_q, attn_w_o, ffn_w_up, ffn_conv_w, ffn_conv_b, ffn_w_down, final_g, loss_target, m_ada_w, m_ada_b, m_norm1_g, m_norm2_g, m_pool_w_in, m_pool_w_grp, m_pool_scale, m_pool_w_out, m_kv_norm_g, m_kv_ada_w, m_kv_ada_b, m_w_kv, m_attn_w_q, m_attn_w_o, m_ffn_w_up, m_ffn_conv_w, m_ffn_conv_b, m_ffn_w_down, m_final_g, v_ada_w, v_ada_b, v_norm1_g, v_norm2_g, v_pool_w_in, v_pool_w_grp, v_pool_scale, v_pool_w_out, v_kv_norm_g, v_kv_ada_w, v_kv_ada_b, v_w_kv, v_attn_w_q, v_attn_w_o, v_ffn_w_up, v_ffn_conv_w, v_ffn_conv_b, v_ffn_w_down, v_final_g):
    S, D = x.shape[1], x.shape[2]
    depth = ada_w.shape[0]
    n_pool = pool_w_in.shape[0]
    n_attn = attn_w_q.shape[0]
    G = len(POOL_WINDOWS)
    NB = len(BRANCHES)
    DA = attn_w_o.shape[1] * N_CHIPS
    H = DA // HEAD_DIM
    F = ffn_conv_b.shape[1]
    Fs = F // N_CHIPS
    Dq = D // N_CHIPS
    ada_ns = ada_w.shape[2]
    kvada_ns = kv_ada_w.shape[1]
    slopes = _alibi_slopes(NB * H).reshape(NB, H)

    ix, iy, ic = lax.axis_index("x"), lax.axis_index("y"), lax.axis_index("c")
    p_me = 2 * ix + iy
    b_me = 4 * ix + 2 * iy + ic
    pos = jnp.stack([ic, p_me]).astype(jnp.int32)
    xs, tgt = x[0], loss_target[0]

    pk, offs = _pack([c, pool_scale, ffn_conv_w])
    rows1 = pk.shape[0]
    got = all_gather8(pk, "gather_small_in").reshape(N_DEV, rows1, PACK_W)
    c8 = got.reshape(N_DEV, -1)[:, :D]
    c16 = jnp.concatenate([c8, jnp.zeros_like(c8)], axis=0)
    chip_rows = got[0::2].reshape(N_CHIPS, -1)
    scale_full = chip_rows[:, int(offs[1]):int(offs[2])].reshape(N_CHIPS, n_pool, Dq).transpose(1, 0, 2).reshape(n_pool, D)
    convw_full = chip_rows[:, int(offs[2]):int(offs[3])].reshape(N_CHIPS, depth, 3, Fs).transpose(1, 2, 0, 3).reshape(depth, 3, F)

    ada_b_loc = lax.dynamic_slice(ada_b, (0, p_me * ada_ns), (depth, ada_ns)).reshape(depth, 1, ada_ns)
    kvb_loc = lax.dynamic_slice(kv_ada_b, (p_me * kvada_ns,), (kvada_ns,)).reshape(1, 1, kvada_ns)
    mods_loc = ada_project(c16, ada_w, ada_b_loc, "ada_project")[:, :N_DEV]
    kvmod_loc = ada_project(c16, kv_ada_w.reshape(1, D, kvada_ns), kvb_loc, "kv_ada_project")[0, :N_DEV]
    mods_cat = jnp.concatenate([mods_loc.transpose(1, 0, 2).reshape(N_DEV, depth * ada_ns), kvmod_loc], axis=1)
    mods_all = all_gather8(mods_cat, "gather_mods").reshape(N_CHIPS, 2, N_DEV, -1)
    mine = lax.dynamic_index_in_dim(mods_all[:, 0], b_me, axis=1, keepdims=False)
    mod = mine[:, :depth * ada_ns].reshape(N_CHIPS, depth, ada_ns).transpose(1, 0, 2).reshape(depth, 6, 1, D)
    kvmod = mine[:, depth * ada_ns:].reshape(2, 1, D)

    comm = _Comm()
    C = D // G
    kv_ns, q_ns, up_ns = w_kv.shape[1], attn_w_q.shape[2], ffn_w_up.shape[2]

    def layer_shards(l):
        sh = []
        if l < n_pool:
            sh += [(("pin", l), pool_w_in[l]), (("pgrp", l), pool_w_grp[l].reshape(-1, C)), (("pout", l), pool_w_out[l])]
        else:
            if l == n_pool:
                sh.append((("kv", 0), w_kv))
            sh += [(("wq", l), attn_w_q[l - n_pool]), (("wo", l), attn_w_o[l - n_pool])]
        sh += [(("up", l), ffn_w_up[l]), (("down", l), ffn_w_down[l])]
        return [(k, w.astype(BF16)) for k, w in sh]

    def weight(key, shape):
        return comm.require(key).reshape(shape)

    dil = [d for _, d in BRANCHES]
    kv_tn = DA // 2
    q_tn = DA // 4
    q_bwd_tn = q_ns
    up_tn = up_ns
    up_per_half = F // up_tn

    def up_gmap(s):
        return s // up_per_half, s % up_per_half

    def vec(v):
        return v.reshape(1, -1)

    saved = []
    xcur = xs
    kvs = None
    wts = {}
    push_gather(comm, layer_shards(0))
    for l in range(depth):
        if l + 1 < depth:
            push_gather(comm, layer_shards(l + 1))
        sh1, sc1, g1, sh2, sc2, g2 = [mod[l, i] for i in range(6)]
        st = {"x0": xcur}
        h1 = norm_mod(xcur, vec(norm1_g[l]), sh1, sc1, "norm_mod", comm=comm)
        st["h1"] = h1
        if l < n_pool:
            wts["pin", l] = weight(("pin", l), (1, D, D))
            u = mm_nn(h1, wts["pin", l], tn=D, out_dtype=F32, name="pool_in_proj", comm=comm)
            wts["pgrp", l] = weight(("pgrp", l), (N_CHIPS, G, C // N_CHIPS, C))
            pooled, z, ys = pool_fwd(u, wts["pgrp", l], vec(scale_full[l]), "pool_mix")
            wts["pout", l] = weight(("pout", l), (1, D, D))
            out, x1 = mm_nn(ys, wts["pout", l], tn=D, out_dtype=BF16, name="pool_out_proj", res=(xcur, g1), comm=comm)
            st.update(pooled=pooled, z=z, ys=ys, out1=out)
        else:
            if l == n_pool:
                wts["kv", 0] = weight(("kv", 0), (N_CHIPS, D, kv_ns))
                hkv = norm_mod(xcur, vec(kv_norm_g), kvmod[0], kvmod[1], "norm_mod", comm=comm)
                kvs = [mm_nn(hkv, wts["kv", 0], tn=kv_tn, out_dtype=BF16, name=f"kv_proj_b{gi}", ncb=4, perm_d=dil[gi], comm=comm,
                             cbmap=functools.partial(lambda jj, gi: 2 * gi + (jj // 2) * 2 * NB + jj % 2, gi=gi)) for gi in range(NB)]
                kv_state = {"x": xcur, "hkv": hkv}
            wts["wq", l] = weight(("wq", l), (N_CHIPS, D, q_ns))
            qs, os_, lses = [], [], []
            for gi in range(NB):
                q_b = mm_nn(h1, wts["wq", l], tn=q_tn, out_dtype=BF16, name=f"q_proj_b{gi}", ncb=4, perm_d=dil[gi], comm=comm,
                            cbmap=functools.partial(lambda jj, gi: 4 * gi + jj, gi=gi))
                o_b, l_b = attn_branch_fwd(q_b, kvs[gi], gi, slopes[gi], f"attn_fwd_b{gi}", comm=comm)
                if dil[gi] > 1:
                    o_b = unpermute_rows(o_b, dil[gi], f"unpermute_o_b{gi}")
                    l_b = unpermute_rows(l_b, dil[gi], f"unpermute_lse_b{gi}")
                qs.append(q_b)
                os_.append(o_b)
                lses.append(l_b)
            o, lse = attn_combine(os_, lses, "attn_combine")
            wts["wo", l] = weight(("wo", l), (1, DA, D))
            out, x1 = mm_nn(o, wts["wo", l], tn=D, out_dtype=BF16, name="attn_out_proj", res=(xcur, g1), comm=comm)
            st.update(qs=qs, o=o, lse=lse, out1=out)
        st["x1"] = x1
        h2 = norm_mod(x1, vec(norm2_g[l]), sh2, sc2, "norm_mod", comm=comm)
        wts["up", l] = weight(("up", l), (N_CHIPS, D, up_ns))
        hu, gated, conv = ffn_up_act(h2, wts["up", l], convw_full[l], vec(ffn_conv_b[l]), "ffn_up_act", comm=comm)
        wts["down", l] = weight(("down", l), (1, F, D))
        out2, x2 = mm_nn(gated, wts["down", l], tn=D, out_dtype=BF16, name="ffn_down_proj", res=(x1, g2), comm=comm)
        st.update(h2=h2, hu=hu, conv=conv, gated=gated, out2=out2)
        saved.append(st)
        xcur = x2
    comm.flush()

    dx, fsums, dout2 = loss_fwd_bwd(xcur, vec(final_g), tgt, (mod[depth - 1, 5], saved[depth - 1]["out2"]), "loss_head")
    loss = lax.psum(0.5 * jnp.sum(fsums[1]) / D, ("x", "y", "c"))
    d_final_g, s_g2 = fsums[0], fsums[2]

    dmods = [None] * depth
    d_n1 = [None] * depth
    d_n2 = [None] * depth
    d_convw = [None] * depth
    d_convb = [None] * depth
    d_scale = [None] * n_pool
    d_grp = [None] * n_pool
    dkvs = [[] for _ in range(NB)]
    exchanged = []
    f_tk = _tile(F, 1408, LANES)
    ct_blocks = DA // _tile(DA, 256, LANES)

    def exchange(name, idx, dw):
        dw4 = dw.reshape(N_CHIPS, -1, dw.shape[-1])
        exchanged.append((name, idx, dw4))
        push_exchange(comm, (name, idx), dw4)

    for l in reversed(range(depth)):
        st = saved[l]
        sh1, sc1, g1, sh2, sc2, g2 = [mod[l, i] for i in range(6)]
        dout2_3 = dout2.reshape(1, S, D)
        exchange("down", l, mm_tn(st["gated"], dout2_3, (1, F, D), tn=D, tk=f_tk, name="ffn_down_dw", comm=comm))
        dhu, s_conv = ffn_act_bwd(dout2, wts["down", l], st["hu"], st["conv"], convw_full[l], "ffn_act_bwd", comm=comm)
        dh2 = mm_nt(dhu, wts["up", l], tn=up_tn, tk=D, out_dtype=F32, name="ffn_up_bwd", gmap=up_gmap, comm=comm)
        exchange("up", l, mm_tn(st["h2"], dhu, (N_CHIPS, D, up_ns), tn=up_tn, tk=D, name="ffn_up_dw", gmap=up_gmap, comm=comm))
        dx, s_n2, dout1 = norm_mod_bwd(dh2, st["x1"], dx, vec(norm2_g[l]), sc2, "norm_mod_bwd_gate", below=(g1, st["out1"]), comm=comm)
        d_convw[l], d_convb[l] = s_conv[0:3], s_conv[3]
        d_n2[l] = s_n2[2]
        dout1_3 = dout1.reshape(1, S, D)
        if l < n_pool:
            dys = mm_nt(dout1_3, wts["pout", l], tn=D, tk=D // 2, out_dtype=F32, name="pool_out_bwd", comm=comm)
            exchange("pout", l, mm_tn(st["ys"], dout1_3, (1, D, D), tn=D, tk=D, name="pool_out_dw", comm=comm))
            du, d_grp, s_sc = pool_bwd(dys, st["z"], st["pooled"], wts["pgrp", l], vec(scale_full[l]), "pool_mix_bwd")
            exchange("pgrp", l, d_grp.astype(BF16).reshape(G, N_CHIPS, C // N_CHIPS, C).transpose(1, 0, 2, 3))
            d_scale[l] = s_sc[0]
            du_3 = du.reshape(1, S, D)
            dh1 = mm_nt(du_3, wts["pin", l], tn=D, tk=D // 2, out_dtype=F32, name="pool_in_bwd", comm=comm)
            exchange("pin", l, mm_tn(st["h1"], du_3, (1, D, D), tn=D, tk=D, name="pool_in_dw", comm=comm))
        else:
            j = l - n_pool
            do = mm_nt(dout1_3, wts["wo", l], tn=D, tk=DA // 2, out_dtype=BF16, name="attn_out_bwd", comm=comm)
            exchange("wo", j, mm_tn(st["o"], dout1_3, (1, DA, D), tn=D, tk=DA, name="attn_out_dw", comm=comm))
            dlt = attn_delta(do, st["o"], "attn_delta")
            dq = None
            for gi in range(NB):
                d = dil[gi]
                do_b, l_b, dl_b = do, st["lse"], dlt
                if d > 1:
                    do_b = permute_rows(do, d, f"permute_do_b{gi}")
                    l_b = permute_rows(st["lse"], d, f"permute_lse_b{gi}")
                    dl_b = permute_rows(dlt, d, f"permute_delta_b{gi}")
                bwd_name = f"attn_bwd_b{gi}"
                if d > 1:
                    dq_b, dkv_b = attn_branch_bwd(st["qs"][gi], kvs[gi], do_b, l_b, dl_b, gi, slopes[gi], bwd_name, comm=comm)
                    dq = unpermute_rows(dq_b, d, f"unpermute_dq_b{gi}", into=dq, total_cols=NB * DA,
                                        colmap=functools.partial(lambda jj, gi: gi * ct_blocks + jj, gi=gi))
                else:
                    dq, dkv_b = attn_branch_bwd(st["qs"][gi], kvs[gi], do_b, l_b, dl_b, gi, slopes[gi], bwd_name,
                                                out_cols=NB * DA, comm=comm)
                dkvs[gi].append(dkv_b)
            dq_3 = dq.reshape(1, S, NB * DA)
            dh1 = mm_nt(dq_3, wts["wq", l], tn=q_bwd_tn, tk=D, out_dtype=F32, name="q_proj_bwd", comm=comm)
            exchange("wq", j, mm_tn(st["h1"], dq_3, (N_CHIPS, D, q_ns), tn=q_bwd_tn, tk=D, name="q_proj_dw", comm=comm))
        below = (mod[l - 1, 5], saved[l - 1]["out2"]) if l > 0 else None
        if l == n_pool or below is None:
            dx, s_n1 = norm_mod_bwd(dh1, st["x0"], dx, vec(norm1_g[l]), sc1, "norm_mod_bwd", comm=comm if l > 0 else None)
        else:
            dx, s_n1, dout2 = norm_mod_bwd(dh1, st["x0"], dx, vec(norm1_g[l]), sc1, "norm_mod_bwd_gate", below=below, comm=comm)
        d_n1[l] = s_n1[2]
        dmods[l] = jnp.stack([s_n1[0], s_n1[1], s_n2[3], s_n2[0], s_n2[1], s_g2])
        if l > 0 and l != n_pool:
            s_g2 = s_n1[3]
        if l == n_pool:
            dkv = None
            for gi in range(NB):
                dkv = unpermute_rows(dkvs[gi], dil[gi], f"unpermute_dkv_b{gi}", into=dkv, total_cols=2 * NB * DA,
                                     colmap=functools.partial(lambda jj, gi: (jj // ct_blocks) * NB * ct_blocks + gi * ct_blocks + jj % ct_blocks,
                                                              gi=gi))
            dkv_3 = dkv.reshape(1, S, 2 * NB * DA)
            dhkv = mm_nt(dkv_3, wts["kv", 0], tn=kv_ns // 2, tk=D, out_dtype=F32, name="kv_proj_bwd", comm=comm)
            exchange("kv", 0, mm_tn(kv_state["hkv"], dkv_3, (N_CHIPS, D, kv_ns), tn=kv_ns // 2, tk=D, name="kv_proj_dw", comm=comm))
            dx, s_kv, dout2 = norm_mod_bwd(dhkv, kv_state["x"], dx, vec(kv_norm_g), kvmod[1], "norm_mod_bwd_gate", below=below, comm=comm)
            s_g2 = s_kv[3]
    grad_x = dx.reshape(1, S, D)

    smalls = [jnp.stack(dmods), jnp.stack([s_kv[0], s_kv[1]]), jnp.stack(d_n1), jnp.stack(d_n2), s_kv[2], jnp.stack(d_convb), d_final_g,
              jnp.stack(d_scale), jnp.stack(d_convw)]
    small_shapes = [s.shape for s in smalls]
    spk, soffs = _pack(smalls)
    srows = spk.shape[0]
    sgot = all_gather8(spk, "gather_small_grads").reshape(N_DEV, srows, PACK_W)
    ssum = sum_rows8(sgot, "sum_small_grads")
    g_mods, g_kvmod, g_n1, g_n2, g_kvn, g_convb, g_fg, g_scale_full, g_convw_full = _unpack(ssum, soffs, small_shapes)
    g_ada_b = g_mods.reshape(depth, 6 * D)
    g_kv_ada_b = g_kvmod.reshape(2 * D)
    g_scale = lax.dynamic_slice(g_scale_full, (0, p_me * Dq), (n_pool, Dq))
    g_convw = lax.dynamic_slice(g_convw_full, (0, 0, p_me * Fs), (depth, 3, Fs))

    small_w = [ada_b, norm1_g, norm2_g, kv_norm_g, kv_ada_b, ffn_conv_b, final_g, pool_scale, ffn_conv_w]
    small_m = [m_ada_b, m_norm1_g, m_norm2_g, m_kv_norm_g, m_kv_ada_b, m_ffn_conv_b, m_final_g, m_pool_scale, m_ffn_conv_w]
    small_v = [v_ada_b, v_norm1_g, v_norm2_g, v_kv_norm_g, v_kv_ada_b, v_ffn_conv_b, v_final_g, v_pool_scale, v_ffn_conv_w]
    small_g = [g_ada_b, g_n1, g_n2, g_kvn, g_kv_ada_b, g_convb, g_fg, g_scale, g_convw]
    sw_shapes = [w.shape for w in small_w]
    pw, woffs = _pack(small_w)
    s_res = adamw(pw, _pack(small_g)[0], _pack(small_m)[0], _pack(small_v)[0], "adamw_small")
    s_g, s_dl, s_m, s_v = [_unpack(r, woffs, sw_shapes) for r in s_res]

    per_dev = sgot.reshape(N_DEV, -1)
    dm_all = per_dev[:, int(soffs[0]):int(soffs[1])].reshape(N_DEV, depth, 6 * D)
    dkvm_all = per_dev[:, int(soffs[1]):int(soffs[2])].reshape(N_DEV, 1, 2 * D)

    def shard_cols(a, ns):
        sl = lax.dynamic_slice_in_dim(a, p_me * ns, ns, axis=2).transpose(1, 0, 2)
        return jnp.concatenate([sl, jnp.zeros_like(sl)], axis=1)

    ada_res = ada_grad_adamw(c16, shard_cols(dm_all, ada_ns), ada_w, m_ada_w, v_ada_w, "ada_grad_adamw")
    kvada_res = ada_grad_adamw(c16, shard_cols(dkvm_all, kvada_ns), kv_ada_w.reshape(1, D, kvada_ns), m_kv_ada_w.reshape(1, D, kvada_ns),
                               v_kv_ada_w.reshape(1, D, kvada_ns), "kv_ada_grad_adamw")
    kvada_res = [r.reshape(D, kvada_ns) for r in kvada_res]

    comm.flush()
    big_names = ["pin", "pgrp", "pout", "kv", "wq", "wo", "up", "down"]
    n_stack = {"pin": n_pool, "pgrp": n_pool, "pout": n_pool, "kv": 1, "wq": n_attn, "wo": n_attn, "up": depth, "down": depth}
    gsum = {nm: None for nm in big_names}
    for nm, idx, dw4 in exchanged:
        gsum[nm] = sum_partials(dw4, comm.store[nm, idx], gsum[nm], idx, n_stack[nm], pos, "sum_partials")
    for nm in big_names:
        comm.push(_Phase(("swap", nm), ("swap", nm), None, ("swap", nm), 0.0, [], None, gsum[nm].shape[0], 0, _build_swap, buffer=gsum[nm]))
    comm.flush()
    gsum = {nm: comm.store["swap", nm] for nm in big_names}
    big_m = [m_pool_w_in, m_pool_w_grp, m_pool_w_out, m_w_kv, m_attn_w_q, m_attn_w_o, m_ffn_w_up, m_ffn_w_down]
    big_v = [v_pool_w_in, v_pool_w_grp, v_pool_w_out, v_w_kv, v_attn_w_q, v_attn_w_o, v_ffn_w_up, v_ffn_w_down]
    big_w = [pool_w_in, pool_w_grp, pool_w_out, w_kv, attn_w_q, attn_w_o, ffn_w_up, ffn_w_down]
    big_res = []
    for nm, w, m_, v_ in zip(big_names, big_w, big_m, big_v):
        cols = gsum[nm].shape[-1]
        res = adamw(w.reshape(-1, cols), gsum[nm].reshape(-1, cols), m_.reshape(-1, cols), v_.reshape(-1, cols), "adamw_big")
        big_res.append([r.reshape(w.shape) for r in res])

    order = ["ada_w", "ada_b", "norm1_g", "norm2_g", "pool_w_in", "pool_w_grp", "pool_scale", "pool_w_out", "kv_norm_g", "kv_ada_w",
             "kv_ada_b", "w_kv", "attn_w_q", "attn_w_o", "ffn_w_up", "ffn_conv_w", "ffn_conv_b", "ffn_w_down", "final_g"]
    small_names = ["ada_b", "norm1_g", "norm2_g", "kv_norm_g", "kv_ada_b", "ffn_conv_b", "final_g", "pool_scale", "ffn_conv_w"]
    results = {"ada_w": ada_res, "kv_ada_w": kvada_res}
    for i, nm in enumerate(small_names):
        results[nm] = [s_g[i], s_dl[i], s_m[i], s_v[i]]
    for i, nm in enumerate(["pool_w_in", "pool_w_grp", "pool_w_out", "w_kv", "attn_w_q", "attn_w_o", "ffn_w_up", "ffn_w_down"]):
        results[nm] = big_res[i]
    outs = [loss, grad_x]
    for kind in range(4):
        outs += [results[nm][kind] for nm in order]
    return tuple(outs)
```

```python
import functools
import math

import numpy as np
import jax
import jax.numpy as jnp
from jax import lax
from jax.experimental import pallas as pl
from jax.experimental.pallas import tpu as pltpu

F32 = jnp.float32
BF16 = jnp.bfloat16
MESH = pl.DeviceIdType.MESH

POOL_WINDOWS = (2, 4, 8, 16)
BRANCHES = ((128, 1), (512, 4), (2048, 16))
HEAD_DIM = 64
ATTN_BLOCK = 128
EPS = 1e-6
LR, B1, B2, ADAM_EPS, WD, STEP = 0.001, 0.9, 0.999, 1e-08, 0.01, 10

VMEM_LIMIT_BYTES = 56 * 1024 * 1024
LANES = 128
PACK_W = 1024
HALO = 16
NEG = -1e30
N_CHIPS = 4
N_DEV = 8


def _alibi_slopes(n):
    def pow2(m):
        start = 2.0 ** (-(2.0 ** -(math.log2(m) - 3)))
        return [start ** (i + 1) for i in range(m)]
    if math.log2(n).is_integer():
        s = pow2(n)
    else:
        c = 2 ** math.floor(math.log2(n))
        s = pow2(c) + pow2(2 * c)[0::2][: n - c]
    s = np.asarray(s, dtype=np.float32)
    return -np.sort(-s)


def _cparams(sem=None):
    return pltpu.CompilerParams(dimension_semantics=sem, vmem_limit_bytes=VMEM_LIMIT_BYTES)


def _tile(n, pref, unit):
    t = (min(pref, n) // unit) * unit
    while t >= unit:
        if n % t == 0:
            return t
        t -= unit
    return n


def _sigmoid(v):
    return 1.0 / (1.0 + jnp.exp(-v))


def all_gather8(xs, name):
    m_per, n = xs.shape

    def body(x_ref, out_ref, send_sems, recv_sems, local_sem):
        x, y, c = lax.axis_index("x"), lax.axis_index("y"), lax.axis_index("c")
        me, sibling = (x, y, c), (x, y, 1 - c)
        chips = [(1 - x, y), (x, 1 - y), (1 - x, 1 - y)]

        def rows(px, py, pc):
            return out_ref.at[pl.ds((4 * px + 2 * py + pc) * m_per, m_per), :]

        def copy(k, block, to, src=None):
            return pltpu.make_async_remote_copy(src_ref=rows(*block) if src is None else src, dst_ref=rows(*block),
                                                send_sem=send_sems.at[k], recv_sem=recv_sems.at[k], device_id=to, device_id_type=MESH)

        mine = pltpu.make_async_copy(x_ref, rows(*me), local_sem)
        mine.start()
        first = [copy(0, me, sibling, src=x_ref)]
        first += [copy(1 + j, me, (*chip, c), src=x_ref) for j, chip in enumerate(chips)]
        for cp in first:
            cp.start()
        passed = [copy(4 + j, (*chip, c), sibling) for j, chip in enumerate(chips)]
        for j, chip in enumerate(chips):
            copy(1 + j, (*chip, c), me).wait_recv()
            passed[j].start()
        copy(0, sibling, me).wait_recv()
        for j, chip in enumerate(chips):
            copy(4 + j, (*chip, 1 - c), me).wait_recv()
        for cp in first + passed:
            cp.wait_send()
        mine.wait()

    return pl.pallas_call(
        body, name=name,
        out_shape=jax.ShapeDtypeStruct((N_DEV * m_per, n), xs.dtype),
        in_specs=[pl.BlockSpec(memory_space=pltpu.VMEM)],
        out_specs=pl.BlockSpec(memory_space=pltpu.VMEM),
        scratch_shapes=[pltpu.SemaphoreType.DMA((7,)), pltpu.SemaphoreType.DMA((7,)), pltpu.SemaphoreType.DMA],
        compiler_params=pltpu.CompilerParams(vmem_limit_bytes=VMEM_LIMIT_BYTES),
    )(xs)


HBM_SPEC = pl.BlockSpec(memory_space=pltpu.HBM)


def _mesh_pos():
    x, y, c = lax.axis_index("x"), lax.axis_index("y"), lax.axis_index("c")
    return x, y, c, [(1 - x, y), (x, 1 - y), (1 - x, 1 - y)]


class _Phase:
    def __init__(self, key, group, after, owner, est_us, ins, out_shape, n_sems, n_local, build, buffer=None):
        self.key, self.group, self.after, self.owner, self.est_us = key, group, after, owner, est_us
        self.ins, self.out_shape, self.buffer = ins, out_shape, buffer
        self.n_sems, self.n_local, self.build = n_sems, n_local, build


def _rcopy(src, dst, send_sems, recv_sems, k, to):
    return pltpu.make_async_remote_copy(src_ref=src, dst_ref=dst, send_sem=send_sems.at[k], recv_sem=recv_sems.at[k],
                                        device_id=to, device_id_type=MESH)


def _build_fetch(in_refs, g, send_sems, recv_sems, loc_sems, sem0, loc0, rows, whole):
    (shard,) = in_refs
    x, y, c, chips = _mesh_pos()
    p_me = 2 * x + y
    locs = [pltpu.make_async_copy(shard.at[i], g.at[p_me, i], loc_sems.at[loc0 + i]) for i in range(2)] if whole else []
    sends = [_rcopy(shard.at[c, rows], g.at[p_me, c, rows], send_sems, recv_sems, sem0 + j, (*chip, c)) for j, chip in enumerate(chips)]

    def recvs():
        blks = [g.at[2 * chip[0] + chip[1], c, rows] for chip in chips]
        return [_rcopy(blk, blk, send_sems, recv_sems, sem0 + j, (*chip, c)) for j, (blk, chip) in enumerate(zip(blks, chips))]
    return sends, recvs, locs


def _build_pass(in_refs, g, send_sems, recv_sems, loc_sems, sem0, loc0, rows):
    x, y, c, chips = _mesh_pos()
    sib = (x, y, 1 - c)
    slots = [2 * chip[0] + chip[1] for chip in chips]
    sends = [_rcopy(g.at[p, c, rows], g.at[p, c, rows], send_sems, recv_sems, sem0 + j, sib) for j, p in enumerate(slots)]

    def recvs():
        return [_rcopy(g.at[p, 1 - c, rows], g.at[p, 1 - c, rows], send_sems, recv_sems, sem0 + j, sib) for j, p in enumerate(slots)]
    return sends, recvs, []


def _build_exchange(in_refs, recv, send_sems, recv_sems, loc_sems, sem0, loc0, rows):
    (dw,) = in_refs
    x, y, c, chips = _mesh_pos()
    targets = [(c, chip, c, j) for j, chip in enumerate(chips)]
    targets += [(1 - c, chip, 1 - c, 3 + j) for j, chip in enumerate([(x, y)] + chips)]
    sends = [_rcopy(dw.at[2 * chip[0] + chip[1], half, rows], recv.at[rel, rows], send_sems, recv_sems, sem0 + rel, (*chip, core))
             for half, chip, core, rel in targets]

    def recvs():
        return [_rcopy(recv.at[rel, rows], recv.at[rel, rows], send_sems, recv_sems, sem0 + rel, (x, y, 1 - c)) for rel in range(7)]
    return sends, recvs, []


def _build_swap(in_refs, g, send_sems, recv_sems, loc_sems, sem0, loc0):
    x, y, c, _ = _mesh_pos()
    sib = (x, y, 1 - c)
    sends = [_rcopy(g.at[l, c], g.at[l, c], send_sems, recv_sems, sem0 + l, sib) for l in range(g.shape[0])]

    def recvs():
        return [_rcopy(g.at[l, 1 - c], g.at[l, 1 - c], send_sems, recv_sems, sem0 + l, sib) for l in range(g.shape[0])]
    return sends, recvs, []


def _plan_refs(phases, store):
    xin, xout, alias, n_sems, n_loc, out_of = [], [], {}, 0, 0, {}
    for ph in phases:
        ph.sem0, ph.loc0 = n_sems, n_loc
        n_sems += ph.n_sems
        n_loc += ph.n_local
        ph.in0, ph.n_in = len(xin), len(ph.ins)
        xin += ph.ins
        if ph.owner not in out_of:
            out_of[ph.owner] = len(xout)
            if ph.owner == ph.key and ph.buffer is None:
                xout.append(ph.out_shape)
            else:
                buf = ph.buffer if ph.buffer is not None else store[ph.group]
                alias[len(xin)] = len(xout)
                xin.append(buf)
                xout.append(jax.ShapeDtypeStruct(buf.shape, buf.dtype))
        ph.out0 = out_of[ph.owner]
    return xin, xout, alias, max(n_sems, 1), max(n_loc, 1)


def _built(ph, xin_refs, xout_refs, sems):
    return ph.build(xin_refs[ph.in0:ph.in0 + ph.n_in], xout_refs[ph.out0], sems[0], sems[1], sems[2], ph.sem0, ph.loc0)


def _start(phases, xin_refs, xout_refs, sems):
    for ph in phases:
        sends, _, locs = _built(ph, xin_refs, xout_refs, sems)
        for cp in locs + sends:
            cp.start()


def _finish(phases, xin_refs, xout_refs, sems):
    for ph in phases:
        sends, recvs, locs = _built(ph, xin_refs, xout_refs, sems)
        for cp in recvs():
            cp.wait_recv()
        for cp in sends:
            cp.wait_send()
        for cp in locs:
            cp.wait()


class _Comm:
    def __init__(self):
        self.queue, self.store, self.n_alone = [], {}, 0

    def push(self, ph):
        self.queue.append(ph)

    def take(self, carry_us):
        taken, t = [], 0.0
        while True:
            queued = {ph.key for ph in self.queue}
            pending = queued | {ph.key for ph in taken}
            room = 1.5 * carry_us if not taken else carry_us - t
            fits = [ph for ph in self.queue if ph.after not in pending and (ph.owner == ph.key or ph.owner not in queued)
                    and ph.est_us <= room]
            if not fits:
                return taken
            ph = max(fits, key=lambda p: p.est_us)
            self.queue.remove(ph)
            taken.append(ph)
            t += ph.est_us

    def require(self, group):
        phases = [ph for ph in self.queue if ph.group == group]
        if phases:
            self.queue = [ph for ph in self.queue if ph.group != group]
            self.run_alone(phases)
        return self.store[group]

    def flush(self):
        phases, self.queue = self.queue, []
        if phases:
            self.run_alone(phases)

    def run_alone(self, phases):
        phases = [ph for ph in phases if ph.after is None] + [ph for ph in phases if ph.after is not None]
        groups, keys = [[]], set()
        for ph in phases:
            if ph.after in keys:
                groups.append([])
                keys = set()
            groups[-1].append(ph)
            keys.add(ph.key)
        xin, xout, alias, n_sems, n_loc = _plan_refs(phases, self.store)
        n_xin, n_xout = len(xin), len(xout)

        def body(*refs):
            xin_refs, xout_refs, sems = refs[:n_xin], refs[n_xin:n_xin + n_xout], refs[n_xin + n_xout:]
            for grp in groups:
                _start(grp, xin_refs, xout_refs, sems)
                _finish(grp, xin_refs, xout_refs, sems)

        self.n_alone += 1
        outs = pl.pallas_call(
            body, name=f"comm_alone_{self.n_alone}", out_shape=xout, in_specs=[HBM_SPEC] * n_xin, out_specs=[HBM_SPEC] * n_xout,
            input_output_aliases=alias,
            scratch_shapes=[pltpu.SemaphoreType.DMA((n_sems,)), pltpu.SemaphoreType.DMA((n_sems,)), pltpu.SemaphoreType.DMA((n_loc,))],
        )(*xin)
        for ph in phases:
            self.store[ph.group] = outs[ph.out0]


def _pcall(body, *, name, grid, in_specs, out_specs, out_shape, args, scratch_shapes=(), aliases=None, comm=None, carry_us=0.0):
    phases = comm.take(carry_us) if comm is not None else []
    n_in, n_out, n_scr = len(in_specs), len(out_specs), len(scratch_shapes)
    if not phases:
        return pl.pallas_call(body, name=name, grid=grid, in_specs=in_specs, out_specs=out_specs, out_shape=out_shape,
                              scratch_shapes=list(scratch_shapes), input_output_aliases=aliases or {},
                              compiler_params=_cparams(("arbitrary",) * len(grid)))(*args)
    xin, xout, xalias, n_sems, n_loc = _plan_refs(phases, comm.store)
    n_xin, n_xout = len(xin), len(xout)
    all_alias = dict(aliases or {})
    all_alias.update({n_in + i: n_out + o for i, o in xalias.items()})

    def carrier(*refs):
        ins, xin_refs = refs[:n_in], refs[n_in:n_in + n_xin]
        outs = refs[n_in + n_xin:n_in + n_xin + n_out]
        xout_refs = refs[n_in + n_xin + n_out:n_in + n_xin + n_out + n_xout]
        rest = refs[n_in + n_xin + n_out + n_xout:]
        scr, sems = rest[:n_scr], rest[n_scr:]
        pids = [pl.program_id(k) for k in range(len(grid))]
        first = functools.reduce(jnp.logical_and, [p == 0 for p in pids])
        last = functools.reduce(jnp.logical_and, [p == n - 1 for p, n in zip(pids, grid)])

        @pl.when(first)
        def _():
            _start(phases, xin_refs, xout_refs, sems)
        body(*ins, *outs, *scr)

        @pl.when(last)
        def _():
            _finish(phases, xin_refs, xout_refs, sems)

    outs = pl.pallas_call(
        carrier, name=name, grid=grid, in_specs=list(in_specs) + [HBM_SPEC] * n_xin, out_specs=list(out_specs) + [HBM_SPEC] * n_xout,
        out_shape=list(out_shape) + xout,
        scratch_shapes=list(scratch_shapes) + [pltpu.SemaphoreType.DMA((n_sems,)), pltpu.SemaphoreType.DMA((n_sems,)),
                                               pltpu.SemaphoreType.DMA((n_loc,))],
        input_output_aliases=all_alias, compiler_params=_cparams(("arbitrary",) * len(grid)))(*args, *xin)
    for ph in phases:
        comm.store[ph.group] = outs[n_out + ph.out0]
    return outs[:n_out]


FETCH_US_PER_MB = 20.4
PASS_US_PER_MB = 3.3
EXCHANGE_US_PER_MB = 14.5


FETCH_PHASE_US = 35.0
EXCHANGE_PHASE_US = 20.0


def _row_chunks(rows, est_us, phase_us):
    n = 1
    while est_us / n > phase_us and rows % (2 * n) == 0 and (rows // (2 * n)) % 16 == 0:
        n *= 2
    return [pl.ds(k * (rows // n), rows // n) for k in range(n)]


def push_gather(comm, keys_shards):
    prev = []
    for key, shard in keys_shards:
        r, c = shard.shape
        sh = shard.reshape(2, r // 2, c)
        half_mb = r // 2 * c * 2 / 1e6
        chunks = _row_chunks(r // 2, 3 * half_mb * FETCH_US_PER_MB, FETCH_PHASE_US)
        n = len(chunks)
        shape = jax.ShapeDtypeStruct((N_CHIPS, 2, r // 2, c), BF16)
        for k, rows in enumerate(chunks):
            comm.push(_Phase(("fetch", key, k), key, None, ("fetch", key, 0), 3 * half_mb * FETCH_US_PER_MB / n, [sh], shape, 3,
                             2 if k == 0 else 0, functools.partial(_build_fetch, rows=rows, whole=k == 0)))
        for ph in prev:
            comm.push(ph)
        prev = [_Phase(("pass", key, k), key, ("fetch", key, k), ("fetch", key, 0), 3 * half_mb * PASS_US_PER_MB / n + 3.0, [], shape, 3, 0,
                       functools.partial(_build_pass, rows=rows)) for k, rows in enumerate(chunks)]
    for ph in prev:
        comm.push(ph)


def push_exchange(comm, key, dw):
    _, r, c = dw.shape
    half_mb = r // 2 * c * 2 / 1e6
    chunks = _row_chunks(r // 2, 6 * half_mb * EXCHANGE_US_PER_MB, EXCHANGE_PHASE_US)
    dw5 = dw.reshape(N_CHIPS, 2, r // 2, c)
    for k, rows in enumerate(chunks):
        comm.push(_Phase(("exchange", key, k), key, None, ("exchange", key, 0), 6 * half_mb * EXCHANGE_US_PER_MB / len(chunks), [dw5],
                         jax.ShapeDtypeStruct((7, r // 2, c), BF16), 7, 0, functools.partial(_build_exchange, rows=rows)))


MM_FLOPS_PER_US = 6.0e8
MM_ROWS = 1024
MM_ROWS_WIDE = 2048


def mm_nn(a, w3, *, tn, out_dtype, name, ncb=None, cbmap=None, res=None, perm_d=1, comm=None):
    M, K = a.shape
    P, _, Ns = w3.shape
    nper = Ns // tn
    ncb = P * nper if ncb is None else ncb
    tm = max(ATTN_BLOCK * perm_d, _tile(M, MM_ROWS_WIDE, 16)) if perm_d > 1 else _tile(M, MM_ROWS, 16)
    rpb = tm // perm_d
    cbm = cbmap if cbmap is not None else (lambda j: j)
    nch = tn // LANES

    def body(*refs):
        if res is None:
            a_ref, w_ref, o_ref = refs[:3]
        else:
            a_ref, w_ref, x_ref, g_ref, o_ref, xo_ref = refs
        acc = jnp.dot(a_ref[...].astype(BF16), w_ref[...], preferred_element_type=F32)
        if perm_d > 1:
            scr = refs[3]
            for cj in range(nch):
                scr[cj] = acc[:, cj * LANES:(cj + 1) * LANES]
            for r in range(perm_d):
                for cj in range(nch):
                    o_ref[r, :, cj * LANES:(cj + 1) * LANES] = scr.at[cj][pl.ds(r, rpb, stride=perm_d), :].astype(o_ref.dtype)
        else:
            o_ref[...] = acc.astype(o_ref.dtype)
        if res is not None:
            xo_ref[...] = x_ref[...] + g_ref[...] * acc

    in_specs = [pl.BlockSpec((tm, K), lambda i, j: (i, 0)),
                pl.BlockSpec((None, K, tn), lambda i, j: (cbm(j) // nper, 0, cbm(j) % nper))]
    scratch = []
    if perm_d > 1:
        out_specs = [pl.BlockSpec((perm_d, rpb, tn), lambda i, j: (0, i, j))]
        out_shape = [jax.ShapeDtypeStruct((perm_d, M // perm_d, ncb * tn), out_dtype)]
        scratch = [pltpu.VMEM((nch, tm, LANES), F32)]
    else:
        out_specs = [pl.BlockSpec((tm, tn), lambda i, j: (i, j))]
        out_shape = [jax.ShapeDtypeStruct((M, ncb * tn), out_dtype)]
    args = [a, w3]
    if res is not None:
        in_specs += [pl.BlockSpec((tm, tn), lambda i, j: (i, j)), pl.BlockSpec((1, tn), lambda i, j: (0, j))]
        out_specs.append(pl.BlockSpec((tm, tn), lambda i, j: (i, j)))
        out_shape.append(jax.ShapeDtypeStruct((M, ncb * tn), F32))
        args += [res[0], res[1]]
    outs = _pcall(body, name=name, grid=(M // tm, ncb), in_specs=in_specs, out_specs=out_specs, out_shape=out_shape, args=args,
                  scratch_shapes=scratch, comm=comm, carry_us=2.0 * M * K * ncb * tn / MM_FLOPS_PER_US)
    if perm_d > 1:
        return outs[0].reshape(M, ncb * tn)
    return outs[0] if res is None else (outs[0], outs[1])


PERM_MXU_ROWS = 512


def _perm_matrix(d, transpose):
    n = ATTN_BLOCK * d
    m = np.zeros((n, n), np.float32)
    for r in range(d):
        for j in range(ATTN_BLOCK):
            m[r * ATTN_BLOCK + j, j * d + r] = 1.0
    return jnp.asarray(m.T if transpose else m, BF16)


def permute_rows(x, d, name):
    S, C = x.shape
    R = ATTN_BLOCK * d
    ct = _tile(C, 256, LANES)
    nch = ct // LANES
    if x.dtype == BF16 and R <= PERM_MXU_ROWS:
        def mxu_body(pm_ref, x_ref, o_ref):
            y = jnp.dot(pm_ref[...], x_ref[...], preferred_element_type=F32)
            for r in range(d):
                o_ref[r] = y[r * ATTN_BLOCK:(r + 1) * ATTN_BLOCK].astype(o_ref.dtype)

        out = pl.pallas_call(mxu_body, name=name, grid=(S // R, C // ct),
                             in_specs=[pl.BlockSpec((R, R), lambda i, j: (0, 0)), pl.BlockSpec((R, ct), lambda i, j: (i, j))],
                             out_specs=pl.BlockSpec((d, ATTN_BLOCK, ct), lambda i, j: (0, i, j)),
                             out_shape=jax.ShapeDtypeStruct((d, S // d, C), x.dtype),
                             compiler_params=_cparams(("parallel", "parallel")))(_perm_matrix(d, False), x)
        return out.reshape(S, C)

    def body(x_ref, o_ref, scr):
        xv = x_ref[...].astype(F32)
        for cj in range(nch):
            scr[cj] = xv[:, cj * LANES:(cj + 1) * LANES]
        for r in range(d):
            for cj in range(nch):
                o_ref[r, :, cj * LANES:(cj + 1) * LANES] = scr.at[cj][pl.ds(r, ATTN_BLOCK, stride=d), :].astype(o_ref.dtype)

    out = pl.pallas_call(body, name=name, grid=(S // R, C // ct), in_specs=[pl.BlockSpec((R, ct), lambda i, j: (i, j))],
                         out_specs=pl.BlockSpec((d, ATTN_BLOCK, ct), lambda i, j: (0, i, j)),
                         out_shape=jax.ShapeDtypeStruct((d, S // d, C), x.dtype), scratch_shapes=[pltpu.VMEM((nch, R, LANES), F32)],
                         compiler_params=_cparams(("parallel", "parallel")))(x)
    return out.reshape(S, C)


def unpermute_rows(ps, d, name, into=None, total_cols=None, colmap=None):
    ps = list(ps) if isinstance(ps, (list, tuple)) else [ps]
    n_p = len(ps)
    p = ps[0]
    S, C = p.shape
    rpb = max(ATTN_BLOCK, 512 // d)
    R = rpb * d
    ct = _tile(C, 256, LANES)
    nch = ct // LANES
    total_cols = C if total_cols is None else total_cols
    cm = colmap if colmap is not None else (lambda j: j)

    def body(*refs):
        p_refs, o_ref, scr = refs[:n_p], refs[-2], refs[-1]

        def summed(idx):
            return functools.reduce(lambda a, b: a + b, [r[idx].astype(F32) for r in p_refs])
        if d == 1:
            o_ref[...] = summed(0).astype(o_ref.dtype)
            return
        for r in range(d):
            for cj in range(nch):
                scr.at[cj][pl.ds(r, rpb, stride=d), :] = summed((r, slice(None), slice(cj * LANES, (cj + 1) * LANES)))
        for cj in range(nch):
            o_ref[:, cj * LANES:(cj + 1) * LANES] = scr[cj].astype(o_ref.dtype)

    use_mxu = p.dtype == BF16 and d > 1 and R <= PERM_MXU_ROWS

    def mxu_body(*refs):
        pm_ref, p_refs, o_ref = refs[0], refs[1:1 + n_p], refs[-1]
        xs = [functools.reduce(lambda a, b: a + b, [r[k].astype(F32) for r in p_refs]).astype(BF16) for k in range(d)]
        o_ref[...] = jnp.dot(pm_ref[...], jnp.concatenate(xs, axis=0), preferred_element_type=F32).astype(o_ref.dtype)

    in_specs = [pl.BlockSpec((d, rpb, ct), lambda i, j: (0, i, j))] * n_p
    args = [a.reshape(d, S // d, C) for a in ps]
    scratch = [pltpu.VMEM((nch, R, LANES), F32)]
    if use_mxu:
        in_specs, args, scratch = [pl.BlockSpec((R, R), lambda i, j: (0, 0))] + in_specs, [_perm_matrix(d, True)] + args, []
    aliases = {}
    if into is not None:
        in_specs.append(pl.BlockSpec(memory_space=pl.ANY))
        aliases = {len(args): 0}
        args.append(into)
    return pl.pallas_call(mxu_body if use_mxu else body, name=name, grid=(S // R, C // ct), in_specs=in_specs,
                          out_specs=pl.BlockSpec((R, ct), lambda i, j: (i, cm(j))),
                          out_shape=jax.ShapeDtypeStruct((S, total_cols), p.dtype), scratch_shapes=scratch,
                          input_output_aliases=aliases, compiler_params=_cparams(("parallel", "parallel")))(*args)


def mm_nt(g3, w3, *, tn, tk, out_dtype, name, gmap=None, comm=None):
    _, M, _ = g3.shape
    P, K, Ns = w3.shape
    nper = Ns // tn
    ns = P * nper
    tm = _tile(M, MM_ROWS_WIDE, 16)
    gm = gmap if gmap is not None else (lambda s: (0, s))

    def body(g_ref, w_ref, o_ref, acc):
        s = pl.program_id(2)

        @pl.when(s == 0)
        def _():
            acc[...] = jnp.zeros_like(acc)
        acc[...] += lax.dot_general(g_ref[...].astype(BF16), w_ref[...], (((1,), (1,)), ((), ())), preferred_element_type=F32)

        @pl.when(s == ns - 1)
        def _():
            o_ref[...] = acc[...].astype(o_ref.dtype)

    return _pcall(
        body, name=name, grid=(M // tm, K // tk, ns),
        in_specs=[pl.BlockSpec((None, tm, tn), lambda i, kj, s: (gm(s)[0], i, gm(s)[1])),
                  pl.BlockSpec((None, tk, tn), lambda i, kj, s: (s // nper, kj, s % nper))],
        out_specs=[pl.BlockSpec((tm, tk), lambda i, kj, s: (i, kj))],
        out_shape=[jax.ShapeDtypeStruct((M, K), out_dtype)], args=[g3, w3],
        scratch_shapes=[pltpu.VMEM((tm, tk), F32)], comm=comm, carry_us=2.0 * M * K * P * Ns / MM_FLOPS_PER_US)[0]


def mm_tn(a, g3, wshape, *, tn, tk, name, gmap=None, comm=None):
    M, K = a.shape
    P, _, Ns = wshape
    nper = Ns // tn
    ns = P * nper
    tm = _tile(M, MM_ROWS_WIDE, 16)
    nm = M // tm
    gm = gmap if gmap is not None else (lambda s: (0, s))

    def body(a_ref, g_ref, o_ref, acc):
        mi = pl.program_id(2)

        @pl.when(mi == 0)
        def _():
            acc[...] = jnp.zeros_like(acc)
        acc[...] += lax.dot_general(a_ref[...].astype(BF16), g_ref[...].astype(BF16), (((0,), (0,)), ((), ())), preferred_element_type=F32)

        @pl.when(mi == nm - 1)
        def _():
            o_ref[...] = acc[...].astype(o_ref.dtype)

    return _pcall(
        body, name=name, grid=(ns, K // tk, nm),
        in_specs=[pl.BlockSpec((tm, tk), lambda s, kj, mi: (mi, kj)),
                  pl.BlockSpec((None, tm, tn), lambda s, kj, mi: (gm(s)[0], mi, gm(s)[1]))],
        out_specs=[pl.BlockSpec((None, tk, tn), lambda s, kj, mi: (s // nper, kj, s % nper))],
        out_shape=[jax.ShapeDtypeStruct((P, K, Ns), BF16)], args=[a, g3],
        scratch_shapes=[pltpu.VMEM((tk, tn), F32)], comm=comm, carry_us=2.0 * M * K * P * Ns / MM_FLOPS_PER_US)[0]


def _vspec(d):
    return pl.BlockSpec((1, d), lambda i: (0, 0))


NORM_US_PER_ELEM = 12.0 / (4096 * 1024)


def norm_mod(x, g, sh, sc, name, comm=None):
    S, D = x.shape
    tm = _tile(S, 512, 16)

    def body(x_ref, g_ref, sh_ref, sc_ref, o_ref):
        xv = x_ref[...]
        r = lax.rsqrt(jnp.mean(xv * xv, axis=-1, keepdims=True) + EPS)
        o_ref[...] = ((xv * r) * g_ref[...] * (1.0 + sc_ref[...]) + sh_ref[...]).astype(o_ref.dtype)

    return _pcall(body, name=name, grid=(S // tm,),
                  in_specs=[pl.BlockSpec((tm, D), lambda i: (i, 0)), _vspec(D), _vspec(D), _vspec(D)],
                  out_specs=[pl.BlockSpec((tm, D), lambda i: (i, 0))], out_shape=[jax.ShapeDtypeStruct((S, D), BF16)],
                  args=[x, g, sh, sc], comm=comm, carry_us=NORM_US_PER_ELEM * S * D)[0]


def _gate_outputs(dx, gate_ref, out_ref, dout_ref):
    dout_ref[...] = (gate_ref[...] * dx).astype(dout_ref.dtype)
    return jnp.sum(dx * out_ref[...].astype(F32), axis=0, keepdims=True)


NORM_BWD_US_PER_ELEM = 28.0 / (4096 * 1024)


def norm_mod_bwd(dh, x, dres, g, sc, name, below=None, comm=None):
    S, D = x.shape
    tm = _tile(S, 256, 16)

    def body(dh_ref, x_ref, dr_ref, g_ref, sc_ref, *rest):
        dx_ref, sums_ref = (rest[2], rest[3]) if below is not None else (rest[0], rest[1])
        xv = x_ref[...]
        dhv = dh_ref[...].astype(F32)
        r = lax.rsqrt(jnp.mean(xv * xv, axis=-1, keepdims=True) + EPS)
        xn = xv * r
        one_sc = 1.0 + sc_ref[...]
        dxn = dhv * g_ref[...] * one_sc
        dx = r * (dxn - xn * jnp.mean(dxn * xn, axis=-1, keepdims=True)) + dr_ref[...]
        dx_ref[...] = dx
        rows = [jnp.sum(dhv, axis=0, keepdims=True), jnp.sum(dhv * xn * g_ref[...], axis=0, keepdims=True),
                jnp.sum(dhv * one_sc * xn, axis=0, keepdims=True)]
        if below is not None:
            rows.append(_gate_outputs(dx, rest[0], rest[1], rest[4]))
        part = jnp.concatenate(rows + [jnp.zeros((8 - len(rows), D), F32)], axis=0)

        @pl.when(pl.program_id(0) == 0)
        def _():
            sums_ref[...] = jnp.zeros_like(sums_ref)
        sums_ref[...] += part

    row = pl.BlockSpec((tm, D), lambda i: (i, 0))
    in_specs, args = [row, row, row, _vspec(D), _vspec(D)], [dh, x, dres, g, sc]
    out_specs = [row, pl.BlockSpec((8, D), lambda i: (0, 0))]
    out_shape = [jax.ShapeDtypeStruct((S, D), F32), jax.ShapeDtypeStruct((8, D), F32)]
    if below is not None:
        in_specs += [_vspec(D), row]
        args += [below[0], below[1]]
        out_specs.append(row)
        out_shape.append(jax.ShapeDtypeStruct((S, D), BF16))
    return _pcall(body, name=name, grid=(S // tm,), in_specs=in_specs, out_specs=out_specs, out_shape=out_shape, args=args,
                  comm=comm, carry_us=NORM_BWD_US_PER_ELEM * S * D)


def loss_fwd_bwd(x, g, target, below, name):
    S, D = x.shape
    tm = _tile(S, 256, 16)

    def body(x_ref, g_ref, t_ref, gate_ref, out_ref, dx_ref, sums_ref, dout_ref):
        xv = x_ref[...]
        r = lax.rsqrt(jnp.mean(xv * xv, axis=-1, keepdims=True) + EPS)
        xn = xv * r
        err = xn * g_ref[...] - t_ref[...]
        dy = err * (1.0 / D)
        dxn = dy * g_ref[...]
        dx = r * (dxn - xn * jnp.mean(dxn * xn, axis=-1, keepdims=True))
        dx_ref[...] = dx
        part = jnp.concatenate([jnp.sum(dy * xn, axis=0, keepdims=True), jnp.sum(err * err, axis=0, keepdims=True),
                                _gate_outputs(dx, gate_ref, out_ref, dout_ref), jnp.zeros((5, D), F32)], axis=0)

        @pl.when(pl.program_id(0) == 0)
        def _():
            sums_ref[...] = jnp.zeros_like(sums_ref)
        sums_ref[...] += part

    row = pl.BlockSpec((tm, D), lambda i: (i, 0))
    return pl.pallas_call(body, name=name, grid=(S // tm,), in_specs=[row, _vspec(D), row, _vspec(D), row],
                          out_specs=[row, pl.BlockSpec((8, D), lambda i: (0, 0)), row],
                          out_shape=[jax.ShapeDtypeStruct((S, D), F32), jax.ShapeDtypeStruct((8, D), F32), jax.ShapeDtypeStruct((S, D), BF16)],
                          compiler_params=_cparams(("arbitrary",)))(x, g, target, below[0], below[1])


def pool_fwd(u, wgrp, scale, name):
    S, D = u.shape
    G = len(POOL_WINDOWS)
    C = D // G
    tm = _tile(S, 256, 16)
    hb = tm // HALO

    def body(up_ref, uc_ref, w_ref, sc_ref, p_ref, z_ref, y_ref):
        i = pl.program_id(0)
        prev = jnp.where(i > 0, up_ref[...], 0.0)
        ext = jnp.concatenate([prev, uc_ref[...]], axis=0)
        t = i * tm + lax.broadcasted_iota(jnp.int32, (tm, 1), 0)
        for gi, w in enumerate(POOL_WINDOWS):
            cs = slice(gi * C, (gi + 1) * C)
            e = ext[:, cs]
            s, k = e, 1
            while k < w:
                s = s + pltpu.roll(s, k, 0)
                k *= 2
            cnt = jnp.minimum(t + 1, w).astype(F32)
            pooled = (s[HALO:] / cnt - e[HALO:]).astype(BF16)
            p_ref[:, cs] = pooled
            z = jnp.dot(pooled, w_ref[:, gi].reshape(C, C), preferred_element_type=F32)
            z_ref[:, cs] = z.astype(BF16)
            y_ref[:, cs] = (z * sc_ref[:, cs]).astype(BF16)

    row = pl.BlockSpec((tm, D), lambda i: (i, 0))
    return pl.pallas_call(
        body, name=name, grid=(S // tm,),
        in_specs=[pl.BlockSpec((HALO, D), lambda i: (jnp.maximum(i * hb - 1, 0), 0)), row,
                  pl.BlockSpec(wgrp.shape, lambda i: (0, 0, 0, 0)), _vspec(D)],
        out_specs=[row, row, row], out_shape=[jax.ShapeDtypeStruct((S, D), BF16)] * 3,
        compiler_params=_cparams(("parallel",)))(u, u, wgrp, scale)


def pool_bwd(dys, z, pooled, wgrp, scale, name):
    S, D = dys.shape
    G = len(POOL_WINDOWS)
    C = D // G
    tm = _tile(S, 256, 16)
    hb = tm // HALO
    nt = S // tm
    n_ext = tm + HALO

    def body(dc_ref, dn_ref, z_ref, p_ref, w_ref, sc_ref, du_ref, dw_ref, sums_ref):
        i = pl.program_id(0)

        @pl.when(i == 0)
        def _():
            dw_ref[...] = jnp.zeros_like(dw_ref)
            sums_ref[...] = jnp.zeros_like(sums_ref)
        dyc = dc_ref[...].astype(F32)
        nxt = jnp.where(i < nt - 1, dn_ref[...].astype(F32), 0.0)
        ext = jnp.concatenate([dyc, nxt], axis=0)
        sums_ref[...] += jnp.concatenate([jnp.sum(dyc * z_ref[...].astype(F32), axis=0, keepdims=True), jnp.zeros((7, D), F32)], axis=0)
        t = i * tm + lax.broadcasted_iota(jnp.int32, (n_ext, 1), 0)
        for gi, w in enumerate(POOL_WINDOWS):
            cs = slice(gi * C, (gi + 1) * C)
            wg = w_ref[:, gi].reshape(C, C)
            dz = (ext[:, cs] * sc_ref[:, cs]).astype(BF16)
            dpool = lax.dot_general(dz, wg, (((1,), (1,)), ((), ())), preferred_element_type=F32)
            dw_ref[gi] += lax.dot_general(p_ref[:, cs], dz[:tm], (((0,), (0,)), ((), ())), preferred_element_type=F32)
            cnt = jnp.minimum(t + 1, w).astype(F32)
            s, k = dpool / cnt, 1
            while k < w:
                s = s + pltpu.roll(s, n_ext - k, 0)
                k *= 2
            du_ref[:, cs] = (s[:tm] - dpool[:tm]).astype(BF16)

    row = pl.BlockSpec((tm, D), lambda i: (i, 0))
    return pl.pallas_call(
        body, name=name, grid=(nt,),
        in_specs=[row, pl.BlockSpec((HALO, D), lambda i: (jnp.minimum((i + 1) * hb, S // HALO - 1), 0)), row, row,
                  pl.BlockSpec(wgrp.shape, lambda i: (0, 0, 0, 0)), _vspec(D)],
        out_specs=[row, pl.BlockSpec((G, C, C), lambda i: (0, 0, 0)), pl.BlockSpec((8, D), lambda i: (0, 0))],
        out_shape=[jax.ShapeDtypeStruct((S, D), BF16), jax.ShapeDtypeStruct((G, C, C), F32), jax.ShapeDtypeStruct((8, D), F32)],
        compiler_params=_cparams(("arbitrary",)))(dys, dys, z, pooled, wgrp, scale)


FFN_ACT_BWD_US_PER_ELEM = 84.0 / (4096 * 2816)


def ffn_up_act(h, w3, conv_w, conv_b, name, comm=None):
    S, D = h.shape
    P, _, Ns = w3.shape
    nh = P // 2
    tm = _tile(S, MM_ROWS, 16)

    def body(h_ref, w_ref, cw_ref, cb_ref, hu_ref, g_ref, c_ref, stash, halo):
        i, j = pl.program_id(0), pl.program_id(1)
        acc = jnp.dot(h_ref[...], w_ref[...], preferred_element_type=F32).astype(BF16)
        hu_ref[...] = acc

        @pl.when(j < nh)
        def _():
            stash[j] = acc.astype(F32)

        @pl.when(j >= nh)
        def _():
            c = j - nh
            a = stash[c]
            ext = jnp.concatenate([jnp.where(i > 0, halo[c], 0.0), a], axis=0)
            conv = cb_ref[...] + pltpu.roll(ext, 2, 0) * cw_ref[0:1, :] + pltpu.roll(ext, 1, 0) * cw_ref[1:2, :] + ext * cw_ref[2:3, :]
            conv = conv[HALO:]
            c_ref[...] = conv.astype(c_ref.dtype)
            g_ref[...] = (conv * _sigmoid(conv) * acc.astype(F32)).astype(g_ref.dtype)
            halo[c] = a[tm - HALO:]

    def gcol(j):
        return jnp.maximum(j - nh, 0)

    gspec = pl.BlockSpec((tm, Ns), lambda i, j: (i, gcol(j)))
    return _pcall(
        body, name=name, grid=(S // tm, P),
        in_specs=[pl.BlockSpec((tm, D), lambda i, j: (i, 0)), pl.BlockSpec((None, D, Ns), lambda i, j: (j, 0, 0)),
                  pl.BlockSpec((3, Ns), lambda i, j: (0, gcol(j))), pl.BlockSpec((1, Ns), lambda i, j: (0, gcol(j)))],
        out_specs=[pl.BlockSpec((tm, Ns), lambda i, j: (i, j)), gspec, gspec],
        out_shape=[jax.ShapeDtypeStruct((S, P * Ns), BF16), jax.ShapeDtypeStruct((S, nh * Ns), BF16), jax.ShapeDtypeStruct((S, nh * Ns), BF16)],
        scratch_shapes=[pltpu.VMEM((nh, tm, Ns), F32), pltpu.VMEM((nh, HALO, Ns), F32)],
        args=[h, w3, conv_w, conv_b], comm=comm, carry_us=2.0 * S * D * P * Ns / MM_FLOPS_PER_US)


def ffn_act_bwd(dout, w_down, hu, conv, conv_w, name, comm=None):
    S, D = dout.shape
    F = w_down.shape[1]
    tm = _tile(S, 256, 16)
    tn = _tile(F, 1408, LANES)
    nb = F // tn
    hb = tm // HALO
    nt = S // tm
    n_ext = tm + HALO
    nt_dims = (((1,), (1,)), ((), ()))

    def body(dc_ref, dn_ref, wd_ref, cc_ref, cn_ref, ac_ref, vc_ref, vn_ref, w_ref, o_ref, sums_ref):
        i = pl.program_id(1)

        @pl.when(i == 0)
        def _():
            sums_ref[...] = jnp.zeros_like(sums_ref)
        cv = jnp.concatenate([cc_ref[...], cn_ref[...]], axis=0).astype(F32)
        v_ext = jnp.concatenate([vc_ref[...], vn_ref[...]], axis=0).astype(F32)
        g_cur = lax.dot_general(dc_ref[...], wd_ref[...], nt_dims, preferred_element_type=F32)
        g_nxt = lax.dot_general(dn_ref[...], wd_ref[...], nt_dims, preferred_element_type=F32)
        g_ext = jnp.concatenate([g_cur, jnp.where(i < nt - 1, g_nxt, 0.0)], axis=0)
        w0, w1, w2 = w_ref[0:1, :], w_ref[1:2, :], w_ref[2:3, :]
        sig = _sigmoid(cv)
        silu = cv * sig
        dconv = g_ext * v_ext * (sig + silu * (1.0 - sig))
        d_p1, d_p2 = pltpu.roll(dconv, n_ext - 1, 0), pltpu.roll(dconv, n_ext - 2, 0)
        da = dconv * w2 + d_p1 * w1 + d_p2 * w0
        o_ref[0] = da[:tm].astype(o_ref.dtype)
        o_ref[1] = (g_ext * silu)[:tm].astype(o_ref.dtype)
        a = ac_ref[...].astype(F32)
        part = jnp.concatenate([jnp.sum(a * d_p2[:tm], axis=0, keepdims=True), jnp.sum(a * d_p1[:tm], axis=0, keepdims=True),
                                jnp.sum(a * dconv[:tm], axis=0, keepdims=True), jnp.sum(dconv[:tm], axis=0, keepdims=True),
                                jnp.zeros((4, tn), F32)], axis=0)
        sums_ref[...] += part

    def nxt(i):
        return jnp.minimum((i + 1) * hb, S // HALO - 1)

    return _pcall(
        body, name=name, grid=(nb, nt),
        in_specs=[pl.BlockSpec((tm, D), lambda j, i: (i, 0)), pl.BlockSpec((HALO, D), lambda j, i: (nxt(i), 0)),
                  pl.BlockSpec((None, tn, D), lambda j, i: (0, j, 0)),
                  pl.BlockSpec((tm, tn), lambda j, i: (i, j)), pl.BlockSpec((HALO, tn), lambda j, i: (nxt(i), j)),
                  pl.BlockSpec((tm, tn), lambda j, i: (i, j)),
                  pl.BlockSpec((tm, tn), lambda j, i: (i, j + nb)), pl.BlockSpec((HALO, tn), lambda j, i: (nxt(i), j + nb)),
                  pl.BlockSpec((3, tn), lambda j, i: (0, j))],
        out_specs=[pl.BlockSpec((2, tm, tn), lambda j, i: (0, i, j)), pl.BlockSpec((8, tn), lambda j, i: (0, j))],
        out_shape=[jax.ShapeDtypeStruct((2, S, F), BF16), jax.ShapeDtypeStruct((8, F), F32)],
        args=[dout, dout, w_down, conv, conv, hu, hu, hu, conv_w], comm=comm, carry_us=FFN_ACT_BWD_US_PER_ELEM * S * F)


def _head_expander(n_heads, da):
    e = np.zeros((LANES, da), np.float32)
    for h in range(n_heads):
        e[h, h * HEAD_DIM:(h + 1) * HEAD_DIM] = 1.0
    return jnp.asarray(e, BF16)


def _split_dot(v, e, dims):
    hi = v.astype(BF16)
    lo = (v - hi.astype(F32)).astype(BF16)
    return (lax.dot_general(hi, e, dims, preferred_element_type=F32) + lax.dot_general(lo, e, dims, preferred_element_type=F32))


def _lane_col(tile, h):
    lane = lax.broadcasted_iota(jnp.int32, tile.shape, 1)
    return jnp.sum(jnp.where(lane == h, tile, 0.0), axis=1, keepdims=True)


ATTN_US_PER_ELEM = (80.0 / (4096 * 1024), 230.0 / (4096 * 1024))


def attn_branch_fwd(q, kv, gi, slopes, name, comm=None):
    S, DA = q.shape
    H = DA // HEAD_DIM
    window, d = BRANCHES[gi]
    n_steps = window // d
    blk = ATTN_BLOCK
    assert n_steps == blk and (S // d) % blk == 0
    nbs = S // d // blk
    scale = HEAD_DIM ** -0.5

    def body(q_ref, kp_ref, kc_ref, vp_ref, vc_ref, o_ref, l_ref, s_scr, p_scr):
        jb = pl.program_id(1)
        row = lax.broadcasted_iota(jnp.int32, (blk, 2 * blk), 0)
        col = lax.broadcasted_iota(jnp.int32, (blk, 2 * blk), 1)
        delta = row + blk - col
        valid = (delta >= 0) & (delta <= n_steps) & ((col >= blk) | (jb > 0))
        dist = jnp.where(valid, (delta * d).astype(F32), -NEG)
        lane = lax.broadcasted_iota(jnp.int32, (blk, LANES), 1)
        ltile = jnp.zeros((blk, LANES), F32)
        for h in range(H):
            hs = slice(h * HEAD_DIM, (h + 1) * HEAD_DIM)
            k2 = jnp.concatenate([kp_ref[:, hs], kc_ref[:, hs]], axis=0)
            s_scr[h] = lax.dot_general(q_ref[:, hs], k2, (((1,), (1,)), ((), ())), preferred_element_type=F32)
        for h in range(H):
            s = s_scr[h] * scale - float(slopes[h]) * dist
            m = jnp.max(s, axis=-1, keepdims=True)
            p = jnp.exp(s - m)
            l = jnp.sum(p, axis=-1, keepdims=True)
            p_scr[h] = (p / l).astype(BF16)
            ltile = jnp.where(lane == h, m + jnp.log(l), ltile)
        for h in range(H):
            hs = slice(h * HEAD_DIM, (h + 1) * HEAD_DIM)
            v2 = jnp.concatenate([vp_ref[:, hs], vc_ref[:, hs]], axis=0)
            o_ref[:, hs] = jnp.dot(p_scr[h], v2, preferred_element_type=F32).astype(o_ref.dtype)
        l_ref[...] = ltile

    def cur(width, off):
        return pl.BlockSpec((blk, width), lambda r, jb: (r * nbs + jb, off))

    def prv(width, off):
        return pl.BlockSpec((blk, width), lambda r, jb: (r * nbs + jnp.maximum(jb - 1, 0), off))

    return _pcall(
        body, name=name, grid=(d, nbs),
        in_specs=[cur(DA, 0), prv(DA, 0), cur(DA, 0), prv(DA, 1), cur(DA, 1)],
        out_specs=[cur(DA, 0), cur(LANES, 0)],
        out_shape=[jax.ShapeDtypeStruct((S, DA), BF16), jax.ShapeDtypeStruct((S, LANES), F32)],
        scratch_shapes=[pltpu.VMEM((H, blk, 2 * blk), F32), pltpu.VMEM((H, blk, 2 * blk), BF16)],
        args=[q, kv, kv, kv, kv], comm=comm, carry_us=ATTN_US_PER_ELEM[0] * S * DA)


def attn_combine(os_, lses, name):
    S, DA = os_[0].shape
    H = DA // HEAD_DIM
    tm = _tile(S, 256, 16)
    expander = _head_expander(H, DA)
    nbr = len(os_)

    def body(*refs):
        o_refs, l_refs, e_ref = refs[:nbr], refs[nbr:2 * nbr], refs[2 * nbr]
        out_ref, lse_ref = refs[2 * nbr + 1:]
        ls = [r[...] for r in l_refs]
        lmax = functools.reduce(jnp.maximum, ls)
        es = [jnp.exp(l - lmax) for l in ls]
        den = functools.reduce(lambda a, b: a + b, es)
        lse_ref[...] = lmax + jnp.log(den)
        acc = jnp.zeros((tm, DA), F32)
        for e, o_ref in zip(es, o_refs):
            acc = acc + _split_dot(e / den, e_ref[...], (((1,), (0,)), ((), ()))) * o_ref[...]
        out_ref[...] = acc.astype(out_ref.dtype)

    row = pl.BlockSpec((tm, DA), lambda i: (i, 0))
    lrow = pl.BlockSpec((tm, LANES), lambda i: (i, 0))
    return pl.pallas_call(
        body, name=name, grid=(S // tm,),
        in_specs=[row] * nbr + [lrow] * nbr + [pl.BlockSpec((LANES, DA), lambda i: (0, 0))],
        out_specs=[row, lrow], out_shape=[jax.ShapeDtypeStruct((S, DA), BF16), jax.ShapeDtypeStruct((S, LANES), F32)],
        compiler_params=_cparams(("parallel",)))(*os_, *lses, expander)


def attn_delta(do, o, name):
    S, DA = o.shape
    H = DA // HEAD_DIM
    tm = _tile(S, 512, 16)
    expander = _head_expander(H, DA)

    def body(do_ref, o_ref, e_ref, d_ref):
        prod = do_ref[...].astype(F32) * o_ref[...].astype(F32)
        d_ref[...] = _split_dot(prod, e_ref[...], (((1,), (1,)), ((), ())))

    row = pl.BlockSpec((tm, DA), lambda i: (i, 0))
    return pl.pallas_call(body, name=name, grid=(S // tm,), in_specs=[row, row, pl.BlockSpec((LANES, DA), lambda i: (0, 0))],
                          out_specs=pl.BlockSpec((tm, LANES), lambda i: (i, 0)), out_shape=jax.ShapeDtypeStruct((S, LANES), F32),
                          compiler_params=_cparams(("parallel",)))(do, o, expander)


def attn_branch_bwd(q, kv, do, lse, dlt, gi, slopes, name, out_cols=None, comm=None):
    S, DA = q.shape
    H = DA // HEAD_DIM
    window, d = BRANCHES[gi]
    n_steps = window // d
    blk = ATTN_BLOCK
    nbs = S // d // blk
    scale = HEAD_DIM ** -0.5
    nt, tn = (((1,), (1,)), ((), ())), (((0,), (0,)), ((), ()))

    def body(*refs):
        k_ref, v_ref, qc_ref, qn_ref, doc_ref, don_ref, lc_ref, ln_ref, dc_ref, dn_ref = refs[:10]
        dq_ref, dkv_ref, carry, s_scr, dp_scr, p_scr, ds_scr = refs[-7:]
        kb = pl.program_id(1)

        @pl.when(kb == 0)
        def _():
            carry[...] = jnp.zeros_like(carry)
        row = lax.broadcasted_iota(jnp.int32, (2 * blk, blk), 0)
        col = lax.broadcasted_iota(jnp.int32, (2 * blk, blk), 1)
        delta = row - col
        valid = (delta >= 0) & (delta <= n_steps) & ((row < blk) | (kb < nbs - 1))
        dist = jnp.where(valid, (delta * d).astype(F32), -NEG)
        l2 = jnp.concatenate([lc_ref[...], ln_ref[...]], axis=0)
        d2 = jnp.concatenate([dc_ref[...], dn_ref[...]], axis=0)
        for h in range(H):
            hs = slice(h * HEAD_DIM, (h + 1) * HEAD_DIM)
            q2 = jnp.concatenate([qc_ref[:, hs], qn_ref[:, hs]], axis=0)
            do2 = jnp.concatenate([doc_ref[:, hs], don_ref[:, hs]], axis=0)
            s_scr[h] = lax.dot_general(q2, k_ref[:, hs], nt, preferred_element_type=F32)
            dp_scr[h] = lax.dot_general(do2, v_ref[:, hs], nt, preferred_element_type=F32)
        for h in range(H):
            p = jnp.exp(s_scr[h] * scale - float(slopes[h]) * dist - _lane_col(l2, h))
            p_scr[h] = p.astype(BF16)
            ds_scr[h] = (p * (dp_scr[h] - _lane_col(d2, h))).astype(BF16)
        for h in range(H):
            hs = slice(h * HEAD_DIM, (h + 1) * HEAD_DIM)
            vs = slice(DA + h * HEAD_DIM, DA + (h + 1) * HEAD_DIM)
            q2 = jnp.concatenate([qc_ref[:, hs], qn_ref[:, hs]], axis=0)
            do2 = jnp.concatenate([doc_ref[:, hs], don_ref[:, hs]], axis=0)
            dvh = lax.dot_general(p_scr[h], do2, tn, preferred_element_type=F32)
            dkh = lax.dot_general(ds_scr[h], q2, tn, preferred_element_type=F32) * scale
            dq2 = jnp.dot(ds_scr[h], k_ref[:, hs], preferred_element_type=F32) * scale
            dq_ref[:, hs] = (carry[:, hs] + dq2[:blk]).astype(dq_ref.dtype)
            carry[:, hs] = dq2[blk:]
            dkv_ref[:, hs] = dkh.astype(dkv_ref.dtype)
            dkv_ref[:, vs] = dvh.astype(dkv_ref.dtype)

    def cur(width, off):
        return pl.BlockSpec((blk, width), lambda r, kb: (r * nbs + kb, off))

    def nxt(width, off):
        return pl.BlockSpec((blk, width), lambda r, kb: (r * nbs + jnp.minimum(kb + 1, nbs - 1), off))

    in_specs = [cur(DA, 0), cur(DA, 1), cur(DA, 0), nxt(DA, 0), cur(DA, 0), nxt(DA, 0),
                cur(LANES, 0), nxt(LANES, 0), cur(LANES, 0), nxt(LANES, 0)]
    args = [kv, kv, q, q, do, do, lse, lse, dlt, dlt]
    return _pcall(
        body, name=name, grid=(d, nbs), in_specs=in_specs, out_specs=[cur(DA, 0), cur(2 * DA, 0)],
        out_shape=[jax.ShapeDtypeStruct((S, out_cols or DA), BF16), jax.ShapeDtypeStruct((S, 2 * DA), BF16)],
        scratch_shapes=[pltpu.VMEM((blk, DA), F32), pltpu.VMEM((H, 2 * blk, blk), F32), pltpu.VMEM((H, 2 * blk, blk), F32),
                        pltpu.VMEM((H, 2 * blk, blk), BF16), pltpu.VMEM((H, 2 * blk, blk), BF16)],
        args=args, comm=comm, carry_us=ATTN_US_PER_ELEM[1] * S * DA)


def ada_project(c16, w3, b3, name):
    L, D, Ns = w3.shape
    tn = _tile(Ns, 512, LANES)

    def body(c_ref, w_ref, b_ref, o_ref):
        cv = c_ref[...]
        cond = (cv * _sigmoid(cv)).astype(BF16)
        o_ref[...] = jnp.dot(cond, w_ref[...].astype(BF16), preferred_element_type=F32) + b_ref[...]

    return pl.pallas_call(
        body, name=name, grid=(L, Ns // tn),
        in_specs=[pl.BlockSpec((16, D), lambda l, j: (0, 0)), pl.BlockSpec((None, D, tn), lambda l, j: (l, 0, j)),
                  pl.BlockSpec((None, 1, tn), lambda l, j: (l, 0, j))],
        out_specs=pl.BlockSpec((None, 16, tn), lambda l, j: (l, 0, j)), out_shape=jax.ShapeDtypeStruct((L, 16, Ns), F32),
        compiler_params=_cparams(("parallel", "parallel")))(c16, w3, b3)


def _adamw(w, g, m, v):
    m = B1 * m + (1.0 - B1) * g
    v = B2 * v + (1.0 - B2) * (g * g)
    m_hat = m / (1.0 - B1 ** STEP)
    v_hat = v / (1.0 - B2 ** STEP)
    delta = -LR * (m_hat / (jnp.sqrt(v_hat) + ADAM_EPS) + WD * w)
    return delta, m, v


def ada_grad_adamw(c16, d3, w3, m3, v3, name):
    L, D, Ns = w3.shape
    tk = _tile(D, 256, 8)

    def body(c_ref, d_ref, w_ref, m_ref, v_ref, g_out, dl_out, m_out, v_out):
        cv = c_ref[...]
        cond = (cv * _sigmoid(cv)).astype(BF16)
        g = lax.dot_general(cond, d_ref[...].astype(BF16), (((0,), (0,)), ((), ())), preferred_element_type=F32)
        g_out[...] = g
        dl_out[...], m_out[...], v_out[...] = _adamw(w_ref[...], g, m_ref[...], v_ref[...])

    wspec = pl.BlockSpec((None, tk, Ns), lambda l, kj: (l, kj, 0))
    return pl.pallas_call(
        body, name=name, grid=(L, D // tk),
        in_specs=[pl.BlockSpec((16, tk), lambda l, kj: (0, kj)), pl.BlockSpec((None, 16, Ns), lambda l, kj: (l, 0, 0)), wspec, wspec, wspec],
        out_specs=[wspec] * 4, out_shape=[jax.ShapeDtypeStruct((L, D, Ns), F32)] * 4,
        compiler_params=_cparams(("parallel", "parallel")))(c16, d3, w3, m3, v3)


def adamw(w, g, m, v, name):
    R, C = w.shape
    tr = _tile(R, 256, 8)

    def body(w_ref, g_ref, m_ref, v_ref, g_out, dl_out, m_out, v_out):
        g = g_ref[...]
        g_out[...] = g
        dl_out[...], m_out[...], v_out[...] = _adamw(w_ref[...], g, m_ref[...], v_ref[...])

    spec = pl.BlockSpec((tr, C), lambda i: (i, 0))
    return pl.pallas_call(body, name=name, grid=(R // tr,), in_specs=[spec] * 4, out_specs=[spec] * 4,
                          out_shape=[jax.ShapeDtypeStruct((R, C), F32)] * 4, compiler_params=_cparams(("parallel",)))(w, g, m, v)


def sum_partials(own, recv, g_prev, layer, n_layers, pos, name):
    _, Rh, C = recv.shape
    tr = _tile(Rh, 256, 16)

    def body(pos_ref, own_ref, recv_ref, *rest):
        acc = own_ref[...].astype(F32)
        for rel in range(7):
            acc = acc + recv_ref[rel].astype(F32)
        rest[-1][...] = acc

    in_specs = [pl.BlockSpec((None, None, tr, C), lambda r, pos: (pos[1], pos[0], r, 0)), pl.BlockSpec((7, tr, C), lambda r, pos: (0, r, 0))]
    args = [pos, own.reshape(N_CHIPS, 2, Rh, C), recv]
    aliases = {}
    if g_prev is not None:
        in_specs.append(pl.BlockSpec(memory_space=pl.ANY))
        args.append(g_prev)
        aliases = {3: 0}
    return pl.pallas_call(
        body, name=name,
        grid_spec=pltpu.PrefetchScalarGridSpec(
            num_scalar_prefetch=1, grid=(Rh // tr,), in_specs=in_specs,
            out_specs=pl.BlockSpec((None, None, tr, C), lambda r, pos: (layer, pos[0], r, 0))),
        out_shape=jax.ShapeDtypeStruct((n_layers, 2, Rh, C), F32), input_output_aliases=aliases,
        compiler_params=_cparams(("parallel",)))(*args)


def sum_rows8(g8, name):
    _, R, C = g8.shape

    def body(g_ref, o_ref):
        acc = g_ref[0]
        for i in range(1, N_DEV):
            acc = acc + g_ref[i]
        o_ref[...] = acc

    return pl.pallas_call(body, name=name, grid=(1,), in_specs=[pl.BlockSpec((N_DEV, R, C), lambda i: (0, 0, 0))],
                          out_specs=pl.BlockSpec((R, C), lambda i: (0, 0)), out_shape=jax.ShapeDtypeStruct((R, C), F32),
                          compiler_params=_cparams(("arbitrary",)))(g8)


def _pack(vecs):
    flat = [v.reshape(-1).astype(F32) for v in vecs]
    sizes = [f.shape[0] for f in flat]
    total = sum(sizes)
    padded = -(-total // (8 * PACK_W)) * (8 * PACK_W)
    buf = jnp.concatenate(flat + [jnp.zeros((padded - total,), F32)])
    offs = np.concatenate([[0], np.cumsum(sizes)])
    return buf.reshape(-1, PACK_W), offs


def _unpack(buf, offs, shapes):
    flat = buf.reshape(-1)
    return [flat[int(offs[i]):int(offs[i + 1])].reshape(s) for i, s in enumerate(shapes)]


def kernel(x, c, ada_w, ada_b, norm1_g, norm2_g, pool_w_in, pool_w_grp, pool_scale, pool_w_out, kv_norm_g, kv_ada_w, kv_ada_b, w_kv, attn_w_q, attn_w_o, ffn_w_up, ffn_conv_w, ffn_conv_b, ffn_w_down, final_g, loss_target, m_ada_w, m_ada_b, m_norm1_g, m_norm2_g, m_pool_w_in, m_pool_w_grp, m_pool_scale, m_pool_w_out, m_kv_norm_g, m_kv_ada_w, m_kv_ada_b, m_w_kv, m_attn_w_q, m_attn_w_o, m_ffn_w_up, m_ffn_conv_w, m_ffn_conv_b, m_ffn_w_down, m_final_g, v_ada_w, v_ada_b, v_norm1_g, v_norm2_g, v_pool_w_in, v_pool_w_grp, v_pool_scale, v_pool_w_out, v_kv_norm_g, v_kv_ada_w, v_kv_ada_b, v_w_kv, v_attn_w_q, v_attn_w_o, v_ffn_w_up, v_ffn_conv_w, v_ffn_conv_b, v_ffn_w_down, v_final_g):
    S, D = x.shape[1], x.shape[2]
    depth = ada_w.shape[0]
    n_pool = pool_w_in.shape[0]
    n_attn = attn_w_q.shape[0]
    G = len(POOL_WINDOWS)
    NB = len(BRANCHES)
    DA = attn_w_o.shape[1] * N_CHIPS
    H = DA // HEAD_DIM
    F = ffn_conv_b.shape[1]
    Fs = F // N_CHIPS
    Dq = D // N_CHIPS
    ada_ns = ada_w.shape[2]
    kvada_ns = kv_ada_w.shape[1]
    slopes = _alibi_slopes(NB * H).reshape(NB, H)

    ix, iy, ic = lax.axis_index("x"), lax.axis_index("y"), lax.axis_index("c")
    p_me = 2 * ix + iy
    b_me = 4 * ix + 2 * iy + ic
    pos = jnp.stack([ic, p_me]).astype(jnp.int32)
    xs, tgt = x[0], loss_target[0]

    pk, offs = _pack([c, pool_scale, ffn_conv_w])
    rows1 = pk.shape[0]
    got = all_gather8(pk, "gather_small_in").reshape(N_DEV, rows1, PACK_W)
    c8 = got.reshape(N_DEV, -1)[:, :D]
    c16 = jnp.concatenate([c8, jnp.zeros_like(c8)], axis=0)
    chip_rows = got[0::2].reshape(N_CHIPS, -1)
    scale_full = chip_rows[:, int(offs[1]):int(offs[2])].reshape(N_CHIPS, n_pool, Dq).transpose(1, 0, 2).reshape(n_pool, D)
    convw_full = chip_rows[:, int(offs[2]):int(offs[3])].reshape(N_CHIPS, depth, 3, Fs).transpose(1, 2, 0, 3).reshape(depth, 3, F)

    ada_b_loc = lax.dynamic_slice(ada_b, (0, p_me * ada_ns), (depth, ada_ns)).reshape(depth, 1, ada_ns)
    kvb_loc = lax.dynamic_slice(kv_ada_b, (p_me * kvada_ns,), (kvada_ns,)).reshape(1, 1, kvada_ns)
    mods_loc = ada_project(c16, ada_w, ada_b_loc, "ada_project")[:, :N_DEV]
    kvmod_loc = ada_project(c16, kv_ada_w.reshape(1, D, kvada_ns), kvb_loc, "kv_ada_project")[0, :N_DEV]
    mods_cat = jnp.concatenate([mods_loc.transpose(1, 0, 2).reshape(N_DEV, depth * ada_ns), kvmod_loc], axis=1)
    mods_all = all_gather8(mods_cat, "gather_mods").reshape(N_CHIPS, 2, N_DEV, -1)
    mine = lax.dynamic_index_in_dim(mods_all[:, 0], b_me, axis=1, keepdims=False)
    mod = mine[:, :depth * ada_ns].reshape(N_CHIPS, depth, ada_ns).transpose(1, 0, 2).reshape(depth, 6, 1, D)
    kvmod = mine[:, depth * ada_ns:].reshape(2, 1, D)

    comm = _Comm()
    C = D // G
    kv_ns, q_ns, up_ns = w_kv.shape[1], attn_w_q.shape[2], ffn_w_up.shape[2]

    def layer_shards(l):
        sh = []
        if l < n_pool:
            sh += [(("pin", l), pool_w_in[l]), (("pgrp", l), pool_w_grp[l].reshape(-1, C)), (("pout", l), pool_w_out[l])]
        else:
            if l == n_pool:
                sh.append((("kv", 0), w_kv))
            sh += [(("wq", l), attn_w_q[l - n_pool]), (("wo", l), attn_w_o[l - n_pool])]
        sh += [(("up", l), ffn_w_up[l]), (("down", l), ffn_w_down[l])]
        return [(k, w.astype(BF16)) for k, w in sh]

    def weight(key, shape):
        return comm.require(key).reshape(shape)

    dil = [d for _, d in BRANCHES]
    kv_tn = DA // 2
    q_tn = DA // 4
    q_bwd_tn = q_ns
    up_tn = up_ns
    up_per_half = F // up_tn

    def up_gmap(s):
        return s // up_per_half, s % up_per_half

    def vec(v):
        return v.reshape(1, -1)

    saved = []
    xcur = xs
    kvs = None
    wts = {}
    push_gather(comm, layer_shards(0))
    for l in range(depth):
        if l + 1 < depth:
            push_gather(comm, layer_shards(l + 1))
        sh1, sc1, g1, sh2, sc2, g2 = [mod[l, i] for i in range(6)]
        st = {"x0": xcur}
        h1 = norm_mod(xcur, vec(norm1_g[l]), sh1, sc1, "norm_mod", comm=comm)
        st["h1"] = h1
        if l < n_pool:
            wts["pin", l] = weight(("pin", l), (1, D, D))
            u = mm_nn(h1, wts["pin", l], tn=D, out_dtype=F32, name="pool_in_proj", comm=comm)
            wts["pgrp", l] = weight(("pgrp", l), (N_CHIPS, G, C // N_CHIPS, C))
            pooled, z, ys = pool_fwd(u, wts["pgrp", l], vec(scale_full[l]), "pool_mix")
            wts["pout", l] = weight(("pout", l), (1, D, D))
            out, x1 = mm_nn(ys, wts["pout", l], tn=D, out_dtype=BF16, name="pool_out_proj", res=(xcur, g1), comm=comm)
            st.update(pooled=pooled, z=z, ys=ys, out1=out)
        else:
            if l == n_pool:
                wts["kv", 0] = weight(("kv", 0), (N_CHIPS, D, kv_ns))
                hkv = norm_mod(xcur, vec(kv_norm_g), kvmod[0], kvmod[1], "norm_mod", comm=comm)
                kvs = [mm_nn(hkv, wts["kv", 0], tn=kv_tn, out_dtype=BF16, name=f"kv_proj_b{gi}", ncb=4, perm_d=dil[gi], comm=comm,
                             cbmap=functools.partial(lambda jj, gi: 2 * gi + (jj // 2) * 2 * NB + jj % 2, gi=gi)) for gi in range(NB)]
                kv_state = {"x": xcur, "hkv": hkv}
            wts["wq", l] = weight(("wq", l), (N_CHIPS, D, q_ns))
            qs, os_, lses = [], [], []
            for gi in range(NB):
                q_b = mm_nn(h1, wts["wq", l], tn=q_tn, out_dtype=BF16, name=f"q_proj_b{gi}", ncb=4, perm_d=dil[gi], comm=comm,
                            cbmap=functools.partial(lambda jj, gi: 4 * gi + jj, gi=gi))
                o_b, l_b = attn_branch_fwd(q_b, kvs[gi], gi, slopes[gi], f"attn_fwd_b{gi}", comm=comm)
                if dil[gi] > 1:
                    o_b = unpermute_rows(o_b, dil[gi], f"unpermute_o_b{gi}")
                    l_b = unpermute_rows(l_b, dil[gi], f"unpermute_lse_b{gi}")
                qs.append(q_b)
                os_.append(o_b)
                lses.append(l_b)
            o, lse = attn_combine(os_, lses, "attn_combine")
            wts["wo", l] = weight(("wo", l), (1, DA, D))
            out, x1 = mm_nn(o, wts["wo", l], tn=D, out_dtype=BF16, name="attn_out_proj", res=(xcur, g1), comm=comm)
            st.update(qs=qs, o=o, lse=lse, out1=out)
        st["x1"] = x1
        h2 = norm_mod(x1, vec(norm2_g[l]), sh2, sc2, "norm_mod", comm=comm)
        wts["up", l] = weight(("up", l), (N_CHIPS, D, up_ns))
        hu, gated, conv = ffn_up_act(h2, wts["up", l], convw_full[l], vec(ffn_conv_b[l]), "ffn_up_act", comm=comm)
        wts["down", l] = weight(("down", l), (1, F, D))
        out2, x2 = mm_nn(gated, wts["down", l], tn=D, out_dtype=BF16, name="ffn_down_proj", res=(x1, g2), comm=comm)
        st.update(h2=h2, hu=hu, conv=conv, gated=gated, out2=out2)
        saved.append(st)
        xcur = x2
    comm.flush()

    dx, fsums, dout2 = loss_fwd_bwd(xcur, vec(final_g), tgt, (mod[depth - 1, 5], saved[depth - 1]["out2"]), "loss_head")
    loss = lax.psum(0.5 * jnp.sum(fsums[1]) / D, ("x", "y", "c"))
    d_final_g, s_g2 = fsums[0], fsums[2]

    dmods = [None] * depth
    d_n1 = [None] * depth
    d_n2 = [None] * depth
    d_convw = [None] * depth
    d_convb = [None] * depth
    d_scale = [None] * n_pool
    d_grp = [None] * n_pool
    dkvs = [[] for _ in range(NB)]
    exchanged = []
    f_tk = _tile(F, 1408, LANES)
    ct_blocks = DA // _tile(DA, 256, LANES)

    def exchange(name, idx, dw):
        dw4 = dw.reshape(N_CHIPS, -1, dw.shape[-1])
        exchanged.append((name, idx, dw4))
        push_exchange(comm, (name, idx), dw4)

    for l in reversed(range(depth)):
        st = saved[l]
        sh1, sc1, g1, sh2, sc2, g2 = [mod[l, i] for i in range(6)]
        dout2_3 = dout2.reshape(1, S, D)
        exchange("down", l, mm_tn(st["gated"], dout2_3, (1, F, D), tn=D, tk=f_tk, name="ffn_down_dw", comm=comm))
        dhu, s_conv = ffn_act_bwd(dout2, wts["down", l], st["hu"], st["conv"], convw_full[l], "ffn_act_bwd", comm=comm)
        dh2 = mm_nt(dhu, wts["up", l], tn=up_tn, tk=D, out_dtype=F32, name="ffn_up_bwd", gmap=up_gmap, comm=comm)
        exchange("up", l, mm_tn(st["h2"], dhu, (N_CHIPS, D, up_ns), tn=up_tn, tk=D, name="ffn_up_dw", gmap=up_gmap, comm=comm))
        dx, s_n2, dout1 = norm_mod_bwd(dh2, st["x1"], dx, vec(norm2_g[l]), sc2, "norm_mod_bwd_gate", below=(g1, st["out1"]), comm=comm)
        d_convw[l], d_convb[l] = s_conv[0:3], s_conv[3]
        d_n2[l] = s_n2[2]
        dout1_3 = dout1.reshape(1, S, D)
        if l < n_pool:
            dys = mm_nt(dout1_3, wts["pout", l], tn=D, tk=D // 2, out_dtype=F32, name="pool_out_bwd", comm=comm)
            exchange("pout", l, mm_tn(st["ys"], dout1_3, (1, D, D), tn=D, tk=D, name="pool_out_dw", comm=comm))
            du, d_grp, s_sc = pool_bwd(dys, st["z"], st["pooled"], wts["pgrp", l], vec(scale_full[l]), "pool_mix_bwd")
            exchange("pgrp", l, d_grp.astype(BF16).reshape(G, N_CHIPS, C // N_CHIPS, C).transpose(1, 0, 2, 3))
            d_scale[l] = s_sc[0]
            du_3 = du.reshape(1, S, D)
            dh1 = mm_nt(du_3, wts["pin", l], tn=D, tk=D // 2, out_dtype=F32, name="pool_in_bwd", comm=comm)
            exchange("pin", l, mm_tn(st["h1"], du_3, (1, D, D), tn=D, tk=D, name="pool_in_dw", comm=comm))
        else:
            j = l - n_pool
            do = mm_nt(dout1_3, wts["wo", l], tn=D, tk=DA // 2, out_dtype=BF16, name="attn_out_bwd", comm=comm)
            exchange("wo", j, mm_tn(st["o"], dout1_3, (1, DA, D), tn=D, tk=DA, name="attn_out_dw", comm=comm))
            dlt = attn_delta(do, st["o"], "attn_delta")
            dq = None
            for gi in range(NB):
                d = dil[gi]
                do_b, l_b, dl_b = do, st["lse"], dlt
                if d > 1:
                    do_b = permute_rows(do, d, f"permute_do_b{gi}")
                    l_b = permute_rows(st["lse"], d, f"permute_lse_b{gi}")
                    dl_b = permute_rows(dlt, d, f"permute_delta_b{gi}")
                bwd_name = f"attn_bwd_b{gi}"
                if d > 1:
                    dq_b, dkv_b = attn_branch_bwd(st["qs"][gi], kvs[gi], do_b, l_b, dl_b, gi, slopes[gi], bwd_name, comm=comm)
                    dq = unpermute_rows(dq_b, d, f"unpermute_dq_b{gi}", into=dq, total_cols=NB * DA,
                                        colmap=functools.partial(lambda jj, gi: gi * ct_blocks + jj, gi=gi))
                else:
                    dq, dkv_b = attn_branch_bwd(st["qs"][gi], kvs[gi], do_b, l_b, dl_b, gi, slopes[gi], bwd_name,
                                                out_cols=NB * DA, comm=comm)
                dkvs[gi].append(dkv_b)
            dq_3 = dq.reshape(1, S, NB * DA)
            dh1 = mm_nt(dq_3, wts["wq", l], tn=q_bwd_tn, tk=D, out_dtype=F32, name="q_proj_bwd", comm=comm)
            exchange("wq", j, mm_tn(st["h1"], dq_3, (N_CHIPS, D, q_ns), tn=q_bwd_tn, tk=D, name="q_proj_dw", comm=comm))
        below = (mod[l - 1, 5], saved[l - 1]["out2"]) if l > 0 else None
        if l == n_pool or below is None:
            dx, s_n1 = norm_mod_bwd(dh1, st["x0"], dx, vec(norm1_g[l]), sc1, "norm_mod_bwd", comm=comm if l > 0 else None)
        else:
            dx, s_n1, dout2 = norm_mod_bwd(dh1, st["x0"], dx, vec(norm1_g[l]), sc1, "norm_mod_bwd_gate", below=below, comm=comm)
        d_n1[l] = s_n1[2]
        dmods[l] = jnp.stack([s_n1[0], s_n1[1], s_n2[3], s_n2[0], s_n2[1], s_g2])
        if l > 0 and l != n_pool:
            s_g2 = s_n1[3]
        if l == n_pool:
            dkv = None
            for gi in range(NB):
                dkv = unpermute_rows(dkvs[gi], dil[gi], f"unpermute_dkv_b{gi}", into=dkv, total_cols=2 * NB * DA,
                                     colmap=functools.partial(lambda jj, gi: (jj // ct_blocks) * NB * ct_blocks + gi * ct_blocks + jj % ct_blocks,
                                                              gi=gi))
            dkv_3 = dkv.reshape(1, S, 2 * NB * DA)
            dhkv = mm_nt(dkv_3, wts["kv", 0], tn=kv_ns // 2, tk=D, out_dtype=F32, name="kv_proj_bwd", comm=comm)
            exchange("kv", 0, mm_tn(kv_state["hkv"], dkv_3, (N_CHIPS, D, kv_ns), tn=kv_ns // 2, tk=D, name="kv_proj_dw", comm=comm))
            dx, s_kv, dout2 = norm_mod_bwd(dhkv, kv_state["x"], dx, vec(kv_norm_g), kvmod[1], "norm_mod_bwd_gate", below=below, comm=comm)
            s_g2 = s_kv[3]
    grad_x = dx.reshape(1, S, D)

    smalls = [jnp.stack(dmods), jnp.stack([s_kv[0], s_kv[1]]), jnp.stack(d_n1), jnp.stack(d_n2), s_kv[2], jnp.stack(d_convb), d_final_g,
              jnp.stack(d_scale), jnp.stack(d_convw)]
    small_shapes = [s.shape for s in smalls]
    spk, soffs = _pack(smalls)
    srows = spk.shape[0]
    sgot = all_gather8(spk, "gather_small_grads").reshape(N_DEV, srows, PACK_W)
    ssum = sum_rows8(sgot, "sum_small_grads")
    g_mods, g_kvmod, g_n1, g_n2, g_kvn, g_convb, g_fg, g_scale_full, g_convw_full = _unpack(ssum, soffs, small_shapes)
    g_ada_b = g_mods.reshape(depth, 6 * D)
    g_kv_ada_b = g_kvmod.reshape(2 * D)
    g_scale = lax.dynamic_slice(g_scale_full, (0, p_me * Dq), (n_pool, Dq))
    g_convw = lax.dynamic_slice(g_convw_full, (0, 0, p_me * Fs), (depth, 3, Fs))

    small_w = [ada_b, norm1_g, norm2_g, kv_norm_g, kv_ada_b, ffn_conv_b, final_g, pool_scale, ffn_conv_w]
    small_m = [m_ada_b, m_norm1_g, m_norm2_g, m_kv_norm_g, m_kv_ada_b, m_ffn_conv_b, m_final_g, m_pool_scale, m_ffn_conv_w]
    small_v = [v_ada_b, v_norm1_g, v_norm2_g, v_kv_norm_g, v_kv_ada_b, v_ffn_conv_b, v_final_g, v_pool_scale, v_ffn_conv_w]
    small_g = [g_ada_b, g_n1, g_n2, g_kvn, g_kv_ada_b, g_convb, g_fg, g_scale, g_convw]
    sw_shapes = [w.shape for w in small_w]
    pw, woffs = _pack(small_w)
    s_res = adamw(pw, _pack(small_g)[0], _pack(small_m)[0], _pack(small_v)[0], "adamw_small")
    s_g, s_dl, s_m, s_v = [_unpack(r, woffs, sw_shapes) for r in s_res]

    per_dev = sgot.reshape(N_DEV, -1)
    dm_all = per_dev[:, int(soffs[0]):int(soffs[1])].reshape(N_DEV, depth, 6 * D)
    dkvm_all = per_dev[:, int(soffs[1]):int(soffs[2])].reshape(N_DEV, 1, 2 * D)

    def shard_cols(a, ns):
        sl = lax.dynamic_slice_in_dim(a, p_me * ns, ns, axis=2).transpose(1, 0, 2)
        return jnp.concatenate([sl, jnp.zeros_like(sl)], axis=1)

    ada_res = ada_grad_adamw(c16, shard_cols(dm_all, ada_ns), ada_w, m_ada_w, v_ada_w, "ada_grad_adamw")
    kvada_res = ada_grad_adamw(c16, shard_cols(dkvm_all, kvada_ns), kv_ada_w.reshape(1, D, kvada_ns), m_kv_ada_w.reshape(1, D, kvada_ns),
                               v_kv_ada_w.reshape(1, D, kvada_ns), "kv_ada_grad_adamw")
    kvada_res = [r.reshape(D, kvada_ns) for r in kvada_res]

    comm.flush()
    big_names = ["pin", "pgrp", "pout", "kv", "wq", "wo", "up", "down"]
    n_stack = {"pin": n_pool, "pgrp": n_pool, "pout": n_pool, "kv": 1, "wq": n_attn, "wo": n_attn, "up": depth, "down": depth}
    gsum = {nm: None for nm in big_names}
    for nm, idx, dw4 in exchanged:
        gsum[nm] = sum_partials(dw4, comm.store[nm, idx], gsum[nm], idx, n_stack[nm], pos, "sum_partials")
    for nm in big_names:
        comm.push(_Phase(("swap", nm), ("swap", nm), None, ("swap", nm), 0.0, [], None, gsum[nm].shape[0], 0, _build_swap, buffer=gsum[nm]))
    comm.flush()
    gsum = {nm: comm.store["swap", nm] for nm in big_names}
    big_m = [m_pool_w_in, m_pool_w_grp, m_pool_w_out, m_w_kv, m_attn_w_q, m_attn_w_o, m_ffn_w_up, m_ffn_w_down]
    big_v = [v_pool_w_in, v_pool_w_grp, v_pool_w_out, v_w_kv, v_attn_w_q, v_attn_w_o, v_ffn_w_up, v_ffn_w_down]
    big_w = [pool_w_in, pool_w_grp, pool_w_out, w_kv, attn_w_q, attn_w_o, ffn_w_up, ffn_w_down]
    big_res = []
    for nm, w, m_, v_ in zip(big_names, big_w, big_m, big_v):
        cols = gsum[nm].shape[-1]
        res = adamw(w.reshape(-1, cols), gsum[nm].reshape(-1, cols), m_.reshape(-1, cols), v_.reshape(-1, cols), "adamw_big")
        big_res.append([r.reshape(w.shape) for r in res])

    order = ["ada_w", "ada_b", "norm1_g", "norm2_g", "pool_w_in", "pool_w_grp", "pool_scale", "pool_w_out", "kv_norm_g", "kv_ada_w",
             "kv_ada_b", "w_kv", "attn_w_q", "attn_w_o", "ffn_w_up", "ffn_conv_w", "ffn_conv_b", "ffn_w_down", "final_g"]
    small_names = ["ada_b", "norm1_g", "norm2_g", "kv_norm_g", "kv_ada_b", "ffn_conv_b", "final_g", "pool_scale", "ffn_conv_w"]
    results = {"ada_w": ada_res, "kv_ada_w": kvada_res}
    for i, nm in enumerate(small_names):
        results[nm] = [s_g[i], s_dl[i], s_m[i], s_v[i]]
    for i, nm in enumerate(["pool_w_in", "pool_w_grp", "pool_w_out", "w_kv", "attn_w_q", "attn_w_o", "ffn_w_up", "ffn_w_down"]):
        results[nm] = big_res[i]
    outs = [loss, grad_x]
    for kind in range(4):
        outs += [results[nm][kind] for nm in order]
    return tuple(outs)
```

```python
import functools
import math

import numpy as np
import jax
import jax.numpy as jnp
from jax import lax
from jax.experimental import pallas as pl
from jax.experimental.pallas import tpu as pltpu

F32 = jnp.float32
BF16 = jnp.bfloat16
MESH = pl.DeviceIdType.MESH

POOL_WINDOWS = (2, 4, 8, 16)
BRANCHES = ((128, 1), (512, 4), (2048, 16))
HEAD_DIM = 64
ATTN_BLOCK = 128
EPS = 1e-6
LR, B1, B2, ADAM_EPS, WD, STEP = 0.001, 0.9, 0.999, 1e-08, 0.01, 10

VMEM_LIMIT_BYTES = 56 * 1024 * 1024
LANES = 128
PACK_W = 1024
HALO = 16
NEG = -1e30
N_CHIPS = 4
N_DEV = 8


def _alibi_slopes(n):
    def pow2(m):
        start = 2.0 ** (-(2.0 ** -(math.log2(m) - 3)))
        return [start ** (i + 1) for i in range(m)]
    if math.log2(n).is_integer():
        s = pow2(n)
    else:
        c = 2 ** math.floor(math.log2(n))
        s = pow2(c) + pow2(2 * c)[0::2][: n - c]
    s = np.asarray(s, dtype=np.float32)
    return -np.sort(-s)


def _cparams(sem=None):
    return pltpu.CompilerParams(dimension_semantics=sem, vmem_limit_bytes=VMEM_LIMIT_BYTES)


def _tile(n, pref, unit):
    t = (min(pref, n) // unit) * unit
    while t >= unit:
        if n % t == 0:
            return t
        t -= unit
    return n


def _sigmoid(v):
    return 1.0 / (1.0 + jnp.exp(-v))


def all_gather8(xs, name):
    m_per, n = xs.shape

    def body(x_ref, out_ref, send_sems, recv_sems, local_sem):
        x, y, c = lax.axis_index("x"), lax.axis_index("y"), lax.axis_index("c")
        me, sibling = (x, y, c), (x, y, 1 - c)
        chips = [(1 - x, y), (x, 1 - y), (1 - x, 1 - y)]

        def rows(px, py, pc):
            return out_ref.at[pl.ds((4 * px + 2 * py + pc) * m_per, m_per), :]

        def copy(k, block, to, src=None):
            return pltpu.make_async_remote_copy(src_ref=rows(*block) if src is None else src, dst_ref=rows(*block),
                                                send_sem=send_sems.at[k], recv_sem=recv_sems.at[k], device_id=to, device_id_type=MESH)

        mine = pltpu.make_async_copy(x_ref, rows(*me), local_sem)
        mine.start()
        first = [copy(0, me, sibling, src=x_ref)]
        first += [copy(1 + j, me, (*chip, c), src=x_ref) for j, chip in enumerate(chips)]
        for cp in first:
            cp.start()
        passed = [copy(4 + j, (*chip, c), sibling) for j, chip in enumerate(chips)]
        for j, chip in enumerate(chips):
            copy(1 + j, (*chip, c), me).wait_recv()
            passed[j].start()
        copy(0, sibling, me).wait_recv()
        for j, chip in enumerate(chips):
            copy(4 + j, (*chip, 1 - c), me).wait_recv()
        for cp in first + passed:
            cp.wait_send()
        mine.wait()

    return pl.pallas_call(
        body, name=name,
        out_shape=jax.ShapeDtypeStruct((N_DEV * m_per, n), xs.dtype),
        in_specs=[pl.BlockSpec(memory_space=pltpu.VMEM)],
        out_specs=pl.BlockSpec(memory_space=pltpu.VMEM),
        scratch_shapes=[pltpu.SemaphoreType.DMA((7,)), pltpu.SemaphoreType.DMA((7,)), pltpu.SemaphoreType.DMA],
        compiler_params=pltpu.CompilerParams(vmem_limit_bytes=VMEM_LIMIT_BYTES),
    )(xs)


HBM_SPEC = pl.BlockSpec(memory_space=pltpu.HBM)


def _mesh_pos():
    x, y, c = lax.axis_index("x"), lax.axis_index("y"), lax.axis_index("c")
    return x, y, c, [(1 - x, y), (x, 1 - y), (1 - x, 1 - y)]


class _Phase:
    def __init__(self, key, group, after, owner, est_us, ins, out_shape, n_sems, n_local, build, buffer=None):
        self.key, self.group, self.after, self.owner, self.est_us = key, group, after, owner, est_us
        self.ins, self.out_shape, self.buffer = ins, out_shape, buffer
        self.n_sems, self.n_local, self.build = n_sems, n_local, build


def _rcopy(src, dst, send_sems, recv_sems, k, to):
    return pltpu.make_async_remote_copy(src_ref=src, dst_ref=dst, send_sem=send_sems.at[k], recv_sem=recv_sems.at[k],
                                        device_id=to, device_id_type=MESH)


def _build_fetch(in_refs, g, send_sems, recv_sems, loc_sems, sem0, loc0, rows, whole):
    (shard,) = in_refs
    x, y, c, chips = _mesh_pos()
    p_me = 2 * x + y
    locs = [pltpu.make_async_copy(shard.at[i], g.at[p_me, i], loc_sems.at[loc0 + i]) for i in range(2)] if whole else []
    sends = [_rcopy(shard.at[c, rows], g.at[p_me, c, rows], send_sems, recv_sems, sem0 + j, (*chip, c)) for j, chip in enumerate(chips)]

    def recvs():
        blks = [g.at[2 * chip[0] + chip[1], c, rows] for chip in chips]
        return [_rcopy(blk, blk, send_sems, recv_sems, sem0 + j, (*chip, c)) for j, (blk, chip) in enumerate(zip(blks, chips))]
    return sends, recvs, locs


def _build_pass(in_refs, g, send_sems, recv_sems, loc_sems, sem0, loc0, rows):
    x, y, c, chips = _mesh_pos()
    sib = (x, y, 1 - c)
    slots = [2 * chip[0] + chip[1] for chip in chips]
    sends = [_rcopy(g.at[p, c, rows], g.at[p, c, rows], send_sems, recv_sems, sem0 + j, sib) for j, p in enumerate(slots)]

    def recvs():
        return [_rcopy(g.at[p, 1 - c, rows], g.at[p, 1 - c, rows], send_sems, recv_sems, sem0 + j, sib) for j, p in enumerate(slots)]
    return sends, recvs, []


def _build_exchange(in_refs, recv, send_sems, recv_sems, loc_sems, sem0, loc0, rows):
    (dw,) = in_refs
    x, y, c, chips = _mesh_pos()
    targets = [(c, chip, c, j) for j, chip in enumerate(chips)]
    targets += [(1 - c, chip, 1 - c, 3 + j) for j, chip in enumerate([(x, y)] + chips)]
    sends = [_rcopy(dw.at[2 * chip[0] + chip[1], half, rows], recv.at[rel, rows], send_sems, recv_sems, sem0 + rel, (*chip, core))
             for half, chip, core, rel in targets]

    def recvs():
        return [_rcopy(recv.at[rel, rows], recv.at[rel, rows], send_sems, recv_sems, sem0 + rel, (x, y, 1 - c)) for rel in range(7)]
    return sends, recvs, []


def _build_swap(in_refs, g, send_sems, recv_sems, loc_sems, sem0, loc0):
    x, y, c, _ = _mesh_pos()
    sib = (x, y, 1 - c)
    sends = [_rcopy(g.at[l, c], g.at[l, c], send_sems, recv_sems, sem0 + l, sib) for l in range(g.shape[0])]

    def recvs():
        return [_rcopy(g.at[l, 1 - c], g.at[l, 1 - c], send_sems, recv_sems, sem0 + l, sib) for l in range(g.shape[0])]
    return sends, recvs, []


def _plan_refs(phases, store):
    xin, xout, alias, n_sems, n_loc, out_of = [], [], {}, 0, 0, {}
    for ph in phases:
        ph.sem0, ph.loc0 = n_sems, n_loc
        n_sems += ph.n_sems
        n_loc += ph.n_local
        ph.in0, ph.n_in = len(xin), len(ph.ins)
        xin += ph.ins
        if ph.owner not in out_of:
            out_of[ph.owner] = len(xout)
            if ph.owner == ph.key and ph.buffer is None:
                xout.append(ph.out_shape)
            else:
                buf = ph.buffer if ph.buffer is not None else store[ph.group]
                alias[len(xin)] = len(xout)
                xin.append(buf)
                xout.append(jax.ShapeDtypeStruct(buf.shape, buf.dtype))
        ph.out0 = out_of[ph.owner]
    return xin, xout, alias, max(n_sems, 1), max(n_loc, 1)


def _built(ph, xin_refs, xout_refs, sems):
    return ph.build(xin_refs[ph.in0:ph.in0 + ph.n_in], xout_refs[ph.out0], sems[0], sems[1], sems[2], ph.sem0, ph.loc0)


def _start(phases, xin_refs, xout_refs, sems):
    for ph in phases:
        sends, _, locs = _built(ph, xin_refs, xout_refs, sems)
        for cp in locs + sends:
            cp.start()


def _finish(phases, xin_refs, xout_refs, sems):
    for ph in phases:
        sends, recvs, locs = _built(ph, xin_refs, xout_refs, sems)
        for cp in recvs():
            cp.wait_recv()
        for cp in sends:
            cp.wait_send()
        for cp in locs:
            cp.wait()


class _Comm:
    def __init__(self):
        self.queue, self.store, self.n_alone = [], {}, 0

    def push(self, ph):
        self.queue.append(ph)

    def take(self, carry_us):
        taken, t = [], 0.0
        while True:
            queued = {ph.key for ph in self.queue}
            pending = queued | {ph.key for ph in taken}
            room = 1.5 * carry_us if not taken else carry_us - t
            fits = [ph for ph in self.queue if ph.after not in pending and (ph.owner == ph.key or ph.owner not in queued)
                    and ph.est_us <= room]
            if not fits:
                return taken
            ph = max(fits, key=lambda p: p.est_us)
            self.queue.remove(ph)
            taken.append(ph)
            t += ph.est_us

    def require(self, group):
        phases = [ph for ph in self.queue if ph.group == group]
        if phases:
            self.queue = [ph for ph in self.queue if ph.group != group]
            self.run_alone(phases)
        return self.store[group]

    def flush(self):
        phases, self.queue = self.queue, []
        if phases:
            self.run_alone(phases)

    def run_alone(self, phases):
        phases = [ph for ph in phases if ph.after is None] + [ph for ph in phases if ph.after is not None]
        groups, keys = [[]], set()
        for ph in phases:
            if ph.after in keys:
                groups.append([])
                keys = set()
            groups[-1].append(ph)
            keys.add(ph.key)
        xin, xout, alias, n_sems, n_loc = _plan_refs(phases, self.store)
        n_xin, n_xout = len(xin), len(xout)

        def body(*refs):
            xin_refs, xout_refs, sems = refs[:n_xin], refs[n_xin:n_xin + n_xout], refs[n_xin + n_xout:]
            for grp in groups:
                _start(grp, xin_refs, xout_refs, sems)
                _finish(grp, xin_refs, xout_refs, sems)

        self.n_alone += 1
        outs = pl.pallas_call(
            body, name=f"comm_alone_{self.n_alone}", out_shape=xout, in_specs=[HBM_SPEC] * n_xin, out_specs=[HBM_SPEC] * n_xout,
            input_output_aliases=alias,
            scratch_shapes=[pltpu.SemaphoreType.DMA((n_sems,)), pltpu.SemaphoreType.DMA((n_sems,)), pltpu.SemaphoreType.DMA((n_loc,))],
        )(*xin)
        for ph in phases:
            self.store[ph.group] = outs[ph.out0]


def _pcall(body, *, name, grid, in_specs, out_specs, out_shape, args, scratch_shapes=(), aliases=None, comm=None, carry_us=0.0):
    phases = comm.take(carry_us) if comm is not None else []
    n_in, n_out, n_scr = len(in_specs), len(out_specs), len(scratch_shapes)
    if not phases:
        return pl.pallas_call(body, name=name, grid=grid, in_specs=in_specs, out_specs=out_specs, out_shape=out_shape,
                              scratch_shapes=list(scratch_shapes), input_output_aliases=aliases or {},
                              compiler_params=_cparams(("arbitrary",) * len(grid)))(*args)
    xin, xout, xalias, n_sems, n_loc = _plan_refs(phases, comm.store)
    n_xin, n_xout = len(xin), len(xout)
    all_alias = dict(aliases or {})
    all_alias.update({n_in + i: n_out + o for i, o in xalias.items()})

    def carrier(*refs):
        ins, xin_refs = refs[:n_in], refs[n_in:n_in + n_xin]
        outs = refs[n_in + n_xin:n_in + n_xin + n_out]
        xout_refs = refs[n_in + n_xin + n_out:n_in + n_xin + n_out + n_xout]
        rest = refs[n_in + n_xin + n_out + n_xout:]
        scr, sems = rest[:n_scr], rest[n_scr:]
        pids = [pl.program_id(k) for k in range(len(grid))]
        first = functools.reduce(jnp.logical_and, [p == 0 for p in pids])
        last = functools.reduce(jnp.logical_and, [p == n - 1 for p, n in zip(pids, grid)])

        @pl.when(first)
        def _():
            _start(phases, xin_refs, xout_refs, sems)
        body(*ins, *outs, *scr)

        @pl.when(last)
        def _():
            _finish(phases, xin_refs, xout_refs, sems)

    outs = pl.pallas_call(
        carrier, name=name, grid=grid, in_specs=list(in_specs) + [HBM_SPEC] * n_xin, out_specs=list(out_specs) + [HBM_SPEC] * n_xout,
        out_shape=list(out_shape) + xout,
        scratch_shapes=list(scratch_shapes) + [pltpu.SemaphoreType.DMA((n_sems,)), pltpu.SemaphoreType.DMA((n_sems,)),
                                               pltpu.SemaphoreType.DMA((n_loc,))],
        input_output_aliases=all_alias, compiler_params=_cparams(("arbitrary",) * len(grid)))(*args, *xin)
    for ph in phases:
        comm.store[ph.group] = outs[n_out + ph.out0]
    return outs[:n_out]


FETCH_US_PER_MB = 20.4
PASS_US_PER_MB = 3.3
EXCHANGE_US_PER_MB = 14.5


FETCH_PHASE_US = 35.0
EXCHANGE_PHASE_US = 20.0


def _row_chunks(rows, est_us, phase_us):
    n = 1
    while est_us / n > phase_us and rows % (2 * n) == 0 and (rows // (2 * n)) % 16 == 0:
        n *= 2
    return [pl.ds(k * (rows // n), rows // n) for k in range(n)]


def push_gather(comm, keys_shards):
    prev = []
    for key, shard in keys_shards:
        r, c = shard.shape
        sh = shard.reshape(2, r // 2, c)
        half_mb = r // 2 * c * 2 / 1e6
        chunks = _row_chunks(r // 2, 3 * half_mb * FETCH_US_PER_MB, FETCH_PHASE_US)
        n = len(chunks)
        shape = jax.ShapeDtypeStruct((N_CHIPS, 2, r // 2, c), BF16)
        for k, rows in enumerate(chunks):
            comm.push(_Phase(("fetch", key, k), key, None, ("fetch", key, 0), 3 * half_mb * FETCH_US_PER_MB / n, [sh], shape, 3,
                             2 if k == 0 else 0, functools.partial(_build_fetch, rows=rows, whole=k == 0)))
        for ph in prev:
            comm.push(ph)
        prev = [_Phase(("pass", key, k), key, ("fetch", key, k), ("fetch", key, 0), 3 * half_mb * PASS_US_PER_MB / n + 3.0, [], shape, 3, 0,
                       functools.partial(_build_pass, rows=rows)) for k, rows in enumerate(chunks)]
    for ph in prev:
        comm.push(ph)


def push_exchange(comm, key, dw):
    _, r, c = dw.shape
    half_mb = r // 2 * c * 2 / 1e6
    chunks = _row_chunks(r // 2, 6 * half_mb * EXCHANGE_US_PER_MB, EXCHANGE_PHASE_US)
    dw5 = dw.reshape(N_CHIPS, 2, r // 2, c)
    for k, rows in enumerate(chunks):
        comm.push(_Phase(("exchange", key, k), key, None, ("exchange", key, 0), 6 * half_mb * EXCHANGE_US_PER_MB / len(chunks), [dw5],
                         jax.ShapeDtypeStruct((7, r // 2, c), BF16), 7, 0, functools.partial(_build_exchange, rows=rows)))


MM_FLOPS_PER_US = 6.0e8
MM_ROWS = 1024
MM_ROWS_WIDE = 2048


def mm_nn(a, w3, *, tn, out_dtype, name, ncb=None, cbmap=None, res=None, perm_d=1, comm=None):
    M, K = a.shape
    P, _, Ns = w3.shape
    nper = Ns // tn
    ncb = P * nper if ncb is None else ncb
    tm = max(ATTN_BLOCK * perm_d, _tile(M, MM_ROWS_WIDE, 16)) if perm_d > 1 else _tile(M, MM_ROWS, 16)
    rpb = tm // perm_d
    cbm = cbmap if cbmap is not None else (lambda j: j)
    nch = tn // LANES

    def body(*refs):
        if res is None:
            a_ref, w_ref, o_ref = refs[:3]
        else:
            a_ref, w_ref, x_ref, g_ref, o_ref, xo_ref = refs
        acc = jnp.dot(a_ref[...].astype(BF16), w_ref[...], preferred_element_type=F32)
        if perm_d > 1:
            scr = refs[3]
            for cj in range(nch):
                scr[cj] = acc[:, cj * LANES:(cj + 1) * LANES]
            for r in range(perm_d):
                for cj in range(nch):
                    o_ref[r, :, cj * LANES:(cj + 1) * LANES] = scr.at[cj][pl.ds(r, rpb, stride=perm_d), :].astype(o_ref.dtype)
        else:
            o_ref[...] = acc.astype(o_ref.dtype)
        if res is not None:
            xo_ref[...] = x_ref[...] + g_ref[...] * acc

    in_specs = [pl.BlockSpec((tm, K), lambda i, j: (i, 0)),
                pl.BlockSpec((None, K, tn), lambda i, j: (cbm(j) // nper, 0, cbm(j) % nper))]
    scratch = []
    if perm_d > 1:
        out_specs = [pl.BlockSpec((perm_d, rpb, tn), lambda i, j: (0, i, j))]
        out_shape = [jax.ShapeDtypeStruct((perm_d, M // perm_d, ncb * tn), out_dtype)]
        scratch = [pltpu.VMEM((nch, tm, LANES), F32)]
    else:
        out_specs = [pl.BlockSpec((tm, tn), lambda i, j: (i, j))]
        out_shape = [jax.ShapeDtypeStruct((M, ncb * tn), out_dtype)]
    args = [a, w3]
    if res is not None:
        in_specs += [pl.BlockSpec((tm, tn), lambda i, j: (i, j)), pl.BlockSpec((1, tn), lambda i, j: (0, j))]
        out_specs.append(pl.BlockSpec((tm, tn), lambda i, j: (i, j)))
        out_shape.append(jax.ShapeDtypeStruct((M, ncb * tn), F32))
        args += [res[0], res[1]]
    outs = _pcall(body, name=name, grid=(M // tm, ncb), in_specs=in_specs, out_specs=out_specs, out_shape=out_shape, args=args,
                  scratch_shapes=scratch, comm=comm, carry_us=2.0 * M * K * ncb * tn / MM_FLOPS_PER_US)
    if perm_d > 1:
        return outs[0].reshape(M, ncb * tn)
    return outs[0] if res is None else (outs[0], outs[1])


PERM_COLS = 1024


def permute_rows(x, d, name):
    S, C = x.shape
    R = ATTN_BLOCK * d
    ct = _tile(C, PERM_COLS, LANES)
    nch = ct // LANES

    def body(x_ref, o_ref, scr):
        xv = x_ref[...].astype(F32)
        for cj in range(nch):
            scr[cj] = xv[:, cj * LANES:(cj + 1) * LANES]
        for r in range(d):
            for cj in range(nch):
                o_ref[r, :, cj * LANES:(cj + 1) * LANES] = scr.at[cj][pl.ds(r, ATTN_BLOCK, stride=d), :].astype(o_ref.dtype)

    out = pl.pallas_call(body, name=name, grid=(S // R, C // ct), in_specs=[pl.BlockSpec((R, ct), lambda i, j: (i, j))],
                         out_specs=pl.BlockSpec((d, ATTN_BLOCK, ct), lambda i, j: (0, i, j)),
                         out_shape=jax.ShapeDtypeStruct((d, S // d, C), x.dtype), scratch_shapes=[pltpu.VMEM((nch, R, LANES), F32)],
                         compiler_params=_cparams(("parallel", "parallel")))(x)
    return out.reshape(S, C)


def unpermute_rows(ps, d, name, into=None, total_cols=None, colmap=None):
    ps = list(ps) if isinstance(ps, (list, tuple)) else [ps]
    n_p = len(ps)
    p = ps[0]
    S, C = p.shape
    rpb = max(ATTN_BLOCK, 512 // d)
    R = rpb * d
    ct = _tile(C, PERM_COLS, LANES)
    nch = ct // LANES
    total_cols = C if total_cols is None else total_cols
    cm = colmap if colmap is not None else (lambda j: j)

    def body(*refs):
        p_refs, o_ref, scr = refs[:n_p], refs[-2], refs[-1]

        def summed(idx):
            return functools.reduce(lambda a, b: a + b, [r[idx].astype(F32) for r in p_refs])
        if d == 1:
            o_ref[...] = summed(0).astype(o_ref.dtype)
            return
        for r in range(d):
            for cj in range(nch):
                scr.at[cj][pl.ds(r, rpb, stride=d), :] = summed((r, slice(None), slice(cj * LANES, (cj + 1) * LANES)))
        for cj in range(nch):
            o_ref[:, cj * LANES:(cj + 1) * LANES] = scr[cj].astype(o_ref.dtype)

    in_specs = [pl.BlockSpec((d, rpb, ct), lambda i, j: (0, i, j))] * n_p
    args = [a.reshape(d, S // d, C) for a in ps]
    aliases = {}
    if into is not None:
        in_specs.append(pl.BlockSpec(memory_space=pl.ANY))
        args.append(into)
        aliases = {n_p: 0}
    return pl.pallas_call(body, name=name, grid=(S // R, C // ct), in_specs=in_specs,
                          out_specs=pl.BlockSpec((R, ct), lambda i, j: (i, cm(j))),
                          out_shape=jax.ShapeDtypeStruct((S, total_cols), p.dtype), scratch_shapes=[pltpu.VMEM((nch, R, LANES), F32)],
                          input_output_aliases=aliases, compiler_params=_cparams(("parallel", "parallel")))(*args)


def mm_nt(g3, w3, *, tn, tk, out_dtype, name, gmap=None, comm=None):
    _, M, _ = g3.shape
    P, K, Ns = w3.shape
    nper = Ns // tn
    ns = P * nper
    tm = _tile(M, MM_ROWS_WIDE, 16)
    gm = gmap if gmap is not None else (lambda s: (0, s))

    def body(g_ref, w_ref, o_ref, acc):
        s = pl.program_id(2)

        @pl.when(s == 0)
        def _():
            acc[...] = jnp.zeros_like(acc)
        acc[...] += lax.dot_general(g_ref[...].astype(BF16), w_ref[...], (((1,), (1,)), ((), ())), preferred_element_type=F32)

        @pl.when(s == ns - 1)
        def _():
            o_ref[...] = acc[...].astype(o_ref.dtype)

    return _pcall(
        body, name=name, grid=(M // tm, K // tk, ns),
        in_specs=[pl.BlockSpec((None, tm, tn), lambda i, kj, s: (gm(s)[0], i, gm(s)[1])),
                  pl.BlockSpec((None, tk, tn), lambda i, kj, s: (s // nper, kj, s % nper))],
        out_specs=[pl.BlockSpec((tm, tk), lambda i, kj, s: (i, kj))],
        out_shape=[jax.ShapeDtypeStruct((M, K), out_dtype)], args=[g3, w3],
        scratch_shapes=[pltpu.VMEM((tm, tk), F32)], comm=comm, carry_us=2.0 * M * K * P * Ns / MM_FLOPS_PER_US)[0]


def mm_tn(a, g3, wshape, *, tn, tk, name, gmap=None, comm=None):
    M, K = a.shape
    P, _, Ns = wshape
    nper = Ns // tn
    ns = P * nper
    tm = _tile(M, MM_ROWS_WIDE, 16)
    nm = M // tm
    gm = gmap if gmap is not None else (lambda s: (0, s))

    def body(a_ref, g_ref, o_ref, acc):
        mi = pl.program_id(2)

        @pl.when(mi == 0)
        def _():
            acc[...] = jnp.zeros_like(acc)
        acc[...] += lax.dot_general(a_ref[...].astype(BF16), g_ref[...].astype(BF16), (((0,), (0,)), ((), ())), preferred_element_type=F32)

        @pl.when(mi == nm - 1)
        def _():
            o_ref[...] = acc[...].astype(o_ref.dtype)

    return _pcall(
        body, name=name, grid=(ns, K // tk, nm),
        in_specs=[pl.BlockSpec((tm, tk), lambda s, kj, mi: (mi, kj)),
                  pl.BlockSpec((None, tm, tn), lambda s, kj, mi: (gm(s)[0], mi, gm(s)[1]))],
        out_specs=[pl.BlockSpec((None, tk, tn), lambda s, kj, mi: (s // nper, kj, s % nper))],
        out_shape=[jax.ShapeDtypeStruct((P, K, Ns), BF16)], args=[a, g3],
        scratch_shapes=[pltpu.VMEM((tk, tn), F32)], comm=comm, carry_us=2.0 * M * K * P * Ns / MM_FLOPS_PER_US)[0]


def _vspec(d):
    return pl.BlockSpec((1, d), lambda i: (0, 0))


NORM_US_PER_ELEM = 12.0 / (4096 * 1024)


def norm_mod(x, g, sh, sc, name, comm=None):
    S, D = x.shape
    tm = _tile(S, 512, 16)

    def body(x_ref, g_ref, sh_ref, sc_ref, o_ref):
        xv = x_ref[...]
        r = lax.rsqrt(jnp.mean(xv * xv, axis=-1, keepdims=True) + EPS)
        o_ref[...] = ((xv * r) * g_ref[...] * (1.0 + sc_ref[...]) + sh_ref[...]).astype(o_ref.dtype)

    return _pcall(body, name=name, grid=(S // tm,),
                  in_specs=[pl.BlockSpec((tm, D), lambda i: (i, 0)), _vspec(D), _vspec(D), _vspec(D)],
                  out_specs=[pl.BlockSpec((tm, D), lambda i: (i, 0))], out_shape=[jax.ShapeDtypeStruct((S, D), BF16)],
                  args=[x, g, sh, sc], comm=comm, carry_us=NORM_US_PER_ELEM * S * D)[0]


def _gate_outputs(dx, gate_ref, out_ref, dout_ref):
    dout_ref[...] = (gate_ref[...] * dx).astype(dout_ref.dtype)
    return jnp.sum(dx * out_ref[...].astype(F32), axis=0, keepdims=True)


NORM_BWD_US_PER_ELEM = 28.0 / (4096 * 1024)


def norm_mod_bwd(dh, x, dres, g, sc, name, below=None, comm=None):
    S, D = x.shape
    tm = _tile(S, 256, 16)

    def body(dh_ref, x_ref, dr_ref, g_ref, sc_ref, *rest):
        dx_ref, sums_ref = (rest[2], rest[3]) if below is not None else (rest[0], rest[1])
        xv = x_ref[...]
        dhv = dh_ref[...].astype(F32)
        r = lax.rsqrt(jnp.mean(xv * xv, axis=-1, keepdims=True) + EPS)
        xn = xv * r
        one_sc = 1.0 + sc_ref[...]
        dxn = dhv * g_ref[...] * one_sc
        dx = r * (dxn - xn * jnp.mean(dxn * xn, axis=-1, keepdims=True)) + dr_ref[...]
        dx_ref[...] = dx
        rows = [jnp.sum(dhv, axis=0, keepdims=True), jnp.sum(dhv * xn * g_ref[...], axis=0, keepdims=True),
                jnp.sum(dhv * one_sc * xn, axis=0, keepdims=True)]
        if below is not None:
            rows.append(_gate_outputs(dx, rest[0], rest[1], rest[4]))
        part = jnp.concatenate(rows + [jnp.zeros((8 - len(rows), D), F32)], axis=0)

        @pl.when(pl.program_id(0) == 0)
        def _():
            sums_ref[...] = jnp.zeros_like(sums_ref)
        sums_ref[...] += part

    row = pl.BlockSpec((tm, D), lambda i: (i, 0))
    in_specs, args = [row, row, row, _vspec(D), _vspec(D)], [dh, x, dres, g, sc]
    out_specs = [row, pl.BlockSpec((8, D), lambda i: (0, 0))]
    out_shape = [jax.ShapeDtypeStruct((S, D), F32), jax.ShapeDtypeStruct((8, D), F32)]
    if below is not None:
        in_specs += [_vspec(D), row]
        args += [below[0], below[1]]
        out_specs.append(row)
        out_shape.append(jax.ShapeDtypeStruct((S, D), BF16))
    return _pcall(body, name=name, grid=(S // tm,), in_specs=in_specs, out_specs=out_specs, out_shape=out_shape, args=args,
                  comm=comm, carry_us=NORM_BWD_US_PER_ELEM * S * D)


def loss_fwd_bwd(x, g, target, below, name):
    S, D = x.shape
    tm = _tile(S, 256, 16)

    def body(x_ref, g_ref, t_ref, gate_ref, out_ref, dx_ref, sums_ref, dout_ref):
        xv = x_ref[...]
        r = lax.rsqrt(jnp.mean(xv * xv, axis=-1, keepdims=True) + EPS)
        xn = xv * r
        err = xn * g_ref[...] - t_ref[...]
        dy = err * (1.0 / D)
        dxn = dy * g_ref[...]
        dx = r * (dxn - xn * jnp.mean(dxn * xn, axis=-1, keepdims=True))
        dx_ref[...] = dx
        part = jnp.concatenate([jnp.sum(dy * xn, axis=0, keepdims=True), jnp.sum(err * err, axis=0, keepdims=True),
                                _gate_outputs(dx, gate_ref, out_ref, dout_ref), jnp.zeros((5, D), F32)], axis=0)

        @pl.when(pl.program_id(0) == 0)
        def _():
            sums_ref[...] = jnp.zeros_like(sums_ref)
        sums_ref[...] += part

    row = pl.BlockSpec((tm, D), lambda i: (i, 0))
    return pl.pallas_call(body, name=name, grid=(S // tm,), in_specs=[row, _vspec(D), row, _vspec(D), row],
                          out_specs=[row, pl.BlockSpec((8, D), lambda i: (0, 0)), row],
                          out_shape=[jax.ShapeDtypeStruct((S, D), F32), jax.ShapeDtypeStruct((8, D), F32), jax.ShapeDtypeStruct((S, D), BF16)],
                          compiler_params=_cparams(("arbitrary",)))(x, g, target, below[0], below[1])


def pool_fwd(u, wgrp, scale, name):
    S, D = u.shape
    G = len(POOL_WINDOWS)
    C = D // G
    tm = _tile(S, 256, 16)
    hb = tm // HALO

    def body(up_ref, uc_ref, w_ref, sc_ref, p_ref, z_ref, y_ref):
        i = pl.program_id(0)
        prev = jnp.where(i > 0, up_ref[...], 0.0)
        ext = jnp.concatenate([prev, uc_ref[...]], axis=0)
        t = i * tm + lax.broadcasted_iota(jnp.int32, (tm, 1), 0)
        for gi, w in enumerate(POOL_WINDOWS):
            cs = slice(gi * C, (gi + 1) * C)
            e = ext[:, cs]
            s, k = e, 1
            while k < w:
                s = s + pltpu.roll(s, k, 0)
                k *= 2
            cnt = jnp.minimum(t + 1, w).astype(F32)
            pooled = (s[HALO:] / cnt - e[HALO:]).astype(BF16)
            p_ref[:, cs] = pooled
            z = jnp.dot(pooled, w_ref[:, gi].reshape(C, C), preferred_element_type=F32)
            z_ref[:, cs] = z.astype(BF16)
            y_ref[:, cs] = (z * sc_ref[:, cs]).astype(BF16)

    row = pl.BlockSpec((tm, D), lambda i: (i, 0))
    return pl.pallas_call(
        body, name=name, grid=(S // tm,),
        in_specs=[pl.BlockSpec((HALO, D), lambda i: (jnp.maximum(i * hb - 1, 0), 0)), row,
                  pl.BlockSpec(wgrp.shape, lambda i: (0, 0, 0, 0)), _vspec(D)],
        out_specs=[row, row, row], out_shape=[jax.ShapeDtypeStruct((S, D), BF16)] * 3,
        compiler_params=_cparams(("parallel",)))(u, u, wgrp, scale)


def pool_bwd(dys, z, pooled, wgrp, scale, name):
    S, D = dys.shape
    G = len(POOL_WINDOWS)
    C = D // G
    tm = _tile(S, 256, 16)
    hb = tm // HALO
    nt = S // tm
    n_ext = tm + HALO

    def body(dc_ref, dn_ref, z_ref, p_ref, w_ref, sc_ref, du_ref, dw_ref, sums_ref):
        i = pl.program_id(0)

        @pl.when(i == 0)
        def _():
            dw_ref[...] = jnp.zeros_like(dw_ref)
            sums_ref[...] = jnp.zeros_like(sums_ref)
        dyc = dc_ref[...].astype(F32)
        nxt = jnp.where(i < nt - 1, dn_ref[...].astype(F32), 0.0)
        ext = jnp.concatenate([dyc, nxt], axis=0)
        sums_ref[...] += jnp.concatenate([jnp.sum(dyc * z_ref[...].astype(F32), axis=0, keepdims=True), jnp.zeros((7, D), F32)], axis=0)
        t = i * tm + lax.broadcasted_iota(jnp.int32, (n_ext, 1), 0)
        for gi, w in enumerate(POOL_WINDOWS):
            cs = slice(gi * C, (gi + 1) * C)
            wg = w_ref[:, gi].reshape(C, C)
            dz = (ext[:, cs] * sc_ref[:, cs]).astype(BF16)
            dpool = lax.dot_general(dz, wg, (((1,), (1,)), ((), ())), preferred_element_type=F32)
            dw_ref[gi] += lax.dot_general(p_ref[:, cs], dz[:tm], (((0,), (0,)), ((), ())), preferred_element_type=F32)
            cnt = jnp.minimum(t + 1, w).astype(F32)
            s, k = dpool / cnt, 1
            while k < w:
                s = s + pltpu.roll(s, n_ext - k, 0)
                k *= 2
            du_ref[:, cs] = (s[:tm] - dpool[:tm]).astype(BF16)

    row = pl.BlockSpec((tm, D), lambda i: (i, 0))
    return pl.pallas_call(
        body, name=name, grid=(nt,),
        in_specs=[row, pl.BlockSpec((HALO, D), lambda i: (jnp.minimum((i + 1) * hb, S // HALO - 1), 0)), row, row,
                  pl.BlockSpec(wgrp.shape, lambda i: (0, 0, 0, 0)), _vspec(D)],
        out_specs=[row, pl.BlockSpec((G, C, C), lambda i: (0, 0, 0)), pl.BlockSpec((8, D), lambda i: (0, 0))],
        out_shape=[jax.ShapeDtypeStruct((S, D), BF16), jax.ShapeDtypeStruct((G, C, C), F32), jax.ShapeDtypeStruct((8, D), F32)],
        compiler_params=_cparams(("arbitrary",)))(dys, dys, z, pooled, wgrp, scale)


FFN_ACT_BWD_US_PER_ELEM = 84.0 / (4096 * 2816)


def ffn_up_act(h, w3, conv_w, conv_b, name, comm=None):
    S, D = h.shape
    P, _, Ns = w3.shape
    nh = P // 2
    tm = _tile(S, MM_ROWS, 16)

    def body(h_ref, w_ref, cw_ref, cb_ref, hu_ref, g_ref, c_ref, stash, halo):
        i, j = pl.program_id(0), pl.program_id(1)
        acc = jnp.dot(h_ref[...], w_ref[...], preferred_element_type=F32).astype(BF16)
        hu_ref[...] = acc

        @pl.when(j < nh)
        def _():
            stash[j] = acc.astype(F32)

        @pl.when(j >= nh)
        def _():
            c = j - nh
            a = stash[c]
            ext = jnp.concatenate([jnp.where(i > 0, halo[c], 0.0), a], axis=0)
            conv = cb_ref[...] + pltpu.roll(ext, 2, 0) * cw_ref[0:1, :] + pltpu.roll(ext, 1, 0) * cw_ref[1:2, :] + ext * cw_ref[2:3, :]
            conv = conv[HALO:]
            c_ref[...] = conv.astype(c_ref.dtype)
            g_ref[...] = (conv * _sigmoid(conv) * acc.astype(F32)).astype(g_ref.dtype)
            halo[c] = a[tm - HALO:]

    def gcol(j):
        return jnp.maximum(j - nh, 0)

    gspec = pl.BlockSpec((tm, Ns), lambda i, j: (i, gcol(j)))
    return _pcall(
        body, name=name, grid=(S // tm, P),
        in_specs=[pl.BlockSpec((tm, D), lambda i, j: (i, 0)), pl.BlockSpec((None, D, Ns), lambda i, j: (j, 0, 0)),
                  pl.BlockSpec((3, Ns), lambda i, j: (0, gcol(j))), pl.BlockSpec((1, Ns), lambda i, j: (0, gcol(j)))],
        out_specs=[pl.BlockSpec((tm, Ns), lambda i, j: (i, j)), gspec, gspec],
        out_shape=[jax.ShapeDtypeStruct((S, P * Ns), BF16), jax.ShapeDtypeStruct((S, nh * Ns), BF16), jax.ShapeDtypeStruct((S, nh * Ns), BF16)],
        scratch_shapes=[pltpu.VMEM((nh, tm, Ns), F32), pltpu.VMEM((nh, HALO, Ns), F32)],
        args=[h, w3, conv_w, conv_b], comm=comm, carry_us=2.0 * S * D * P * Ns / MM_FLOPS_PER_US)


def ffn_act_bwd(dout, w_down, hu, conv, conv_w, name, comm=None):
    S, D = dout.shape
    F = w_down.shape[1]
    tm = _tile(S, 256, 16)
    tn = _tile(F, 1408, LANES)
    nb = F // tn
    hb = tm // HALO
    nt = S // tm
    n_ext = tm + HALO
    nt_dims = (((1,), (1,)), ((), ()))

    def body(dc_ref, dn_ref, wd_ref, cc_ref, cn_ref, ac_ref, vc_ref, vn_ref, w_ref, o_ref, sums_ref):
        i = pl.program_id(1)

        @pl.when(i == 0)
        def _():
            sums_ref[...] = jnp.zeros_like(sums_ref)
        cv = jnp.concatenate([cc_ref[...], cn_ref[...]], axis=0).astype(F32)
        v_ext = jnp.concatenate([vc_ref[...], vn_ref[...]], axis=0).astype(F32)
        g_cur = lax.dot_general(dc_ref[...], wd_ref[...], nt_dims, preferred_element_type=F32)
        g_nxt = lax.dot_general(dn_ref[...], wd_ref[...], nt_dims, preferred_element_type=F32)
        g_ext = jnp.concatenate([g_cur, jnp.where(i < nt - 1, g_nxt, 0.0)], axis=0)
        w0, w1, w2 = w_ref[0:1, :], w_ref[1:2, :], w_ref[2:3, :]
        sig = _sigmoid(cv)
        silu = cv * sig
        dconv = g_ext * v_ext * (sig + silu * (1.0 - sig))
        d_p1, d_p2 = pltpu.roll(dconv, n_ext - 1, 0), pltpu.roll(dconv, n_ext - 2, 0)
        da = dconv * w2 + d_p1 * w1 + d_p2 * w0
        o_ref[0] = da[:tm].astype(o_ref.dtype)
        o_ref[1] = (g_ext * silu)[:tm].astype(o_ref.dtype)
        a = ac_ref[...].astype(F32)
        part = jnp.concatenate([jnp.sum(a * d_p2[:tm], axis=0, keepdims=True), jnp.sum(a * d_p1[:tm], axis=0, keepdims=True),
                                jnp.sum(a * dconv[:tm], axis=0, keepdims=True), jnp.sum(dconv[:tm], axis=0, keepdims=True),
                                jnp.zeros((4, tn), F32)], axis=0)
        sums_ref[...] += part

    def nxt(i):
        return jnp.minimum((i + 1) * hb, S // HALO - 1)

    return _pcall(
        body, name=name, grid=(nb, nt),
        in_specs=[pl.BlockSpec((tm, D), lambda j, i: (i, 0)), pl.BlockSpec((HALO, D), lambda j, i: (nxt(i), 0)),
                  pl.BlockSpec((None, tn, D), lambda j, i: (0, j, 0)),
                  pl.BlockSpec((tm, tn), lambda j, i: (i, j)), pl.BlockSpec((HALO, tn), lambda j, i: (nxt(i), j)),
                  pl.BlockSpec((tm, tn), lambda j, i: (i, j)),
                  pl.BlockSpec((tm, tn), lambda j, i: (i, j + nb)), pl.BlockSpec((HALO, tn), lambda j, i: (nxt(i), j + nb)),
                  pl.BlockSpec((3, tn), lambda j, i: (0, j))],
        out_specs=[pl.BlockSpec((2, tm, tn), lambda j, i: (0, i, j)), pl.BlockSpec((8, tn), lambda j, i: (0, j))],
        out_shape=[jax.ShapeDtypeStruct((2, S, F), BF16), jax.ShapeDtypeStruct((8, F), F32)],
        args=[dout, dout, w_down, conv, conv, hu, hu, hu, conv_w], comm=comm, carry_us=FFN_ACT_BWD_US_PER_ELEM * S * F)


def _head_expander(n_heads, da):
    e = np.zeros((LANES, da), np.float32)
    for h in range(n_heads):
        e[h, h * HEAD_DIM:(h + 1) * HEAD_DIM] = 1.0
    return jnp.asarray(e, BF16)


def _split_dot(v, e, dims):
    hi = v.astype(BF16)
    lo = (v - hi.astype(F32)).astype(BF16)
    return (lax.dot_general(hi, e, dims, preferred_element_type=F32) + lax.dot_general(lo, e, dims, preferred_element_type=F32))


def _lane_col(tile, h):
    lane = lax.broadcasted_iota(jnp.int32, tile.shape, 1)
    return jnp.sum(jnp.where(lane == h, tile, 0.0), axis=1, keepdims=True)


ATTN_US_PER_ELEM = (80.0 / (4096 * 1024), 230.0 / (4096 * 1024))


def attn_branch_fwd(q, kv, gi, slopes, name, comm=None):
    S, DA = q.shape
    H = DA // HEAD_DIM
    window, d = BRANCHES[gi]
    n_steps = window // d
    blk = ATTN_BLOCK
    assert n_steps == blk and (S // d) % blk == 0
    nbs = S // d // blk
    scale = HEAD_DIM ** -0.5

    def body(q_ref, kp_ref, kc_ref, vp_ref, vc_ref, o_ref, l_ref, s_scr, p_scr):
        jb = pl.program_id(1)
        row = lax.broadcasted_iota(jnp.int32, (blk, 2 * blk), 0)
        col = lax.broadcasted_iota(jnp.int32, (blk, 2 * blk), 1)
        delta = row + blk - col
        valid = (delta >= 0) & (delta <= n_steps) & ((col >= blk) | (jb > 0))
        dist = jnp.where(valid, (delta * d).astype(F32), -NEG)
        lane = lax.broadcasted_iota(jnp.int32, (blk, LANES), 1)
        ltile = jnp.zeros((blk, LANES), F32)
        for h in range(H):
            hs = slice(h * HEAD_DIM, (h + 1) * HEAD_DIM)
            k2 = jnp.concatenate([kp_ref[:, hs], kc_ref[:, hs]], axis=0)
            s_scr[h] = lax.dot_general(q_ref[:, hs], k2, (((1,), (1,)), ((), ())), preferred_element_type=F32)
        for h in range(H):
            s = s_scr[h] * scale - float(slopes[h]) * dist
            m = jnp.max(s, axis=-1, keepdims=True)
            p = jnp.exp(s - m)
            l = jnp.sum(p, axis=-1, keepdims=True)
            p_scr[h] = (p / l).astype(BF16)
            ltile = jnp.where(lane == h, m + jnp.log(l), ltile)
        for h in range(H):
            hs = slice(h * HEAD_DIM, (h + 1) * HEAD_DIM)
            v2 = jnp.concatenate([vp_ref[:, hs], vc_ref[:, hs]], axis=0)
            o_ref[:, hs] = jnp.dot(p_scr[h], v2, preferred_element_type=F32).astype(o_ref.dtype)
        l_ref[...] = ltile

    def cur(width, off):
        return pl.BlockSpec((blk, width), lambda r, jb: (r * nbs + jb, off))

    def prv(width, off):
        return pl.BlockSpec((blk, width), lambda r, jb: (r * nbs + jnp.maximum(jb - 1, 0), off))

    return _pcall(
        body, name=name, grid=(d, nbs),
        in_specs=[cur(DA, 0), prv(DA, 0), cur(DA, 0), prv(DA, 1), cur(DA, 1)],
        out_specs=[cur(DA, 0), cur(LANES, 0)],
        out_shape=[jax.ShapeDtypeStruct((S, DA), BF16), jax.ShapeDtypeStruct((S, LANES), F32)],
        scratch_shapes=[pltpu.VMEM((H, blk, 2 * blk), F32), pltpu.VMEM((H, blk, 2 * blk), BF16)],
        args=[q, kv, kv, kv, kv], comm=comm, carry_us=ATTN_US_PER_ELEM[0] * S * DA)


def attn_combine(os_, lses, name):
    S, DA = os_[0].shape
    H = DA // HEAD_DIM
    tm = _tile(S, 256, 16)
    expander = _head_expander(H, DA)
    nbr = len(os_)

    def body(*refs):
        o_refs, l_refs, e_ref = refs[:nbr], refs[nbr:2 * nbr], refs[2 * nbr]
        out_ref, lse_ref = refs[2 * nbr + 1:]
        ls = [r[...] for r in l_refs]
        lmax = functools.reduce(jnp.maximum, ls)
        es = [jnp.exp(l - lmax) for l in ls]
        den = functools.reduce(lambda a, b: a + b, es)
        lse_ref[...] = lmax + jnp.log(den)
        acc = jnp.zeros((tm, DA), F32)
        for e, o_ref in zip(es, o_refs):
            acc = acc + _split_dot(e / den, e_ref[...], (((1,), (0,)), ((), ()))) * o_ref[...]
        out_ref[...] = acc.astype(out_ref.dtype)

    row = pl.BlockSpec((tm, DA), lambda i: (i, 0))
    lrow = pl.BlockSpec((tm, LANES), lambda i: (i, 0))
    return pl.pallas_call(
        body, name=name, grid=(S // tm,),
        in_specs=[row] * nbr + [lrow] * nbr + [pl.BlockSpec((LANES, DA), lambda i: (0, 0))],
        out_specs=[row, lrow], out_shape=[jax.ShapeDtypeStruct((S, DA), BF16), jax.ShapeDtypeStruct((S, LANES), F32)],
        compiler_params=_cparams(("parallel",)))(*os_, *lses, expander)


def attn_delta(do, o, name):
    S, DA = o.shape
    H = DA // HEAD_DIM
    tm = _tile(S, 512, 16)
    expander = _head_expander(H, DA)

    def body(do_ref, o_ref, e_ref, d_ref):
        prod = do_ref[...].astype(F32) * o_ref[...].astype(F32)
        d_ref[...] = _split_dot(prod, e_ref[...], (((1,), (1,)), ((), ())))

    row = pl.BlockSpec((tm, DA), lambda i: (i, 0))
    return pl.pallas_call(body, name=name, grid=(S // tm,), in_specs=[row, row, pl.BlockSpec((LANES, DA), lambda i: (0, 0))],
                          out_specs=pl.BlockSpec((tm, LANES), lambda i: (i, 0)), out_shape=jax.ShapeDtypeStruct((S, LANES), F32),
                          compiler_params=_cparams(("parallel",)))(do, o, expander)


def attn_branch_bwd(q, kv, do, lse, dlt, gi, slopes, name, out_cols=None, comm=None):
    S, DA = q.shape
    H = DA // HEAD_DIM
    window, d = BRANCHES[gi]
    n_steps = window // d
    blk = ATTN_BLOCK
    nbs = S // d // blk
    scale = HEAD_DIM ** -0.5
    nt, tn = (((1,), (1,)), ((), ())), (((0,), (0,)), ((), ()))

    def body(*refs):
        k_ref, v_ref, qc_ref, qn_ref, doc_ref, don_ref, lc_ref, ln_ref, dc_ref, dn_ref = refs[:10]
        dq_ref, dkv_ref, carry, s_scr, dp_scr, p_scr, ds_scr = refs[-7:]
        kb = pl.program_id(1)

        @pl.when(kb == 0)
        def _():
            carry[...] = jnp.zeros_like(carry)
        row = lax.broadcasted_iota(jnp.int32, (2 * blk, blk), 0)
        col = lax.broadcasted_iota(jnp.int32, (2 * blk, blk), 1)
        delta = row - col
        valid = (delta >= 0) & (delta <= n_steps) & ((row < blk) | (kb < nbs - 1))
        dist = jnp.where(valid, (delta * d).astype(F32), -NEG)
        l2 = jnp.concatenate([lc_ref[...], ln_ref[...]], axis=0)
        d2 = jnp.concatenate([dc_ref[...], dn_ref[...]], axis=0)
        for h in range(H):
            hs = slice(h * HEAD_DIM, (h + 1) * HEAD_DIM)
            q2 = jnp.concatenate([qc_ref[:, hs], qn_ref[:, hs]], axis=0)
            do2 = jnp.concatenate([doc_ref[:, hs], don_ref[:, hs]], axis=0)
            s_scr[h] = lax.dot_general(q2, k_ref[:, hs], nt, preferred_element_type=F32)
            dp_scr[h] = lax.dot_general(do2, v_ref[:, hs], nt, preferred_element_type=F32)
        for h in range(H):
            p = jnp.exp(s_scr[h] * scale - float(slopes[h]) * dist - _lane_col(l2, h))
            p_scr[h] = p.astype(BF16)
            ds_scr[h] = (p * (dp_scr[h] - _lane_col(d2, h))).astype(BF16)
        for h in range(H):
            hs = slice(h * HEAD_DIM, (h + 1) * HEAD_DIM)
            vs = slice(DA + h * HEAD_DIM, DA + (h + 1) * HEAD_DIM)
            q2 = jnp.concatenate([qc_ref[:, hs], qn_ref[:, hs]], axis=0)
            do2 = jnp.concatenate([doc_ref[:, hs], don_ref[:, hs]], axis=0)
            dvh = lax.dot_general(p_scr[h], do2, tn, preferred_element_type=F32)
            dkh = lax.dot_general(ds_scr[h], q2, tn, preferred_element_type=F32) * scale
            dq2 = jnp.dot(ds_scr[h], k_ref[:, hs], preferred_element_type=F32) * scale
            dq_ref[:, hs] = (carry[:, hs] + dq2[:blk]).astype(dq_ref.dtype)
            carry[:, hs] = dq2[blk:]
            dkv_ref[:, hs] = dkh.astype(dkv_ref.dtype)
            dkv_ref[:, vs] = dvh.astype(dkv_ref.dtype)

    def cur(width, off):
        return pl.BlockSpec((blk, width), lambda r, kb: (r * nbs + kb, off))

    def nxt(width, off):
        return pl.BlockSpec((blk, width), lambda r, kb: (r * nbs + jnp.minimum(kb + 1, nbs - 1), off))

    in_specs = [cur(DA, 0), cur(DA, 1), cur(DA, 0), nxt(DA, 0), cur(DA, 0), nxt(DA, 0),
                cur(LANES, 0), nxt(LANES, 0), cur(LANES, 0), nxt(LANES, 0)]
    args = [kv, kv, q, q, do, do, lse, lse, dlt, dlt]
    return _pcall(
        body, name=name, grid=(d, nbs), in_specs=in_specs, out_specs=[cur(DA, 0), cur(2 * DA, 0)],
        out_shape=[jax.ShapeDtypeStruct((S, out_cols or DA), BF16), jax.ShapeDtypeStruct((S, 2 * DA), BF16)],
        scratch_shapes=[pltpu.VMEM((blk, DA), F32), pltpu.VMEM((H, 2 * blk, blk), F32), pltpu.VMEM((H, 2 * blk, blk), F32),
                        pltpu.VMEM((H, 2 * blk, blk), BF16), pltpu.VMEM((H, 2 * blk, blk), BF16)],
        args=args, comm=comm, carry_us=ATTN_US_PER_ELEM[1] * S * DA)


def ada_project(c16, w3, b3, name):
    L, D, Ns = w3.shape
    tn = _tile(Ns, 512, LANES)

    def body(c_ref, w_ref, b_ref, o_ref):
        cv = c_ref[...]
        cond = (cv * _sigmoid(cv)).astype(BF16)
        o_ref[...] = jnp.dot(cond, w_ref[...].astype(BF16), preferred_element_type=F32) + b_ref[...]

    return pl.pallas_call(
        body, name=name, grid=(L, Ns // tn),
        in_specs=[pl.BlockSpec((16, D), lambda l, j: (0, 0)), pl.BlockSpec((None, D, tn), lambda l, j: (l, 0, j)),
                  pl.BlockSpec((None, 1, tn), lambda l, j: (l, 0, j))],
        out_specs=pl.BlockSpec((None, 16, tn), lambda l, j: (l, 0, j)), out_shape=jax.ShapeDtypeStruct((L, 16, Ns), F32),
        compiler_params=_cparams(("parallel", "parallel")))(c16, w3, b3)


def _adamw(w, g, m, v):
    m = B1 * m + (1.0 - B1) * g
    v = B2 * v + (1.0 - B2) * (g * g)
    m_hat = m / (1.0 - B1 ** STEP)
    v_hat = v / (1.0 - B2 ** STEP)
    delta = -LR * (m_hat / (jnp.sqrt(v_hat) + ADAM_EPS) + WD * w)
    return delta, m, v


def ada_grad_adamw(c16, d3, w3, m3, v3, name):
    L, D, Ns = w3.shape
    tk = _tile(D, 256, 8)

    def body(c_ref, d_ref, w_ref, m_ref, v_ref, g_out, dl_out, m_out, v_out):
        cv = c_ref[...]
        cond = (cv * _sigmoid(cv)).astype(BF16)
        g = lax.dot_general(cond, d_ref[...].astype(BF16), (((0,), (0,)), ((), ())), preferred_element_type=F32)
        g_out[...] = g
        dl_out[...], m_out[...], v_out[...] = _adamw(w_ref[...], g, m_ref[...], v_ref[...])

    wspec = pl.BlockSpec((None, tk, Ns), lambda l, kj: (l, kj, 0))
    return pl.pallas_call(
        body, name=name, grid=(L, D // tk),
        in_specs=[pl.BlockSpec((16, tk), lambda l, kj: (0, kj)), pl.BlockSpec((None, 16, Ns), lambda l, kj: (l, 0, 0)), wspec, wspec, wspec],
        out_specs=[wspec] * 4, out_shape=[jax.ShapeDtypeStruct((L, D, Ns), F32)] * 4,
        compiler_params=_cparams(("parallel", "parallel")))(c16, d3, w3, m3, v3)


def adamw(w, g, m, v, name):
    R, C = w.shape
    tr = _tile(R, 256, 8)

    def body(w_ref, g_ref, m_ref, v_ref, g_out, dl_out, m_out, v_out):
        g = g_ref[...]
        g_out[...] = g
        dl_out[...], m_out[...], v_out[...] = _adamw(w_ref[...], g, m_ref[...], v_ref[...])

    spec = pl.BlockSpec((tr, C), lambda i: (i, 0))
    return pl.pallas_call(body, name=name, grid=(R // tr,), in_specs=[spec] * 4, out_specs=[spec] * 4,
                          out_shape=[jax.ShapeDtypeStruct((R, C), F32)] * 4, compiler_params=_cparams(("parallel",)))(w, g, m, v)


def sum_partials(own, recv, g_prev, layer, n_layers, pos, name):
    _, Rh, C = recv.shape
    tr = _tile(Rh, 256, 16)

    def body(pos_ref, own_ref, recv_ref, *rest):
        acc = own_ref[...].astype(F32)
        for rel in range(7):
            acc = acc + recv_ref[rel].astype(F32)
        rest[-1][...] = acc

    in_specs = [pl.BlockSpec((None, None, tr, C), lambda r, pos: (pos[1], pos[0], r, 0)), pl.BlockSpec((7, tr, C), lambda r, pos: (0, r, 0))]
    args = [pos, own.reshape(N_CHIPS, 2, Rh, C), recv]
    aliases = {}
    if g_prev is not None:
        in_specs.append(pl.BlockSpec(memory_space=pl.ANY))
        args.append(g_prev)
        aliases = {3: 0}
    return pl.pallas_call(
        body, name=name,
        grid_spec=pltpu.PrefetchScalarGridSpec(
            num_scalar_prefetch=1, grid=(Rh // tr,), in_specs=in_specs,
            out_specs=pl.BlockSpec((None, None, tr, C), lambda r, pos: (layer, pos[0], r, 0))),
        out_shape=jax.ShapeDtypeStruct((n_layers, 2, Rh, C), F32), input_output_aliases=aliases,
        compiler_params=_cparams(("parallel",)))(*args)


def sum_rows8(g8, name):
    _, R, C = g8.shape

    def body(g_ref, o_ref):
        acc = g_ref[0]
        for i in range(1, N_DEV):
            acc = acc + g_ref[i]
        o_ref[...] = acc

    return pl.pallas_call(body, name=name, grid=(1,), in_specs=[pl.BlockSpec((N_DEV, R, C), lambda i: (0, 0, 0))],
                          out_specs=pl.BlockSpec((R, C), lambda i: (0, 0)), out_shape=jax.ShapeDtypeStruct((R, C), F32),
                          compiler_params=_cparams(("arbitrary",)))(g8)


def _pack(vecs):
    flat = [v.reshape(-1).astype(F32) for v in vecs]
    sizes = [f.shape[0] for f in flat]
    total = sum(sizes)
    padded = -(-total // (8 * PACK_W)) * (8 * PACK_W)
    buf = jnp.concatenate(flat + [jnp.zeros((padded - total,), F32)])
    offs = np.concatenate([[0], np.cumsum(sizes)])
    return buf.reshape(-1, PACK_W), offs


def _unpack(buf, offs, shapes):
    flat = buf.reshape(-1)
    return [flat[int(offs[i]):int(offs[i + 1])].reshape(s) for i, s in enumerate(shapes)]


def kernel(x, c, ada_w, ada_b, norm1_g, norm2_g, pool_w_in, pool_w_grp, pool_scale, pool_w_out, kv_norm_g, kv_ada_w, kv_ada_b, w_kv, attn_w_q, attn_w_o, ffn_w_up, ffn_conv_w, ffn_conv_b, ffn_w_down, final_g, loss_target, m_ada_w, m_ada_b, m_norm1_g, m_norm2_g, m_pool_w_in, m_pool_w_grp, m_pool_scale, m_pool_w_out, m_kv_norm_g, m_kv_ada_w, m_kv_ada_b, m_w_kv, m_attn_w_q, m_attn_w_o, m_ffn_w_up, m_ffn_conv_w, m_ffn_conv_b, m_ffn_w_down, m_final_g, v_ada_w, v_ada_b, v_norm1_g, v_norm2_g, v_pool_w_in, v_pool_w_grp, v_pool_scale, v_pool_w_out, v_kv_norm_g, v_kv_ada_w, v_kv_ada_b, v_w_kv, v_attn_w_q, v_attn_w_o, v_ffn_w_up, v_ffn_conv_w, v_ffn_conv_b, v_ffn_w_down, v_final_g):
    S, D = x.shape[1], x.shape[2]
    depth = ada_w.shape[0]
    n_pool = pool_w_in.shape[0]
    n_attn = attn_w_q.shape[0]
    G = len(POOL_WINDOWS)
    NB = len(BRANCHES)
    DA = attn_w_o.shape[1] * N_CHIPS
    H = DA // HEAD_DIM
    F = ffn_conv_b.shape[1]
    Fs = F // N_CHIPS
    Dq = D // N_CHIPS
    ada_ns = ada_w.shape[2]
    kvada_ns = kv_ada_w.shape[1]
    slopes = _alibi_slopes(NB * H).reshape(NB, H)

    ix, iy, ic = lax.axis_index("x"), lax.axis_index("y"), lax.axis_index("c")
    p_me = 2 * ix + iy
    b_me = 4 * ix + 2 * iy + ic
    pos = jnp.stack([ic, p_me]).astype(jnp.int32)
    xs, tgt = x[0], loss_target[0]

    pk, offs = _pack([c, pool_scale, ffn_conv_w])
    rows1 = pk.shape[0]
    got = all_gather8(pk, "gather_small_in").reshape(N_DEV, rows1, PACK_W)
    c8 = got.reshape(N_DEV, -1)[:, :D]
    c16 = jnp.concatenate([c8, jnp.zeros_like(c8)], axis=0)
    chip_rows = got[0::2].reshape(N_CHIPS, -1)
    scale_full = chip_rows[:, int(offs[1]):int(offs[2])].reshape(N_CHIPS, n_pool, Dq).transpose(1, 0, 2).reshape(n_pool, D)
    convw_full = chip_rows[:, int(offs[2]):int(offs[3])].reshape(N_CHIPS, depth, 3, Fs).transpose(1, 2, 0, 3).reshape(depth, 3, F)

    ada_b_loc = lax.dynamic_slice(ada_b, (0, p_me * ada_ns), (depth, ada_ns)).reshape(depth, 1, ada_ns)
    kvb_loc = lax.dynamic_slice(kv_ada_b, (p_me * kvada_ns,), (kvada_ns,)).reshape(1, 1, kvada_ns)
    mods_loc = ada_project(c16, ada_w, ada_b_loc, "ada_project")[:, :N_DEV]
    kvmod_loc = ada_project(c16, kv_ada_w.reshape(1, D, kvada_ns), kvb_loc, "kv_ada_project")[0, :N_DEV]
    mods_cat = jnp.concatenate([mods_loc.transpose(1, 0, 2).reshape(N_DEV, depth * ada_ns), kvmod_loc], axis=1)
    mods_all = all_gather8(mods_cat, "gather_mods").reshape(N_CHIPS, 2, N_DEV, -1)
    mine = lax.dynamic_index_in_dim(mods_all[:, 0], b_me, axis=1, keepdims=False)
    mod = mine[:, :depth * ada_ns].reshape(N_CHIPS, depth, ada_ns).transpose(1, 0, 2).reshape(depth, 6, 1, D)
    kvmod = mine[:, depth * ada_ns:].reshape(2, 1, D)

    comm = _Comm()
    C = D // G
    kv_ns, q_ns, up_ns = w_kv.shape[1], attn_w_q.shape[2], ffn_w_up.shape[2]

    def layer_shards(l):
        sh = []
        if l < n_pool:
            sh += [(("pin", l), pool_w_in[l]), (("pgrp", l), pool_w_grp[l].reshape(-1, C)), (("pout", l), pool_w_out[l])]
        else:
            if l == n_pool:
                sh.append((("kv", 0), w_kv))
            sh += [(("wq", l), attn_w_q[l - n_pool]), (("wo", l), attn_w_o[l - n_pool])]
        sh += [(("up", l), ffn_w_up[l]), (("down", l), ffn_w_down[l])]
        return [(k, w.astype(BF16)) for k, w in sh]

    def weight(key, shape):
        return comm.require(key).reshape(shape)

    dil = [d for _, d in BRANCHES]
    kv_tn = DA // 2
    q_tn = DA // 4
    q_bwd_tn = q_ns
    up_tn = up_ns
    up_per_half = F // up_tn

    def up_gmap(s):
        return s // up_per_half, s % up_per_half

    def vec(v):
        return v.reshape(1, -1)

    saved = []
    xcur = xs
    kvs = None
    wts = {}
    push_gather(comm, layer_shards(0))
    for l in range(depth):
        if l + 1 < depth:
            push_gather(comm, layer_shards(l + 1))
        sh1, sc1, g1, sh2, sc2, g2 = [mod[l, i] for i in range(6)]
        st = {"x0": xcur}
        h1 = norm_mod(xcur, vec(norm1_g[l]), sh1, sc1, "norm_mod", comm=comm)
        st["h1"] = h1
        if l < n_pool:
            wts["pin", l] = weight(("pin", l), (1, D, D))
            u = mm_nn(h1, wts["pin", l], tn=D, out_dtype=F32, name="pool_in_proj", comm=comm)
            wts["pgrp", l] = weight(("pgrp", l), (N_CHIPS, G, C // N_CHIPS, C))
            pooled, z, ys = pool_fwd(u, wts["pgrp", l], vec(scale_full[l]), "pool_mix")
            wts["pout", l] = weight(("pout", l), (1, D, D))
            out, x1 = mm_nn(ys, wts["pout", l], tn=D, out_dtype=BF16, name="pool_out_proj", res=(xcur, g1), comm=comm)
            st.update(pooled=pooled, z=z, ys=ys, out1=out)
        else:
            if l == n_pool:
                wts["kv", 0] = weight(("kv", 0), (N_CHIPS, D, kv_ns))
                hkv = norm_mod(xcur, vec(kv_norm_g), kvmod[0], kvmod[1], "norm_mod", comm=comm)
                kvs = [mm_nn(hkv, wts["kv", 0], tn=kv_tn, out_dtype=BF16, name=f"kv_proj_b{gi}", ncb=4, perm_d=dil[gi], comm=comm,
                             cbmap=functools.partial(lambda jj, gi: 2 * gi + (jj // 2) * 2 * NB + jj % 2, gi=gi)) for gi in range(NB)]
                kv_state = {"x": xcur, "hkv": hkv}
            wts["wq", l] = weight(("wq", l), (N_CHIPS, D, q_ns))
            qs, os_, lses = [], [], []
            for gi in range(NB):
                q_b = mm_nn(h1, wts["wq", l], tn=q_tn, out_dtype=BF16, name=f"q_proj_b{gi}", ncb=4, perm_d=dil[gi], comm=comm,
                            cbmap=functools.partial(lambda jj, gi: 4 * gi + jj, gi=gi))
                o_b, l_b = attn_branch_fwd(q_b, kvs[gi], gi, slopes[gi], f"attn_fwd_b{gi}", comm=comm)
                if dil[gi] > 1:
                    o_b = unpermute_rows(o_b, dil[gi], f"unpermute_o_b{gi}")
                    l_b = unpermute_rows(l_b, dil[gi], f"unpermute_lse_b{gi}")
                qs.append(q_b)
                os_.append(o_b)
                lses.append(l_b)
            o, lse = attn_combine(os_, lses, "attn_combine")
            wts["wo", l] = weight(("wo", l), (1, DA, D))
            out, x1 = mm_nn(o, wts["wo", l], tn=D, out_dtype=BF16, name="attn_out_proj", res=(xcur, g1), comm=comm)
            st.update(qs=qs, o=o, lse=lse, out1=out)
        st["x1"] = x1
        h2 = norm_mod(x1, vec(norm2_g[l]), sh2, sc2, "norm_mod", comm=comm)
        wts["up", l] = weight(("up", l), (N_CHIPS, D, up_ns))
        hu, gated, conv = ffn_up_act(h2, wts["up", l], convw_full[l], vec(ffn_conv_b[l]), "ffn_up_act", comm=comm)
        wts["down", l] = weight(("down", l), (1, F, D))
        out2, x2 = mm_nn(gated, wts["down", l], tn=D, out_dtype=BF16, name="ffn_down_proj", res=(x1, g2), comm=comm)
        st.update(h2=h2, hu=hu, conv=conv, gated=gated, out2=out2)
        saved.append(st)
        xcur = x2
    comm.flush()

    dx, fsums, dout2 = loss_fwd_bwd(xcur, vec(final_g), tgt, (mod[depth - 1, 5], saved[depth - 1]["out2"]), "loss_head")
    loss = lax.psum(0.5 * jnp.sum(fsums[1]) / D, ("x", "y", "c"))
    d_final_g, s_g2 = fsums[0], fsums[2]

    dmods = [None] * depth
    d_n1 = [None] * depth
    d_n2 = [None] * depth
    d_convw = [None] * depth
    d_convb = [None] * depth
    d_scale = [None] * n_pool
    d_grp = [None] * n_pool
    dkvs = [[] for _ in range(NB)]
    exchanged = []
    f_tk = _tile(F, 1408, LANES)
    ct_blocks = DA // _tile(DA, PERM_COLS, LANES)

    def exchange(name, idx, dw):
        dw4 = dw.reshape(N_CHIPS, -1, dw.shape[-1])
        exchanged.append((name, idx, dw4))
        push_exchange(comm, (name, idx), dw4)

    for l in reversed(range(depth)):
        st = saved[l]
        sh1, sc1, g1, sh2, sc2, g2 = [mod[l, i] for i in range(6)]
        dout2_3 = dout2.reshape(1, S, D)
        exchange("down", l, mm_tn(st["gated"], dout2_3, (1, F, D), tn=D, tk=f_tk, name="ffn_down_dw", comm=comm))
        dhu, s_conv = ffn_act_bwd(dout2, wts["down", l], st["hu"], st["conv"], convw_full[l], "ffn_act_bwd", comm=comm)
        dh2 = mm_nt(dhu, wts["up", l], tn=up_tn, tk=D, out_dtype=F32, name="ffn_up_bwd", gmap=up_gmap, comm=comm)
        exchange("up", l, mm_tn(st["h2"], dhu, (N_CHIPS, D, up_ns), tn=up_tn, tk=D, name="ffn_up_dw", gmap=up_gmap, comm=comm))
        dx, s_n2, dout1 = norm_mod_bwd(dh2, st["x1"], dx, vec(norm2_g[l]), sc2, "norm_mod_bwd_gate", below=(g1, st["out1"]), comm=comm)
        d_convw[l], d_convb[l] = s_conv[0:3], s_conv[3]
        d_n2[l] = s_n2[2]
        dout1_3 = dout1.reshape(1, S, D)
        if l < n_pool:
            dys = mm_nt(dout1_3, wts["pout", l], tn=D, tk=D // 2, out_dtype=F32, name="pool_out_bwd", comm=comm)
            exchange("pout", l, mm_tn(st["ys"], dout1_3, (1, D, D), tn=D, tk=D, name="pool_out_dw", comm=comm))
            du, d_grp, s_sc = pool_bwd(dys, st["z"], st["pooled"], wts["pgrp", l], vec(scale_full[l]), "pool_mix_bwd")
            exchange("pgrp", l, d_grp.astype(BF16).reshape(G, N_CHIPS, C // N_CHIPS, C).transpose(1, 0, 2, 3))
            d_scale[l] = s_sc[0]
            du_3 = du.reshape(1, S, D)
            dh1 = mm_nt(du_3, wts["pin", l], tn=D, tk=D // 2, out_dtype=F32, name="pool_in_bwd", comm=comm)
            exchange("pin", l, mm_tn(st["h1"], du_3, (1, D, D), tn=D, tk=D, name="pool_in_dw", comm=comm))
        else:
            j = l - n_pool
            do = mm_nt(dout1_3, wts["wo", l], tn=D, tk=DA // 2, out_dtype=BF16, name="attn_out_bwd", comm=comm)
            exchange("wo", j, mm_tn(st["o"], dout1_3, (1, DA, D), tn=D, tk=DA, name="attn_out_dw", comm=comm))
            dlt = attn_delta(do, st["o"], "attn_delta")
            dq = None
            for gi in range(NB):
                d = dil[gi]
                do_b, l_b, dl_b = do, st["lse"], dlt
                if d > 1:
                    do_b = permute_rows(do, d, f"permute_do_b{gi}")
                    l_b = permute_rows(st["lse"], d, f"permute_lse_b{gi}")
                    dl_b = permute_rows(dlt, d, f"permute_delta_b{gi}")
                bwd_name = f"attn_bwd_b{gi}"
                if d > 1:
                    dq_b, dkv_b = attn_branch_bwd(st["qs"][gi], kvs[gi], do_b, l_b, dl_b, gi, slopes[gi], bwd_name, comm=comm)
                    dq = unpermute_rows(dq_b, d, f"unpermute_dq_b{gi}", into=dq, total_cols=NB * DA,
                                        colmap=functools.partial(lambda jj, gi: gi * ct_blocks + jj, gi=gi))
                else:
                    dq, dkv_b = attn_branch_bwd(st["qs"][gi], kvs[gi], do_b, l_b, dl_b, gi, slopes[gi], bwd_name,
                                                out_cols=NB * DA, comm=comm)
                dkvs[gi].append(dkv_b)
            dq_3 = dq.reshape(1, S, NB * DA)
            dh1 = mm_nt(dq_3, wts["wq", l], tn=q_bwd_tn, tk=D, out_dtype=F32, name="q_proj_bwd", comm=comm)
            exchange("wq", j, mm_tn(st["h1"], dq_3, (N_CHIPS, D, q_ns), tn=q_bwd_tn, tk=D, name="q_proj_dw", comm=comm))
        below = (mod[l - 1, 5], saved[l - 1]["out2"]) if l > 0 else None
        if l == n_pool or below is None:
            dx, s_n1 = norm_mod_bwd(dh1, st["x0"], dx, vec(norm1_g[l]), sc1, "norm_mod_bwd", comm=comm if l > 0 else None)
        else:
            dx, s_n1, dout2 = norm_mod_bwd(dh1, st["x0"], dx, vec(norm1_g[l]), sc1, "norm_mod_bwd_gate", below=below, comm=comm)
        d_n1[l] = s_n1[2]
        dmods[l] = jnp.stack([s_n1[0], s_n1[1], s_n2[3], s_n2[0], s_n2[1], s_g2])
        if l > 0 and l != n_pool:
            s_g2 = s_n1[3]
        if l == n_pool:
            dkv = None
            for gi in range(NB):
                dkv = unpermute_rows(dkvs[gi], dil[gi], f"unpermute_dkv_b{gi}", into=dkv, total_cols=2 * NB * DA,
                                     colmap=functools.partial(lambda jj, gi: (jj // ct_blocks) * NB * ct_blocks + gi * ct_blocks + jj % ct_blocks,
                                                              gi=gi))
            dkv_3 = dkv.reshape(1, S, 2 * NB * DA)
            dhkv = mm_nt(dkv_3, wts["kv", 0], tn=kv_ns // 2, tk=D, out_dtype=F32, name="kv_proj_bwd", comm=comm)
            exchange("kv", 0, mm_tn(kv_state["hkv"], dkv_3, (N_CHIPS, D, kv_ns), tn=kv_ns // 2, tk=D, name="kv_proj_dw", comm=comm))
            dx, s_kv, dout2 = norm_mod_bwd(dhkv, kv_state["x"], dx, vec(kv_norm_g), kvmod[1], "norm_mod_bwd_gate", below=below, comm=comm)
            s_g2 = s_kv[3]
    grad_x = dx.reshape(1, S, D)

    smalls = [jnp.stack(dmods), jnp.stack([s_kv[0], s_kv[1]]), jnp.stack(d_n1), jnp.stack(d_n2), s_kv[2], jnp.stack(d_convb), d_final_g,
              jnp.stack(d_scale), jnp.stack(d_convw)]
    small_shapes = [s.shape for s in smalls]
    spk, soffs = _pack(smalls)
    srows = spk.shape[0]
    sgot = all_gather8(spk, "gather_small_grads").reshape(N_DEV, srows, PACK_W)
    ssum = sum_rows8(sgot, "sum_small_grads")
    g_mods, g_kvmod, g_n1, g_n2, g_kvn, g_convb, g_fg, g_scale_full, g_convw_full = _unpack(ssum, soffs, small_shapes)
    g_ada_b = g_mods.reshape(depth, 6 * D)
    g_kv_ada_b = g_kvmod.reshape(2 * D)
    g_scale = lax.dynamic_slice(g_scale_full, (0, p_me * Dq), (n_pool, Dq))
    g_convw = lax.dynamic_slice(g_convw_full, (0, 0, p_me * Fs), (depth, 3, Fs))

    small_w = [ada_b, norm1_g, norm2_g, kv_norm_g, kv_ada_b, ffn_conv_b, final_g, pool_scale, ffn_conv_w]
    small_m = [m_ada_b, m_norm1_g, m_norm2_g, m_kv_norm_g, m_kv_ada_b, m_ffn_conv_b, m_final_g, m_pool_scale, m_ffn_conv_w]
    small_v = [v_ada_b, v_norm1_g, v_norm2_g, v_kv_norm_g, v_kv_ada_b, v_ffn_conv_b, v_final_g, v_pool_scale, v_ffn_conv_w]
    small_g = [g_ada_b, g_n1, g_n2, g_kvn, g_kv_ada_b, g_convb, g_fg, g_scale, g_convw]
    sw_shapes = [w.shape for w in small_w]
    pw, woffs = _pack(small_w)
    s_res = adamw(pw, _pack(small_g)[0], _pack(small_m)[0], _pack(small_v)[0], "adamw_small")
    s_g, s_dl, s_m, s_v = [_unpack(r, woffs, sw_shapes) for r in s_res]

    per_dev = sgot.reshape(N_DEV, -1)
    dm_all = per_dev[:, int(soffs[0]):int(soffs[1])].reshape(N_DEV, depth, 6 * D)
    dkvm_all = per_dev[:, int(soffs[1]):int(soffs[2])].reshape(N_DEV, 1, 2 * D)

    def shard_cols(a, ns):
        sl = lax.dynamic_slice_in_dim(a, p_me * ns, ns, axis=2).transpose(1, 0, 2)
        return jnp.concatenate([sl, jnp.zeros_like(sl)], axis=1)

    ada_res = ada_grad_adamw(c16, shard_cols(dm_all, ada_ns), ada_w, m_ada_w, v_ada_w, "ada_grad_adamw")
    kvada_res = ada_grad_adamw(c16, shard_cols(dkvm_all, kvada_ns), kv_ada_w.reshape(1, D, kvada_ns), m_kv_ada_w.reshape(1, D, kvada_ns),
                               v_kv_ada_w.reshape(1, D, kvada_ns), "kv_ada_grad_adamw")
    kvada_res = [r.reshape(D, kvada_ns) for r in kvada_res]

    comm.flush()
    big_names = ["pin", "pgrp", "pout", "kv", "wq", "wo", "up", "down"]
    n_stack = {"pin": n_pool, "pgrp": n_pool, "pout": n_pool, "kv": 1, "wq": n_attn, "wo": n_attn, "up": depth, "down": depth}
    gsum = {nm: None for nm in big_names}
    for nm, idx, dw4 in exchanged:
        gsum[nm] = sum_partials(dw4, comm.store[nm, idx], gsum[nm], idx, n_stack[nm], pos, "sum_partials")
    for nm in big_names:
        comm.push(_Phase(("swap", nm), ("swap", nm), None, ("swap", nm), 0.0, [], None, gsum[nm].shape[0], 0, _build_swap, buffer=gsum[nm]))
    comm.flush()
    gsum = {nm: comm.store["swap", nm] for nm in big_names}
    big_m = [m_pool_w_in, m_pool_w_grp, m_pool_w_out, m_w_kv, m_attn_w_q, m_attn_w_o, m_ffn_w_up, m_ffn_w_down]
    big_v = [v_pool_w_in, v_pool_w_grp, v_pool_w_out, v_w_kv, v_attn_w_q, v_attn_w_o, v_ffn_w_up, v_ffn_w_down]
    big_w = [pool_w_in, pool_w_grp, pool_w_out, w_kv, attn_w_q, attn_w_o, ffn_w_up, ffn_w_down]
    big_res = []
    for nm, w, m_, v_ in zip(big_names, big_w, big_m, big_v):
        cols = gsum[nm].shape[-1]
        res = adamw(w.reshape(-1, cols), gsum[nm].reshape(-1, cols), m_.reshape(-1, cols), v_.reshape(-1, cols), "adamw_big")
        big_res.append([r.reshape(w.shape) for r in res])

    order = ["ada_w", "ada_b", "norm1_g", "norm2_g", "pool_w_in", "pool_w_grp", "pool_scale", "pool_w_out", "kv_norm_g", "kv_ada_w",
             "kv_ada_b", "w_kv", "attn_w_q", "attn_w_o", "ffn_w_up", "ffn_conv_w", "ffn_conv_b", "ffn_w_down", "final_g"]
    small_names = ["ada_b", "norm1_g", "norm2_g", "kv_norm_g", "kv_ada_b", "ffn_conv_b", "final_g", "pool_scale", "ffn_conv_w"]
    results = {"ada_w": ada_res, "kv_ada_w": kvada_res}
    for i, nm in enumerate(small_names):
        results[nm] = [s_g[i], s_dl[i], s_m[i], s_v[i]]
    for i, nm in enumerate(["pool_w_in", "pool_w_grp", "pool_w_out", "w_kv", "attn_w_q", "attn_w_o", "ffn_w_up", "ffn_w_down"]):
        results[nm] = big_res[i]
    outs = [loss, grad_x]
    for kind in range(4):
        outs += [results[nm][kind] for nm in order]
    return tuple(outs)
```

```python
import functools
import math

import numpy as np
import jax
import jax.numpy as jnp
from jax import lax
from jax.experimental import pallas as pl
from jax.experimental.pallas import tpu as pltpu

F32 = jnp.float32
BF16 = jnp.bfloat16
MESH = pl.DeviceIdType.MESH

POOL_WINDOWS = (2, 4, 8, 16)
BRANCHES = ((128, 1), (512, 4), (2048, 16))
HEAD_DIM = 64
ATTN_BLOCK = 128
EPS = 1e-6
LR, B1, B2, ADAM_EPS, WD, STEP = 0.001, 0.9, 0.999, 1e-08, 0.01, 10

VMEM_LIMIT_BYTES = 56 * 1024 * 1024
LANES = 128
PACK_W = 1024
HALO = 16
NEG = -1e30
N_CHIPS = 4
N_DEV = 8


def _alibi_slopes(n):
    def pow2(m):
        start = 2.0 ** (-(2.0 ** -(math.log2(m) - 3)))
        return [start ** (i + 1) for i in range(m)]
    if math.log2(n).is_integer():
        s = pow2(n)
    else:
        c = 2 ** math.floor(math.log2(n))
        s = pow2(c) + pow2(2 * c)[0::2][: n - c]
    s = np.asarray(s, dtype=np.float32)
    return -np.sort(-s)


def _cparams(sem=None):
    return pltpu.CompilerParams(dimension_semantics=sem, vmem_limit_bytes=VMEM_LIMIT_BYTES)


def _tile(n, pref, unit):
    t = (min(pref, n) // unit) * unit
    while t >= unit:
        if n % t == 0:
            return t
        t -= unit
    return n


def _sigmoid(v):
    return 1.0 / (1.0 + jnp.exp(-v))


def all_gather8(xs, name):
    m_per, n = xs.shape

    def body(x_ref, out_ref, send_sems, recv_sems, local_sem):
        x, y, c = lax.axis_index("x"), lax.axis_index("y"), lax.axis_index("c")
        me, sibling = (x, y, c), (x, y, 1 - c)
        chips = [(1 - x, y), (x, 1 - y), (1 - x, 1 - y)]

        def rows(px, py, pc):
            return out_ref.at[pl.ds((4 * px + 2 * py + pc) * m_per, m_per), :]

        def copy(k, block, to, src=None):
            return pltpu.make_async_remote_copy(src_ref=rows(*block) if src is None else src, dst_ref=rows(*block),
                                                send_sem=send_sems.at[k], recv_sem=recv_sems.at[k], device_id=to, device_id_type=MESH)

        mine = pltpu.make_async_copy(x_ref, rows(*me), local_sem)
        mine.start()
        first = [copy(0, me, sibling, src=x_ref)]
        first += [copy(1 + j, me, (*chip, c), src=x_ref) for j, chip in enumerate(chips)]
        for cp in first:
            cp.start()
        passed = [copy(4 + j, (*chip, c), sibling) for j, chip in enumerate(chips)]
        for j, chip in enumerate(chips):
            copy(1 + j, (*chip, c), me).wait_recv()
            passed[j].start()
        copy(0, sibling, me).wait_recv()
        for j, chip in enumerate(chips):
            copy(4 + j, (*chip, 1 - c), me).wait_recv()
        for cp in first + passed:
            cp.wait_send()
        mine.wait()

    return pl.pallas_call(
        body, name=name,
        out_shape=jax.ShapeDtypeStruct((N_DEV * m_per, n), xs.dtype),
        in_specs=[pl.BlockSpec(memory_space=pltpu.VMEM)],
        out_specs=pl.BlockSpec(memory_space=pltpu.VMEM),
        scratch_shapes=[pltpu.SemaphoreType.DMA((7,)), pltpu.SemaphoreType.DMA((7,)), pltpu.SemaphoreType.DMA],
        compiler_params=pltpu.CompilerParams(vmem_limit_bytes=VMEM_LIMIT_BYTES),
    )(xs)


HBM_SPEC = pl.BlockSpec(memory_space=pltpu.HBM)


def _mesh_pos():
    x, y, c = lax.axis_index("x"), lax.axis_index("y"), lax.axis_index("c")
    return x, y, c, [(1 - x, y), (x, 1 - y), (1 - x, 1 - y)]


class _Phase:
    def __init__(self, key, group, after, owner, est_us, ins, out_shape, n_sems, n_local, build, buffer=None):
        self.key, self.group, self.after, self.owner, self.est_us = key, group, after, owner, est_us
        self.ins, self.out_shape, self.buffer = ins, out_shape, buffer
        self.n_sems, self.n_local, self.build = n_sems, n_local, build


def _rcopy(src, dst, send_sems, recv_sems, k, to):
    return pltpu.make_async_remote_copy(src_ref=src, dst_ref=dst, send_sem=send_sems.at[k], recv_sem=recv_sems.at[k],
                                        device_id=to, device_id_type=MESH)


def _build_fetch(in_refs, g, send_sems, recv_sems, loc_sems, sem0, loc0, rows, whole):
    (shard,) = in_refs
    x, y, c, chips = _mesh_pos()
    p_me = 2 * x + y
    locs = [pltpu.make_async_copy(shard.at[i], g.at[p_me, i], loc_sems.at[loc0 + i]) for i in range(2)] if whole else []
    sends = [_rcopy(shard.at[c, rows], g.at[p_me, c, rows], send_sems, recv_sems, sem0 + j, (*chip, c)) for j, chip in enumerate(chips)]

    def recvs():
        blks = [g.at[2 * chip[0] + chip[1], c, rows] for chip in chips]
        return [_rcopy(blk, blk, send_sems, recv_sems, sem0 + j, (*chip, c)) for j, (blk, chip) in enumerate(zip(blks, chips))]
    return sends, recvs, locs


def _build_pass(in_refs, g, send_sems, recv_sems, loc_sems, sem0, loc0, rows):
    x, y, c, chips = _mesh_pos()
    sib = (x, y, 1 - c)
    slots = [2 * chip[0] + chip[1] for chip in chips]
    sends = [_rcopy(g.at[p, c, rows], g.at[p, c, rows], send_sems, recv_sems, sem0 + j, sib) for j, p in enumerate(slots)]

    def recvs():
        return [_rcopy(g.at[p, 1 - c, rows], g.at[p, 1 - c, rows], send_sems, recv_sems, sem0 + j, sib) for j, p in enumerate(slots)]
    return sends, recvs, []


def _build_exchange(in_refs, recv, send_sems, recv_sems, loc_sems, sem0, loc0, rows):
    (dw,) = in_refs
    x, y, c, chips = _mesh_pos()
    targets = [(c, chip, c, j) for j, chip in enumerate(chips)]
    targets += [(1 - c, chip, 1 - c, 3 + j) for j, chip in enumerate([(x, y)] + chips)]
    sends = [_rcopy(dw.at[2 * chip[0] + chip[1], half, rows], recv.at[rel, rows], send_sems, recv_sems, sem0 + rel, (*chip, core))
             for half, chip, core, rel in targets]

    def recvs():
        return [_rcopy(recv.at[rel, rows], recv.at[rel, rows], send_sems, recv_sems, sem0 + rel, (x, y, 1 - c)) for rel in range(7)]
    return sends, recvs, []


def _build_swap(in_refs, g, send_sems, recv_sems, loc_sems, sem0, loc0):
    x, y, c, _ = _mesh_pos()
    sib = (x, y, 1 - c)
    sends = [_rcopy(g.at[l, c], g.at[l, c], send_sems, recv_sems, sem0 + l, sib) for l in range(g.shape[0])]

    def recvs():
        return [_rcopy(g.at[l, 1 - c], g.at[l, 1 - c], send_sems, recv_sems, sem0 + l, sib) for l in range(g.shape[0])]
    return sends, recvs, []


def _plan_refs(phases, store):
    xin, xout, alias, n_sems, n_loc, out_of = [], [], {}, 0, 0, {}
    for ph in phases:
        ph.sem0, ph.loc0 = n_sems, n_loc
        n_sems += ph.n_sems
        n_loc += ph.n_local
        ph.in0, ph.n_in = len(xin), len(ph.ins)
        xin += ph.ins
        if ph.owner not in out_of:
            out_of[ph.owner] = len(xout)
            if ph.owner == ph.key and ph.buffer is None:
                xout.append(ph.out_shape)
            else:
                buf = ph.buffer if ph.buffer is not None else store[ph.group]
                alias[len(xin)] = len(xout)
                xin.append(buf)
                xout.append(jax.ShapeDtypeStruct(buf.shape, buf.dtype))
        ph.out0 = out_of[ph.owner]
    return xin, xout, alias, max(n_sems, 1), max(n_loc, 1)


def _built(ph, xin_refs, xout_refs, sems):
    return ph.build(xin_refs[ph.in0:ph.in0 + ph.n_in], xout_refs[ph.out0], sems[0], sems[1], sems[2], ph.sem0, ph.loc0)


def _start(phases, xin_refs, xout_refs, sems):
    for ph in phases:
        sends, _, locs = _built(ph, xin_refs, xout_refs, sems)
        for cp in locs + sends:
            cp.start()


def _finish(phases, xin_refs, xout_refs, sems):
    for ph in phases:
        sends, recvs, locs = _built(ph, xin_refs, xout_refs, sems)
        for cp in recvs():
            cp.wait_recv()
        for cp in sends:
            cp.wait_send()
        for cp in locs:
            cp.wait()


class _Comm:
    def __init__(self):
        self.queue, self.store, self.n_alone = [], {}, 0

    def push(self, ph):
        self.queue.append(ph)

    def take(self, carry_us):
        taken, t = [], 0.0
        while True:
            queued = {ph.key for ph in self.queue}
            pending = queued | {ph.key for ph in taken}
            room = 1.5 * carry_us if not taken else carry_us - t
            fits = [ph for ph in self.queue if ph.after not in pending and (ph.owner == ph.key or ph.owner not in queued)
                    and ph.est_us <= room]
            if not fits:
                return taken
            ph = max(fits, key=lambda p: p.est_us)
            self.queue.remove(ph)
            taken.append(ph)
            t += ph.est_us

    def require(self, group):
        phases = [ph for ph in self.queue if ph.group == group]
        if phases:
            self.queue = [ph for ph in self.queue if ph.group != group]
            self.run_alone(phases)
        return self.store[group]

    def flush(self):
        phases, self.queue = self.queue, []
        if phases:
            self.run_alone(phases)

    def run_alone(self, phases):
        phases = [ph for ph in phases if ph.after is None] + [ph for ph in phases if ph.after is not None]
        groups, keys = [[]], set()
        for ph in phases:
            if ph.after in keys:
                groups.append([])
                keys = set()
            groups[-1].append(ph)
            keys.add(ph.key)
        xin, xout, alias, n_sems, n_loc = _plan_refs(phases, self.store)
        n_xin, n_xout = len(xin), len(xout)

        def body(*refs):
            xin_refs, xout_refs, sems = refs[:n_xin], refs[n_xin:n_xin + n_xout], refs[n_xin + n_xout:]
            for grp in groups:
                _start(grp, xin_refs, xout_refs, sems)
                _finish(grp, xin_refs, xout_refs, sems)

        self.n_alone += 1
        outs = pl.pallas_call(
            body, name=f"comm_alone_{self.n_alone}", out_shape=xout, in_specs=[HBM_SPEC] * n_xin, out_specs=[HBM_SPEC] * n_xout,
            input_output_aliases=alias,
            scratch_shapes=[pltpu.SemaphoreType.DMA((n_sems,)), pltpu.SemaphoreType.DMA((n_sems,)), pltpu.SemaphoreType.DMA((n_loc,))],
        )(*xin)
        for ph in phases:
            self.store[ph.group] = outs[ph.out0]


def _pcall(body, *, name, grid, in_specs, out_specs, out_shape, args, scratch_shapes=(), aliases=None, comm=None, carry_us=0.0):
    phases = comm.take(carry_us) if comm is not None else []
    n_in, n_out, n_scr = len(in_specs), len(out_specs), len(scratch_shapes)
    if not phases:
        return pl.pallas_call(body, name=name, grid=grid, in_specs=in_specs, out_specs=out_specs, out_shape=out_shape,
                              scratch_shapes=list(scratch_shapes), input_output_aliases=aliases or {},
                              compiler_params=_cparams(("arbitrary",) * len(grid)))(*args)
    xin, xout, xalias, n_sems, n_loc = _plan_refs(phases, comm.store)
    n_xin, n_xout = len(xin), len(xout)
    all_alias = dict(aliases or {})
    all_alias.update({n_in + i: n_out + o for i, o in xalias.items()})

    def carrier(*refs):
        ins, xin_refs = refs[:n_in], refs[n_in:n_in + n_xin]
        outs = refs[n_in + n_xin:n_in + n_xin + n_out]
        xout_refs = refs[n_in + n_xin + n_out:n_in + n_xin + n_out + n_xout]
        rest = refs[n_in + n_xin + n_out + n_xout:]
        scr, sems = rest[:n_scr], rest[n_scr:]
        pids = [pl.program_id(k) for k in range(len(grid))]
        first = functools.reduce(jnp.logical_and, [p == 0 for p in pids])
        last = functools.reduce(jnp.logical_and, [p == n - 1 for p, n in zip(pids, grid)])

        @pl.when(first)
        def _():
            _start(phases, xin_refs, xout_refs, sems)
        body(*ins, *outs, *scr)

        @pl.when(last)
        def _():
            _finish(phases, xin_refs, xout_refs, sems)

    outs = pl.pallas_call(
        carrier, name=name, grid=grid, in_specs=list(in_specs) + [HBM_SPEC] * n_xin, out_specs=list(out_specs) + [HBM_SPEC] * n_xout,
        out_shape=list(out_shape) + xout,
        scratch_shapes=list(scratch_shapes) + [pltpu.SemaphoreType.DMA((n_sems,)), pltpu.SemaphoreType.DMA((n_sems,)),
                                               pltpu.SemaphoreType.DMA((n_loc,))],
        input_output_aliases=all_alias, compiler_params=_cparams(("arbitrary",) * len(grid)))(*args, *xin)
    for ph in phases:
        comm.store[ph.group] = outs[n_out + ph.out0]
    return outs[:n_out]


FETCH_US_PER_MB = 20.4
PASS_US_PER_MB = 3.3
EXCHANGE_US_PER_MB = 14.5


FETCH_PHASE_US = 35.0
EXCHANGE_PHASE_US = 20.0


def _row_chunks(rows, est_us, phase_us):
    n = 1
    while est_us / n > phase_us and rows % (2 * n) == 0 and (rows // (2 * n)) % 16 == 0:
        n *= 2
    return [pl.ds(k * (rows // n), rows // n) for k in range(n)]


def push_gather(comm, keys_shards):
    prev = []
    for key, shard in keys_shards:
        r, c = shard.shape
        sh = shard.reshape(2, r // 2, c)
        half_mb = r // 2 * c * 2 / 1e6
        chunks = _row_chunks(r // 2, 3 * half_mb * FETCH_US_PER_MB, FETCH_PHASE_US)
        n = len(chunks)
        shape = jax.ShapeDtypeStruct((N_CHIPS, 2, r // 2, c), BF16)
        for k, rows in enumerate(chunks):
            comm.push(_Phase(("fetch", key, k), key, None, ("fetch", key, 0), 3 * half_mb * FETCH_US_PER_MB / n, [sh], shape, 3,
                             2 if k == 0 else 0, functools.partial(_build_fetch, rows=rows, whole=k == 0)))
        for ph in prev:
            comm.push(ph)
        prev = [_Phase(("pass", key, k), key, ("fetch", key, k), ("fetch", key, 0), 3 * half_mb * PASS_US_PER_MB / n + 3.0, [], shape, 3, 0,
                       functools.partial(_build_pass, rows=rows)) for k, rows in enumerate(chunks)]
    for ph in prev:
        comm.push(ph)


def push_exchange(comm, key, dw):
    _, r, c = dw.shape
    half_mb = r // 2 * c * 2 / 1e6
    chunks = _row_chunks(r // 2, 6 * half_mb * EXCHANGE_US_PER_MB, EXCHANGE_PHASE_US)
    dw5 = dw.reshape(N_CHIPS, 2, r // 2, c)
    for k, rows in enumerate(chunks):
        comm.push(_Phase(("exchange", key, k), key, None, ("exchange", key, 0), 6 * half_mb * EXCHANGE_US_PER_MB / len(chunks), [dw5],
                         jax.ShapeDtypeStruct((7, r // 2, c), BF16), 7, 0, functools.partial(_build_exchange, rows=rows)))


MM_FLOPS_PER_US = 6.0e8
MM_ROWS = 1024
MM_ROWS_WIDE = 2048
ROW_STEP = 512


def mm_nn(a, w3, *, tn, out_dtype, name, ncb=None, cbmap=None, res=None, perm_d=1, comm=None):
    M, K = a.shape
    P, _, Ns = w3.shape
    nper = Ns // tn
    ncb = P * nper if ncb is None else ncb
    tm = max(ATTN_BLOCK * perm_d, _tile(M, MM_ROWS_WIDE, 16)) if perm_d > 1 else _tile(M, MM_ROWS, 16)
    rpb = tm // perm_d
    cbm = cbmap if cbmap is not None else (lambda j: j)
    nch = tn // LANES

    def body(*refs):
        if res is None:
            a_ref, w_ref, o_ref = refs[:3]
        else:
            a_ref, w_ref, x_ref, g_ref, o_ref, xo_ref = refs
        acc = jnp.dot(a_ref[...].astype(BF16), w_ref[...], preferred_element_type=F32)
        if perm_d > 1:
            scr = refs[3]
            for cj in range(nch):
                scr[cj] = acc[:, cj * LANES:(cj + 1) * LANES]
            for r in range(perm_d):
                for cj in range(nch):
                    o_ref[r, :, cj * LANES:(cj + 1) * LANES] = scr.at[cj][pl.ds(r, rpb, stride=perm_d), :].astype(o_ref.dtype)
        else:
            o_ref[...] = acc.astype(o_ref.dtype)
        if res is not None:
            xo_ref[...] = x_ref[...] + g_ref[...] * acc

    in_specs = [pl.BlockSpec((tm, K), lambda i, j: (i, 0)),
                pl.BlockSpec((None, K, tn), lambda i, j: (cbm(j) // nper, 0, cbm(j) % nper))]
    scratch = []
    if perm_d > 1:
        out_specs = [pl.BlockSpec((perm_d, rpb, tn), lambda i, j: (0, i, j))]
        out_shape = [jax.ShapeDtypeStruct((perm_d, M // perm_d, ncb * tn), out_dtype)]
        scratch = [pltpu.VMEM((nch, tm, LANES), F32)]
    else:
        out_specs = [pl.BlockSpec((tm, tn), lambda i, j: (i, j))]
        out_shape = [jax.ShapeDtypeStruct((M, ncb * tn), out_dtype)]
    args = [a, w3]
    if res is not None:
        in_specs += [pl.BlockSpec((tm, tn), lambda i, j: (i, j)), pl.BlockSpec((1, tn), lambda i, j: (0, j))]
        out_specs.append(pl.BlockSpec((tm, tn), lambda i, j: (i, j)))
        out_shape.append(jax.ShapeDtypeStruct((M, ncb * tn), F32))
        args += [res[0], res[1]]
    outs = _pcall(body, name=name, grid=(M // tm, ncb), in_specs=in_specs, out_specs=out_specs, out_shape=out_shape, args=args,
                  scratch_shapes=scratch, comm=comm, carry_us=2.0 * M * K * ncb * tn / MM_FLOPS_PER_US)
    if perm_d > 1:
        return outs[0].reshape(M, ncb * tn)
    return outs[0] if res is None else (outs[0], outs[1])


PERM_COLS = 1024


def permute_rows(x, d, name):
    S, C = x.shape
    R = ATTN_BLOCK * d
    ct = _tile(C, PERM_COLS, LANES)
    nch = ct // LANES

    def body(x_ref, o_ref, scr):
        xv = x_ref[...].astype(F32)
        for cj in range(nch):
            scr[cj] = xv[:, cj * LANES:(cj + 1) * LANES]
        for r in range(d):
            for cj in range(nch):
                o_ref[r, :, cj * LANES:(cj + 1) * LANES] = scr.at[cj][pl.ds(r, ATTN_BLOCK, stride=d), :].astype(o_ref.dtype)

    out = pl.pallas_call(body, name=name, grid=(S // R, C // ct), in_specs=[pl.BlockSpec((R, ct), lambda i, j: (i, j))],
                         out_specs=pl.BlockSpec((d, ATTN_BLOCK, ct), lambda i, j: (0, i, j)),
                         out_shape=jax.ShapeDtypeStruct((d, S // d, C), x.dtype), scratch_shapes=[pltpu.VMEM((nch, R, LANES), F32)],
                         compiler_params=_cparams(("parallel", "parallel")))(x)
    return out.reshape(S, C)


def unpermute_rows(ps, d, name, into=None, total_cols=None, colmap=None):
    ps = list(ps) if isinstance(ps, (list, tuple)) else [ps]
    n_p = len(ps)
    p = ps[0]
    S, C = p.shape
    rpb = max(ATTN_BLOCK, 512 // d)
    R = rpb * d
    ct = _tile(C, PERM_COLS, LANES)
    nch = ct // LANES
    total_cols = C if total_cols is None else total_cols
    cm = colmap if colmap is not None else (lambda j: j)

    def body(*refs):
        p_refs, o_ref, scr = refs[:n_p], refs[-2], refs[-1]

        def summed(idx):
            return functools.reduce(lambda a, b: a + b, [r[idx].astype(F32) for r in p_refs])
        if d == 1:
            o_ref[...] = summed(0).astype(o_ref.dtype)
            return
        for r in range(d):
            for cj in range(nch):
                scr.at[cj][pl.ds(r, rpb, stride=d), :] = summed((r, slice(None), slice(cj * LANES, (cj + 1) * LANES)))
        for cj in range(nch):
            o_ref[:, cj * LANES:(cj + 1) * LANES] = scr[cj].astype(o_ref.dtype)

    in_specs = [pl.BlockSpec((d, rpb, ct), lambda i, j: (0, i, j))] * n_p
    args = [a.reshape(d, S // d, C) for a in ps]
    aliases = {}
    if into is not None:
        in_specs.append(pl.BlockSpec(memory_space=pl.ANY))
        args.append(into)
        aliases = {n_p: 0}
    return pl.pallas_call(body, name=name, grid=(S // R, C // ct), in_specs=in_specs,
                          out_specs=pl.BlockSpec((R, ct), lambda i, j: (i, cm(j))),
                          out_shape=jax.ShapeDtypeStruct((S, total_cols), p.dtype), scratch_shapes=[pltpu.VMEM((nch, R, LANES), F32)],
                          input_output_aliases=aliases, compiler_params=_cparams(("parallel", "parallel")))(*args)


def mm_nt(g3, w3, *, tn, tk, out_dtype, name, gmap=None, comm=None):
    _, M, _ = g3.shape
    P, K, Ns = w3.shape
    nper = Ns // tn
    ns = P * nper
    tm = _tile(M, MM_ROWS_WIDE, 16)
    gm = gmap if gmap is not None else (lambda s: (0, s))

    def body(g_ref, w_ref, o_ref, acc):
        s = pl.program_id(2)

        @pl.when(s == 0)
        def _():
            acc[...] = jnp.zeros_like(acc)
        acc[...] += lax.dot_general(g_ref[...].astype(BF16), w_ref[...], (((1,), (1,)), ((), ())), preferred_element_type=F32)

        @pl.when(s == ns - 1)
        def _():
            o_ref[...] = acc[...].astype(o_ref.dtype)

    return _pcall(
        body, name=name, grid=(M // tm, K // tk, ns),
        in_specs=[pl.BlockSpec((None, tm, tn), lambda i, kj, s: (gm(s)[0], i, gm(s)[1])),
                  pl.BlockSpec((None, tk, tn), lambda i, kj, s: (s // nper, kj, s % nper))],
        out_specs=[pl.BlockSpec((tm, tk), lambda i, kj, s: (i, kj))],
        out_shape=[jax.ShapeDtypeStruct((M, K), out_dtype)], args=[g3, w3],
        scratch_shapes=[pltpu.VMEM((tm, tk), F32)], comm=comm, carry_us=2.0 * M * K * P * Ns / MM_FLOPS_PER_US)[0]


def mm_tn(a, g3, wshape, *, tn, tk, name, gmap=None, comm=None):
    M, K = a.shape
    P, _, Ns = wshape
    nper = Ns // tn
    ns = P * nper
    tm = _tile(M, MM_ROWS_WIDE, 16)
    nm = M // tm
    gm = gmap if gmap is not None else (lambda s: (0, s))

    def body(a_ref, g_ref, o_ref, acc):
        mi = pl.program_id(2)

        @pl.when(mi == 0)
        def _():
            acc[...] = jnp.zeros_like(acc)
        acc[...] += lax.dot_general(a_ref[...].astype(BF16), g_ref[...].astype(BF16), (((0,), (0,)), ((), ())), preferred_element_type=F32)

        @pl.when(mi == nm - 1)
        def _():
            o_ref[...] = acc[...].astype(o_ref.dtype)

    return _pcall(
        body, name=name, grid=(ns, K // tk, nm),
        in_specs=[pl.BlockSpec((tm, tk), lambda s, kj, mi: (mi, kj)),
                  pl.BlockSpec((None, tm, tn), lambda s, kj, mi: (gm(s)[0], mi, gm(s)[1]))],
        out_specs=[pl.BlockSpec((None, tk, tn), lambda s, kj, mi: (s // nper, kj, s % nper))],
        out_shape=[jax.ShapeDtypeStruct((P, K, Ns), BF16)], args=[a, g3],
        scratch_shapes=[pltpu.VMEM((tk, tn), F32)], comm=comm, carry_us=2.0 * M * K * P * Ns / MM_FLOPS_PER_US)[0]


def _vspec(d):
    return pl.BlockSpec((1, d), lambda i: (0, 0))


NORM_US_PER_ELEM = 12.0 / (4096 * 1024)


def norm_mod(x, g, sh, sc, name, comm=None):
    S, D = x.shape
    tm = _tile(S, 512, 16)

    def body(x_ref, g_ref, sh_ref, sc_ref, o_ref):
        xv = x_ref[...]
        r = lax.rsqrt(jnp.mean(xv * xv, axis=-1, keepdims=True) + EPS)
        o_ref[...] = ((xv * r) * g_ref[...] * (1.0 + sc_ref[...]) + sh_ref[...]).astype(o_ref.dtype)

    return _pcall(body, name=name, grid=(S // tm,),
                  in_specs=[pl.BlockSpec((tm, D), lambda i: (i, 0)), _vspec(D), _vspec(D), _vspec(D)],
                  out_specs=[pl.BlockSpec((tm, D), lambda i: (i, 0))], out_shape=[jax.ShapeDtypeStruct((S, D), BF16)],
                  args=[x, g, sh, sc], comm=comm, carry_us=NORM_US_PER_ELEM * S * D)[0]


def _gate_outputs(dx, gate_ref, out_ref, dout_ref):
    dout_ref[...] = (gate_ref[...] * dx).astype(dout_ref.dtype)
    return jnp.sum(dx * out_ref[...].astype(F32), axis=0, keepdims=True)


NORM_BWD_US_PER_ELEM = 28.0 / (4096 * 1024)


def norm_mod_bwd(dh, x, dres, g, sc, name, below=None, comm=None):
    S, D = x.shape
    tm = _tile(S, ROW_STEP, 16)

    def body(dh_ref, x_ref, dr_ref, g_ref, sc_ref, *rest):
        dx_ref, sums_ref = (rest[2], rest[3]) if below is not None else (rest[0], rest[1])
        xv = x_ref[...]
        dhv = dh_ref[...].astype(F32)
        r = lax.rsqrt(jnp.mean(xv * xv, axis=-1, keepdims=True) + EPS)
        xn = xv * r
        one_sc = 1.0 + sc_ref[...]
        dxn = dhv * g_ref[...] * one_sc
        dx = r * (dxn - xn * jnp.mean(dxn * xn, axis=-1, keepdims=True)) + dr_ref[...]
        dx_ref[...] = dx
        rows = [jnp.sum(dhv, axis=0, keepdims=True), jnp.sum(dhv * xn * g_ref[...], axis=0, keepdims=True),
                jnp.sum(dhv * one_sc * xn, axis=0, keepdims=True)]
        if below is not None:
            rows.append(_gate_outputs(dx, rest[0], rest[1], rest[4]))
        part = jnp.concatenate(rows + [jnp.zeros((8 - len(rows), D), F32)], axis=0)

        @pl.when(pl.program_id(0) == 0)
        def _():
            sums_ref[...] = jnp.zeros_like(sums_ref)
        sums_ref[...] += part

    row = pl.BlockSpec((tm, D), lambda i: (i, 0))
    in_specs, args = [row, row, row, _vspec(D), _vspec(D)], [dh, x, dres, g, sc]
    out_specs = [row, pl.BlockSpec((8, D), lambda i: (0, 0))]
    out_shape = [jax.ShapeDtypeStruct((S, D), F32), jax.ShapeDtypeStruct((8, D), F32)]
    if below is not None:
        in_specs += [_vspec(D), row]
        args += [below[0], below[1]]
        out_specs.append(row)
        out_shape.append(jax.ShapeDtypeStruct((S, D), BF16))
    return _pcall(body, name=name, grid=(S // tm,), in_specs=in_specs, out_specs=out_specs, out_shape=out_shape, args=args,
                  comm=comm, carry_us=NORM_BWD_US_PER_ELEM * S * D)


def loss_fwd_bwd(x, g, target, below, name):
    S, D = x.shape
    tm = _tile(S, ROW_STEP, 16)

    def body(x_ref, g_ref, t_ref, gate_ref, out_ref, dx_ref, sums_ref, dout_ref):
        xv = x_ref[...]
        r = lax.rsqrt(jnp.mean(xv * xv, axis=-1, keepdims=True) + EPS)
        xn = xv * r
        err = xn * g_ref[...] - t_ref[...]
        dy = err * (1.0 / D)
        dxn = dy * g_ref[...]
        dx = r * (dxn - xn * jnp.mean(dxn * xn, axis=-1, keepdims=True))
        dx_ref[...] = dx
        part = jnp.concatenate([jnp.sum(dy * xn, axis=0, keepdims=True), jnp.sum(err * err, axis=0, keepdims=True),
                                _gate_outputs(dx, gate_ref, out_ref, dout_ref), jnp.zeros((5, D), F32)], axis=0)

        @pl.when(pl.program_id(0) == 0)
        def _():
            sums_ref[...] = jnp.zeros_like(sums_ref)
        sums_ref[...] += part

    row = pl.BlockSpec((tm, D), lambda i: (i, 0))
    return pl.pallas_call(body, name=name, grid=(S // tm,), in_specs=[row, _vspec(D), row, _vspec(D), row],
                          out_specs=[row, pl.BlockSpec((8, D), lambda i: (0, 0)), row],
                          out_shape=[jax.ShapeDtypeStruct((S, D), F32), jax.ShapeDtypeStruct((8, D), F32), jax.ShapeDtypeStruct((S, D), BF16)],
                          compiler_params=_cparams(("arbitrary",)))(x, g, target, below[0], below[1])


def pool_fwd(u, wgrp, scale, name):
    S, D = u.shape
    G = len(POOL_WINDOWS)
    C = D // G
    tm = _tile(S, 256, 16)
    hb = tm // HALO

    def body(up_ref, uc_ref, w_ref, sc_ref, p_ref, z_ref, y_ref):
        i = pl.program_id(0)
        prev = jnp.where(i > 0, up_ref[...], 0.0)
        ext = jnp.concatenate([prev, uc_ref[...]], axis=0)
        t = i * tm + lax.broadcasted_iota(jnp.int32, (tm, 1), 0)
        for gi, w in enumerate(POOL_WINDOWS):
            cs = slice(gi * C, (gi + 1) * C)
            e = ext[:, cs]
            s, k = e, 1
            while k < w:
                s = s + pltpu.roll(s, k, 0)
                k *= 2
            cnt = jnp.minimum(t + 1, w).astype(F32)
            pooled = (s[HALO:] / cnt - e[HALO:]).astype(BF16)
            p_ref[:, cs] = pooled
            z = jnp.dot(pooled, w_ref[:, gi].reshape(C, C), preferred_element_type=F32)
            z_ref[:, cs] = z.astype(BF16)
            y_ref[:, cs] = (z * sc_ref[:, cs]).astype(BF16)

    row = pl.BlockSpec((tm, D), lambda i: (i, 0))
    return pl.pallas_call(
        body, name=name, grid=(S // tm,),
        in_specs=[pl.BlockSpec((HALO, D), lambda i: (jnp.maximum(i * hb - 1, 0), 0)), row,
                  pl.BlockSpec(wgrp.shape, lambda i: (0, 0, 0, 0)), _vspec(D)],
        out_specs=[row, row, row], out_shape=[jax.ShapeDtypeStruct((S, D), BF16)] * 3,
        compiler_params=_cparams(("parallel",)))(u, u, wgrp, scale)


def pool_bwd(dys, z, pooled, wgrp, scale, name):
    S, D = dys.shape
    G = len(POOL_WINDOWS)
    C = D // G
    tm = _tile(S, 256, 16)
    hb = tm // HALO
    nt = S // tm
    n_ext = tm + HALO

    def body(dc_ref, dn_ref, z_ref, p_ref, w_ref, sc_ref, du_ref, dw_ref, sums_ref):
        i = pl.program_id(0)

        @pl.when(i == 0)
        def _():
            dw_ref[...] = jnp.zeros_like(dw_ref)
            sums_ref[...] = jnp.zeros_like(sums_ref)
        dyc = dc_ref[...].astype(F32)
        nxt = jnp.where(i < nt - 1, dn_ref[...].astype(F32), 0.0)
        ext = jnp.concatenate([dyc, nxt], axis=0)
        sums_ref[...] += jnp.concatenate([jnp.sum(dyc * z_ref[...].astype(F32), axis=0, keepdims=True), jnp.zeros((7, D), F32)], axis=0)
        t = i * tm + lax.broadcasted_iota(jnp.int32, (n_ext, 1), 0)
        for gi, w in enumerate(POOL_WINDOWS):
            cs = slice(gi * C, (gi + 1) * C)
            wg = w_ref[:, gi].reshape(C, C)
            dz = (ext[:, cs] * sc_ref[:, cs]).astype(BF16)
            dpool = lax.dot_general(dz, wg, (((1,), (1,)), ((), ())), preferred_element_type=F32)
            dw_ref[gi] += lax.dot_general(p_ref[:, cs], dz[:tm], (((0,), (0,)), ((), ())), preferred_element_type=F32)
            cnt = jnp.minimum(t + 1, w).astype(F32)
            s, k = dpool / cnt, 1
            while k < w:
                s = s + pltpu.roll(s, n_ext - k, 0)
                k *= 2
            du_ref[:, cs] = (s[:tm] - dpool[:tm]).astype(BF16)

    row = pl.BlockSpec((tm, D), lambda i: (i, 0))
    return pl.pallas_call(
        body, name=name, grid=(nt,),
        in_specs=[row, pl.BlockSpec((HALO, D), lambda i: (jnp.minimum((i + 1) * hb, S // HALO - 1), 0)), row, row,
                  pl.BlockSpec(wgrp.shape, lambda i: (0, 0, 0, 0)), _vspec(D)],
        out_specs=[row, pl.BlockSpec((G, C, C), lambda i: (0, 0, 0)), pl.BlockSpec((8, D), lambda i: (0, 0))],
        out_shape=[jax.ShapeDtypeStruct((S, D), BF16), jax.ShapeDtypeStruct((G, C, C), F32), jax.ShapeDtypeStruct((8, D), F32)],
        compiler_params=_cparams(("arbitrary",)))(dys, dys, z, pooled, wgrp, scale)


FFN_ACT_BWD_US_PER_ELEM = 84.0 / (4096 * 2816)


def ffn_up_act(h, w3, conv_w, conv_b, name, comm=None):
    S, D = h.shape
    P, _, Ns = w3.shape
    nh = P // 2
    tm = _tile(S, MM_ROWS, 16)

    def body(h_ref, w_ref, cw_ref, cb_ref, hu_ref, g_ref, c_ref, stash, halo):
        i, j = pl.program_id(0), pl.program_id(1)
        acc = jnp.dot(h_ref[...], w_ref[...], preferred_element_type=F32).astype(BF16)
        hu_ref[...] = acc

        @pl.when(j < nh)
        def _():
            stash[j] = acc.astype(F32)

        @pl.when(j >= nh)
        def _():
            c = j - nh
            a = stash[c]
            ext = jnp.concatenate([jnp.where(i > 0, halo[c], 0.0), a], axis=0)
            conv = cb_ref[...] + pltpu.roll(ext, 2, 0) * cw_ref[0:1, :] + pltpu.roll(ext, 1, 0) * cw_ref[1:2, :] + ext * cw_ref[2:3, :]
            conv = conv[HALO:]
            c_ref[...] = conv.astype(c_ref.dtype)
            g_ref[...] = (conv * _sigmoid(conv) * acc.astype(F32)).astype(g_ref.dtype)
            halo[c] = a[tm - HALO:]

    def gcol(j):
        return jnp.maximum(j - nh, 0)

    gspec = pl.BlockSpec((tm, Ns), lambda i, j: (i, gcol(j)))
    return _pcall(
        body, name=name, grid=(S // tm, P),
        in_specs=[pl.BlockSpec((tm, D), lambda i, j: (i, 0)), pl.BlockSpec((None, D, Ns), lambda i, j: (j, 0, 0)),
                  pl.BlockSpec((3, Ns), lambda i, j: (0, gcol(j))), pl.BlockSpec((1, Ns), lambda i, j: (0, gcol(j)))],
        out_specs=[pl.BlockSpec((tm, Ns), lambda i, j: (i, j)), gspec, gspec],
        out_shape=[jax.ShapeDtypeStruct((S, P * Ns), BF16), jax.ShapeDtypeStruct((S, nh * Ns), BF16), jax.ShapeDtypeStruct((S, nh * Ns), BF16)],
        scratch_shapes=[pltpu.VMEM((nh, tm, Ns), F32), pltpu.VMEM((nh, HALO, Ns), F32)],
        args=[h, w3, conv_w, conv_b], comm=comm, carry_us=2.0 * S * D * P * Ns / MM_FLOPS_PER_US)


def ffn_act_bwd(dout, w_down, hu, conv, conv_w, name, comm=None):
    S, D = dout.shape
    F = w_down.shape[1]
    tm = _tile(S, 256, 16)
    tn = _tile(F, 1408, LANES)
    nb = F // tn
    hb = tm // HALO
    nt = S // tm
    n_ext = tm + HALO
    nt_dims = (((1,), (1,)), ((), ()))

    def body(dc_ref, dn_ref, wd_ref, cc_ref, cn_ref, ac_ref, vc_ref, vn_ref, w_ref, o_ref, sums_ref):
        i = pl.program_id(1)

        @pl.when(i == 0)
        def _():
            sums_ref[...] = jnp.zeros_like(sums_ref)
        cv = jnp.concatenate([cc_ref[...], cn_ref[...]], axis=0).astype(F32)
        v_ext = jnp.concatenate([vc_ref[...], vn_ref[...]], axis=0).astype(F32)
        g_cur = lax.dot_general(dc_ref[...], wd_ref[...], nt_dims, preferred_element_type=F32)
        g_nxt = lax.dot_general(dn_ref[...], wd_ref[...], nt_dims, preferred_element_type=F32)
        g_ext = jnp.concatenate([g_cur, jnp.where(i < nt - 1, g_nxt, 0.0)], axis=0)
        w0, w1, w2 = w_ref[0:1, :], w_ref[1:2, :], w_ref[2:3, :]
        sig = _sigmoid(cv)
        silu = cv * sig
        dconv = g_ext * v_ext * (sig + silu * (1.0 - sig))
        d_p1, d_p2 = pltpu.roll(dconv, n_ext - 1, 0), pltpu.roll(dconv, n_ext - 2, 0)
        da = dconv * w2 + d_p1 * w1 + d_p2 * w0
        o_ref[0] = da[:tm].astype(o_ref.dtype)
        o_ref[1] = (g_ext * silu)[:tm].astype(o_ref.dtype)
        a = ac_ref[...].astype(F32)
        part = jnp.concatenate([jnp.sum(a * d_p2[:tm], axis=0, keepdims=True), jnp.sum(a * d_p1[:tm], axis=0, keepdims=True),
                                jnp.sum(a * dconv[:tm], axis=0, keepdims=True), jnp.sum(dconv[:tm], axis=0, keepdims=True),
                                jnp.zeros((4, tn), F32)], axis=0)
        sums_ref[...] += part

    def nxt(i):
        return jnp.minimum((i + 1) * hb, S // HALO - 1)

    return _pcall(
        body, name=name, grid=(nb, nt),
        in_specs=[pl.BlockSpec((tm, D), lambda j, i: (i, 0)), pl.BlockSpec((HALO, D), lambda j, i: (nxt(i), 0)),
                  pl.BlockSpec((None, tn, D), lambda j, i: (0, j, 0)),
                  pl.BlockSpec((tm, tn), lambda j, i: (i, j)), pl.BlockSpec((HALO, tn), lambda j, i: (nxt(i), j)),
                  pl.BlockSpec((tm, tn), lambda j, i: (i, j)),
                  pl.BlockSpec((tm, tn), lambda j, i: (i, j + nb)), pl.BlockSpec((HALO, tn), lambda j, i: (nxt(i), j + nb)),
                  pl.BlockSpec((3, tn), lambda j, i: (0, j))],
        out_specs=[pl.BlockSpec((2, tm, tn), lambda j, i: (0, i, j)), pl.BlockSpec((8, tn), lambda j, i: (0, j))],
        out_shape=[jax.ShapeDtypeStruct((2, S, F), BF16), jax.ShapeDtypeStruct((8, F), F32)],
        args=[dout, dout, w_down, conv, conv, hu, hu, hu, conv_w], comm=comm, carry_us=FFN_ACT_BWD_US_PER_ELEM * S * F)


def _head_expander(n_heads, da):
    e = np.zeros((LANES, da), np.float32)
    for h in range(n_heads):
        e[h, h * HEAD_DIM:(h + 1) * HEAD_DIM] = 1.0
    return jnp.asarray(e, BF16)


def _split_dot(v, e, dims):
    hi = v.astype(BF16)
    lo = (v - hi.astype(F32)).astype(BF16)
    return (lax.dot_general(hi, e, dims, preferred_element_type=F32) + lax.dot_general(lo, e, dims, preferred_element_type=F32))


def _lane_col(tile, h):
    lane = lax.broadcasted_iota(jnp.int32, tile.shape, 1)
    return jnp.sum(jnp.where(lane == h, tile, 0.0), axis=1, keepdims=True)


ATTN_US_PER_ELEM = (80.0 / (4096 * 1024), 230.0 / (4096 * 1024))


def attn_branch_fwd(q, kv, gi, slopes, name, comm=None):
    S, DA = q.shape
    H = DA // HEAD_DIM
    window, d = BRANCHES[gi]
    n_steps = window // d
    blk = ATTN_BLOCK
    assert n_steps == blk and (S // d) % blk == 0
    nbs = S // d // blk
    scale = HEAD_DIM ** -0.5

    def body(q_ref, kp_ref, kc_ref, vp_ref, vc_ref, o_ref, l_ref, s_scr, p_scr):
        jb = pl.program_id(1)
        row = lax.broadcasted_iota(jnp.int32, (blk, 2 * blk), 0)
        col = lax.broadcasted_iota(jnp.int32, (blk, 2 * blk), 1)
        delta = row + blk - col
        valid = (delta >= 0) & (delta <= n_steps) & ((col >= blk) | (jb > 0))
        dist = jnp.where(valid, (delta * d).astype(F32), -NEG)
        lane = lax.broadcasted_iota(jnp.int32, (blk, LANES), 1)
        ltile = jnp.zeros((blk, LANES), F32)
        for h in range(H):
            hs = slice(h * HEAD_DIM, (h + 1) * HEAD_DIM)
            k2 = jnp.concatenate([kp_ref[:, hs], kc_ref[:, hs]], axis=0)
            s_scr[h] = lax.dot_general(q_ref[:, hs], k2, (((1,), (1,)), ((), ())), preferred_element_type=F32)
        for h in range(H):
            s = s_scr[h] * scale - float(slopes[h]) * dist
            m = jnp.max(s, axis=-1, keepdims=True)
            p = jnp.exp(s - m)
            l = jnp.sum(p, axis=-1, keepdims=True)
            p_scr[h] = (p / l).astype(BF16)
            ltile = jnp.where(lane == h, m + jnp.log(l), ltile)
        for h in range(H):
            hs = slice(h * HEAD_DIM, (h + 1) * HEAD_DIM)
            v2 = jnp.concatenate([vp_ref[:, hs], vc_ref[:, hs]], axis=0)
            o_ref[:, hs] = jnp.dot(p_scr[h], v2, preferred_element_type=F32).astype(o_ref.dtype)
        l_ref[...] = ltile

    def cur(width, off):
        return pl.BlockSpec((blk, width), lambda r, jb: (r * nbs + jb, off))

    def prv(width, off):
        return pl.BlockSpec((blk, width), lambda r, jb: (r * nbs + jnp.maximum(jb - 1, 0), off))

    return _pcall(
        body, name=name, grid=(d, nbs),
        in_specs=[cur(DA, 0), prv(DA, 0), cur(DA, 0), prv(DA, 1), cur(DA, 1)],
        out_specs=[cur(DA, 0), cur(LANES, 0)],
        out_shape=[jax.ShapeDtypeStruct((S, DA), BF16), jax.ShapeDtypeStruct((S, LANES), F32)],
        scratch_shapes=[pltpu.VMEM((H, blk, 2 * blk), F32), pltpu.VMEM((H, blk, 2 * blk), BF16)],
        args=[q, kv, kv, kv, kv], comm=comm, carry_us=ATTN_US_PER_ELEM[0] * S * DA)


def attn_combine(os_, lses, name):
    S, DA = os_[0].shape
    H = DA // HEAD_DIM
    tm = _tile(S, ROW_STEP, 16)
    expander = _head_expander(H, DA)
    nbr = len(os_)

    def body(*refs):
        o_refs, l_refs, e_ref = refs[:nbr], refs[nbr:2 * nbr], refs[2 * nbr]
        out_ref, lse_ref = refs[2 * nbr + 1:]
        ls = [r[...] for r in l_refs]
        lmax = functools.reduce(jnp.maximum, ls)
        es = [jnp.exp(l - lmax) for l in ls]
        den = functools.reduce(lambda a, b: a + b, es)
        lse_ref[...] = lmax + jnp.log(den)
        acc = jnp.zeros((tm, DA), F32)
        for e, o_ref in zip(es, o_refs):
            acc = acc + _split_dot(e / den, e_ref[...], (((1,), (0,)), ((), ()))) * o_ref[...]
        out_ref[...] = acc.astype(out_ref.dtype)

    row = pl.BlockSpec((tm, DA), lambda i: (i, 0))
    lrow = pl.BlockSpec((tm, LANES), lambda i: (i, 0))
    return pl.pallas_call(
        body, name=name, grid=(S // tm,),
        in_specs=[row] * nbr + [lrow] * nbr + [pl.BlockSpec((LANES, DA), lambda i: (0, 0))],
        out_specs=[row, lrow], out_shape=[jax.ShapeDtypeStruct((S, DA), BF16), jax.ShapeDtypeStruct((S, LANES), F32)],
        compiler_params=_cparams(("parallel",)))(*os_, *lses, expander)


def attn_delta(do, o, name):
    S, DA = o.shape
    H = DA // HEAD_DIM
    tm = _tile(S, 512, 16)
    expander = _head_expander(H, DA)

    def body(do_ref, o_ref, e_ref, d_ref):
        prod = do_ref[...].astype(F32) * o_ref[...].astype(F32)
        d_ref[...] = _split_dot(prod, e_ref[...], (((1,), (1,)), ((), ())))

    row = pl.BlockSpec((tm, DA), lambda i: (i, 0))
    return pl.pallas_call(body, name=name, grid=(S // tm,), in_specs=[row, row, pl.BlockSpec((LANES, DA), lambda i: (0, 0))],
                          out_specs=pl.BlockSpec((tm, LANES), lambda i: (i, 0)), out_shape=jax.ShapeDtypeStruct((S, LANES), F32),
                          compiler_params=_cparams(("parallel",)))(do, o, expander)


def attn_branch_bwd(q, kv, do, lse, dlt, gi, slopes, name, out_cols=None, comm=None):
    S, DA = q.shape
    H = DA // HEAD_DIM
    window, d = BRANCHES[gi]
    n_steps = window // d
    blk = ATTN_BLOCK
    nbs = S // d // blk
    scale = HEAD_DIM ** -0.5
    nt, tn = (((1,), (1,)), ((), ())), (((0,), (0,)), ((), ()))

    def body(*refs):
        k_ref, v_ref, qc_ref, qn_ref, doc_ref, don_ref, lc_ref, ln_ref, dc_ref, dn_ref = refs[:10]
        dq_ref, dkv_ref, carry, s_scr, dp_scr, p_scr, ds_scr = refs[-7:]
        kb = pl.program_id(1)

        @pl.when(kb == 0)
        def _():
            carry[...] = jnp.zeros_like(carry)
        row = lax.broadcasted_iota(jnp.int32, (2 * blk, blk), 0)
        col = lax.broadcasted_iota(jnp.int32, (2 * blk, blk), 1)
        delta = row - col
        valid = (delta >= 0) & (delta <= n_steps) & ((row < blk) | (kb < nbs - 1))
        dist = jnp.where(valid, (delta * d).astype(F32), -NEG)
        l2 = jnp.concatenate([lc_ref[...], ln_ref[...]], axis=0)
        d2 = jnp.concatenate([dc_ref[...], dn_ref[...]], axis=0)
        for h in range(H):
            hs = slice(h * HEAD_DIM, (h + 1) * HEAD_DIM)
            q2 = jnp.concatenate([qc_ref[:, hs], qn_ref[:, hs]], axis=0)
            do2 = jnp.concatenate([doc_ref[:, hs], don_ref[:, hs]], axis=0)
            s_scr[h] = lax.dot_general(q2, k_ref[:, hs], nt, preferred_element_type=F32)
            dp_scr[h] = lax.dot_general(do2, v_ref[:, hs], nt, preferred_element_type=F32)
        for h in range(H):
            p = jnp.exp(s_scr[h] * scale - float(slopes[h]) * dist - _lane_col(l2, h))
            p_scr[h] = p.astype(BF16)
            ds_scr[h] = (p * (dp_scr[h] - _lane_col(d2, h))).astype(BF16)
        for h in range(H):
            hs = slice(h * HEAD_DIM, (h + 1) * HEAD_DIM)
            vs = slice(DA + h * HEAD_DIM, DA + (h + 1) * HEAD_DIM)
            q2 = jnp.concatenate([qc_ref[:, hs], qn_ref[:, hs]], axis=0)
            do2 = jnp.concatenate([doc_ref[:, hs], don_ref[:, hs]], axis=0)
            dvh = lax.dot_general(p_scr[h], do2, tn, preferred_element_type=F32)
            dkh = lax.dot_general(ds_scr[h], q2, tn, preferred_element_type=F32) * scale
            dq2 = jnp.dot(ds_scr[h], k_ref[:, hs], preferred_element_type=F32) * scale
            dq_ref[:, hs] = (carry[:, hs] + dq2[:blk]).astype(dq_ref.dtype)
            carry[:, hs] = dq2[blk:]
            dkv_ref[:, hs] = dkh.astype(dkv_ref.dtype)
            dkv_ref[:, vs] = dvh.astype(dkv_ref.dtype)

    def cur(width, off):
        return pl.BlockSpec((blk, width), lambda r, kb: (r * nbs + kb, off))

    def nxt(width, off):
        return pl.BlockSpec((blk, width), lambda r, kb: (r * nbs + jnp.minimum(kb + 1, nbs - 1), off))

    in_specs = [cur(DA, 0), cur(DA, 1), cur(DA, 0), nxt(DA, 0), cur(DA, 0), nxt(DA, 0),
                cur(LANES, 0), nxt(LANES, 0), cur(LANES, 0), nxt(LANES, 0)]
    args = [kv, kv, q, q, do, do, lse, lse, dlt, dlt]
    return _pcall(
        body, name=name, grid=(d, nbs), in_specs=in_specs, out_specs=[cur(DA, 0), cur(2 * DA, 0)],
        out_shape=[jax.ShapeDtypeStruct((S, out_cols or DA), BF16), jax.ShapeDtypeStruct((S, 2 * DA), BF16)],
        scratch_shapes=[pltpu.VMEM((blk, DA), F32), pltpu.VMEM((H, 2 * blk, blk), F32), pltpu.VMEM((H, 2 * blk, blk), F32),
                        pltpu.VMEM((H, 2 * blk, blk), BF16), pltpu.VMEM((H, 2 * blk, blk), BF16)],
        args=args, comm=comm, carry_us=ATTN_US_PER_ELEM[1] * S * DA)


def ada_project(c16, w3, b3, name):
    L, D, Ns = w3.shape
    tn = _tile(Ns, 512, LANES)

    def body(c_ref, w_ref, b_ref, o_ref):
        cv = c_ref[...]
        cond = (cv * _sigmoid(cv)).astype(BF16)
        o_ref[...] = jnp.dot(cond, w_ref[...].astype(BF16), preferred_element_type=F32) + b_ref[...]

    return pl.pallas_call(
        body, name=name, grid=(L, Ns // tn),
        in_specs=[pl.BlockSpec((16, D), lambda l, j: (0, 0)), pl.BlockSpec((None, D, tn), lambda l, j: (l, 0, j)),
                  pl.BlockSpec((None, 1, tn), lambda l, j: (l, 0, j))],
        out_specs=pl.BlockSpec((None, 16, tn), lambda l, j: (l, 0, j)), out_shape=jax.ShapeDtypeStruct((L, 16, Ns), F32),
        compiler_params=_cparams(("parallel", "parallel")))(c16, w3, b3)


def _adamw(w, g, m, v):
    m = B1 * m + (1.0 - B1) * g
    v = B2 * v + (1.0 - B2) * (g * g)
    m_hat = m / (1.0 - B1 ** STEP)
    v_hat = v / (1.0 - B2 ** STEP)
    delta = -LR * (m_hat / (jnp.sqrt(v_hat) + ADAM_EPS) + WD * w)
    return delta, m, v


def ada_grad_adamw(c16, d3, w3, m3, v3, name):
    L, D, Ns = w3.shape
    tk = _tile(D, 256, 8)

    def body(c_ref, d_ref, w_ref, m_ref, v_ref, g_out, dl_out, m_out, v_out):
        cv = c_ref[...]
        cond = (cv * _sigmoid(cv)).astype(BF16)
        g = lax.dot_general(cond, d_ref[...].astype(BF16), (((0,), (0,)), ((), ())), preferred_element_type=F32)
        g_out[...] = g
        dl_out[...], m_out[...], v_out[...] = _adamw(w_ref[...], g, m_ref[...], v_ref[...])

    wspec = pl.BlockSpec((None, tk, Ns), lambda l, kj: (l, kj, 0))
    return pl.pallas_call(
        body, name=name, grid=(L, D // tk),
        in_specs=[pl.BlockSpec((16, tk), lambda l, kj: (0, kj)), pl.BlockSpec((None, 16, Ns), lambda l, kj: (l, 0, 0)), wspec, wspec, wspec],
        out_specs=[wspec] * 4, out_shape=[jax.ShapeDtypeStruct((L, D, Ns), F32)] * 4,
        compiler_params=_cparams(("parallel", "parallel")))(c16, d3, w3, m3, v3)


def adamw(w, g, m, v, name):
    R, C = w.shape
    tr = _tile(R, 256, 8)

    def body(w_ref, g_ref, m_ref, v_ref, g_out, dl_out, m_out, v_out):
        g = g_ref[...]
        g_out[...] = g
        dl_out[...], m_out[...], v_out[...] = _adamw(w_ref[...], g, m_ref[...], v_ref[...])

    spec = pl.BlockSpec((tr, C), lambda i: (i, 0))
    return pl.pallas_call(body, name=name, grid=(R // tr,), in_specs=[spec] * 4, out_specs=[spec] * 4,
                          out_shape=[jax.ShapeDtypeStruct((R, C), F32)] * 4, compiler_params=_cparams(("parallel",)))(w, g, m, v)


def sum_partials(own, recv, g_prev, layer, n_layers, pos, name):
    _, Rh, C = recv.shape
    tr = _tile(Rh, 256, 16)

    def body(pos_ref, own_ref, recv_ref, *rest):
        acc = own_ref[...].astype(F32)
        for rel in range(7):
            acc = acc + recv_ref[rel].astype(F32)
        rest[-1][...] = acc

    in_specs = [pl.BlockSpec((None, None, tr, C), lambda r, pos: (pos[1], pos[0], r, 0)), pl.BlockSpec((7, tr, C), lambda r, pos: (0, r, 0))]
    args = [pos, own.reshape(N_CHIPS, 2, Rh, C), recv]
    aliases = {}
    if g_prev is not None:
        in_specs.append(pl.BlockSpec(memory_space=pl.ANY))
        args.append(g_prev)
        aliases = {3: 0}
    return pl.pallas_call(
        body, name=name,
        grid_spec=pltpu.PrefetchScalarGridSpec(
            num_scalar_prefetch=1, grid=(Rh // tr,), in_specs=in_specs,
            out_specs=pl.BlockSpec((None, None, tr, C), lambda r, pos: (layer, pos[0], r, 0))),
        out_shape=jax.ShapeDtypeStruct((n_layers, 2, Rh, C), F32), input_output_aliases=aliases,
        compiler_params=_cparams(("parallel",)))(*args)


def sum_rows8(g8, name):
    _, R, C = g8.shape

    def body(g_ref, o_ref):
        acc = g_ref[0]
        for i in range(1, N_DEV):
            acc = acc + g_ref[i]
        o_ref[...] = acc

    return pl.pallas_call(body, name=name, grid=(1,), in_specs=[pl.BlockSpec((N_DEV, R, C), lambda i: (0, 0, 0))],
                          out_specs=pl.BlockSpec((R, C), lambda i: (0, 0)), out_shape=jax.ShapeDtypeStruct((R, C), F32),
                          compiler_params=_cparams(("arbitrary",)))(g8)


def _pack(vecs):
    flat = [v.reshape(-1).astype(F32) for v in vecs]
    sizes = [f.shape[0] for f in flat]
    total = sum(sizes)
    padded = -(-total // (8 * PACK_W)) * (8 * PACK_W)
    buf = jnp.concatenate(flat + [jnp.zeros((padded - total,), F32)])
    offs = np.concatenate([[0], np.cumsum(sizes)])
    return buf.reshape(-1, PACK_W), offs


def _unpack(buf, offs, shapes):
    flat = buf.reshape(-1)
    return [flat[int(offs[i]):int(offs[i + 1])].reshape(s) for i, s in enumerate(shapes)]


def kernel(x, c, ada_w, ada_b, norm1_g, norm2_g, pool_w_in, pool_w_grp, pool_scale, pool_w_out, kv_norm_g, kv_ada_w, kv_ada_b, w_kv, attn_w_q, attn_w_o, ffn_w_up, ffn_conv_w, ffn_conv_b, ffn_w_down, final_g, loss_target, m_ada_w, m_ada_b, m_norm1_g, m_norm2_g, m_pool_w_in, m_pool_w_grp, m_pool_scale, m_pool_w_out, m_kv_norm_g, m_kv_ada_w, m_kv_ada_b, m_w_kv, m_attn_w_q, m_attn_w_o, m_ffn_w_up, m_ffn_conv_w, m_ffn_conv_b, m_ffn_w_down, m_final_g, v_ada_w, v_ada_b, v_norm1_g, v_norm2_g, v_pool_w_in, v_pool_w_grp, v_pool_scale, v_pool_w_out, v_kv_norm_g, v_kv_ada_w, v_kv_ada_b, v_w_kv, v_attn_w_q, v_attn_w_o, v_ffn_w_up, v_ffn_conv_w, v_ffn_conv_b, v_ffn_w_down, v_final_g):
    S, D = x.shape[1], x.shape[2]
    depth = ada_w.shape[0]
    n_pool = pool_w_in.shape[0]
    n_attn = attn_w_q.shape[0]
    G = len(POOL_WINDOWS)
    NB = len(BRANCHES)
    DA = attn_w_o.shape[1] * N_CHIPS
    H = DA // HEAD_DIM
    F = ffn_conv_b.shape[1]
    Fs = F // N_CHIPS
    Dq = D // N_CHIPS
    ada_ns = ada_w.shape[2]
    kvada_ns = kv_ada_w.shape[1]
    slopes = _alibi_slopes(NB * H).reshape(NB, H)

    ix, iy, ic = lax.axis_index("x"), lax.axis_index("y"), lax.axis_index("c")
    p_me = 2 * ix + iy
    b_me = 4 * ix + 2 * iy + ic
    pos = jnp.stack([ic, p_me]).astype(jnp.int32)
    xs, tgt = x[0], loss_target[0]

    pk, offs = _pack([c, pool_scale, ffn_conv_w])
    rows1 = pk.shape[0]
    got = all_gather8(pk, "gather_small_in").reshape(N_DEV, rows1, PACK_W)
    c8 = got.reshape(N_DEV, -1)[:, :D]
    c16 = jnp.concatenate([c8, jnp.zeros_like(c8)], axis=0)
    chip_rows = got[0::2].reshape(N_CHIPS, -1)
    scale_full = chip_rows[:, int(offs[1]):int(offs[2])].reshape(N_CHIPS, n_pool, Dq).transpose(1, 0, 2).reshape(n_pool, D)
    convw_full = chip_rows[:, int(offs[2]):int(offs[3])].reshape(N_CHIPS, depth, 3, Fs).transpose(1, 2, 0, 3).reshape(depth, 3, F)

    ada_b_loc = lax.dynamic_slice(ada_b, (0, p_me * ada_ns), (depth, ada_ns)).reshape(depth, 1, ada_ns)
    kvb_loc = lax.dynamic_slice(kv_ada_b, (p_me * kvada_ns,), (kvada_ns,)).reshape(1, 1, kvada_ns)
    mods_loc = ada_project(c16, ada_w, ada_b_loc, "ada_project")[:, :N_DEV]
    kvmod_loc = ada_project(c16, kv_ada_w.reshape(1, D, kvada_ns), kvb_loc, "kv_ada_project")[0, :N_DEV]
    mods_cat = jnp.concatenate([mods_loc.transpose(1, 0, 2).reshape(N_DEV, depth * ada_ns), kvmod_loc], axis=1)
    mods_all = all_gather8(mods_cat, "gather_mods").reshape(N_CHIPS, 2, N_DEV, -1)
    mine = lax.dynamic_index_in_dim(mods_all[:, 0], b_me, axis=1, keepdims=False)
    mod = mine[:, :depth * ada_ns].reshape(N_CHIPS, depth, ada_ns).transpose(1, 0, 2).reshape(depth, 6, 1, D)
    kvmod = mine[:, depth * ada_ns:].reshape(2, 1, D)

    comm = _Comm()
    C = D // G
    kv_ns, q_ns, up_ns = w_kv.shape[1], attn_w_q.shape[2], ffn_w_up.shape[2]

    def layer_shards(l):
        sh = []
        if l < n_pool:
            sh += [(("pin", l), pool_w_in[l]), (("pgrp", l), pool_w_grp[l].reshape(-1, C)), (("pout", l), pool_w_out[l])]
        else:
            if l == n_pool:
                sh.append((("kv", 0), w_kv))
            sh += [(("wq", l), attn_w_q[l - n_pool]), (("wo", l), attn_w_o[l - n_pool])]
        sh += [(("up", l), ffn_w_up[l]), (("down", l), ffn_w_down[l])]
        return [(k, w.astype(BF16)) for k, w in sh]

    def weight(key, shape):
        return comm.require(key).reshape(shape)

    dil = [d for _, d in BRANCHES]
    kv_tn = DA // 2
    q_tn = DA // 4
    q_bwd_tn = q_ns
    up_tn = up_ns
    up_per_half = F // up_tn

    def up_gmap(s):
        return s // up_per_half, s % up_per_half

    def vec(v):
        return v.reshape(1, -1)

    saved = []
    xcur = xs
    kvs = None
    wts = {}
    push_gather(comm, layer_shards(0))
    for l in range(depth):
        if l + 1 < depth:
            push_gather(comm, layer_shards(l + 1))
        sh1, sc1, g1, sh2, sc2, g2 = [mod[l, i] for i in range(6)]
        st = {"x0": xcur}
        h1 = norm_mod(xcur, vec(norm1_g[l]), sh1, sc1, "norm_mod", comm=comm)
        st["h1"] = h1
        if l < n_pool:
            wts["pin", l] = weight(("pin", l), (1, D, D))
            u = mm_nn(h1, wts["pin", l], tn=D, out_dtype=F32, name="pool_in_proj", comm=comm)
            wts["pgrp", l] = weight(("pgrp", l), (N_CHIPS, G, C // N_CHIPS, C))
            pooled, z, ys = pool_fwd(u, wts["pgrp", l], vec(scale_full[l]), "pool_mix")
            wts["pout", l] = weight(("pout", l), (1, D, D))
            out, x1 = mm_nn(ys, wts["pout", l], tn=D, out_dtype=BF16, name="pool_out_proj", res=(xcur, g1), comm=comm)
            st.update(pooled=pooled, z=z, ys=ys, out1=out)
        else:
            if l == n_pool:
                wts["kv", 0] = weight(("kv", 0), (N_CHIPS, D, kv_ns))
                hkv = norm_mod(xcur, vec(kv_norm_g), kvmod[0], kvmod[1], "norm_mod", comm=comm)
                kvs = [mm_nn(hkv, wts["kv", 0], tn=kv_tn, out_dtype=BF16, name=f"kv_proj_b{gi}", ncb=4, perm_d=dil[gi], comm=comm,
                             cbmap=functools.partial(lambda jj, gi: 2 * gi + (jj // 2) * 2 * NB + jj % 2, gi=gi)) for gi in range(NB)]
                kv_state = {"x": xcur, "hkv": hkv}
            wts["wq", l] = weight(("wq", l), (N_CHIPS, D, q_ns))
            qs, os_, lses = [], [], []
            for gi in range(NB):
                q_b = mm_nn(h1, wts["wq", l], tn=q_tn, out_dtype=BF16, name=f"q_proj_b{gi}", ncb=4, perm_d=dil[gi], comm=comm,
                            cbmap=functools.partial(lambda jj, gi: 4 * gi + jj, gi=gi))
                o_b, l_b = attn_branch_fwd(q_b, kvs[gi], gi, slopes[gi], f"attn_fwd_b{gi}", comm=comm)
                if dil[gi] > 1:
                    o_b = unpermute_rows(o_b, dil[gi], f"unpermute_o_b{gi}")
                    l_b = unpermute_rows(l_b, dil[gi], f"unpermute_lse_b{gi}")
                qs.append(q_b)
                os_.append(o_b)
                lses.append(l_b)
            o, lse = attn_combine(os_, lses, "attn_combine")
            wts["wo", l] = weight(("wo", l), (1, DA, D))
            out, x1 = mm_nn(o, wts["wo", l], tn=D, out_dtype=BF16, name="attn_out_proj", res=(xcur, g1), comm=comm)
            st.update(qs=qs, o=o, lse=lse, out1=out)
        st["x1"] = x1
        h2 = norm_mod(x1, vec(norm2_g[l]), sh2, sc2, "norm_mod", comm=comm)
        wts["up", l] = weight(("up", l), (N_CHIPS, D, up_ns))
        hu, gated, conv = ffn_up_act(h2, wts["up", l], convw_full[l], vec(ffn_conv_b[l]), "ffn_up_act", comm=comm)
        wts["down", l] = weight(("down", l), (1, F, D))
        out2, x2 = mm_nn(gated, wts["down", l], tn=D, out_dtype=BF16, name="ffn_down_proj", res=(x1, g2), comm=comm)
        st.update(h2=h2, hu=hu, conv=conv, gated=gated, out2=out2)
        saved.append(st)
        xcur = x2
    comm.flush()

    dx, fsums, dout2 = loss_fwd_bwd(xcur, vec(final_g), tgt, (mod[depth - 1, 5], saved[depth - 1]["out2"]), "loss_head")
    loss = lax.psum(0.5 * jnp.sum(fsums[1]) / D, ("x", "y", "c"))
    d_final_g, s_g2 = fsums[0], fsums[2]

    dmods = [None] * depth
    d_n1 = [None] * depth
    d_n2 = [None] * depth
    d_convw = [None] * depth
    d_convb = [None] * depth
    d_scale = [None] * n_pool
    d_grp = [None] * n_pool
    dkvs = [[] for _ in range(NB)]
    exchanged = []
    f_tk = _tile(F, 1408, LANES)
    ct_blocks = DA // _tile(DA, PERM_COLS, LANES)

    def exchange(name, idx, dw):
        dw4 = dw.reshape(N_CHIPS, -1, dw.shape[-1])
        exchanged.append((name, idx, dw4))
        push_exchange(comm, (name, idx), dw4)

    for l in reversed(range(depth)):
        st = saved[l]
        sh1, sc1, g1, sh2, sc2, g2 = [mod[l, i] for i in range(6)]
        dout2_3 = dout2.reshape(1, S, D)
        exchange("down", l, mm_tn(st["gated"], dout2_3, (1, F, D), tn=D, tk=f_tk, name="ffn_down_dw", comm=comm))
        dhu, s_conv = ffn_act_bwd(dout2, wts["down", l], st["hu"], st["conv"], convw_full[l], "ffn_act_bwd", comm=comm)
        dh2 = mm_nt(dhu, wts["up", l], tn=up_tn, tk=D, out_dtype=F32, name="ffn_up_bwd", gmap=up_gmap, comm=comm)
        exchange("up", l, mm_tn(st["h2"], dhu, (N_CHIPS, D, up_ns), tn=up_tn, tk=D, name="ffn_up_dw", gmap=up_gmap, comm=comm))
        dx, s_n2, dout1 = norm_mod_bwd(dh2, st["x1"], dx, vec(norm2_g[l]), sc2, "norm_mod_bwd_gate", below=(g1, st["out1"]), comm=comm)
        d_convw[l], d_convb[l] = s_conv[0:3], s_conv[3]
        d_n2[l] = s_n2[2]
        dout1_3 = dout1.reshape(1, S, D)
        if l < n_pool:
            dys = mm_nt(dout1_3, wts["pout", l], tn=D, tk=D // 2, out_dtype=F32, name="pool_out_bwd", comm=comm)
            exchange("pout", l, mm_tn(st["ys"], dout1_3, (1, D, D), tn=D, tk=D, name="pool_out_dw", comm=comm))
            du, d_grp, s_sc = pool_bwd(dys, st["z"], st["pooled"], wts["pgrp", l], vec(scale_full[l]), "pool_mix_bwd")
            exchange("pgrp", l, d_grp.astype(BF16).reshape(G, N_CHIPS, C // N_CHIPS, C).transpose(1, 0, 2, 3))
            d_scale[l] = s_sc[0]
            du_3 = du.reshape(1, S, D)
            dh1 = mm_nt(du_3, wts["pin", l], tn=D, tk=D // 2, out_dtype=F32, name="pool_in_bwd", comm=comm)
            exchange("pin", l, mm_tn(st["h1"], du_3, (1, D, D), tn=D, tk=D, name="pool_in_dw", comm=comm))
        else:
            j = l - n_pool
            do = mm_nt(dout1_3, wts["wo", l], tn=D, tk=DA // 2, out_dtype=BF16, name="attn_out_bwd", comm=comm)
            exchange("wo", j, mm_tn(st["o"], dout1_3, (1, DA, D), tn=D, tk=DA, name="attn_out_dw", comm=comm))
            dlt = attn_delta(do, st["o"], "attn_delta")
            dq = None
            for gi in range(NB):
                d = dil[gi]
                do_b, l_b, dl_b = do, st["lse"], dlt
                if d > 1:
                    do_b = permute_rows(do, d, f"permute_do_b{gi}")
                    l_b = permute_rows(st["lse"], d, f"permute_lse_b{gi}")
                    dl_b = permute_rows(dlt, d, f"permute_delta_b{gi}")
                bwd_name = f"attn_bwd_b{gi}"
                if d > 1:
                    dq_b, dkv_b = attn_branch_bwd(st["qs"][gi], kvs[gi], do_b, l_b, dl_b, gi, slopes[gi], bwd_name, comm=comm)
                    dq = unpermute_rows(dq_b, d, f"unpermute_dq_b{gi}", into=dq, total_cols=NB * DA,
                                        colmap=functools.partial(lambda jj, gi: gi * ct_blocks + jj, gi=gi))
                else:
                    dq, dkv_b = attn_branch_bwd(st["qs"][gi], kvs[gi], do_b, l_b, dl_b, gi, slopes[gi], bwd_name,
                                                out_cols=NB * DA, comm=comm)
                dkvs[gi].append(dkv_b)
            dq_3 = dq.reshape(1, S, NB * DA)
            dh1 = mm_nt(dq_3, wts["wq", l], tn=q_bwd_tn, tk=D, out_dtype=F32, name="q_proj_bwd", comm=comm)
            exchange("wq", j, mm_tn(st["h1"], dq_3, (N_CHIPS, D, q_ns), tn=q_bwd_tn, tk=D, name="q_proj_dw", comm=comm))
        below = (mod[l - 1, 5], saved[l - 1]["out2"]) if l > 0 else None
        if l == n_pool or below is None:
            dx, s_n1 = norm_mod_bwd(dh1, st["x0"], dx, vec(norm1_g[l]), sc1, "norm_mod_bwd", comm=comm if l > 0 else None)
        else:
            dx, s_n1, dout2 = norm_mod_bwd(dh1, st["x0"], dx, vec(norm1_g[l]), sc1, "norm_mod_bwd_gate", below=below, comm=comm)
        d_n1[l] = s_n1[2]
        dmods[l] = jnp.stack([s_n1[0], s_n1[1], s_n2[3], s_n2[0], s_n2[1], s_g2])
        if l > 0 and l != n_pool:
            s_g2 = s_n1[3]
        if l == n_pool:
            dkv = None
            for gi in range(NB):
                dkv = unpermute_rows(dkvs[gi], dil[gi], f"unpermute_dkv_b{gi}", into=dkv, total_cols=2 * NB * DA,
                                     colmap=functools.partial(lambda jj, gi: (jj // ct_blocks) * NB * ct_blocks + gi * ct_blocks + jj % ct_blocks,
                                                              gi=gi))
            dkv_3 = dkv.reshape(1, S, 2 * NB * DA)
            dhkv = mm_nt(dkv_3, wts["kv", 0], tn=kv_ns // 2, tk=D, out_dtype=F32, name="kv_proj_bwd", comm=comm)
            exchange("kv", 0, mm_tn(kv_state["hkv"], dkv_3, (N_CHIPS, D, kv_ns), tn=kv_ns // 2, tk=D, name="kv_proj_dw", comm=comm))
            dx, s_kv, dout2 = norm_mod_bwd(dhkv, kv_state["x"], dx, vec(kv_norm_g), kvmod[1], "norm_mod_bwd_gate", below=below, comm=comm)
            s_g2 = s_kv[3]
    grad_x = dx.reshape(1, S, D)

    smalls = [jnp.stack(dmods), jnp.stack([s_kv[0], s_kv[1]]), jnp.stack(d_n1), jnp.stack(d_n2), s_kv[2], jnp.stack(d_convb), d_final_g,
              jnp.stack(d_scale), jnp.stack(d_convw)]
    small_shapes = [s.shape for s in smalls]
    spk, soffs = _pack(smalls)
    srows = spk.shape[0]
    sgot = all_gather8(spk, "gather_small_grads").reshape(N_DEV, srows, PACK_W)
    ssum = sum_rows8(sgot, "sum_small_grads")
    g_mods, g_kvmod, g_n1, g_n2, g_kvn, g_convb, g_fg, g_scale_full, g_convw_full = _unpack(ssum, soffs, small_shapes)
    g_ada_b = g_mods.reshape(depth, 6 * D)
    g_kv_ada_b = g_kvmod.reshape(2 * D)
    g_scale = lax.dynamic_slice(g_scale_full, (0, p_me * Dq), (n_pool, Dq))
    g_convw = lax.dynamic_slice(g_convw_full, (0, 0, p_me * Fs), (depth, 3, Fs))

    small_w = [ada_b, norm1_g, norm2_g, kv_norm_g, kv_ada_b, ffn_conv_b, final_g, pool_scale, ffn_conv_w]
    small_m = [m_ada_b, m_norm1_g, m_norm2_g, m_kv_norm_g, m_kv_ada_b, m_ffn_conv_b, m_final_g, m_pool_scale, m_ffn_conv_w]
    small_v = [v_ada_b, v_norm1_g, v_norm2_g, v_kv_norm_g, v_kv_ada_b, v_ffn_conv_b, v_final_g, v_pool_scale, v_ffn_conv_w]
    small_g = [g_ada_b, g_n1, g_n2, g_kvn, g_kv_ada_b, g_convb, g_fg, g_scale, g_convw]
    sw_shapes = [w.shape for w in small_w]
    pw, woffs = _pack(small_w)
    s_res = adamw(pw, _pack(small_g)[0], _pack(small_m)[0], _pack(small_v)[0], "adamw_small")
    s_g, s_dl, s_m, s_v = [_unpack(r, woffs, sw_shapes) for r in s_res]

    per_dev = sgot.reshape(N_DEV, -1)
    dm_all = per_dev[:, int(soffs[0]):int(soffs[1])].reshape(N_DEV, depth, 6 * D)
    dkvm_all = per_dev[:, int(soffs[1]):int(soffs[2])].reshape(N_DEV, 1, 2 * D)

    def shard_cols(a, ns):
        sl = lax.dynamic_slice_in_dim(a, p_me * ns, ns, axis=2).transpose(1, 0, 2)
        return jnp.concatenate([sl, jnp.zeros_like(sl)], axis=1)

    ada_res = ada_grad_adamw(c16, shard_cols(dm_all, ada_ns), ada_w, m_ada_w, v_ada_w, "ada_grad_adamw")
    kvada_res = ada_grad_adamw(c16, shard_cols(dkvm_all, kvada_ns), kv_ada_w.reshape(1, D, kvada_ns), m_kv_ada_w.reshape(1, D, kvada_ns),
                               v_kv_ada_w.reshape(1, D, kvada_ns), "kv_ada_grad_adamw")
    kvada_res = [r.reshape(D, kvada_ns) for r in kvada_res]

    comm.flush()
    big_names = ["pin", "pgrp", "pout", "kv", "wq", "wo", "up", "down"]
    n_stack = {"pin": n_pool, "pgrp": n_pool, "pout": n_pool, "kv": 1, "wq": n_attn, "wo": n_attn, "up": depth, "down": depth}
    gsum = {nm: None for nm in big_names}
    for nm, idx, dw4 in exchanged:
        gsum[nm] = sum_partials(dw4, comm.store[nm, idx], gsum[nm], idx, n_stack[nm], pos, "sum_partials")
    for nm in big_names:
        comm.push(_Phase(("swap", nm), ("swap", nm), None, ("swap", nm), 0.0, [], None, gsum[nm].shape[0], 0, _build_swap, buffer=gsum[nm]))
    comm.flush()
    gsum = {nm: comm.store["swap", nm] for nm in big_names}
    big_m = [m_pool_w_in, m_pool_w_grp, m_pool_w_out, m_w_kv, m_attn_w_q, m_attn_w_o, m_ffn_w_up, m_ffn_w_down]
    big_v = [v_pool_w_in, v_pool_w_grp, v_pool_w_out, v_w_kv, v_attn_w_q, v_attn_w_o, v_ffn_w_up, v_ffn_w_down]
    big_w = [pool_w_in, pool_w_grp, pool_w_out, w_kv, attn_w_q, attn_w_o, ffn_w_up, ffn_w_down]
    big_res = []
    for nm, w, m_, v_ in zip(big_names, big_w, big_m, big_v):
        cols = gsum[nm].shape[-1]
        res = adamw(w.reshape(-1, cols), gsum[nm].reshape(-1, cols), m_.reshape(-1, cols), v_.reshape(-1, cols), "adamw_big")
        big_res.append([r.reshape(w.shape) for r in res])

    order = ["ada_w", "ada_b", "norm1_g", "norm2_g", "pool_w_in", "pool_w_grp", "pool_scale", "pool_w_out", "kv_norm_g", "kv_ada_w",
             "kv_ada_b", "w_kv", "attn_w_q", "attn_w_o", "ffn_w_up", "ffn_conv_w", "ffn_conv_b", "ffn_w_down", "final_g"]
    small_names = ["ada_b", "norm1_g", "norm2_g", "kv_norm_g", "kv_ada_b", "ffn_conv_b", "final_g", "pool_scale", "ffn_conv_w"]
    results = {"ada_w": ada_res, "kv_ada_w": kvada_res}
    for i, nm in enumerate(small_names):
        results[nm] = [s_g[i], s_dl[i], s_m[i], s_v[i]]
    for i, nm in enumerate(["pool_w_in", "pool_w_grp", "pool_w_out", "w_kv", "attn_w_q", "attn_w_o", "ffn_w_up", "ffn_w_down"]):
        results[nm] = big_res[i]
    outs = [loss, grad_x]
    for kind in range(4):
        outs += [results[nm][kind] for nm in order]
    return tuple(outs)
```
